```python
import math
import jax
import jax.numpy as jnp
from jax import lax
import numpy as np

D_MODEL = 2048
BATCH = 2
SEQ = 4096
DEPTH = 1
DEC_BATCH = 8
DEC_SEQ = 1
PAST_LEN = 16384
PAGE_SIZE = 128

DA_HEADS = 8
DA_DK = 64
DA_DV = 2 * DA_DK
SA_HEADS = 8
SA_KV_HEADS = 2
SA_DH = 128
IDX_HEADS = 16
IDX_DK = 64
SA_TOPK_MAX = 256
N_BUCKETS = 32
MAX_DISTANCE = 128
N_ALL_HEADS = DA_HEADS + SA_HEADS
N_EXPERTS = 64
N_GROUPS = 8
TOPK_GROUPS = 4
TOP_K = 8
D_EXPERT = 512
D_SHARED = 512
ROUTED_SCALE = 2.5
MOE_BLOCK = 128
Q_BLOCK = 128
EPS = 1e-6

IN_SIZES = [DA_HEADS * 2 * DA_DK, DA_HEADS * 2 * DA_DK, DA_HEADS * DA_DV,
            SA_HEADS * SA_DH, SA_KV_HEADS * SA_DH, SA_KV_HEADS * SA_DH,
            IDX_HEADS * IDX_DK, IDX_DK, IDX_HEADS, D_MODEL, D_MODEL]
IN_COLS = sum(IN_SIZES)
IN_SPLIT_POINTS = [sum(IN_SIZES[:i + 1]) for i in range(len(IN_SIZES) - 1)]

kernel_name = 'hybrid_diffattn_dsa_moe_step'


def rmsnorm(x, g):
    xf = x.astype(jnp.float32)
    y = xf * lax.rsqrt(jnp.mean(xf * xf, axis=-1, keepdims=True) + EPS)
    return (y * g.astype(jnp.float32)).astype(x.dtype)


def rel_bucket(dist):
    max_exact = N_BUCKETS // 2
    d = jnp.maximum(dist, 0)
    large = max_exact + (jnp.log(jnp.maximum(d, 1).astype(jnp.float32) / max_exact)
                         / math.log(MAX_DISTANCE / max_exact)
                         * (N_BUCKETS - max_exact)).astype(jnp.int32)
    large = jnp.minimum(large, N_BUCKETS - 1)
    return jnp.where(d < max_exact, d, large)


def rel_bias(table, dist):
    return table[rel_bucket(dist)].astype(jnp.float32)


def query_blocking(t):
    blk = Q_BLOCK if t % Q_BLOCK == 0 else t
    return blk, t // blk


def to_blocks(x, nb):
    return x.reshape(x.shape[0], nb, x.shape[1] // nb, *x.shape[2:]).swapaxes(0, 1)


def from_blocks(o):
    o = o.swapaxes(0, 1)
    return o.reshape(o.shape[0], o.shape[1] * o.shape[2], *o.shape[3:])


def take_rows(arr, idx):
    return jax.vmap(lambda a, i: a[i])(arr, idx)


def gather_paged(pool, page_table, pos):
    phys = jax.vmap(lambda t, i: t[i])(page_table, pos // PAGE_SIZE)
    return pool[phys, pos % PAGE_SIZE]


def gather_all_pages(pool, page_table):
    g = pool[page_table]
    return g.reshape(g.shape[0], g.shape[1] * g.shape[2], *g.shape[3:])


def diff_attention(q, k, v, q_pos, k_pos, lam, table):
    T = q.shape[1]
    blk, nb = query_blocking(T)

    def block(args):
        qb, pb = args
        s = jnp.einsum('bqhmd,bkhmd->bmhqk', qb, k,
                       preferred_element_type=jnp.float32) * DA_DK ** -0.5
        dist = pb[:, None] - k_pos[None, :]
        bias = jnp.moveaxis(rel_bias(table, dist)[..., :DA_HEADS], -1, 0)
        s = jnp.where(dist >= 0, s + bias, -jnp.inf)
        p = jax.nn.softmax(s, axis=-1)
        a = p[:, 0] - lam * p[:, 1]
        return jnp.einsum('bhqk,bkhd->bqhd', a.astype(v.dtype), v)

    return from_blocks(lax.map(block, (to_blocks(q, nb), q_pos.reshape(nb, blk))))


def sparse_attention(q, q_ix, w_ix, k_ix, q_pos, k_pos, gather_kv, table):
    B, T = q.shape[:2]
    n_keys = k_pos.shape[0]
    topk = min(SA_TOPK_MAX, n_keys // 4)
    blk, nb = query_blocking(T)
    grp = SA_HEADS // SA_KV_HEADS

    def block(args):
        qb, qib, wib, pb = args
        sc = jax.nn.relu(jnp.einsum('bqhd,bkd->bqhk', qib, k_ix,
                                    preferred_element_type=jnp.float32))
        idx_score = jnp.einsum('bqhk,bqh->bqk', sc, wib.astype(jnp.float32)) \
            * (IDX_DK ** -0.5 * IDX_HEADS ** -0.5)
        causal = pb[:, None] >= k_pos[None, :]
        idx_score = jnp.where(causal[None], idx_score, -jnp.inf)
        sel_score, sel = lax.top_k(idx_score, topk)
        ks, vs = gather_kv(sel)
        dist = pb[None, :, None] - k_pos[sel]
        bias = rel_bias(table, dist)[..., DA_HEADS:].reshape(B, blk, topk, SA_KV_HEADS, grp)
        qg = qb.reshape(B, blk, SA_KV_HEADS, grp, SA_DH)
        s = jnp.einsum('bqgrd,bqkgd->bqgrk', qg, ks,
                       preferred_element_type=jnp.float32) * SA_DH ** -0.5
        s = s + jnp.moveaxis(bias, 2, -1)
        s = jnp.where(jnp.isfinite(sel_score)[:, :, None, None, :], s, -jnp.inf)
        p = jax.nn.softmax(s, axis=-1)
        o = jnp.einsum('bqgrk,bqkgd->bqgrd', p.astype(vs.dtype), vs)
        return o.reshape(B, blk, SA_HEADS, SA_DH)

    out = lax.map(block, (to_blocks(q, nb), to_blocks(q_ix, nb), to_blocks(w_ix, nb),
                          q_pos.reshape(nb, blk)))
    return from_blocks(out)


def routed_experts(hf, eidx, wts, w_gate, w_up, w_down):
    N = hf.shape[0]
    A = N * TOP_K
    blk = min(MOE_BLOCK, max(8, A // N_EXPERTS))
    e_flat = eidx.reshape(A)
    order = jnp.argsort(e_flat)
    e_sorted = e_flat[order]
    counts = jnp.bincount(e_flat, length=N_EXPERTS)
    padded = (counts + blk - 1) // blk * blk
    starts = jnp.cumsum(counts) - counts
    pends = jnp.cumsum(padded)
    pstarts = pends - padded
    dest = pstarts[e_sorted] + jnp.arange(A) - starts[e_sorted]
    n_blocks = -(-A // blk) + N_EXPERTS
    R = n_blocks * blk
    row_tok = jnp.full((R,), N, jnp.int32).at[dest].set((order // TOP_K).astype(jnp.int32))
    row_w = jnp.zeros((R,), wts.dtype).at[dest].set(wts.reshape(A)[order])
    blk_e = jnp.minimum(jnp.searchsorted(pends, jnp.arange(n_blocks) * blk, side='right'),
                        N_EXPERTS - 1)

    def one_block(args):
        tok, wr, e = args
        xb = hf[jnp.minimum(tok, N - 1)]
        hb = jax.nn.silu(xb @ w_gate[e]) * (xb @ w_up[e])
        return (hb @ w_down[e]) * wr[:, None].astype(hf.dtype)

    out = lax.map(one_block, (row_tok.reshape(n_blocks, blk), row_w.reshape(n_blocks, blk), blk_e))
    return jax.ops.segment_sum(out.reshape(R, -1), row_tok, num_segments=N + 1)[:N]


def moe(h, w_router, b_router, w_gate, w_up, w_down, w_sh_gate, w_sh_up, w_sh_down):
    B, T, D = h.shape
    hf = h.reshape(B * T, D)
    N = B * T
    scores = jax.nn.sigmoid((hf @ w_router).astype(jnp.float32))
    biased = scores + b_router.astype(jnp.float32)
    grp_score = lax.top_k(biased.reshape(N, N_GROUPS, N_EXPERTS // N_GROUPS), 2)[0].sum(-1)
    _, gidx = lax.top_k(grp_score, TOPK_GROUPS)
    gmask = jnp.any(gidx[..., None] == jnp.arange(N_GROUPS), axis=1)
    emask = jnp.repeat(gmask, N_EXPERTS // N_GROUPS, axis=1)
    _, eidx = lax.top_k(jnp.where(emask, biased, -jnp.inf), TOP_K)
    wts = jnp.take_along_axis(scores, eidx, axis=1)
    wts = wts / jnp.sum(wts, axis=-1, keepdims=True) * ROUTED_SCALE
    y = routed_experts(hf, eidx, wts, w_gate, w_up, w_down)
    y = y + (jax.nn.silu(hf @ w_sh_gate) * (hf @ w_sh_up)) @ w_sh_down
    return y.reshape(B, T, D).astype(h.dtype)


def decoder_layer(x, c, mix_fn, layer_idx, w_ada, b_ada, g_attn, g_ffn, w_in,
                  lq1, lk1, lq2, lk2, g_subln, w_proj_da, w_proj_sa, w_out,
                  w_router, b_router, w_gate, w_up, w_down, w_sh_gate, w_sh_up, w_sh_down):
    B, T, _ = x.shape
    lam_init = 0.8 - 0.6 * math.exp(-0.3 * layer_idx)
    mod = (jax.nn.silu(c) @ w_ada + b_ada)[:, None, :]
    sh1, sc1, gt1, sh2, sc2, gt2 = jnp.split(mod, 6, axis=-1)
    h = rmsnorm(x, g_attn) * (1 + sc1) + sh1
    parts = jnp.split(h @ w_in, IN_SPLIT_POINTS, axis=-1)
    q_da = parts[0].reshape(B, T, DA_HEADS, 2, DA_DK)
    k_da = parts[1].reshape(B, T, DA_HEADS, 2, DA_DK)
    v_da = parts[2].reshape(B, T, DA_HEADS, DA_DV)
    q_sa = parts[3].reshape(B, T, SA_HEADS, SA_DH)
    k_sa = parts[4].reshape(B, T, SA_KV_HEADS, SA_DH)
    v_sa = parts[5].reshape(B, T, SA_KV_HEADS, SA_DH)
    q_ix = parts[6].reshape(B, T, IDX_HEADS, IDX_DK)
    k_ix = parts[7]
    w_ix = parts[8]
    gate_da, gate_sa = parts[9], parts[10]
    lam = (jnp.exp(jnp.sum(lq1.astype(jnp.float32) * lk1.astype(jnp.float32)))
           - jnp.exp(jnp.sum(lq2.astype(jnp.float32) * lk2.astype(jnp.float32))) + lam_init)
    o_da, o_sa = mix_fn(q_da, k_da, v_da, q_sa, k_sa, v_sa, q_ix, k_ix, w_ix, lam)
    o_da = rmsnorm(o_da, g_subln) * (1 - lam_init)
    m = (jax.nn.sigmoid(gate_da) * (o_da.reshape(B, T, -1) @ w_proj_da)
         + jax.nn.sigmoid(gate_sa) * (o_sa.reshape(B, T, -1) @ w_proj_sa))
    x = x + gt1 * (m @ w_out)
    h2 = rmsnorm(x, g_ffn) * (1 + sc2) + sh2
    x = x + gt2 * moe(h2, w_router, b_router, w_gate, w_up, w_down, w_sh_gate, w_sh_up, w_sh_down)
    return x, (k_da, v_da, k_sa, v_sa, k_ix)


def setup_inputs(seed: int = 0) -> dict:
    key = jax.random.key(seed)
    k = jax.random.split(key, 33)
    n_pages = PAST_LEN // PAGE_SIZE
    n_used = DEC_BATCH * n_pages
    n_pool = n_used + max(1, n_used // 4)
    D = D_MODEL
    L = DEPTH
    E = N_EXPERTS

    def nrm(kk, shape, scale):
        return jax.random.normal(kk, shape, jnp.float32) * scale

    page_table = jax.random.permutation(k[9], n_pool)[:n_used].reshape(DEC_BATCH, n_pages).astype(jnp.int32)
    return {
        'x_prompt': nrm(k[0], (BATCH, SEQ, D), 1.0),
        'x_sample': nrm(k[1], (DEC_BATCH, DEC_SEQ, D), 1.0),
        'c_prompt': nrm(k[2], (BATCH, D), 1.0),
        'c_sample': nrm(k[3], (DEC_BATCH, D), 1.0),
        'cache_da_k': nrm(k[4], (L, n_pool, PAGE_SIZE, DA_HEADS, 2, DA_DK), 1.0),
        'cache_da_v': nrm(k[5], (L, n_pool, PAGE_SIZE, DA_HEADS, DA_DV), 1.0),
        'cache_sa_k': nrm(k[6], (L, n_pool, PAGE_SIZE, SA_KV_HEADS, SA_DH), 1.0),
        'cache_sa_v': nrm(k[7], (L, n_pool, PAGE_SIZE, SA_KV_HEADS, SA_DH), 1.0),
        'cache_idx_k': nrm(k[8], (L, n_pool, PAGE_SIZE, IDX_DK), 1.0),
        'page_table': page_table,
        'rel_bias_table': nrm(k[10], (N_BUCKETS, N_ALL_HEADS), 0.5),
        'w_ada': nrm(k[11], (L, D, 6 * D), 0.5 * D ** -0.5),
        'b_ada': nrm(k[12], (L, 6 * D), 0.02),
        'g_attn': 1.0 + nrm(k[13], (L, D), 0.02),
        'g_ffn': 1.0 + nrm(k[14], (L, D), 0.02),
        'w_in': nrm(k[15], (L, D, IN_COLS), D ** -0.5),
        'lambda_q1': nrm(k[16], (L, DA_DK), 0.1),
        'lambda_k1': nrm(k[17], (L, DA_DK), 0.1),
        'lambda_q2': nrm(k[18], (L, DA_DK), 0.1),
        'lambda_k2': nrm(k[19], (L, DA_DK), 0.1),
        'g_subln': 1.0 + nrm(k[20], (L, DA_DV), 0.02),
        'w_proj_da': nrm(k[21], (L, DA_HEADS * DA_DV, D), (DA_HEADS * DA_DV) ** -0.5),
        'w_proj_sa': nrm(k[22], (L, SA_HEADS * SA_DH, D), (SA_HEADS * SA_DH) ** -0.5),
        'w_out': nrm(k[23], (L, D, D), D ** -0.5),
        'w_router': nrm(k[24], (L, D, E), D ** -0.5),
        'b_router': nrm(k[25], (L, E), 0.01),
        'w_gate': nrm(k[26], (L, E, D, D_EXPERT), D ** -0.5),
        'w_up': nrm(k[27], (L, E, D, D_EXPERT), D ** -0.5),
        'w_down': nrm(k[28], (L, E, D_EXPERT, D), D_EXPERT ** -0.5),
        'w_sh_gate': nrm(k[29], (L, D, D_SHARED), D ** -0.5),
        'w_sh_up': nrm(k[30], (L, D, D_SHARED), D ** -0.5),
        'w_sh_down': nrm(k[31], (L, D_SHARED, D), D_SHARED ** -0.5),
        'g_final': 1.0 + nrm(k[32], (D,), 0.02),
    }


def reference(x_prompt, x_sample, c_prompt, c_sample, cache_da_k, cache_da_v, cache_sa_k,
              cache_sa_v, cache_idx_k, page_table, rel_bias_table, w_ada, b_ada, g_attn, g_ffn,
              w_in, lambda_q1, lambda_k1, lambda_q2, lambda_k2, g_subln, w_proj_da, w_proj_sa,
              w_out, w_router, b_router, w_gate, w_up, w_down, w_sh_gate, w_sh_up, w_sh_down,
              g_final):
    seq = x_prompt.shape[1]
    dec_seq = x_sample.shape[1]
    past_len = page_table.shape[1] * PAGE_SIZE
    pos_p = jnp.arange(seq, dtype=jnp.int32)
    q_pos_s = past_len + jnp.arange(dec_seq, dtype=jnp.int32)
    k_pos_s = jnp.arange(past_len + dec_seq, dtype=jnp.int32)

    def prompt_mix(q_da, k_da, v_da, q_sa, k_sa, v_sa, q_ix, k_ix, w_ix, lam):
        o_da = diff_attention(q_da, k_da, v_da, pos_p, pos_p, lam, rel_bias_table)
        o_sa = sparse_attention(q_sa, q_ix, w_ix, k_ix, pos_p, pos_p,
                                lambda sel: (take_rows(k_sa, sel), take_rows(v_sa, sel)),
                                rel_bias_table)
        return o_da, o_sa

    def sample_mix(l):
        pk_da, pv_da = cache_da_k[l], cache_da_v[l]
        pk_sa, pv_sa, pk_ix = cache_sa_k[l], cache_sa_v[l], cache_idx_k[l]

        def mix(q_da, k_da, v_da, q_sa, k_sa, v_sa, q_ix, k_ix, w_ix, lam):
            k_all = jnp.concatenate([gather_all_pages(pk_da, page_table), k_da], axis=1)
            v_all = jnp.concatenate([gather_all_pages(pv_da, page_table), v_da], axis=1)
            o_da = diff_attention(q_da, k_all, v_all, q_pos_s, k_pos_s, lam, rel_bias_table)
            k_ix_all = jnp.concatenate([gather_all_pages(pk_ix, page_table), k_ix], axis=1)

            def gather_kv(sel):
                in_past = (sel < past_len)[..., None, None]
                pp = jnp.minimum(sel, past_len - 1)
                nw = jnp.clip(sel - past_len, 0, dec_seq - 1)
                ks = jnp.where(in_past, gather_paged(pk_sa, page_table, pp), take_rows(k_sa, nw))
                vs = jnp.where(in_past, gather_paged(pv_sa, page_table, pp), take_rows(v_sa, nw))
                return ks, vs

            o_sa = sparse_attention(q_sa, q_ix, w_ix, k_ix_all, q_pos_s, k_pos_s, gather_kv,
                                    rel_bias_table)
            return o_da, o_sa
        return mix

    xp, xs = x_prompt, x_sample
    rows_p, rows_s = [], []
    for l in range(DEPTH):
        lw = (w_ada[l], b_ada[l], g_attn[l], g_ffn[l], w_in[l], lambda_q1[l], lambda_k1[l],
              lambda_q2[l], lambda_k2[l], g_subln[l], w_proj_da[l], w_proj_sa[l], w_out[l],
              w_router[l], b_router[l], w_gate[l], w_up[l], w_down[l], w_sh_gate[l],
              w_sh_up[l], w_sh_down[l])
        xp, rp = decoder_layer(xp, c_prompt, prompt_mix, l, *lw)
        xs, rs = decoder_layer(xs, c_sample, sample_mix(l), l, *lw)
        rows_p.append(rp)
        rows_s.append(rs)

    y_prompt = rmsnorm(xp, g_final)
    y_sample = rmsnorm(xs, g_final)
    new_da_k_p = jnp.stack([r[0] for r in rows_p])
    new_da_v_p = jnp.stack([r[1] for r in rows_p])
    new_sa_k_p = jnp.stack([r[2] for r in rows_p])
    new_sa_v_p = jnp.stack([r[3] for r in rows_p])
    new_idx_k_p = jnp.stack([r[4] for r in rows_p])
    new_da_k_s = jnp.stack([r[0] for r in rows_s])
    new_da_v_s = jnp.stack([r[1] for r in rows_s])
    new_sa_k_s = jnp.stack([r[2] for r in rows_s])
    new_sa_v_s = jnp.stack([r[3] for r in rows_s])
    new_idx_k_s = jnp.stack([r[4] for r in rows_s])
    return (y_prompt, y_sample, new_da_k_p, new_da_v_p, new_sa_k_p, new_sa_v_p, new_idx_k_p,
            new_da_k_s, new_da_v_s, new_sa_k_s, new_sa_v_s, new_idx_k_s)
```

```python
import functools
import math

import jax
import jax.numpy as jnp
from jax import lax
from jax.experimental import pallas as pl
from jax.experimental.pallas import tpu as pltpu

DA_HEADS = 8
DA_DK = 64
DA_DV = 2 * DA_DK
SA_HEADS = 8
SA_KV_HEADS = 2
SA_DH = 128
SA_GROUP = SA_HEADS // SA_KV_HEADS
IDX_HEADS = 16
IDX_DK = 64
SA_TOPK_MAX = 256
N_BUCKETS = 32
MAX_DISTANCE = 128
N_EXPERTS = 64
N_GROUPS = 8
GROUP_SIZE = N_EXPERTS // N_GROUPS
TOPK_GROUPS = 4
TOP_K = 8
ROUTED_SCALE = 2.5
PAGE_SIZE = 128
EPS = 1e-6

LANES = 128
VMEM_LIMIT = 56 * 1024 * 1024

BF16 = jnp.bfloat16
NEG = -1e30
INT_MIN = -(2 ** 31)
KEY_NEG_INF = (0xFF800000 ^ 0x7FFFFFFF) - (1 << 32)

_NT = (((1,), (1,)), ((), ()))


def _cparams(sem):
    return pltpu.CompilerParams(dimension_semantics=sem, vmem_limit_bytes=VMEM_LIMIT)


def _dot(a, b):
    return jnp.dot(a, b, preferred_element_type=jnp.float32)


def _dot_nt(a, b):
    return lax.dot_general(a, b, _NT, preferred_element_type=jnp.float32)


def _bf(x):
    return x.astype(BF16)


def _rounded(x):
    return x.astype(BF16).astype(jnp.float32)


def _sigmoid(x):
    return 1.0 / (1.0 + jnp.exp(-x))


def _silu(x):
    return x * _sigmoid(x)


def _float_key(s):
    b = pltpu.bitcast(s, jnp.int32)
    return b ^ ((b >> 31) & jnp.int32(0x7FFFFFFF))


def _linear_small_kernel(x_ref, w_ref, b_ref, o_ref, *, silu_in):
    x = x_ref[...]
    if silu_in:
        x = _silu(x)
    o_ref[...] = _dot(_bf(x), _bf(w_ref[...])) + b_ref[...]


def _linear_small(x, w, b=None, *, silu_in=False, tn=512):
    m0, k = x.shape
    m = -(-m0 // 16) * 16
    x = jnp.pad(x, ((0, m - m0), (0, 0)))
    n = w.shape[1]
    tn = min(tn, n)
    if b is None:
        b = jnp.zeros((1, n), jnp.float32)
    out = pl.pallas_call(
        functools.partial(_linear_small_kernel, silu_in=silu_in),
        grid=(pl.cdiv(n, tn),),
        in_specs=[pl.BlockSpec((m, k), lambda j: (0, 0)),
                  pl.BlockSpec((k, tn), lambda j: (0, j)),
                  pl.BlockSpec((1, tn), lambda j: (0, j))],
        out_specs=pl.BlockSpec((m, tn), lambda j: (0, j)),
        out_shape=jax.ShapeDtypeStruct((m, n), jnp.float32),
        compiler_params=_cparams(("arbitrary",)),
        name="linear_small",
    )(x, w, b.reshape(1, n))
    return out[:m0]


def _norm_mod_kernel(x_ref, g_ref, sc_ref, sh_ref, o_ref):
    x = x_ref[...]
    y = x * lax.rsqrt(jnp.mean(x * x, axis=-1, keepdims=True) + EPS) * g_ref[...]
    o_ref[...] = (y * (1.0 + sc_ref[0]) + sh_ref[0]).astype(o_ref.dtype)


def _norm_mod(x, g, sc, sh, rows_per_batch, tm):
    n, d = x.shape
    per = rows_per_batch // tm
    return pl.pallas_call(
        _norm_mod_kernel,
        grid=(n // tm,),
        in_specs=[pl.BlockSpec((tm, d), lambda i: (i, 0)),
                  pl.BlockSpec((1, d), lambda i: (0, 0)),
                  pl.BlockSpec((1, 1, d), lambda i: (i // per, 0, 0)),
                  pl.BlockSpec((1, 1, d), lambda i: (i // per, 0, 0))],
        out_specs=pl.BlockSpec((tm, d), lambda i: (i, 0)),
        out_shape=jax.ShapeDtypeStruct((n, d), BF16),
        compiler_params=_cparams(("arbitrary",)),
        name="norm_mod",
    )(x, g.reshape(1, d), sc[:, None, :], sh[:, None, :])


def _mm_kernel(x_ref, w_ref, *rest, scale, sigmoid, n_out):
    o_refs, wbf_ref = rest[:n_out], rest[n_out]

    @pl.when(pl.program_id(1) == 0)
    def _():
        wbf_ref[...] = w_ref[...].astype(BF16)

    acc = _dot(x_ref[...], wbf_ref[...])
    if scale != 1.0:
        acc = acc * scale
    if sigmoid:
        acc = _sigmoid(acc)
    for o in o_refs:
        o[...] = acc.astype(o.dtype)


def _mm(x, w, col0, ncols, out_dtypes, *, scale=1.0, sigmoid=False, tm=512, tn=512):
    m, k = x.shape
    tn = min(tn, ncols)
    tm = min(tm, m)
    assert col0 % tn == 0 and ncols % tn == 0 and m % tm == 0
    jb = col0 // tn
    outs = pl.pallas_call(
        functools.partial(_mm_kernel, scale=scale, sigmoid=sigmoid, n_out=len(out_dtypes)),
        grid=(ncols // tn, m // tm),
        in_specs=[pl.BlockSpec((tm, k), lambda j, i: (i, 0)),
                  pl.BlockSpec((k, tn), lambda j, i: (0, jb + j))],
        out_specs=[pl.BlockSpec((tm, tn), lambda j, i: (i, j)) for _ in out_dtypes],
        out_shape=[jax.ShapeDtypeStruct((m, ncols), dt) for dt in out_dtypes],
        scratch_shapes=[pltpu.VMEM((k, tn), BF16)],
        compiler_params=_cparams(("arbitrary", "arbitrary")),
        name="mm_cols",
    )(x, w)
    return outs


def _rel_bucket(dist):
    max_exact = N_BUCKETS // 2
    d = jnp.maximum(dist, 0)
    large = max_exact + (jnp.log(jnp.maximum(d, 1).astype(jnp.float32) / max_exact)
                         / math.log(MAX_DISTANCE / max_exact)
                         * (N_BUCKETS - max_exact)).astype(jnp.int32)
    large = jnp.minimum(large, N_BUCKETS - 1)
    return jnp.where(d < max_exact, d, large)


def _bias_by_distance(table, dists):
    return table[_rel_bucket(dists)].astype(jnp.float32).T


def _near_tiles(table, t):
    i = jnp.arange(t, dtype=jnp.int32)[:, None]
    j = jnp.arange(t, dtype=jnp.int32)[None, :]
    tiles = []
    for off in (0, t):
        d = off + i - j
        b = _bias_by_distance(table, d.reshape(-1)).reshape(-1, t, t)
        tiles.append(jnp.where(d[None] >= 0, b, NEG))
    return jnp.stack(tiles, axis=1)


def _flash_step(s, v, m_ref, l_ref, acc_ref, shift):
    m_old = m_ref[...]
    m_new = jnp.maximum(m_old, jnp.max(s, axis=-1, keepdims=True) + shift)
    p = jnp.exp(s - (m_new - shift))
    alpha = jnp.exp(m_old - m_new)
    l_ref[...] = alpha * l_ref[...] + jnp.sum(p, axis=-1, keepdims=True)
    acc_ref[...] = alpha * acc_ref[...] + _dot(p.astype(v.dtype), v)
    m_ref[...] = m_new


def _da_prompt_kernel(cfar_ref, lam_ref, q_ref, k_ref, v_ref, tile_ref, g_ref, o_ref,
                      m_ref, l_ref, acc_ref, *, tq, out_scale):
    h = pl.program_id(1)
    qi = pl.program_id(2)
    q = q_ref[...]
    lane = lax.broadcasted_iota(jnp.int32, q.shape, 1)
    zero = jnp.zeros_like(q)
    qm = (jnp.where(lane < DA_DK, q, zero), jnp.where(lane >= DA_DK, q, zero))
    cfar = cfar_ref[h]

    m_ref[...] = jnp.full(m_ref.shape, NEG, jnp.float32)
    l_ref[...] = jnp.zeros(l_ref.shape, jnp.float32)
    acc_ref[...] = jnp.zeros(acc_ref.shape, jnp.float32)

    def chunk(kc, bias, shift):
        rows = pl.ds(pl.multiple_of(kc * tq, tq), tq)
        ks = k_ref[rows, :]
        vs = v_ref[rows, :]
        for mp in range(2):
            s = _dot_nt(qm[mp], ks)
            if bias is not None:
                s = s + bias
            _flash_step(s, vs, m_ref.at[mp], l_ref.at[mp], acc_ref.at[mp], shift)

    def far_body(kc, carry):
        chunk(kc, None, cfar)
        return carry

    lax.fori_loop(0, jnp.maximum(qi - 1, 0), far_body, 0)

    @pl.when(qi >= 1)
    def _():
        chunk(qi - 1, tile_ref[0, 1], 0.0)

    chunk(qi, tile_ref[0, 0], 0.0)

    lam = lam_ref[0]
    o = acc_ref[0] / l_ref[0] - lam * (acc_ref[1] / l_ref[1])
    o = o * lax.rsqrt(jnp.mean(o * o, axis=-1, keepdims=True) + EPS) * g_ref[...]
    o_ref[...] = (o * out_scale).astype(o_ref.dtype)


def _da_prompt(q, k, v, tiles, cfar, lam, g_subln, nb, t, tq, out_scale):
    n = q.shape[0]
    nq = t // tq
    grid_spec = pltpu.PrefetchScalarGridSpec(
        num_scalar_prefetch=0,
        grid=(nb, DA_HEADS, nq),
        in_specs=[pl.BlockSpec(memory_space=pltpu.SMEM),
                  pl.BlockSpec(memory_space=pltpu.SMEM),
                  pl.BlockSpec((tq, LANES), lambda b, h, i: (b * nq + i, h)),
                  pl.BlockSpec((t, LANES), lambda b, h, i: (b, h)),
                  pl.BlockSpec((t, LANES), lambda b, h, i: (b, h)),
                  pl.BlockSpec((1, 2, tq, tq), lambda b, h, i: (h, 0, 0, 0)),
                  pl.BlockSpec((1, DA_DV), lambda b, h, i: (0, 0))],
        out_specs=pl.BlockSpec((tq, LANES), lambda b, h, i: (b * nq + i, h)),
        scratch_shapes=[pltpu.VMEM((2, tq, 1), jnp.float32),
                        pltpu.VMEM((2, tq, 1), jnp.float32),
                        pltpu.VMEM((2, tq, DA_DV), jnp.float32)])
    return pl.pallas_call(
        functools.partial(_da_prompt_kernel, tq=tq, out_scale=out_scale),
        grid_spec=grid_spec,
        out_shape=jax.ShapeDtypeStruct((n, DA_HEADS * DA_DV), BF16),
        compiler_params=_cparams(("arbitrary", "arbitrary", "arbitrary")),
        name="da_prompt",
    )(cfar, lam, q, k, v, tiles, g_subln.reshape(1, DA_DV))


def _count_ge(key_ref, cand, n_chunks, tq):
    def body(kc, acc):
        ge = jnp.where(key_ref[kc] >= cand, 1.0, 0.0)
        for c in range(tq // LANES):
            acc = acc + ge[:, c * LANES:(c + 1) * LANES]
        return acc
    acc = lax.fori_loop(0, n_chunks, body, jnp.zeros((tq, LANES), jnp.float32))
    return jnp.sum(acc, axis=-1, keepdims=True)


def _sa_prompt_kernel(cfar_ref, qs_ref, qx_ref, mq_ref, mk_ref, ks_ref, vs_ref, tile_ref, o_ref,
                      k2_ref, key_ref, cut_ref, m_ref, l_ref, acc_ref, *, tq, topk):
    qi = pl.program_id(1)
    n_chunks = qi + 1
    t = mk_ref.shape[0]

    @pl.when(qi == 0)
    def _():
        kix = mk_ref[:, :LANES]
        lane = lax.broadcasted_iota(jnp.int32, kix.shape, 1)
        k2_ref[0] = jnp.where(lane < IDX_DK, kix, 0.0).astype(BF16)
        k2_ref[1] = jnp.where(lane >= IDX_DK, pltpu.roll(kix, IDX_DK, axis=1), 0.0).astype(BF16)

    wix = mq_ref[:, IDX_DK:IDX_DK + IDX_HEADS]
    wcols = [wix[:, hh:hh + 1] for hh in range(IDX_HEADS)]
    row = lax.broadcasted_iota(jnp.int32, (tq, tq), 0)
    col = lax.broadcasted_iota(jnp.int32, (tq, tq), 1)

    def score_body(kc, carry):
        rows = pl.ds(pl.multiple_of(kc * tq, tq), tq)
        ke = k2_ref[0, rows, :]
        ko = k2_ref[1, rows, :]
        sc = jnp.zeros((tq, tq), jnp.float32)
        for p in range(IDX_HEADS // 2):
            qp = qx_ref[:, p * LANES:(p + 1) * LANES]
            sc = sc + wcols[2 * p] * jnp.maximum(_dot_nt(qp, ke), 0.0)
            sc = sc + wcols[2 * p + 1] * jnp.maximum(_dot_nt(qp, ko), 0.0)
        sc = sc * (IDX_DK ** -0.5 * IDX_HEADS ** -0.5)
        sc = jnp.where((kc < qi) | (row >= col), sc, -jnp.inf)
        key_ref[kc] = _float_key(sc)
        return carry

    lax.fori_loop(0, n_chunks, score_body, 0)

    def bit_body(it, carry):
        thr, cnt_thr = carry
        cand = thr + (jnp.int32(1) << (31 - it))
        cnt = _count_ge(key_ref, cand, n_chunks, tq)
        ok = cnt >= float(topk)
        return jnp.where(ok, cand, thr), jnp.where(ok, cnt, cnt_thr)

    thr0 = jnp.full((tq, 1), INT_MIN, jnp.int32)
    cnt0 = jnp.full((tq, 1), 3.0e38, jnp.float32)
    thr, cnt_thr = lax.fori_loop(0, 32, bit_body, (thr0, cnt0))
    tied = (cnt_thr > float(topk)) & (thr > jnp.int32(KEY_NEG_INF))
    need_tie = jnp.max(jnp.where(tied, 1.0, 0.0)) > 0.0
    thr = jnp.maximum(thr, jnp.int32(KEY_NEG_INF + 1))

    cut_ref[...] = jnp.full((tq, 1), 2 ** 30, jnp.int32)

    @pl.when(need_tie)
    def _():
        def gt_body(kc, acc):
            g = jnp.where(key_ref[kc] > thr, 1.0, 0.0)
            return acc + jnp.sum(g, axis=-1, keepdims=True)
        n_gt = lax.fori_loop(0, n_chunks, gt_body, jnp.zeros((tq, 1), jnp.float32))
        need = float(topk) - n_gt
        n_bits = max(1, int(t).bit_length())

        def cut_body(it, cut):
            cand = cut + (jnp.int32(1) << (n_bits - 1 - it))

            def eq_body(kc, acc):
                pos = kc * tq + col
                e = jnp.where((key_ref[kc] == thr) & (pos < cand), 1.0, 0.0)
                return acc + jnp.sum(e, axis=-1, keepdims=True)
            n_eq = lax.fori_loop(0, n_chunks, eq_body, jnp.zeros((tq, 1), jnp.float32))
            return jnp.where(n_eq <= need, cand, cut)
        cut_ref[...] = lax.fori_loop(0, n_bits, cut_body, jnp.zeros((tq, 1), jnp.int32))

    cut = cut_ref[...]

    m_ref[...] = jnp.full(m_ref.shape, NEG, jnp.float32)
    l_ref[...] = jnp.zeros(l_ref.shape, jnp.float32)
    acc_ref[...] = jnp.zeros(acc_ref.shape, jnp.float32)
    qg = [jnp.concatenate([qs_ref[:, (g * SA_GROUP + j) * LANES:(g * SA_GROUP + j + 1) * LANES]
                           for j in range(SA_GROUP)], axis=0) for g in range(SA_KV_HEADS)]

    def attend(kc, kind):
        rows = pl.ds(pl.multiple_of(kc * tq, tq), tq)
        key = key_ref[kc]
        pos = kc * tq + col
        sel = (key > thr) | ((key == thr) & (pos < cut))
        for g in range(SA_KV_HEADS):
            ks = ks_ref[rows, g * SA_DH:(g + 1) * SA_DH]
            vs = vs_ref[rows, g * SA_DH:(g + 1) * SA_DH]
            s_all = _dot_nt(qg[g], ks)
            ps = []
            for j in range(SA_GROUP):
                hh = g * SA_GROUP + j
                s = s_all[j * tq:(j + 1) * tq]
                if kind == 'far':
                    s = s + cfar_ref[hh]
                else:
                    s = s + tile_ref[hh, kind]
                s = jnp.where(sel, s, NEG)
                m_old = m_ref[hh]
                m_new = jnp.maximum(m_old, jnp.max(s, axis=-1, keepdims=True))
                p = jnp.exp(s - m_new)
                alpha = jnp.exp(m_old - m_new)
                l_ref[hh] = alpha * l_ref[hh] + jnp.sum(p, axis=-1, keepdims=True)
                acc_ref[hh] = alpha * acc_ref[hh]
                m_ref[hh] = m_new
                ps.append(p.astype(BF16))
            pv = _dot(jnp.concatenate(ps, axis=0), vs)
            for j in range(SA_GROUP):
                hh = g * SA_GROUP + j
                acc_ref[hh] = acc_ref[hh] + pv[j * tq:(j + 1) * tq]

    def far_body(kc, carry):
        attend(kc, 'far')
        return carry

    lax.fori_loop(0, jnp.maximum(qi - 1, 0), far_body, 0)

    @pl.when(qi >= 1)
    def _():
        attend(qi - 1, 1)

    attend(qi, 0)

    for hh in range(SA_HEADS):
        o_ref[:, hh * SA_DH:(hh + 1) * SA_DH] = (acc_ref[hh] / l_ref[hh]).astype(o_ref.dtype)


def _sa_prompt(q_sa, q_ix, misc, k_sa, v_sa, tiles, cfar, nb, t, tq, topk):
    n = q_sa.shape[0]
    nq = t // tq
    mw = misc.shape[1]
    kvw = SA_KV_HEADS * SA_DH
    return pl.pallas_call(
        functools.partial(_sa_prompt_kernel, tq=tq, topk=topk),
        grid=(nb, nq),
        in_specs=[pl.BlockSpec(memory_space=pltpu.SMEM),
                  pl.BlockSpec((tq, SA_HEADS * SA_DH), lambda b, i: (b * nq + i, 0)),
                  pl.BlockSpec((tq, IDX_HEADS * IDX_DK), lambda b, i: (b * nq + i, 0)),
                  pl.BlockSpec((tq, mw), lambda b, i: (b * nq + i, 0)),
                  pl.BlockSpec((t, mw), lambda b, i: (b, 0)),
                  pl.BlockSpec((t, kvw), lambda b, i: (b, 0)),
                  pl.BlockSpec((t, kvw), lambda b, i: (b, 0)),
                  pl.BlockSpec((SA_HEADS, 2, tq, tq), lambda b, i: (0, 0, 0, 0))],
        out_specs=pl.BlockSpec((tq, SA_HEADS * SA_DH), lambda b, i: (b * nq + i, 0)),
        out_shape=jax.ShapeDtypeStruct((n, SA_HEADS * SA_DH), BF16),
        scratch_shapes=[pltpu.VMEM((2, t, LANES), BF16),
                        pltpu.VMEM((nq, tq, tq), jnp.int32),
                        pltpu.VMEM((tq, 1), jnp.int32),
                        pltpu.VMEM((SA_HEADS, tq, 1), jnp.float32),
                        pltpu.VMEM((SA_HEADS, tq, 1), jnp.float32),
                        pltpu.VMEM((SA_HEADS, tq, SA_DH), jnp.float32)],
        compiler_params=_cparams(("arbitrary", "arbitrary")),
        name="sa_prompt",
    )(cfar, q_sa, q_ix, misc, misc, k_sa, v_sa, tiles)


def _proj_gate_kernel(oda_ref, osa_ref, wpd_ref, wps_ref, gda_ref, gsa_ref, o_ref, wbf_ref):
    @pl.when(pl.program_id(1) == 0)
    def _():
        wbf_ref[0] = wpd_ref[...].astype(BF16)
        wbf_ref[1] = wps_ref[...].astype(BF16)

    a = _dot(oda_ref[...], wbf_ref[0])
    b = _dot(osa_ref[...], wbf_ref[1])
    o_ref[...] = (gda_ref[...].astype(jnp.float32) * a
                  + gsa_ref[...].astype(jnp.float32) * b).astype(o_ref.dtype)


def _proj_gate(o_da, o_sa, w_pd, w_ps, gates, d, tm=512, tn=512):
    n, kd = o_da.shape
    ks = o_sa.shape[1]
    tn = min(tn, d)
    tm = min(tm, n)
    nj = d // tn
    return pl.pallas_call(
        _proj_gate_kernel,
        grid=(nj, n // tm),
        in_specs=[pl.BlockSpec((tm, kd), lambda j, i: (i, 0)),
                  pl.BlockSpec((tm, ks), lambda j, i: (i, 0)),
                  pl.BlockSpec((kd, tn), lambda j, i: (0, j)),
                  pl.BlockSpec((ks, tn), lambda j, i: (0, j)),
                  pl.BlockSpec((tm, tn), lambda j, i: (i, j)),
                  pl.BlockSpec((tm, tn), lambda j, i: (i, nj + j))],
        out_specs=pl.BlockSpec((tm, tn), lambda j, i: (i, j)),
        out_shape=jax.ShapeDtypeStruct((n, d), BF16),
        scratch_shapes=[pltpu.VMEM((2, kd, tn), BF16)],
        compiler_params=_cparams(("arbitrary", "arbitrary")),
        name="proj_gate",
    )(o_da, o_sa, w_pd, w_ps, gates, gates)


def _mm_resid_kernel(m_ref, w_ref, x_ref, gt_ref, o_ref, wbf_ref):
    @pl.when(pl.program_id(1) == 0)
    def _():
        wbf_ref[...] = w_ref[...].astype(BF16)

    o_ref[...] = x_ref[...] + gt_ref[0] * _dot(m_ref[...], wbf_ref[...])


def _mm_resid(m, w, x, gt, rows_per_batch, tm=512, tn=512):
    n, k = m.shape
    d = w.shape[1]
    tn = min(tn, d)
    tm = min(tm, n)
    per = rows_per_batch // tm
    return pl.pallas_call(
        _mm_resid_kernel,
        grid=(d // tn, n // tm),
        in_specs=[pl.BlockSpec((tm, k), lambda j, i: (i, 0)),
                  pl.BlockSpec((k, tn), lambda j, i: (0, j)),
                  pl.BlockSpec((tm, tn), lambda j, i: (i, j)),
                  pl.BlockSpec((1, 1, tn), lambda j, i: (i // per, 0, j))],
        out_specs=pl.BlockSpec((tm, tn), lambda j, i: (i, j)),
        out_shape=jax.ShapeDtypeStruct((n, d), jnp.float32),
        scratch_shapes=[pltpu.VMEM((k, tn), BF16)],
        compiler_params=_cparams(("arbitrary", "arbitrary")),
        name="mm_resid",
    )(m, w, x, gt[:, None, :])


def _norm_router_kernel(x_ref, g_ref, sc_ref, sh_ref, wr_ref, th_ref, tl_ref, h_ref, lg_ref):
    last = pl.num_programs(0) - 1

    @pl.when(pl.program_id(0) < last)
    def _():
        x = x_ref[...]
        y = x * lax.rsqrt(jnp.mean(x * x, axis=-1, keepdims=True) + EPS) * g_ref[...]
        h = y * (1.0 + sc_ref[0]) + sh_ref[0]
        h_ref[...] = h.astype(h_ref.dtype)
        lg_ref[...] = _dot_nt(_bf(wr_ref[...]), _bf(h))

    @pl.when(pl.program_id(0) == last)
    def _():
        h_ref[...] = th_ref[...]
        lg_ref[...] = tl_ref[...]


def _norm_router(x, g, sc, sh, w_router_t, tail_h, tail_lg, rows_per_batch, tm):
    n, d = x.shape
    per = rows_per_batch // tm
    nt = n // tm
    assert tail_h.shape == (tm, d) and tail_lg.shape == (N_EXPERTS, tm)
    row = lambda i: jnp.minimum(i, nt - 1)
    return pl.pallas_call(
        _norm_router_kernel,
        grid=(nt + 1,),
        in_specs=[pl.BlockSpec((tm, d), lambda i: (row(i), 0)),
                  pl.BlockSpec((1, d), lambda i: (0, 0)),
                  pl.BlockSpec((1, 1, d), lambda i: (row(i) // per, 0, 0)),
                  pl.BlockSpec((1, 1, d), lambda i: (row(i) // per, 0, 0)),
                  pl.BlockSpec((N_EXPERTS, d), lambda i: (0, 0)),
                  pl.BlockSpec((tm, d), lambda i: (0, 0)),
                  pl.BlockSpec((N_EXPERTS, tm), lambda i: (0, 0))],
        out_specs=[pl.BlockSpec((tm, d), lambda i: (i, 0)),
                   pl.BlockSpec((N_EXPERTS, tm), lambda i: (0, i))],
        out_shape=[jax.ShapeDtypeStruct((n + tm, d), BF16),
                   jax.ShapeDtypeStruct((N_EXPERTS, n + tm), jnp.float32)],
        compiler_params=_cparams(("arbitrary",)),
        name="norm_router",
    )(x, g.reshape(1, d), sc[:, None, :], sh[:, None, :], w_router_t, tail_h, tail_lg)


def _route_kernel(lg_ref, b_ref, eidx_ref, wts_ref, rank_ref, cnt_ref, *, tn):
    shape = (N_GROUPS, GROUP_SIZE, tn)
    sc = _sigmoid(lg_ref[...])
    biased = sc + b_ref[...]
    e_iota = lax.broadcasted_iota(jnp.int32, shape, 1)
    g_iota3 = lax.broadcasted_iota(jnp.int32, shape, 0)
    flat_iota = g_iota3 * GROUP_SIZE + e_iota
    g_iota = lax.broadcasted_iota(jnp.int32, (N_GROUPS, 1, tn), 0)
    ninf = -jnp.inf

    m1 = jnp.max(biased, axis=1, keepdims=True)
    first = jnp.min(jnp.where(biased == m1, e_iota, GROUP_SIZE), axis=1, keepdims=True)
    m2 = jnp.max(jnp.where(e_iota == first, ninf, biased), axis=1, keepdims=True)
    cur = m1 + m2
    gsel = jnp.zeros((N_GROUPS, 1, tn), jnp.float32)
    for _ in range(TOPK_GROUPS):
        mx = jnp.max(cur, axis=0, keepdims=True)
        idx = jnp.min(jnp.where(cur == mx, g_iota, N_GROUPS), axis=0, keepdims=True)
        hit = g_iota == idx
        gsel = jnp.where(hit, 1.0, gsel)
        cur = jnp.where(hit, ninf, cur)

    cur = jnp.where(gsel > 0.0, biased, ninf)
    hits, ws = [], []
    for k in range(TOP_K):
        mx = jnp.max(jnp.max(cur, axis=1, keepdims=True), axis=0, keepdims=True)
        cand = jnp.where(cur == mx, flat_iota, N_EXPERTS)
        idx = jnp.min(jnp.min(cand, axis=1, keepdims=True), axis=0, keepdims=True)
        hit = flat_iota == idx
        w = jnp.sum(jnp.sum(jnp.where(hit, sc, 0.0), axis=1, keepdims=True), axis=0, keepdims=True)
        eidx_ref[k:k + 1, :] = idx.reshape(1, tn)
        hits.append(hit)
        ws.append(w)
        cur = jnp.where(hit, ninf, cur)
    wsum = ws[0]
    for w in ws[1:]:
        wsum = wsum + w
    for k in range(TOP_K):
        wts_ref[k:k + 1, :] = (ws[k] / wsum * ROUTED_SCALE).reshape(1, tn)

    member = jnp.zeros(shape, jnp.float32)
    for hit in hits:
        member = jnp.where(hit, 1.0, member)
    member2 = member.reshape(N_EXPERTS, tn)
    r = lax.broadcasted_iota(jnp.int32, (tn, tn), 0)
    c = lax.broadcasted_iota(jnp.int32, (tn, tn), 1)
    upper = jnp.where(r < c, 1.0, 0.0).astype(BF16)
    prefix = _dot(member2.astype(BF16), upper).reshape(shape)
    for k in range(TOP_K):
        rk = jnp.sum(jnp.sum(jnp.where(hits[k], prefix, 0.0), axis=1, keepdims=True), axis=0, keepdims=True)
        rank_ref[k:k + 1, :] = rk.reshape(1, tn)
    cnt = jnp.sum(member2, axis=1, keepdims=True)
    cnt_ref[0] = jnp.broadcast_to(cnt, (N_EXPERTS, LANES))


def _route(logits_t, b_router, tn):
    n_pad = logits_t.shape[1]
    nt = n_pad // tn
    lg3 = logits_t.reshape(N_GROUPS, GROUP_SIZE, n_pad)
    b3 = b_router.astype(jnp.float32).reshape(N_GROUPS, GROUP_SIZE, 1)
    row = lambda dt: jax.ShapeDtypeStruct((TOP_K, n_pad), dt)
    return pl.pallas_call(
        functools.partial(_route_kernel, tn=tn),
        grid=(nt,),
        in_specs=[pl.BlockSpec((N_GROUPS, GROUP_SIZE, tn), lambda i: (0, 0, i)),
                  pl.BlockSpec((N_GROUPS, GROUP_SIZE, 1), lambda i: (0, 0, 0))],
        out_specs=[pl.BlockSpec((TOP_K, tn), lambda i: (0, i)),
                   pl.BlockSpec((TOP_K, tn), lambda i: (0, i)),
                   pl.BlockSpec((TOP_K, tn), lambda i: (0, i)),
                   pl.BlockSpec((1, N_EXPERTS, LANES), lambda i: (i, 0, 0))],
        out_shape=[row(jnp.int32), row(jnp.float32), row(jnp.float32),
                   jax.ShapeDtypeStruct((nt, N_EXPERTS, LANES), jnp.float32)],
        compiler_params=_cparams(("arbitrary",)),
        name="route",
    )(lg3, b3)


def _dest_kernel(eidx_ref, rank_ref, base_ref, o_ref, *, tn):
    e_iota = lax.broadcasted_iota(jnp.int32, (N_EXPERTS, tn), 0)
    base = base_ref[0][:, :1]
    for k in range(TOP_K):
        onehot = e_iota == eidx_ref[k:k + 1, :]
        b = jnp.sum(jnp.where(onehot, base, 0.0), axis=0, keepdims=True)
        o_ref[k:k + 1, :] = (b + rank_ref[k:k + 1, :]).astype(jnp.int32)


def _dest(eidx, rank, base, tn):
    n_pad = eidx.shape[1]
    return pl.pallas_call(
        functools.partial(_dest_kernel, tn=tn),
        grid=(n_pad // tn,),
        in_specs=[pl.BlockSpec((TOP_K, tn), lambda i: (0, i)),
                  pl.BlockSpec((TOP_K, tn), lambda i: (0, i)),
                  pl.BlockSpec((1, N_EXPERTS, LANES), lambda i: (i, 0, 0))],
        out_specs=pl.BlockSpec((TOP_K, tn), lambda i: (0, i)),
        out_shape=jax.ShapeDtypeStruct((TOP_K, n_pad), jnp.int32),
        compiler_params=_cparams(("arbitrary",)),
        name="dest",
    )(eidx, rank, base)


def _pack_words(lo_f32, hi_f32):
    lo = lax.shift_right_logical(pltpu.bitcast(lo_f32, jnp.uint32), jnp.uint32(16))
    hi = pltpu.bitcast(hi_f32, jnp.uint32) & jnp.uint32(0xFFFF0000)
    return hi | lo


def _unpack_words(w):
    lo = pltpu.bitcast(lax.shift_left(w, jnp.uint32(16)), jnp.float32)
    hi = pltpu.bitcast(w & jnp.uint32(0xFFFF0000), jnp.float32)
    return lo, hi


def _bf16_exact(x):
    return x.astype(BF16).astype(jnp.float32)


def _dispatch_kernel(pstart_ref, pcnt_ref, h_ref, dest_hbm, xs_hbm, dsm, pk, zrow, sem_d, sem_r,
                     *, tn, nt):
    i = pl.program_id(0)
    half = pk.shape[1]

    @pl.when(i < nt)
    def _():
        cp = pltpu.make_async_copy(dest_hbm.at[pl.ds(i * (TOP_K * tn), TOP_K * tn)], dsm, sem_d)
        cp.start()
        x = h_ref[...]
        pk[...] = _pack_words(x[:, :half].astype(jnp.float32), x[:, half:].astype(jnp.float32))
        cp.wait()

        def body(r, carry):
            for k in range(TOP_K):
                d = dsm[k * tn + r]
                pltpu.make_async_copy(pk.at[pl.ds(r, 1), :], xs_hbm.at[pl.ds(d, 1), :], sem_r).start()
            return carry

        lax.fori_loop(0, tn, body, 0)
        for k in range(TOP_K):
            pltpu.make_async_copy(pk, xs_hbm.at[pl.ds(0, tn), :], sem_r).wait()

    @pl.when(i == nt)
    def _():
        zrow[...] = jnp.zeros(zrow.shape, zrow.dtype)

        def per_expert(e, carry):
            s0 = pstart_ref[e]
            c = pcnt_ref[e]

            def start(r, cc):
                pltpu.make_async_copy(zrow.at[pl.ds(0, 1), :], xs_hbm.at[pl.ds(s0 + r, 1), :], sem_r).start()
                return cc

            def wait(r, cc):
                pltpu.make_async_copy(zrow.at[pl.ds(0, 1), :], xs_hbm.at[pl.ds(s0, 1), :], sem_r).wait()
                return cc

            lax.fori_loop(0, c, start, 0)
            lax.fori_loop(0, c, wait, 0)
            return carry

        lax.fori_loop(0, N_EXPERTS, per_expert, 0)

        t0 = pstart_ref[N_EXPERTS]
        groups = pcnt_ref[N_EXPERTS]
        rows8 = lambda r: pl.ds(pl.multiple_of(t0 + r * 8, 8), 8)

        def tstart(r, cc):
            pltpu.make_async_copy(zrow, xs_hbm.at[rows8(r), :], sem_r).start()
            return cc

        def twait(r, cc):
            pltpu.make_async_copy(zrow, xs_hbm.at[rows8(0), :], sem_r).wait()
            return cc

        lax.fori_loop(0, groups, tstart, 0)
        lax.fori_loop(0, groups, twait, 0)


def _dispatch(h2, dest_flat, pad_start, pad_cnt, n_rows, tn):
    n_pad, d = h2.shape
    nt = n_pad // tn
    grid_spec = pltpu.PrefetchScalarGridSpec(
        num_scalar_prefetch=2,
        grid=(nt + 1,),
        in_specs=[pl.BlockSpec((tn, d), lambda i, a, b: (jnp.minimum(i, nt - 1), 0)),
                  pl.BlockSpec(memory_space=pl.ANY)],
        out_specs=pl.BlockSpec(memory_space=pl.ANY),
        scratch_shapes=[pltpu.SMEM((TOP_K * tn,), jnp.int32),
                        pltpu.VMEM((tn, d // 2), jnp.uint32),
                        pltpu.VMEM((8, d // 2), jnp.uint32),
                        pltpu.SemaphoreType.DMA(()),
                        pltpu.SemaphoreType.DMA(())])
    return pl.pallas_call(
        functools.partial(_dispatch_kernel, tn=tn, nt=nt),
        grid_spec=grid_spec,
        out_shape=jax.ShapeDtypeStruct((n_rows, d // 2), jnp.uint32),
        compiler_params=_cparams(("arbitrary",)),
        name="dispatch",
    )(pad_start, pad_cnt, h2, dest_flat)


def _expert_kernel(blk_e_ref, nused_ref, x_ref, wg_ref, wu_ref, wd_ref, y_ref, wgb, wub, wdb):
    i = pl.program_id(0)
    nused = nused_ref[0]
    ii = jnp.minimum(i, nused - 1)
    e = blk_e_ref[ii]
    e_prev = blk_e_ref[jnp.maximum(ii - 1, 0)]
    half = x_ref.shape[1]

    @pl.when((i < nused) & ((i == 0) | (e != e_prev)))
    def _():
        wgb[...] = wg_ref[0].astype(BF16)
        wub[...] = wu_ref[0].astype(BF16)
        wdb[...] = wd_ref[0].astype(BF16)

    @pl.when(i < nused)
    def _():
        lo, hi = _unpack_words(x_ref[...])
        xl = lo.astype(BF16)
        xh = hi.astype(BF16)
        g = _dot(xl, wgb[:half, :]) + _dot(xh, wgb[half:, :])
        u = _dot(xl, wub[:half, :]) + _dot(xh, wub[half:, :])
        hmid = (_silu(g) * u).astype(BF16)
        y = _dot(hmid, wdb[...])
        y_ref[...] = _pack_words(_bf16_exact(y[:, :half]), _bf16_exact(y[:, half:]))

    @pl.when(i >= nused)
    def _():
        y_ref[...] = jnp.zeros(y_ref.shape, y_ref.dtype)


def _experts(xs, blk_e, nused, w_gate, w_up, w_down, tb):
    n_rows, half = xs.shape
    _, d, f = w_gate.shape
    nblk = n_rows // tb

    def xmap(i, be, nu):
        return (jnp.minimum(i, nu[0] - 1), 0)

    def wmap(i, be, nu):
        return (be[jnp.minimum(i, nu[0] - 1)], 0, 0)

    grid_spec = pltpu.PrefetchScalarGridSpec(
        num_scalar_prefetch=2,
        grid=(nblk,),
        in_specs=[pl.BlockSpec((tb, half), xmap),
                  pl.BlockSpec((1, d, f), wmap),
                  pl.BlockSpec((1, d, f), wmap),
                  pl.BlockSpec((1, f, d), wmap)],
        out_specs=pl.BlockSpec((tb, half), lambda i, be, nu: (i, 0)),
        scratch_shapes=[pltpu.VMEM((d, f), BF16),
                        pltpu.VMEM((d, f), BF16),
                        pltpu.VMEM((f, d), BF16)])
    return pl.pallas_call(
        _expert_kernel,
        grid_spec=grid_spec,
        out_shape=jax.ShapeDtypeStruct((n_rows, half), jnp.uint32),
        compiler_params=_cparams(("arbitrary",)),
        name="experts",
    )(blk_e, nused, xs, w_gate, w_up, w_down)


def _ffn_kernel(h_ref, wg_ref, wu_ref, wd_ref, o_ref, wgb, wub, wdb):
    @pl.when(pl.program_id(0) == 0)
    def _():
        wgb[...] = wg_ref[...].astype(BF16)
        wub[...] = wu_ref[...].astype(BF16)
        wdb[...] = wd_ref[...].astype(BF16)

    x = h_ref[...]
    hmid = (_silu(_dot(x, wgb[...])) * _dot(x, wub[...])).astype(BF16)
    o_ref[...] = _dot(hmid, wdb[...]).astype(o_ref.dtype)


def _ffn_shared(h2, wg, wu, wd, tm):
    n_pad, d = h2.shape
    f = wg.shape[1]
    return pl.pallas_call(
        _ffn_kernel,
        grid=(n_pad // tm,),
        in_specs=[pl.BlockSpec((tm, d), lambda i: (i, 0)),
                  pl.BlockSpec((d, f), lambda i: (0, 0)),
                  pl.BlockSpec((d, f), lambda i: (0, 0)),
                  pl.BlockSpec((f, d), lambda i: (0, 0))],
        out_specs=pl.BlockSpec((tm, d), lambda i: (i, 0)),
        out_shape=jax.ShapeDtypeStruct((n_pad, d), BF16),
        scratch_shapes=[pltpu.VMEM((d, f), BF16),
                        pltpu.VMEM((d, f), BF16),
                        pltpu.VMEM((f, d), BF16)],
        compiler_params=_cparams(("arbitrary",)),
        name="ffn_shared",
    )(h2, wg, wu, wd)


def _combine_kernel(dest_hbm, ys_hbm, wts_ref, sh_ref, x_ref, gt_ref, g_ref, o_ref,
                    dsm, buf, sem_d, sem_r, *, tn, tile0, final_norm):
    i = pl.program_id(0)
    cp = pltpu.make_async_copy(dest_hbm.at[pl.ds((tile0 + i) * (TOP_K * tn), TOP_K * tn)], dsm, sem_d)
    cp.start()
    cp.wait()

    def body(r, carry):
        for k in range(TOP_K):
            d = dsm[k * tn + r]
            pltpu.make_async_copy(ys_hbm.at[pl.ds(d, 1), :], buf.at[k, pl.ds(r, 1), :], sem_r).start()
        return carry

    lax.fori_loop(0, tn, body, 0)
    for k in range(TOP_K):
        pltpu.make_async_copy(ys_hbm.at[pl.ds(0, tn), :], buf.at[k], sem_r).wait()

    half = buf.shape[2]
    wts = wts_ref[...]
    acc_lo = jnp.zeros((tn, half), jnp.float32)
    acc_hi = jnp.zeros((tn, half), jnp.float32)
    for k in range(TOP_K):
        lo, hi = _unpack_words(buf[k])
        wk = wts[:, k:k + 1]
        acc_lo = acc_lo + wk * lo
        acc_hi = acc_hi + wk * hi
    sh = sh_ref[...].astype(jnp.float32)
    gt = gt_ref[0]
    x_lo = x_ref[:, :half] + gt[:, :half] * (acc_lo + sh[:, :half])
    x_hi = x_ref[:, half:] + gt[:, half:] * (acc_hi + sh[:, half:])
    if final_norm:
        ms = (jnp.sum(x_lo * x_lo, axis=-1, keepdims=True)
              + jnp.sum(x_hi * x_hi, axis=-1, keepdims=True)) / (2 * half)
        inv = lax.rsqrt(ms + EPS)
        g = g_ref[...]
        x_lo = x_lo * inv * g[:, :half]
        x_hi = x_hi * inv * g[:, half:]
    o_ref[:, :half] = x_lo
    o_ref[:, half:] = x_hi


def _combine(dest_flat, ys, wts_tok, shared, x1, gt3, g_final, tile0, tn, final_norm):
    rows, d = x1.shape
    nt = rows // tn
    gr = gt3.shape[1]
    per = nt // gt3.shape[0]
    return pl.pallas_call(
        functools.partial(_combine_kernel, tn=tn, tile0=tile0, final_norm=final_norm),
        grid=(nt,),
        in_specs=[pl.BlockSpec(memory_space=pl.ANY),
                  pl.BlockSpec(memory_space=pl.ANY),
                  pl.BlockSpec((tn, TOP_K), lambda i: (tile0 + i, 0)),
                  pl.BlockSpec((tn, d), lambda i: (tile0 + i, 0)),
                  pl.BlockSpec((tn, d), lambda i: (i, 0)),
                  pl.BlockSpec((1, gr, d), lambda i: (i // per, 0, 0)),
                  pl.BlockSpec((1, d), lambda i: (0, 0))],
        out_specs=pl.BlockSpec((tn, d), lambda i: (i, 0)),
        out_shape=jax.ShapeDtypeStruct((rows, d), jnp.float32),
        scratch_shapes=[pltpu.SMEM((TOP_K * tn,), jnp.int32),
                        pltpu.VMEM((TOP_K, tn, d // 2), jnp.uint32),
                        pltpu.SemaphoreType.DMA(()),
                        pltpu.SemaphoreType.DMA(())],
        compiler_params=_cparams(("arbitrary",)),
        name="combine",
    )(dest_flat, ys, wts_tok, shared, x1, gt3, g_final.reshape(1, d))


def _page_specs(shape, n, pg):
    def mk(u):
        return pl.BlockSpec((1,) + shape, lambda b, j, pt: (pt[b, j * pg + u], 0, 0))
    return [mk(u) for u in range(n)]


def _online_rows(s, m_ref, l_ref):
    m_old = m_ref[...]
    m_new = jnp.maximum(m_old, jnp.max(s, axis=-1, keepdims=True))
    p = jnp.exp(s - m_new)
    alpha = jnp.exp(m_old - m_new)
    l_ref[...] = alpha * l_ref[...] + jnp.sum(p, axis=-1, keepdims=True)
    m_ref[...] = m_new
    return p, alpha


def _da_sample_kernel(pt_ref, q_ref, kn_ref, vn_ref, bl_ref, cf_ref, b0_ref, *rest, pg, n_pages):
    k_refs, v_refs = rest[:pg], rest[pg:2 * pg]
    o_ref, m_ref, l_ref, acc_ref = rest[2 * pg:]
    j = pl.program_id(1)
    q = q_ref[0]

    @pl.when(j == 0)
    def _():
        s_new = jnp.sum(_rounded(q) * _rounded(kn_ref[0]), axis=-1, keepdims=True)
        m_ref[...] = s_new + b0_ref[...]
        l_ref[...] = jnp.ones(l_ref.shape, jnp.float32)
        acc_ref[...] = jnp.broadcast_to(_rounded(vn_ref[0]), acc_ref.shape)

    qb = _bf(q)
    for u in range(pg):
        s = _dot_nt(qb, _bf(k_refs[u][0]))
        is_last = (j * pg + u) == (n_pages - 1)
        s = s + jnp.where(is_last, bl_ref[...], cf_ref[...])
        p, alpha = _online_rows(s, m_ref, l_ref)
        acc_ref[...] = alpha * acc_ref[...] + _dot(_bf(p), _bf(v_refs[u][0]))

    @pl.when(j == pl.num_programs(1) - 1)
    def _():
        o_ref[0] = acc_ref[...] / l_ref[...]


def _da_sample(page_table, qbd, knew, vnew, bias_last, cfar, bias0, kc, vc, pg):
    nb, n_pages = page_table.shape
    r, w = qbd.shape[1:]
    full = lambda shp: pl.BlockSpec(shp, lambda b, j, pt: (0,) * len(shp))
    per_b = lambda shp: pl.BlockSpec((1,) + shp, lambda b, j, pt: (b, 0, 0))
    grid_spec = pltpu.PrefetchScalarGridSpec(
        num_scalar_prefetch=1,
        grid=(nb, n_pages // pg),
        in_specs=[per_b((r, w)), per_b((1, w)), per_b((1, w)),
                  full((r, PAGE_SIZE)), full((r, 1)), full((r, 1))]
                 + _page_specs((PAGE_SIZE, w), pg, pg) + _page_specs((PAGE_SIZE, w), pg, pg),
        out_specs=per_b((r, w)),
        scratch_shapes=[pltpu.VMEM((r, 1), jnp.float32),
                        pltpu.VMEM((r, 1), jnp.float32),
                        pltpu.VMEM((r, w), jnp.float32)])
    return pl.pallas_call(
        functools.partial(_da_sample_kernel, pg=pg, n_pages=n_pages),
        grid_spec=grid_spec,
        out_shape=jax.ShapeDtypeStruct((nb, r, w), jnp.float32),
        compiler_params=_cparams(("arbitrary", "arbitrary")),
        name="da_sample",
    )(page_table, qbd, knew, vnew, bias_last, cfar, bias0, *([kc] * pg), *([vc] * pg))


def _idx_sample_kernel(pt_ref, q_ref, w_ref, kn_ref, *rest, pg, n_pages, topk):
    k_refs = rest[:pg]
    sel_ref, sc_ref = rest[pg:]
    j = pl.program_id(1)
    q = q_ref[0]
    w = _rounded(w_ref[0])
    scale = IDX_DK ** -0.5 * IDX_HEADS ** -0.5
    rows = sc_ref.shape[0]
    lane = lax.broadcasted_iota(jnp.int32, (1, PAGE_SIZE), 1)

    @pl.when(j == 0)
    def _():
        sc_ref[...] = jnp.full(sc_ref.shape, -jnp.inf, jnp.float32)
        d = jnp.maximum(jnp.sum(_rounded(q) * _rounded(kn_ref[0]), axis=-1, keepdims=True), 0.0)
        s_new = jnp.sum(w * _rounded(d), axis=0, keepdims=True) * scale
        sc_ref[n_pages:n_pages + 1, :] = jnp.where(lane == 0, s_new, -jnp.inf)

    qb = _bf(q)
    for u in range(pg):
        d = _rounded(jnp.maximum(_dot_nt(qb, _bf(k_refs[u][0])), 0.0))
        sc_ref[pl.ds(j * pg + u, 1), :] = jnp.sum(w * d, axis=0, keepdims=True) * scale

    @pl.when(j == pl.num_programs(1) - 1)
    def _():
        key = _float_key(sc_ref[...])
        pos = (lax.broadcasted_iota(jnp.int32, key.shape, 0) * PAGE_SIZE
               + lax.broadcasted_iota(jnp.int32, key.shape, 1))

        def bit_body(it, thr):
            cand = thr + (jnp.int32(1) << (31 - it))
            cnt = jnp.sum(jnp.where(key >= cand, 1.0, 0.0))
            return jnp.where(cnt >= float(topk), cand, thr)

        thr = lax.fori_loop(0, 32, bit_body, jnp.int32(INT_MIN))
        thr = jnp.maximum(thr, jnp.int32(KEY_NEG_INF + 1))
        need = float(topk) - jnp.sum(jnp.where(key > thr, 1.0, 0.0))
        n_bits = max(1, int(rows * PAGE_SIZE).bit_length())

        def cut_body(it, cut):
            cand = cut + (jnp.int32(1) << (n_bits - 1 - it))
            n_eq = jnp.sum(jnp.where((key == thr) & (pos < cand), 1.0, 0.0))
            return jnp.where(n_eq <= need, cand, cut)

        cut = lax.fori_loop(0, n_bits, cut_body, jnp.int32(0))
        sel_ref[0] = jnp.where((key > thr) | ((key == thr) & (pos < cut)), 1.0, 0.0)


def _idx_sample(page_table, qix, wix, knew, kc, pg, topk):
    nb, n_pages = page_table.shape
    rows = -(-(n_pages + 1) // 8) * 8
    per_b = lambda shp: pl.BlockSpec((1,) + shp, lambda b, j, pt: (b, 0, 0))
    grid_spec = pltpu.PrefetchScalarGridSpec(
        num_scalar_prefetch=1,
        grid=(nb, n_pages // pg),
        in_specs=[per_b((IDX_HEADS, IDX_DK)), per_b((IDX_HEADS, 1)), per_b((1, IDX_DK))]
                 + _page_specs((PAGE_SIZE, IDX_DK), pg, pg),
        out_specs=per_b((rows, PAGE_SIZE)),
        scratch_shapes=[pltpu.VMEM((rows, PAGE_SIZE), jnp.float32)])
    return pl.pallas_call(
        functools.partial(_idx_sample_kernel, pg=pg, n_pages=n_pages, topk=topk),
        grid_spec=grid_spec,
        out_shape=jax.ShapeDtypeStruct((nb, rows, PAGE_SIZE), jnp.float32),
        compiler_params=_cparams(("arbitrary", "arbitrary")),
        name="idx_sample",
    )(page_table, qix, wix, knew, *([kc] * pg))


def _sa_sample_kernel(pt_ref, q_ref, kn_ref, vn_ref, sel_ref, bl_ref, cf_ref, b0_ref, *rest,
                      pg, n_pages):
    k_refs, v_refs = rest[:pg], rest[pg:2 * pg]
    o_ref, m_ref, l_ref, acc_ref = rest[2 * pg:]
    j = pl.program_id(1)
    q = q_ref[0]
    scale = SA_DH ** -0.5

    @pl.when(j == 0)
    def _():
        on = sel_ref[0, n_pages:n_pages + 1, 0:1] > 0.0
        s_new = jnp.sum(_rounded(q) * _rounded(kn_ref[0]), axis=-1, keepdims=True) * scale + b0_ref[...]
        m_ref[...] = jnp.where(on, s_new, NEG)
        l_ref[...] = jnp.where(on, jnp.ones(l_ref.shape, jnp.float32), 0.0)
        acc_ref[...] = jnp.where(on, jnp.broadcast_to(_rounded(vn_ref[0]), acc_ref.shape), 0.0)

    qb = _bf(q)
    for u in range(pg):
        page = j * pg + u
        s = _dot_nt(qb, _bf(k_refs[u][0])) * scale
        s = s + jnp.where(page == (n_pages - 1), bl_ref[...], cf_ref[...])
        s = jnp.where(sel_ref[0, pl.ds(page, 1), :] > 0.0, s, NEG)
        p, alpha = _online_rows(s, m_ref, l_ref)
        acc_ref[...] = alpha * acc_ref[...] + _dot(_bf(p), _bf(v_refs[u][0]))

    @pl.when(j == pl.num_programs(1) - 1)
    def _():
        o_ref[0] = acc_ref[...] / l_ref[...]


def _sa_sample(page_table, qbd, knew, vnew, sel, bias_last, cfar, bias0, kc, vc, pg):
    nb, n_pages = page_table.shape
    r, w = qbd.shape[1:]
    srows = sel.shape[1]
    full = lambda shp: pl.BlockSpec(shp, lambda b, j, pt: (0,) * len(shp))
    per_b = lambda shp: pl.BlockSpec((1,) + shp, lambda b, j, pt: (b, 0, 0))
    grid_spec = pltpu.PrefetchScalarGridSpec(
        num_scalar_prefetch=1,
        grid=(nb, n_pages // pg),
        in_specs=[per_b((r, w)), per_b((1, w)), per_b((1, w)), per_b((srows, PAGE_SIZE)),
                  full((r, PAGE_SIZE)), full((r, 1)), full((r, 1))]
                 + _page_specs((PAGE_SIZE, w), pg, pg) + _page_specs((PAGE_SIZE, w), pg, pg),
        out_specs=per_b((r, w)),
        scratch_shapes=[pltpu.VMEM((r, 1), jnp.float32),
                        pltpu.VMEM((r, 1), jnp.float32),
                        pltpu.VMEM((r, w), jnp.float32)])
    return pl.pallas_call(
        functools.partial(_sa_sample_kernel, pg=pg, n_pages=n_pages),
        grid_spec=grid_spec,
        out_shape=jax.ShapeDtypeStruct((nb, r, w), jnp.float32),
        compiler_params=_cparams(("arbitrary", "arbitrary")),
        name="sa_sample",
    )(page_table, qbd, knew, vnew, sel, bias_last, cfar, bias0, *([kc] * pg), *([vc] * pg))


ROW_TILE = 256
COMBINE_TILE = 128
EXPERT_BLOCK = 256
TAIL_ROWS = 256
PAGES_PER_STEP = 4


def _rms_rows(x, g):
    return x * lax.rsqrt(jnp.mean(x * x, axis=-1, keepdims=True) + EPS) * g


def _pad_rows(x, rows):
    return jnp.pad(x, ((0, rows - x.shape[0]), (0, 0)))


def _tile_major(dest, tn):
    k, n = dest.shape
    return dest.reshape(k, n // tn, tn).transpose(1, 0, 2).reshape(-1)


def kernel(x_prompt, x_sample, c_prompt, c_sample, cache_da_k, cache_da_v, cache_sa_k, cache_sa_v, cache_idx_k, page_table, rel_bias_table, w_ada, b_ada, g_attn, g_ffn, w_in, lambda_q1, lambda_k1, lambda_q2, lambda_k2, g_subln, w_proj_da, w_proj_sa, w_out, w_router, b_router, w_gate, w_up, w_down, w_sh_gate, w_sh_up, w_sh_down, g_final):
    f32, bf16 = jnp.float32, BF16
    nb, t, d = x_prompt.shape
    nbs, dec_seq, _ = x_sample.shape
    assert dec_seq == 1
    depth = w_in.shape[0]
    n_pages = page_table.shape[1]
    past_len = n_pages * PAGE_SIZE
    n = nb * t
    n_pad = n + TAIL_ROWS
    tq = min(256, t)
    assert tq >= MAX_DISTANCE and t % tq == 0 and n % ROW_TILE == 0 and nbs <= TAIL_ROWS
    assert TAIL_ROWS == ROW_TILE and EXPERT_BLOCK % 8 == 0
    topk_p = min(SA_TOPK_MAX, t // 4)
    topk_s = min(SA_TOPK_MAX, (past_len + dec_seq) // 4)
    assert topk_p <= tq

    sizes = [DA_HEADS * 2 * DA_DK, DA_HEADS * 2 * DA_DK, DA_HEADS * DA_DV, SA_HEADS * SA_DH,
             SA_KV_HEADS * SA_DH, SA_KV_HEADS * SA_DH, IDX_HEADS * IDX_DK, IDX_DK, IDX_HEADS, d, d]
    offs = [sum(sizes[:i]) for i in range(len(sizes) + 1)]
    (o_qda, o_kda, o_vda, o_qsa, o_ksa, o_vsa, o_qix, o_kix, o_wix, o_gda, o_gsa, _) = offs
    misc_w = 2 * LANES

    table = rel_bias_table.astype(f32)
    cfar = table[N_BUCKETS - 1]
    tiles_da = _near_tiles(table[:, :DA_HEADS], tq)
    tiles_sa = _near_tiles(table[:, DA_HEADS:], tq)
    last_dist = past_len - ((n_pages - 1) * PAGE_SIZE + jnp.arange(PAGE_SIZE, dtype=jnp.int32))
    bias_last = _bias_by_distance(table, last_dist)
    bias0 = _bias_by_distance(table, jnp.zeros((1,), jnp.int32))
    rep2 = lambda a: jnp.repeat(a, 2, axis=0)

    xp = x_prompt.reshape(n, d)
    xs = x_sample.reshape(nbs, d)
    c_all = jnp.concatenate([c_prompt, c_sample], axis=0)
    leaves_p, leaves_s = [], []
    for l in range(depth):
        lam_init = 0.8 - 0.6 * math.exp(-0.3 * l)
        lam = (jnp.exp(jnp.sum(lambda_q1[l].astype(f32) * lambda_k1[l].astype(f32)))
               - jnp.exp(jnp.sum(lambda_q2[l].astype(f32) * lambda_k2[l].astype(f32))) + lam_init)
        mod = _linear_small(c_all, w_ada[l], b_ada[l], silu_in=True)
        sh1, sc1, gt1, sh2, sc2, gt2 = jnp.split(mod[:nb], 6, axis=-1)
        sh1s, sc1s, gt1s, sh2s, sc2s, gt2s = jnp.split(mod[nb:], 6, axis=-1)
        w = w_in[l]

        h = _norm_mod(xp, g_attn[l], sc1, sh1, t, ROW_TILE)
        q_da, = _mm(h, w, o_qda, sizes[0], (bf16,), scale=DA_DK ** -0.5)
        k_da, k_da_b = _mm(h, w, o_kda, sizes[1], (f32, bf16))
        v_da, v_da_b = _mm(h, w, o_vda, sizes[2], (f32, bf16))
        q_sa, = _mm(h, w, o_qsa, sizes[3], (bf16,), scale=SA_DH ** -0.5)
        k_sa, k_sa_b = _mm(h, w, o_ksa, sizes[4], (f32, bf16))
        v_sa, v_sa_b = _mm(h, w, o_vsa, sizes[5], (f32, bf16))
        q_ix, = _mm(h, w, o_qix, sizes[6], (bf16,))
        misc, = _mm(h, w, o_kix, misc_w, (f32,))
        gates, = _mm(h, w[:, o_gda:], 0, 2 * d, (bf16,), sigmoid=True)
        k_ix = misc[:, :IDX_DK]

        o_da = _da_prompt(q_da, k_da_b, v_da_b, tiles_da, cfar[:DA_HEADS], lam.reshape(1),
                          g_subln[l].astype(f32), nb, t, tq, 1.0 - lam_init)
        o_sa = _sa_prompt(q_sa, q_ix, misc, k_sa_b, v_sa_b, tiles_sa, cfar[DA_HEADS:], nb, t, tq, topk_p)
        m = _proj_gate(o_da, o_sa, w_proj_da[l], w_proj_sa[l], gates, d)
        x1 = _mm_resid(m, w_out[l], xp, gt1, t)

        hs = _rms_rows(xs, g_attn[l].astype(f32)) * (1.0 + sc1s) + sh1s
        ps = _linear_small(hs, w)
        seg = lambda i: ps[:, offs[i]:offs[i + 1]]
        q_da_s, k_da_s, v_da_s, q_sa_s, k_sa_s, v_sa_s, q_ix_s, k_ix_s, w_ix_s, gda_s, gsa_s = [
            seg(i) for i in range(11)]

        r_da = 2 * DA_HEADS
        q16 = (q_da_s * DA_DK ** -0.5).reshape(nbs, r_da, DA_DK)
        qbd_da = (jnp.eye(r_da, dtype=f32)[None, :, :, None] * q16[:, None, :, :]).reshape(nbs, r_da, -1)
        kc_da = cache_da_k[l].reshape(-1, PAGE_SIZE, DA_HEADS * 2 * DA_DK)
        vc_da = cache_da_v[l].reshape(-1, PAGE_SIZE, DA_HEADS * DA_DV)
        a_da = _da_sample(page_table, qbd_da, k_da_s[:, None, :], v_da_s[:, None, :],
                          rep2(bias_last[:DA_HEADS]), rep2(cfar[:DA_HEADS, None]), rep2(bias0[:DA_HEADS]),
                          kc_da, vc_da, PAGES_PER_STEP)
        a5 = a_da.reshape(nbs, DA_HEADS, 2, DA_HEADS, DA_DV)
        hd = jnp.arange(DA_HEADS)
        a_diag = a5[:, hd, :, hd, :]
        o_da_s = (a_diag[:, :, 0] - lam * a_diag[:, :, 1]).transpose(1, 0, 2)
        o_da_s = _rms_rows(o_da_s, g_subln[l].astype(f32)) * (1.0 - lam_init)

        sel = _idx_sample(page_table, q_ix_s.reshape(nbs, IDX_HEADS, IDX_DK), w_ix_s[:, :, None],
                          k_ix_s[:, None, :], cache_idx_k[l], PAGES_PER_STEP, topk_s)
        grp_mask = (jnp.arange(SA_HEADS)[:, None] // SA_GROUP == jnp.arange(SA_KV_HEADS)[None, :])
        q8 = q_sa_s.reshape(nbs, SA_HEADS, SA_DH)
        qbd_sa = (grp_mask.astype(f32)[None, :, :, None] * q8[:, :, None, :]).reshape(nbs, SA_HEADS, -1)
        r_sa = 2 * SA_HEADS
        qbd_sa = jnp.pad(qbd_sa, ((0, 0), (0, r_sa - SA_HEADS), (0, 0)))
        kc_sa = cache_sa_k[l].reshape(-1, PAGE_SIZE, SA_KV_HEADS * SA_DH)
        vc_sa = cache_sa_v[l].reshape(-1, PAGE_SIZE, SA_KV_HEADS * SA_DH)
        a_sa = _sa_sample(page_table, qbd_sa, k_sa_s[:, None, :], v_sa_s[:, None, :], sel,
                          _pad_rows(bias_last[DA_HEADS:], r_sa), _pad_rows(cfar[DA_HEADS:, None], r_sa),
                          _pad_rows(bias0[DA_HEADS:], r_sa), kc_sa, vc_sa, PAGES_PER_STEP)
        a4 = a_sa[:, :SA_HEADS].reshape(nbs, SA_HEADS, SA_KV_HEADS, SA_DH)
        o_sa_s = jnp.sum(a4 * grp_mask.astype(f32)[None, :, :, None], axis=2)

        pda = _linear_small(o_da_s.reshape(nbs, -1), w_proj_da[l])
        psa = _linear_small(o_sa_s.reshape(nbs, -1), w_proj_sa[l])
        ms = _sigmoid(gda_s) * pda + _sigmoid(gsa_s) * psa
        x1s = xs + gt1s * _linear_small(ms, w_out[l])
        h2s = _rms_rows(x1s, g_ffn[l].astype(f32)) * (1.0 + sc2s) + sh2s
        lg_s = _linear_small(h2s, w_router[l])

        h2_all, lg_all = _norm_router(x1, g_ffn[l], sc2, sh2, w_router[l].T,
                                      _pad_rows(h2s.astype(bf16), TAIL_ROWS),
                                      _pad_rows(lg_s, TAIL_ROWS).T, t, ROW_TILE)
        eidx, wts, rank, cnt = _route(lg_all, b_router[l], ROW_TILE)
        cnt_tile = cnt[:, :, 0]
        total = jnp.sum(cnt_tile, axis=0)
        padded = jnp.ceil(total / EXPERT_BLOCK) * EXPERT_BLOCK
        pends = jnp.cumsum(padded)
        pstart = pends - padded
        base = pstart[None, :] + jnp.cumsum(cnt_tile, axis=0) - cnt_tile
        dest = _dest(eidx, rank, jnp.broadcast_to(base[:, :, None], base.shape + (LANES,)), ROW_TILE)
        n_blk = -(-(n_pad * TOP_K) // EXPERT_BLOCK) + N_EXPERTS
        blk_start = (jnp.arange(n_blk) * EXPERT_BLOCK).astype(f32)
        blk_e = jnp.minimum(jnp.searchsorted(pends, blk_start, side='right'), N_EXPERTS - 1).astype(jnp.int32)
        n_used = (pends[-1] / EXPERT_BLOCK).astype(jnp.int32).reshape(1)
        n_rows = n_blk * EXPERT_BLOCK
        pad_start = jnp.concatenate([pstart + total, pends[-1:]]).astype(jnp.int32)
        pad_cnt = jnp.concatenate([padded - total, (n_rows - pends[-1:]) / 8]).astype(jnp.int32)
        xs_sorted = _dispatch(h2_all, _tile_major(dest, ROW_TILE), pad_start, pad_cnt, n_rows, ROW_TILE)
        ys = _experts(xs_sorted, blk_e, n_used, w_gate[l], w_up[l], w_down[l], EXPERT_BLOCK)
        shared = _ffn_shared(h2_all, w_sh_gate[l], w_sh_up[l], w_sh_down[l], ROW_TILE)
        dest_c = _tile_major(dest, COMBINE_TILE)
        wts_tok = wts.T
        last = l == depth - 1
        xp = _combine(dest_c, ys, wts_tok, shared, x1, gt2[:, None, :], g_final.astype(f32),
                      0, COMBINE_TILE, last)
        tail = _combine(dest_c, ys, wts_tok, shared, _pad_rows(x1s, TAIL_ROWS),
                        _pad_rows(gt2s, TAIL_ROWS).reshape(-1, COMBINE_TILE, d), g_final.astype(f32),
                        n // COMBINE_TILE, COMBINE_TILE, last)
        xs = tail[:nbs]
        leaves_p.append((k_da, v_da, k_sa, v_sa, k_ix))
        leaves_s.append((k_da_s, v_da_s, k_sa_s, v_sa_s, k_ix_s))

    shapes = [(DA_HEADS, 2, DA_DK), (DA_HEADS, DA_DV), (SA_KV_HEADS, SA_DH), (SA_KV_HEADS, SA_DH), (IDX_DK,)]
    out_p = [jnp.stack([lv[i].reshape((nb, t) + shapes[i]) for lv in leaves_p]) for i in range(5)]
    out_s = [jnp.stack([lv[i].reshape((nbs, dec_seq) + shapes[i]) for lv in leaves_s]) for i in range(5)]
    return (xp.reshape(nb, t, d), xs.reshape(nbs, dec_seq, d), *out_p, *out_s)
```

```python
import functools
import math

import jax
import jax.numpy as jnp
from jax import lax
from jax.experimental import pallas as pl
from jax.experimental.pallas import tpu as pltpu

DA_HEADS = 8
DA_DK = 64
DA_DV = 2 * DA_DK
SA_HEADS = 8
SA_KV_HEADS = 2
SA_DH = 128
SA_GROUP = SA_HEADS // SA_KV_HEADS
IDX_HEADS = 16
IDX_DK = 64
SA_TOPK_MAX = 256
N_BUCKETS = 32
MAX_DISTANCE = 128
N_EXPERTS = 64
N_GROUPS = 8
GROUP_SIZE = N_EXPERTS // N_GROUPS
TOPK_GROUPS = 4
TOP_K = 8
ROUTED_SCALE = 2.5
PAGE_SIZE = 128
EPS = 1e-6

LANES = 128
VMEM_LIMIT = 56 * 1024 * 1024

BF16 = jnp.bfloat16
NEG = -1e30
INT_MIN = -(2 ** 31)
KEY_NEG_INF = (0xFF800000 ^ 0x7FFFFFFF) - (1 << 32)

_NT = (((1,), (1,)), ((), ()))


def _cparams(sem):
    return pltpu.CompilerParams(dimension_semantics=sem, vmem_limit_bytes=VMEM_LIMIT)


def _dot(a, b):
    return jnp.dot(a, b, preferred_element_type=jnp.float32)


def _dot_nt(a, b):
    return lax.dot_general(a, b, _NT, preferred_element_type=jnp.float32)


def _bf(x):
    return x.astype(BF16)


def _rounded(x):
    return x.astype(BF16).astype(jnp.float32)


def _sigmoid(x):
    return 1.0 / (1.0 + jnp.exp(-x))


def _silu(x):
    return x * _sigmoid(x)


def _float_key(s):
    b = pltpu.bitcast(s, jnp.int32)
    return b ^ ((b >> 31) & jnp.int32(0x7FFFFFFF))


def _linear_small_kernel(x_ref, w_ref, b_ref, o_ref, *, silu_in):
    x = x_ref[...]
    if silu_in:
        x = _silu(x)
    o_ref[...] = _dot(_bf(x), _bf(w_ref[...])) + b_ref[...]


def _linear_small(x, w, b=None, *, silu_in=False, tn=512):
    m0, k = x.shape
    m = -(-m0 // 16) * 16
    x = jnp.pad(x, ((0, m - m0), (0, 0)))
    n = w.shape[1]
    tn = min(tn, n)
    if b is None:
        b = jnp.zeros((1, n), jnp.float32)
    out = pl.pallas_call(
        functools.partial(_linear_small_kernel, silu_in=silu_in),
        grid=(pl.cdiv(n, tn),),
        in_specs=[pl.BlockSpec((m, k), lambda j: (0, 0)),
                  pl.BlockSpec((k, tn), lambda j: (0, j)),
                  pl.BlockSpec((1, tn), lambda j: (0, j))],
        out_specs=pl.BlockSpec((m, tn), lambda j: (0, j)),
        out_shape=jax.ShapeDtypeStruct((m, n), jnp.float32),
        compiler_params=_cparams(("arbitrary",)),
        name="linear_small",
    )(x, w, b.reshape(1, n))
    return out[:m0]


def _norm_mod_kernel(x_ref, g_ref, sc_ref, sh_ref, o_ref):
    x = x_ref[...]
    y = x * lax.rsqrt(jnp.mean(x * x, axis=-1, keepdims=True) + EPS) * g_ref[...]
    o_ref[...] = (y * (1.0 + sc_ref[0]) + sh_ref[0]).astype(o_ref.dtype)


def _norm_mod(x, g, sc, sh, rows_per_batch, tm):
    n, d = x.shape
    per = rows_per_batch // tm
    return pl.pallas_call(
        _norm_mod_kernel,
        grid=(n // tm,),
        in_specs=[pl.BlockSpec((tm, d), lambda i: (i, 0)),
                  pl.BlockSpec((1, d), lambda i: (0, 0)),
                  pl.BlockSpec((1, 1, d), lambda i: (i // per, 0, 0)),
                  pl.BlockSpec((1, 1, d), lambda i: (i // per, 0, 0))],
        out_specs=pl.BlockSpec((tm, d), lambda i: (i, 0)),
        out_shape=jax.ShapeDtypeStruct((n, d), BF16),
        compiler_params=_cparams(("arbitrary",)),
        name="norm_mod",
    )(x, g.reshape(1, d), sc[:, None, :], sh[:, None, :])


def _mm_kernel(x_ref, w_ref, *rest, scale, sigmoid, n_out):
    o_refs, wbf_ref = rest[:n_out], rest[n_out]

    @pl.when(pl.program_id(1) == 0)
    def _():
        wbf_ref[...] = w_ref[...].astype(BF16)

    acc = _dot(x_ref[...], wbf_ref[...])
    if scale != 1.0:
        acc = acc * scale
    if sigmoid:
        acc = _sigmoid(acc)
    for o in o_refs:
        o[...] = acc.astype(o.dtype)


def _mm(x, w, col0, ncols, out_dtypes, *, scale=1.0, sigmoid=False, tm=512, tn=512):
    m, k = x.shape
    tn = min(tn, ncols)
    tm = min(tm, m)
    assert col0 % tn == 0 and ncols % tn == 0 and m % tm == 0
    jb = col0 // tn
    outs = pl.pallas_call(
        functools.partial(_mm_kernel, scale=scale, sigmoid=sigmoid, n_out=len(out_dtypes)),
        grid=(ncols // tn, m // tm),
        in_specs=[pl.BlockSpec((tm, k), lambda j, i: (i, 0)),
                  pl.BlockSpec((k, tn), lambda j, i: (0, jb + j))],
        out_specs=[pl.BlockSpec((tm, tn), lambda j, i: (i, j)) for _ in out_dtypes],
        out_shape=[jax.ShapeDtypeStruct((m, ncols), dt) for dt in out_dtypes],
        scratch_shapes=[pltpu.VMEM((k, tn), BF16)],
        compiler_params=_cparams(("arbitrary", "arbitrary")),
        name="mm_cols",
    )(x, w)
    return outs


def _rel_bucket(dist):
    max_exact = N_BUCKETS // 2
    d = jnp.maximum(dist, 0)
    large = max_exact + (jnp.log(jnp.maximum(d, 1).astype(jnp.float32) / max_exact)
                         / math.log(MAX_DISTANCE / max_exact)
                         * (N_BUCKETS - max_exact)).astype(jnp.int32)
    large = jnp.minimum(large, N_BUCKETS - 1)
    return jnp.where(d < max_exact, d, large)


def _bias_by_distance(table, dists):
    return table[_rel_bucket(dists)].astype(jnp.float32).T


def _toeplitz_kernel(u_ref, o_ref):
    t = o_ref.shape[2]
    x = jnp.broadcast_to(u_ref[0], (t, 2 * t))
    o_ref[0, 0] = pltpu.roll(x, 0, 1, stride=1, stride_axis=0)[:, :t]


def _near_tiles(table, t):
    nh = table.shape[1]
    k = jnp.arange(2 * t, dtype=jnp.int32)
    gens = []
    for off in (0, t):
        d = jnp.where(k < t, off - k, off + 2 * t - k)
        gens.append(jnp.where(d[None] >= 0, _bias_by_distance(table, d), NEG))
    u = jnp.stack(gens, axis=1).reshape(nh * 2, 1, 2 * t)
    return pl.pallas_call(
        _toeplitz_kernel,
        grid=(nh, 2),
        in_specs=[pl.BlockSpec((1, 1, 2 * t), lambda h, o: (h * 2 + o, 0, 0))],
        out_specs=pl.BlockSpec((1, 1, t, t), lambda h, o: (h, o, 0, 0)),
        out_shape=jax.ShapeDtypeStruct((nh, 2, t, t), jnp.float32),
        compiler_params=_cparams(("arbitrary", "arbitrary")),
        name="bias_tiles",
    )(u)


def _flash_step(s, v, m_ref, l_ref, acc_ref, shift):
    m_old = m_ref[...]
    m_new = jnp.maximum(m_old, jnp.max(s, axis=-1, keepdims=True) + shift)
    p = jnp.exp(s - (m_new - shift))
    alpha = jnp.exp(m_old - m_new)
    l_ref[...] = alpha * l_ref[...] + jnp.sum(p, axis=-1, keepdims=True)
    acc_ref[...] = alpha * acc_ref[...] + _dot(p.astype(v.dtype), v)
    m_ref[...] = m_new


def _da_prompt_kernel(cfar_ref, lam_ref, q_ref, k_ref, v_ref, tile_ref, g_ref, o_ref,
                      m_ref, l_ref, acc_ref, *, tq, out_scale):
    h = pl.program_id(1)
    qi = pl.program_id(2)
    q = q_ref[...]
    lane = lax.broadcasted_iota(jnp.int32, q.shape, 1)
    zero = jnp.zeros_like(q)
    qm = (jnp.where(lane < DA_DK, q, zero), jnp.where(lane >= DA_DK, q, zero))
    cfar = cfar_ref[h]

    m_ref[...] = jnp.full(m_ref.shape, NEG, jnp.float32)
    l_ref[...] = jnp.zeros(l_ref.shape, jnp.float32)
    acc_ref[...] = jnp.zeros(acc_ref.shape, jnp.float32)

    def chunk(kc, bias, shift):
        rows = pl.ds(pl.multiple_of(kc * tq, tq), tq)
        ks = k_ref[rows, :]
        vs = v_ref[rows, :]
        for mp in range(2):
            s = _dot_nt(qm[mp], ks)
            if bias is not None:
                s = s + bias
            _flash_step(s, vs, m_ref.at[mp], l_ref.at[mp], acc_ref.at[mp], shift)

    def far_body(kc, carry):
        chunk(kc, None, cfar)
        return carry

    lax.fori_loop(0, jnp.maximum(qi - 1, 0), far_body, 0)

    @pl.when(qi >= 1)
    def _():
        chunk(qi - 1, tile_ref[0, 1], 0.0)

    chunk(qi, tile_ref[0, 0], 0.0)

    lam = lam_ref[0]
    o = acc_ref[0] / l_ref[0] - lam * (acc_ref[1] / l_ref[1])
    o = o * lax.rsqrt(jnp.mean(o * o, axis=-1, keepdims=True) + EPS) * g_ref[...]
    o_ref[...] = (o * out_scale).astype(o_ref.dtype)


def _da_prompt(q, k, v, tiles, cfar, lam, g_subln, nb, t, tq, out_scale):
    n = q.shape[0]
    nq = t // tq
    grid_spec = pltpu.PrefetchScalarGridSpec(
        num_scalar_prefetch=0,
        grid=(nb, DA_HEADS, nq),
        in_specs=[pl.BlockSpec(memory_space=pltpu.SMEM),
                  pl.BlockSpec(memory_space=pltpu.SMEM),
                  pl.BlockSpec((tq, LANES), lambda b, h, i: (b * nq + i, h)),
                  pl.BlockSpec((t, LANES), lambda b, h, i: (b, h)),
                  pl.BlockSpec((t, LANES), lambda b, h, i: (b, h)),
                  pl.BlockSpec((1, 2, tq, tq), lambda b, h, i: (h, 0, 0, 0)),
                  pl.BlockSpec((1, DA_DV), lambda b, h, i: (0, 0))],
        out_specs=pl.BlockSpec((tq, LANES), lambda b, h, i: (b * nq + i, h)),
        scratch_shapes=[pltpu.VMEM((2, tq, 1), jnp.float32),
                        pltpu.VMEM((2, tq, 1), jnp.float32),
                        pltpu.VMEM((2, tq, DA_DV), jnp.float32)])
    return pl.pallas_call(
        functools.partial(_da_prompt_kernel, tq=tq, out_scale=out_scale),
        grid_spec=grid_spec,
        out_shape=jax.ShapeDtypeStruct((n, DA_HEADS * DA_DV), BF16),
        compiler_params=_cparams(("arbitrary", "arbitrary", "arbitrary")),
        name="da_prompt",
    )(cfar, lam, q, k, v, tiles, g_subln.reshape(1, DA_DV))


def _count_ge(key_ref, cand, n_chunks, tq):
    def body(kc, acc):
        ge = jnp.where(key_ref[kc] >= cand, 1.0, 0.0)
        for c in range(tq // LANES):
            acc = acc + ge[:, c * LANES:(c + 1) * LANES]
        return acc
    acc = lax.fori_loop(0, n_chunks, body, jnp.zeros((tq, LANES), jnp.float32))
    return jnp.sum(acc, axis=-1, keepdims=True)


def _sa_prompt_kernel(cfar_ref, qs_ref, qx_ref, mq_ref, mk_ref, ks_ref, vs_ref, tile_ref, o_ref,
                      k2_ref, key_ref, cut_ref, m_ref, l_ref, acc_ref, *, tq, topk):
    qi = pl.program_id(1)
    n_chunks = qi + 1
    t = mk_ref.shape[0]

    @pl.when(qi == 0)
    def _():
        kix = mk_ref[:, :LANES]
        lane = lax.broadcasted_iota(jnp.int32, kix.shape, 1)
        k2_ref[0] = jnp.where(lane < IDX_DK, kix, 0.0).astype(BF16)
        k2_ref[1] = jnp.where(lane >= IDX_DK, pltpu.roll(kix, IDX_DK, axis=1), 0.0).astype(BF16)

    wix = mq_ref[:, IDX_DK:IDX_DK + IDX_HEADS]
    wcols = [wix[:, hh:hh + 1] for hh in range(IDX_HEADS)]
    row = lax.broadcasted_iota(jnp.int32, (tq, tq), 0)
    col = lax.broadcasted_iota(jnp.int32, (tq, tq), 1)

    def score_body(kc, carry):
        rows = pl.ds(pl.multiple_of(kc * tq, tq), tq)
        ke = k2_ref[0, rows, :]
        ko = k2_ref[1, rows, :]
        sc = jnp.zeros((tq, tq), jnp.float32)
        for p in range(IDX_HEADS // 2):
            qp = qx_ref[:, p * LANES:(p + 1) * LANES]
            sc = sc + wcols[2 * p] * jnp.maximum(_dot_nt(qp, ke), 0.0)
            sc = sc + wcols[2 * p + 1] * jnp.maximum(_dot_nt(qp, ko), 0.0)
        sc = sc * (IDX_DK ** -0.5 * IDX_HEADS ** -0.5)
        sc = jnp.where((kc < qi) | (row >= col), sc, -jnp.inf)
        key_ref[kc] = _float_key(sc)
        return carry

    lax.fori_loop(0, n_chunks, score_body, 0)

    def bit_body(it, carry):
        thr, cnt_thr = carry
        cand = thr + (jnp.int32(1) << (31 - it))
        cnt = _count_ge(key_ref, cand, n_chunks, tq)
        ok = cnt >= float(topk)
        return jnp.where(ok, cand, thr), jnp.where(ok, cnt, cnt_thr)

    thr0 = jnp.full((tq, 1), INT_MIN, jnp.int32)
    cnt0 = jnp.full((tq, 1), 3.0e38, jnp.float32)
    thr, cnt_thr = lax.fori_loop(0, 32, bit_body, (thr0, cnt0))
    tied = (cnt_thr > float(topk)) & (thr > jnp.int32(KEY_NEG_INF))
    need_tie = jnp.max(jnp.where(tied, 1.0, 0.0)) > 0.0
    thr = jnp.maximum(thr, jnp.int32(KEY_NEG_INF + 1))

    cut_ref[...] = jnp.full((tq, 1), 2 ** 30, jnp.int32)

    @pl.when(need_tie)
    def _():
        def gt_body(kc, acc):
            g = jnp.where(key_ref[kc] > thr, 1.0, 0.0)
            return acc + jnp.sum(g, axis=-1, keepdims=True)
        n_gt = lax.fori_loop(0, n_chunks, gt_body, jnp.zeros((tq, 1), jnp.float32))
        need = float(topk) - n_gt
        n_bits = max(1, int(t).bit_length())

        def cut_body(it, cut):
            cand = cut + (jnp.int32(1) << (n_bits - 1 - it))

            def eq_body(kc, acc):
                pos = kc * tq + col
                e = jnp.where((key_ref[kc] == thr) & (pos < cand), 1.0, 0.0)
                return acc + jnp.sum(e, axis=-1, keepdims=True)
            n_eq = lax.fori_loop(0, n_chunks, eq_body, jnp.zeros((tq, 1), jnp.float32))
            return jnp.where(n_eq <= need, cand, cut)
        cut_ref[...] = lax.fori_loop(0, n_bits, cut_body, jnp.zeros((tq, 1), jnp.int32))

    cut = cut_ref[...]

    m_ref[...] = jnp.full(m_ref.shape, NEG, jnp.float32)
    l_ref[...] = jnp.zeros(l_ref.shape, jnp.float32)
    acc_ref[...] = jnp.zeros(acc_ref.shape, jnp.float32)
    qg = [jnp.concatenate([qs_ref[:, (g * SA_GROUP + j) * LANES:(g * SA_GROUP + j + 1) * LANES]
                           for j in range(SA_GROUP)], axis=0) for g in range(SA_KV_HEADS)]

    def attend(kc, kind):
        rows = pl.ds(pl.multiple_of(kc * tq, tq), tq)
        key = key_ref[kc]
        pos = kc * tq + col
        sel = (key > thr) | ((key == thr) & (pos < cut))
        for g in range(SA_KV_HEADS):
            ks = ks_ref[rows, g * SA_DH:(g + 1) * SA_DH]
            vs = vs_ref[rows, g * SA_DH:(g + 1) * SA_DH]
            s_all = _dot_nt(qg[g], ks)
            ps = []
            for j in range(SA_GROUP):
                hh = g * SA_GROUP + j
                s = s_all[j * tq:(j + 1) * tq]
                if kind == 'far':
                    s = s + cfar_ref[hh]
                else:
                    s = s + tile_ref[hh, kind]
                s = jnp.where(sel, s, NEG)
                m_old = m_ref[hh]
                m_new = jnp.maximum(m_old, jnp.max(s, axis=-1, keepdims=True))
                p = jnp.exp(s - m_new)
                alpha = jnp.exp(m_old - m_new)
                l_ref[hh] = alpha * l_ref[hh] + jnp.sum(p, axis=-1, keepdims=True)
                acc_ref[hh] = alpha * acc_ref[hh]
                m_ref[hh] = m_new
                ps.append(p.astype(BF16))
            pv = _dot(jnp.concatenate(ps, axis=0), vs)
            for j in range(SA_GROUP):
                hh = g * SA_GROUP + j
                acc_ref[hh] = acc_ref[hh] + pv[j * tq:(j + 1) * tq]

    def far_body(kc, carry):
        attend(kc, 'far')
        return carry

    lax.fori_loop(0, jnp.maximum(qi - 1, 0), far_body, 0)

    @pl.when(qi >= 1)
    def _():
        attend(qi - 1, 1)

    attend(qi, 0)

    for hh in range(SA_HEADS):
        o_ref[:, hh * SA_DH:(hh + 1) * SA_DH] = (acc_ref[hh] / l_ref[hh]).astype(o_ref.dtype)


def _sa_prompt(q_sa, q_ix, misc, k_sa, v_sa, tiles, cfar, nb, t, tq, topk):
    n = q_sa.shape[0]
    nq = t // tq
    mw = misc.shape[1]
    kvw = SA_KV_HEADS * SA_DH
    return pl.pallas_call(
        functools.partial(_sa_prompt_kernel, tq=tq, topk=topk),
        grid=(nb, nq),
        in_specs=[pl.BlockSpec(memory_space=pltpu.SMEM),
                  pl.BlockSpec((tq, SA_HEADS * SA_DH), lambda b, i: (b * nq + i, 0)),
                  pl.BlockSpec((tq, IDX_HEADS * IDX_DK), lambda b, i: (b * nq + i, 0)),
                  pl.BlockSpec((tq, mw), lambda b, i: (b * nq + i, 0)),
                  pl.BlockSpec((t, mw), lambda b, i: (b, 0)),
                  pl.BlockSpec((t, kvw), lambda b, i: (b, 0)),
                  pl.BlockSpec((t, kvw), lambda b, i: (b, 0)),
                  pl.BlockSpec((SA_HEADS, 2, tq, tq), lambda b, i: (0, 0, 0, 0))],
        out_specs=pl.BlockSpec((tq, SA_HEADS * SA_DH), lambda b, i: (b * nq + i, 0)),
        out_shape=jax.ShapeDtypeStruct((n, SA_HEADS * SA_DH), BF16),
        scratch_shapes=[pltpu.VMEM((2, t, LANES), BF16),
                        pltpu.VMEM((nq, tq, tq), jnp.int32),
                        pltpu.VMEM((tq, 1), jnp.int32),
                        pltpu.VMEM((SA_HEADS, tq, 1), jnp.float32),
                        pltpu.VMEM((SA_HEADS, tq, 1), jnp.float32),
                        pltpu.VMEM((SA_HEADS, tq, SA_DH), jnp.float32)],
        compiler_params=_cparams(("arbitrary", "arbitrary")),
        name="sa_prompt",
    )(cfar, q_sa, q_ix, misc, misc, k_sa, v_sa, tiles)


def _proj_gate_kernel(oda_ref, osa_ref, wpd_ref, wps_ref, gda_ref, gsa_ref, o_ref, wbf_ref):
    @pl.when(pl.program_id(1) == 0)
    def _():
        wbf_ref[0] = wpd_ref[...].astype(BF16)
        wbf_ref[1] = wps_ref[...].astype(BF16)

    a = _dot(oda_ref[...], wbf_ref[0])
    b = _dot(osa_ref[...], wbf_ref[1])
    o_ref[...] = (gda_ref[...].astype(jnp.float32) * a
                  + gsa_ref[...].astype(jnp.float32) * b).astype(o_ref.dtype)


def _proj_gate(o_da, o_sa, w_pd, w_ps, gates, d, tm=512, tn=512):
    n, kd = o_da.shape
    ks = o_sa.shape[1]
    tn = min(tn, d)
    tm = min(tm, n)
    nj = d // tn
    return pl.pallas_call(
        _proj_gate_kernel,
        grid=(nj, n // tm),
        in_specs=[pl.BlockSpec((tm, kd), lambda j, i: (i, 0)),
                  pl.BlockSpec((tm, ks), lambda j, i: (i, 0)),
                  pl.BlockSpec((kd, tn), lambda j, i: (0, j)),
                  pl.BlockSpec((ks, tn), lambda j, i: (0, j)),
                  pl.BlockSpec((tm, tn), lambda j, i: (i, j)),
                  pl.BlockSpec((tm, tn), lambda j, i: (i, nj + j))],
        out_specs=pl.BlockSpec((tm, tn), lambda j, i: (i, j)),
        out_shape=jax.ShapeDtypeStruct((n, d), BF16),
        scratch_shapes=[pltpu.VMEM((2, kd, tn), BF16)],
        compiler_params=_cparams(("arbitrary", "arbitrary")),
        name="proj_gate",
    )(o_da, o_sa, w_pd, w_ps, gates, gates)


def _mm_resid_kernel(m_ref, w_ref, x_ref, gt_ref, o_ref, wbf_ref):
    @pl.when(pl.program_id(1) == 0)
    def _():
        wbf_ref[...] = w_ref[...].astype(BF16)

    o_ref[...] = x_ref[...] + gt_ref[0] * _dot(m_ref[...], wbf_ref[...])


def _mm_resid(m, w, x, gt, rows_per_batch, tm=512, tn=512):
    n, k = m.shape
    d = w.shape[1]
    tn = min(tn, d)
    tm = min(tm, n)
    per = rows_per_batch // tm
    return pl.pallas_call(
        _mm_resid_kernel,
        grid=(d // tn, n // tm),
        in_specs=[pl.BlockSpec((tm, k), lambda j, i: (i, 0)),
                  pl.BlockSpec((k, tn), lambda j, i: (0, j)),
                  pl.BlockSpec((tm, tn), lambda j, i: (i, j)),
                  pl.BlockSpec((1, 1, tn), lambda j, i: (i // per, 0, j))],
        out_specs=pl.BlockSpec((tm, tn), lambda j, i: (i, j)),
        out_shape=jax.ShapeDtypeStruct((n, d), jnp.float32),
        scratch_shapes=[pltpu.VMEM((k, tn), BF16)],
        compiler_params=_cparams(("arbitrary", "arbitrary")),
        name="mm_resid",
    )(m, w, x, gt[:, None, :])


def _norm_router_kernel(x_ref, g_ref, sc_ref, sh_ref, wr_ref, th_ref, tl_ref, h_ref, lg_ref):
    last = pl.num_programs(0) - 1

    @pl.when(pl.program_id(0) < last)
    def _():
        x = x_ref[...]
        y = x * lax.rsqrt(jnp.mean(x * x, axis=-1, keepdims=True) + EPS) * g_ref[...]
        h = y * (1.0 + sc_ref[0]) + sh_ref[0]
        h_ref[...] = h.astype(h_ref.dtype)
        lg_ref[...] = _dot_nt(_bf(wr_ref[...]), _bf(h))

    @pl.when(pl.program_id(0) == last)
    def _():
        h_ref[...] = th_ref[...]
        lg_ref[...] = tl_ref[...]


def _norm_router(x, g, sc, sh, w_router_t, tail_h, tail_lg, rows_per_batch, tm):
    n, d = x.shape
    per = rows_per_batch // tm
    nt = n // tm
    assert tail_h.shape == (tm, d) and tail_lg.shape == (N_EXPERTS, tm)
    row = lambda i: jnp.minimum(i, nt - 1)
    return pl.pallas_call(
        _norm_router_kernel,
        grid=(nt + 1,),
        in_specs=[pl.BlockSpec((tm, d), lambda i: (row(i), 0)),
                  pl.BlockSpec((1, d), lambda i: (0, 0)),
                  pl.BlockSpec((1, 1, d), lambda i: (row(i) // per, 0, 0)),
                  pl.BlockSpec((1, 1, d), lambda i: (row(i) // per, 0, 0)),
                  pl.BlockSpec((N_EXPERTS, d), lambda i: (0, 0)),
                  pl.BlockSpec((tm, d), lambda i: (0, 0)),
                  pl.BlockSpec((N_EXPERTS, tm), lambda i: (0, 0))],
        out_specs=[pl.BlockSpec((tm, d), lambda i: (i, 0)),
                   pl.BlockSpec((N_EXPERTS, tm), lambda i: (0, i))],
        out_shape=[jax.ShapeDtypeStruct((n + tm, d), BF16),
                   jax.ShapeDtypeStruct((N_EXPERTS, n + tm), jnp.float32)],
        compiler_params=_cparams(("arbitrary",)),
        name="norm_router",
    )(x, g.reshape(1, d), sc[:, None, :], sh[:, None, :], w_router_t, tail_h, tail_lg)


def _route_kernel(lg_ref, b_ref, eidx_ref, wts_ref, rank_ref, cnt_ref, *, tn):
    shape = (N_GROUPS, GROUP_SIZE, tn)
    sc = _sigmoid(lg_ref[...])
    biased = sc + b_ref[...]
    e_iota = lax.broadcasted_iota(jnp.int32, shape, 1)
    g_iota3 = lax.broadcasted_iota(jnp.int32, shape, 0)
    flat_iota = g_iota3 * GROUP_SIZE + e_iota
    g_iota = lax.broadcasted_iota(jnp.int32, (N_GROUPS, 1, tn), 0)
    ninf = -jnp.inf

    m1 = jnp.max(biased, axis=1, keepdims=True)
    first = jnp.min(jnp.where(biased == m1, e_iota, GROUP_SIZE), axis=1, keepdims=True)
    m2 = jnp.max(jnp.where(e_iota == first, ninf, biased), axis=1, keepdims=True)
    cur = m1 + m2
    gsel = jnp.zeros((N_GROUPS, 1, tn), jnp.float32)
    for _ in range(TOPK_GROUPS):
        mx = jnp.max(cur, axis=0, keepdims=True)
        idx = jnp.min(jnp.where(cur == mx, g_iota, N_GROUPS), axis=0, keepdims=True)
        hit = g_iota == idx
        gsel = jnp.where(hit, 1.0, gsel)
        cur = jnp.where(hit, ninf, cur)

    cur = jnp.where(gsel > 0.0, biased, ninf)
    hits, ws = [], []
    for k in range(TOP_K):
        mx = jnp.max(jnp.max(cur, axis=1, keepdims=True), axis=0, keepdims=True)
        cand = jnp.where(cur == mx, flat_iota, N_EXPERTS)
        idx = jnp.min(jnp.min(cand, axis=1, keepdims=True), axis=0, keepdims=True)
        hit = flat_iota == idx
        w = jnp.sum(jnp.sum(jnp.where(hit, sc, 0.0), axis=1, keepdims=True), axis=0, keepdims=True)
        eidx_ref[k:k + 1, :] = idx.reshape(1, tn)
        hits.append(hit)
        ws.append(w)
        cur = jnp.where(hit, ninf, cur)
    wsum = ws[0]
    for w in ws[1:]:
        wsum = wsum + w
    for k in range(TOP_K):
        wts_ref[k:k + 1, :] = (ws[k] / wsum * ROUTED_SCALE).reshape(1, tn)

    member = jnp.zeros(shape, jnp.float32)
    for hit in hits:
        member = jnp.where(hit, 1.0, member)
    member2 = member.reshape(N_EXPERTS, tn)
    r = lax.broadcasted_iota(jnp.int32, (tn, tn), 0)
    c = lax.broadcasted_iota(jnp.int32, (tn, tn), 1)
    upper = jnp.where(r < c, 1.0, 0.0).astype(BF16)
    prefix = _dot(member2.astype(BF16), upper).reshape(shape)
    for k in range(TOP_K):
        rk = jnp.sum(jnp.sum(jnp.where(hits[k], prefix, 0.0), axis=1, keepdims=True), axis=0, keepdims=True)
        rank_ref[k:k + 1, :] = rk.reshape(1, tn)
    cnt = jnp.sum(member2, axis=1, keepdims=True)
    cnt_ref[0] = jnp.broadcast_to(cnt, (N_EXPERTS, LANES))


def _route(logits_t, b_router, tn):
    n_pad = logits_t.shape[1]
    nt = n_pad // tn
    lg3 = logits_t.reshape(N_GROUPS, GROUP_SIZE, n_pad)
    b3 = b_router.astype(jnp.float32).reshape(N_GROUPS, GROUP_SIZE, 1)
    row = lambda dt: jax.ShapeDtypeStruct((TOP_K, n_pad), dt)
    return pl.pallas_call(
        functools.partial(_route_kernel, tn=tn),
        grid=(nt,),
        in_specs=[pl.BlockSpec((N_GROUPS, GROUP_SIZE, tn), lambda i: (0, 0, i)),
                  pl.BlockSpec((N_GROUPS, GROUP_SIZE, 1), lambda i: (0, 0, 0))],
        out_specs=[pl.BlockSpec((TOP_K, tn), lambda i: (0, i)),
                   pl.BlockSpec((TOP_K, tn), lambda i: (0, i)),
                   pl.BlockSpec((TOP_K, tn), lambda i: (0, i)),
                   pl.BlockSpec((1, N_EXPERTS, LANES), lambda i: (i, 0, 0))],
        out_shape=[row(jnp.int32), row(jnp.float32), row(jnp.float32),
                   jax.ShapeDtypeStruct((nt, N_EXPERTS, LANES), jnp.float32)],
        compiler_params=_cparams(("arbitrary",)),
        name="route",
    )(lg3, b3)


def _dest_kernel(eidx_ref, rank_ref, base_ref, o_ref, *, tn):
    e_iota = lax.broadcasted_iota(jnp.int32, (N_EXPERTS, tn), 0)
    base = base_ref[0][:, :1]
    for k in range(TOP_K):
        onehot = e_iota == eidx_ref[k:k + 1, :]
        b = jnp.sum(jnp.where(onehot, base, 0.0), axis=0, keepdims=True)
        o_ref[k:k + 1, :] = (b + rank_ref[k:k + 1, :]).astype(jnp.int32)


def _dest(eidx, rank, base, tn):
    n_pad = eidx.shape[1]
    return pl.pallas_call(
        functools.partial(_dest_kernel, tn=tn),
        grid=(n_pad // tn,),
        in_specs=[pl.BlockSpec((TOP_K, tn), lambda i: (0, i)),
                  pl.BlockSpec((TOP_K, tn), lambda i: (0, i)),
                  pl.BlockSpec((1, N_EXPERTS, LANES), lambda i: (i, 0, 0))],
        out_specs=pl.BlockSpec((TOP_K, tn), lambda i: (0, i)),
        out_shape=jax.ShapeDtypeStruct((TOP_K, n_pad), jnp.int32),
        compiler_params=_cparams(("arbitrary",)),
        name="dest",
    )(eidx, rank, base)


def _pack_words(lo_f32, hi_f32):
    lo = lax.shift_right_logical(pltpu.bitcast(lo_f32, jnp.uint32), jnp.uint32(16))
    hi = pltpu.bitcast(hi_f32, jnp.uint32) & jnp.uint32(0xFFFF0000)
    return hi | lo


def _unpack_words(w):
    lo = pltpu.bitcast(lax.shift_left(w, jnp.uint32(16)), jnp.float32)
    hi = pltpu.bitcast(w & jnp.uint32(0xFFFF0000), jnp.float32)
    return lo, hi


def _bf16_exact(x):
    return x.astype(BF16).astype(jnp.float32)


def _dispatch_kernel(pstart_ref, pcnt_ref, h_ref, dest_hbm, xs_hbm, dsm, pk, zrow, sem_d, sem_r,
                     *, tn, nt):
    i = pl.program_id(0)
    half = pk.shape[1]

    @pl.when(i < nt)
    def _():
        cp = pltpu.make_async_copy(dest_hbm.at[pl.ds(i * (TOP_K * tn), TOP_K * tn)], dsm, sem_d)
        cp.start()
        x = h_ref[...]
        pk[...] = _pack_words(x[:, :half].astype(jnp.float32), x[:, half:].astype(jnp.float32))
        cp.wait()

        def body(r, carry):
            for k in range(TOP_K):
                d = dsm[k * tn + r]
                pltpu.make_async_copy(pk.at[pl.ds(r, 1), :], xs_hbm.at[pl.ds(d, 1), :], sem_r).start()
            return carry

        lax.fori_loop(0, tn, body, 0)
        for k in range(TOP_K):
            pltpu.make_async_copy(pk, xs_hbm.at[pl.ds(0, tn), :], sem_r).wait()

    @pl.when(i == nt)
    def _():
        zrow[...] = jnp.zeros(zrow.shape, zrow.dtype)

        def per_expert(e, carry):
            s0 = pstart_ref[e]
            c = pcnt_ref[e]

            def start(r, cc):
                pltpu.make_async_copy(zrow.at[pl.ds(0, 1), :], xs_hbm.at[pl.ds(s0 + r, 1), :], sem_r).start()
                return cc

            def wait(r, cc):
                pltpu.make_async_copy(zrow.at[pl.ds(0, 1), :], xs_hbm.at[pl.ds(s0, 1), :], sem_r).wait()
                return cc

            lax.fori_loop(0, c, start, 0)
            lax.fori_loop(0, c, wait, 0)
            return carry

        lax.fori_loop(0, N_EXPERTS, per_expert, 0)

        t0 = pstart_ref[N_EXPERTS]
        groups = pcnt_ref[N_EXPERTS]
        rows8 = lambda r: pl.ds(pl.multiple_of(t0 + r * 8, 8), 8)

        def tstart(r, cc):
            pltpu.make_async_copy(zrow, xs_hbm.at[rows8(r), :], sem_r).start()
            return cc

        def twait(r, cc):
            pltpu.make_async_copy(zrow, xs_hbm.at[rows8(0), :], sem_r).wait()
            return cc

        lax.fori_loop(0, groups, tstart, 0)
        lax.fori_loop(0, groups, twait, 0)


def _dispatch(h2, dest_flat, pad_start, pad_cnt, n_rows, tn):
    n_pad, d = h2.shape
    nt = n_pad // tn
    grid_spec = pltpu.PrefetchScalarGridSpec(
        num_scalar_prefetch=2,
        grid=(nt + 1,),
        in_specs=[pl.BlockSpec((tn, d), lambda i, a, b: (jnp.minimum(i, nt - 1), 0)),
                  pl.BlockSpec(memory_space=pl.ANY)],
        out_specs=pl.BlockSpec(memory_space=pl.ANY),
        scratch_shapes=[pltpu.SMEM((TOP_K * tn,), jnp.int32),
                        pltpu.VMEM((tn, d // 2), jnp.uint32),
                        pltpu.VMEM((8, d // 2), jnp.uint32),
                        pltpu.SemaphoreType.DMA(()),
                        pltpu.SemaphoreType.DMA(())])
    return pl.pallas_call(
        functools.partial(_dispatch_kernel, tn=tn, nt=nt),
        grid_spec=grid_spec,
        out_shape=jax.ShapeDtypeStruct((n_rows, d // 2), jnp.uint32),
        compiler_params=_cparams(("arbitrary",)),
        name="dispatch",
    )(pad_start, pad_cnt, h2, dest_flat)


def _expert_kernel(blk_e_ref, nused_ref, x_ref, wg_ref, wu_ref, wd_ref, y_ref, wgb, wub, wdb):
    i = pl.program_id(0)
    nused = nused_ref[0]
    ii = jnp.minimum(i, nused - 1)
    e = blk_e_ref[ii]
    e_prev = blk_e_ref[jnp.maximum(ii - 1, 0)]
    half = x_ref.shape[1]

    @pl.when((i < nused) & ((i == 0) | (e != e_prev)))
    def _():
        wgb[...] = wg_ref[0].astype(BF16)
        wub[...] = wu_ref[0].astype(BF16)
        wdb[...] = wd_ref[0].astype(BF16)

    @pl.when(i < nused)
    def _():
        lo, hi = _unpack_words(x_ref[...])
        xl = lo.astype(BF16)
        xh = hi.astype(BF16)
        g = _dot(xl, wgb[:half, :]) + _dot(xh, wgb[half:, :])
        u = _dot(xl, wub[:half, :]) + _dot(xh, wub[half:, :])
        hmid = (_silu(g) * u).astype(BF16)
        y = _dot(hmid, wdb[...])
        y_ref[...] = _pack_words(_bf16_exact(y[:, :half]), _bf16_exact(y[:, half:]))

    @pl.when(i >= nused)
    def _():
        y_ref[...] = jnp.zeros(y_ref.shape, y_ref.dtype)


def _experts(xs, blk_e, nused, w_gate, w_up, w_down, tb):
    n_rows, half = xs.shape
    _, d, f = w_gate.shape
    nblk = n_rows // tb

    def xmap(i, be, nu):
        return (jnp.minimum(i, nu[0] - 1), 0)

    def wmap(i, be, nu):
        return (be[jnp.minimum(i, nu[0] - 1)], 0, 0)

    grid_spec = pltpu.PrefetchScalarGridSpec(
        num_scalar_prefetch=2,
        grid=(nblk,),
        in_specs=[pl.BlockSpec((tb, half), xmap),
                  pl.BlockSpec((1, d, f), wmap),
                  pl.BlockSpec((1, d, f), wmap),
                  pl.BlockSpec((1, f, d), wmap)],
        out_specs=pl.BlockSpec((tb, half), lambda i, be, nu: (i, 0)),
        scratch_shapes=[pltpu.VMEM((d, f), BF16),
                        pltpu.VMEM((d, f), BF16),
                        pltpu.VMEM((f, d), BF16)])
    return pl.pallas_call(
        _expert_kernel,
        grid_spec=grid_spec,
        out_shape=jax.ShapeDtypeStruct((n_rows, half), jnp.uint32),
        compiler_params=_cparams(("arbitrary",)),
        name="experts",
    )(blk_e, nused, xs, w_gate, w_up, w_down)


def _ffn_kernel(h_ref, wg_ref, wu_ref, wd_ref, o_ref, wgb, wub, wdb):
    @pl.when(pl.program_id(0) == 0)
    def _():
        wgb[...] = wg_ref[...].astype(BF16)
        wub[...] = wu_ref[...].astype(BF16)
        wdb[...] = wd_ref[...].astype(BF16)

    x = h_ref[...]
    hmid = (_silu(_dot(x, wgb[...])) * _dot(x, wub[...])).astype(BF16)
    o_ref[...] = _dot(hmid, wdb[...]).astype(o_ref.dtype)


def _ffn_shared(h2, wg, wu, wd, tm):
    n_pad, d = h2.shape
    f = wg.shape[1]
    return pl.pallas_call(
        _ffn_kernel,
        grid=(n_pad // tm,),
        in_specs=[pl.BlockSpec((tm, d), lambda i: (i, 0)),
                  pl.BlockSpec((d, f), lambda i: (0, 0)),
                  pl.BlockSpec((d, f), lambda i: (0, 0)),
                  pl.BlockSpec((f, d), lambda i: (0, 0))],
        out_specs=pl.BlockSpec((tm, d), lambda i: (i, 0)),
        out_shape=jax.ShapeDtypeStruct((n_pad, d), BF16),
        scratch_shapes=[pltpu.VMEM((d, f), BF16),
                        pltpu.VMEM((d, f), BF16),
                        pltpu.VMEM((f, d), BF16)],
        compiler_params=_cparams(("arbitrary",)),
        name="ffn_shared",
    )(h2, wg, wu, wd)


def _combine_kernel(dest_hbm, ys_hbm, wts_ref, sh_ref, x_ref, gt_ref, g_ref, o_ref,
                    dsm, buf, sem_d, sem_r, *, tn, tile0, final_norm):
    i = pl.program_id(0)
    cp = pltpu.make_async_copy(dest_hbm.at[pl.ds((tile0 + i) * (TOP_K * tn), TOP_K * tn)], dsm, sem_d)
    cp.start()
    cp.wait()

    def body(r, carry):
        for k in range(TOP_K):
            d = dsm[k * tn + r]
            pltpu.make_async_copy(ys_hbm.at[pl.ds(d, 1), :], buf.at[k, pl.ds(r, 1), :], sem_r).start()
        return carry

    lax.fori_loop(0, tn, body, 0)
    for k in range(TOP_K):
        pltpu.make_async_copy(ys_hbm.at[pl.ds(0, tn), :], buf.at[k], sem_r).wait()

    half = buf.shape[2]
    wts = wts_ref[...]
    acc_lo = jnp.zeros((tn, half), jnp.float32)
    acc_hi = jnp.zeros((tn, half), jnp.float32)
    for k in range(TOP_K):
        lo, hi = _unpack_words(buf[k])
        wk = wts[:, k:k + 1]
        acc_lo = acc_lo + wk * lo
        acc_hi = acc_hi + wk * hi
    sh = sh_ref[...].astype(jnp.float32)
    gt = gt_ref[0]
    x_lo = x_ref[:, :half] + gt[:, :half] * (acc_lo + sh[:, :half])
    x_hi = x_ref[:, half:] + gt[:, half:] * (acc_hi + sh[:, half:])
    if final_norm:
        ms = (jnp.sum(x_lo * x_lo, axis=-1, keepdims=True)
              + jnp.sum(x_hi * x_hi, axis=-1, keepdims=True)) / (2 * half)
        inv = lax.rsqrt(ms + EPS)
        g = g_ref[...]
        x_lo = x_lo * inv * g[:, :half]
        x_hi = x_hi * inv * g[:, half:]
    o_ref[:, :half] = x_lo
    o_ref[:, half:] = x_hi


def _combine(dest_flat, ys, wts_tok, shared, x1, gt3, g_final, tile0, tn, final_norm):
    rows, d = x1.shape
    nt = rows // tn
    gr = gt3.shape[1]
    per = nt // gt3.shape[0]
    return pl.pallas_call(
        functools.partial(_combine_kernel, tn=tn, tile0=tile0, final_norm=final_norm),
        grid=(nt,),
        in_specs=[pl.BlockSpec(memory_space=pl.ANY),
                  pl.BlockSpec(memory_space=pl.ANY),
                  pl.BlockSpec((tn, TOP_K), lambda i: (tile0 + i, 0)),
                  pl.BlockSpec((tn, d), lambda i: (tile0 + i, 0)),
                  pl.BlockSpec((tn, d), lambda i: (i, 0)),
                  pl.BlockSpec((1, gr, d), lambda i: (i // per, 0, 0)),
                  pl.BlockSpec((1, d), lambda i: (0, 0))],
        out_specs=pl.BlockSpec((tn, d), lambda i: (i, 0)),
        out_shape=jax.ShapeDtypeStruct((rows, d), jnp.float32),
        scratch_shapes=[pltpu.SMEM((TOP_K * tn,), jnp.int32),
                        pltpu.VMEM((TOP_K, tn, d // 2), jnp.uint32),
                        pltpu.SemaphoreType.DMA(()),
                        pltpu.SemaphoreType.DMA(())],
        compiler_params=_cparams(("arbitrary",)),
        name="combine",
    )(dest_flat, ys, wts_tok, shared, x1, gt3, g_final.reshape(1, d))


def _page_specs(shape, n, pg):
    def mk(u):
        return pl.BlockSpec((1,) + shape, lambda b, j, pt: (pt[b, j * pg + u], 0, 0))
    return [mk(u) for u in range(n)]


def _online_rows(s, m_ref, l_ref):
    m_old = m_ref[...]
    m_new = jnp.maximum(m_old, jnp.max(s, axis=-1, keepdims=True))
    p = jnp.exp(s - m_new)
    alpha = jnp.exp(m_old - m_new)
    l_ref[...] = alpha * l_ref[...] + jnp.sum(p, axis=-1, keepdims=True)
    m_ref[...] = m_new
    return p, alpha


def _da_sample_kernel(pt_ref, q_ref, kn_ref, vn_ref, bl_ref, cf_ref, b0_ref, ex_ref, hm_ref, *rest,
                      pg, n_pages):
    k_refs, v_refs = rest[:pg], rest[pg:2 * pg]
    o_ref, m_ref, l_ref, acc_ref = rest[2 * pg:]
    j = pl.program_id(1)
    q = q_ref[0]

    @pl.when(j == 0)
    def _():
        s_new = jnp.sum(_rounded(q) * _rounded(kn_ref[0]), axis=-1, keepdims=True)
        m_ref[...] = s_new + b0_ref[...]
        l_ref[...] = jnp.ones(l_ref.shape, jnp.float32)
        acc_ref[...] = _rounded(vn_ref[0])

    qb = _bf(q)
    for u in range(pg):
        s = _dot(qb, _bf(k_refs[u][0]))
        is_last = (j * pg + u) == (n_pages - 1)
        s = s + jnp.where(is_last, bl_ref[...], cf_ref[...])
        p, alpha = _online_rows(s, m_ref, l_ref)
        pe = _bf(_dot(_bf(p), ex_ref[...]) * hm_ref[...])
        acc_ref[...] = alpha * acc_ref[...] + _dot(pe, _bf(v_refs[u][0]))

    @pl.when(j == pl.num_programs(1) - 1)
    def _():
        o_ref[0] = acc_ref[...] / l_ref[...]


def _da_sample(page_table, qbd, knew, vnew, bias_last, cfar, bias0, kt, v2, pg):
    nb, n_pages = page_table.shape
    r, w = qbd.shape[1:]
    rows_v, dv = v2.shape[1:]
    nh = rows_v // PAGE_SIZE
    col = jnp.arange(rows_v, dtype=jnp.int32)
    expand = (col[None, :] // nh == jnp.arange(PAGE_SIZE, dtype=jnp.int32)[:, None]).astype(BF16)
    head_mask = (col[None, :] % nh == (jnp.arange(r, dtype=jnp.int32)[:, None] // 2)).astype(jnp.float32)
    full = lambda shp: pl.BlockSpec(shp, lambda b, j, pt: (0,) * len(shp))
    per_b = lambda shp: pl.BlockSpec((1,) + shp, lambda b, j, pt: (b, 0, 0))
    grid_spec = pltpu.PrefetchScalarGridSpec(
        num_scalar_prefetch=1,
        grid=(nb, n_pages // pg),
        in_specs=[per_b((r, w)), per_b((1, w)), per_b((r, dv)),
                  full((r, PAGE_SIZE)), full((r, 1)), full((r, 1)),
                  full((PAGE_SIZE, rows_v)), full((r, rows_v))]
                 + _page_specs((w, PAGE_SIZE), pg, pg) + _page_specs((rows_v, dv), pg, pg),
        out_specs=per_b((r, dv)),
        scratch_shapes=[pltpu.VMEM((r, 1), jnp.float32),
                        pltpu.VMEM((r, 1), jnp.float32),
                        pltpu.VMEM((r, dv), jnp.float32)])
    return pl.pallas_call(
        functools.partial(_da_sample_kernel, pg=pg, n_pages=n_pages),
        grid_spec=grid_spec,
        out_shape=jax.ShapeDtypeStruct((nb, r, dv), jnp.float32),
        compiler_params=_cparams(("arbitrary", "arbitrary")),
        name="da_sample",
    )(page_table, qbd, knew, vnew, bias_last, cfar, bias0, expand, head_mask,
      *([kt] * pg), *([v2] * pg))


def _idx_sample_kernel(pt_ref, q_ref, w_ref, kn_ref, *rest, pg, n_pages, topk):
    k_refs = rest[:pg]
    sel_ref, sc_ref = rest[pg:]
    j = pl.program_id(1)
    q = q_ref[0]
    w = _rounded(w_ref[0])
    scale = IDX_DK ** -0.5 * IDX_HEADS ** -0.5
    rows = sc_ref.shape[0]
    lane = lax.broadcasted_iota(jnp.int32, (1, PAGE_SIZE), 1)

    @pl.when(j == 0)
    def _():
        sc_ref[...] = jnp.full(sc_ref.shape, -jnp.inf, jnp.float32)
        d = jnp.maximum(jnp.sum(_rounded(q) * _rounded(kn_ref[0]), axis=-1, keepdims=True), 0.0)
        s_new = jnp.sum(w * _rounded(d), axis=0, keepdims=True) * scale
        sc_ref[n_pages:n_pages + 1, :] = jnp.where(lane == 0, s_new, -jnp.inf)

    qb = _bf(q)
    for u in range(pg):
        d = _rounded(jnp.maximum(_dot(qb, _bf(k_refs[u][0])), 0.0))
        sc_ref[pl.ds(j * pg + u, 1), :] = jnp.sum(w * d, axis=0, keepdims=True) * scale

    @pl.when(j == pl.num_programs(1) - 1)
    def _():
        key = _float_key(sc_ref[...])
        pos = (lax.broadcasted_iota(jnp.int32, key.shape, 0) * PAGE_SIZE
               + lax.broadcasted_iota(jnp.int32, key.shape, 1))

        def bit_body(it, thr):
            cand = thr + (jnp.int32(1) << (31 - it))
            cnt = jnp.sum(jnp.where(key >= cand, 1.0, 0.0))
            return jnp.where(cnt >= float(topk), cand, thr)

        thr = lax.fori_loop(0, 32, bit_body, jnp.int32(INT_MIN))
        thr = jnp.maximum(thr, jnp.int32(KEY_NEG_INF + 1))
        need = float(topk) - jnp.sum(jnp.where(key > thr, 1.0, 0.0))
        n_bits = max(1, int(rows * PAGE_SIZE).bit_length())

        def cut_body(it, cut):
            cand = cut + (jnp.int32(1) << (n_bits - 1 - it))
            n_eq = jnp.sum(jnp.where((key == thr) & (pos < cand), 1.0, 0.0))
            return jnp.where(n_eq <= need, cand, cut)

        cut = lax.fori_loop(0, n_bits, cut_body, jnp.int32(0))
        sel_ref[0] = jnp.where((key > thr) | ((key == thr) & (pos < cut)), 1.0, 0.0)


def _idx_sample(page_table, qix, wix, knew, kc, pg, topk):
    nb, n_pages = page_table.shape
    rows = -(-(n_pages + 1) // 8) * 8
    per_b = lambda shp: pl.BlockSpec((1,) + shp, lambda b, j, pt: (b, 0, 0))
    grid_spec = pltpu.PrefetchScalarGridSpec(
        num_scalar_prefetch=1,
        grid=(nb, n_pages // pg),
        in_specs=[per_b((IDX_HEADS, IDX_DK)), per_b((IDX_HEADS, 1)), per_b((1, IDX_DK))]
                 + _page_specs((IDX_DK, PAGE_SIZE), pg, pg),
        out_specs=per_b((rows, PAGE_SIZE)),
        scratch_shapes=[pltpu.VMEM((rows, PAGE_SIZE), jnp.float32)])
    return pl.pallas_call(
        functools.partial(_idx_sample_kernel, pg=pg, n_pages=n_pages, topk=topk),
        grid_spec=grid_spec,
        out_shape=jax.ShapeDtypeStruct((nb, rows, PAGE_SIZE), jnp.float32),
        compiler_params=_cparams(("arbitrary", "arbitrary")),
        name="idx_sample",
    )(page_table, qix, wix, knew, *([kc] * pg))


def _sa_sample_kernel(pt_ref, q_ref, kn_ref, vn_ref, sel_ref, bl_ref, cf_ref, b0_ref, gm_ref, *rest,
                      pg, n_pages):
    k_refs, v_refs = rest[:pg], rest[pg:2 * pg]
    o_ref, m_ref, l_ref, acc_ref = rest[2 * pg:]
    j = pl.program_id(1)
    q = q_ref[0]
    scale = SA_DH ** -0.5
    srow = sel_ref.shape[2]

    @pl.when(j == 0)
    def _():
        on = sel_ref[0, n_pages:n_pages + 1, 0:1] > 0.0
        s_new = jnp.sum(_rounded(q) * _rounded(kn_ref[0]), axis=-1, keepdims=True) * scale + b0_ref[...]
        m_ref[...] = jnp.where(on, s_new, NEG)
        l_ref[...] = jnp.where(on, jnp.ones(l_ref.shape, jnp.float32), 0.0)
        acc_ref[...] = jnp.where(on, _rounded(vn_ref[0]), 0.0)

    qb = _bf(q)
    for u in range(pg):
        page = j * pg + u
        s = _dot_nt(qb, _bf(k_refs[u][0])) * scale
        s = s + jnp.where(page == (n_pages - 1), bl_ref[...], cf_ref[...])
        keep = (sel_ref[0, pl.ds(page, 1), :] > 0.0) & (gm_ref[...] > 0.0)
        s = jnp.where(keep, s, NEG)
        p, alpha = _online_rows(s, m_ref, l_ref)
        acc_ref[...] = alpha * acc_ref[...] + _dot(_bf(p), _bf(v_refs[u][0]))

    @pl.when(j == pl.num_programs(1) - 1)
    def _():
        o_ref[0] = acc_ref[...] / l_ref[...]


def _sa_sample(page_table, q, knew, vnew, sel2, bias_last2, cfar, bias0, group_mask, k2, v2, pg):
    nb, n_pages = page_table.shape
    r, dh = q.shape[1:]
    srows, cols = sel2.shape[1:]
    full = lambda shp: pl.BlockSpec(shp, lambda b, j, pt: (0,) * len(shp))
    per_b = lambda shp: pl.BlockSpec((1,) + shp, lambda b, j, pt: (b, 0, 0))
    grid_spec = pltpu.PrefetchScalarGridSpec(
        num_scalar_prefetch=1,
        grid=(nb, n_pages // pg),
        in_specs=[per_b((r, dh)), per_b((r, dh)), per_b((r, dh)), per_b((srows, cols)),
                  full((r, cols)), full((r, 1)), full((r, 1)), full((r, cols))]
                 + _page_specs((cols, dh), pg, pg) + _page_specs((cols, dh), pg, pg),
        out_specs=per_b((r, dh)),
        scratch_shapes=[pltpu.VMEM((r, 1), jnp.float32),
                        pltpu.VMEM((r, 1), jnp.float32),
                        pltpu.VMEM((r, dh), jnp.float32)])
    return pl.pallas_call(
        functools.partial(_sa_sample_kernel, pg=pg, n_pages=n_pages),
        grid_spec=grid_spec,
        out_shape=jax.ShapeDtypeStruct((nb, r, dh), jnp.float32),
        compiler_params=_cparams(("arbitrary", "arbitrary")),
        name="sa_sample",
    )(page_table, q, knew, vnew, sel2, bias_last2, cfar, bias0, group_mask, *([k2] * pg), *([v2] * pg))


ROW_TILE = 256
COMBINE_TILE = 128
EXPERT_BLOCK = 256
TAIL_ROWS = 256
PAGES_PER_STEP = 4


def _rms_rows(x, g):
    return x * lax.rsqrt(jnp.mean(x * x, axis=-1, keepdims=True) + EPS) * g


def _pad_rows(x, rows):
    return jnp.pad(x, ((0, rows - x.shape[0]), (0, 0)))


def _tile_major(dest, tn):
    k, n = dest.shape
    return dest.reshape(k, n // tn, tn).transpose(1, 0, 2).reshape(-1)


def kernel(x_prompt, x_sample, c_prompt, c_sample, cache_da_k, cache_da_v, cache_sa_k, cache_sa_v, cache_idx_k, page_table, rel_bias_table, w_ada, b_ada, g_attn, g_ffn, w_in, lambda_q1, lambda_k1, lambda_q2, lambda_k2, g_subln, w_proj_da, w_proj_sa, w_out, w_router, b_router, w_gate, w_up, w_down, w_sh_gate, w_sh_up, w_sh_down, g_final):
    f32, bf16 = jnp.float32, BF16
    nb, t, d = x_prompt.shape
    nbs, dec_seq, _ = x_sample.shape
    assert dec_seq == 1
    depth = w_in.shape[0]
    n_pages = page_table.shape[1]
    past_len = n_pages * PAGE_SIZE
    n = nb * t
    n_pad = n + TAIL_ROWS
    tq = min(256, t)
    assert tq >= MAX_DISTANCE and t % tq == 0 and n % ROW_TILE == 0 and nbs <= TAIL_ROWS
    assert TAIL_ROWS == ROW_TILE and EXPERT_BLOCK % 8 == 0
    topk_p = min(SA_TOPK_MAX, t // 4)
    topk_s = min(SA_TOPK_MAX, (past_len + dec_seq) // 4)
    assert topk_p <= tq

    sizes = [DA_HEADS * 2 * DA_DK, DA_HEADS * 2 * DA_DK, DA_HEADS * DA_DV, SA_HEADS * SA_DH,
             SA_KV_HEADS * SA_DH, SA_KV_HEADS * SA_DH, IDX_HEADS * IDX_DK, IDX_DK, IDX_HEADS, d, d]
    offs = [sum(sizes[:i]) for i in range(len(sizes) + 1)]
    (o_qda, o_kda, o_vda, o_qsa, o_ksa, o_vsa, o_qix, o_kix, o_wix, o_gda, o_gsa, _) = offs
    misc_w = 2 * LANES

    table = rel_bias_table.astype(f32)
    cfar = table[N_BUCKETS - 1]
    tiles_da = _near_tiles(table[:, :DA_HEADS], tq)
    tiles_sa = _near_tiles(table[:, DA_HEADS:], tq)
    last_dist = past_len - ((n_pages - 1) * PAGE_SIZE + jnp.arange(PAGE_SIZE, dtype=jnp.int32))
    bias_last = _bias_by_distance(table, last_dist)
    bias0 = _bias_by_distance(table, jnp.zeros((1,), jnp.int32))
    rep2 = lambda a: jnp.repeat(a, 2, axis=0)

    xp = x_prompt.reshape(n, d)
    xs = x_sample.reshape(nbs, d)
    c_all = jnp.concatenate([c_prompt, c_sample], axis=0)
    leaves_p, leaves_s = [], []
    for l in range(depth):
        lam_init = 0.8 - 0.6 * math.exp(-0.3 * l)
        lam = (jnp.exp(jnp.sum(lambda_q1[l].astype(f32) * lambda_k1[l].astype(f32)))
               - jnp.exp(jnp.sum(lambda_q2[l].astype(f32) * lambda_k2[l].astype(f32))) + lam_init)
        mod = _linear_small(c_all, w_ada[l], b_ada[l], silu_in=True)
        sh1, sc1, gt1, sh2, sc2, gt2 = jnp.split(mod[:nb], 6, axis=-1)
        sh1s, sc1s, gt1s, sh2s, sc2s, gt2s = jnp.split(mod[nb:], 6, axis=-1)
        w = w_in[l]

        h = _norm_mod(xp, g_attn[l], sc1, sh1, t, ROW_TILE)
        q_da, = _mm(h, w, o_qda, sizes[0], (bf16,), scale=DA_DK ** -0.5)
        k_da, k_da_b = _mm(h, w, o_kda, sizes[1], (f32, bf16))
        v_da, v_da_b = _mm(h, w, o_vda, sizes[2], (f32, bf16))
        q_sa, = _mm(h, w, o_qsa, sizes[3], (bf16,), scale=SA_DH ** -0.5)
        k_sa, k_sa_b = _mm(h, w, o_ksa, sizes[4], (f32, bf16))
        v_sa, v_sa_b = _mm(h, w, o_vsa, sizes[5], (f32, bf16))
        q_ix, = _mm(h, w, o_qix, sizes[6], (bf16,))
        misc, = _mm(h, w, o_kix, misc_w, (f32,))
        gates, = _mm(h, w[:, o_gda:], 0, 2 * d, (bf16,), sigmoid=True)
        k_ix = misc[:, :IDX_DK]

        o_da = _da_prompt(q_da, k_da_b, v_da_b, tiles_da, cfar[:DA_HEADS], lam.reshape(1),
                          g_subln[l].astype(f32), nb, t, tq, 1.0 - lam_init)
        o_sa = _sa_prompt(q_sa, q_ix, misc, k_sa_b, v_sa_b, tiles_sa, cfar[DA_HEADS:], nb, t, tq, topk_p)
        m = _proj_gate(o_da, o_sa, w_proj_da[l], w_proj_sa[l], gates, d)
        x1 = _mm_resid(m, w_out[l], xp, gt1, t)

        hs = _rms_rows(xs, g_attn[l].astype(f32)) * (1.0 + sc1s) + sh1s
        ps = _linear_small(hs, w)
        seg = lambda i: ps[:, offs[i]:offs[i + 1]]
        q_da_s, k_da_s, v_da_s, q_sa_s, k_sa_s, v_sa_s, q_ix_s, k_ix_s, w_ix_s, gda_s, gsa_s = [
            seg(i) for i in range(11)]

        r_da = 2 * DA_HEADS
        q16 = (q_da_s * DA_DK ** -0.5).reshape(nbs, r_da, DA_DK)
        qbd_da = (jnp.eye(r_da, dtype=f32)[None, :, :, None] * q16[:, None, :, :]).reshape(nbs, r_da, -1)
        kt_da = cache_da_k[l].transpose(0, 2, 3, 4, 1).reshape(-1, DA_HEADS * 2 * DA_DK, PAGE_SIZE)
        v2_da = cache_da_v[l].reshape(-1, PAGE_SIZE * DA_HEADS, DA_DV)
        a_da = _da_sample(page_table, qbd_da, k_da_s[:, None, :],
                          rep2(v_da_s.reshape(nbs * DA_HEADS, DA_DV)).reshape(nbs, r_da, DA_DV),
                          rep2(bias_last[:DA_HEADS]), rep2(cfar[:DA_HEADS, None]), rep2(bias0[:DA_HEADS]),
                          kt_da, v2_da, PAGES_PER_STEP)
        a4 = a_da.reshape(nbs, DA_HEADS, 2, DA_DV)
        o_da_s = a4[:, :, 0] - lam * a4[:, :, 1]
        o_da_s = _rms_rows(o_da_s, g_subln[l].astype(f32)) * (1.0 - lam_init)

        sel = _idx_sample(page_table, q_ix_s.reshape(nbs, IDX_HEADS, IDX_DK), w_ix_s[:, :, None],
                          k_ix_s[:, None, :], cache_idx_k[l].transpose(0, 2, 1), PAGES_PER_STEP, topk_s)
        r_sa = 2 * SA_HEADS
        kv_of_row = jnp.minimum(jnp.arange(r_sa) // SA_GROUP, SA_KV_HEADS - 1)
        pad_sa = lambda a: jnp.pad(a, ((0, 0), (0, r_sa - SA_HEADS), (0, 0)))
        rep_kv = lambda a: jnp.repeat(a, SA_KV_HEADS, axis=-1)
        col_kv = jnp.arange(PAGE_SIZE * SA_KV_HEADS) % SA_KV_HEADS
        k2_sa = cache_sa_k[l].reshape(-1, PAGE_SIZE * SA_KV_HEADS, SA_DH)
        v2_sa = cache_sa_v[l].reshape(-1, PAGE_SIZE * SA_KV_HEADS, SA_DH)
        a_sa = _sa_sample(page_table, pad_sa(q_sa_s.reshape(nbs, SA_HEADS, SA_DH)),
                          k_sa_s.reshape(nbs, SA_KV_HEADS, SA_DH)[:, kv_of_row],
                          v_sa_s.reshape(nbs, SA_KV_HEADS, SA_DH)[:, kv_of_row],
                          rep_kv(sel), rep_kv(_pad_rows(bias_last[DA_HEADS:], r_sa)),
                          _pad_rows(cfar[DA_HEADS:, None], r_sa), _pad_rows(bias0[DA_HEADS:], r_sa),
                          (col_kv[None, :] == kv_of_row[:, None]).astype(f32),
                          k2_sa, v2_sa, PAGES_PER_STEP)
        o_sa_s = a_sa[:, :SA_HEADS]

        pda = _linear_small(o_da_s.reshape(nbs, -1), w_proj_da[l])
        psa = _linear_small(o_sa_s.reshape(nbs, -1), w_proj_sa[l])
        ms = _sigmoid(gda_s) * pda + _sigmoid(gsa_s) * psa
        x1s = xs + gt1s * _linear_small(ms, w_out[l])
        h2s = _rms_rows(x1s, g_ffn[l].astype(f32)) * (1.0 + sc2s) + sh2s
        lg_s = _linear_small(h2s, w_router[l])

        h2_all, lg_all = _norm_router(x1, g_ffn[l], sc2, sh2, w_router[l].T,
                                      _pad_rows(h2s.astype(bf16), TAIL_ROWS),
                                      _pad_rows(lg_s, TAIL_ROWS).T, t, ROW_TILE)
        eidx, wts, rank, cnt = _route(lg_all, b_router[l], ROW_TILE)
        cnt_tile = cnt[:, :, 0]
        total = jnp.sum(cnt_tile, axis=0)
        padded = jnp.ceil(total / EXPERT_BLOCK) * EXPERT_BLOCK
        pends = jnp.cumsum(padded)
        pstart = pends - padded
        base = pstart[None, :] + jnp.cumsum(cnt_tile, axis=0) - cnt_tile
        dest = _dest(eidx, rank, jnp.broadcast_to(base[:, :, None], base.shape + (LANES,)), ROW_TILE)
        n_blk = -(-(n_pad * TOP_K) // EXPERT_BLOCK) + N_EXPERTS
        blk_start = (jnp.arange(n_blk) * EXPERT_BLOCK).astype(f32)
        blk_e = jnp.minimum(jnp.sum(pends[None, :] <= blk_start[:, None], axis=1), N_EXPERTS - 1).astype(jnp.int32)
        n_used = (pends[-1] / EXPERT_BLOCK).astype(jnp.int32).reshape(1)
        n_rows = n_blk * EXPERT_BLOCK
        pad_start = jnp.concatenate([pstart + total, pends[-1:]]).astype(jnp.int32)
        pad_cnt = jnp.concatenate([padded - total, (n_rows - pends[-1:]) / 8]).astype(jnp.int32)
        xs_sorted = _dispatch(h2_all, _tile_major(dest, ROW_TILE), pad_start, pad_cnt, n_rows, ROW_TILE)
        ys = _experts(xs_sorted, blk_e, n_used, w_gate[l], w_up[l], w_down[l], EXPERT_BLOCK)
        shared = _ffn_shared(h2_all, w_sh_gate[l], w_sh_up[l], w_sh_down[l], ROW_TILE)
        dest_c = _tile_major(dest, COMBINE_TILE)
        wts_tok = wts.T
        last = l == depth - 1
        xp = _combine(dest_c, ys, wts_tok, shared, x1, gt2[:, None, :], g_final.astype(f32),
                      0, COMBINE_TILE, last)
        tail = _combine(dest_c, ys, wts_tok, shared, _pad_rows(x1s, TAIL_ROWS),
                        _pad_rows(gt2s, TAIL_ROWS).reshape(-1, COMBINE_TILE, d), g_final.astype(f32),
                        n // COMBINE_TILE, COMBINE_TILE, last)
        xs = tail[:nbs]
        leaves_p.append((k_da, v_da, k_sa, v_sa, k_ix))
        leaves_s.append((k_da_s, v_da_s, k_sa_s, v_sa_s, k_ix_s))

    shapes = [(DA_HEADS, 2, DA_DK), (DA_HEADS, DA_DV), (SA_KV_HEADS, SA_DH), (SA_KV_HEADS, SA_DH), (IDX_DK,)]
    out_p = [jnp.stack([lv[i].reshape((nb, t) + shapes[i]) for lv in leaves_p]) for i in range(5)]
    out_s = [jnp.stack([lv[i].reshape((nbs, dec_seq) + shapes[i]) for lv in leaves_s]) for i in range(5)]
    return (xp.reshape(nb, t, d), xs.reshape(nbs, dec_seq, d), *out_p, *out_s)
```

```python
import functools
import math

import jax
import jax.numpy as jnp
from jax import lax
from jax.experimental import pallas as pl
from jax.experimental.pallas import tpu as pltpu

DA_HEADS = 8
DA_DK = 64
DA_DV = 2 * DA_DK
SA_HEADS = 8
SA_KV_HEADS = 2
SA_DH = 128
SA_GROUP = SA_HEADS // SA_KV_HEADS
IDX_HEADS = 16
IDX_DK = 64
SA_TOPK_MAX = 256
N_BUCKETS = 32
MAX_DISTANCE = 128
N_EXPERTS = 64
N_GROUPS = 8
GROUP_SIZE = N_EXPERTS // N_GROUPS
TOPK_GROUPS = 4
TOP_K = 8
ROUTED_SCALE = 2.5
PAGE_SIZE = 128
EPS = 1e-6

LANES = 128
VMEM_LIMIT = 56 * 1024 * 1024

BF16 = jnp.bfloat16
NEG = -1e30
INT_MIN = -(2 ** 31)
KEY_NEG_INF = (0xFF800000 ^ 0x7FFFFFFF) - (1 << 32)

_NT = (((1,), (1,)), ((), ()))


def _cparams(sem):
    return pltpu.CompilerParams(dimension_semantics=sem, vmem_limit_bytes=VMEM_LIMIT)


def _dot(a, b):
    return jnp.dot(a, b, preferred_element_type=jnp.float32)


def _dot_nt(a, b):
    return lax.dot_general(a, b, _NT, preferred_element_type=jnp.float32)


def _bf(x):
    return x.astype(BF16)


def _rounded(x):
    return x.astype(BF16).astype(jnp.float32)


def _sigmoid(x):
    return 1.0 / (1.0 + jnp.exp(-x))


def _silu(x):
    return x * _sigmoid(x)


def _float_key(s):
    b = pltpu.bitcast(s, jnp.int32)
    return b ^ ((b >> 31) & jnp.int32(0x7FFFFFFF))


def _linear_small_kernel(x_ref, w_ref, b_ref, o_ref, *, silu_in):
    x = x_ref[...]
    if silu_in:
        x = _silu(x)
    o_ref[...] = _dot(_bf(x), _bf(w_ref[...])) + b_ref[...]


def _linear_small(x, w, b=None, *, silu_in=False, tn=512):
    m0, k = x.shape
    m = -(-m0 // 16) * 16
    x = jnp.pad(x, ((0, m - m0), (0, 0)))
    n = w.shape[1]
    tn = min(tn, n)
    if b is None:
        b = jnp.zeros((1, n), jnp.float32)
    out = pl.pallas_call(
        functools.partial(_linear_small_kernel, silu_in=silu_in),
        grid=(pl.cdiv(n, tn),),
        in_specs=[pl.BlockSpec((m, k), lambda j: (0, 0)),
                  pl.BlockSpec((k, tn), lambda j: (0, j)),
                  pl.BlockSpec((1, tn), lambda j: (0, j))],
        out_specs=pl.BlockSpec((m, tn), lambda j: (0, j)),
        out_shape=jax.ShapeDtypeStruct((m, n), jnp.float32),
        compiler_params=_cparams(("arbitrary",)),
        name="linear_small",
    )(x, w, b.reshape(1, n))
    return out[:m0]


def _norm_mod_kernel(x_ref, g_ref, sc_ref, sh_ref, o_ref):
    x = x_ref[...]
    y = x * lax.rsqrt(jnp.mean(x * x, axis=-1, keepdims=True) + EPS) * g_ref[...]
    o_ref[...] = (y * (1.0 + sc_ref[0]) + sh_ref[0]).astype(o_ref.dtype)


def _norm_mod(x, g, sc, sh, rows_per_batch, tm):
    n, d = x.shape
    per = rows_per_batch // tm
    return pl.pallas_call(
        _norm_mod_kernel,
        grid=(n // tm,),
        in_specs=[pl.BlockSpec((tm, d), lambda i: (i, 0)),
                  pl.BlockSpec((1, d), lambda i: (0, 0)),
                  pl.BlockSpec((1, 1, d), lambda i: (i // per, 0, 0)),
                  pl.BlockSpec((1, 1, d), lambda i: (i // per, 0, 0))],
        out_specs=pl.BlockSpec((tm, d), lambda i: (i, 0)),
        out_shape=jax.ShapeDtypeStruct((n, d), BF16),
        compiler_params=_cparams(("arbitrary",)),
        name="norm_mod",
    )(x, g.reshape(1, d), sc[:, None, :], sh[:, None, :])


def _mm_kernel(x_ref, w_ref, *rest, scale, sigmoid, n_out):
    o_refs, wbf_ref = rest[:n_out], rest[n_out]

    @pl.when(pl.program_id(1) == 0)
    def _():
        wbf_ref[...] = w_ref[...].astype(BF16)

    acc = _dot(x_ref[...], wbf_ref[...])
    if scale != 1.0:
        acc = acc * scale
    if sigmoid:
        acc = _sigmoid(acc)
    for o in o_refs:
        o[...] = acc.astype(o.dtype)


def _mm(x, w, col0, ncols, out_dtypes, *, scale=1.0, sigmoid=False, tm=512, tn=512):
    m, k = x.shape
    tn = min(tn, ncols)
    tm = min(tm, m)
    assert col0 % tn == 0 and ncols % tn == 0 and m % tm == 0
    jb = col0 // tn
    outs = pl.pallas_call(
        functools.partial(_mm_kernel, scale=scale, sigmoid=sigmoid, n_out=len(out_dtypes)),
        grid=(ncols // tn, m // tm),
        in_specs=[pl.BlockSpec((tm, k), lambda j, i: (i, 0)),
                  pl.BlockSpec((k, tn), lambda j, i: (0, jb + j))],
        out_specs=[pl.BlockSpec((tm, tn), lambda j, i: (i, j)) for _ in out_dtypes],
        out_shape=[jax.ShapeDtypeStruct((m, ncols), dt) for dt in out_dtypes],
        scratch_shapes=[pltpu.VMEM((k, tn), BF16)],
        compiler_params=_cparams(("arbitrary", "arbitrary")),
        name="mm_cols",
    )(x, w)
    return outs


def _rel_bucket(dist):
    max_exact = N_BUCKETS // 2
    d = jnp.maximum(dist, 0)
    large = max_exact + (jnp.log(jnp.maximum(d, 1).astype(jnp.float32) / max_exact)
                         / math.log(MAX_DISTANCE / max_exact)
                         * (N_BUCKETS - max_exact)).astype(jnp.int32)
    large = jnp.minimum(large, N_BUCKETS - 1)
    return jnp.where(d < max_exact, d, large)


def _bias_by_distance(table, dists):
    return table[_rel_bucket(dists)].astype(jnp.float32).T


def _toeplitz_kernel(u_ref, o_ref):
    t = o_ref.shape[2]
    x = jnp.broadcast_to(u_ref[0], (t, 2 * t))
    o_ref[0, 0] = pltpu.roll(x, 0, 1, stride=1, stride_axis=0)[:, :t]


def _near_tiles(table, t):
    nh = table.shape[1]
    k = jnp.arange(2 * t, dtype=jnp.int32)
    gens = []
    for off in (0, t):
        d = jnp.where(k < t, off - k, off + 2 * t - k)
        gens.append(jnp.where(d[None] >= 0, _bias_by_distance(table, d), NEG))
    u = jnp.stack(gens, axis=1).reshape(nh * 2, 1, 2 * t)
    return pl.pallas_call(
        _toeplitz_kernel,
        grid=(nh, 2),
        in_specs=[pl.BlockSpec((1, 1, 2 * t), lambda h, o: (h * 2 + o, 0, 0))],
        out_specs=pl.BlockSpec((1, 1, t, t), lambda h, o: (h, o, 0, 0)),
        out_shape=jax.ShapeDtypeStruct((nh, 2, t, t), jnp.float32),
        compiler_params=_cparams(("arbitrary", "arbitrary")),
        name="bias_tiles",
    )(u)


def _fold_lanes(x, op):
    out = x[:, :LANES]
    for c in range(1, x.shape[1] // LANES):
        out = op(out, x[:, c * LANES:(c + 1) * LANES])
    return out


def _pairwise(n, fn):
    def body(i, carry):
        fn(2 * i)
        fn(2 * i + 1)
        return carry

    lax.fori_loop(0, n // 2, body, 0)

    @pl.when(n % 2 == 1)
    def _():
        fn(n - 1)


def _da_prompt_kernel(cfar_ref, lam_ref, q_ref, k_ref, v_ref, tile_ref, g_ref, o_ref,
                      s_ref, mpart_ref, shift_ref, lpart_ref, acc_ref, *, tq, out_scale):
    h = pl.program_id(1)
    qi = pl.program_id(2)
    q = q_ref[...]
    lane = lax.broadcasted_iota(jnp.int32, q.shape, 1)
    zero = jnp.zeros_like(q)
    q2 = jnp.concatenate([jnp.where(lane < DA_DK, q, zero), jnp.where(lane >= DA_DK, q, zero)], axis=0)
    cfar = cfar_ref[h]
    n_far = jnp.maximum(qi - 1, 0)
    r2 = 2 * tq

    def chunk_rows(kc):
        return pl.ds(pl.multiple_of(kc * tq, tq), tq)

    def scores(kc, bias):
        s = _dot_nt(q2, k_ref[chunk_rows(kc), :])
        if bias is not None:
            s = s + jnp.concatenate([bias, bias], axis=0)
        s_ref[kc] = s
        mpart_ref[...] = jnp.maximum(mpart_ref[...], _fold_lanes(s, jnp.maximum))

    mpart_ref[...] = jnp.full((r2, LANES), NEG, jnp.float32)
    _pairwise(n_far, lambda kc: scores(kc, None))
    m_far = jnp.max(mpart_ref[...], axis=-1, keepdims=True) + cfar
    mpart_ref[...] = jnp.full((r2, LANES), NEG, jnp.float32)

    @pl.when(qi >= 1)
    def _():
        scores(qi - 1, tile_ref[0, 1])

    scores(qi, tile_ref[0, 0])
    m = jnp.maximum(m_far, jnp.max(mpart_ref[...], axis=-1, keepdims=True))
    shift_ref[0] = jnp.broadcast_to(m - cfar, (r2, LANES))
    shift_ref[1] = jnp.broadcast_to(m, (r2, LANES))

    lpart_ref[...] = jnp.zeros((r2, LANES), jnp.float32)
    acc_ref[...] = jnp.zeros((r2, DA_DV), jnp.float32)

    def weights(kc, which):
        s = s_ref[kc]
        sh = shift_ref[which]
        ps = [jnp.exp(s[:, c * LANES:(c + 1) * LANES] - sh) for c in range(tq // LANES)]
        tot = ps[0]
        for pc in ps[1:]:
            tot = tot + pc
        lpart_ref[...] = lpart_ref[...] + tot
        p = jnp.concatenate(ps, axis=1).astype(BF16)
        acc_ref[...] = acc_ref[...] + _dot(p, v_ref[chunk_rows(kc), :])

    _pairwise(n_far, lambda kc: weights(kc, 0))

    @pl.when(qi >= 1)
    def _():
        weights(qi - 1, 1)

    weights(qi, 1)

    lam = lam_ref[0]
    a = acc_ref[...] / jnp.sum(lpart_ref[...], axis=-1, keepdims=True)
    o = a[:tq] - lam * a[tq:]
    o = o * lax.rsqrt(jnp.mean(o * o, axis=-1, keepdims=True) + EPS) * g_ref[...]
    o_ref[...] = (o * out_scale).astype(o_ref.dtype)


def _da_prompt(q, k, v, tiles, cfar, lam, g_subln, nb, t, tq, out_scale):
    n = q.shape[0]
    nq = t // tq
    grid_spec = pltpu.PrefetchScalarGridSpec(
        num_scalar_prefetch=0,
        grid=(nb, DA_HEADS, nq),
        in_specs=[pl.BlockSpec(memory_space=pltpu.SMEM),
                  pl.BlockSpec(memory_space=pltpu.SMEM),
                  pl.BlockSpec((tq, LANES), lambda b, h, i: (b * nq + i, h)),
                  pl.BlockSpec((t, LANES), lambda b, h, i: (b, h)),
                  pl.BlockSpec((t, LANES), lambda b, h, i: (b, h)),
                  pl.BlockSpec((1, 2, tq, tq), lambda b, h, i: (h, 0, 0, 0)),
                  pl.BlockSpec((1, DA_DV), lambda b, h, i: (0, 0))],
        out_specs=pl.BlockSpec((tq, LANES), lambda b, h, i: (b * nq + i, h)),
        scratch_shapes=[pltpu.VMEM((nq, 2 * tq, tq), jnp.float32),
                        pltpu.VMEM((2 * tq, LANES), jnp.float32),
                        pltpu.VMEM((2, 2 * tq, LANES), jnp.float32),
                        pltpu.VMEM((2 * tq, LANES), jnp.float32),
                        pltpu.VMEM((2 * tq, DA_DV), jnp.float32)])
    return pl.pallas_call(
        functools.partial(_da_prompt_kernel, tq=tq, out_scale=out_scale),
        grid_spec=grid_spec,
        out_shape=jax.ShapeDtypeStruct((n, DA_HEADS * DA_DV), BF16),
        compiler_params=_cparams(("arbitrary", "arbitrary", "arbitrary")),
        name="da_prompt",
    )(cfar, lam, q, k, v, tiles, g_subln.reshape(1, DA_DV))


def _count_ge(key_ref, cand, n_chunks, tq):
    def body(kc, acc):
        ge = jnp.where(key_ref[kc] >= cand, 1.0, 0.0)
        for c in range(tq // LANES):
            acc = acc + ge[:, c * LANES:(c + 1) * LANES]
        return acc
    acc = lax.fori_loop(0, n_chunks, body, jnp.zeros((tq, LANES), jnp.float32))
    return jnp.sum(acc, axis=-1, keepdims=True)


def _sa_prompt_kernel(cfar_ref, qs_ref, qx_ref, mq_ref, mk_ref, ks_ref, vs_ref, tile_ref, o_ref,
                      k2_ref, key_ref, cut_ref, m_ref, l_ref, acc_ref, *, tq, topk):
    qi = pl.program_id(1)
    n_chunks = qi + 1
    t = mk_ref.shape[0]

    @pl.when(qi == 0)
    def _():
        kix = mk_ref[:, :LANES]
        lane = lax.broadcasted_iota(jnp.int32, kix.shape, 1)
        k2_ref[0] = jnp.where(lane < IDX_DK, kix, 0.0).astype(BF16)
        k2_ref[1] = jnp.where(lane >= IDX_DK, pltpu.roll(kix, IDX_DK, axis=1), 0.0).astype(BF16)

    wix = mq_ref[:, IDX_DK:IDX_DK + IDX_HEADS]
    wcols = [wix[:, hh:hh + 1] for hh in range(IDX_HEADS)]
    row = lax.broadcasted_iota(jnp.int32, (tq, tq), 0)
    col = lax.broadcasted_iota(jnp.int32, (tq, tq), 1)

    def score_body(kc, carry):
        rows = pl.ds(pl.multiple_of(kc * tq, tq), tq)
        ke = k2_ref[0, rows, :]
        ko = k2_ref[1, rows, :]
        sc = jnp.zeros((tq, tq), jnp.float32)
        for p in range(IDX_HEADS // 2):
            qp = qx_ref[:, p * LANES:(p + 1) * LANES]
            sc = sc + wcols[2 * p] * jnp.maximum(_dot_nt(qp, ke), 0.0)
            sc = sc + wcols[2 * p + 1] * jnp.maximum(_dot_nt(qp, ko), 0.0)
        sc = sc * (IDX_DK ** -0.5 * IDX_HEADS ** -0.5)
        sc = jnp.where((kc < qi) | (row >= col), sc, -jnp.inf)
        key_ref[kc] = _float_key(sc)
        return carry

    lax.fori_loop(0, n_chunks, score_body, 0)

    def bit_body(it, carry):
        thr, cnt_thr = carry
        cand = thr + (jnp.int32(1) << (31 - it))
        cnt = _count_ge(key_ref, cand, n_chunks, tq)
        ok = cnt >= float(topk)
        return jnp.where(ok, cand, thr), jnp.where(ok, cnt, cnt_thr)

    thr0 = jnp.full((tq, 1), INT_MIN, jnp.int32)
    cnt0 = jnp.full((tq, 1), 3.0e38, jnp.float32)
    thr, cnt_thr = lax.fori_loop(0, 32, bit_body, (thr0, cnt0))
    tied = (cnt_thr > float(topk)) & (thr > jnp.int32(KEY_NEG_INF))
    need_tie = jnp.max(jnp.where(tied, 1.0, 0.0)) > 0.0
    thr = jnp.maximum(thr, jnp.int32(KEY_NEG_INF + 1))

    cut_ref[...] = jnp.full((tq, 1), 2 ** 30, jnp.int32)

    @pl.when(need_tie)
    def _():
        def gt_body(kc, acc):
            g = jnp.where(key_ref[kc] > thr, 1.0, 0.0)
            return acc + jnp.sum(g, axis=-1, keepdims=True)
        n_gt = lax.fori_loop(0, n_chunks, gt_body, jnp.zeros((tq, 1), jnp.float32))
        need = float(topk) - n_gt
        n_bits = max(1, int(t).bit_length())

        def cut_body(it, cut):
            cand = cut + (jnp.int32(1) << (n_bits - 1 - it))

            def eq_body(kc, acc):
                pos = kc * tq + col
                e = jnp.where((key_ref[kc] == thr) & (pos < cand), 1.0, 0.0)
                return acc + jnp.sum(e, axis=-1, keepdims=True)
            n_eq = lax.fori_loop(0, n_chunks, eq_body, jnp.zeros((tq, 1), jnp.float32))
            return jnp.where(n_eq <= need, cand, cut)
        cut_ref[...] = lax.fori_loop(0, n_bits, cut_body, jnp.zeros((tq, 1), jnp.int32))

    cut = cut_ref[...]

    m_ref[...] = jnp.full(m_ref.shape, NEG, jnp.float32)
    l_ref[...] = jnp.zeros(l_ref.shape, jnp.float32)
    acc_ref[...] = jnp.zeros(acc_ref.shape, jnp.float32)
    qg = [jnp.concatenate([qs_ref[:, (g * SA_GROUP + j) * LANES:(g * SA_GROUP + j + 1) * LANES]
                           for j in range(SA_GROUP)], axis=0) for g in range(SA_KV_HEADS)]

    def attend(kc, kind):
        rows = pl.ds(pl.multiple_of(kc * tq, tq), tq)
        key = key_ref[kc]
        pos = kc * tq + col
        sel = (key > thr) | ((key == thr) & (pos < cut))
        for g in range(SA_KV_HEADS):
            ks = ks_ref[rows, g * SA_DH:(g + 1) * SA_DH]
            vs = vs_ref[rows, g * SA_DH:(g + 1) * SA_DH]
            s_all = _dot_nt(qg[g], ks)
            ps = []
            for j in range(SA_GROUP):
                hh = g * SA_GROUP + j
                s = s_all[j * tq:(j + 1) * tq]
                if kind == 'far':
                    s = s + cfar_ref[hh]
                else:
                    s = s + tile_ref[hh, kind]
                s = jnp.where(sel, s, NEG)
                m_old = m_ref[hh]
                m_new = jnp.maximum(m_old, jnp.max(s, axis=-1, keepdims=True))
                p = jnp.exp(s - m_new)
                alpha = jnp.exp(m_old - m_new)
                l_ref[hh] = alpha * l_ref[hh] + jnp.sum(p, axis=-1, keepdims=True)
                acc_ref[hh] = alpha * acc_ref[hh]
                m_ref[hh] = m_new
                ps.append(p.astype(BF16))
            pv = _dot(jnp.concatenate(ps, axis=0), vs)
            for j in range(SA_GROUP):
                hh = g * SA_GROUP + j
                acc_ref[hh] = acc_ref[hh] + pv[j * tq:(j + 1) * tq]

    def far_body(kc, carry):
        attend(kc, 'far')
        return carry

    lax.fori_loop(0, jnp.maximum(qi - 1, 0), far_body, 0)

    @pl.when(qi >= 1)
    def _():
        attend(qi - 1, 1)

    attend(qi, 0)

    for hh in range(SA_HEADS):
        o_ref[:, hh * SA_DH:(hh + 1) * SA_DH] = (acc_ref[hh] / l_ref[hh]).astype(o_ref.dtype)


def _sa_prompt(q_sa, q_ix, misc, k_sa, v_sa, tiles, cfar, nb, t, tq, topk):
    n = q_sa.shape[0]
    nq = t // tq
    mw = misc.shape[1]
    kvw = SA_KV_HEADS * SA_DH
    return pl.pallas_call(
        functools.partial(_sa_prompt_kernel, tq=tq, topk=topk),
        grid=(nb, nq),
        in_specs=[pl.BlockSpec(memory_space=pltpu.SMEM),
                  pl.BlockSpec((tq, SA_HEADS * SA_DH), lambda b, i: (b * nq + i, 0)),
                  pl.BlockSpec((tq, IDX_HEADS * IDX_DK), lambda b, i: (b * nq + i, 0)),
                  pl.BlockSpec((tq, mw), lambda b, i: (b * nq + i, 0)),
                  pl.BlockSpec((t, mw), lambda b, i: (b, 0)),
                  pl.BlockSpec((t, kvw), lambda b, i: (b, 0)),
                  pl.BlockSpec((t, kvw), lambda b, i: (b, 0)),
                  pl.BlockSpec((SA_HEADS, 2, tq, tq), lambda b, i: (0, 0, 0, 0))],
        out_specs=pl.BlockSpec((tq, SA_HEADS * SA_DH), lambda b, i: (b * nq + i, 0)),
        out_shape=jax.ShapeDtypeStruct((n, SA_HEADS * SA_DH), BF16),
        scratch_shapes=[pltpu.VMEM((2, t, LANES), BF16),
                        pltpu.VMEM((nq, tq, tq), jnp.int32),
                        pltpu.VMEM((tq, 1), jnp.int32),
                        pltpu.VMEM((SA_HEADS, tq, 1), jnp.float32),
                        pltpu.VMEM((SA_HEADS, tq, 1), jnp.float32),
                        pltpu.VMEM((SA_HEADS, tq, SA_DH), jnp.float32)],
        compiler_params=_cparams(("arbitrary", "arbitrary")),
        name="sa_prompt",
    )(cfar, q_sa, q_ix, misc, misc, k_sa, v_sa, tiles)


def _proj_gate_kernel(oda_ref, osa_ref, wpd_ref, wps_ref, gda_ref, gsa_ref, o_ref, wbf_ref):
    @pl.when(pl.program_id(1) == 0)
    def _():
        wbf_ref[0] = wpd_ref[...].astype(BF16)
        wbf_ref[1] = wps_ref[...].astype(BF16)

    a = _dot(oda_ref[...], wbf_ref[0])
    b = _dot(osa_ref[...], wbf_ref[1])
    o_ref[...] = (gda_ref[...].astype(jnp.float32) * a
                  + gsa_ref[...].astype(jnp.float32) * b).astype(o_ref.dtype)


def _proj_gate(o_da, o_sa, w_pd, w_ps, gates, d, tm=512, tn=512):
    n, kd = o_da.shape
    ks = o_sa.shape[1]
    tn = min(tn, d)
    tm = min(tm, n)
    nj = d // tn
    return pl.pallas_call(
        _proj_gate_kernel,
        grid=(nj, n // tm),
        in_specs=[pl.BlockSpec((tm, kd), lambda j, i: (i, 0)),
                  pl.BlockSpec((tm, ks), lambda j, i: (i, 0)),
                  pl.BlockSpec((kd, tn), lambda j, i: (0, j)),
                  pl.BlockSpec((ks, tn), lambda j, i: (0, j)),
                  pl.BlockSpec((tm, tn), lambda j, i: (i, j)),
                  pl.BlockSpec((tm, tn), lambda j, i: (i, nj + j))],
        out_specs=pl.BlockSpec((tm, tn), lambda j, i: (i, j)),
        out_shape=jax.ShapeDtypeStruct((n, d), BF16),
        scratch_shapes=[pltpu.VMEM((2, kd, tn), BF16)],
        compiler_params=_cparams(("arbitrary", "arbitrary")),
        name="proj_gate",
    )(o_da, o_sa, w_pd, w_ps, gates, gates)


def _mm_resid_kernel(m_ref, w_ref, x_ref, gt_ref, o_ref, wbf_ref):
    @pl.when(pl.program_id(1) == 0)
    def _():
        wbf_ref[...] = w_ref[...].astype(BF16)

    o_ref[...] = x_ref[...] + gt_ref[0] * _dot(m_ref[...], wbf_ref[...])


def _mm_resid(m, w, x, gt, rows_per_batch, tm=512, tn=512):
    n, k = m.shape
    d = w.shape[1]
    tn = min(tn, d)
    tm = min(tm, n)
    per = rows_per_batch // tm
    return pl.pallas_call(
        _mm_resid_kernel,
        grid=(d // tn, n // tm),
        in_specs=[pl.BlockSpec((tm, k), lambda j, i: (i, 0)),
                  pl.BlockSpec((k, tn), lambda j, i: (0, j)),
                  pl.BlockSpec((tm, tn), lambda j, i: (i, j)),
                  pl.BlockSpec((1, 1, tn), lambda j, i: (i // per, 0, j))],
        out_specs=pl.BlockSpec((tm, tn), lambda j, i: (i, j)),
        out_shape=jax.ShapeDtypeStruct((n, d), jnp.float32),
        scratch_shapes=[pltpu.VMEM((k, tn), BF16)],
        compiler_params=_cparams(("arbitrary", "arbitrary")),
        name="mm_resid",
    )(m, w, x, gt[:, None, :])


def _norm_router_kernel(x_ref, g_ref, sc_ref, sh_ref, wr_ref, th_ref, tl_ref, h_ref, lg_ref):
    last = pl.num_programs(0) - 1

    @pl.when(pl.program_id(0) < last)
    def _():
        x = x_ref[...]
        y = x * lax.rsqrt(jnp.mean(x * x, axis=-1, keepdims=True) + EPS) * g_ref[...]
        h = y * (1.0 + sc_ref[0]) + sh_ref[0]
        h_ref[...] = h.astype(h_ref.dtype)
        lg_ref[...] = _dot_nt(_bf(wr_ref[...]), _bf(h))

    @pl.when(pl.program_id(0) == last)
    def _():
        h_ref[...] = th_ref[...]
        lg_ref[...] = tl_ref[...]


def _norm_router(x, g, sc, sh, w_router_t, tail_h, tail_lg, rows_per_batch, tm):
    n, d = x.shape
    per = rows_per_batch // tm
    nt = n // tm
    assert tail_h.shape == (tm, d) and tail_lg.shape == (N_EXPERTS, tm)
    row = lambda i: jnp.minimum(i, nt - 1)
    return pl.pallas_call(
        _norm_router_kernel,
        grid=(nt + 1,),
        in_specs=[pl.BlockSpec((tm, d), lambda i: (row(i), 0)),
                  pl.BlockSpec((1, d), lambda i: (0, 0)),
                  pl.BlockSpec((1, 1, d), lambda i: (row(i) // per, 0, 0)),
                  pl.BlockSpec((1, 1, d), lambda i: (row(i) // per, 0, 0)),
                  pl.BlockSpec((N_EXPERTS, d), lambda i: (0, 0)),
                  pl.BlockSpec((tm, d), lambda i: (0, 0)),
                  pl.BlockSpec((N_EXPERTS, tm), lambda i: (0, 0))],
        out_specs=[pl.BlockSpec((tm, d), lambda i: (i, 0)),
                   pl.BlockSpec((N_EXPERTS, tm), lambda i: (0, i))],
        out_shape=[jax.ShapeDtypeStruct((n + tm, d), BF16),
                   jax.ShapeDtypeStruct((N_EXPERTS, n + tm), jnp.float32)],
        compiler_params=_cparams(("arbitrary",)),
        name="norm_router",
    )(x, g.reshape(1, d), sc[:, None, :], sh[:, None, :], w_router_t, tail_h, tail_lg)


def _route_kernel(lg_ref, b_ref, eidx_ref, wts_ref, rank_ref, cnt_ref, *, tn):
    shape = (N_GROUPS, GROUP_SIZE, tn)
    sc = _sigmoid(lg_ref[...])
    biased = sc + b_ref[...]
    e_iota = lax.broadcasted_iota(jnp.int32, shape, 1)
    g_iota3 = lax.broadcasted_iota(jnp.int32, shape, 0)
    flat_iota = g_iota3 * GROUP_SIZE + e_iota
    g_iota = lax.broadcasted_iota(jnp.int32, (N_GROUPS, 1, tn), 0)
    ninf = -jnp.inf

    m1 = jnp.max(biased, axis=1, keepdims=True)
    first = jnp.min(jnp.where(biased == m1, e_iota, GROUP_SIZE), axis=1, keepdims=True)
    m2 = jnp.max(jnp.where(e_iota == first, ninf, biased), axis=1, keepdims=True)
    cur = m1 + m2
    gsel = jnp.zeros((N_GROUPS, 1, tn), jnp.float32)
    for _ in range(TOPK_GROUPS):
        mx = jnp.max(cur, axis=0, keepdims=True)
        idx = jnp.min(jnp.where(cur == mx, g_iota, N_GROUPS), axis=0, keepdims=True)
        hit = g_iota == idx
        gsel = jnp.where(hit, 1.0, gsel)
        cur = jnp.where(hit, ninf, cur)

    cur = jnp.where(gsel > 0.0, biased, ninf)
    hits, ws = [], []
    for k in range(TOP_K):
        mx = jnp.max(jnp.max(cur, axis=1, keepdims=True), axis=0, keepdims=True)
        cand = jnp.where(cur == mx, flat_iota, N_EXPERTS)
        idx = jnp.min(jnp.min(cand, axis=1, keepdims=True), axis=0, keepdims=True)
        hit = flat_iota == idx
        w = jnp.sum(jnp.sum(jnp.where(hit, sc, 0.0), axis=1, keepdims=True), axis=0, keepdims=True)
        eidx_ref[k:k + 1, :] = idx.reshape(1, tn)
        hits.append(hit)
        ws.append(w)
        cur = jnp.where(hit, ninf, cur)
    wsum = ws[0]
    for w in ws[1:]:
        wsum = wsum + w
    for k in range(TOP_K):
        wts_ref[k:k + 1, :] = (ws[k] / wsum * ROUTED_SCALE).reshape(1, tn)

    member = jnp.zeros(shape, jnp.float32)
    for hit in hits:
        member = jnp.where(hit, 1.0, member)
    member2 = member.reshape(N_EXPERTS, tn)
    r = lax.broadcasted_iota(jnp.int32, (tn, tn), 0)
    c = lax.broadcasted_iota(jnp.int32, (tn, tn), 1)
    upper = jnp.where(r < c, 1.0, 0.0).astype(BF16)
    prefix = _dot(member2.astype(BF16), upper).reshape(shape)
    for k in range(TOP_K):
        rk = jnp.sum(jnp.sum(jnp.where(hits[k], prefix, 0.0), axis=1, keepdims=True), axis=0, keepdims=True)
        rank_ref[k:k + 1, :] = rk.reshape(1, tn)
    cnt = jnp.sum(member2, axis=1, keepdims=True)
    cnt_ref[0] = jnp.broadcast_to(cnt, (N_EXPERTS, LANES))


def _route(logits_t, b_router, tn):
    n_pad = logits_t.shape[1]
    nt = n_pad // tn
    lg3 = logits_t.reshape(N_GROUPS, GROUP_SIZE, n_pad)
    b3 = b_router.astype(jnp.float32).reshape(N_GROUPS, GROUP_SIZE, 1)
    row = lambda dt: jax.ShapeDtypeStruct((TOP_K, n_pad), dt)
    return pl.pallas_call(
        functools.partial(_route_kernel, tn=tn),
        grid=(nt,),
        in_specs=[pl.BlockSpec((N_GROUPS, GROUP_SIZE, tn), lambda i: (0, 0, i)),
                  pl.BlockSpec((N_GROUPS, GROUP_SIZE, 1), lambda i: (0, 0, 0))],
        out_specs=[pl.BlockSpec((TOP_K, tn), lambda i: (0, i)),
                   pl.BlockSpec((TOP_K, tn), lambda i: (0, i)),
                   pl.BlockSpec((TOP_K, tn), lambda i: (0, i)),
                   pl.BlockSpec((1, N_EXPERTS, LANES), lambda i: (i, 0, 0))],
        out_shape=[row(jnp.int32), row(jnp.float32), row(jnp.float32),
                   jax.ShapeDtypeStruct((nt, N_EXPERTS, LANES), jnp.float32)],
        compiler_params=_cparams(("arbitrary",)),
        name="route",
    )(lg3, b3)


def _dest_kernel(eidx_ref, rank_ref, base_ref, o_ref, *, tn):
    e_iota = lax.broadcasted_iota(jnp.int32, (N_EXPERTS, tn), 0)
    base = base_ref[0][:, :1]
    for k in range(TOP_K):
        onehot = e_iota == eidx_ref[k:k + 1, :]
        b = jnp.sum(jnp.where(onehot, base, 0.0), axis=0, keepdims=True)
        o_ref[k:k + 1, :] = (b + rank_ref[k:k + 1, :]).astype(jnp.int32)


def _dest(eidx, rank, base, tn):
    n_pad = eidx.shape[1]
    return pl.pallas_call(
        functools.partial(_dest_kernel, tn=tn),
        grid=(n_pad // tn,),
        in_specs=[pl.BlockSpec((TOP_K, tn), lambda i: (0, i)),
                  pl.BlockSpec((TOP_K, tn), lambda i: (0, i)),
                  pl.BlockSpec((1, N_EXPERTS, LANES), lambda i: (i, 0, 0))],
        out_specs=pl.BlockSpec((TOP_K, tn), lambda i: (0, i)),
        out_shape=jax.ShapeDtypeStruct((TOP_K, n_pad), jnp.int32),
        compiler_params=_cparams(("arbitrary",)),
        name="dest",
    )(eidx, rank, base)


def _pack_words(lo_f32, hi_f32):
    lo = lax.shift_right_logical(pltpu.bitcast(lo_f32, jnp.uint32), jnp.uint32(16))
    hi = pltpu.bitcast(hi_f32, jnp.uint32) & jnp.uint32(0xFFFF0000)
    return hi | lo


def _unpack_words(w):
    lo = pltpu.bitcast(lax.shift_left(w, jnp.uint32(16)), jnp.float32)
    hi = pltpu.bitcast(w & jnp.uint32(0xFFFF0000), jnp.float32)
    return lo, hi


def _bf16_exact(x):
    return x.astype(BF16).astype(jnp.float32)


def _dispatch_kernel(pstart_ref, pcnt_ref, h_ref, dest_hbm, xs_hbm, dsm, pk, zrow, sem_d, sem_r,
                     *, tn, nt):
    i = pl.program_id(0)
    half = pk.shape[1]

    @pl.when(i < nt)
    def _():
        cp = pltpu.make_async_copy(dest_hbm.at[pl.ds(i * (TOP_K * tn), TOP_K * tn)], dsm, sem_d)
        cp.start()
        x = h_ref[...]
        pk[...] = _pack_words(x[:, :half].astype(jnp.float32), x[:, half:].astype(jnp.float32))
        cp.wait()

        def body(r, carry):
            for k in range(TOP_K):
                d = dsm[k * tn + r]
                pltpu.make_async_copy(pk.at[pl.ds(r, 1), :], xs_hbm.at[pl.ds(d, 1), :], sem_r).start()
            return carry

        lax.fori_loop(0, tn, body, 0)
        for k in range(TOP_K):
            pltpu.make_async_copy(pk, xs_hbm.at[pl.ds(0, tn), :], sem_r).wait()

    @pl.when(i == nt)
    def _():
        zrow[...] = jnp.zeros(zrow.shape, zrow.dtype)

        def per_expert(e, carry):
            s0 = pstart_ref[e]
            c = pcnt_ref[e]

            def start(r, cc):
                pltpu.make_async_copy(zrow.at[pl.ds(0, 1), :], xs_hbm.at[pl.ds(s0 + r, 1), :], sem_r).start()
                return cc

            def wait(r, cc):
                pltpu.make_async_copy(zrow.at[pl.ds(0, 1), :], xs_hbm.at[pl.ds(s0, 1), :], sem_r).wait()
                return cc

            lax.fori_loop(0, c, start, 0)
            lax.fori_loop(0, c, wait, 0)
            return carry

        lax.fori_loop(0, N_EXPERTS, per_expert, 0)

        t0 = pstart_ref[N_EXPERTS]
        groups = pcnt_ref[N_EXPERTS]
        rows8 = lambda r: pl.ds(pl.multiple_of(t0 + r * 8, 8), 8)

        def tstart(r, cc):
            pltpu.make_async_copy(zrow, xs_hbm.at[rows8(r), :], sem_r).start()
            return cc

        def twait(r, cc):
            pltpu.make_async_copy(zrow, xs_hbm.at[rows8(0), :], sem_r).wait()
            return cc

        lax.fori_loop(0, groups, tstart, 0)
        lax.fori_loop(0, groups, twait, 0)


def _dispatch(h2, dest_flat, pad_start, pad_cnt, n_rows, tn):
    n_pad, d = h2.shape
    nt = n_pad // tn
    grid_spec = pltpu.PrefetchScalarGridSpec(
        num_scalar_prefetch=2,
        grid=(nt + 1,),
        in_specs=[pl.BlockSpec((tn, d), lambda i, a, b: (jnp.minimum(i, nt - 1), 0)),
                  pl.BlockSpec(memory_space=pl.ANY)],
        out_specs=pl.BlockSpec(memory_space=pl.ANY),
        scratch_shapes=[pltpu.SMEM((TOP_K * tn,), jnp.int32),
                        pltpu.VMEM((tn, d // 2), jnp.uint32),
                        pltpu.VMEM((8, d // 2), jnp.uint32),
                        pltpu.SemaphoreType.DMA(()),
                        pltpu.SemaphoreType.DMA(())])
    return pl.pallas_call(
        functools.partial(_dispatch_kernel, tn=tn, nt=nt),
        grid_spec=grid_spec,
        out_shape=jax.ShapeDtypeStruct((n_rows, d // 2), jnp.uint32),
        compiler_params=_cparams(("arbitrary",)),
        name="dispatch",
    )(pad_start, pad_cnt, h2, dest_flat)


def _expert_kernel(blk_e_ref, nused_ref, x_ref, wg_ref, wu_ref, wd_ref, y_ref, wgb, wub, wdb):
    i = pl.program_id(0)
    nused = nused_ref[0]
    ii = jnp.minimum(i, nused - 1)
    e = blk_e_ref[ii]
    e_prev = blk_e_ref[jnp.maximum(ii - 1, 0)]
    half = x_ref.shape[1]

    @pl.when((i < nused) & ((i == 0) | (e != e_prev)))
    def _():
        wgb[...] = wg_ref[0].astype(BF16)
        wub[...] = wu_ref[0].astype(BF16)
        wdb[...] = wd_ref[0].astype(BF16)

    @pl.when(i < nused)
    def _():
        lo, hi = _unpack_words(x_ref[...])
        xl = lo.astype(BF16)
        xh = hi.astype(BF16)
        g = _dot(xl, wgb[:half, :]) + _dot(xh, wgb[half:, :])
        u = _dot(xl, wub[:half, :]) + _dot(xh, wub[half:, :])
        hmid = (_silu(g) * u).astype(BF16)
        y = _dot(hmid, wdb[...])
        y_ref[...] = _pack_words(_bf16_exact(y[:, :half]), _bf16_exact(y[:, half:]))

    @pl.when(i >= nused)
    def _():
        y_ref[...] = jnp.zeros(y_ref.shape, y_ref.dtype)


def _experts(xs, blk_e, nused, w_gate, w_up, w_down, tb):
    n_rows, half = xs.shape
    _, d, f = w_gate.shape
    nblk = n_rows // tb

    def xmap(i, be, nu):
        return (jnp.minimum(i, nu[0] - 1), 0)

    def wmap(i, be, nu):
        return (be[jnp.minimum(i, nu[0] - 1)], 0, 0)

    grid_spec = pltpu.PrefetchScalarGridSpec(
        num_scalar_prefetch=2,
        grid=(nblk,),
        in_specs=[pl.BlockSpec((tb, half), xmap),
                  pl.BlockSpec((1, d, f), wmap),
                  pl.BlockSpec((1, d, f), wmap),
                  pl.BlockSpec((1, f, d), wmap)],
        out_specs=pl.BlockSpec((tb, half), lambda i, be, nu: (i, 0)),
        scratch_shapes=[pltpu.VMEM((d, f), BF16),
                        pltpu.VMEM((d, f), BF16),
                        pltpu.VMEM((f, d), BF16)])
    return pl.pallas_call(
        _expert_kernel,
        grid_spec=grid_spec,
        out_shape=jax.ShapeDtypeStruct((n_rows, half), jnp.uint32),
        compiler_params=_cparams(("arbitrary",)),
        name="experts",
    )(blk_e, nused, xs, w_gate, w_up, w_down)


def _ffn_kernel(h_ref, wg_ref, wu_ref, wd_ref, o_ref, wgb, wub, wdb):
    @pl.when(pl.program_id(0) == 0)
    def _():
        wgb[...] = wg_ref[...].astype(BF16)
        wub[...] = wu_ref[...].astype(BF16)
        wdb[...] = wd_ref[...].astype(BF16)

    x = h_ref[...]
    hmid = (_silu(_dot(x, wgb[...])) * _dot(x, wub[...])).astype(BF16)
    o_ref[...] = _dot(hmid, wdb[...]).astype(o_ref.dtype)


def _ffn_shared(h2, wg, wu, wd, tm):
    n_pad, d = h2.shape
    f = wg.shape[1]
    return pl.pallas_call(
        _ffn_kernel,
        grid=(n_pad // tm,),
        in_specs=[pl.BlockSpec((tm, d), lambda i: (i, 0)),
                  pl.BlockSpec((d, f), lambda i: (0, 0)),
                  pl.BlockSpec((d, f), lambda i: (0, 0)),
                  pl.BlockSpec((f, d), lambda i: (0, 0))],
        out_specs=pl.BlockSpec((tm, d), lambda i: (i, 0)),
        out_shape=jax.ShapeDtypeStruct((n_pad, d), BF16),
        scratch_shapes=[pltpu.VMEM((d, f), BF16),
                        pltpu.VMEM((d, f), BF16),
                        pltpu.VMEM((f, d), BF16)],
        compiler_params=_cparams(("arbitrary",)),
        name="ffn_shared",
    )(h2, wg, wu, wd)


def _combine_kernel(dest_hbm, ys_hbm, wts_ref, sh_ref, x_ref, gt_ref, g_ref, o_ref,
                    dsm, buf, sem_d, sem_r, *, tn, tile0, final_norm):
    i = pl.program_id(0)
    cp = pltpu.make_async_copy(dest_hbm.at[pl.ds((tile0 + i) * (TOP_K * tn), TOP_K * tn)], dsm, sem_d)
    cp.start()
    cp.wait()

    def body(r, carry):
        for k in range(TOP_K):
            d = dsm[k * tn + r]
            pltpu.make_async_copy(ys_hbm.at[pl.ds(d, 1), :], buf.at[k, pl.ds(r, 1), :], sem_r).start()
        return carry

    lax.fori_loop(0, tn, body, 0)
    for k in range(TOP_K):
        pltpu.make_async_copy(ys_hbm.at[pl.ds(0, tn), :], buf.at[k], sem_r).wait()

    half = buf.shape[2]
    wts = wts_ref[...]
    acc_lo = jnp.zeros((tn, half), jnp.float32)
    acc_hi = jnp.zeros((tn, half), jnp.float32)
    for k in range(TOP_K):
        lo, hi = _unpack_words(buf[k])
        wk = wts[:, k:k + 1]
        acc_lo = acc_lo + wk * lo
        acc_hi = acc_hi + wk * hi
    sh = sh_ref[...].astype(jnp.float32)
    gt = gt_ref[0]
    x_lo = x_ref[:, :half] + gt[:, :half] * (acc_lo + sh[:, :half])
    x_hi = x_ref[:, half:] + gt[:, half:] * (acc_hi + sh[:, half:])
    if final_norm:
        ms = (jnp.sum(x_lo * x_lo, axis=-1, keepdims=True)
              + jnp.sum(x_hi * x_hi, axis=-1, keepdims=True)) / (2 * half)
        inv = lax.rsqrt(ms + EPS)
        g = g_ref[...]
        x_lo = x_lo * inv * g[:, :half]
        x_hi = x_hi * inv * g[:, half:]
    o_ref[:, :half] = x_lo
    o_ref[:, half:] = x_hi


def _combine(dest_flat, ys, wts_tok, shared, x1, gt3, g_final, tile0, tn, final_norm):
    rows, d = x1.shape
    nt = rows // tn
    gr = gt3.shape[1]
    per = nt // gt3.shape[0]
    return pl.pallas_call(
        functools.partial(_combine_kernel, tn=tn, tile0=tile0, final_norm=final_norm),
        grid=(nt,),
        in_specs=[pl.BlockSpec(memory_space=pl.ANY),
                  pl.BlockSpec(memory_space=pl.ANY),
                  pl.BlockSpec((tn, TOP_K), lambda i: (tile0 + i, 0)),
                  pl.BlockSpec((tn, d), lambda i: (tile0 + i, 0)),
                  pl.BlockSpec((tn, d), lambda i: (i, 0)),
                  pl.BlockSpec((1, gr, d), lambda i: (i // per, 0, 0)),
                  pl.BlockSpec((1, d), lambda i: (0, 0))],
        out_specs=pl.BlockSpec((tn, d), lambda i: (i, 0)),
        out_shape=jax.ShapeDtypeStruct((rows, d), jnp.float32),
        scratch_shapes=[pltpu.SMEM((TOP_K * tn,), jnp.int32),
                        pltpu.VMEM((TOP_K, tn, d // 2), jnp.uint32),
                        pltpu.SemaphoreType.DMA(()),
                        pltpu.SemaphoreType.DMA(())],
        compiler_params=_cparams(("arbitrary",)),
        name="combine",
    )(dest_flat, ys, wts_tok, shared, x1, gt3, g_final.reshape(1, d))


def _page_specs(shape, n, pg):
    def mk(u):
        return pl.BlockSpec((1,) + shape, lambda b, j, pt: (pt[b, j * pg + u], 0, 0))
    return [mk(u) for u in range(n)]


def _online_rows(s, m_ref, l_ref):
    m_old = m_ref[...]
    m_new = jnp.maximum(m_old, jnp.max(s, axis=-1, keepdims=True))
    p = jnp.exp(s - m_new)
    alpha = jnp.exp(m_old - m_new)
    l_ref[...] = alpha * l_ref[...] + jnp.sum(p, axis=-1, keepdims=True)
    m_ref[...] = m_new
    return p, alpha


def _da_sample_kernel(pt_ref, q_ref, kn_ref, vn_ref, bl_ref, cf_ref, b0_ref, ex_ref, hm_ref, *rest,
                      pg, n_pages):
    k_refs, v_refs = rest[:pg], rest[pg:2 * pg]
    o_ref, m_ref, l_ref, acc_ref = rest[2 * pg:]
    j = pl.program_id(1)
    q = q_ref[0]

    @pl.when(j == 0)
    def _():
        s_new = jnp.sum(_rounded(q) * _rounded(kn_ref[0]), axis=-1, keepdims=True)
        m_ref[...] = s_new + b0_ref[...]
        l_ref[...] = jnp.ones(l_ref.shape, jnp.float32)
        acc_ref[...] = _rounded(vn_ref[0])

    qb = _bf(q)
    for u in range(pg):
        s = _dot(qb, _bf(k_refs[u][0]))
        is_last = (j * pg + u) == (n_pages - 1)
        s = s + jnp.where(is_last, bl_ref[...], cf_ref[...])
        p, alpha = _online_rows(s, m_ref, l_ref)
        pe = _bf(_dot(_bf(p), ex_ref[...]) * hm_ref[...])
        acc_ref[...] = alpha * acc_ref[...] + _dot(pe, _bf(v_refs[u][0]))

    @pl.when(j == pl.num_programs(1) - 1)
    def _():
        o_ref[0] = acc_ref[...] / l_ref[...]


def _da_sample(page_table, qbd, knew, vnew, bias_last, cfar, bias0, kt, v2, pg):
    nb, n_pages = page_table.shape
    r, w = qbd.shape[1:]
    rows_v, dv = v2.shape[1:]
    nh = rows_v // PAGE_SIZE
    col = jnp.arange(rows_v, dtype=jnp.int32)
    expand = (col[None, :] // nh == jnp.arange(PAGE_SIZE, dtype=jnp.int32)[:, None]).astype(BF16)
    head_mask = (col[None, :] % nh == (jnp.arange(r, dtype=jnp.int32)[:, None] // 2)).astype(jnp.float32)
    full = lambda shp: pl.BlockSpec(shp, lambda b, j, pt: (0,) * len(shp))
    per_b = lambda shp: pl.BlockSpec((1,) + shp, lambda b, j, pt: (b, 0, 0))
    grid_spec = pltpu.PrefetchScalarGridSpec(
        num_scalar_prefetch=1,
        grid=(nb, n_pages // pg),
        in_specs=[per_b((r, w)), per_b((1, w)), per_b((r, dv)),
                  full((r, PAGE_SIZE)), full((r, 1)), full((r, 1)),
                  full((PAGE_SIZE, rows_v)), full((r, rows_v))]
                 + _page_specs((w, PAGE_SIZE), pg, pg) + _page_specs((rows_v, dv), pg, pg),
        out_specs=per_b((r, dv)),
        scratch_shapes=[pltpu.VMEM((r, 1), jnp.float32),
                        pltpu.VMEM((r, 1), jnp.float32),
                        pltpu.VMEM((r, dv), jnp.float32)])
    return pl.pallas_call(
        functools.partial(_da_sample_kernel, pg=pg, n_pages=n_pages),
        grid_spec=grid_spec,
        out_shape=jax.ShapeDtypeStruct((nb, r, dv), jnp.float32),
        compiler_params=_cparams(("arbitrary", "arbitrary")),
        name="da_sample",
    )(page_table, qbd, knew, vnew, bias_last, cfar, bias0, expand, head_mask,
      *([kt] * pg), *([v2] * pg))


def _idx_sample_kernel(pt_ref, q_ref, w_ref, kn_ref, *rest, pg, n_pages, topk):
    k_refs = rest[:pg]
    sel_ref, sc_ref = rest[pg:]
    j = pl.program_id(1)
    q = q_ref[0]
    w = _rounded(w_ref[0])
    scale = IDX_DK ** -0.5 * IDX_HEADS ** -0.5
    rows = sc_ref.shape[0]
    lane = lax.broadcasted_iota(jnp.int32, (1, PAGE_SIZE), 1)

    @pl.when(j == 0)
    def _():
        sc_ref[...] = jnp.full(sc_ref.shape, -jnp.inf, jnp.float32)
        d = jnp.maximum(jnp.sum(_rounded(q) * _rounded(kn_ref[0]), axis=-1, keepdims=True), 0.0)
        s_new = jnp.sum(w * _rounded(d), axis=0, keepdims=True) * scale
        sc_ref[n_pages:n_pages + 1, :] = jnp.where(lane == 0, s_new, -jnp.inf)

    qb = _bf(q)
    for u in range(pg):
        d = _rounded(jnp.maximum(_dot(qb, _bf(k_refs[u][0])), 0.0))
        sc_ref[pl.ds(j * pg + u, 1), :] = jnp.sum(w * d, axis=0, keepdims=True) * scale

    @pl.when(j == pl.num_programs(1) - 1)
    def _():
        key = _float_key(sc_ref[...])
        pos = (lax.broadcasted_iota(jnp.int32, key.shape, 0) * PAGE_SIZE
               + lax.broadcasted_iota(jnp.int32, key.shape, 1))

        def bit_body(it, thr):
            cand = thr + (jnp.int32(1) << (31 - it))
            cnt = jnp.sum(jnp.where(key >= cand, 1.0, 0.0))
            return jnp.where(cnt >= float(topk), cand, thr)

        thr = lax.fori_loop(0, 32, bit_body, jnp.int32(INT_MIN))
        thr = jnp.maximum(thr, jnp.int32(KEY_NEG_INF + 1))
        need = float(topk) - jnp.sum(jnp.where(key > thr, 1.0, 0.0))
        n_bits = max(1, int(rows * PAGE_SIZE).bit_length())

        def cut_body(it, cut):
            cand = cut + (jnp.int32(1) << (n_bits - 1 - it))
            n_eq = jnp.sum(jnp.where((key == thr) & (pos < cand), 1.0, 0.0))
            return jnp.where(n_eq <= need, cand, cut)

        cut = lax.fori_loop(0, n_bits, cut_body, jnp.int32(0))
        sel_ref[0] = jnp.where((key > thr) | ((key == thr) & (pos < cut)), 1.0, 0.0)


def _idx_sample(page_table, qix, wix, knew, kc, pg, topk):
    nb, n_pages = page_table.shape
    rows = -(-(n_pages + 1) // 8) * 8
    per_b = lambda shp: pl.BlockSpec((1,) + shp, lambda b, j, pt: (b, 0, 0))
    grid_spec = pltpu.PrefetchScalarGridSpec(
        num_scalar_prefetch=1,
        grid=(nb, n_pages // pg),
        in_specs=[per_b((IDX_HEADS, IDX_DK)), per_b((IDX_HEADS, 1)), per_b((1, IDX_DK))]
                 + _page_specs((IDX_DK, PAGE_SIZE), pg, pg),
        out_specs=per_b((rows, PAGE_SIZE)),
        scratch_shapes=[pltpu.VMEM((rows, PAGE_SIZE), jnp.float32)])
    return pl.pallas_call(
        functools.partial(_idx_sample_kernel, pg=pg, n_pages=n_pages, topk=topk),
        grid_spec=grid_spec,
        out_shape=jax.ShapeDtypeStruct((nb, rows, PAGE_SIZE), jnp.float32),
        compiler_params=_cparams(("arbitrary", "arbitrary")),
        name="idx_sample",
    )(page_table, qix, wix, knew, *([kc] * pg))


def _sa_sample_kernel(pt_ref, q_ref, kn_ref, vn_ref, sel_ref, bl_ref, cf_ref, b0_ref, gm_ref, *rest,
                      pg, n_pages):
    k_refs, v_refs = rest[:pg], rest[pg:2 * pg]
    o_ref, m_ref, l_ref, acc_ref = rest[2 * pg:]
    j = pl.program_id(1)
    q = q_ref[0]
    scale = SA_DH ** -0.5
    srow = sel_ref.shape[2]

    @pl.when(j == 0)
    def _():
        on = sel_ref[0, n_pages:n_pages + 1, 0:1] > 0.0
        s_new = jnp.sum(_rounded(q) * _rounded(kn_ref[0]), axis=-1, keepdims=True) * scale + b0_ref[...]
        m_ref[...] = jnp.where(on, s_new, NEG)
        l_ref[...] = jnp.where(on, jnp.ones(l_ref.shape, jnp.float32), 0.0)
        acc_ref[...] = jnp.where(on, _rounded(vn_ref[0]), 0.0)

    qb = _bf(q)
    for u in range(pg):
        page = j * pg + u
        s = _dot_nt(qb, _bf(k_refs[u][0])) * scale
        s = s + jnp.where(page == (n_pages - 1), bl_ref[...], cf_ref[...])
        keep = (sel_ref[0, pl.ds(page, 1), :] > 0.0) & (gm_ref[...] > 0.0)
        s = jnp.where(keep, s, NEG)
        p, alpha = _online_rows(s, m_ref, l_ref)
        acc_ref[...] = alpha * acc_ref[...] + _dot(_bf(p), _bf(v_refs[u][0]))

    @pl.when(j == pl.num_programs(1) - 1)
    def _():
        o_ref[0] = acc_ref[...] / l_ref[...]


def _sa_sample(page_table, q, knew, vnew, sel2, bias_last2, cfar, bias0, group_mask, k2, v2, pg):
    nb, n_pages = page_table.shape
    r, dh = q.shape[1:]
    srows, cols = sel2.shape[1:]
    full = lambda shp: pl.BlockSpec(shp, lambda b, j, pt: (0,) * len(shp))
    per_b = lambda shp: pl.BlockSpec((1,) + shp, lambda b, j, pt: (b, 0, 0))
    grid_spec = pltpu.PrefetchScalarGridSpec(
        num_scalar_prefetch=1,
        grid=(nb, n_pages // pg),
        in_specs=[per_b((r, dh)), per_b((r, dh)), per_b((r, dh)), per_b((srows, cols)),
                  full((r, cols)), full((r, 1)), full((r, 1)), full((r, cols))]
                 + _page_specs((cols, dh), pg, pg) + _page_specs((cols, dh), pg, pg),
        out_specs=per_b((r, dh)),
        scratch_shapes=[pltpu.VMEM((r, 1), jnp.float32),
                        pltpu.VMEM((r, 1), jnp.float32),
                        pltpu.VMEM((r, dh), jnp.float32)])
    return pl.pallas_call(
        functools.partial(_sa_sample_kernel, pg=pg, n_pages=n_pages),
        grid_spec=grid_spec,
        out_shape=jax.ShapeDtypeStruct((nb, r, dh), jnp.float32),
        compiler_params=_cparams(("arbitrary", "arbitrary")),
        name="sa_sample",
    )(page_table, q, knew, vnew, sel2, bias_last2, cfar, bias0, group_mask, *([k2] * pg), *([v2] * pg))


ROW_TILE = 256
COMBINE_TILE = 128
EXPERT_BLOCK = 256
TAIL_ROWS = 256
PAGES_PER_STEP_DA = 8
PAGES_PER_STEP_SA = 16
PAGES_PER_STEP_IDX = 32


def _rms_rows(x, g):
    return x * lax.rsqrt(jnp.mean(x * x, axis=-1, keepdims=True) + EPS) * g


def _pad_rows(x, rows):
    return jnp.pad(x, ((0, rows - x.shape[0]), (0, 0)))


def _tile_major(dest, tn):
    k, n = dest.shape
    return dest.reshape(k, n // tn, tn).transpose(1, 0, 2).reshape(-1)


def kernel(x_prompt, x_sample, c_prompt, c_sample, cache_da_k, cache_da_v, cache_sa_k, cache_sa_v, cache_idx_k, page_table, rel_bias_table, w_ada, b_ada, g_attn, g_ffn, w_in, lambda_q1, lambda_k1, lambda_q2, lambda_k2, g_subln, w_proj_da, w_proj_sa, w_out, w_router, b_router, w_gate, w_up, w_down, w_sh_gate, w_sh_up, w_sh_down, g_final):
    f32, bf16 = jnp.float32, BF16
    nb, t, d = x_prompt.shape
    nbs, dec_seq, _ = x_sample.shape
    assert dec_seq == 1
    depth = w_in.shape[0]
    n_pages = page_table.shape[1]
    past_len = n_pages * PAGE_SIZE
    n = nb * t
    n_pad = n + TAIL_ROWS
    tq = min(256, t)
    assert tq >= MAX_DISTANCE and t % tq == 0 and n % ROW_TILE == 0 and nbs <= TAIL_ROWS
    assert TAIL_ROWS == ROW_TILE and EXPERT_BLOCK % 8 == 0
    topk_p = min(SA_TOPK_MAX, t // 4)
    topk_s = min(SA_TOPK_MAX, (past_len + dec_seq) // 4)
    assert topk_p <= tq

    sizes = [DA_HEADS * 2 * DA_DK, DA_HEADS * 2 * DA_DK, DA_HEADS * DA_DV, SA_HEADS * SA_DH,
             SA_KV_HEADS * SA_DH, SA_KV_HEADS * SA_DH, IDX_HEADS * IDX_DK, IDX_DK, IDX_HEADS, d, d]
    offs = [sum(sizes[:i]) for i in range(len(sizes) + 1)]
    (o_qda, o_kda, o_vda, o_qsa, o_ksa, o_vsa, o_qix, o_kix, o_wix, o_gda, o_gsa, _) = offs
    misc_w = 2 * LANES

    table = rel_bias_table.astype(f32)
    cfar = table[N_BUCKETS - 1]
    tiles_da = _near_tiles(table[:, :DA_HEADS], tq)
    tiles_sa = _near_tiles(table[:, DA_HEADS:], tq)
    last_dist = past_len - ((n_pages - 1) * PAGE_SIZE + jnp.arange(PAGE_SIZE, dtype=jnp.int32))
    bias_last = _bias_by_distance(table, last_dist)
    bias0 = _bias_by_distance(table, jnp.zeros((1,), jnp.int32))
    rep2 = lambda a: jnp.repeat(a, 2, axis=0)

    xp = x_prompt.reshape(n, d)
    xs = x_sample.reshape(nbs, d)
    c_all = jnp.concatenate([c_prompt, c_sample], axis=0)
    leaves_p, leaves_s = [], []
    for l in range(depth):
        lam_init = 0.8 - 0.6 * math.exp(-0.3 * l)
        lam = (jnp.exp(jnp.sum(lambda_q1[l].astype(f32) * lambda_k1[l].astype(f32)))
               - jnp.exp(jnp.sum(lambda_q2[l].astype(f32) * lambda_k2[l].astype(f32))) + lam_init)
        mod = _linear_small(c_all, w_ada[l], b_ada[l], silu_in=True)
        sh1, sc1, gt1, sh2, sc2, gt2 = jnp.split(mod[:nb], 6, axis=-1)
        sh1s, sc1s, gt1s, sh2s, sc2s, gt2s = jnp.split(mod[nb:], 6, axis=-1)
        w = w_in[l]

        h = _norm_mod(xp, g_attn[l], sc1, sh1, t, ROW_TILE)
        q_da, = _mm(h, w, o_qda, sizes[0], (bf16,), scale=DA_DK ** -0.5)
        k_da, k_da_b = _mm(h, w, o_kda, sizes[1], (f32, bf16))
        v_da, v_da_b = _mm(h, w, o_vda, sizes[2], (f32, bf16))
        q_sa, = _mm(h, w, o_qsa, sizes[3], (bf16,), scale=SA_DH ** -0.5)
        k_sa, k_sa_b = _mm(h, w, o_ksa, sizes[4], (f32, bf16))
        v_sa, v_sa_b = _mm(h, w, o_vsa, sizes[5], (f32, bf16))
        q_ix, = _mm(h, w, o_qix, sizes[6], (bf16,))
        misc, = _mm(h, w, o_kix, misc_w, (f32,))
        gates, = _mm(h, w[:, o_gda:], 0, 2 * d, (bf16,), sigmoid=True)
        k_ix = misc[:, :IDX_DK]

        o_da = _da_prompt(q_da, k_da_b, v_da_b, tiles_da, cfar[:DA_HEADS], lam.reshape(1),
                          g_subln[l].astype(f32), nb, t, tq, 1.0 - lam_init)
        o_sa = _sa_prompt(q_sa, q_ix, misc, k_sa_b, v_sa_b, tiles_sa, cfar[DA_HEADS:], nb, t, tq, topk_p)
        m = _proj_gate(o_da, o_sa, w_proj_da[l], w_proj_sa[l], gates, d)
        x1 = _mm_resid(m, w_out[l], xp, gt1, t)

        hs = _rms_rows(xs, g_attn[l].astype(f32)) * (1.0 + sc1s) + sh1s
        ps = _linear_small(hs, w)
        seg = lambda i: ps[:, offs[i]:offs[i + 1]]
        q_da_s, k_da_s, v_da_s, q_sa_s, k_sa_s, v_sa_s, q_ix_s, k_ix_s, w_ix_s, gda_s, gsa_s = [
            seg(i) for i in range(11)]

        r_da = 2 * DA_HEADS
        q16 = (q_da_s * DA_DK ** -0.5).reshape(nbs, r_da, DA_DK)
        qbd_da = (jnp.eye(r_da, dtype=f32)[None, :, :, None] * q16[:, None, :, :]).reshape(nbs, r_da, -1)
        kt_da = cache_da_k[l].transpose(0, 2, 3, 4, 1).reshape(-1, DA_HEADS * 2 * DA_DK, PAGE_SIZE)
        v2_da = cache_da_v[l].reshape(-1, PAGE_SIZE * DA_HEADS, DA_DV)
        a_da = _da_sample(page_table, qbd_da, k_da_s[:, None, :],
                          rep2(v_da_s.reshape(nbs * DA_HEADS, DA_DV)).reshape(nbs, r_da, DA_DV),
                          rep2(bias_last[:DA_HEADS]), rep2(cfar[:DA_HEADS, None]), rep2(bias0[:DA_HEADS]),
                          kt_da, v2_da, math.gcd(PAGES_PER_STEP_DA, n_pages))
        a4 = a_da.reshape(nbs, DA_HEADS, 2, DA_DV)
        o_da_s = a4[:, :, 0] - lam * a4[:, :, 1]
        o_da_s = _rms_rows(o_da_s, g_subln[l].astype(f32)) * (1.0 - lam_init)

        sel = _idx_sample(page_table, q_ix_s.reshape(nbs, IDX_HEADS, IDX_DK), w_ix_s[:, :, None],
                          k_ix_s[:, None, :], cache_idx_k[l].transpose(0, 2, 1),
                          math.gcd(PAGES_PER_STEP_IDX, n_pages), topk_s)
        r_sa = 2 * SA_HEADS
        kv_of_row = jnp.minimum(jnp.arange(r_sa) // SA_GROUP, SA_KV_HEADS - 1)
        pad_sa = lambda a: jnp.pad(a, ((0, 0), (0, r_sa - SA_HEADS), (0, 0)))
        rep_kv = lambda a: jnp.repeat(a, SA_KV_HEADS, axis=-1)
        col_kv = jnp.arange(PAGE_SIZE * SA_KV_HEADS) % SA_KV_HEADS
        k2_sa = cache_sa_k[l].reshape(-1, PAGE_SIZE * SA_KV_HEADS, SA_DH)
        v2_sa = cache_sa_v[l].reshape(-1, PAGE_SIZE * SA_KV_HEADS, SA_DH)
        a_sa = _sa_sample(page_table, pad_sa(q_sa_s.reshape(nbs, SA_HEADS, SA_DH)),
                          k_sa_s.reshape(nbs, SA_KV_HEADS, SA_DH)[:, kv_of_row],
                          v_sa_s.reshape(nbs, SA_KV_HEADS, SA_DH)[:, kv_of_row],
                          rep_kv(sel), rep_kv(_pad_rows(bias_last[DA_HEADS:], r_sa)),
                          _pad_rows(cfar[DA_HEADS:, None], r_sa), _pad_rows(bias0[DA_HEADS:], r_sa),
                          (col_kv[None, :] == kv_of_row[:, None]).astype(f32),
                          k2_sa, v2_sa, math.gcd(PAGES_PER_STEP_SA, n_pages))
        o_sa_s = a_sa[:, :SA_HEADS]

        pda = _linear_small(o_da_s.reshape(nbs, -1), w_proj_da[l])
        psa = _linear_small(o_sa_s.reshape(nbs, -1), w_proj_sa[l])
        ms = _sigmoid(gda_s) * pda + _sigmoid(gsa_s) * psa
        x1s = xs + gt1s * _linear_small(ms, w_out[l])
        h2s = _rms_rows(x1s, g_ffn[l].astype(f32)) * (1.0 + sc2s) + sh2s
        lg_s = _linear_small(h2s, w_router[l])

        h2_all, lg_all = _norm_router(x1, g_ffn[l], sc2, sh2, w_router[l].T,
                                      _pad_rows(h2s.astype(bf16), TAIL_ROWS),
                                      _pad_rows(lg_s, TAIL_ROWS).T, t, ROW_TILE)
        eidx, wts, rank, cnt = _route(lg_all, b_router[l], ROW_TILE)
        cnt_tile = cnt[:, :, 0]
        total = jnp.sum(cnt_tile, axis=0)
        padded = jnp.ceil(total / EXPERT_BLOCK) * EXPERT_BLOCK
        pends = jnp.cumsum(padded)
        pstart = pends - padded
        base = pstart[None, :] + jnp.cumsum(cnt_tile, axis=0) - cnt_tile
        dest = _dest(eidx, rank, jnp.broadcast_to(base[:, :, None], base.shape + (LANES,)), ROW_TILE)
        n_blk = -(-(n_pad * TOP_K) // EXPERT_BLOCK) + N_EXPERTS
        blk_start = (jnp.arange(n_blk) * EXPERT_BLOCK).astype(f32)
        blk_e = jnp.minimum(jnp.sum(pends[None, :] <= blk_start[:, None], axis=1), N_EXPERTS - 1).astype(jnp.int32)
        n_used = (pends[-1] / EXPERT_BLOCK).astype(jnp.int32).reshape(1)
        n_rows = n_blk * EXPERT_BLOCK
        pad_start = jnp.concatenate([pstart + total, pends[-1:]]).astype(jnp.int32)
        pad_cnt = jnp.concatenate([padded - total, (n_rows - pends[-1:]) / 8]).astype(jnp.int32)
        xs_sorted = _dispatch(h2_all, _tile_major(dest, ROW_TILE), pad_start, pad_cnt, n_rows, ROW_TILE)
        ys = _experts(xs_sorted, blk_e, n_used, w_gate[l], w_up[l], w_down[l], EXPERT_BLOCK)
        shared = _ffn_shared(h2_all, w_sh_gate[l], w_sh_up[l], w_sh_down[l], ROW_TILE)
        dest_c = _tile_major(dest, COMBINE_TILE)
        wts_tok = wts.T
        last = l == depth - 1
        xp = _combine(dest_c, ys, wts_tok, shared, x1, gt2[:, None, :], g_final.astype(f32),
                      0, COMBINE_TILE, last)
        tail = _combine(dest_c, ys, wts_tok, shared, _pad_rows(x1s, TAIL_ROWS),
                        _pad_rows(gt2s, TAIL_ROWS).reshape(-1, COMBINE_TILE, d), g_final.astype(f32),
                        n // COMBINE_TILE, COMBINE_TILE, last)
        xs = tail[:nbs]
        leaves_p.append((k_da, v_da, k_sa, v_sa, k_ix))
        leaves_s.append((k_da_s, v_da_s, k_sa_s, v_sa_s, k_ix_s))

    shapes = [(DA_HEADS, 2, DA_DK), (DA_HEADS, DA_DV), (SA_KV_HEADS, SA_DH), (SA_KV_HEADS, SA_DH), (IDX_DK,)]
    out_p = [jnp.stack([lv[i].reshape((nb, t) + shapes[i]) for lv in leaves_p]) for i in range(5)]
    out_s = [jnp.stack([lv[i].reshape((nbs, dec_seq) + shapes[i]) for lv in leaves_s]) for i in range(5)]
    return (xp.reshape(nb, t, d), xs.reshape(nbs, dec_seq, d), *out_p, *out_s)
```

```python
import functools
import math

import jax
import jax.numpy as jnp
from jax import lax
from jax.experimental import pallas as pl
from jax.experimental.pallas import tpu as pltpu

DA_HEADS = 8
DA_DK = 64
DA_DV = 2 * DA_DK
SA_HEADS = 8
SA_KV_HEADS = 2
SA_DH = 128
SA_GROUP = SA_HEADS // SA_KV_HEADS
IDX_HEADS = 16
IDX_DK = 64
SA_TOPK_MAX = 256
N_BUCKETS = 32
MAX_DISTANCE = 128
N_EXPERTS = 64
N_GROUPS = 8
GROUP_SIZE = N_EXPERTS // N_GROUPS
TOPK_GROUPS = 4
TOP_K = 8
ROUTED_SCALE = 2.5
PAGE_SIZE = 128
EPS = 1e-6

LANES = 128
VMEM_LIMIT = 56 * 1024 * 1024

BF16 = jnp.bfloat16
NEG = -1e30
INT_MIN = -(2 ** 31)
KEY_NEG_INF = (0xFF800000 ^ 0x7FFFFFFF) - (1 << 32)

_NT = (((1,), (1,)), ((), ()))


def _cparams(sem):
    return pltpu.CompilerParams(dimension_semantics=sem, vmem_limit_bytes=VMEM_LIMIT)


def _dot(a, b):
    return jnp.dot(a, b, preferred_element_type=jnp.float32)


def _dot_nt(a, b):
    return lax.dot_general(a, b, _NT, preferred_element_type=jnp.float32)


def _bf(x):
    return x.astype(BF16)


def _rounded(x):
    return x.astype(BF16).astype(jnp.float32)


def _sigmoid(x):
    return 1.0 / (1.0 + jnp.exp(-x))


def _silu(x):
    return x * _sigmoid(x)


def _float_key(s):
    b = pltpu.bitcast(s, jnp.int32)
    return b ^ ((b >> 31) & jnp.int32(0x7FFFFFFF))


def _linear_small_kernel(x_ref, w_ref, b_ref, o_ref, *, silu_in):
    x = x_ref[...]
    if silu_in:
        x = _silu(x)
    o_ref[...] = _dot(_bf(x), _bf(w_ref[...])) + b_ref[...]


def _linear_small(x, w, b=None, *, silu_in=False, tn=512):
    m0, k = x.shape
    m = -(-m0 // 16) * 16
    x = jnp.pad(x, ((0, m - m0), (0, 0)))
    n = w.shape[1]
    tn = min(tn, n)
    if b is None:
        b = jnp.zeros((1, n), jnp.float32)
    out = pl.pallas_call(
        functools.partial(_linear_small_kernel, silu_in=silu_in),
        grid=(pl.cdiv(n, tn),),
        in_specs=[pl.BlockSpec((m, k), lambda j: (0, 0)),
                  pl.BlockSpec((k, tn), lambda j: (0, j)),
                  pl.BlockSpec((1, tn), lambda j: (0, j))],
        out_specs=pl.BlockSpec((m, tn), lambda j: (0, j)),
        out_shape=jax.ShapeDtypeStruct((m, n), jnp.float32),
        compiler_params=_cparams(("arbitrary",)),
        name="linear_small",
    )(x, w, b.reshape(1, n))
    return out[:m0]


def _norm_mod_kernel(x_ref, g_ref, sc_ref, sh_ref, o_ref):
    x = x_ref[...]
    y = x * lax.rsqrt(jnp.mean(x * x, axis=-1, keepdims=True) + EPS) * g_ref[...]
    o_ref[...] = (y * (1.0 + sc_ref[0]) + sh_ref[0]).astype(o_ref.dtype)


def _norm_mod(x, g, sc, sh, rows_per_batch, tm):
    n, d = x.shape
    per = rows_per_batch // tm
    return pl.pallas_call(
        _norm_mod_kernel,
        grid=(n // tm,),
        in_specs=[pl.BlockSpec((tm, d), lambda i: (i, 0)),
                  pl.BlockSpec((1, d), lambda i: (0, 0)),
                  pl.BlockSpec((1, 1, d), lambda i: (i // per, 0, 0)),
                  pl.BlockSpec((1, 1, d), lambda i: (i // per, 0, 0))],
        out_specs=pl.BlockSpec((tm, d), lambda i: (i, 0)),
        out_shape=jax.ShapeDtypeStruct((n, d), BF16),
        compiler_params=_cparams(("arbitrary",)),
        name="norm_mod",
    )(x, g.reshape(1, d), sc[:, None, :], sh[:, None, :])


def _mm_kernel(x_ref, w_ref, *rest, scale, sigmoid, n_out):
    o_refs, wbf_ref = rest[:n_out], rest[n_out]

    @pl.when(pl.program_id(1) == 0)
    def _():
        wbf_ref[...] = w_ref[...].astype(BF16)

    acc = _dot(x_ref[...], wbf_ref[...])
    if scale != 1.0:
        acc = acc * scale
    if sigmoid:
        acc = _sigmoid(acc)
    for o in o_refs:
        o[...] = acc.astype(o.dtype)


def _mm(x, w, col0, ncols, out_dtypes, *, scale=1.0, sigmoid=False, tm=512, tn=1024):
    m, k = x.shape
    tn = min(tn, ncols)
    tm = min(tm, m)
    assert col0 % tn == 0 and ncols % tn == 0 and m % tm == 0
    jb = col0 // tn
    outs = pl.pallas_call(
        functools.partial(_mm_kernel, scale=scale, sigmoid=sigmoid, n_out=len(out_dtypes)),
        grid=(ncols // tn, m // tm),
        in_specs=[pl.BlockSpec((tm, k), lambda j, i: (i, 0)),
                  pl.BlockSpec((k, tn), lambda j, i: (0, jb + j))],
        out_specs=[pl.BlockSpec((tm, tn), lambda j, i: (i, j)) for _ in out_dtypes],
        out_shape=[jax.ShapeDtypeStruct((m, ncols), dt) for dt in out_dtypes],
        scratch_shapes=[pltpu.VMEM((k, tn), BF16)],
        compiler_params=_cparams(("arbitrary", "arbitrary")),
        name="mm_cols",
    )(x, w)
    return outs


def _rel_bucket(dist):
    max_exact = N_BUCKETS // 2
    d = jnp.maximum(dist, 0)
    large = max_exact + (jnp.log(jnp.maximum(d, 1).astype(jnp.float32) / max_exact)
                         / math.log(MAX_DISTANCE / max_exact)
                         * (N_BUCKETS - max_exact)).astype(jnp.int32)
    large = jnp.minimum(large, N_BUCKETS - 1)
    return jnp.where(d < max_exact, d, large)


def _bias_by_distance(table, dists):
    return table[_rel_bucket(dists)].astype(jnp.float32).T


def _toeplitz_kernel(u_ref, o_ref):
    t = o_ref.shape[2]
    x = jnp.broadcast_to(u_ref[0], (t, 2 * t))
    o_ref[0, 0] = pltpu.roll(x, 0, 1, stride=1, stride_axis=0)[:, :t]


def _near_tiles(table, t):
    nh = table.shape[1]
    k = jnp.arange(2 * t, dtype=jnp.int32)
    gens = []
    for off in (0, t):
        d = jnp.where(k < t, off - k, off + 2 * t - k)
        gens.append(jnp.where(d[None] >= 0, _bias_by_distance(table, d), NEG))
    u = jnp.stack(gens, axis=1).reshape(nh * 2, 1, 2 * t)
    return pl.pallas_call(
        _toeplitz_kernel,
        grid=(nh, 2),
        in_specs=[pl.BlockSpec((1, 1, 2 * t), lambda h, o: (h * 2 + o, 0, 0))],
        out_specs=pl.BlockSpec((1, 1, t, t), lambda h, o: (h, o, 0, 0)),
        out_shape=jax.ShapeDtypeStruct((nh, 2, t, t), jnp.float32),
        compiler_params=_cparams(("arbitrary", "arbitrary")),
        name="bias_tiles",
    )(u)


def _fold_lanes(x, op):
    out = x[:, :LANES]
    for c in range(1, x.shape[1] // LANES):
        out = op(out, x[:, c * LANES:(c + 1) * LANES])
    return out


def _pairwise(n, fn):
    def body(i, carry):
        fn(2 * i)
        fn(2 * i + 1)
        return carry

    lax.fori_loop(0, n // 2, body, 0)

    @pl.when(n % 2 == 1)
    def _():
        fn(n - 1)


def _da_prompt_kernel(cfar_ref, lam_ref, q_ref, k_ref, v_ref, tile_ref, g_ref, o_ref,
                      s_ref, mpart_ref, shift_ref, lpart_ref, acc_ref, *, tq, out_scale):
    h = pl.program_id(1)
    qi = pl.program_id(2)
    q = q_ref[...]
    lane = lax.broadcasted_iota(jnp.int32, q.shape, 1)
    zero = jnp.zeros_like(q)
    q2 = jnp.concatenate([jnp.where(lane < DA_DK, q, zero), jnp.where(lane >= DA_DK, q, zero)], axis=0)
    cfar = cfar_ref[h]
    n_far = jnp.maximum(qi - 1, 0)
    r2 = 2 * tq

    def chunk_rows(kc):
        return pl.ds(pl.multiple_of(kc * tq, tq), tq)

    def scores(kc, bias):
        s = _dot_nt(q2, k_ref[chunk_rows(kc), :])
        if bias is not None:
            s = s + jnp.concatenate([bias, bias], axis=0)
        s_ref[kc] = s
        mpart_ref[...] = jnp.maximum(mpart_ref[...], _fold_lanes(s, jnp.maximum))

    mpart_ref[...] = jnp.full((r2, LANES), NEG, jnp.float32)
    _pairwise(n_far, lambda kc: scores(kc, None))
    m_far = jnp.max(mpart_ref[...], axis=-1, keepdims=True) + cfar
    mpart_ref[...] = jnp.full((r2, LANES), NEG, jnp.float32)

    @pl.when(qi >= 1)
    def _():
        scores(qi - 1, tile_ref[0, 1])

    scores(qi, tile_ref[0, 0])
    m = jnp.maximum(m_far, jnp.max(mpart_ref[...], axis=-1, keepdims=True))
    shift_ref[0] = jnp.broadcast_to(m - cfar, (r2, LANES))
    shift_ref[1] = jnp.broadcast_to(m, (r2, LANES))

    lpart_ref[...] = jnp.zeros((r2, LANES), jnp.float32)
    acc_ref[...] = jnp.zeros((r2, DA_DV), jnp.float32)

    def weights(kc, which):
        s = s_ref[kc]
        sh = shift_ref[which]
        ps = [jnp.exp(s[:, c * LANES:(c + 1) * LANES] - sh) for c in range(tq // LANES)]
        tot = ps[0]
        for pc in ps[1:]:
            tot = tot + pc
        lpart_ref[...] = lpart_ref[...] + tot
        p = jnp.concatenate(ps, axis=1).astype(BF16)
        acc_ref[...] = acc_ref[...] + _dot(p, v_ref[chunk_rows(kc), :])

    _pairwise(n_far, lambda kc: weights(kc, 0))

    @pl.when(qi >= 1)
    def _():
        weights(qi - 1, 1)

    weights(qi, 1)

    lam = lam_ref[0]
    a = acc_ref[...] / jnp.sum(lpart_ref[...], axis=-1, keepdims=True)
    o = a[:tq] - lam * a[tq:]
    o = o * lax.rsqrt(jnp.mean(o * o, axis=-1, keepdims=True) + EPS) * g_ref[...]
    o_ref[...] = (o * out_scale).astype(o_ref.dtype)


def _da_prompt(q, k, v, tiles, cfar, lam, g_subln, nb, t, tq, out_scale):
    n = q.shape[0]
    nq = t // tq
    grid_spec = pltpu.PrefetchScalarGridSpec(
        num_scalar_prefetch=0,
        grid=(nb, DA_HEADS, nq),
        in_specs=[pl.BlockSpec(memory_space=pltpu.SMEM),
                  pl.BlockSpec(memory_space=pltpu.SMEM),
                  pl.BlockSpec((tq, LANES), lambda b, h, i: (b * nq + i, h)),
                  pl.BlockSpec((t, LANES), lambda b, h, i: (b, h)),
                  pl.BlockSpec((t, LANES), lambda b, h, i: (b, h)),
                  pl.BlockSpec((1, 2, tq, tq), lambda b, h, i: (h, 0, 0, 0)),
                  pl.BlockSpec((1, DA_DV), lambda b, h, i: (0, 0))],
        out_specs=pl.BlockSpec((tq, LANES), lambda b, h, i: (b * nq + i, h)),
        scratch_shapes=[pltpu.VMEM((nq, 2 * tq, tq), jnp.float32),
                        pltpu.VMEM((2 * tq, LANES), jnp.float32),
                        pltpu.VMEM((2, 2 * tq, LANES), jnp.float32),
                        pltpu.VMEM((2 * tq, LANES), jnp.float32),
                        pltpu.VMEM((2 * tq, DA_DV), jnp.float32)])
    return pl.pallas_call(
        functools.partial(_da_prompt_kernel, tq=tq, out_scale=out_scale),
        grid_spec=grid_spec,
        out_shape=jax.ShapeDtypeStruct((n, DA_HEADS * DA_DV), BF16),
        compiler_params=_cparams(("arbitrary", "arbitrary", "arbitrary")),
        name="da_prompt",
    )(cfar, lam, q, k, v, tiles, g_subln.reshape(1, DA_DV))


def _count_ge(key_ref, cand, n_chunks, tq):
    def body(kc, acc):
        ge = jnp.where(key_ref[kc] >= cand, 1.0, 0.0)
        for c in range(tq // LANES):
            acc = acc + ge[:, c * LANES:(c + 1) * LANES]
        return acc
    acc = lax.fori_loop(0, n_chunks, body, jnp.zeros((tq, LANES), jnp.float32))
    return jnp.sum(acc, axis=-1, keepdims=True)


def _sa_prompt_kernel(cfar_ref, qs_ref, qx_ref, mq_ref, mk_ref, ks_ref, vs_ref, tile_ref, o_ref,
                      k2_ref, key_ref, cut_ref, s_ref, mpart_ref, shift_ref, lpart_ref, acc_ref,
                      *, tq, topk):
    qi = pl.program_id(1)
    n_chunks = qi + 1
    t = mk_ref.shape[0]

    @pl.when(qi == 0)
    def _():
        kix = mk_ref[:, :LANES].astype(jnp.float32)
        lane = lax.broadcasted_iota(jnp.int32, kix.shape, 1)
        k2_ref[0] = jnp.where(lane < IDX_DK, kix, 0.0).astype(BF16)
        k2_ref[1] = jnp.where(lane >= IDX_DK, pltpu.roll(kix, IDX_DK, axis=1), 0.0).astype(BF16)

    wix = mq_ref[:, IDX_DK:IDX_DK + IDX_HEADS]
    wcols = [wix[:, hh:hh + 1] for hh in range(IDX_HEADS)]
    row = lax.broadcasted_iota(jnp.int32, (tq, tq), 0)
    col = lax.broadcasted_iota(jnp.int32, (tq, tq), 1)

    def score_body(kc, carry):
        rows = pl.ds(pl.multiple_of(kc * tq, tq), tq)
        ke = k2_ref[0, rows, :]
        ko = k2_ref[1, rows, :]
        sc = jnp.zeros((tq, tq), jnp.float32)
        for p in range(IDX_HEADS // 2):
            qp = qx_ref[:, p * LANES:(p + 1) * LANES]
            sc = sc + wcols[2 * p] * jnp.maximum(_dot_nt(qp, ke), 0.0)
            sc = sc + wcols[2 * p + 1] * jnp.maximum(_dot_nt(qp, ko), 0.0)
        sc = sc * (IDX_DK ** -0.5 * IDX_HEADS ** -0.5)
        sc = jnp.where((kc < qi) | (row >= col), sc, -jnp.inf)
        key_ref[kc] = _float_key(sc)
        return carry

    lax.fori_loop(0, n_chunks, score_body, 0)

    def bit_body(it, carry):
        thr, cnt_thr = carry
        cand = thr + (jnp.int32(1) << (31 - it))
        cnt = _count_ge(key_ref, cand, n_chunks, tq)
        ok = cnt >= float(topk)
        return jnp.where(ok, cand, thr), jnp.where(ok, cnt, cnt_thr)

    thr0 = jnp.full((tq, 1), INT_MIN, jnp.int32)
    cnt0 = jnp.full((tq, 1), 3.0e38, jnp.float32)
    thr, cnt_thr = lax.fori_loop(0, 32, bit_body, (thr0, cnt0))
    tied = (cnt_thr > float(topk)) & (thr > jnp.int32(KEY_NEG_INF))
    need_tie = jnp.max(jnp.where(tied, 1.0, 0.0)) > 0.0
    thr = jnp.maximum(thr, jnp.int32(KEY_NEG_INF + 1))

    cut_ref[...] = jnp.full((tq, 1), 2 ** 30, jnp.int32)

    @pl.when(need_tie)
    def _():
        def gt_body(kc, acc):
            g = jnp.where(key_ref[kc] > thr, 1.0, 0.0)
            return acc + jnp.sum(g, axis=-1, keepdims=True)
        n_gt = lax.fori_loop(0, n_chunks, gt_body, jnp.zeros((tq, 1), jnp.float32))
        need = float(topk) - n_gt
        n_bits = max(1, int(t).bit_length())

        def cut_body(it, cut):
            cand = cut + (jnp.int32(1) << (n_bits - 1 - it))

            def eq_body(kc, acc):
                pos = kc * tq + col
                e = jnp.where((key_ref[kc] == thr) & (pos < cand), 1.0, 0.0)
                return acc + jnp.sum(e, axis=-1, keepdims=True)
            n_eq = lax.fori_loop(0, n_chunks, eq_body, jnp.zeros((tq, 1), jnp.float32))
            return jnp.where(n_eq <= need, cand, cut)
        cut_ref[...] = lax.fori_loop(0, n_bits, cut_body, jnp.zeros((tq, 1), jnp.int32))

    cut = cut_ref[...]

    n_far = jnp.maximum(qi - 1, 0)
    rg = SA_GROUP * tq

    def chunk_rows(kc):
        return pl.ds(pl.multiple_of(kc * tq, tq), tq)

    for g in range(SA_KV_HEADS):
        heads = [g * SA_GROUP + j for j in range(SA_GROUP)]
        qg = jnp.concatenate([qs_ref[:, hh * LANES:(hh + 1) * LANES] for hh in heads], axis=0)
        cf_rows = jnp.concatenate([jnp.full((tq, 1), cfar_ref[hh], jnp.float32) for hh in heads], axis=0)

        def scores(kc, kind, qg=qg, heads=heads, g=g):
            key = key_ref[kc]
            sel = (key > thr) | ((key == thr) & (kc * tq + col < cut))
            s_all = _dot_nt(qg, ks_ref[chunk_rows(kc), g * SA_DH:(g + 1) * SA_DH])
            parts = []
            for j, hh in enumerate(heads):
                s = s_all[j * tq:(j + 1) * tq]
                if kind is not None:
                    s = s + tile_ref[hh, kind]
                parts.append(jnp.where(sel, s, NEG))
            s = jnp.concatenate(parts, axis=0)
            s_ref[kc] = s
            mpart_ref[...] = jnp.maximum(mpart_ref[...], _fold_lanes(s, jnp.maximum))

        mpart_ref[...] = jnp.full((rg, LANES), NEG, jnp.float32)
        _pairwise(n_far, lambda kc, f=scores: f(kc, None))
        m_far = jnp.max(mpart_ref[...], axis=-1, keepdims=True) + cf_rows
        mpart_ref[...] = jnp.full((rg, LANES), NEG, jnp.float32)

        @pl.when(qi >= 1)
        def _(f=scores):
            f(qi - 1, 1)

        scores(qi, 0)
        m = jnp.maximum(m_far, jnp.max(mpart_ref[...], axis=-1, keepdims=True))
        shift_ref[0] = jnp.broadcast_to(m - cf_rows, (rg, LANES))
        shift_ref[1] = jnp.broadcast_to(m, (rg, LANES))
        lpart_ref[...] = jnp.zeros((rg, LANES), jnp.float32)
        acc_ref[...] = jnp.zeros((rg, SA_DH), jnp.float32)

        def weights(kc, which, g=g):
            s = s_ref[kc]
            sh = shift_ref[which]
            ps = [jnp.exp(s[:, c * LANES:(c + 1) * LANES] - sh) for c in range(tq // LANES)]
            tot = ps[0]
            for pc in ps[1:]:
                tot = tot + pc
            lpart_ref[...] = lpart_ref[...] + tot
            p = jnp.concatenate(ps, axis=1).astype(BF16)
            acc_ref[...] = acc_ref[...] + _dot(p, vs_ref[chunk_rows(kc), g * SA_DH:(g + 1) * SA_DH])

        _pairwise(n_far, lambda kc, f=weights: f(kc, 0))

        @pl.when(qi >= 1)
        def _(f=weights):
            f(qi - 1, 1)

        weights(qi, 1)
        a = acc_ref[...] / jnp.sum(lpart_ref[...], axis=-1, keepdims=True)
        for j, hh in enumerate(heads):
            o_ref[:, hh * SA_DH:(hh + 1) * SA_DH] = a[j * tq:(j + 1) * tq].astype(o_ref.dtype)


def _sa_prompt(q_sa, q_ix, misc_q, misc_k, k_sa, v_sa, tiles, cfar, nb, t, tq, topk):
    n = q_sa.shape[0]
    nq = t // tq
    mw = misc_q.shape[1]
    kvw = SA_KV_HEADS * SA_DH
    once = dict(pipeline_mode=pl.Buffered(1))
    rg = SA_GROUP * tq
    return pl.pallas_call(
        functools.partial(_sa_prompt_kernel, tq=tq, topk=topk),
        grid=(nb, nq),
        in_specs=[pl.BlockSpec(memory_space=pltpu.SMEM),
                  pl.BlockSpec((tq, SA_HEADS * SA_DH), lambda b, i: (b * nq + i, 0)),
                  pl.BlockSpec((tq, IDX_HEADS * IDX_DK), lambda b, i: (b * nq + i, 0)),
                  pl.BlockSpec((tq, mw), lambda b, i: (b * nq + i, 0)),
                  pl.BlockSpec((t, mw), lambda b, i: (b, 0), **once),
                  pl.BlockSpec((t, kvw), lambda b, i: (b, 0), **once),
                  pl.BlockSpec((t, kvw), lambda b, i: (b, 0), **once),
                  pl.BlockSpec((SA_HEADS, 2, tq, tq), lambda b, i: (0, 0, 0, 0), **once)],
        out_specs=pl.BlockSpec((tq, SA_HEADS * SA_DH), lambda b, i: (b * nq + i, 0)),
        out_shape=jax.ShapeDtypeStruct((n, SA_HEADS * SA_DH), BF16),
        scratch_shapes=[pltpu.VMEM((2, t, LANES), BF16),
                        pltpu.VMEM((nq, tq, tq), jnp.int32),
                        pltpu.VMEM((tq, 1), jnp.int32),
                        pltpu.VMEM((nq, rg, tq), jnp.float32),
                        pltpu.VMEM((rg, LANES), jnp.float32),
                        pltpu.VMEM((2, rg, LANES), jnp.float32),
                        pltpu.VMEM((rg, LANES), jnp.float32),
                        pltpu.VMEM((rg, SA_DH), jnp.float32)],
        compiler_params=_cparams(("arbitrary", "arbitrary")),
        name="sa_prompt",
    )(cfar, q_sa, q_ix, misc_q, misc_k, k_sa, v_sa, tiles)


def _proj_gate_kernel(oda_ref, osa_ref, wpd_ref, wps_ref, gda_ref, gsa_ref, o_ref, wbf_ref):
    @pl.when(pl.program_id(1) == 0)
    def _():
        wbf_ref[0] = wpd_ref[...].astype(BF16)
        wbf_ref[1] = wps_ref[...].astype(BF16)

    a = _dot(oda_ref[...], wbf_ref[0])
    b = _dot(osa_ref[...], wbf_ref[1])
    o_ref[...] = (gda_ref[...].astype(jnp.float32) * a
                  + gsa_ref[...].astype(jnp.float32) * b).astype(o_ref.dtype)


def _proj_gate(o_da, o_sa, w_pd, w_ps, gates, d, tm=512, tn=512):
    n, kd = o_da.shape
    ks = o_sa.shape[1]
    tn = min(tn, d)
    tm = min(tm, n)
    nj = d // tn
    return pl.pallas_call(
        _proj_gate_kernel,
        grid=(nj, n // tm),
        in_specs=[pl.BlockSpec((tm, kd), lambda j, i: (i, 0)),
                  pl.BlockSpec((tm, ks), lambda j, i: (i, 0)),
                  pl.BlockSpec((kd, tn), lambda j, i: (0, j)),
                  pl.BlockSpec((ks, tn), lambda j, i: (0, j)),
                  pl.BlockSpec((tm, tn), lambda j, i: (i, j)),
                  pl.BlockSpec((tm, tn), lambda j, i: (i, nj + j))],
        out_specs=pl.BlockSpec((tm, tn), lambda j, i: (i, j)),
        out_shape=jax.ShapeDtypeStruct((n, d), BF16),
        scratch_shapes=[pltpu.VMEM((2, kd, tn), BF16)],
        compiler_params=_cparams(("arbitrary", "arbitrary")),
        name="proj_gate",
    )(o_da, o_sa, w_pd, w_ps, gates, gates)


def _mm_resid_kernel(m_ref, w_ref, x_ref, gt_ref, o_ref, wbf_ref):
    @pl.when(pl.program_id(1) == 0)
    def _():
        wbf_ref[...] = w_ref[...].astype(BF16)

    o_ref[...] = x_ref[...] + gt_ref[0] * _dot(m_ref[...], wbf_ref[...])


def _mm_resid(m, w, x, gt, rows_per_batch, tm=512, tn=512):
    n, k = m.shape
    d = w.shape[1]
    tn = min(tn, d)
    tm = min(tm, n)
    per = rows_per_batch // tm
    return pl.pallas_call(
        _mm_resid_kernel,
        grid=(d // tn, n // tm),
        in_specs=[pl.BlockSpec((tm, k), lambda j, i: (i, 0)),
                  pl.BlockSpec((k, tn), lambda j, i: (0, j)),
                  pl.BlockSpec((tm, tn), lambda j, i: (i, j)),
                  pl.BlockSpec((1, 1, tn), lambda j, i: (i // per, 0, j))],
        out_specs=pl.BlockSpec((tm, tn), lambda j, i: (i, j)),
        out_shape=jax.ShapeDtypeStruct((n, d), jnp.float32),
        scratch_shapes=[pltpu.VMEM((k, tn), BF16)],
        compiler_params=_cparams(("arbitrary", "arbitrary")),
        name="mm_resid",
    )(m, w, x, gt[:, None, :])


def _norm_router_kernel(x_ref, g_ref, sc_ref, sh_ref, wr_ref, th_ref, tl_ref, h_ref, lg_ref):
    last = pl.num_programs(0) - 1

    @pl.when(pl.program_id(0) < last)
    def _():
        x = x_ref[...]
        y = x * lax.rsqrt(jnp.mean(x * x, axis=-1, keepdims=True) + EPS) * g_ref[...]
        h = y * (1.0 + sc_ref[0]) + sh_ref[0]
        h_ref[...] = h.astype(h_ref.dtype)
        lg_ref[...] = _dot_nt(_bf(wr_ref[...]), _bf(h))

    @pl.when(pl.program_id(0) == last)
    def _():
        h_ref[...] = th_ref[...]
        lg_ref[...] = tl_ref[...]


def _norm_router(x, g, sc, sh, w_router_t, tail_h, tail_lg, rows_per_batch, tm):
    n, d = x.shape
    per = rows_per_batch // tm
    nt = n // tm
    assert tail_h.shape == (tm, d) and tail_lg.shape == (N_EXPERTS, tm)
    row = lambda i: jnp.minimum(i, nt - 1)
    return pl.pallas_call(
        _norm_router_kernel,
        grid=(nt + 1,),
        in_specs=[pl.BlockSpec((tm, d), lambda i: (row(i), 0)),
                  pl.BlockSpec((1, d), lambda i: (0, 0)),
                  pl.BlockSpec((1, 1, d), lambda i: (row(i) // per, 0, 0)),
                  pl.BlockSpec((1, 1, d), lambda i: (row(i) // per, 0, 0)),
                  pl.BlockSpec((N_EXPERTS, d), lambda i: (0, 0)),
                  pl.BlockSpec((tm, d), lambda i: (0, 0)),
                  pl.BlockSpec((N_EXPERTS, tm), lambda i: (0, 0))],
        out_specs=[pl.BlockSpec((tm, d), lambda i: (i, 0)),
                   pl.BlockSpec((N_EXPERTS, tm), lambda i: (0, i))],
        out_shape=[jax.ShapeDtypeStruct((n + tm, d), BF16),
                   jax.ShapeDtypeStruct((N_EXPERTS, n + tm), jnp.float32)],
        compiler_params=_cparams(("arbitrary",)),
        name="norm_router",
    )(x, g.reshape(1, d), sc[:, None, :], sh[:, None, :], w_router_t, tail_h, tail_lg)


def _route_kernel(lg_ref, b_ref, eidx_ref, wts_ref, rank_ref, cnt_ref, *, tn):
    shape = (N_GROUPS, GROUP_SIZE, tn)
    sc = _sigmoid(lg_ref[...])
    biased = sc + b_ref[...]
    e_iota = lax.broadcasted_iota(jnp.int32, shape, 1)
    g_iota3 = lax.broadcasted_iota(jnp.int32, shape, 0)
    flat_iota = g_iota3 * GROUP_SIZE + e_iota
    g_iota = lax.broadcasted_iota(jnp.int32, (N_GROUPS, 1, tn), 0)
    ninf = -jnp.inf

    m1 = jnp.max(biased, axis=1, keepdims=True)
    first = jnp.min(jnp.where(biased == m1, e_iota, GROUP_SIZE), axis=1, keepdims=True)
    m2 = jnp.max(jnp.where(e_iota == first, ninf, biased), axis=1, keepdims=True)
    cur = m1 + m2
    gsel = jnp.zeros((N_GROUPS, 1, tn), jnp.float32)
    for _ in range(TOPK_GROUPS):
        mx = jnp.max(cur, axis=0, keepdims=True)
        idx = jnp.min(jnp.where(cur == mx, g_iota, N_GROUPS), axis=0, keepdims=True)
        hit = g_iota == idx
        gsel = jnp.where(hit, 1.0, gsel)
        cur = jnp.where(hit, ninf, cur)

    cur = jnp.where(gsel > 0.0, biased, ninf)
    hits, ws = [], []
    for k in range(TOP_K):
        mx = jnp.max(jnp.max(cur, axis=1, keepdims=True), axis=0, keepdims=True)
        cand = jnp.where(cur == mx, flat_iota, N_EXPERTS)
        idx = jnp.min(jnp.min(cand, axis=1, keepdims=True), axis=0, keepdims=True)
        hit = flat_iota == idx
        w = jnp.sum(jnp.sum(jnp.where(hit, sc, 0.0), axis=1, keepdims=True), axis=0, keepdims=True)
        eidx_ref[k:k + 1, :] = idx.reshape(1, tn)
        hits.append(hit)
        ws.append(w)
        cur = jnp.where(hit, ninf, cur)
    wsum = ws[0]
    for w in ws[1:]:
        wsum = wsum + w
    for k in range(TOP_K):
        wts_ref[k:k + 1, :] = (ws[k] / wsum * ROUTED_SCALE).reshape(1, tn)

    member = jnp.zeros(shape, jnp.float32)
    for hit in hits:
        member = jnp.where(hit, 1.0, member)
    member2 = member.reshape(N_EXPERTS, tn)
    r = lax.broadcasted_iota(jnp.int32, (tn, tn), 0)
    c = lax.broadcasted_iota(jnp.int32, (tn, tn), 1)
    upper = jnp.where(r < c, 1.0, 0.0).astype(BF16)
    prefix = _dot(member2.astype(BF16), upper).reshape(shape)
    for k in range(TOP_K):
        rk = jnp.sum(jnp.sum(jnp.where(hits[k], prefix, 0.0), axis=1, keepdims=True), axis=0, keepdims=True)
        rank_ref[k:k + 1, :] = rk.reshape(1, tn)
    cnt = jnp.sum(member2, axis=1, keepdims=True)
    cnt_ref[0] = jnp.broadcast_to(cnt, (N_EXPERTS, LANES))


def _route(logits_t, b_router, tn):
    n_pad = logits_t.shape[1]
    nt = n_pad // tn
    lg3 = logits_t.reshape(N_GROUPS, GROUP_SIZE, n_pad)
    b3 = b_router.astype(jnp.float32).reshape(N_GROUPS, GROUP_SIZE, 1)
    row = lambda dt: jax.ShapeDtypeStruct((TOP_K, n_pad), dt)
    return pl.pallas_call(
        functools.partial(_route_kernel, tn=tn),
        grid=(nt,),
        in_specs=[pl.BlockSpec((N_GROUPS, GROUP_SIZE, tn), lambda i: (0, 0, i)),
                  pl.BlockSpec((N_GROUPS, GROUP_SIZE, 1), lambda i: (0, 0, 0))],
        out_specs=[pl.BlockSpec((TOP_K, tn), lambda i: (0, i)),
                   pl.BlockSpec((TOP_K, tn), lambda i: (0, i)),
                   pl.BlockSpec((TOP_K, tn), lambda i: (0, i)),
                   pl.BlockSpec((1, N_EXPERTS, LANES), lambda i: (i, 0, 0))],
        out_shape=[row(jnp.int32), row(jnp.float32), row(jnp.float32),
                   jax.ShapeDtypeStruct((nt, N_EXPERTS, LANES), jnp.float32)],
        compiler_params=_cparams(("arbitrary",)),
        name="route",
    )(lg3, b3)


def _dest_kernel(eidx_ref, rank_ref, base_ref, o_ref, *, tn):
    e_iota = lax.broadcasted_iota(jnp.int32, (N_EXPERTS, tn), 0)
    base = base_ref[0][:, :1]
    for k in range(TOP_K):
        onehot = e_iota == eidx_ref[k:k + 1, :]
        b = jnp.sum(jnp.where(onehot, base, 0.0), axis=0, keepdims=True)
        o_ref[k:k + 1, :] = (b + rank_ref[k:k + 1, :]).astype(jnp.int32)


def _dest(eidx, rank, base, tn):
    n_pad = eidx.shape[1]
    return pl.pallas_call(
        functools.partial(_dest_kernel, tn=tn),
        grid=(n_pad // tn,),
        in_specs=[pl.BlockSpec((TOP_K, tn), lambda i: (0, i)),
                  pl.BlockSpec((TOP_K, tn), lambda i: (0, i)),
                  pl.BlockSpec((1, N_EXPERTS, LANES), lambda i: (i, 0, 0))],
        out_specs=pl.BlockSpec((TOP_K, tn), lambda i: (0, i)),
        out_shape=jax.ShapeDtypeStruct((TOP_K, n_pad), jnp.int32),
        compiler_params=_cparams(("arbitrary",)),
        name="dest",
    )(eidx, rank, base)


def _pack_words(lo_f32, hi_f32):
    lo = lax.shift_right_logical(pltpu.bitcast(lo_f32, jnp.uint32), jnp.uint32(16))
    hi = pltpu.bitcast(hi_f32, jnp.uint32) & jnp.uint32(0xFFFF0000)
    return hi | lo


def _unpack_words(w):
    lo = pltpu.bitcast(lax.shift_left(w, jnp.uint32(16)), jnp.float32)
    hi = pltpu.bitcast(w & jnp.uint32(0xFFFF0000), jnp.float32)
    return lo, hi


def _bf16_exact(x):
    return x.astype(BF16).astype(jnp.float32)


def _dispatch_kernel(pstart_ref, pcnt_ref, h_ref, dest_hbm, xs_hbm, dsm, pk, zrow, sem_d, sem_r,
                     *, tn, nt):
    i = pl.program_id(0)
    half = pk.shape[1]

    @pl.when(i < nt)
    def _():
        cp = pltpu.make_async_copy(dest_hbm.at[pl.ds(i * (TOP_K * tn), TOP_K * tn)], dsm, sem_d)
        cp.start()
        x = h_ref[...]
        pk[...] = _pack_words(x[:, :half].astype(jnp.float32), x[:, half:].astype(jnp.float32))
        cp.wait()

        def body(r, carry):
            for k in range(TOP_K):
                d = dsm[k * tn + r]
                pltpu.make_async_copy(pk.at[pl.ds(r, 1), :], xs_hbm.at[pl.ds(d, 1), :], sem_r).start()
            return carry

        lax.fori_loop(0, tn, body, 0)
        for k in range(TOP_K):
            pltpu.make_async_copy(pk, xs_hbm.at[pl.ds(0, tn), :], sem_r).wait()

    @pl.when(i == nt)
    def _():
        zrow[...] = jnp.zeros(zrow.shape, zrow.dtype)

        def per_expert(e, carry):
            s0 = pstart_ref[e]
            c = pcnt_ref[e]

            def start(r, cc):
                pltpu.make_async_copy(zrow.at[pl.ds(0, 1), :], xs_hbm.at[pl.ds(s0 + r, 1), :], sem_r).start()
                return cc

            def wait(r, cc):
                pltpu.make_async_copy(zrow.at[pl.ds(0, 1), :], xs_hbm.at[pl.ds(s0, 1), :], sem_r).wait()
                return cc

            lax.fori_loop(0, c, start, 0)
            lax.fori_loop(0, c, wait, 0)
            return carry

        lax.fori_loop(0, N_EXPERTS, per_expert, 0)

        t0 = pstart_ref[N_EXPERTS]
        groups = pcnt_ref[N_EXPERTS]
        rows8 = lambda r: pl.ds(pl.multiple_of(t0 + r * 8, 8), 8)

        def tstart(r, cc):
            pltpu.make_async_copy(zrow, xs_hbm.at[rows8(r), :], sem_r).start()
            return cc

        def twait(r, cc):
            pltpu.make_async_copy(zrow, xs_hbm.at[rows8(0), :], sem_r).wait()
            return cc

        lax.fori_loop(0, groups, tstart, 0)
        lax.fori_loop(0, groups, twait, 0)


def _dispatch(h2, dest_flat, pad_start, pad_cnt, n_rows, tn):
    n_pad, d = h2.shape
    nt = n_pad // tn
    grid_spec = pltpu.PrefetchScalarGridSpec(
        num_scalar_prefetch=2,
        grid=(nt + 1,),
        in_specs=[pl.BlockSpec((tn, d), lambda i, a, b: (jnp.minimum(i, nt - 1), 0)),
                  pl.BlockSpec(memory_space=pl.ANY)],
        out_specs=pl.BlockSpec(memory_space=pl.ANY),
        scratch_shapes=[pltpu.SMEM((TOP_K * tn,), jnp.int32),
                        pltpu.VMEM((tn, d // 2), jnp.uint32),
                        pltpu.VMEM((8, d // 2), jnp.uint32),
                        pltpu.SemaphoreType.DMA(()),
                        pltpu.SemaphoreType.DMA(())])
    return pl.pallas_call(
        functools.partial(_dispatch_kernel, tn=tn, nt=nt),
        grid_spec=grid_spec,
        out_shape=jax.ShapeDtypeStruct((n_rows, d // 2), jnp.uint32),
        compiler_params=_cparams(("arbitrary",)),
        name="dispatch",
    )(pad_start, pad_cnt, h2, dest_flat)


def _expert_kernel(blk_e_ref, nused_ref, x_ref, wg_ref, wu_ref, wd_ref, y_ref, wgb, wub, wdb):
    i = pl.program_id(0)
    nused = nused_ref[0]
    ii = jnp.minimum(i, nused - 1)
    e = blk_e_ref[ii]
    e_prev = blk_e_ref[jnp.maximum(ii - 1, 0)]
    half = x_ref.shape[1]

    @pl.when((i < nused) & ((i == 0) | (e != e_prev)))
    def _():
        wgb[...] = wg_ref[0].astype(BF16)
        wub[...] = wu_ref[0].astype(BF16)
        wdb[...] = wd_ref[0].astype(BF16)

    @pl.when(i < nused)
    def _():
        lo, hi = _unpack_words(x_ref[...])
        xl = lo.astype(BF16)
        xh = hi.astype(BF16)
        g = _dot(xl, wgb[:half, :]) + _dot(xh, wgb[half:, :])
        u = _dot(xl, wub[:half, :]) + _dot(xh, wub[half:, :])
        hmid = (_silu(g) * u).astype(BF16)
        y = _dot(hmid, wdb[...])
        y_ref[...] = _pack_words(_bf16_exact(y[:, :half]), _bf16_exact(y[:, half:]))

    @pl.when(i >= nused)
    def _():
        y_ref[...] = jnp.zeros(y_ref.shape, y_ref.dtype)


def _experts(xs, blk_e, nused, w_gate, w_up, w_down, tb):
    n_rows, half = xs.shape
    _, d, f = w_gate.shape
    nblk = n_rows // tb

    def xmap(i, be, nu):
        return (jnp.minimum(i, nu[0] - 1), 0)

    def wmap(i, be, nu):
        return (be[jnp.minimum(i, nu[0] - 1)], 0, 0)

    grid_spec = pltpu.PrefetchScalarGridSpec(
        num_scalar_prefetch=2,
        grid=(nblk,),
        in_specs=[pl.BlockSpec((tb, half), xmap),
                  pl.BlockSpec((1, d, f), wmap),
                  pl.BlockSpec((1, d, f), wmap),
                  pl.BlockSpec((1, f, d), wmap)],
        out_specs=pl.BlockSpec((tb, half), lambda i, be, nu: (i, 0)),
        scratch_shapes=[pltpu.VMEM((d, f), BF16),
                        pltpu.VMEM((d, f), BF16),
                        pltpu.VMEM((f, d), BF16)])
    return pl.pallas_call(
        _expert_kernel,
        grid_spec=grid_spec,
        out_shape=jax.ShapeDtypeStruct((n_rows, half), jnp.uint32),
        compiler_params=_cparams(("arbitrary",)),
        name="experts",
    )(blk_e, nused, xs, w_gate, w_up, w_down)


def _ffn_kernel(h_ref, wg_ref, wu_ref, wd_ref, o_ref, wgb, wub, wdb):
    @pl.when(pl.program_id(0) == 0)
    def _():
        wgb[...] = wg_ref[...].astype(BF16)
        wub[...] = wu_ref[...].astype(BF16)
        wdb[...] = wd_ref[...].astype(BF16)

    x = h_ref[...]
    hmid = (_silu(_dot(x, wgb[...])) * _dot(x, wub[...])).astype(BF16)
    o_ref[...] = _dot(hmid, wdb[...]).astype(o_ref.dtype)


def _ffn_shared(h2, wg, wu, wd, tm):
    n_pad, d = h2.shape
    f = wg.shape[1]
    return pl.pallas_call(
        _ffn_kernel,
        grid=(n_pad // tm,),
        in_specs=[pl.BlockSpec((tm, d), lambda i: (i, 0)),
                  pl.BlockSpec((d, f), lambda i: (0, 0)),
                  pl.BlockSpec((d, f), lambda i: (0, 0)),
                  pl.BlockSpec((f, d), lambda i: (0, 0))],
        out_specs=pl.BlockSpec((tm, d), lambda i: (i, 0)),
        out_shape=jax.ShapeDtypeStruct((n_pad, d), BF16),
        scratch_shapes=[pltpu.VMEM((d, f), BF16),
                        pltpu.VMEM((d, f), BF16),
                        pltpu.VMEM((f, d), BF16)],
        compiler_params=_cparams(("arbitrary",)),
        name="ffn_shared",
    )(h2, wg, wu, wd)


def _combine_kernel(dest_hbm, ys_hbm, wts_ref, sh_ref, x_ref, gt_ref, g_ref, o_ref,
                    dsm, buf, sem_d, sem_r, *, tn, tile0, final_norm):
    i = pl.program_id(0)
    cp = pltpu.make_async_copy(dest_hbm.at[pl.ds((tile0 + i) * (TOP_K * tn), TOP_K * tn)], dsm, sem_d)
    cp.start()
    cp.wait()

    def body(r, carry):
        for k in range(TOP_K):
            d = dsm[k * tn + r]
            pltpu.make_async_copy(ys_hbm.at[pl.ds(d, 1), :], buf.at[k, pl.ds(r, 1), :], sem_r).start()
        return carry

    lax.fori_loop(0, tn, body, 0)
    for k in range(TOP_K):
        pltpu.make_async_copy(ys_hbm.at[pl.ds(0, tn), :], buf.at[k], sem_r).wait()

    half = buf.shape[2]
    wts = wts_ref[...]
    acc_lo = jnp.zeros((tn, half), jnp.float32)
    acc_hi = jnp.zeros((tn, half), jnp.float32)
    for k in range(TOP_K):
        lo, hi = _unpack_words(buf[k])
        wk = wts[:, k:k + 1]
        acc_lo = acc_lo + wk * lo
        acc_hi = acc_hi + wk * hi
    sh = sh_ref[...].astype(jnp.float32)
    gt = gt_ref[0]
    x_lo = x_ref[:, :half] + gt[:, :half] * (acc_lo + sh[:, :half])
    x_hi = x_ref[:, half:] + gt[:, half:] * (acc_hi + sh[:, half:])
    if final_norm:
        ms = (jnp.sum(x_lo * x_lo, axis=-1, keepdims=True)
              + jnp.sum(x_hi * x_hi, axis=-1, keepdims=True)) / (2 * half)
        inv = lax.rsqrt(ms + EPS)
        g = g_ref[...]
        x_lo = x_lo * inv * g[:, :half]
        x_hi = x_hi * inv * g[:, half:]
    o_ref[:, :half] = x_lo
    o_ref[:, half:] = x_hi


def _combine(dest_flat, ys, wts_tok, shared, x1, gt3, g_final, tile0, tn, final_norm):
    rows, d = x1.shape
    nt = rows // tn
    gr = gt3.shape[1]
    per = nt // gt3.shape[0]
    return pl.pallas_call(
        functools.partial(_combine_kernel, tn=tn, tile0=tile0, final_norm=final_norm),
        grid=(nt,),
        in_specs=[pl.BlockSpec(memory_space=pl.ANY),
                  pl.BlockSpec(memory_space=pl.ANY),
                  pl.BlockSpec((tn, TOP_K), lambda i: (tile0 + i, 0)),
                  pl.BlockSpec((tn, d), lambda i: (tile0 + i, 0)),
                  pl.BlockSpec((tn, d), lambda i: (i, 0)),
                  pl.BlockSpec((1, gr, d), lambda i: (i // per, 0, 0)),
                  pl.BlockSpec((1, d), lambda i: (0, 0))],
        out_specs=pl.BlockSpec((tn, d), lambda i: (i, 0)),
        out_shape=jax.ShapeDtypeStruct((rows, d), jnp.float32),
        scratch_shapes=[pltpu.SMEM((TOP_K * tn,), jnp.int32),
                        pltpu.VMEM((TOP_K, tn, d // 2), jnp.uint32),
                        pltpu.SemaphoreType.DMA(()),
                        pltpu.SemaphoreType.DMA(())],
        compiler_params=_cparams(("arbitrary",)),
        name="combine",
    )(dest_flat, ys, wts_tok, shared, x1, gt3, g_final.reshape(1, d))


def _page_specs(shape, n, pg):
    def mk(u):
        return pl.BlockSpec((1,) + shape, lambda b, j, pt: (pt[b, j * pg + u], 0, 0))
    return [mk(u) for u in range(n)]


def _kv_page_specs(kshape, vshape, pg, ns):
    def mk(shape, first):
        def one(u):
            def index(b, j, pt):
                step = jnp.minimum(j, ns - 1) if first else jnp.maximum(j - ns, 0)
                return (pt[b, step * pg + u], 0, 0)
            return pl.BlockSpec((1,) + shape, index)
        return [one(u) for u in range(pg)]
    return mk(kshape, True) + mk(vshape, False)


def _softmax_pages(s_ref):
    s = s_ref[...]
    m = jnp.max(jnp.max(s, axis=0, keepdims=True), axis=2, keepdims=True)
    e = jnp.exp(s - m)
    return e / jnp.sum(jnp.sum(e, axis=0, keepdims=True), axis=2, keepdims=True)


def _da_sample_kernel(pt_ref, lam_ref, q_ref, kn_ref, vn_ref, bl_ref, cf_ref, b0_ref, ex_ref, hm_ref,
                      *rest, pg, n_pages):
    k_refs, v_refs = rest[:pg], rest[pg:2 * pg]
    o_ref, s_ref, a_ref, acc_ref = rest[2 * pg:]
    j = pl.program_id(1)
    ns = n_pages // pg
    r = q_ref.shape[1]
    nh = r // 2
    lane = lax.broadcasted_iota(jnp.int32, (r, PAGE_SIZE), 1)

    @pl.when(j < ns)
    def _():
        q = q_ref[0]
        qb = _bf(q)
        for u in range(pg):
            page = j * pg + u
            s = _dot(qb, _bf(k_refs[u][0]))
            s_ref[page] = s + jnp.where(page == (n_pages - 1), bl_ref[...], cf_ref[...])

        @pl.when(j == 0)
        def _():
            s_new = jnp.sum(_rounded(q) * _rounded(kn_ref[0]), axis=-1, keepdims=True) + b0_ref[...]
            s_ref[n_pages] = jnp.where(lane == 0, s_new, NEG)

    @pl.when(j == ns)
    def _():
        p = _softmax_pages(s_ref)
        a = p[:, :nh, :] - lam_ref[0] * p[:, nh:, :]
        a_ref[...] = _bf(jnp.concatenate([a, jnp.zeros_like(a)], axis=1))
        a_new = a_ref[n_pages][:, 0:1].astype(jnp.float32)
        acc_ref[...] = a_new * _rounded(vn_ref[0])

    @pl.when(j >= ns)
    def _():
        acc = acc_ref[...]
        for u in range(pg):
            page = (j - ns) * pg + u
            pe = _bf(_dot(a_ref[page], ex_ref[...]) * hm_ref[...])
            acc = acc + _dot(pe, _bf(v_refs[u][0]))
        acc_ref[...] = acc

    @pl.when(j == pl.num_programs(1) - 1)
    def _():
        o_ref[0] = acc_ref[...]


def _da_sample(page_table, lam, qbd, knew, vnew, bias_last, cfar, bias0, kt, v2, pg):
    nb, n_pages = page_table.shape
    r, w = qbd.shape[1:]
    rows_v, dv = v2.shape[1:]
    nh = rows_v // PAGE_SIZE
    ns = n_pages // pg
    col = jnp.arange(rows_v, dtype=jnp.int32)
    expand = (col[None, :] // nh == jnp.arange(PAGE_SIZE, dtype=jnp.int32)[:, None]).astype(BF16)
    head_mask = (col[None, :] % nh == jnp.arange(r, dtype=jnp.int32)[:, None]).astype(jnp.float32)
    full = lambda shp: pl.BlockSpec(shp, lambda b, j, pt: (0,) * len(shp))
    per_b = lambda shp: pl.BlockSpec((1,) + shp, lambda b, j, pt: (b, 0, 0))
    grid_spec = pltpu.PrefetchScalarGridSpec(
        num_scalar_prefetch=1,
        grid=(nb, 2 * ns),
        in_specs=[pl.BlockSpec(memory_space=pltpu.SMEM),
                  per_b((r, w)), per_b((1, w)), per_b((r, dv)),
                  full((r, PAGE_SIZE)), full((r, 1)), full((r, 1)),
                  full((PAGE_SIZE, rows_v)), full((r, rows_v))]
                 + _kv_page_specs((w, PAGE_SIZE), (rows_v, dv), pg, ns),
        out_specs=per_b((r, dv)),
        scratch_shapes=[pltpu.VMEM((n_pages + 1, r, PAGE_SIZE), jnp.float32),
                        pltpu.VMEM((n_pages + 1, r, PAGE_SIZE), BF16),
                        pltpu.VMEM((r, dv), jnp.float32)])
    return pl.pallas_call(
        functools.partial(_da_sample_kernel, pg=pg, n_pages=n_pages),
        grid_spec=grid_spec,
        out_shape=jax.ShapeDtypeStruct((nb, r, dv), jnp.float32),
        compiler_params=_cparams(("arbitrary", "arbitrary")),
        name="da_sample",
    )(page_table, lam, qbd, knew, vnew, bias_last, cfar, bias0, expand, head_mask,
      *([kt] * pg), *([v2] * pg))


def _idx_sample_kernel(pt_ref, q_ref, w_ref, kn_ref, *rest, pg, n_pages, topk):
    k_refs = rest[:pg]
    sel_ref, sc_ref = rest[pg:]
    j = pl.program_id(1)
    q = q_ref[0]
    w = _rounded(w_ref[0])
    scale = IDX_DK ** -0.5 * IDX_HEADS ** -0.5
    rows = sc_ref.shape[0]
    lane = lax.broadcasted_iota(jnp.int32, (1, PAGE_SIZE), 1)

    @pl.when(j == 0)
    def _():
        sc_ref[...] = jnp.full(sc_ref.shape, -jnp.inf, jnp.float32)
        d = jnp.maximum(jnp.sum(_rounded(q) * _rounded(kn_ref[0]), axis=-1, keepdims=True), 0.0)
        s_new = jnp.sum(w * _rounded(d), axis=0, keepdims=True) * scale
        sc_ref[n_pages:n_pages + 1, :] = jnp.where(lane == 0, s_new, -jnp.inf)

    qb = _bf(q)
    for u in range(pg):
        d = _rounded(jnp.maximum(_dot(qb, _bf(k_refs[u][0])), 0.0))
        sc_ref[pl.ds(j * pg + u, 1), :] = jnp.sum(w * d, axis=0, keepdims=True) * scale

    @pl.when(j == pl.num_programs(1) - 1)
    def _():
        key = _float_key(sc_ref[...])
        pos = (lax.broadcasted_iota(jnp.int32, key.shape, 0) * PAGE_SIZE
               + lax.broadcasted_iota(jnp.int32, key.shape, 1))

        def bit_body(it, thr):
            cand = thr + (jnp.int32(1) << (31 - it))
            cnt = jnp.sum(jnp.where(key >= cand, 1.0, 0.0))
            return jnp.where(cnt >= float(topk), cand, thr)

        thr = lax.fori_loop(0, 32, bit_body, jnp.int32(INT_MIN))
        thr = jnp.maximum(thr, jnp.int32(KEY_NEG_INF + 1))
        need = float(topk) - jnp.sum(jnp.where(key > thr, 1.0, 0.0))
        n_bits = max(1, int(rows * PAGE_SIZE).bit_length())

        def cut_body(it, cut):
            cand = cut + (jnp.int32(1) << (n_bits - 1 - it))
            n_eq = jnp.sum(jnp.where((key == thr) & (pos < cand), 1.0, 0.0))
            return jnp.where(n_eq <= need, cand, cut)

        cut = lax.fori_loop(0, n_bits, cut_body, jnp.int32(0))
        sel_ref[0] = jnp.where((key > thr) | ((key == thr) & (pos < cut)), 1.0, 0.0)


def _idx_sample(page_table, qix, wix, knew, kc, pg, topk):
    nb, n_pages = page_table.shape
    rows = -(-(n_pages + 1) // 8) * 8
    per_b = lambda shp: pl.BlockSpec((1,) + shp, lambda b, j, pt: (b, 0, 0))
    grid_spec = pltpu.PrefetchScalarGridSpec(
        num_scalar_prefetch=1,
        grid=(nb, n_pages // pg),
        in_specs=[per_b((IDX_HEADS, IDX_DK)), per_b((IDX_HEADS, 1)), per_b((1, IDX_DK))]
                 + _page_specs((IDX_DK, PAGE_SIZE), pg, pg),
        out_specs=per_b((rows, PAGE_SIZE)),
        scratch_shapes=[pltpu.VMEM((rows, PAGE_SIZE), jnp.float32)])
    return pl.pallas_call(
        functools.partial(_idx_sample_kernel, pg=pg, n_pages=n_pages, topk=topk),
        grid_spec=grid_spec,
        out_shape=jax.ShapeDtypeStruct((nb, rows, PAGE_SIZE), jnp.float32),
        compiler_params=_cparams(("arbitrary", "arbitrary")),
        name="idx_sample",
    )(page_table, qix, wix, knew, *([kc] * pg))


def _sa_sample_kernel(pt_ref, q_ref, kn_ref, vn_ref, sel_ref, bl_ref, cf_ref, b0_ref, gm_ref, *rest,
                      pg, n_pages):
    k_refs, v_refs = rest[:pg], rest[pg:2 * pg]
    o_ref, s_ref, p_ref, acc_ref = rest[2 * pg:]
    j = pl.program_id(1)
    ns = n_pages // pg
    scale = SA_DH ** -0.5
    r, cols = gm_ref.shape
    lane = lax.broadcasted_iota(jnp.int32, (r, cols), 1)

    @pl.when(j < ns)
    def _():
        q = q_ref[0]
        qb = _bf(q)
        for u in range(pg):
            page = j * pg + u
            s = _dot_nt(qb, _bf(k_refs[u][0])) * scale
            s = s + jnp.where(page == (n_pages - 1), bl_ref[...], cf_ref[...])
            keep = (sel_ref[0, pl.ds(page, 1), :] > 0.0) & (gm_ref[...] > 0.0)
            s_ref[page] = jnp.where(keep, s, NEG)

        @pl.when(j == 0)
        def _():
            on = sel_ref[0, n_pages:n_pages + 1, 0:1] > 0.0
            s_new = jnp.sum(_rounded(q) * _rounded(kn_ref[0]), axis=-1, keepdims=True) * scale + b0_ref[...]
            s_ref[n_pages] = jnp.where((lane == 0) & on, s_new, NEG)

    @pl.when(j == ns)
    def _():
        p_ref[...] = _bf(_softmax_pages(s_ref))
        acc_ref[...] = p_ref[n_pages][:, 0:1].astype(jnp.float32) * _rounded(vn_ref[0])

    @pl.when(j >= ns)
    def _():
        acc = acc_ref[...]
        for u in range(pg):
            acc = acc + _dot(p_ref[(j - ns) * pg + u], _bf(v_refs[u][0]))
        acc_ref[...] = acc

    @pl.when(j == pl.num_programs(1) - 1)
    def _():
        o_ref[0] = acc_ref[...]


def _sa_sample(page_table, q, knew, vnew, sel2, bias_last2, cfar, bias0, group_mask, k2, v2, pg):
    nb, n_pages = page_table.shape
    r, dh = q.shape[1:]
    srows, cols = sel2.shape[1:]
    ns = n_pages // pg
    full = lambda shp: pl.BlockSpec(shp, lambda b, j, pt: (0,) * len(shp))
    per_b = lambda shp: pl.BlockSpec((1,) + shp, lambda b, j, pt: (b, 0, 0))
    grid_spec = pltpu.PrefetchScalarGridSpec(
        num_scalar_prefetch=1,
        grid=(nb, 2 * ns),
        in_specs=[per_b((r, dh)), per_b((r, dh)), per_b((r, dh)), per_b((srows, cols)),
                  full((r, cols)), full((r, 1)), full((r, 1)), full((r, cols))]
                 + _kv_page_specs((cols, dh), (cols, dh), pg, ns),
        out_specs=per_b((r, dh)),
        scratch_shapes=[pltpu.VMEM((n_pages + 1, r, cols), jnp.float32),
                        pltpu.VMEM((n_pages + 1, r, cols), BF16),
                        pltpu.VMEM((r, dh), jnp.float32)])
    return pl.pallas_call(
        functools.partial(_sa_sample_kernel, pg=pg, n_pages=n_pages),
        grid_spec=grid_spec,
        out_shape=jax.ShapeDtypeStruct((nb, r, dh), jnp.float32),
        compiler_params=_cparams(("arbitrary", "arbitrary")),
        name="sa_sample",
    )(page_table, q, knew, vnew, sel2, bias_last2, cfar, bias0, group_mask, *([k2] * pg), *([v2] * pg))


ROW_TILE = 256
COMBINE_TILE = 128
EXPERT_BLOCK = 256
TAIL_ROWS = 256
PAGES_PER_STEP_DA = 8
PAGES_PER_STEP_SA = 16
PAGES_PER_STEP_IDX = 32


def _rms_rows(x, g):
    return x * lax.rsqrt(jnp.mean(x * x, axis=-1, keepdims=True) + EPS) * g


def _pad_rows(x, rows):
    return jnp.pad(x, ((0, rows - x.shape[0]), (0, 0)))


def _tile_major(dest, tn):
    k, n = dest.shape
    return dest.reshape(k, n // tn, tn).transpose(1, 0, 2).reshape(-1)


def kernel(x_prompt, x_sample, c_prompt, c_sample, cache_da_k, cache_da_v, cache_sa_k, cache_sa_v, cache_idx_k, page_table, rel_bias_table, w_ada, b_ada, g_attn, g_ffn, w_in, lambda_q1, lambda_k1, lambda_q2, lambda_k2, g_subln, w_proj_da, w_proj_sa, w_out, w_router, b_router, w_gate, w_up, w_down, w_sh_gate, w_sh_up, w_sh_down, g_final):
    f32, bf16 = jnp.float32, BF16
    nb, t, d = x_prompt.shape
    nbs, dec_seq, _ = x_sample.shape
    assert dec_seq == 1
    depth = w_in.shape[0]
    n_pages = page_table.shape[1]
    past_len = n_pages * PAGE_SIZE
    n = nb * t
    n_pad = n + TAIL_ROWS
    tq = min(256, t)
    assert tq >= MAX_DISTANCE and t % tq == 0 and n % ROW_TILE == 0 and nbs <= TAIL_ROWS
    assert TAIL_ROWS == ROW_TILE and EXPERT_BLOCK % 8 == 0
    topk_p = min(SA_TOPK_MAX, t // 4)
    topk_s = min(SA_TOPK_MAX, (past_len + dec_seq) // 4)
    assert topk_p <= tq

    sizes = [DA_HEADS * 2 * DA_DK, DA_HEADS * 2 * DA_DK, DA_HEADS * DA_DV, SA_HEADS * SA_DH,
             SA_KV_HEADS * SA_DH, SA_KV_HEADS * SA_DH, IDX_HEADS * IDX_DK, IDX_DK, IDX_HEADS, d, d]
    offs = [sum(sizes[:i]) for i in range(len(sizes) + 1)]
    (o_qda, o_kda, o_vda, o_qsa, o_ksa, o_vsa, o_qix, o_kix, o_wix, o_gda, o_gsa, _) = offs
    misc_w = 2 * LANES

    table = rel_bias_table.astype(f32)
    cfar = table[N_BUCKETS - 1]
    tiles_da = _near_tiles(table[:, :DA_HEADS], tq)
    tiles_sa = _near_tiles(table[:, DA_HEADS:], tq)
    last_dist = past_len - ((n_pages - 1) * PAGE_SIZE + jnp.arange(PAGE_SIZE, dtype=jnp.int32))
    bias_last = _bias_by_distance(table, last_dist)
    bias0 = _bias_by_distance(table, jnp.zeros((1,), jnp.int32))
    rep2 = lambda a: jnp.repeat(a, 2, axis=0)

    xp = x_prompt.reshape(n, d)
    xs = x_sample.reshape(nbs, d)
    c_all = jnp.concatenate([c_prompt, c_sample], axis=0)
    leaves_p, leaves_s = [], []
    for l in range(depth):
        lam_init = 0.8 - 0.6 * math.exp(-0.3 * l)
        lam = (jnp.exp(jnp.sum(lambda_q1[l].astype(f32) * lambda_k1[l].astype(f32)))
               - jnp.exp(jnp.sum(lambda_q2[l].astype(f32) * lambda_k2[l].astype(f32))) + lam_init)
        mod = _linear_small(c_all, w_ada[l], b_ada[l], silu_in=True)
        sh1, sc1, gt1, sh2, sc2, gt2 = jnp.split(mod[:nb], 6, axis=-1)
        sh1s, sc1s, gt1s, sh2s, sc2s, gt2s = jnp.split(mod[nb:], 6, axis=-1)
        w = w_in[l]

        h = _norm_mod(xp, g_attn[l], sc1, sh1, t, ROW_TILE)
        q_da, = _mm(h, w, o_qda, sizes[0], (bf16,), scale=DA_DK ** -0.5)
        k_da, k_da_b = _mm(h, w, o_kda, sizes[1], (f32, bf16))
        v_da, v_da_b = _mm(h, w, o_vda, sizes[2], (f32, bf16))
        q_sa, = _mm(h, w, o_qsa, sizes[3], (bf16,), scale=SA_DH ** -0.5)
        k_sa, k_sa_b = _mm(h, w, o_ksa, sizes[4], (f32, bf16))
        v_sa, v_sa_b = _mm(h, w, o_vsa, sizes[5], (f32, bf16))
        q_ix, = _mm(h, w, o_qix, sizes[6], (bf16,), tn=512)
        misc, misc_b = _mm(h, w, o_kix, misc_w, (f32, bf16))
        gates, = _mm(h, w[:, o_gda:], 0, 2 * d, (bf16,), sigmoid=True)
        k_ix = misc[:, :IDX_DK]

        o_da = _da_prompt(q_da, k_da_b, v_da_b, tiles_da, cfar[:DA_HEADS], lam.reshape(1),
                          g_subln[l].astype(f32), nb, t, tq, 1.0 - lam_init)
        o_sa = _sa_prompt(q_sa, q_ix, misc, misc_b, k_sa_b, v_sa_b, tiles_sa, cfar[DA_HEADS:], nb, t, tq, topk_p)
        m = _proj_gate(o_da, o_sa, w_proj_da[l], w_proj_sa[l], gates, d)
        x1 = _mm_resid(m, w_out[l], xp, gt1, t)

        hs = _rms_rows(xs, g_attn[l].astype(f32)) * (1.0 + sc1s) + sh1s
        ps = _linear_small(hs, w)
        seg = lambda i: ps[:, offs[i]:offs[i + 1]]
        q_da_s, k_da_s, v_da_s, q_sa_s, k_sa_s, v_sa_s, q_ix_s, k_ix_s, w_ix_s, gda_s, gsa_s = [
            seg(i) for i in range(11)]

        r_da = 2 * DA_HEADS
        q16 = (q_da_s * DA_DK ** -0.5).reshape(nbs, DA_HEADS, 2, DA_DK).transpose(0, 2, 1, 3).reshape(nbs, r_da, DA_DK)
        blk_of_row = 2 * (jnp.arange(r_da) % DA_HEADS) + jnp.arange(r_da) // DA_HEADS
        place = (blk_of_row[:, None] == jnp.arange(r_da)[None, :]).astype(f32)
        qbd_da = (place[None, :, :, None] * q16[:, :, None, :]).reshape(nbs, r_da, -1)
        both = lambda a: jnp.concatenate([a, a], axis=0)
        kt_da = cache_da_k[l].transpose(0, 2, 3, 4, 1).reshape(-1, DA_HEADS * 2 * DA_DK, PAGE_SIZE)
        v2_da = cache_da_v[l].reshape(-1, PAGE_SIZE * DA_HEADS, DA_DV)
        v_rows = jnp.pad(v_da_s.reshape(nbs, DA_HEADS, DA_DV), ((0, 0), (0, r_da - DA_HEADS), (0, 0)))
        a_da = _da_sample(page_table, lam.reshape(1), qbd_da, k_da_s[:, None, :], v_rows,
                          both(bias_last[:DA_HEADS]), both(cfar[:DA_HEADS, None]), both(bias0[:DA_HEADS]),
                          kt_da, v2_da, math.gcd(PAGES_PER_STEP_DA, n_pages))
        o_da_s = a_da[:, :DA_HEADS]
        o_da_s = _rms_rows(o_da_s, g_subln[l].astype(f32)) * (1.0 - lam_init)

        sel = _idx_sample(page_table, q_ix_s.reshape(nbs, IDX_HEADS, IDX_DK), w_ix_s[:, :, None],
                          k_ix_s[:, None, :], cache_idx_k[l].transpose(0, 2, 1),
                          math.gcd(PAGES_PER_STEP_IDX, n_pages), topk_s)
        r_sa = 2 * SA_HEADS
        kv_of_row = jnp.minimum(jnp.arange(r_sa) // SA_GROUP, SA_KV_HEADS - 1)
        pad_sa = lambda a: jnp.pad(a, ((0, 0), (0, r_sa - SA_HEADS), (0, 0)))
        rep_kv = lambda a: jnp.repeat(a, SA_KV_HEADS, axis=-1)
        col_kv = jnp.arange(PAGE_SIZE * SA_KV_HEADS) % SA_KV_HEADS
        k2_sa = cache_sa_k[l].reshape(-1, PAGE_SIZE * SA_KV_HEADS, SA_DH)
        v2_sa = cache_sa_v[l].reshape(-1, PAGE_SIZE * SA_KV_HEADS, SA_DH)
        a_sa = _sa_sample(page_table, pad_sa(q_sa_s.reshape(nbs, SA_HEADS, SA_DH)),
                          k_sa_s.reshape(nbs, SA_KV_HEADS, SA_DH)[:, kv_of_row],
                          v_sa_s.reshape(nbs, SA_KV_HEADS, SA_DH)[:, kv_of_row],
                          rep_kv(sel), rep_kv(_pad_rows(bias_last[DA_HEADS:], r_sa)),
                          _pad_rows(cfar[DA_HEADS:, None], r_sa), _pad_rows(bias0[DA_HEADS:], r_sa),
                          (col_kv[None, :] == kv_of_row[:, None]).astype(f32),
                          k2_sa, v2_sa, math.gcd(PAGES_PER_STEP_SA, n_pages))
        o_sa_s = a_sa[:, :SA_HEADS]

        pda = _linear_small(o_da_s.reshape(nbs, -1), w_proj_da[l])
        psa = _linear_small(o_sa_s.reshape(nbs, -1), w_proj_sa[l])
        ms = _sigmoid(gda_s) * pda + _sigmoid(gsa_s) * psa
        x1s = xs + gt1s * _linear_small(ms, w_out[l])
        h2s = _rms_rows(x1s, g_ffn[l].astype(f32)) * (1.0 + sc2s) + sh2s
        lg_s = _linear_small(h2s, w_router[l])

        h2_all, lg_all = _norm_router(x1, g_ffn[l], sc2, sh2, w_router[l].T,
                                      _pad_rows(h2s.astype(bf16), TAIL_ROWS),
                                      _pad_rows(lg_s, TAIL_ROWS).T, t, ROW_TILE)
        eidx, wts, rank, cnt = _route(lg_all, b_router[l], ROW_TILE)
        cnt_tile = cnt[:, :, 0]
        total = jnp.sum(cnt_tile, axis=0)
        padded = jnp.ceil(total / EXPERT_BLOCK) * EXPERT_BLOCK
        pends = jnp.cumsum(padded)
        pstart = pends - padded
        base = pstart[None, :] + jnp.cumsum(cnt_tile, axis=0) - cnt_tile
        dest = _dest(eidx, rank, jnp.broadcast_to(base[:, :, None], base.shape + (LANES,)), ROW_TILE)
        n_blk = -(-(n_pad * TOP_K) // EXPERT_BLOCK) + N_EXPERTS
        blk_start = (jnp.arange(n_blk) * EXPERT_BLOCK).astype(f32)
        blk_e = jnp.minimum(jnp.sum(pends[None, :] <= blk_start[:, None], axis=1), N_EXPERTS - 1).astype(jnp.int32)
        n_used = (pends[-1] / EXPERT_BLOCK).astype(jnp.int32).reshape(1)
        n_rows = n_blk * EXPERT_BLOCK
        pad_start = jnp.concatenate([pstart + total, pends[-1:]]).astype(jnp.int32)
        pad_cnt = jnp.concatenate([padded - total, (n_rows - pends[-1:]) / 8]).astype(jnp.int32)
        xs_sorted = _dispatch(h2_all, _tile_major(dest, ROW_TILE), pad_start, pad_cnt, n_rows, ROW_TILE)
        ys = _experts(xs_sorted, blk_e, n_used, w_gate[l], w_up[l], w_down[l], EXPERT_BLOCK)
        shared = _ffn_shared(h2_all, w_sh_gate[l], w_sh_up[l], w_sh_down[l], ROW_TILE)
        dest_c = _tile_major(dest, COMBINE_TILE)
        wts_tok = wts.T
        last = l == depth - 1
        xp = _combine(dest_c, ys, wts_tok, shared, x1, gt2[:, None, :], g_final.astype(f32),
                      0, COMBINE_TILE, last)
        tail = _combine(dest_c, ys, wts_tok, shared, _pad_rows(x1s, TAIL_ROWS),
                        _pad_rows(gt2s, TAIL_ROWS).reshape(-1, COMBINE_TILE, d), g_final.astype(f32),
                        n // COMBINE_TILE, COMBINE_TILE, last)
        xs = tail[:nbs]
        leaves_p.append((k_da, v_da, k_sa, v_sa, k_ix))
        leaves_s.append((k_da_s, v_da_s, k_sa_s, v_sa_s, k_ix_s))

    shapes = [(DA_HEADS, 2, DA_DK), (DA_HEADS, DA_DV), (SA_KV_HEADS, SA_DH), (SA_KV_HEADS, SA_DH), (IDX_DK,)]
    out_p = [jnp.stack([lv[i].reshape((nb, t) + shapes[i]) for lv in leaves_p]) for i in range(5)]
    out_s = [jnp.stack([lv[i].reshape((nbs, dec_seq) + shapes[i]) for lv in leaves_s]) for i in range(5)]
    return (xp.reshape(nb, t, d), xs.reshape(nbs, dec_seq, d), *out_p, *out_s)
```

```python
import functools
import math

import jax
import jax.numpy as jnp
from jax import lax
from jax.experimental import pallas as pl
from jax.experimental.pallas import tpu as pltpu

DA_HEADS = 8
DA_DK = 64
DA_DV = 2 * DA_DK
SA_HEADS = 8
SA_KV_HEADS = 2
SA_DH = 128
SA_GROUP = SA_HEADS // SA_KV_HEADS
IDX_HEADS = 16
IDX_DK = 64
SA_TOPK_MAX = 256
N_BUCKETS = 32
MAX_DISTANCE = 128
N_EXPERTS = 64
N_GROUPS = 8
GROUP_SIZE = N_EXPERTS // N_GROUPS
TOPK_GROUPS = 4
TOP_K = 8
ROUTED_SCALE = 2.5
PAGE_SIZE = 128
EPS = 1e-6

LANES = 128
VMEM_LIMIT = 56 * 1024 * 1024

BF16 = jnp.bfloat16
NEG = -1e30
INT_MIN = -(2 ** 31)
KEY_NEG_INF = (0xFF800000 ^ 0x7FFFFFFF) - (1 << 32)

_NT = (((1,), (1,)), ((), ()))


def _cparams(sem):
    return pltpu.CompilerParams(dimension_semantics=sem, vmem_limit_bytes=VMEM_LIMIT)


def _dot(a, b):
    return jnp.dot(a, b, preferred_element_type=jnp.float32)


def _dot_nt(a, b):
    return lax.dot_general(a, b, _NT, preferred_element_type=jnp.float32)


def _bf(x):
    return x.astype(BF16)


def _rounded(x):
    return x.astype(BF16).astype(jnp.float32)


def _sigmoid(x):
    return 1.0 / (1.0 + jnp.exp(-x))


def _silu(x):
    return x * _sigmoid(x)


def _float_key(s):
    b = pltpu.bitcast(s, jnp.int32)
    return b ^ ((b >> 31) & jnp.int32(0x7FFFFFFF))


def _linear_small_kernel(x_ref, w_ref, b_ref, o_ref, *, silu_in):
    x = x_ref[...]
    if silu_in:
        x = _silu(x)
    o_ref[...] = _dot(_bf(x), _bf(w_ref[...])) + b_ref[...]


def _linear_small(x, w, b=None, *, silu_in=False, tn=512):
    m0, k = x.shape
    m = -(-m0 // 16) * 16
    x = jnp.pad(x, ((0, m - m0), (0, 0)))
    n = w.shape[1]
    tn = min(tn, n)
    if b is None:
        b = jnp.zeros((1, n), jnp.float32)
    out = pl.pallas_call(
        functools.partial(_linear_small_kernel, silu_in=silu_in),
        grid=(pl.cdiv(n, tn),),
        in_specs=[pl.BlockSpec((m, k), lambda j: (0, 0)),
                  pl.BlockSpec((k, tn), lambda j: (0, j)),
                  pl.BlockSpec((1, tn), lambda j: (0, j))],
        out_specs=pl.BlockSpec((m, tn), lambda j: (0, j)),
        out_shape=jax.ShapeDtypeStruct((m, n), jnp.float32),
        compiler_params=_cparams(("arbitrary",)),
        name="linear_small",
    )(x, w, b.reshape(1, n))
    return out[:m0]


def _norm_mod_kernel(x_ref, g_ref, sc_ref, sh_ref, o_ref):
    x = x_ref[...]
    y = x * lax.rsqrt(jnp.mean(x * x, axis=-1, keepdims=True) + EPS) * g_ref[...]
    o_ref[...] = (y * (1.0 + sc_ref[0]) + sh_ref[0]).astype(o_ref.dtype)


def _norm_mod(x, g, sc, sh, rows_per_batch, tm):
    n, d = x.shape
    per = rows_per_batch // tm
    return pl.pallas_call(
        _norm_mod_kernel,
        grid=(n // tm,),
        in_specs=[pl.BlockSpec((tm, d), lambda i: (i, 0)),
                  pl.BlockSpec((1, d), lambda i: (0, 0)),
                  pl.BlockSpec((1, 1, d), lambda i: (i // per, 0, 0)),
                  pl.BlockSpec((1, 1, d), lambda i: (i // per, 0, 0))],
        out_specs=pl.BlockSpec((tm, d), lambda i: (i, 0)),
        out_shape=jax.ShapeDtypeStruct((n, d), BF16),
        compiler_params=_cparams(("arbitrary",)),
        name="norm_mod",
    )(x, g.reshape(1, d), sc[:, None, :], sh[:, None, :])


def _mm_kernel(x_ref, w_ref, *rest, scale, sigmoid, n_out):
    o_refs, wbf_ref = rest[:n_out], rest[n_out]

    @pl.when(pl.program_id(1) == 0)
    def _():
        wbf_ref[...] = w_ref[...].astype(BF16)

    acc = _dot(x_ref[...], wbf_ref[...])
    if scale != 1.0:
        acc = acc * scale
    if sigmoid:
        acc = _sigmoid(acc)
    for o in o_refs:
        o[...] = acc.astype(o.dtype)


def _mm(x, w, col0, ncols, out_dtypes, *, scale=1.0, sigmoid=False, tm=512, tn=1024):
    m, k = x.shape
    tn = min(tn, ncols)
    tm = min(tm, m)
    assert col0 % tn == 0 and ncols % tn == 0 and m % tm == 0
    jb = col0 // tn
    outs = pl.pallas_call(
        functools.partial(_mm_kernel, scale=scale, sigmoid=sigmoid, n_out=len(out_dtypes)),
        grid=(ncols // tn, m // tm),
        in_specs=[pl.BlockSpec((tm, k), lambda j, i: (i, 0)),
                  pl.BlockSpec((k, tn), lambda j, i: (0, jb + j))],
        out_specs=[pl.BlockSpec((tm, tn), lambda j, i: (i, j)) for _ in out_dtypes],
        out_shape=[jax.ShapeDtypeStruct((m, ncols), dt) for dt in out_dtypes],
        scratch_shapes=[pltpu.VMEM((k, tn), BF16)],
        compiler_params=_cparams(("arbitrary", "arbitrary")),
        name="mm_cols",
    )(x, w)
    return outs


def _rel_bucket(dist):
    max_exact = N_BUCKETS // 2
    d = jnp.maximum(dist, 0)
    large = max_exact + (jnp.log(jnp.maximum(d, 1).astype(jnp.float32) / max_exact)
                         / math.log(MAX_DISTANCE / max_exact)
                         * (N_BUCKETS - max_exact)).astype(jnp.int32)
    large = jnp.minimum(large, N_BUCKETS - 1)
    return jnp.where(d < max_exact, d, large)


def _bias_by_distance(table, dists):
    return table[_rel_bucket(dists)].astype(jnp.float32).T


def _toeplitz_kernel(u_ref, o_ref):
    t = o_ref.shape[2]
    x = jnp.broadcast_to(u_ref[0], (t, 2 * t))
    o_ref[0, 0] = pltpu.roll(x, 0, 1, stride=1, stride_axis=0)[:, :t]


def _near_tiles(table, t):
    nh = table.shape[1]
    k = jnp.arange(2 * t, dtype=jnp.int32)
    gens = []
    for off in (0, t):
        d = jnp.where(k < t, off - k, off + 2 * t - k)
        gens.append(jnp.where(d[None] >= 0, _bias_by_distance(table, d), NEG))
    u = jnp.stack(gens, axis=1).reshape(nh * 2, 1, 2 * t)
    return pl.pallas_call(
        _toeplitz_kernel,
        grid=(nh, 2),
        in_specs=[pl.BlockSpec((1, 1, 2 * t), lambda h, o: (h * 2 + o, 0, 0))],
        out_specs=pl.BlockSpec((1, 1, t, t), lambda h, o: (h, o, 0, 0)),
        out_shape=jax.ShapeDtypeStruct((nh, 2, t, t), jnp.float32),
        compiler_params=_cparams(("arbitrary", "arbitrary")),
        name="bias_tiles",
    )(u)


def _fold_lanes(x, op):
    out = x[:, :LANES]
    for c in range(1, x.shape[1] // LANES):
        out = op(out, x[:, c * LANES:(c + 1) * LANES])
    return out


def _pairwise(n, fn):
    def body(i, carry):
        fn(2 * i)
        fn(2 * i + 1)
        return carry

    lax.fori_loop(0, n // 2, body, 0)

    @pl.when(n % 2 == 1)
    def _():
        fn(n - 1)


def _da_prompt_kernel(cfar_ref, lam_ref, q_ref, k_ref, v_ref, tile_ref, g_ref, o_ref,
                      s_ref, mpart_ref, shift_ref, lpart_ref, acc_ref, *, tq, out_scale):
    h = pl.program_id(1)
    qi = pl.program_id(2)
    q = q_ref[...]
    lane = lax.broadcasted_iota(jnp.int32, q.shape, 1)
    zero = jnp.zeros_like(q)
    q2 = jnp.concatenate([jnp.where(lane < DA_DK, q, zero), jnp.where(lane >= DA_DK, q, zero)], axis=0)
    cfar = cfar_ref[h]
    n_far = jnp.maximum(qi - 1, 0)
    r2 = 2 * tq

    def chunk_rows(kc):
        return pl.ds(pl.multiple_of(kc * tq, tq), tq)

    def scores(kc, bias):
        s = _dot_nt(q2, k_ref[chunk_rows(kc), :])
        if bias is not None:
            s = s + jnp.concatenate([bias, bias], axis=0)
        s_ref[kc] = s
        mpart_ref[...] = jnp.maximum(mpart_ref[...], _fold_lanes(s, jnp.maximum))

    mpart_ref[...] = jnp.full((r2, LANES), NEG, jnp.float32)
    _pairwise(n_far, lambda kc: scores(kc, None))
    m_far = jnp.max(mpart_ref[...], axis=-1, keepdims=True) + cfar
    mpart_ref[...] = jnp.full((r2, LANES), NEG, jnp.float32)

    @pl.when(qi >= 1)
    def _():
        scores(qi - 1, tile_ref[0, 1])

    scores(qi, tile_ref[0, 0])
    m = jnp.maximum(m_far, jnp.max(mpart_ref[...], axis=-1, keepdims=True))
    shift_ref[0] = jnp.broadcast_to(m - cfar, (r2, LANES))
    shift_ref[1] = jnp.broadcast_to(m, (r2, LANES))

    lpart_ref[...] = jnp.zeros((r2, LANES), jnp.float32)
    acc_ref[...] = jnp.zeros((r2, DA_DV), jnp.float32)

    def weights(kc, which):
        s = s_ref[kc]
        sh = shift_ref[which]
        ps = [jnp.exp(s[:, c * LANES:(c + 1) * LANES] - sh) for c in range(tq // LANES)]
        tot = ps[0]
        for pc in ps[1:]:
            tot = tot + pc
        lpart_ref[...] = lpart_ref[...] + tot
        p = jnp.concatenate(ps, axis=1).astype(BF16)
        acc_ref[...] = acc_ref[...] + _dot(p, v_ref[chunk_rows(kc), :])

    _pairwise(n_far, lambda kc: weights(kc, 0))

    @pl.when(qi >= 1)
    def _():
        weights(qi - 1, 1)

    weights(qi, 1)

    lam = lam_ref[0]
    a = acc_ref[...] / jnp.sum(lpart_ref[...], axis=-1, keepdims=True)
    o = a[:tq] - lam * a[tq:]
    o = o * lax.rsqrt(jnp.mean(o * o, axis=-1, keepdims=True) + EPS) * g_ref[...]
    o_ref[...] = (o * out_scale).astype(o_ref.dtype)


def _da_prompt(q, k, v, tiles, cfar, lam, g_subln, nb, t, tq, out_scale):
    n = q.shape[0]
    nq = t // tq
    grid_spec = pltpu.PrefetchScalarGridSpec(
        num_scalar_prefetch=0,
        grid=(nb, DA_HEADS, nq),
        in_specs=[pl.BlockSpec(memory_space=pltpu.SMEM),
                  pl.BlockSpec(memory_space=pltpu.SMEM),
                  pl.BlockSpec((tq, LANES), lambda b, h, i: (b * nq + i, h)),
                  pl.BlockSpec((t, LANES), lambda b, h, i: (b, h)),
                  pl.BlockSpec((t, LANES), lambda b, h, i: (b, h)),
                  pl.BlockSpec((1, 2, tq, tq), lambda b, h, i: (h, 0, 0, 0)),
                  pl.BlockSpec((1, DA_DV), lambda b, h, i: (0, 0))],
        out_specs=pl.BlockSpec((tq, LANES), lambda b, h, i: (b * nq + i, h)),
        scratch_shapes=[pltpu.VMEM((nq, 2 * tq, tq), jnp.float32),
                        pltpu.VMEM((2 * tq, LANES), jnp.float32),
                        pltpu.VMEM((2, 2 * tq, LANES), jnp.float32),
                        pltpu.VMEM((2 * tq, LANES), jnp.float32),
                        pltpu.VMEM((2 * tq, DA_DV), jnp.float32)])
    return pl.pallas_call(
        functools.partial(_da_prompt_kernel, tq=tq, out_scale=out_scale),
        grid_spec=grid_spec,
        out_shape=jax.ShapeDtypeStruct((n, DA_HEADS * DA_DV), BF16),
        compiler_params=_cparams(("arbitrary", "arbitrary", "arbitrary")),
        name="da_prompt",
    )(cfar, lam, q, k, v, tiles, g_subln.reshape(1, DA_DV))


def _count_ge(key_ref, cand, n_chunks, tq):
    def body(kc, acc):
        ge = jnp.where(key_ref[kc] >= cand, 1.0, 0.0)
        for c in range(tq // LANES):
            acc = acc + ge[:, c * LANES:(c + 1) * LANES]
        return acc
    acc = lax.fori_loop(0, n_chunks, body, jnp.zeros((tq, LANES), jnp.float32))
    return jnp.sum(acc, axis=-1, keepdims=True)


def _sa_prompt_kernel(cfar_ref, qs_ref, qx_ref, mq_ref, mk_ref, ks_ref, vs_ref, tile_ref, o_ref,
                      k2_ref, key_ref, cut_ref, s_ref, mpart_ref, shift_ref, lpart_ref, acc_ref,
                      *, tq, topk):
    qi = pl.program_id(1)
    n_chunks = qi + 1
    t = mk_ref.shape[0]

    @pl.when(qi == 0)
    def _():
        kix = mk_ref[:, :LANES].astype(jnp.float32)
        lane = lax.broadcasted_iota(jnp.int32, kix.shape, 1)
        k2_ref[0] = jnp.where(lane < IDX_DK, kix, 0.0).astype(BF16)
        k2_ref[1] = jnp.where(lane >= IDX_DK, pltpu.roll(kix, IDX_DK, axis=1), 0.0).astype(BF16)

    wix = mq_ref[:, IDX_DK:IDX_DK + IDX_HEADS]
    wcols = [wix[:, hh:hh + 1] for hh in range(IDX_HEADS)]
    row = lax.broadcasted_iota(jnp.int32, (tq, tq), 0)
    col = lax.broadcasted_iota(jnp.int32, (tq, tq), 1)

    def score_body(kc, carry):
        rows = pl.ds(pl.multiple_of(kc * tq, tq), tq)
        ke = k2_ref[0, rows, :]
        ko = k2_ref[1, rows, :]
        sc = jnp.zeros((tq, tq), jnp.float32)
        for p in range(IDX_HEADS // 2):
            qp = qx_ref[:, p * LANES:(p + 1) * LANES]
            sc = sc + wcols[2 * p] * jnp.maximum(_dot_nt(qp, ke), 0.0)
            sc = sc + wcols[2 * p + 1] * jnp.maximum(_dot_nt(qp, ko), 0.0)
        sc = sc * (IDX_DK ** -0.5 * IDX_HEADS ** -0.5)
        sc = jnp.where((kc < qi) | (row >= col), sc, -jnp.inf)
        key_ref[kc] = _float_key(sc)
        return carry

    lax.fori_loop(0, n_chunks, score_body, 0)

    def bit_body(it, carry):
        thr, cnt_thr = carry
        cand = thr + (jnp.int32(1) << (31 - it))
        cnt = _count_ge(key_ref, cand, n_chunks, tq)
        ok = cnt >= float(topk)
        return jnp.where(ok, cand, thr), jnp.where(ok, cnt, cnt_thr)

    thr0 = jnp.full((tq, 1), INT_MIN, jnp.int32)
    cnt0 = jnp.full((tq, 1), 3.0e38, jnp.float32)
    thr, cnt_thr = lax.fori_loop(0, 32, bit_body, (thr0, cnt0))
    tied = (cnt_thr > float(topk)) & (thr > jnp.int32(KEY_NEG_INF))
    need_tie = jnp.max(jnp.where(tied, 1.0, 0.0)) > 0.0
    thr = jnp.maximum(thr, jnp.int32(KEY_NEG_INF + 1))

    cut_ref[...] = jnp.full((tq, 1), 2 ** 30, jnp.int32)

    @pl.when(need_tie)
    def _():
        def gt_body(kc, acc):
            g = jnp.where(key_ref[kc] > thr, 1.0, 0.0)
            return acc + jnp.sum(g, axis=-1, keepdims=True)
        n_gt = lax.fori_loop(0, n_chunks, gt_body, jnp.zeros((tq, 1), jnp.float32))
        need = float(topk) - n_gt
        n_bits = max(1, int(t).bit_length())

        def cut_body(it, cut):
            cand = cut + (jnp.int32(1) << (n_bits - 1 - it))

            def eq_body(kc, acc):
                pos = kc * tq + col
                e = jnp.where((key_ref[kc] == thr) & (pos < cand), 1.0, 0.0)
                return acc + jnp.sum(e, axis=-1, keepdims=True)
            n_eq = lax.fori_loop(0, n_chunks, eq_body, jnp.zeros((tq, 1), jnp.float32))
            return jnp.where(n_eq <= need, cand, cut)
        cut_ref[...] = lax.fori_loop(0, n_bits, cut_body, jnp.zeros((tq, 1), jnp.int32))

    cut = cut_ref[...]

    n_far = jnp.maximum(qi - 1, 0)
    rg = SA_GROUP * tq

    def chunk_rows(kc):
        return pl.ds(pl.multiple_of(kc * tq, tq), tq)

    for g in range(SA_KV_HEADS):
        heads = [g * SA_GROUP + j for j in range(SA_GROUP)]
        qg = jnp.concatenate([qs_ref[:, hh * LANES:(hh + 1) * LANES] for hh in heads], axis=0)
        cf_rows = jnp.concatenate([jnp.full((tq, 1), cfar_ref[hh], jnp.float32) for hh in heads], axis=0)

        def scores(kc, kind, qg=qg, heads=heads, g=g):
            key = key_ref[kc]
            sel = (key > thr) | ((key == thr) & (kc * tq + col < cut))
            s_all = _dot_nt(qg, ks_ref[chunk_rows(kc), g * SA_DH:(g + 1) * SA_DH])
            parts = []
            for j, hh in enumerate(heads):
                s = s_all[j * tq:(j + 1) * tq]
                if kind is not None:
                    s = s + tile_ref[hh, kind]
                parts.append(jnp.where(sel, s, NEG))
            s = jnp.concatenate(parts, axis=0)
            s_ref[kc] = s
            mpart_ref[...] = jnp.maximum(mpart_ref[...], _fold_lanes(s, jnp.maximum))

        mpart_ref[...] = jnp.full((rg, LANES), NEG, jnp.float32)
        _pairwise(n_far, lambda kc, f=scores: f(kc, None))
        m_far = jnp.max(mpart_ref[...], axis=-1, keepdims=True) + cf_rows
        mpart_ref[...] = jnp.full((rg, LANES), NEG, jnp.float32)

        @pl.when(qi >= 1)
        def _(f=scores):
            f(qi - 1, 1)

        scores(qi, 0)
        m = jnp.maximum(m_far, jnp.max(mpart_ref[...], axis=-1, keepdims=True))
        shift_ref[0] = jnp.broadcast_to(m - cf_rows, (rg, LANES))
        shift_ref[1] = jnp.broadcast_to(m, (rg, LANES))
        lpart_ref[...] = jnp.zeros((rg, LANES), jnp.float32)
        acc_ref[...] = jnp.zeros((rg, SA_DH), jnp.float32)

        def weights(kc, which, g=g):
            s = s_ref[kc]
            sh = shift_ref[which]
            ps = [jnp.exp(s[:, c * LANES:(c + 1) * LANES] - sh) for c in range(tq // LANES)]
            tot = ps[0]
            for pc in ps[1:]:
                tot = tot + pc
            lpart_ref[...] = lpart_ref[...] + tot
            p = jnp.concatenate(ps, axis=1).astype(BF16)
            acc_ref[...] = acc_ref[...] + _dot(p, vs_ref[chunk_rows(kc), g * SA_DH:(g + 1) * SA_DH])

        _pairwise(n_far, lambda kc, f=weights: f(kc, 0))

        @pl.when(qi >= 1)
        def _(f=weights):
            f(qi - 1, 1)

        weights(qi, 1)
        a = acc_ref[...] / jnp.sum(lpart_ref[...], axis=-1, keepdims=True)
        for j, hh in enumerate(heads):
            o_ref[:, hh * SA_DH:(hh + 1) * SA_DH] = a[j * tq:(j + 1) * tq].astype(o_ref.dtype)


def _sa_prompt(q_sa, q_ix, misc_q, misc_k, k_sa, v_sa, tiles, cfar, nb, t, tq, topk):
    n = q_sa.shape[0]
    nq = t // tq
    mw = misc_q.shape[1]
    kvw = SA_KV_HEADS * SA_DH
    once = dict(pipeline_mode=pl.Buffered(1))
    rg = SA_GROUP * tq
    return pl.pallas_call(
        functools.partial(_sa_prompt_kernel, tq=tq, topk=topk),
        grid=(nb, nq),
        in_specs=[pl.BlockSpec(memory_space=pltpu.SMEM),
                  pl.BlockSpec((tq, SA_HEADS * SA_DH), lambda b, i: (b * nq + i, 0)),
                  pl.BlockSpec((tq, IDX_HEADS * IDX_DK), lambda b, i: (b * nq + i, 0)),
                  pl.BlockSpec((tq, mw), lambda b, i: (b * nq + i, 0)),
                  pl.BlockSpec((t, mw), lambda b, i: (b, 0), **once),
                  pl.BlockSpec((t, kvw), lambda b, i: (b, 0), **once),
                  pl.BlockSpec((t, kvw), lambda b, i: (b, 0), **once),
                  pl.BlockSpec((SA_HEADS, 2, tq, tq), lambda b, i: (0, 0, 0, 0), **once)],
        out_specs=pl.BlockSpec((tq, SA_HEADS * SA_DH), lambda b, i: (b * nq + i, 0)),
        out_shape=jax.ShapeDtypeStruct((n, SA_HEADS * SA_DH), BF16),
        scratch_shapes=[pltpu.VMEM((2, t, LANES), BF16),
                        pltpu.VMEM((nq, tq, tq), jnp.int32),
                        pltpu.VMEM((tq, 1), jnp.int32),
                        pltpu.VMEM((nq, rg, tq), jnp.float32),
                        pltpu.VMEM((rg, LANES), jnp.float32),
                        pltpu.VMEM((2, rg, LANES), jnp.float32),
                        pltpu.VMEM((rg, LANES), jnp.float32),
                        pltpu.VMEM((rg, SA_DH), jnp.float32)],
        compiler_params=_cparams(("arbitrary", "arbitrary")),
        name="sa_prompt",
    )(cfar, q_sa, q_ix, misc_q, misc_k, k_sa, v_sa, tiles)


def _proj_gate_kernel(oda_ref, osa_ref, wpd_ref, wps_ref, gda_ref, gsa_ref, o_ref, wbf_ref):
    @pl.when(pl.program_id(1) == 0)
    def _():
        wbf_ref[0] = wpd_ref[...].astype(BF16)
        wbf_ref[1] = wps_ref[...].astype(BF16)

    a = _dot(oda_ref[...], wbf_ref[0])
    b = _dot(osa_ref[...], wbf_ref[1])
    o_ref[...] = (gda_ref[...].astype(jnp.float32) * a
                  + gsa_ref[...].astype(jnp.float32) * b).astype(o_ref.dtype)


def _proj_gate(o_da, o_sa, w_pd, w_ps, gates, d, tm=512, tn=512):
    n, kd = o_da.shape
    ks = o_sa.shape[1]
    tn = min(tn, d)
    tm = min(tm, n)
    nj = d // tn
    return pl.pallas_call(
        _proj_gate_kernel,
        grid=(nj, n // tm),
        in_specs=[pl.BlockSpec((tm, kd), lambda j, i: (i, 0)),
                  pl.BlockSpec((tm, ks), lambda j, i: (i, 0)),
                  pl.BlockSpec((kd, tn), lambda j, i: (0, j)),
                  pl.BlockSpec((ks, tn), lambda j, i: (0, j)),
                  pl.BlockSpec((tm, tn), lambda j, i: (i, j)),
                  pl.BlockSpec((tm, tn), lambda j, i: (i, nj + j))],
        out_specs=pl.BlockSpec((tm, tn), lambda j, i: (i, j)),
        out_shape=jax.ShapeDtypeStruct((n, d), BF16),
        scratch_shapes=[pltpu.VMEM((2, kd, tn), BF16)],
        compiler_params=_cparams(("arbitrary", "arbitrary")),
        name="proj_gate",
    )(o_da, o_sa, w_pd, w_ps, gates, gates)


def _mm_resid_kernel(m_ref, w_ref, x_ref, gt_ref, o_ref, wbf_ref):
    @pl.when(pl.program_id(1) == 0)
    def _():
        wbf_ref[...] = w_ref[...].astype(BF16)

    o_ref[...] = x_ref[...] + gt_ref[0] * _dot(m_ref[...], wbf_ref[...])


def _mm_resid(m, w, x, gt, rows_per_batch, tm=512, tn=512):
    n, k = m.shape
    d = w.shape[1]
    tn = min(tn, d)
    tm = min(tm, n)
    per = rows_per_batch // tm
    return pl.pallas_call(
        _mm_resid_kernel,
        grid=(d // tn, n // tm),
        in_specs=[pl.BlockSpec((tm, k), lambda j, i: (i, 0)),
                  pl.BlockSpec((k, tn), lambda j, i: (0, j)),
                  pl.BlockSpec((tm, tn), lambda j, i: (i, j)),
                  pl.BlockSpec((1, 1, tn), lambda j, i: (i // per, 0, j))],
        out_specs=pl.BlockSpec((tm, tn), lambda j, i: (i, j)),
        out_shape=jax.ShapeDtypeStruct((n, d), jnp.float32),
        scratch_shapes=[pltpu.VMEM((k, tn), BF16)],
        compiler_params=_cparams(("arbitrary", "arbitrary")),
        name="mm_resid",
    )(m, w, x, gt[:, None, :])


def _norm_router_kernel(x_ref, g_ref, sc_ref, sh_ref, wr_ref, th_ref, tl_ref, h_ref, lg_ref):
    last = pl.num_programs(0) - 1

    @pl.when(pl.program_id(0) < last)
    def _():
        x = x_ref[...]
        y = x * lax.rsqrt(jnp.mean(x * x, axis=-1, keepdims=True) + EPS) * g_ref[...]
        h = y * (1.0 + sc_ref[0]) + sh_ref[0]
        h_ref[...] = h.astype(h_ref.dtype)
        lg_ref[...] = _dot_nt(_bf(wr_ref[...]), _bf(h))

    @pl.when(pl.program_id(0) == last)
    def _():
        h_ref[...] = th_ref[...]
        lg_ref[...] = tl_ref[...]


def _norm_router(x, g, sc, sh, w_router_t, tail_h, tail_lg, rows_per_batch, tm):
    n, d = x.shape
    per = rows_per_batch // tm
    nt = n // tm
    assert tail_h.shape == (tm, d) and tail_lg.shape == (N_EXPERTS, tm)
    row = lambda i: jnp.minimum(i, nt - 1)
    return pl.pallas_call(
        _norm_router_kernel,
        grid=(nt + 1,),
        in_specs=[pl.BlockSpec((tm, d), lambda i: (row(i), 0)),
                  pl.BlockSpec((1, d), lambda i: (0, 0)),
                  pl.BlockSpec((1, 1, d), lambda i: (row(i) // per, 0, 0)),
                  pl.BlockSpec((1, 1, d), lambda i: (row(i) // per, 0, 0)),
                  pl.BlockSpec((N_EXPERTS, d), lambda i: (0, 0)),
                  pl.BlockSpec((tm, d), lambda i: (0, 0)),
                  pl.BlockSpec((N_EXPERTS, tm), lambda i: (0, 0))],
        out_specs=[pl.BlockSpec((tm, d), lambda i: (i, 0)),
                   pl.BlockSpec((N_EXPERTS, tm), lambda i: (0, i))],
        out_shape=[jax.ShapeDtypeStruct((n + tm, d), BF16),
                   jax.ShapeDtypeStruct((N_EXPERTS, n + tm), jnp.float32)],
        compiler_params=_cparams(("arbitrary",)),
        name="norm_router",
    )(x, g.reshape(1, d), sc[:, None, :], sh[:, None, :], w_router_t, tail_h, tail_lg)


def _route_kernel(lg_ref, b_ref, eidx_ref, wts_ref, rank_ref, cnt_ref, *, tn):
    shape = (N_GROUPS, GROUP_SIZE, tn)
    sc = _sigmoid(lg_ref[...])
    biased = sc + b_ref[...]
    e_iota = lax.broadcasted_iota(jnp.int32, shape, 1)
    g_iota3 = lax.broadcasted_iota(jnp.int32, shape, 0)
    flat_iota = g_iota3 * GROUP_SIZE + e_iota
    g_iota = lax.broadcasted_iota(jnp.int32, (N_GROUPS, 1, tn), 0)
    ninf = -jnp.inf

    m1 = jnp.max(biased, axis=1, keepdims=True)
    first = jnp.min(jnp.where(biased == m1, e_iota, GROUP_SIZE), axis=1, keepdims=True)
    m2 = jnp.max(jnp.where(e_iota == first, ninf, biased), axis=1, keepdims=True)
    cur = m1 + m2
    gsel = jnp.zeros((N_GROUPS, 1, tn), jnp.float32)
    for _ in range(TOPK_GROUPS):
        mx = jnp.max(cur, axis=0, keepdims=True)
        idx = jnp.min(jnp.where(cur == mx, g_iota, N_GROUPS), axis=0, keepdims=True)
        hit = g_iota == idx
        gsel = jnp.where(hit, 1.0, gsel)
        cur = jnp.where(hit, ninf, cur)

    cur = jnp.where(gsel > 0.0, biased, ninf)
    hits, ws = [], []
    for k in range(TOP_K):
        mx = jnp.max(jnp.max(cur, axis=1, keepdims=True), axis=0, keepdims=True)
        cand = jnp.where(cur == mx, flat_iota, N_EXPERTS)
        idx = jnp.min(jnp.min(cand, axis=1, keepdims=True), axis=0, keepdims=True)
        hit = flat_iota == idx
        w = jnp.sum(jnp.sum(jnp.where(hit, sc, 0.0), axis=1, keepdims=True), axis=0, keepdims=True)
        eidx_ref[k:k + 1, :] = idx.reshape(1, tn)
        hits.append(hit)
        ws.append(w)
        cur = jnp.where(hit, ninf, cur)
    wsum = ws[0]
    for w in ws[1:]:
        wsum = wsum + w
    for k in range(TOP_K):
        wts_ref[k:k + 1, :] = (ws[k] / wsum * ROUTED_SCALE).reshape(1, tn)

    member = jnp.zeros(shape, jnp.float32)
    for hit in hits:
        member = jnp.where(hit, 1.0, member)
    member2 = member.reshape(N_EXPERTS, tn)
    r = lax.broadcasted_iota(jnp.int32, (tn, tn), 0)
    c = lax.broadcasted_iota(jnp.int32, (tn, tn), 1)
    upper = jnp.where(r < c, 1.0, 0.0).astype(BF16)
    prefix = _dot(member2.astype(BF16), upper).reshape(shape)
    for k in range(TOP_K):
        rk = jnp.sum(jnp.sum(jnp.where(hits[k], prefix, 0.0), axis=1, keepdims=True), axis=0, keepdims=True)
        rank_ref[k:k + 1, :] = rk.reshape(1, tn)
    cnt = jnp.sum(member2, axis=1, keepdims=True)
    cnt_ref[0] = jnp.broadcast_to(cnt, (N_EXPERTS, LANES))


def _route(logits_t, b_router, tn):
    n_pad = logits_t.shape[1]
    nt = n_pad // tn
    lg3 = logits_t.reshape(N_GROUPS, GROUP_SIZE, n_pad)
    b3 = b_router.astype(jnp.float32).reshape(N_GROUPS, GROUP_SIZE, 1)
    row = lambda dt: jax.ShapeDtypeStruct((TOP_K, n_pad), dt)
    return pl.pallas_call(
        functools.partial(_route_kernel, tn=tn),
        grid=(nt,),
        in_specs=[pl.BlockSpec((N_GROUPS, GROUP_SIZE, tn), lambda i: (0, 0, i)),
                  pl.BlockSpec((N_GROUPS, GROUP_SIZE, 1), lambda i: (0, 0, 0))],
        out_specs=[pl.BlockSpec((TOP_K, tn), lambda i: (0, i)),
                   pl.BlockSpec((TOP_K, tn), lambda i: (0, i)),
                   pl.BlockSpec((TOP_K, tn), lambda i: (0, i)),
                   pl.BlockSpec((1, N_EXPERTS, LANES), lambda i: (i, 0, 0))],
        out_shape=[row(jnp.int32), row(jnp.float32), row(jnp.float32),
                   jax.ShapeDtypeStruct((nt, N_EXPERTS, LANES), jnp.float32)],
        compiler_params=_cparams(("arbitrary",)),
        name="route",
    )(lg3, b3)


def _dest_kernel(eidx_ref, rank_ref, base_ref, o_ref, *, tn):
    e_iota = lax.broadcasted_iota(jnp.int32, (N_EXPERTS, tn), 0)
    base = base_ref[0][:, :1]
    for k in range(TOP_K):
        onehot = e_iota == eidx_ref[k:k + 1, :]
        b = jnp.sum(jnp.where(onehot, base, 0.0), axis=0, keepdims=True)
        o_ref[k:k + 1, :] = (b + rank_ref[k:k + 1, :]).astype(jnp.int32)


def _dest(eidx, rank, base, tn):
    n_pad = eidx.shape[1]
    return pl.pallas_call(
        functools.partial(_dest_kernel, tn=tn),
        grid=(n_pad // tn,),
        in_specs=[pl.BlockSpec((TOP_K, tn), lambda i: (0, i)),
                  pl.BlockSpec((TOP_K, tn), lambda i: (0, i)),
                  pl.BlockSpec((1, N_EXPERTS, LANES), lambda i: (i, 0, 0))],
        out_specs=pl.BlockSpec((TOP_K, tn), lambda i: (0, i)),
        out_shape=jax.ShapeDtypeStruct((TOP_K, n_pad), jnp.int32),
        compiler_params=_cparams(("arbitrary",)),
        name="dest",
    )(eidx, rank, base)


def _pack_words(lo_f32, hi_f32):
    lo = lax.shift_right_logical(pltpu.bitcast(lo_f32, jnp.uint32), jnp.uint32(16))
    hi = pltpu.bitcast(hi_f32, jnp.uint32) & jnp.uint32(0xFFFF0000)
    return hi | lo


def _unpack_words(w):
    lo = pltpu.bitcast(lax.shift_left(w, jnp.uint32(16)), jnp.float32)
    hi = pltpu.bitcast(w & jnp.uint32(0xFFFF0000), jnp.float32)
    return lo, hi


def _bf16_exact(x):
    return x.astype(BF16).astype(jnp.float32)


def _dispatch_kernel(pstart_ref, pcnt_ref, h_ref, dest_hbm, xs_hbm, dsm, pk, zrow, sem_d, sem_r,
                     *, tn, nt):
    i = pl.program_id(0)
    half = pk.shape[1]

    @pl.when(i < nt)
    def _():
        cp = pltpu.make_async_copy(dest_hbm.at[pl.ds(i * (TOP_K * tn), TOP_K * tn)], dsm, sem_d)
        cp.start()
        x = h_ref[...]
        pk[...] = _pack_words(x[:, :half].astype(jnp.float32), x[:, half:].astype(jnp.float32))
        cp.wait()

        def body(r, carry):
            for k in range(TOP_K):
                d = dsm[k * tn + r]
                pltpu.make_async_copy(pk.at[pl.ds(r, 1), :], xs_hbm.at[pl.ds(d, 1), :],
                                      sem_r).start(priority=k % 2)
            return carry

        lax.fori_loop(0, tn, body, 0)
        for k in range(TOP_K):
            pltpu.make_async_copy(pk, xs_hbm.at[pl.ds(0, tn), :], sem_r).wait()

    @pl.when(i == nt)
    def _():
        zrow[...] = jnp.zeros(zrow.shape, zrow.dtype)

        def per_expert(e, carry):
            s0 = pstart_ref[e]
            c = pcnt_ref[e]

            def start(r, cc):
                pltpu.make_async_copy(zrow.at[pl.ds(0, 1), :], xs_hbm.at[pl.ds(s0 + r, 1), :], sem_r).start()
                return cc

            def wait(r, cc):
                pltpu.make_async_copy(zrow.at[pl.ds(0, 1), :], xs_hbm.at[pl.ds(s0, 1), :], sem_r).wait()
                return cc

            lax.fori_loop(0, c, start, 0)
            lax.fori_loop(0, c, wait, 0)
            return carry

        lax.fori_loop(0, N_EXPERTS, per_expert, 0)

        t0 = pstart_ref[N_EXPERTS]
        groups = pcnt_ref[N_EXPERTS]
        rows8 = lambda r: pl.ds(pl.multiple_of(t0 + r * 8, 8), 8)

        def tstart(r, cc):
            pltpu.make_async_copy(zrow, xs_hbm.at[rows8(r), :], sem_r).start()
            return cc

        def twait(r, cc):
            pltpu.make_async_copy(zrow, xs_hbm.at[rows8(0), :], sem_r).wait()
            return cc

        lax.fori_loop(0, groups, tstart, 0)
        lax.fori_loop(0, groups, twait, 0)


def _dispatch(h2, dest_flat, pad_start, pad_cnt, n_rows, tn):
    n_pad, d = h2.shape
    nt = n_pad // tn
    grid_spec = pltpu.PrefetchScalarGridSpec(
        num_scalar_prefetch=2,
        grid=(nt + 1,),
        in_specs=[pl.BlockSpec((tn, d), lambda i, a, b: (jnp.minimum(i, nt - 1), 0)),
                  pl.BlockSpec(memory_space=pl.ANY)],
        out_specs=pl.BlockSpec(memory_space=pl.ANY),
        scratch_shapes=[pltpu.SMEM((TOP_K * tn,), jnp.int32),
                        pltpu.VMEM((tn, d // 2), jnp.uint32),
                        pltpu.VMEM((8, d // 2), jnp.uint32),
                        pltpu.SemaphoreType.DMA(()),
                        pltpu.SemaphoreType.DMA(())])
    return pl.pallas_call(
        functools.partial(_dispatch_kernel, tn=tn, nt=nt),
        grid_spec=grid_spec,
        out_shape=jax.ShapeDtypeStruct((n_rows, d // 2), jnp.uint32),
        compiler_params=_cparams(("arbitrary",)),
        name="dispatch",
    )(pad_start, pad_cnt, h2, dest_flat)


def _expert_kernel(blk_e_ref, nused_ref, next_e_ref, slot_ref, x_ref, wg_hbm, wu_hbm, wd_hbm, y_ref,
                   wg_f, wu_f, wd_f, wgb, wub, wdb, sems):
    i = pl.program_id(0)
    nused = nused_ref[0]
    ii = jnp.minimum(i, nused - 1)
    e = blk_e_ref[ii]
    e_prev = blk_e_ref[jnp.maximum(ii - 1, 0)]
    half = x_ref.shape[1]

    def copies(ex, sl):
        return [pltpu.make_async_copy(src.at[ex], dst.at[sl], sems.at[sl, t])
                for t, (src, dst) in enumerate(((wg_hbm, wg_f), (wu_hbm, wu_f), (wd_hbm, wd_f)))]

    @pl.when(i == 0)
    def _():
        for c in copies(e, slot_ref[ii]):
            c.start()

    @pl.when((i < nused) & ((i == 0) | (e != e_prev)))
    def _():
        sl = slot_ref[ii]
        for c in copies(e, sl):
            c.wait()
        nxt = next_e_ref[ii]

        @pl.when(nxt >= 0)
        def _():
            for c in copies(nxt, 1 - sl):
                c.start()

        wgb[...] = wg_f[sl].astype(BF16)
        wub[...] = wu_f[sl].astype(BF16)
        wdb[...] = wd_f[sl].astype(BF16)

    @pl.when(i < nused)
    def _():
        lo, hi = _unpack_words(x_ref[...])
        xl = lo.astype(BF16)
        xh = hi.astype(BF16)
        g = _dot(xl, wgb[:half, :]) + _dot(xh, wgb[half:, :])
        u = _dot(xl, wub[:half, :]) + _dot(xh, wub[half:, :])
        hmid = (_silu(g) * u).astype(BF16)
        y = _dot(hmid, wdb[...])
        y_ref[...] = _pack_words(_bf16_exact(y[:, :half]), _bf16_exact(y[:, half:]))

    @pl.when(i >= nused)
    def _():
        y_ref[...] = jnp.zeros(y_ref.shape, y_ref.dtype)


def _experts(xs, blk_e, nused, next_e, slot, w_gate, w_up, w_down, tb):
    n_rows, half = xs.shape
    _, d, f = w_gate.shape
    nblk = n_rows // tb

    def xmap(i, be, nu, ne, sl):
        return (jnp.minimum(i, nu[0] - 1), 0)

    grid_spec = pltpu.PrefetchScalarGridSpec(
        num_scalar_prefetch=4,
        grid=(nblk,),
        in_specs=[pl.BlockSpec((tb, half), xmap),
                  pl.BlockSpec(memory_space=pl.ANY),
                  pl.BlockSpec(memory_space=pl.ANY),
                  pl.BlockSpec(memory_space=pl.ANY)],
        out_specs=pl.BlockSpec((tb, half), lambda i, be, nu, ne, sl: (i, 0)),
        scratch_shapes=[pltpu.VMEM((2, d, f), jnp.float32),
                        pltpu.VMEM((2, d, f), jnp.float32),
                        pltpu.VMEM((2, f, d), jnp.float32),
                        pltpu.VMEM((d, f), BF16),
                        pltpu.VMEM((d, f), BF16),
                        pltpu.VMEM((f, d), BF16),
                        pltpu.SemaphoreType.DMA((2, 3))])
    return pl.pallas_call(
        _expert_kernel,
        grid_spec=grid_spec,
        out_shape=jax.ShapeDtypeStruct((n_rows, half), jnp.uint32),
        compiler_params=_cparams(("arbitrary",)),
        name="experts",
    )(blk_e, nused, next_e, slot, xs, w_gate, w_up, w_down)


def _ffn_kernel(h_ref, wg_ref, wu_ref, wd_ref, o_ref, wgb, wub, wdb):
    @pl.when(pl.program_id(0) == 0)
    def _():
        wgb[...] = wg_ref[...].astype(BF16)
        wub[...] = wu_ref[...].astype(BF16)
        wdb[...] = wd_ref[...].astype(BF16)

    x = h_ref[...]
    hmid = (_silu(_dot(x, wgb[...])) * _dot(x, wub[...])).astype(BF16)
    o_ref[...] = _dot(hmid, wdb[...]).astype(o_ref.dtype)


def _ffn_shared(h2, wg, wu, wd, tm):
    n_pad, d = h2.shape
    f = wg.shape[1]
    return pl.pallas_call(
        _ffn_kernel,
        grid=(n_pad // tm,),
        in_specs=[pl.BlockSpec((tm, d), lambda i: (i, 0)),
                  pl.BlockSpec((d, f), lambda i: (0, 0)),
                  pl.BlockSpec((d, f), lambda i: (0, 0)),
                  pl.BlockSpec((f, d), lambda i: (0, 0))],
        out_specs=pl.BlockSpec((tm, d), lambda i: (i, 0)),
        out_shape=jax.ShapeDtypeStruct((n_pad, d), BF16),
        scratch_shapes=[pltpu.VMEM((d, f), BF16),
                        pltpu.VMEM((d, f), BF16),
                        pltpu.VMEM((f, d), BF16)],
        compiler_params=_cparams(("arbitrary",)),
        name="ffn_shared",
    )(h2, wg, wu, wd)


def _combine_kernel(dest_hbm, ys_hbm, wts_ref, sh_ref, x_ref, gt_ref, g_ref, o_ref,
                    dsm, buf, sem_d, sem_r, *, tn, tile0, final_norm):
    i = pl.program_id(0)
    cp = pltpu.make_async_copy(dest_hbm.at[pl.ds((tile0 + i) * (TOP_K * tn), TOP_K * tn)], dsm, sem_d)
    cp.start()
    cp.wait()

    def body(r, carry):
        for k in range(TOP_K):
            d = dsm[k * tn + r]
            pltpu.make_async_copy(ys_hbm.at[pl.ds(d, 1), :], buf.at[k, pl.ds(r, 1), :],
                                  sem_r).start(priority=k % 2)
        return carry

    lax.fori_loop(0, tn, body, 0)
    for k in range(TOP_K):
        pltpu.make_async_copy(ys_hbm.at[pl.ds(0, tn), :], buf.at[k], sem_r).wait()

    half = buf.shape[2]
    wts = wts_ref[...]
    acc_lo = jnp.zeros((tn, half), jnp.float32)
    acc_hi = jnp.zeros((tn, half), jnp.float32)
    for k in range(TOP_K):
        lo, hi = _unpack_words(buf[k])
        wk = wts[:, k:k + 1]
        acc_lo = acc_lo + wk * lo
        acc_hi = acc_hi + wk * hi
    sh = sh_ref[...].astype(jnp.float32)
    gt = gt_ref[0]
    x_lo = x_ref[:, :half] + gt[:, :half] * (acc_lo + sh[:, :half])
    x_hi = x_ref[:, half:] + gt[:, half:] * (acc_hi + sh[:, half:])
    if final_norm:
        ms = (jnp.sum(x_lo * x_lo, axis=-1, keepdims=True)
              + jnp.sum(x_hi * x_hi, axis=-1, keepdims=True)) / (2 * half)
        inv = lax.rsqrt(ms + EPS)
        g = g_ref[...]
        x_lo = x_lo * inv * g[:, :half]
        x_hi = x_hi * inv * g[:, half:]
    o_ref[:, :half] = x_lo
    o_ref[:, half:] = x_hi


def _combine(dest_flat, ys, wts_tok, shared, x1, gt3, g_final, tile0, tn, final_norm):
    rows, d = x1.shape
    nt = rows // tn
    gr = gt3.shape[1]
    per = nt // gt3.shape[0]
    return pl.pallas_call(
        functools.partial(_combine_kernel, tn=tn, tile0=tile0, final_norm=final_norm),
        grid=(nt,),
        in_specs=[pl.BlockSpec(memory_space=pl.ANY),
                  pl.BlockSpec(memory_space=pl.ANY),
                  pl.BlockSpec((tn, TOP_K), lambda i: (tile0 + i, 0)),
                  pl.BlockSpec((tn, d), lambda i: (tile0 + i, 0)),
                  pl.BlockSpec((tn, d), lambda i: (i, 0)),
                  pl.BlockSpec((1, gr, d), lambda i: (i // per, 0, 0)),
                  pl.BlockSpec((1, d), lambda i: (0, 0))],
        out_specs=pl.BlockSpec((tn, d), lambda i: (i, 0)),
        out_shape=jax.ShapeDtypeStruct((rows, d), jnp.float32),
        scratch_shapes=[pltpu.SMEM((TOP_K * tn,), jnp.int32),
                        pltpu.VMEM((TOP_K, tn, d // 2), jnp.uint32),
                        pltpu.SemaphoreType.DMA(()),
                        pltpu.SemaphoreType.DMA(())],
        compiler_params=_cparams(("arbitrary",)),
        name="combine",
    )(dest_flat, ys, wts_tok, shared, x1, gt3, g_final.reshape(1, d))


def _page_specs(shape, n, pg):
    def mk(u):
        return pl.BlockSpec((1,) + shape, lambda b, j, pt: (pt[b, j * pg + u], 0, 0))
    return [mk(u) for u in range(n)]


def _kv_page_specs(kshape, vshape, pg, ns):
    def mk(shape, first):
        def one(u):
            def index(b, j, pt):
                step = jnp.minimum(j, ns - 1) if first else jnp.maximum(j - ns, 0)
                return (pt[b, step * pg + u], 0, 0)
            return pl.BlockSpec((1,) + shape, index)
        return [one(u) for u in range(pg)]
    return mk(kshape, True) + mk(vshape, False)


def _softmax_pages(s_ref):
    s = s_ref[...]
    m = jnp.max(jnp.max(s, axis=0, keepdims=True), axis=2, keepdims=True)
    e = jnp.exp(s - m)
    return e / jnp.sum(jnp.sum(e, axis=0, keepdims=True), axis=2, keepdims=True)


def _da_sample_kernel(pt_ref, lam_ref, q_ref, kn_ref, vn_ref, bl_ref, cf_ref, b0_ref, ex_ref, hm_ref,
                      *rest, pg, n_pages):
    k_refs, v_refs = rest[:pg], rest[pg:2 * pg]
    o_ref, s_ref, a_ref, acc_ref = rest[2 * pg:]
    j = pl.program_id(1)
    ns = n_pages // pg
    r = q_ref.shape[1]
    nh = r // 2
    lane = lax.broadcasted_iota(jnp.int32, (r, PAGE_SIZE), 1)

    @pl.when(j < ns)
    def _():
        q = q_ref[0]
        qb = _bf(q)
        for u in range(pg):
            page = j * pg + u
            s = _dot(qb, _bf(k_refs[u][0]))
            s_ref[page] = s + jnp.where(page == (n_pages - 1), bl_ref[...], cf_ref[...])

        @pl.when(j == 0)
        def _():
            s_new = jnp.sum(_rounded(q) * _rounded(kn_ref[0]), axis=-1, keepdims=True) + b0_ref[...]
            s_ref[n_pages] = jnp.where(lane == 0, s_new, NEG)

    @pl.when(j == ns)
    def _():
        p = _softmax_pages(s_ref)
        a = p[:, :nh, :] - lam_ref[0] * p[:, nh:, :]
        a_ref[...] = _bf(jnp.concatenate([a, jnp.zeros_like(a)], axis=1))
        a_new = a_ref[n_pages][:, 0:1].astype(jnp.float32)
        acc_ref[...] = a_new * _rounded(vn_ref[0])

    @pl.when(j >= ns)
    def _():
        acc = acc_ref[...]
        for u in range(pg):
            page = (j - ns) * pg + u
            pe = _bf(_dot(a_ref[page], ex_ref[...]) * hm_ref[...])
            acc = acc + _dot(pe, _bf(v_refs[u][0]))
        acc_ref[...] = acc

    @pl.when(j == pl.num_programs(1) - 1)
    def _():
        o_ref[0] = acc_ref[...]


def _da_sample(page_table, lam, qbd, knew, vnew, bias_last, cfar, bias0, kt, v2, pg):
    nb, n_pages = page_table.shape
    r, w = qbd.shape[1:]
    rows_v, dv = v2.shape[1:]
    nh = rows_v // PAGE_SIZE
    ns = n_pages // pg
    col = jnp.arange(rows_v, dtype=jnp.int32)
    expand = (col[None, :] // nh == jnp.arange(PAGE_SIZE, dtype=jnp.int32)[:, None]).astype(BF16)
    head_mask = (col[None, :] % nh == jnp.arange(r, dtype=jnp.int32)[:, None]).astype(jnp.float32)
    full = lambda shp: pl.BlockSpec(shp, lambda b, j, pt: (0,) * len(shp))
    per_b = lambda shp: pl.BlockSpec((1,) + shp, lambda b, j, pt: (b, 0, 0))
    grid_spec = pltpu.PrefetchScalarGridSpec(
        num_scalar_prefetch=1,
        grid=(nb, 2 * ns),
        in_specs=[pl.BlockSpec(memory_space=pltpu.SMEM),
                  per_b((r, w)), per_b((1, w)), per_b((r, dv)),
                  full((r, PAGE_SIZE)), full((r, 1)), full((r, 1)),
                  full((PAGE_SIZE, rows_v)), full((r, rows_v))]
                 + _kv_page_specs((w, PAGE_SIZE), (rows_v, dv), pg, ns),
        out_specs=per_b((r, dv)),
        scratch_shapes=[pltpu.VMEM((n_pages + 1, r, PAGE_SIZE), jnp.float32),
                        pltpu.VMEM((n_pages + 1, r, PAGE_SIZE), BF16),
                        pltpu.VMEM((r, dv), jnp.float32)])
    return pl.pallas_call(
        functools.partial(_da_sample_kernel, pg=pg, n_pages=n_pages),
        grid_spec=grid_spec,
        out_shape=jax.ShapeDtypeStruct((nb, r, dv), jnp.float32),
        compiler_params=_cparams(("arbitrary", "arbitrary")),
        name="da_sample",
    )(page_table, lam, qbd, knew, vnew, bias_last, cfar, bias0, expand, head_mask,
      *([kt] * pg), *([v2] * pg))


def _idx_sample_kernel(pt_ref, q_ref, w_ref, kn_ref, *rest, pg, n_pages, topk):
    k_refs = rest[:pg]
    sel_ref, sc_ref = rest[pg:]
    j = pl.program_id(1)
    q = q_ref[0]
    w = _rounded(w_ref[0])
    scale = IDX_DK ** -0.5 * IDX_HEADS ** -0.5
    rows = sc_ref.shape[0]
    lane = lax.broadcasted_iota(jnp.int32, (1, PAGE_SIZE), 1)

    @pl.when(j == 0)
    def _():
        sc_ref[...] = jnp.full(sc_ref.shape, -jnp.inf, jnp.float32)
        d = jnp.maximum(jnp.sum(_rounded(q) * _rounded(kn_ref[0]), axis=-1, keepdims=True), 0.0)
        s_new = jnp.sum(w * _rounded(d), axis=0, keepdims=True) * scale
        sc_ref[n_pages:n_pages + 1, :] = jnp.where(lane == 0, s_new, -jnp.inf)

    qb = _bf(q)
    for u in range(pg):
        d = _rounded(jnp.maximum(_dot(qb, _bf(k_refs[u][0])), 0.0))
        sc_ref[pl.ds(j * pg + u, 1), :] = jnp.sum(w * d, axis=0, keepdims=True) * scale

    @pl.when(j == pl.num_programs(1) - 1)
    def _():
        key = _float_key(sc_ref[...])
        pos = (lax.broadcasted_iota(jnp.int32, key.shape, 0) * PAGE_SIZE
               + lax.broadcasted_iota(jnp.int32, key.shape, 1))

        def bit_body(it, thr):
            cand = thr + (jnp.int32(1) << (31 - it))
            cnt = jnp.sum(jnp.where(key >= cand, 1.0, 0.0))
            return jnp.where(cnt >= float(topk), cand, thr)

        thr = lax.fori_loop(0, 32, bit_body, jnp.int32(INT_MIN))
        thr = jnp.maximum(thr, jnp.int32(KEY_NEG_INF + 1))
        need = float(topk) - jnp.sum(jnp.where(key > thr, 1.0, 0.0))
        n_bits = max(1, int(rows * PAGE_SIZE).bit_length())

        def cut_body(it, cut):
            cand = cut + (jnp.int32(1) << (n_bits - 1 - it))
            n_eq = jnp.sum(jnp.where((key == thr) & (pos < cand), 1.0, 0.0))
            return jnp.where(n_eq <= need, cand, cut)

        cut = lax.fori_loop(0, n_bits, cut_body, jnp.int32(0))
        sel_ref[0] = jnp.where((key > thr) | ((key == thr) & (pos < cut)), 1.0, 0.0)


def _idx_sample(page_table, qix, wix, knew, kc, pg, topk):
    nb, n_pages = page_table.shape
    rows = -(-(n_pages + 1) // 8) * 8
    per_b = lambda shp: pl.BlockSpec((1,) + shp, lambda b, j, pt: (b, 0, 0))
    grid_spec = pltpu.PrefetchScalarGridSpec(
        num_scalar_prefetch=1,
        grid=(nb, n_pages // pg),
        in_specs=[per_b((IDX_HEADS, IDX_DK)), per_b((IDX_HEADS, 1)), per_b((1, IDX_DK))]
                 + _page_specs((IDX_DK, PAGE_SIZE), pg, pg),
        out_specs=per_b((rows, PAGE_SIZE)),
        scratch_shapes=[pltpu.VMEM((rows, PAGE_SIZE), jnp.float32)])
    return pl.pallas_call(
        functools.partial(_idx_sample_kernel, pg=pg, n_pages=n_pages, topk=topk),
        grid_spec=grid_spec,
        out_shape=jax.ShapeDtypeStruct((nb, rows, PAGE_SIZE), jnp.float32),
        compiler_params=_cparams(("arbitrary", "arbitrary")),
        name="idx_sample",
    )(page_table, qix, wix, knew, *([kc] * pg))


def _sa_sample_kernel(pt_ref, q_ref, kn_ref, vn_ref, sel_ref, bl_ref, cf_ref, b0_ref, gm_ref, *rest,
                      pg, n_pages):
    k_refs, v_refs = rest[:pg], rest[pg:2 * pg]
    o_ref, s_ref, p_ref, acc_ref = rest[2 * pg:]
    j = pl.program_id(1)
    ns = n_pages // pg
    scale = SA_DH ** -0.5
    r, cols = gm_ref.shape
    lane = lax.broadcasted_iota(jnp.int32, (r, cols), 1)

    @pl.when(j < ns)
    def _():
        q = q_ref[0]
        qb = _bf(q)
        for u in range(pg):
            page = j * pg + u
            s = _dot_nt(qb, _bf(k_refs[u][0])) * scale
            s = s + jnp.where(page == (n_pages - 1), bl_ref[...], cf_ref[...])
            keep = (sel_ref[0, pl.ds(page, 1), :] > 0.0) & (gm_ref[...] > 0.0)
            s_ref[page] = jnp.where(keep, s, NEG)

        @pl.when(j == 0)
        def _():
            on = sel_ref[0, n_pages:n_pages + 1, 0:1] > 0.0
            s_new = jnp.sum(_rounded(q) * _rounded(kn_ref[0]), axis=-1, keepdims=True) * scale + b0_ref[...]
            s_ref[n_pages] = jnp.where((lane == 0) & on, s_new, NEG)

    @pl.when(j == ns)
    def _():
        p_ref[...] = _bf(_softmax_pages(s_ref))
        acc_ref[...] = p_ref[n_pages][:, 0:1].astype(jnp.float32) * _rounded(vn_ref[0])

    @pl.when(j >= ns)
    def _():
        acc = acc_ref[...]
        for u in range(pg):
            acc = acc + _dot(p_ref[(j - ns) * pg + u], _bf(v_refs[u][0]))
        acc_ref[...] = acc

    @pl.when(j == pl.num_programs(1) - 1)
    def _():
        o_ref[0] = acc_ref[...]


def _sa_sample(page_table, q, knew, vnew, sel2, bias_last2, cfar, bias0, group_mask, k2, v2, pg):
    nb, n_pages = page_table.shape
    r, dh = q.shape[1:]
    srows, cols = sel2.shape[1:]
    ns = n_pages // pg
    full = lambda shp: pl.BlockSpec(shp, lambda b, j, pt: (0,) * len(shp))
    per_b = lambda shp: pl.BlockSpec((1,) + shp, lambda b, j, pt: (b, 0, 0))
    grid_spec = pltpu.PrefetchScalarGridSpec(
        num_scalar_prefetch=1,
        grid=(nb, 2 * ns),
        in_specs=[per_b((r, dh)), per_b((r, dh)), per_b((r, dh)), per_b((srows, cols)),
                  full((r, cols)), full((r, 1)), full((r, 1)), full((r, cols))]
                 + _kv_page_specs((cols, dh), (cols, dh), pg, ns),
        out_specs=per_b((r, dh)),
        scratch_shapes=[pltpu.VMEM((n_pages + 1, r, cols), jnp.float32),
                        pltpu.VMEM((n_pages + 1, r, cols), BF16),
                        pltpu.VMEM((r, dh), jnp.float32)])
    return pl.pallas_call(
        functools.partial(_sa_sample_kernel, pg=pg, n_pages=n_pages),
        grid_spec=grid_spec,
        out_shape=jax.ShapeDtypeStruct((nb, r, dh), jnp.float32),
        compiler_params=_cparams(("arbitrary", "arbitrary")),
        name="sa_sample",
    )(page_table, q, knew, vnew, sel2, bias_last2, cfar, bias0, group_mask, *([k2] * pg), *([v2] * pg))


ROW_TILE = 256
COMBINE_TILE = 128
EXPERT_BLOCK = 256
TAIL_ROWS = 256
PAGES_PER_STEP_DA = 8
PAGES_PER_STEP_SA = 16
PAGES_PER_STEP_IDX = 32


def _rms_rows(x, g):
    return x * lax.rsqrt(jnp.mean(x * x, axis=-1, keepdims=True) + EPS) * g


def _pad_rows(x, rows):
    return jnp.pad(x, ((0, rows - x.shape[0]), (0, 0)))


def _tile_major(dest, tn):
    k, n = dest.shape
    return dest.reshape(k, n // tn, tn).transpose(1, 0, 2).reshape(-1)


def kernel(x_prompt, x_sample, c_prompt, c_sample, cache_da_k, cache_da_v, cache_sa_k, cache_sa_v, cache_idx_k, page_table, rel_bias_table, w_ada, b_ada, g_attn, g_ffn, w_in, lambda_q1, lambda_k1, lambda_q2, lambda_k2, g_subln, w_proj_da, w_proj_sa, w_out, w_router, b_router, w_gate, w_up, w_down, w_sh_gate, w_sh_up, w_sh_down, g_final):
    f32, bf16 = jnp.float32, BF16
    nb, t, d = x_prompt.shape
    nbs, dec_seq, _ = x_sample.shape
    assert dec_seq == 1
    depth = w_in.shape[0]
    n_pages = page_table.shape[1]
    past_len = n_pages * PAGE_SIZE
    n = nb * t
    n_pad = n + TAIL_ROWS
    tq = min(256, t)
    assert tq >= MAX_DISTANCE and t % tq == 0 and n % ROW_TILE == 0 and nbs <= TAIL_ROWS
    assert TAIL_ROWS == ROW_TILE and EXPERT_BLOCK % 8 == 0
    topk_p = min(SA_TOPK_MAX, t // 4)
    topk_s = min(SA_TOPK_MAX, (past_len + dec_seq) // 4)
    assert topk_p <= tq

    sizes = [DA_HEADS * 2 * DA_DK, DA_HEADS * 2 * DA_DK, DA_HEADS * DA_DV, SA_HEADS * SA_DH,
             SA_KV_HEADS * SA_DH, SA_KV_HEADS * SA_DH, IDX_HEADS * IDX_DK, IDX_DK, IDX_HEADS, d, d]
    offs = [sum(sizes[:i]) for i in range(len(sizes) + 1)]
    (o_qda, o_kda, o_vda, o_qsa, o_ksa, o_vsa, o_qix, o_kix, o_wix, o_gda, o_gsa, _) = offs
    misc_w = 2 * LANES

    table = rel_bias_table.astype(f32)
    cfar = table[N_BUCKETS - 1]
    tiles_da = _near_tiles(table[:, :DA_HEADS], tq)
    tiles_sa = _near_tiles(table[:, DA_HEADS:], tq)
    last_dist = past_len - ((n_pages - 1) * PAGE_SIZE + jnp.arange(PAGE_SIZE, dtype=jnp.int32))
    bias_last = _bias_by_distance(table, last_dist)
    bias0 = _bias_by_distance(table, jnp.zeros((1,), jnp.int32))
    rep2 = lambda a: jnp.repeat(a, 2, axis=0)

    xp = x_prompt.reshape(n, d)
    xs = x_sample.reshape(nbs, d)
    c_all = jnp.concatenate([c_prompt, c_sample], axis=0)
    leaves_p, leaves_s = [], []
    for l in range(depth):
        lam_init = 0.8 - 0.6 * math.exp(-0.3 * l)
        lam = (jnp.exp(jnp.sum(lambda_q1[l].astype(f32) * lambda_k1[l].astype(f32)))
               - jnp.exp(jnp.sum(lambda_q2[l].astype(f32) * lambda_k2[l].astype(f32))) + lam_init)
        mod = _linear_small(c_all, w_ada[l], b_ada[l], silu_in=True)
        sh1, sc1, gt1, sh2, sc2, gt2 = jnp.split(mod[:nb], 6, axis=-1)
        sh1s, sc1s, gt1s, sh2s, sc2s, gt2s = jnp.split(mod[nb:], 6, axis=-1)
        w = w_in[l]

        h = _norm_mod(xp, g_attn[l], sc1, sh1, t, ROW_TILE)
        q_da, = _mm(h, w, o_qda, sizes[0], (bf16,), scale=DA_DK ** -0.5)
        k_da, k_da_b = _mm(h, w, o_kda, sizes[1], (f32, bf16))
        v_da, v_da_b = _mm(h, w, o_vda, sizes[2], (f32, bf16))
        q_sa, = _mm(h, w, o_qsa, sizes[3], (bf16,), scale=SA_DH ** -0.5)
        k_sa, k_sa_b = _mm(h, w, o_ksa, sizes[4], (f32, bf16))
        v_sa, v_sa_b = _mm(h, w, o_vsa, sizes[5], (f32, bf16))
        q_ix, = _mm(h, w, o_qix, sizes[6], (bf16,), tn=512)
        misc, misc_b = _mm(h, w, o_kix, misc_w, (f32, bf16))
        gates, = _mm(h, w[:, o_gda:], 0, 2 * d, (bf16,), sigmoid=True)
        k_ix = misc[:, :IDX_DK]

        o_da = _da_prompt(q_da, k_da_b, v_da_b, tiles_da, cfar[:DA_HEADS], lam.reshape(1),
                          g_subln[l].astype(f32), nb, t, tq, 1.0 - lam_init)
        o_sa = _sa_prompt(q_sa, q_ix, misc, misc_b, k_sa_b, v_sa_b, tiles_sa, cfar[DA_HEADS:], nb, t, tq, topk_p)
        m = _proj_gate(o_da, o_sa, w_proj_da[l], w_proj_sa[l], gates, d)
        x1 = _mm_resid(m, w_out[l], xp, gt1, t)

        hs = _rms_rows(xs, g_attn[l].astype(f32)) * (1.0 + sc1s) + sh1s
        ps = _linear_small(hs, w)
        seg = lambda i: ps[:, offs[i]:offs[i + 1]]
        q_da_s, k_da_s, v_da_s, q_sa_s, k_sa_s, v_sa_s, q_ix_s, k_ix_s, w_ix_s, gda_s, gsa_s = [
            seg(i) for i in range(11)]

        r_da = 2 * DA_HEADS
        q16 = (q_da_s * DA_DK ** -0.5).reshape(nbs, DA_HEADS, 2, DA_DK).transpose(0, 2, 1, 3).reshape(nbs, r_da, DA_DK)
        blk_of_row = 2 * (jnp.arange(r_da) % DA_HEADS) + jnp.arange(r_da) // DA_HEADS
        place = (blk_of_row[:, None] == jnp.arange(r_da)[None, :]).astype(f32)
        qbd_da = (place[None, :, :, None] * q16[:, :, None, :]).reshape(nbs, r_da, -1)
        both = lambda a: jnp.concatenate([a, a], axis=0)
        kt_da = cache_da_k[l].transpose(0, 2, 3, 4, 1).reshape(-1, DA_HEADS * 2 * DA_DK, PAGE_SIZE)
        v2_da = cache_da_v[l].reshape(-1, PAGE_SIZE * DA_HEADS, DA_DV)
        v_rows = jnp.pad(v_da_s.reshape(nbs, DA_HEADS, DA_DV), ((0, 0), (0, r_da - DA_HEADS), (0, 0)))
        a_da = _da_sample(page_table, lam.reshape(1), qbd_da, k_da_s[:, None, :], v_rows,
                          both(bias_last[:DA_HEADS]), both(cfar[:DA_HEADS, None]), both(bias0[:DA_HEADS]),
                          kt_da, v2_da, math.gcd(PAGES_PER_STEP_DA, n_pages))
        o_da_s = a_da[:, :DA_HEADS]
        o_da_s = _rms_rows(o_da_s, g_subln[l].astype(f32)) * (1.0 - lam_init)

        sel = _idx_sample(page_table, q_ix_s.reshape(nbs, IDX_HEADS, IDX_DK), w_ix_s[:, :, None],
                          k_ix_s[:, None, :], cache_idx_k[l].transpose(0, 2, 1),
                          math.gcd(PAGES_PER_STEP_IDX, n_pages), topk_s)
        r_sa = 2 * SA_HEADS
        kv_of_row = jnp.minimum(jnp.arange(r_sa) // SA_GROUP, SA_KV_HEADS - 1)
        pad_sa = lambda a: jnp.pad(a, ((0, 0), (0, r_sa - SA_HEADS), (0, 0)))
        rep_kv = lambda a: jnp.repeat(a, SA_KV_HEADS, axis=-1)
        col_kv = jnp.arange(PAGE_SIZE * SA_KV_HEADS) % SA_KV_HEADS
        k2_sa = cache_sa_k[l].reshape(-1, PAGE_SIZE * SA_KV_HEADS, SA_DH)
        v2_sa = cache_sa_v[l].reshape(-1, PAGE_SIZE * SA_KV_HEADS, SA_DH)
        a_sa = _sa_sample(page_table, pad_sa(q_sa_s.reshape(nbs, SA_HEADS, SA_DH)),
                          k_sa_s.reshape(nbs, SA_KV_HEADS, SA_DH)[:, kv_of_row],
                          v_sa_s.reshape(nbs, SA_KV_HEADS, SA_DH)[:, kv_of_row],
                          rep_kv(sel), rep_kv(_pad_rows(bias_last[DA_HEADS:], r_sa)),
                          _pad_rows(cfar[DA_HEADS:, None], r_sa), _pad_rows(bias0[DA_HEADS:], r_sa),
                          (col_kv[None, :] == kv_of_row[:, None]).astype(f32),
                          k2_sa, v2_sa, math.gcd(PAGES_PER_STEP_SA, n_pages))
        o_sa_s = a_sa[:, :SA_HEADS]

        pda = _linear_small(o_da_s.reshape(nbs, -1), w_proj_da[l])
        psa = _linear_small(o_sa_s.reshape(nbs, -1), w_proj_sa[l])
        ms = _sigmoid(gda_s) * pda + _sigmoid(gsa_s) * psa
        x1s = xs + gt1s * _linear_small(ms, w_out[l])
        h2s = _rms_rows(x1s, g_ffn[l].astype(f32)) * (1.0 + sc2s) + sh2s
        lg_s = _linear_small(h2s, w_router[l])

        h2_all, lg_all = _norm_router(x1, g_ffn[l], sc2, sh2, w_router[l].T,
                                      _pad_rows(h2s.astype(bf16), TAIL_ROWS),
                                      _pad_rows(lg_s, TAIL_ROWS).T, t, ROW_TILE)
        eidx, wts, rank, cnt = _route(lg_all, b_router[l], ROW_TILE)
        cnt_tile = cnt[:, :, 0]
        total = jnp.sum(cnt_tile, axis=0)
        padded = jnp.ceil(total / EXPERT_BLOCK) * EXPERT_BLOCK
        pends = jnp.cumsum(padded)
        pstart = pends - padded
        base = pstart[None, :] + jnp.cumsum(cnt_tile, axis=0) - cnt_tile
        dest = _dest(eidx, rank, jnp.broadcast_to(base[:, :, None], base.shape + (LANES,)), ROW_TILE)
        n_blk = -(-(n_pad * TOP_K) // EXPERT_BLOCK) + N_EXPERTS
        blk_start = (jnp.arange(n_blk) * EXPERT_BLOCK).astype(f32)
        blk_e = jnp.minimum(jnp.sum(pends[None, :] <= blk_start[:, None], axis=1), N_EXPERTS - 1).astype(jnp.int32)
        n_used = (pends[-1] / EXPERT_BLOCK).astype(jnp.int32).reshape(1)
        n_rows = n_blk * EXPERT_BLOCK
        pad_start = jnp.concatenate([pstart + total, pends[-1:]]).astype(jnp.int32)
        pad_cnt = jnp.concatenate([padded - total, (n_rows - pends[-1:]) / 8]).astype(jnp.int32)
        xs_sorted = _dispatch(h2_all, _tile_major(dest, ROW_TILE), pad_start, pad_cnt, n_rows, ROW_TILE)
        owns = padded > 0
        e_ids = jnp.arange(N_EXPERTS, dtype=jnp.int32)
        later = jnp.where(owns[None, :] & (e_ids[None, :] > e_ids[:, None]), e_ids[None, :], N_EXPERTS)
        next_owner = jnp.min(later, axis=1)
        next_owner = jnp.where(next_owner < N_EXPERTS, next_owner, -1).astype(jnp.int32)
        slot_of_e = ((jnp.cumsum(owns.astype(jnp.int32)) - 1) % 2).astype(jnp.int32)
        ys = _experts(xs_sorted, blk_e, n_used, next_owner[blk_e], slot_of_e[blk_e],
                      w_gate[l], w_up[l], w_down[l], EXPERT_BLOCK)
        shared = _ffn_shared(h2_all, w_sh_gate[l], w_sh_up[l], w_sh_down[l], ROW_TILE)
        dest_c = _tile_major(dest, COMBINE_TILE)
        wts_tok = wts.T
        last = l == depth - 1
        xp = _combine(dest_c, ys, wts_tok, shared, x1, gt2[:, None, :], g_final.astype(f32),
                      0, COMBINE_TILE, last)
        tail = _combine(dest_c, ys, wts_tok, shared, _pad_rows(x1s, TAIL_ROWS),
                        _pad_rows(gt2s, TAIL_ROWS).reshape(-1, COMBINE_TILE, d), g_final.astype(f32),
                        n // COMBINE_TILE, COMBINE_TILE, last)
        xs = tail[:nbs]
        leaves_p.append((k_da, v_da, k_sa, v_sa, k_ix))
        leaves_s.append((k_da_s, v_da_s, k_sa_s, v_sa_s, k_ix_s))

    shapes = [(DA_HEADS, 2, DA_DK), (DA_HEADS, DA_DV), (SA_KV_HEADS, SA_DH), (SA_KV_HEADS, SA_DH), (IDX_DK,)]
    out_p = [jnp.stack([lv[i].reshape((nb, t) + shapes[i]) for lv in leaves_p]) for i in range(5)]
    out_s = [jnp.stack([lv[i].reshape((nbs, dec_seq) + shapes[i]) for lv in leaves_s]) for i in range(5)]
    return (xp.reshape(nb, t, d), xs.reshape(nbs, dec_seq, d), *out_p, *out_s)
```

```python
import functools
import math

import jax
import jax.numpy as jnp
from jax import lax
from jax.experimental import pallas as pl
from jax.experimental.pallas import tpu as pltpu

DA_HEADS = 8
DA_DK = 64
DA_DV = 2 * DA_DK
SA_HEADS = 8
SA_KV_HEADS = 2
SA_DH = 128
SA_GROUP = SA_HEADS // SA_KV_HEADS
IDX_HEADS = 16
IDX_DK = 64
SA_TOPK_MAX = 256
N_BUCKETS = 32
MAX_DISTANCE = 128
N_EXPERTS = 64
N_GROUPS = 8
GROUP_SIZE = N_EXPERTS // N_GROUPS
TOPK_GROUPS = 4
TOP_K = 8
ROUTED_SCALE = 2.5
PAGE_SIZE = 128
EPS = 1e-6

LANES = 128
VMEM_LIMIT = 56 * 1024 * 1024

BF16 = jnp.bfloat16
NEG = -1e30
INT_MIN = -(2 ** 31)
KEY_NEG_INF = (0xFF800000 ^ 0x7FFFFFFF) - (1 << 32)

_NT = (((1,), (1,)), ((), ()))


def _cparams(sem):
    return pltpu.CompilerParams(dimension_semantics=sem, vmem_limit_bytes=VMEM_LIMIT)


def _dot(a, b):
    return jnp.dot(a, b, preferred_element_type=jnp.float32)


def _dot_nt(a, b):
    return lax.dot_general(a, b, _NT, preferred_element_type=jnp.float32)


def _bf(x):
    return x.astype(BF16)


def _rounded(x):
    return x.astype(BF16).astype(jnp.float32)


def _sigmoid(x):
    return 1.0 / (1.0 + jnp.exp(-x))


def _silu(x):
    return x * _sigmoid(x)


def _float_key(s):
    b = pltpu.bitcast(s, jnp.int32)
    return b ^ ((b >> 31) & jnp.int32(0x7FFFFFFF))


def _linear_small_kernel(x_ref, w_ref, b_ref, o_ref, *, silu_in):
    x = x_ref[...]
    if silu_in:
        x = _silu(x)
    o_ref[...] = _dot(_bf(x), _bf(w_ref[...])) + b_ref[...]


def _linear_small(x, w, b=None, *, silu_in=False, tn=512):
    m0, k = x.shape
    m = -(-m0 // 16) * 16
    x = jnp.pad(x, ((0, m - m0), (0, 0)))
    n = w.shape[1]
    tn = min(tn, n)
    if b is None:
        b = jnp.zeros((1, n), jnp.float32)
    out = pl.pallas_call(
        functools.partial(_linear_small_kernel, silu_in=silu_in),
        grid=(pl.cdiv(n, tn),),
        in_specs=[pl.BlockSpec((m, k), lambda j: (0, 0)),
                  pl.BlockSpec((k, tn), lambda j: (0, j)),
                  pl.BlockSpec((1, tn), lambda j: (0, j))],
        out_specs=pl.BlockSpec((m, tn), lambda j: (0, j)),
        out_shape=jax.ShapeDtypeStruct((m, n), jnp.float32),
        compiler_params=_cparams(("arbitrary",)),
        name="linear_small",
    )(x, w, b.reshape(1, n))
    return out[:m0]


def _norm_mod_kernel(x_ref, g_ref, sc_ref, sh_ref, o_ref):
    x = x_ref[...]
    y = x * lax.rsqrt(jnp.mean(x * x, axis=-1, keepdims=True) + EPS) * g_ref[...]
    o_ref[...] = (y * (1.0 + sc_ref[0]) + sh_ref[0]).astype(o_ref.dtype)


def _norm_mod(x, g, sc, sh, rows_per_batch, tm):
    n, d = x.shape
    per = rows_per_batch // tm
    return pl.pallas_call(
        _norm_mod_kernel,
        grid=(n // tm,),
        in_specs=[pl.BlockSpec((tm, d), lambda i: (i, 0)),
                  pl.BlockSpec((1, d), lambda i: (0, 0)),
                  pl.BlockSpec((1, 1, d), lambda i: (i // per, 0, 0)),
                  pl.BlockSpec((1, 1, d), lambda i: (i // per, 0, 0))],
        out_specs=pl.BlockSpec((tm, d), lambda i: (i, 0)),
        out_shape=jax.ShapeDtypeStruct((n, d), BF16),
        compiler_params=_cparams(("arbitrary",)),
        name="norm_mod",
    )(x, g.reshape(1, d), sc[:, None, :], sh[:, None, :])


def _mm_kernel(x_ref, w_ref, *rest, scale, sigmoid, n_out):
    o_refs, wbf_ref = rest[:n_out], rest[n_out]

    @pl.when(pl.program_id(1) == 0)
    def _():
        wbf_ref[...] = w_ref[...].astype(BF16)

    acc = _dot(x_ref[...], wbf_ref[...])
    if scale != 1.0:
        acc = acc * scale
    if sigmoid:
        acc = _sigmoid(acc)
    for o in o_refs:
        o[...] = acc.astype(o.dtype)


def _mm(x, w, col0, ncols, out_dtypes, *, scale=1.0, sigmoid=False, tm=512, tn=1024):
    m, k = x.shape
    tn = min(tn, ncols)
    tm = min(tm, m)
    assert col0 % tn == 0 and ncols % tn == 0 and m % tm == 0
    jb = col0 // tn
    outs = pl.pallas_call(
        functools.partial(_mm_kernel, scale=scale, sigmoid=sigmoid, n_out=len(out_dtypes)),
        grid=(ncols // tn, m // tm),
        in_specs=[pl.BlockSpec((tm, k), lambda j, i: (i, 0)),
                  pl.BlockSpec((k, tn), lambda j, i: (0, jb + j))],
        out_specs=[pl.BlockSpec((tm, tn), lambda j, i: (i, j)) for _ in out_dtypes],
        out_shape=[jax.ShapeDtypeStruct((m, ncols), dt) for dt in out_dtypes],
        scratch_shapes=[pltpu.VMEM((k, tn), BF16)],
        compiler_params=_cparams(("arbitrary", "arbitrary")),
        name="mm_cols",
    )(x, w)
    return outs


def _rel_bucket(dist):
    max_exact = N_BUCKETS // 2
    d = jnp.maximum(dist, 0)
    large = max_exact + (jnp.log(jnp.maximum(d, 1).astype(jnp.float32) / max_exact)
                         / math.log(MAX_DISTANCE / max_exact)
                         * (N_BUCKETS - max_exact)).astype(jnp.int32)
    large = jnp.minimum(large, N_BUCKETS - 1)
    return jnp.where(d < max_exact, d, large)


def _bias_by_distance(table, dists):
    return table[_rel_bucket(dists)].astype(jnp.float32).T


def _toeplitz_kernel(u_ref, o_ref):
    t = o_ref.shape[2]
    x = jnp.broadcast_to(u_ref[0], (t, 2 * t))
    o_ref[0, 0] = pltpu.roll(x, 0, 1, stride=1, stride_axis=0)[:, :t]


def _near_tiles(table, t):
    nh = table.shape[1]
    k = jnp.arange(2 * t, dtype=jnp.int32)
    gens = []
    for off in (0, t):
        d = jnp.where(k < t, off - k, off + 2 * t - k)
        gens.append(jnp.where(d[None] >= 0, _bias_by_distance(table, d), NEG))
    u = jnp.stack(gens, axis=1).reshape(nh * 2, 1, 2 * t)
    return pl.pallas_call(
        _toeplitz_kernel,
        grid=(nh, 2),
        in_specs=[pl.BlockSpec((1, 1, 2 * t), lambda h, o: (h * 2 + o, 0, 0))],
        out_specs=pl.BlockSpec((1, 1, t, t), lambda h, o: (h, o, 0, 0)),
        out_shape=jax.ShapeDtypeStruct((nh, 2, t, t), jnp.float32),
        compiler_params=_cparams(("arbitrary", "arbitrary")),
        name="bias_tiles",
    )(u)


def _fold_lanes(x, op):
    out = x[:, :LANES]
    for c in range(1, x.shape[1] // LANES):
        out = op(out, x[:, c * LANES:(c + 1) * LANES])
    return out


def _pairwise(n, fn):
    def body(i, carry):
        fn(2 * i)
        fn(2 * i + 1)
        return carry

    lax.fori_loop(0, n // 2, body, 0)

    @pl.when(n % 2 == 1)
    def _():
        fn(n - 1)


def _da_prompt_kernel(cfar_ref, lam_ref, q_ref, k_ref, v_ref, tile_ref, g_ref, o_ref,
                      s_ref, mpart_ref, shift_ref, lpart_ref, acc_ref, *, tq, out_scale):
    h = pl.program_id(1)
    qi = pl.program_id(2)
    q = q_ref[...]
    lane = lax.broadcasted_iota(jnp.int32, q.shape, 1)
    zero = jnp.zeros_like(q)
    q2 = jnp.concatenate([jnp.where(lane < DA_DK, q, zero), jnp.where(lane >= DA_DK, q, zero)], axis=0)
    cfar = cfar_ref[h]
    n_far = jnp.maximum(qi - 1, 0)
    r2 = 2 * tq

    def chunk_rows(kc):
        return pl.ds(pl.multiple_of(kc * tq, tq), tq)

    def scores(kc, bias):
        s = _dot_nt(q2, k_ref[chunk_rows(kc), :])
        if bias is not None:
            s = s + jnp.concatenate([bias, bias], axis=0)
        s_ref[kc] = s
        mpart_ref[...] = jnp.maximum(mpart_ref[...], _fold_lanes(s, jnp.maximum))

    mpart_ref[...] = jnp.full((r2, LANES), NEG, jnp.float32)
    _pairwise(n_far, lambda kc: scores(kc, None))
    m_far = jnp.max(mpart_ref[...], axis=-1, keepdims=True) + cfar
    mpart_ref[...] = jnp.full((r2, LANES), NEG, jnp.float32)

    @pl.when(qi >= 1)
    def _():
        scores(qi - 1, tile_ref[0, 1])

    scores(qi, tile_ref[0, 0])
    m = jnp.maximum(m_far, jnp.max(mpart_ref[...], axis=-1, keepdims=True))
    shift_ref[0] = jnp.broadcast_to(m - cfar, (r2, LANES))
    shift_ref[1] = jnp.broadcast_to(m, (r2, LANES))

    lpart_ref[...] = jnp.zeros((r2, LANES), jnp.float32)
    acc_ref[...] = jnp.zeros((r2, DA_DV), jnp.float32)

    def weights(kc, which):
        s = s_ref[kc]
        sh = shift_ref[which]
        ps = [jnp.exp(s[:, c * LANES:(c + 1) * LANES] - sh) for c in range(tq // LANES)]
        tot = ps[0]
        for pc in ps[1:]:
            tot = tot + pc
        lpart_ref[...] = lpart_ref[...] + tot
        p = jnp.concatenate(ps, axis=1).astype(BF16)
        acc_ref[...] = acc_ref[...] + _dot(p, v_ref[chunk_rows(kc), :])

    _pairwise(n_far, lambda kc: weights(kc, 0))

    @pl.when(qi >= 1)
    def _():
        weights(qi - 1, 1)

    weights(qi, 1)

    lam = lam_ref[0]
    a = acc_ref[...] / jnp.sum(lpart_ref[...], axis=-1, keepdims=True)
    o = a[:tq] - lam * a[tq:]
    o = o * lax.rsqrt(jnp.mean(o * o, axis=-1, keepdims=True) + EPS) * g_ref[...]
    o_ref[...] = (o * out_scale).astype(o_ref.dtype)


def _da_prompt(q, k, v, tiles, cfar, lam, g_subln, nb, t, tq, out_scale):
    n = q.shape[0]
    nq = t // tq
    grid_spec = pltpu.PrefetchScalarGridSpec(
        num_scalar_prefetch=0,
        grid=(nb, DA_HEADS, nq),
        in_specs=[pl.BlockSpec(memory_space=pltpu.SMEM),
                  pl.BlockSpec(memory_space=pltpu.SMEM),
                  pl.BlockSpec((tq, LANES), lambda b, h, i: (b * nq + i, h)),
                  pl.BlockSpec((t, LANES), lambda b, h, i: (b, h)),
                  pl.BlockSpec((t, LANES), lambda b, h, i: (b, h)),
                  pl.BlockSpec((1, 2, tq, tq), lambda b, h, i: (h, 0, 0, 0)),
                  pl.BlockSpec((1, DA_DV), lambda b, h, i: (0, 0))],
        out_specs=pl.BlockSpec((tq, LANES), lambda b, h, i: (b * nq + i, h)),
        scratch_shapes=[pltpu.VMEM((nq, 2 * tq, tq), jnp.float32),
                        pltpu.VMEM((2 * tq, LANES), jnp.float32),
                        pltpu.VMEM((2, 2 * tq, LANES), jnp.float32),
                        pltpu.VMEM((2 * tq, LANES), jnp.float32),
                        pltpu.VMEM((2 * tq, DA_DV), jnp.float32)])
    return pl.pallas_call(
        functools.partial(_da_prompt_kernel, tq=tq, out_scale=out_scale),
        grid_spec=grid_spec,
        out_shape=jax.ShapeDtypeStruct((n, DA_HEADS * DA_DV), BF16),
        compiler_params=_cparams(("arbitrary", "arbitrary", "arbitrary")),
        name="da_prompt",
    )(cfar, lam, q, k, v, tiles, g_subln.reshape(1, DA_DV))


def _sa_prompt_kernel(cfar_ref, qs_ref, qx_ref, mq_ref, mk_ref, ks_ref, vs_ref, tile_ref, o_ref,
                      k2_ref, key_ref, hi_ref, lo_ref, cut_ref, s_ref, mpart_ref, shift_ref, lpart_ref,
                      acc_ref, *, tq, topk):
    qi = pl.program_id(1)
    n_chunks = qi + 1
    t = mk_ref.shape[0]

    @pl.when(qi == 0)
    def _():
        kix = mk_ref[:, :LANES].astype(jnp.float32)
        lane = lax.broadcasted_iota(jnp.int32, kix.shape, 1)
        k2_ref[0] = jnp.where(lane < IDX_DK, kix, 0.0).astype(BF16)
        k2_ref[1] = jnp.where(lane >= IDX_DK, pltpu.roll(kix, IDX_DK, axis=1), 0.0).astype(BF16)

    wix = mq_ref[:, IDX_DK:IDX_DK + IDX_HEADS]
    wcols = [wix[:, hh:hh + 1] for hh in range(IDX_HEADS)]
    row = lax.broadcasted_iota(jnp.int32, (tq, tq), 0)
    col = lax.broadcasted_iota(jnp.int32, (tq, tq), 1)

    def score_body(kc, carry):
        rows = pl.ds(pl.multiple_of(kc * tq, tq), tq)
        ke = k2_ref[0, rows, :]
        ko = k2_ref[1, rows, :]
        sc = jnp.zeros((tq, tq), jnp.float32)
        for p in range(IDX_HEADS // 2):
            qp = qx_ref[:, p * LANES:(p + 1) * LANES]
            sc = sc + wcols[2 * p] * jnp.maximum(_dot_nt(qp, ke), 0.0)
            sc = sc + wcols[2 * p + 1] * jnp.maximum(_dot_nt(qp, ko), 0.0)
        sc = sc * (IDX_DK ** -0.5 * IDX_HEADS ** -0.5)
        sc = jnp.where((kc < qi) | (row >= col), sc, -jnp.inf)
        key = _float_key(sc)
        key_ref[kc] = key
        hi_ref[kc] = (key >> 16).astype(jnp.int16)
        return carry

    lax.fori_loop(0, n_chunks, score_body, 0)

    i16 = jnp.int16
    i16_min = -(2 ** 15)
    one_i = jnp.ones((tq, tq), i16)
    zero_i = jnp.zeros((tq, tq), i16)
    ones_col = jnp.ones((tq, LANES), BF16)

    def wide16(x):
        return jnp.concatenate([x] * (tq // LANES), axis=1).astype(i16)

    def count_ge16(ref16, cand):
        c16 = wide16(cand)

        def body(kc, acc):
            return acc + jnp.where(ref16[kc] >= c16, one_i, zero_i)
        acc = lax.fori_loop(0, n_chunks, body, zero_i)
        return _dot(acc.astype(jnp.float32).astype(BF16), ones_col)

    def search16(ref16, base, need):
        def bit_body(it, carry):
            v, cnt_v = carry
            cand = v + (jnp.int32(1) << (15 - it))
            cnt = base + count_ge16(ref16, cand)
            ok = cnt >= need
            return jnp.where(ok, cand, v), jnp.where(ok, cnt, cnt_v)
        v0 = jnp.full((tq, LANES), i16_min, jnp.int32)
        c0 = jnp.full((tq, LANES), 3.0e38, jnp.float32)
        return lax.fori_loop(0, 16, bit_body, (v0, c0))

    zero_cnt = jnp.zeros((tq, LANES), jnp.float32)
    t_hi, _ = search16(hi_ref, zero_cnt, float(topk))
    n_above = jnp.where(t_hi < 2 ** 15 - 1, count_ge16(hi_ref, jnp.minimum(t_hi + 1, 2 ** 15 - 1)), 0.0)
    t_hi16 = wide16(t_hi)

    def lower_body(kc, carry):
        lo = ((key_ref[kc] & jnp.int32(0xFFFF)) - 2 ** 15).astype(i16)
        lo_ref[kc] = jnp.where(hi_ref[kc] == t_hi16, lo, jnp.full((tq, tq), i16_min, i16))
        return carry

    lax.fori_loop(0, n_chunks, lower_body, 0)
    t_lo, cnt_w = search16(lo_ref, n_above, float(topk))
    thr = ((t_hi << 16) + (t_lo + 2 ** 15))[:, :1]
    cnt_thr = cnt_w[:, :1]
    tied = (cnt_thr > float(topk)) & (thr > jnp.int32(KEY_NEG_INF))
    need_tie = jnp.max(jnp.where(tied, 1.0, 0.0)) > 0.0
    thr = jnp.maximum(thr, jnp.int32(KEY_NEG_INF + 1))

    cut_ref[...] = jnp.full((tq, 1), 2 ** 30, jnp.int32)

    @pl.when(need_tie)
    def _():
        def gt_body(kc, acc):
            g = jnp.where(key_ref[kc] > thr, 1.0, 0.0)
            return acc + jnp.sum(g, axis=-1, keepdims=True)
        n_gt = lax.fori_loop(0, n_chunks, gt_body, jnp.zeros((tq, 1), jnp.float32))
        need = float(topk) - n_gt
        n_bits = max(1, int(t).bit_length())

        def cut_body(it, cut):
            cand = cut + (jnp.int32(1) << (n_bits - 1 - it))

            def eq_body(kc, acc):
                pos = kc * tq + col
                e = jnp.where((key_ref[kc] == thr) & (pos < cand), 1.0, 0.0)
                return acc + jnp.sum(e, axis=-1, keepdims=True)
            n_eq = lax.fori_loop(0, n_chunks, eq_body, jnp.zeros((tq, 1), jnp.float32))
            return jnp.where(n_eq <= need, cand, cut)
        cut_ref[...] = lax.fori_loop(0, n_bits, cut_body, jnp.zeros((tq, 1), jnp.int32))

    cut = cut_ref[...]

    n_far = jnp.maximum(qi - 1, 0)
    rg = SA_GROUP * tq

    def chunk_rows(kc):
        return pl.ds(pl.multiple_of(kc * tq, tq), tq)

    for g in range(SA_KV_HEADS):
        heads = [g * SA_GROUP + j for j in range(SA_GROUP)]
        qg = jnp.concatenate([qs_ref[:, hh * LANES:(hh + 1) * LANES] for hh in heads], axis=0)
        cf_rows = jnp.concatenate([jnp.full((tq, 1), cfar_ref[hh], jnp.float32) for hh in heads], axis=0)

        def scores(kc, kind, qg=qg, heads=heads, g=g):
            key = key_ref[kc]
            sel = (key > thr) | ((key == thr) & (kc * tq + col < cut))
            s_all = _dot_nt(qg, ks_ref[chunk_rows(kc), g * SA_DH:(g + 1) * SA_DH])
            parts = []
            for j, hh in enumerate(heads):
                s = s_all[j * tq:(j + 1) * tq]
                if kind is not None:
                    s = s + tile_ref[hh, kind]
                parts.append(jnp.where(sel, s, NEG))
            s = jnp.concatenate(parts, axis=0)
            s_ref[kc] = s
            mpart_ref[...] = jnp.maximum(mpart_ref[...], _fold_lanes(s, jnp.maximum))

        mpart_ref[...] = jnp.full((rg, LANES), NEG, jnp.float32)
        _pairwise(n_far, lambda kc, f=scores: f(kc, None))
        m_far = jnp.max(mpart_ref[...], axis=-1, keepdims=True) + cf_rows
        mpart_ref[...] = jnp.full((rg, LANES), NEG, jnp.float32)

        @pl.when(qi >= 1)
        def _(f=scores):
            f(qi - 1, 1)

        scores(qi, 0)
        m = jnp.maximum(m_far, jnp.max(mpart_ref[...], axis=-1, keepdims=True))
        shift_ref[0] = jnp.broadcast_to(m - cf_rows, (rg, LANES))
        shift_ref[1] = jnp.broadcast_to(m, (rg, LANES))
        lpart_ref[...] = jnp.zeros((rg, LANES), jnp.float32)
        acc_ref[...] = jnp.zeros((rg, SA_DH), jnp.float32)

        def weights(kc, which, g=g):
            s = s_ref[kc]
            sh = shift_ref[which]
            ps = [jnp.exp(s[:, c * LANES:(c + 1) * LANES] - sh) for c in range(tq // LANES)]
            tot = ps[0]
            for pc in ps[1:]:
                tot = tot + pc
            lpart_ref[...] = lpart_ref[...] + tot
            p = jnp.concatenate(ps, axis=1).astype(BF16)
            acc_ref[...] = acc_ref[...] + _dot(p, vs_ref[chunk_rows(kc), g * SA_DH:(g + 1) * SA_DH])

        _pairwise(n_far, lambda kc, f=weights: f(kc, 0))

        @pl.when(qi >= 1)
        def _(f=weights):
            f(qi - 1, 1)

        weights(qi, 1)
        a = acc_ref[...] / jnp.sum(lpart_ref[...], axis=-1, keepdims=True)
        for j, hh in enumerate(heads):
            o_ref[:, hh * SA_DH:(hh + 1) * SA_DH] = a[j * tq:(j + 1) * tq].astype(o_ref.dtype)


def _sa_prompt(q_sa, q_ix, misc_q, misc_k, k_sa, v_sa, tiles, cfar, nb, t, tq, topk):
    n = q_sa.shape[0]
    nq = t // tq
    mw = misc_q.shape[1]
    kvw = SA_KV_HEADS * SA_DH
    once = dict(pipeline_mode=pl.Buffered(1))
    rg = SA_GROUP * tq
    return pl.pallas_call(
        functools.partial(_sa_prompt_kernel, tq=tq, topk=topk),
        grid=(nb, nq),
        in_specs=[pl.BlockSpec(memory_space=pltpu.SMEM),
                  pl.BlockSpec((tq, SA_HEADS * SA_DH), lambda b, i: (b * nq + i, 0)),
                  pl.BlockSpec((tq, IDX_HEADS * IDX_DK), lambda b, i: (b * nq + i, 0)),
                  pl.BlockSpec((tq, mw), lambda b, i: (b * nq + i, 0)),
                  pl.BlockSpec((t, mw), lambda b, i: (b, 0), **once),
                  pl.BlockSpec((t, kvw), lambda b, i: (b, 0), **once),
                  pl.BlockSpec((t, kvw), lambda b, i: (b, 0), **once),
                  pl.BlockSpec((SA_HEADS, 2, tq, tq), lambda b, i: (0, 0, 0, 0), **once)],
        out_specs=pl.BlockSpec((tq, SA_HEADS * SA_DH), lambda b, i: (b * nq + i, 0)),
        out_shape=jax.ShapeDtypeStruct((n, SA_HEADS * SA_DH), BF16),
        scratch_shapes=[pltpu.VMEM((2, t, LANES), BF16),
                        pltpu.VMEM((nq, tq, tq), jnp.int32),
                        pltpu.VMEM((nq, tq, tq), jnp.int16),
                        pltpu.VMEM((nq, tq, tq), jnp.int16),
                        pltpu.VMEM((tq, 1), jnp.int32),
                        pltpu.VMEM((nq, rg, tq), jnp.float32),
                        pltpu.VMEM((rg, LANES), jnp.float32),
                        pltpu.VMEM((2, rg, LANES), jnp.float32),
                        pltpu.VMEM((rg, LANES), jnp.float32),
                        pltpu.VMEM((rg, SA_DH), jnp.float32)],
        compiler_params=_cparams(("arbitrary", "arbitrary")),
        name="sa_prompt",
    )(cfar, q_sa, q_ix, misc_q, misc_k, k_sa, v_sa, tiles)


def _proj_gate_kernel(oda_ref, osa_ref, wpd_ref, wps_ref, gda_ref, gsa_ref, o_ref, wbf_ref):
    @pl.when(pl.program_id(1) == 0)
    def _():
        wbf_ref[0] = wpd_ref[...].astype(BF16)
        wbf_ref[1] = wps_ref[...].astype(BF16)

    a = _dot(oda_ref[...], wbf_ref[0])
    b = _dot(osa_ref[...], wbf_ref[1])
    o_ref[...] = (gda_ref[...].astype(jnp.float32) * a
                  + gsa_ref[...].astype(jnp.float32) * b).astype(o_ref.dtype)


def _proj_gate(o_da, o_sa, w_pd, w_ps, gates, d, tm=512, tn=512):
    n, kd = o_da.shape
    ks = o_sa.shape[1]
    tn = min(tn, d)
    tm = min(tm, n)
    nj = d // tn
    return pl.pallas_call(
        _proj_gate_kernel,
        grid=(nj, n // tm),
        in_specs=[pl.BlockSpec((tm, kd), lambda j, i: (i, 0)),
                  pl.BlockSpec((tm, ks), lambda j, i: (i, 0)),
                  pl.BlockSpec((kd, tn), lambda j, i: (0, j)),
                  pl.BlockSpec((ks, tn), lambda j, i: (0, j)),
                  pl.BlockSpec((tm, tn), lambda j, i: (i, j)),
                  pl.BlockSpec((tm, tn), lambda j, i: (i, nj + j))],
        out_specs=pl.BlockSpec((tm, tn), lambda j, i: (i, j)),
        out_shape=jax.ShapeDtypeStruct((n, d), BF16),
        scratch_shapes=[pltpu.VMEM((2, kd, tn), BF16)],
        compiler_params=_cparams(("arbitrary", "arbitrary")),
        name="proj_gate",
    )(o_da, o_sa, w_pd, w_ps, gates, gates)


def _mm_resid_kernel(m_ref, w_ref, x_ref, gt_ref, o_ref, wbf_ref):
    @pl.when(pl.program_id(1) == 0)
    def _():
        wbf_ref[...] = w_ref[...].astype(BF16)

    o_ref[...] = x_ref[...] + gt_ref[0] * _dot(m_ref[...], wbf_ref[...])


def _mm_resid(m, w, x, gt, rows_per_batch, tm=512, tn=512):
    n, k = m.shape
    d = w.shape[1]
    tn = min(tn, d)
    tm = min(tm, n)
    per = rows_per_batch // tm
    return pl.pallas_call(
        _mm_resid_kernel,
        grid=(d // tn, n // tm),
        in_specs=[pl.BlockSpec((tm, k), lambda j, i: (i, 0)),
                  pl.BlockSpec((k, tn), lambda j, i: (0, j)),
                  pl.BlockSpec((tm, tn), lambda j, i: (i, j)),
                  pl.BlockSpec((1, 1, tn), lambda j, i: (i // per, 0, j))],
        out_specs=pl.BlockSpec((tm, tn), lambda j, i: (i, j)),
        out_shape=jax.ShapeDtypeStruct((n, d), jnp.float32),
        scratch_shapes=[pltpu.VMEM((k, tn), BF16)],
        compiler_params=_cparams(("arbitrary", "arbitrary")),
        name="mm_resid",
    )(m, w, x, gt[:, None, :])


def _norm_router_kernel(x_ref, g_ref, sc_ref, sh_ref, wr_ref, th_ref, tl_ref, h_ref, lg_ref):
    last = pl.num_programs(0) - 1

    @pl.when(pl.program_id(0) < last)
    def _():
        x = x_ref[...]
        y = x * lax.rsqrt(jnp.mean(x * x, axis=-1, keepdims=True) + EPS) * g_ref[...]
        h = y * (1.0 + sc_ref[0]) + sh_ref[0]
        h_ref[...] = h.astype(h_ref.dtype)
        lg_ref[...] = _dot_nt(_bf(wr_ref[...]), _bf(h))

    @pl.when(pl.program_id(0) == last)
    def _():
        h_ref[...] = th_ref[...]
        lg_ref[...] = tl_ref[...]


def _norm_router(x, g, sc, sh, w_router_t, tail_h, tail_lg, rows_per_batch, tm):
    n, d = x.shape
    per = rows_per_batch // tm
    nt = n // tm
    assert tail_h.shape == (tm, d) and tail_lg.shape == (N_EXPERTS, tm)
    row = lambda i: jnp.minimum(i, nt - 1)
    return pl.pallas_call(
        _norm_router_kernel,
        grid=(nt + 1,),
        in_specs=[pl.BlockSpec((tm, d), lambda i: (row(i), 0)),
                  pl.BlockSpec((1, d), lambda i: (0, 0)),
                  pl.BlockSpec((1, 1, d), lambda i: (row(i) // per, 0, 0)),
                  pl.BlockSpec((1, 1, d), lambda i: (row(i) // per, 0, 0)),
                  pl.BlockSpec((N_EXPERTS, d), lambda i: (0, 0)),
                  pl.BlockSpec((tm, d), lambda i: (0, 0)),
                  pl.BlockSpec((N_EXPERTS, tm), lambda i: (0, 0))],
        out_specs=[pl.BlockSpec((tm, d), lambda i: (i, 0)),
                   pl.BlockSpec((N_EXPERTS, tm), lambda i: (0, i))],
        out_shape=[jax.ShapeDtypeStruct((n + tm, d), BF16),
                   jax.ShapeDtypeStruct((N_EXPERTS, n + tm), jnp.float32)],
        compiler_params=_cparams(("arbitrary",)),
        name="norm_router",
    )(x, g.reshape(1, d), sc[:, None, :], sh[:, None, :], w_router_t, tail_h, tail_lg)


def _route_kernel(lg_ref, b_ref, eidx_ref, wts_ref, rank_ref, cnt_ref, *, tn):
    shape = (N_GROUPS, GROUP_SIZE, tn)
    sc = _sigmoid(lg_ref[...])
    biased = sc + b_ref[...]
    e_iota = lax.broadcasted_iota(jnp.int32, shape, 1)
    g_iota3 = lax.broadcasted_iota(jnp.int32, shape, 0)
    flat_iota = g_iota3 * GROUP_SIZE + e_iota
    g_iota = lax.broadcasted_iota(jnp.int32, (N_GROUPS, 1, tn), 0)
    ninf = -jnp.inf

    m1 = jnp.max(biased, axis=1, keepdims=True)
    first = jnp.min(jnp.where(biased == m1, e_iota, GROUP_SIZE), axis=1, keepdims=True)
    m2 = jnp.max(jnp.where(e_iota == first, ninf, biased), axis=1, keepdims=True)
    cur = m1 + m2
    gsel = jnp.zeros((N_GROUPS, 1, tn), jnp.float32)
    for _ in range(TOPK_GROUPS):
        mx = jnp.max(cur, axis=0, keepdims=True)
        idx = jnp.min(jnp.where(cur == mx, g_iota, N_GROUPS), axis=0, keepdims=True)
        hit = g_iota == idx
        gsel = jnp.where(hit, 1.0, gsel)
        cur = jnp.where(hit, ninf, cur)

    cur = jnp.where(gsel > 0.0, biased, ninf)
    hits, ws = [], []
    for k in range(TOP_K):
        mx = jnp.max(jnp.max(cur, axis=1, keepdims=True), axis=0, keepdims=True)
        cand = jnp.where(cur == mx, flat_iota, N_EXPERTS)
        idx = jnp.min(jnp.min(cand, axis=1, keepdims=True), axis=0, keepdims=True)
        hit = flat_iota == idx
        w = jnp.sum(jnp.sum(jnp.where(hit, sc, 0.0), axis=1, keepdims=True), axis=0, keepdims=True)
        eidx_ref[k:k + 1, :] = idx.reshape(1, tn)
        hits.append(hit)
        ws.append(w)
        cur = jnp.where(hit, ninf, cur)
    wsum = ws[0]
    for w in ws[1:]:
        wsum = wsum + w
    for k in range(TOP_K):
        wts_ref[k:k + 1, :] = (ws[k] / wsum * ROUTED_SCALE).reshape(1, tn)

    member = jnp.zeros(shape, jnp.float32)
    for hit in hits:
        member = jnp.where(hit, 1.0, member)
    member2 = member.reshape(N_EXPERTS, tn)
    r = lax.broadcasted_iota(jnp.int32, (tn, tn), 0)
    c = lax.broadcasted_iota(jnp.int32, (tn, tn), 1)
    upper = jnp.where(r < c, 1.0, 0.0).astype(BF16)
    prefix = _dot(member2.astype(BF16), upper).reshape(shape)
    for k in range(TOP_K):
        rk = jnp.sum(jnp.sum(jnp.where(hits[k], prefix, 0.0), axis=1, keepdims=True), axis=0, keepdims=True)
        rank_ref[k:k + 1, :] = rk.reshape(1, tn)
    cnt = jnp.sum(member2, axis=1, keepdims=True)
    cnt_ref[0] = jnp.broadcast_to(cnt, (N_EXPERTS, LANES))


def _route(logits_t, b_router, tn):
    n_pad = logits_t.shape[1]
    nt = n_pad // tn
    lg3 = logits_t.reshape(N_GROUPS, GROUP_SIZE, n_pad)
    b3 = b_router.astype(jnp.float32).reshape(N_GROUPS, GROUP_SIZE, 1)
    row = lambda dt: jax.ShapeDtypeStruct((TOP_K, n_pad), dt)
    return pl.pallas_call(
        functools.partial(_route_kernel, tn=tn),
        grid=(nt,),
        in_specs=[pl.BlockSpec((N_GROUPS, GROUP_SIZE, tn), lambda i: (0, 0, i)),
                  pl.BlockSpec((N_GROUPS, GROUP_SIZE, 1), lambda i: (0, 0, 0))],
        out_specs=[pl.BlockSpec((TOP_K, tn), lambda i: (0, i)),
                   pl.BlockSpec((TOP_K, tn), lambda i: (0, i)),
                   pl.BlockSpec((TOP_K, tn), lambda i: (0, i)),
                   pl.BlockSpec((1, N_EXPERTS, LANES), lambda i: (i, 0, 0))],
        out_shape=[row(jnp.int32), row(jnp.float32), row(jnp.float32),
                   jax.ShapeDtypeStruct((nt, N_EXPERTS, LANES), jnp.float32)],
        compiler_params=_cparams(("arbitrary",)),
        name="route",
    )(lg3, b3)


def _dest_kernel(eidx_ref, rank_ref, base_ref, o_ref, *, tn):
    e_iota = lax.broadcasted_iota(jnp.int32, (N_EXPERTS, tn), 0)
    base = base_ref[0][:, :1]
    for k in range(TOP_K):
        onehot = e_iota == eidx_ref[k:k + 1, :]
        b = jnp.sum(jnp.where(onehot, base, 0.0), axis=0, keepdims=True)
        o_ref[k:k + 1, :] = (b + rank_ref[k:k + 1, :]).astype(jnp.int32)


def _dest(eidx, rank, base, tn):
    n_pad = eidx.shape[1]
    return pl.pallas_call(
        functools.partial(_dest_kernel, tn=tn),
        grid=(n_pad // tn,),
        in_specs=[pl.BlockSpec((TOP_K, tn), lambda i: (0, i)),
                  pl.BlockSpec((TOP_K, tn), lambda i: (0, i)),
                  pl.BlockSpec((1, N_EXPERTS, LANES), lambda i: (i, 0, 0))],
        out_specs=pl.BlockSpec((TOP_K, tn), lambda i: (0, i)),
        out_shape=jax.ShapeDtypeStruct((TOP_K, n_pad), jnp.int32),
        compiler_params=_cparams(("arbitrary",)),
        name="dest",
    )(eidx, rank, base)


def _pack_words(lo_f32, hi_f32):
    lo = lax.shift_right_logical(pltpu.bitcast(lo_f32, jnp.uint32), jnp.uint32(16))
    hi = pltpu.bitcast(hi_f32, jnp.uint32) & jnp.uint32(0xFFFF0000)
    return hi | lo


def _unpack_words(w):
    lo = pltpu.bitcast(lax.shift_left(w, jnp.uint32(16)), jnp.float32)
    hi = pltpu.bitcast(w & jnp.uint32(0xFFFF0000), jnp.float32)
    return lo, hi


def _bf16_exact(x):
    return x.astype(BF16).astype(jnp.float32)


def _dispatch_kernel(pstart_ref, pcnt_ref, h_ref, dest_hbm, xs_hbm, dsm, pk, zrow, sem_d, sem_r,
                     *, tn, nt):
    i = pl.program_id(0)
    half = pk.shape[1]

    @pl.when(i < nt)
    def _():
        cp = pltpu.make_async_copy(dest_hbm.at[pl.ds(i * (TOP_K * tn), TOP_K * tn)], dsm, sem_d)
        cp.start()
        x = h_ref[...]
        pk[...] = _pack_words(x[:, :half].astype(jnp.float32), x[:, half:].astype(jnp.float32))
        cp.wait()

        def body(r, carry):
            for k in range(TOP_K):
                d = dsm[k * tn + r]
                pltpu.make_async_copy(pk.at[pl.ds(r, 1), :], xs_hbm.at[pl.ds(d, 1), :],
                                      sem_r).start(priority=k % 2)
            return carry

        lax.fori_loop(0, tn, body, 0)
        for k in range(TOP_K):
            pltpu.make_async_copy(pk, xs_hbm.at[pl.ds(0, tn), :], sem_r).wait()

    @pl.when(i == nt)
    def _():
        zrow[...] = jnp.zeros(zrow.shape, zrow.dtype)

        def per_expert(e, carry):
            s0 = pstart_ref[e]
            c = pcnt_ref[e]

            def start(r, cc):
                pltpu.make_async_copy(zrow.at[pl.ds(0, 1), :], xs_hbm.at[pl.ds(s0 + r, 1), :], sem_r).start()
                return cc

            def wait(r, cc):
                pltpu.make_async_copy(zrow.at[pl.ds(0, 1), :], xs_hbm.at[pl.ds(s0, 1), :], sem_r).wait()
                return cc

            lax.fori_loop(0, c, start, 0)
            lax.fori_loop(0, c, wait, 0)
            return carry

        lax.fori_loop(0, N_EXPERTS, per_expert, 0)

        t0 = pstart_ref[N_EXPERTS]
        groups = pcnt_ref[N_EXPERTS]
        rows8 = lambda r: pl.ds(pl.multiple_of(t0 + r * 8, 8), 8)

        def tstart(r, cc):
            pltpu.make_async_copy(zrow, xs_hbm.at[rows8(r), :], sem_r).start()
            return cc

        def twait(r, cc):
            pltpu.make_async_copy(zrow, xs_hbm.at[rows8(0), :], sem_r).wait()
            return cc

        lax.fori_loop(0, groups, tstart, 0)
        lax.fori_loop(0, groups, twait, 0)


def _dispatch(h2, dest_flat, pad_start, pad_cnt, n_rows, tn):
    n_pad, d = h2.shape
    nt = n_pad // tn
    grid_spec = pltpu.PrefetchScalarGridSpec(
        num_scalar_prefetch=2,
        grid=(nt + 1,),
        in_specs=[pl.BlockSpec((tn, d), lambda i, a, b: (jnp.minimum(i, nt - 1), 0)),
                  pl.BlockSpec(memory_space=pl.ANY)],
        out_specs=pl.BlockSpec(memory_space=pl.ANY),
        scratch_shapes=[pltpu.SMEM((TOP_K * tn,), jnp.int32),
                        pltpu.VMEM((tn, d // 2), jnp.uint32),
                        pltpu.VMEM((8, d // 2), jnp.uint32),
                        pltpu.SemaphoreType.DMA(()),
                        pltpu.SemaphoreType.DMA(())])
    return pl.pallas_call(
        functools.partial(_dispatch_kernel, tn=tn, nt=nt),
        grid_spec=grid_spec,
        out_shape=jax.ShapeDtypeStruct((n_rows, d // 2), jnp.uint32),
        compiler_params=_cparams(("arbitrary",)),
        name="dispatch",
    )(pad_start, pad_cnt, h2, dest_flat)


def _expert_kernel(blk_e_ref, nused_ref, next_e_ref, slot_ref, x_ref, wg_hbm, wu_hbm, wd_hbm, y_ref,
                   wg_f, wu_f, wd_f, wgb, wub, wdb, sems):
    i = pl.program_id(0)
    nused = nused_ref[0]
    ii = jnp.minimum(i, nused - 1)
    e = blk_e_ref[ii]
    e_prev = blk_e_ref[jnp.maximum(ii - 1, 0)]
    half = x_ref.shape[1]

    def copies(ex, sl):
        return [pltpu.make_async_copy(src.at[ex], dst.at[sl], sems.at[sl, t])
                for t, (src, dst) in enumerate(((wg_hbm, wg_f), (wu_hbm, wu_f), (wd_hbm, wd_f)))]

    @pl.when(i == 0)
    def _():
        for c in copies(e, slot_ref[ii]):
            c.start()

    @pl.when((i < nused) & ((i == 0) | (e != e_prev)))
    def _():
        sl = slot_ref[ii]
        for c in copies(e, sl):
            c.wait()
        nxt = next_e_ref[ii]

        @pl.when(nxt >= 0)
        def _():
            for c in copies(nxt, 1 - sl):
                c.start()

        wgb[...] = wg_f[sl].astype(BF16)
        wub[...] = wu_f[sl].astype(BF16)
        wdb[...] = wd_f[sl].astype(BF16)

    @pl.when(i < nused)
    def _():
        lo, hi = _unpack_words(x_ref[...])
        xl = lo.astype(BF16)
        xh = hi.astype(BF16)
        g = _dot(xl, wgb[:half, :]) + _dot(xh, wgb[half:, :])
        u = _dot(xl, wub[:half, :]) + _dot(xh, wub[half:, :])
        hmid = (_silu(g) * u).astype(BF16)
        y = _dot(hmid, wdb[...])
        y_ref[...] = _pack_words(_bf16_exact(y[:, :half]), _bf16_exact(y[:, half:]))

    @pl.when(i >= nused)
    def _():
        y_ref[...] = jnp.zeros(y_ref.shape, y_ref.dtype)


def _experts(xs, blk_e, nused, next_e, slot, w_gate, w_up, w_down, tb):
    n_rows, half = xs.shape
    _, d, f = w_gate.shape
    nblk = n_rows // tb

    def xmap(i, be, nu, ne, sl):
        return (jnp.minimum(i, nu[0] - 1), 0)

    grid_spec = pltpu.PrefetchScalarGridSpec(
        num_scalar_prefetch=4,
        grid=(nblk,),
        in_specs=[pl.BlockSpec((tb, half), xmap),
                  pl.BlockSpec(memory_space=pl.ANY),
                  pl.BlockSpec(memory_space=pl.ANY),
                  pl.BlockSpec(memory_space=pl.ANY)],
        out_specs=pl.BlockSpec((tb, half), lambda i, be, nu, ne, sl: (i, 0)),
        scratch_shapes=[pltpu.VMEM((2, d, f), jnp.float32),
                        pltpu.VMEM((2, d, f), jnp.float32),
                        pltpu.VMEM((2, f, d), jnp.float32),
                        pltpu.VMEM((d, f), BF16),
                        pltpu.VMEM((d, f), BF16),
                        pltpu.VMEM((f, d), BF16),
                        pltpu.SemaphoreType.DMA((2, 3))])
    return pl.pallas_call(
        _expert_kernel,
        grid_spec=grid_spec,
        out_shape=jax.ShapeDtypeStruct((n_rows, half), jnp.uint32),
        compiler_params=_cparams(("arbitrary",)),
        name="experts",
    )(blk_e, nused, next_e, slot, xs, w_gate, w_up, w_down)


def _ffn_kernel(h_ref, wg_ref, wu_ref, wd_ref, o_ref, wgb, wub, wdb):
    @pl.when(pl.program_id(0) == 0)
    def _():
        wgb[...] = wg_ref[...].astype(BF16)
        wub[...] = wu_ref[...].astype(BF16)
        wdb[...] = wd_ref[...].astype(BF16)

    x = h_ref[...]
    hmid = (_silu(_dot(x, wgb[...])) * _dot(x, wub[...])).astype(BF16)
    o_ref[...] = _dot(hmid, wdb[...]).astype(o_ref.dtype)


def _ffn_shared(h2, wg, wu, wd, tm):
    n_pad, d = h2.shape
    f = wg.shape[1]
    return pl.pallas_call(
        _ffn_kernel,
        grid=(n_pad // tm,),
        in_specs=[pl.BlockSpec((tm, d), lambda i: (i, 0)),
                  pl.BlockSpec((d, f), lambda i: (0, 0)),
                  pl.BlockSpec((d, f), lambda i: (0, 0)),
                  pl.BlockSpec((f, d), lambda i: (0, 0))],
        out_specs=pl.BlockSpec((tm, d), lambda i: (i, 0)),
        out_shape=jax.ShapeDtypeStruct((n_pad, d), BF16),
        scratch_shapes=[pltpu.VMEM((d, f), BF16),
                        pltpu.VMEM((d, f), BF16),
                        pltpu.VMEM((f, d), BF16)],
        compiler_params=_cparams(("arbitrary",)),
        name="ffn_shared",
    )(h2, wg, wu, wd)


def _combine_kernel(dest_hbm, ys_hbm, wts_ref, sh_ref, x_ref, gt_ref, g_ref, o_ref,
                    dsm, buf, sem_d, sem_r, *, tn, tile0, final_norm):
    i = pl.program_id(0)
    cp = pltpu.make_async_copy(dest_hbm.at[pl.ds((tile0 + i) * (TOP_K * tn), TOP_K * tn)], dsm, sem_d)
    cp.start()
    cp.wait()

    def body(r, carry):
        for k in range(TOP_K):
            d = dsm[k * tn + r]
            pltpu.make_async_copy(ys_hbm.at[pl.ds(d, 1), :], buf.at[k, pl.ds(r, 1), :],
                                  sem_r).start(priority=k % 2)
        return carry

    lax.fori_loop(0, tn, body, 0)
    for k in range(TOP_K):
        pltpu.make_async_copy(ys_hbm.at[pl.ds(0, tn), :], buf.at[k], sem_r).wait()

    half = buf.shape[2]
    wts = wts_ref[...]
    acc_lo = jnp.zeros((tn, half), jnp.float32)
    acc_hi = jnp.zeros((tn, half), jnp.float32)
    for k in range(TOP_K):
        lo, hi = _unpack_words(buf[k])
        wk = wts[:, k:k + 1]
        acc_lo = acc_lo + wk * lo
        acc_hi = acc_hi + wk * hi
    sh = sh_ref[...].astype(jnp.float32)
    gt = gt_ref[0]
    x_lo = x_ref[:, :half] + gt[:, :half] * (acc_lo + sh[:, :half])
    x_hi = x_ref[:, half:] + gt[:, half:] * (acc_hi + sh[:, half:])
    if final_norm:
        ms = (jnp.sum(x_lo * x_lo, axis=-1, keepdims=True)
              + jnp.sum(x_hi * x_hi, axis=-1, keepdims=True)) / (2 * half)
        inv = lax.rsqrt(ms + EPS)
        g = g_ref[...]
        x_lo = x_lo * inv * g[:, :half]
        x_hi = x_hi * inv * g[:, half:]
    o_ref[:, :half] = x_lo
    o_ref[:, half:] = x_hi


def _combine(dest_flat, ys, wts_tok, shared, x1, gt3, g_final, tile0, tn, final_norm):
    rows, d = x1.shape
    nt = rows // tn
    gr = gt3.shape[1]
    per = nt // gt3.shape[0]
    return pl.pallas_call(
        functools.partial(_combine_kernel, tn=tn, tile0=tile0, final_norm=final_norm),
        grid=(nt,),
        in_specs=[pl.BlockSpec(memory_space=pl.ANY),
                  pl.BlockSpec(memory_space=pl.ANY),
                  pl.BlockSpec((tn, TOP_K), lambda i: (tile0 + i, 0)),
                  pl.BlockSpec((tn, d), lambda i: (tile0 + i, 0)),
                  pl.BlockSpec((tn, d), lambda i: (i, 0)),
                  pl.BlockSpec((1, gr, d), lambda i: (i // per, 0, 0)),
                  pl.BlockSpec((1, d), lambda i: (0, 0))],
        out_specs=pl.BlockSpec((tn, d), lambda i: (i, 0)),
        out_shape=jax.ShapeDtypeStruct((rows, d), jnp.float32),
        scratch_shapes=[pltpu.SMEM((TOP_K * tn,), jnp.int32),
                        pltpu.VMEM((TOP_K, tn, d // 2), jnp.uint32),
                        pltpu.SemaphoreType.DMA(()),
                        pltpu.SemaphoreType.DMA(())],
        compiler_params=_cparams(("arbitrary",)),
        name="combine",
    )(dest_flat, ys, wts_tok, shared, x1, gt3, g_final.reshape(1, d))


def _page_specs(shape, n, pg):
    def mk(u):
        return pl.BlockSpec((1,) + shape, lambda b, j, pt: (pt[b, j * pg + u], 0, 0))
    return [mk(u) for u in range(n)]


def _kv_page_specs(kshape, vshape, pg, ns):
    def mk(shape, first):
        def one(u):
            def index(b, j, pt):
                step = jnp.minimum(j, ns - 1) if first else jnp.maximum(j - ns, 0)
                return (pt[b, step * pg + u], 0, 0)
            return pl.BlockSpec((1,) + shape, index)
        return [one(u) for u in range(pg)]
    return mk(kshape, True) + mk(vshape, False)


def _softmax_pages(s_ref):
    s = s_ref[...]
    m = jnp.max(jnp.max(s, axis=0, keepdims=True), axis=2, keepdims=True)
    e = jnp.exp(s - m)
    return e / jnp.sum(jnp.sum(e, axis=0, keepdims=True), axis=2, keepdims=True)


def _da_sample_kernel(pt_ref, lam_ref, q_ref, kn_ref, vn_ref, bl_ref, cf_ref, b0_ref, ex_ref, hm_ref,
                      *rest, pg, n_pages):
    k_refs, v_refs = rest[:pg], rest[pg:2 * pg]
    o_ref, s_ref, a_ref, acc_ref = rest[2 * pg:]
    j = pl.program_id(1)
    ns = n_pages // pg
    r = q_ref.shape[1]
    nh = r // 2
    lane = lax.broadcasted_iota(jnp.int32, (r, PAGE_SIZE), 1)

    @pl.when(j < ns)
    def _():
        q = q_ref[0]
        qb = _bf(q)
        for u in range(pg):
            page = j * pg + u
            s = _dot(qb, _bf(k_refs[u][0]))
            s_ref[page] = s + jnp.where(page == (n_pages - 1), bl_ref[...], cf_ref[...])

        @pl.when(j == 0)
        def _():
            s_new = jnp.sum(_rounded(q) * _rounded(kn_ref[0]), axis=-1, keepdims=True) + b0_ref[...]
            s_ref[n_pages] = jnp.where(lane == 0, s_new, NEG)

    @pl.when(j == ns)
    def _():
        p = _softmax_pages(s_ref)
        a = p[:, :nh, :] - lam_ref[0] * p[:, nh:, :]
        a_ref[...] = _bf(jnp.concatenate([a, jnp.zeros_like(a)], axis=1))
        a_new = a_ref[n_pages][:, 0:1].astype(jnp.float32)
        acc_ref[...] = a_new * _rounded(vn_ref[0])

    @pl.when(j >= ns)
    def _():
        acc = acc_ref[...]
        for u in range(pg):
            page = (j - ns) * pg + u
            pe = _bf(_dot(a_ref[page], ex_ref[...]) * hm_ref[...])
            acc = acc + _dot(pe, _bf(v_refs[u][0]))
        acc_ref[...] = acc

    @pl.when(j == pl.num_programs(1) - 1)
    def _():
        o_ref[0] = acc_ref[...]


def _da_sample(page_table, lam, qbd, knew, vnew, bias_last, cfar, bias0, kt, v2, pg):
    nb, n_pages = page_table.shape
    r, w = qbd.shape[1:]
    rows_v, dv = v2.shape[1:]
    nh = rows_v // PAGE_SIZE
    ns = n_pages // pg
    col = jnp.arange(rows_v, dtype=jnp.int32)
    expand = (col[None, :] // nh == jnp.arange(PAGE_SIZE, dtype=jnp.int32)[:, None]).astype(BF16)
    head_mask = (col[None, :] % nh == jnp.arange(r, dtype=jnp.int32)[:, None]).astype(jnp.float32)
    full = lambda shp: pl.BlockSpec(shp, lambda b, j, pt: (0,) * len(shp))
    per_b = lambda shp: pl.BlockSpec((1,) + shp, lambda b, j, pt: (b, 0, 0))
    grid_spec = pltpu.PrefetchScalarGridSpec(
        num_scalar_prefetch=1,
        grid=(nb, 2 * ns),
        in_specs=[pl.BlockSpec(memory_space=pltpu.SMEM),
                  per_b((r, w)), per_b((1, w)), per_b((r, dv)),
                  full((r, PAGE_SIZE)), full((r, 1)), full((r, 1)),
                  full((PAGE_SIZE, rows_v)), full((r, rows_v))]
                 + _kv_page_specs((w, PAGE_SIZE), (rows_v, dv), pg, ns),
        out_specs=per_b((r, dv)),
        scratch_shapes=[pltpu.VMEM((n_pages + 1, r, PAGE_SIZE), jnp.float32),
                        pltpu.VMEM((n_pages + 1, r, PAGE_SIZE), BF16),
                        pltpu.VMEM((r, dv), jnp.float32)])
    return pl.pallas_call(
        functools.partial(_da_sample_kernel, pg=pg, n_pages=n_pages),
        grid_spec=grid_spec,
        out_shape=jax.ShapeDtypeStruct((nb, r, dv), jnp.float32),
        compiler_params=_cparams(("arbitrary", "arbitrary")),
        name="da_sample",
    )(page_table, lam, qbd, knew, vnew, bias_last, cfar, bias0, expand, head_mask,
      *([kt] * pg), *([v2] * pg))


def _idx_sample_kernel(pt_ref, q_ref, w_ref, kn_ref, *rest, pg, n_pages, topk):
    k_refs = rest[:pg]
    sel_ref, sc_ref = rest[pg:]
    j = pl.program_id(1)
    q = q_ref[0]
    w = _rounded(w_ref[0])
    scale = IDX_DK ** -0.5 * IDX_HEADS ** -0.5
    rows = sc_ref.shape[0]
    lane = lax.broadcasted_iota(jnp.int32, (1, PAGE_SIZE), 1)

    @pl.when(j == 0)
    def _():
        sc_ref[...] = jnp.full(sc_ref.shape, -jnp.inf, jnp.float32)
        d = jnp.maximum(jnp.sum(_rounded(q) * _rounded(kn_ref[0]), axis=-1, keepdims=True), 0.0)
        s_new = jnp.sum(w * _rounded(d), axis=0, keepdims=True) * scale
        sc_ref[n_pages:n_pages + 1, :] = jnp.where(lane == 0, s_new, -jnp.inf)

    qb = _bf(q)
    for u in range(pg):
        d = _rounded(jnp.maximum(_dot(qb, _bf(k_refs[u][0])), 0.0))
        sc_ref[pl.ds(j * pg + u, 1), :] = jnp.sum(w * d, axis=0, keepdims=True) * scale

    @pl.when(j == pl.num_programs(1) - 1)
    def _():
        key = _float_key(sc_ref[...])
        pos = (lax.broadcasted_iota(jnp.int32, key.shape, 0) * PAGE_SIZE
               + lax.broadcasted_iota(jnp.int32, key.shape, 1))

        def bit_body(it, thr):
            cand = thr + (jnp.int32(1) << (31 - it))
            cnt = jnp.sum(jnp.where(key >= cand, 1.0, 0.0))
            return jnp.where(cnt >= float(topk), cand, thr)

        thr = lax.fori_loop(0, 32, bit_body, jnp.int32(INT_MIN))
        thr = jnp.maximum(thr, jnp.int32(KEY_NEG_INF + 1))
        need = float(topk) - jnp.sum(jnp.where(key > thr, 1.0, 0.0))
        n_bits = max(1, int(rows * PAGE_SIZE).bit_length())

        def cut_body(it, cut):
            cand = cut + (jnp.int32(1) << (n_bits - 1 - it))
            n_eq = jnp.sum(jnp.where((key == thr) & (pos < cand), 1.0, 0.0))
            return jnp.where(n_eq <= need, cand, cut)

        cut = lax.fori_loop(0, n_bits, cut_body, jnp.int32(0))
        sel_ref[0] = jnp.where((key > thr) | ((key == thr) & (pos < cut)), 1.0, 0.0)


def _idx_sample(page_table, qix, wix, knew, kc, pg, topk):
    nb, n_pages = page_table.shape
    rows = -(-(n_pages + 1) // 8) * 8
    per_b = lambda shp: pl.BlockSpec((1,) + shp, lambda b, j, pt: (b, 0, 0))
    grid_spec = pltpu.PrefetchScalarGridSpec(
        num_scalar_prefetch=1,
        grid=(nb, n_pages // pg),
        in_specs=[per_b((IDX_HEADS, IDX_DK)), per_b((IDX_HEADS, 1)), per_b((1, IDX_DK))]
                 + _page_specs((IDX_DK, PAGE_SIZE), pg, pg),
        out_specs=per_b((rows, PAGE_SIZE)),
        scratch_shapes=[pltpu.VMEM((rows, PAGE_SIZE), jnp.float32)])
    return pl.pallas_call(
        functools.partial(_idx_sample_kernel, pg=pg, n_pages=n_pages, topk=topk),
        grid_spec=grid_spec,
        out_shape=jax.ShapeDtypeStruct((nb, rows, PAGE_SIZE), jnp.float32),
        compiler_params=_cparams(("arbitrary", "arbitrary")),
        name="idx_sample",
    )(page_table, qix, wix, knew, *([kc] * pg))


def _sa_sample_kernel(pt_ref, q_ref, kn_ref, vn_ref, sel_ref, bl_ref, cf_ref, b0_ref, gm_ref, *rest,
                      pg, n_pages):
    k_refs, v_refs = rest[:pg], rest[pg:2 * pg]
    o_ref, s_ref, p_ref, acc_ref = rest[2 * pg:]
    j = pl.program_id(1)
    ns = n_pages // pg
    scale = SA_DH ** -0.5
    r, cols = gm_ref.shape
    lane = lax.broadcasted_iota(jnp.int32, (r, cols), 1)

    @pl.when(j < ns)
    def _():
        q = q_ref[0]
        qb = _bf(q)
        for u in range(pg):
            page = j * pg + u
            s = _dot_nt(qb, _bf(k_refs[u][0])) * scale
            s = s + jnp.where(page == (n_pages - 1), bl_ref[...], cf_ref[...])
            keep = (sel_ref[0, pl.ds(page, 1), :] > 0.0) & (gm_ref[...] > 0.0)
            s_ref[page] = jnp.where(keep, s, NEG)

        @pl.when(j == 0)
        def _():
            on = sel_ref[0, n_pages:n_pages + 1, 0:1] > 0.0
            s_new = jnp.sum(_rounded(q) * _rounded(kn_ref[0]), axis=-1, keepdims=True) * scale + b0_ref[...]
            s_ref[n_pages] = jnp.where((lane == 0) & on, s_new, NEG)

    @pl.when(j == ns)
    def _():
        p_ref[...] = _bf(_softmax_pages(s_ref))
        acc_ref[...] = p_ref[n_pages][:, 0:1].astype(jnp.float32) * _rounded(vn_ref[0])

    @pl.when(j >= ns)
    def _():
        acc = acc_ref[...]
        for u in range(pg):
            acc = acc + _dot(p_ref[(j - ns) * pg + u], _bf(v_refs[u][0]))
        acc_ref[...] = acc

    @pl.when(j == pl.num_programs(1) - 1)
    def _():
        o_ref[0] = acc_ref[...]


def _sa_sample(page_table, q, knew, vnew, sel2, bias_last2, cfar, bias0, group_mask, k2, v2, pg):
    nb, n_pages = page_table.shape
    r, dh = q.shape[1:]
    srows, cols = sel2.shape[1:]
    ns = n_pages // pg
    full = lambda shp: pl.BlockSpec(shp, lambda b, j, pt: (0,) * len(shp))
    per_b = lambda shp: pl.BlockSpec((1,) + shp, lambda b, j, pt: (b, 0, 0))
    grid_spec = pltpu.PrefetchScalarGridSpec(
        num_scalar_prefetch=1,
        grid=(nb, 2 * ns),
        in_specs=[per_b((r, dh)), per_b((r, dh)), per_b((r, dh)), per_b((srows, cols)),
                  full((r, cols)), full((r, 1)), full((r, 1)), full((r, cols))]
                 + _kv_page_specs((cols, dh), (cols, dh), pg, ns),
        out_specs=per_b((r, dh)),
        scratch_shapes=[pltpu.VMEM((n_pages + 1, r, cols), jnp.float32),
                        pltpu.VMEM((n_pages + 1, r, cols), BF16),
                        pltpu.VMEM((r, dh), jnp.float32)])
    return pl.pallas_call(
        functools.partial(_sa_sample_kernel, pg=pg, n_pages=n_pages),
        grid_spec=grid_spec,
        out_shape=jax.ShapeDtypeStruct((nb, r, dh), jnp.float32),
        compiler_params=_cparams(("arbitrary", "arbitrary")),
        name="sa_sample",
    )(page_table, q, knew, vnew, sel2, bias_last2, cfar, bias0, group_mask, *([k2] * pg), *([v2] * pg))


ROW_TILE = 256
COMBINE_TILE = 128
EXPERT_BLOCK = 256
TAIL_ROWS = 256
PAGES_PER_STEP_DA = 8
PAGES_PER_STEP_SA = 16
PAGES_PER_STEP_IDX = 32


def _rms_rows(x, g):
    return x * lax.rsqrt(jnp.mean(x * x, axis=-1, keepdims=True) + EPS) * g


def _pad_rows(x, rows):
    return jnp.pad(x, ((0, rows - x.shape[0]), (0, 0)))


def _tile_major(dest, tn):
    k, n = dest.shape
    return dest.reshape(k, n // tn, tn).transpose(1, 0, 2).reshape(-1)


def kernel(x_prompt, x_sample, c_prompt, c_sample, cache_da_k, cache_da_v, cache_sa_k, cache_sa_v, cache_idx_k, page_table, rel_bias_table, w_ada, b_ada, g_attn, g_ffn, w_in, lambda_q1, lambda_k1, lambda_q2, lambda_k2, g_subln, w_proj_da, w_proj_sa, w_out, w_router, b_router, w_gate, w_up, w_down, w_sh_gate, w_sh_up, w_sh_down, g_final):
    f32, bf16 = jnp.float32, BF16
    nb, t, d = x_prompt.shape
    nbs, dec_seq, _ = x_sample.shape
    assert dec_seq == 1
    depth = w_in.shape[0]
    n_pages = page_table.shape[1]
    past_len = n_pages * PAGE_SIZE
    n = nb * t
    n_pad = n + TAIL_ROWS
    tq = min(256, t)
    assert tq >= MAX_DISTANCE and t % tq == 0 and n % ROW_TILE == 0 and nbs <= TAIL_ROWS
    assert TAIL_ROWS == ROW_TILE and EXPERT_BLOCK % 8 == 0
    topk_p = min(SA_TOPK_MAX, t // 4)
    topk_s = min(SA_TOPK_MAX, (past_len + dec_seq) // 4)
    assert topk_p <= tq

    sizes = [DA_HEADS * 2 * DA_DK, DA_HEADS * 2 * DA_DK, DA_HEADS * DA_DV, SA_HEADS * SA_DH,
             SA_KV_HEADS * SA_DH, SA_KV_HEADS * SA_DH, IDX_HEADS * IDX_DK, IDX_DK, IDX_HEADS, d, d]
    offs = [sum(sizes[:i]) for i in range(len(sizes) + 1)]
    (o_qda, o_kda, o_vda, o_qsa, o_ksa, o_vsa, o_qix, o_kix, o_wix, o_gda, o_gsa, _) = offs
    misc_w = 2 * LANES

    table = rel_bias_table.astype(f32)
    cfar = table[N_BUCKETS - 1]
    tiles_da = _near_tiles(table[:, :DA_HEADS], tq)
    tiles_sa = _near_tiles(table[:, DA_HEADS:], tq)
    last_dist = past_len - ((n_pages - 1) * PAGE_SIZE + jnp.arange(PAGE_SIZE, dtype=jnp.int32))
    bias_last = _bias_by_distance(table, last_dist)
    bias0 = _bias_by_distance(table, jnp.zeros((1,), jnp.int32))
    rep2 = lambda a: jnp.repeat(a, 2, axis=0)

    xp = x_prompt.reshape(n, d)
    xs = x_sample.reshape(nbs, d)
    c_all = jnp.concatenate([c_prompt, c_sample], axis=0)
    leaves_p, leaves_s = [], []
    for l in range(depth):
        lam_init = 0.8 - 0.6 * math.exp(-0.3 * l)
        lam = (jnp.exp(jnp.sum(lambda_q1[l].astype(f32) * lambda_k1[l].astype(f32)))
               - jnp.exp(jnp.sum(lambda_q2[l].astype(f32) * lambda_k2[l].astype(f32))) + lam_init)
        mod = _linear_small(c_all, w_ada[l], b_ada[l], silu_in=True)
        sh1, sc1, gt1, sh2, sc2, gt2 = jnp.split(mod[:nb], 6, axis=-1)
        sh1s, sc1s, gt1s, sh2s, sc2s, gt2s = jnp.split(mod[nb:], 6, axis=-1)
        w = w_in[l]

        h = _norm_mod(xp, g_attn[l], sc1, sh1, t, ROW_TILE)
        q_da, = _mm(h, w, o_qda, sizes[0], (bf16,), scale=DA_DK ** -0.5)
        k_da, k_da_b = _mm(h, w, o_kda, sizes[1], (f32, bf16))
        v_da, v_da_b = _mm(h, w, o_vda, sizes[2], (f32, bf16))
        q_sa, = _mm(h, w, o_qsa, sizes[3], (bf16,), scale=SA_DH ** -0.5)
        k_sa, k_sa_b = _mm(h, w, o_ksa, sizes[4], (f32, bf16))
        v_sa, v_sa_b = _mm(h, w, o_vsa, sizes[5], (f32, bf16))
        q_ix, = _mm(h, w, o_qix, sizes[6], (bf16,), tn=512)
        misc, misc_b = _mm(h, w, o_kix, misc_w, (f32, bf16))
        gates, = _mm(h, w[:, o_gda:], 0, 2 * d, (bf16,), sigmoid=True)
        k_ix = misc[:, :IDX_DK]

        o_da = _da_prompt(q_da, k_da_b, v_da_b, tiles_da, cfar[:DA_HEADS], lam.reshape(1),
                          g_subln[l].astype(f32), nb, t, tq, 1.0 - lam_init)
        o_sa = _sa_prompt(q_sa, q_ix, misc, misc_b, k_sa_b, v_sa_b, tiles_sa, cfar[DA_HEADS:], nb, t, tq, topk_p)
        m = _proj_gate(o_da, o_sa, w_proj_da[l], w_proj_sa[l], gates, d)
        x1 = _mm_resid(m, w_out[l], xp, gt1, t)

        hs = _rms_rows(xs, g_attn[l].astype(f32)) * (1.0 + sc1s) + sh1s
        ps = _linear_small(hs, w)
        seg = lambda i: ps[:, offs[i]:offs[i + 1]]
        q_da_s, k_da_s, v_da_s, q_sa_s, k_sa_s, v_sa_s, q_ix_s, k_ix_s, w_ix_s, gda_s, gsa_s = [
            seg(i) for i in range(11)]

        r_da = 2 * DA_HEADS
        q16 = (q_da_s * DA_DK ** -0.5).reshape(nbs, DA_HEADS, 2, DA_DK).transpose(0, 2, 1, 3).reshape(nbs, r_da, DA_DK)
        blk_of_row = 2 * (jnp.arange(r_da) % DA_HEADS) + jnp.arange(r_da) // DA_HEADS
        place = (blk_of_row[:, None] == jnp.arange(r_da)[None, :]).astype(f32)
        qbd_da = (place[None, :, :, None] * q16[:, :, None, :]).reshape(nbs, r_da, -1)
        both = lambda a: jnp.concatenate([a, a], axis=0)
        kt_da = cache_da_k[l].transpose(0, 2, 3, 4, 1).reshape(-1, DA_HEADS * 2 * DA_DK, PAGE_SIZE)
        v2_da = cache_da_v[l].reshape(-1, PAGE_SIZE * DA_HEADS, DA_DV)
        v_rows = jnp.pad(v_da_s.reshape(nbs, DA_HEADS, DA_DV), ((0, 0), (0, r_da - DA_HEADS), (0, 0)))
        a_da = _da_sample(page_table, lam.reshape(1), qbd_da, k_da_s[:, None, :], v_rows,
                          both(bias_last[:DA_HEADS]), both(cfar[:DA_HEADS, None]), both(bias0[:DA_HEADS]),
                          kt_da, v2_da, math.gcd(PAGES_PER_STEP_DA, n_pages))
        o_da_s = a_da[:, :DA_HEADS]
        o_da_s = _rms_rows(o_da_s, g_subln[l].astype(f32)) * (1.0 - lam_init)

        sel = _idx_sample(page_table, q_ix_s.reshape(nbs, IDX_HEADS, IDX_DK), w_ix_s[:, :, None],
                          k_ix_s[:, None, :], cache_idx_k[l].transpose(0, 2, 1),
                          math.gcd(PAGES_PER_STEP_IDX, n_pages), topk_s)
        r_sa = 2 * SA_HEADS
        kv_of_row = jnp.minimum(jnp.arange(r_sa) // SA_GROUP, SA_KV_HEADS - 1)
        pad_sa = lambda a: jnp.pad(a, ((0, 0), (0, r_sa - SA_HEADS), (0, 0)))
        rep_kv = lambda a: jnp.repeat(a, SA_KV_HEADS, axis=-1)
        col_kv = jnp.arange(PAGE_SIZE * SA_KV_HEADS) % SA_KV_HEADS
        k2_sa = cache_sa_k[l].reshape(-1, PAGE_SIZE * SA_KV_HEADS, SA_DH)
        v2_sa = cache_sa_v[l].reshape(-1, PAGE_SIZE * SA_KV_HEADS, SA_DH)
        a_sa = _sa_sample(page_table, pad_sa(q_sa_s.reshape(nbs, SA_HEADS, SA_DH)),
                          k_sa_s.reshape(nbs, SA_KV_HEADS, SA_DH)[:, kv_of_row],
                          v_sa_s.reshape(nbs, SA_KV_HEADS, SA_DH)[:, kv_of_row],
                          rep_kv(sel), rep_kv(_pad_rows(bias_last[DA_HEADS:], r_sa)),
                          _pad_rows(cfar[DA_HEADS:, None], r_sa), _pad_rows(bias0[DA_HEADS:], r_sa),
                          (col_kv[None, :] == kv_of_row[:, None]).astype(f32),
                          k2_sa, v2_sa, math.gcd(PAGES_PER_STEP_SA, n_pages))
        o_sa_s = a_sa[:, :SA_HEADS]

        pda = _linear_small(o_da_s.reshape(nbs, -1), w_proj_da[l])
        psa = _linear_small(o_sa_s.reshape(nbs, -1), w_proj_sa[l])
        ms = _sigmoid(gda_s) * pda + _sigmoid(gsa_s) * psa
        x1s = xs + gt1s * _linear_small(ms, w_out[l])
        h2s = _rms_rows(x1s, g_ffn[l].astype(f32)) * (1.0 + sc2s) + sh2s
        lg_s = _linear_small(h2s, w_router[l])

        h2_all, lg_all = _norm_router(x1, g_ffn[l], sc2, sh2, w_router[l].T,
                                      _pad_rows(h2s.astype(bf16), TAIL_ROWS),
                                      _pad_rows(lg_s, TAIL_ROWS).T, t, ROW_TILE)
        eidx, wts, rank, cnt = _route(lg_all, b_router[l], ROW_TILE)
        cnt_tile = cnt[:, :, 0]
        total = jnp.sum(cnt_tile, axis=0)
        padded = jnp.ceil(total / EXPERT_BLOCK) * EXPERT_BLOCK
        pends = jnp.cumsum(padded)
        pstart = pends - padded
        base = pstart[None, :] + jnp.cumsum(cnt_tile, axis=0) - cnt_tile
        dest = _dest(eidx, rank, jnp.broadcast_to(base[:, :, None], base.shape + (LANES,)), ROW_TILE)
        n_blk = -(-(n_pad * TOP_K) // EXPERT_BLOCK) + N_EXPERTS
        blk_start = (jnp.arange(n_blk) * EXPERT_BLOCK).astype(f32)
        blk_e = jnp.minimum(jnp.sum(pends[None, :] <= blk_start[:, None], axis=1), N_EXPERTS - 1).astype(jnp.int32)
        n_used = (pends[-1] / EXPERT_BLOCK).astype(jnp.int32).reshape(1)
        n_rows = n_blk * EXPERT_BLOCK
        pad_start = jnp.concatenate([pstart + total, pends[-1:]]).astype(jnp.int32)
        pad_cnt = jnp.concatenate([padded - total, (n_rows - pends[-1:]) / 8]).astype(jnp.int32)
        xs_sorted = _dispatch(h2_all, _tile_major(dest, ROW_TILE), pad_start, pad_cnt, n_rows, ROW_TILE)
        owns = padded > 0
        e_ids = jnp.arange(N_EXPERTS, dtype=jnp.int32)
        later = jnp.where(owns[None, :] & (e_ids[None, :] > e_ids[:, None]), e_ids[None, :], N_EXPERTS)
        next_owner = jnp.min(later, axis=1)
        next_owner = jnp.where(next_owner < N_EXPERTS, next_owner, -1).astype(jnp.int32)
        slot_of_e = ((jnp.cumsum(owns.astype(jnp.int32)) - 1) % 2).astype(jnp.int32)
        ys = _experts(xs_sorted, blk_e, n_used, next_owner[blk_e], slot_of_e[blk_e],
                      w_gate[l], w_up[l], w_down[l], EXPERT_BLOCK)
        shared = _ffn_shared(h2_all, w_sh_gate[l], w_sh_up[l], w_sh_down[l], ROW_TILE)
        dest_c = _tile_major(dest, COMBINE_TILE)
        wts_tok = wts.T
        last = l == depth - 1
        xp = _combine(dest_c, ys, wts_tok, shared, x1, gt2[:, None, :], g_final.astype(f32),
                      0, COMBINE_TILE, last)
        tail = _combine(dest_c, ys, wts_tok, shared, _pad_rows(x1s, TAIL_ROWS),
                        _pad_rows(gt2s, TAIL_ROWS).reshape(-1, COMBINE_TILE, d), g_final.astype(f32),
                        n // COMBINE_TILE, COMBINE_TILE, last)
        xs = tail[:nbs]
        leaves_p.append((k_da, v_da, k_sa, v_sa, k_ix))
        leaves_s.append((k_da_s, v_da_s, k_sa_s, v_sa_s, k_ix_s))

    shapes = [(DA_HEADS, 2, DA_DK), (DA_HEADS, DA_DV), (SA_KV_HEADS, SA_DH), (SA_KV_HEADS, SA_DH), (IDX_DK,)]
    out_p = [jnp.stack([lv[i].reshape((nb, t) + shapes[i]) for lv in leaves_p]) for i in range(5)]
    out_s = [jnp.stack([lv[i].reshape((nbs, dec_seq) + shapes[i]) for lv in leaves_s]) for i in range(5)]
    return (xp.reshape(nb, t, d), xs.reshape(nbs, dec_seq, d), *out_p, *out_s)
```

```python
import functools
import math

import jax
import jax.numpy as jnp
from jax import lax
from jax.experimental import pallas as pl
from jax.experimental.pallas import tpu as pltpu

DA_HEADS = 8
DA_DK = 64
DA_DV = 2 * DA_DK
SA_HEADS = 8
SA_KV_HEADS = 2
SA_DH = 128
SA_GROUP = SA_HEADS // SA_KV_HEADS
IDX_HEADS = 16
IDX_DK = 64
SA_TOPK_MAX = 256
N_BUCKETS = 32
MAX_DISTANCE = 128
N_EXPERTS = 64
N_GROUPS = 8
GROUP_SIZE = N_EXPERTS // N_GROUPS
TOPK_GROUPS = 4
TOP_K = 8
ROUTED_SCALE = 2.5
PAGE_SIZE = 128
EPS = 1e-6

LANES = 128
VMEM_LIMIT = 56 * 1024 * 1024

BF16 = jnp.bfloat16
NEG = -1e30
INT_MIN = -(2 ** 31)
KEY_NEG_INF = (0xFF800000 ^ 0x7FFFFFFF) - (1 << 32)

_NT = (((1,), (1,)), ((), ()))


def _cparams(sem):
    return pltpu.CompilerParams(dimension_semantics=sem, vmem_limit_bytes=VMEM_LIMIT)


def _dot(a, b):
    return jnp.dot(a, b, preferred_element_type=jnp.float32)


def _dot_nt(a, b):
    return lax.dot_general(a, b, _NT, preferred_element_type=jnp.float32)


def _bf(x):
    return x.astype(BF16)


def _rounded(x):
    return x.astype(BF16).astype(jnp.float32)


def _sigmoid(x):
    return 1.0 / (1.0 + jnp.exp(-x))


def _silu(x):
    return x * _sigmoid(x)


def _float_key(s):
    b = pltpu.bitcast(s, jnp.int32)
    return b ^ ((b >> 31) & jnp.int32(0x7FFFFFFF))


def _linear_small_kernel(x_ref, w_ref, b_ref, o_ref, *, silu_in):
    x = x_ref[...]
    if silu_in:
        x = _silu(x)
    o_ref[...] = _dot(_bf(x), _bf(w_ref[...])) + b_ref[...]


def _linear_small(x, w, b=None, *, silu_in=False, tn=512):
    m0, k = x.shape
    m = -(-m0 // 16) * 16
    x = jnp.pad(x, ((0, m - m0), (0, 0)))
    n = w.shape[1]
    tn = min(tn, n)
    if b is None:
        b = jnp.zeros((1, n), jnp.float32)
    out = pl.pallas_call(
        functools.partial(_linear_small_kernel, silu_in=silu_in),
        grid=(pl.cdiv(n, tn),),
        in_specs=[pl.BlockSpec((m, k), lambda j: (0, 0)),
                  pl.BlockSpec((k, tn), lambda j: (0, j)),
                  pl.BlockSpec((1, tn), lambda j: (0, j))],
        out_specs=pl.BlockSpec((m, tn), lambda j: (0, j)),
        out_shape=jax.ShapeDtypeStruct((m, n), jnp.float32),
        compiler_params=_cparams(("arbitrary",)),
        name="linear_small",
    )(x, w, b.reshape(1, n))
    return out[:m0]


def _norm_mod_kernel(x_ref, g_ref, sc_ref, sh_ref, o_ref):
    x = x_ref[...]
    y = x * lax.rsqrt(jnp.mean(x * x, axis=-1, keepdims=True) + EPS) * g_ref[...]
    o_ref[...] = (y * (1.0 + sc_ref[0]) + sh_ref[0]).astype(o_ref.dtype)


def _norm_mod(x, g, sc, sh, rows_per_batch, tm):
    n, d = x.shape
    per = rows_per_batch // tm
    return pl.pallas_call(
        _norm_mod_kernel,
        grid=(n // tm,),
        in_specs=[pl.BlockSpec((tm, d), lambda i: (i, 0)),
                  pl.BlockSpec((1, d), lambda i: (0, 0)),
                  pl.BlockSpec((1, 1, d), lambda i: (i // per, 0, 0)),
                  pl.BlockSpec((1, 1, d), lambda i: (i // per, 0, 0))],
        out_specs=pl.BlockSpec((tm, d), lambda i: (i, 0)),
        out_shape=jax.ShapeDtypeStruct((n, d), BF16),
        compiler_params=_cparams(("arbitrary",)),
        name="norm_mod",
    )(x, g.reshape(1, d), sc[:, None, :], sh[:, None, :])


def _mm_kernel(x_ref, w_ref, *rest, scale, sigmoid, n_out, transposed_out):
    o_refs, wbf_ref = rest[:n_out], rest[-1]

    @pl.when(pl.program_id(1) == 0)
    def _():
        wbf_ref[...] = w_ref[...].astype(BF16)

    acc = _dot(x_ref[...], wbf_ref[...])
    if scale != 1.0:
        acc = acc * scale
    if sigmoid:
        acc = _sigmoid(acc)
    for o in o_refs:
        o[...] = acc.astype(o.dtype)
    if transposed_out:
        rest[n_out][0] = acc.T


def _mm(x, w, col0, ncols, out_dtypes, *, scale=1.0, sigmoid=False, tm=512, tn=1024, rows_per_batch=None):
    m, k = x.shape
    tn = min(tn, ncols)
    tm = min(tm, m)
    assert col0 % tn == 0 and ncols % tn == 0 and m % tm == 0
    jb = col0 // tn
    out_specs = [pl.BlockSpec((tm, tn), lambda j, i: (i, j)) for _ in out_dtypes]
    out_shape = [jax.ShapeDtypeStruct((m, ncols), dt) for dt in out_dtypes]
    if rows_per_batch is not None:
        per = rows_per_batch // tm
        out_specs.append(pl.BlockSpec((1, tn, tm), lambda j, i: (i // per, j, i % per)))
        out_shape.append(jax.ShapeDtypeStruct((m // rows_per_batch, ncols, rows_per_batch), jnp.float32))
    outs = pl.pallas_call(
        functools.partial(_mm_kernel, scale=scale, sigmoid=sigmoid, n_out=len(out_dtypes),
                          transposed_out=rows_per_batch is not None),
        grid=(ncols // tn, m // tm),
        in_specs=[pl.BlockSpec((tm, k), lambda j, i: (i, 0)),
                  pl.BlockSpec((k, tn), lambda j, i: (0, jb + j))],
        out_specs=out_specs,
        out_shape=out_shape,
        scratch_shapes=[pltpu.VMEM((k, tn), BF16)],
        compiler_params=_cparams(("arbitrary", "arbitrary")),
        name="mm_cols",
    )(x, w)
    return outs


def _rel_bucket(dist):
    max_exact = N_BUCKETS // 2
    d = jnp.maximum(dist, 0)
    large = max_exact + (jnp.log(jnp.maximum(d, 1).astype(jnp.float32) / max_exact)
                         / math.log(MAX_DISTANCE / max_exact)
                         * (N_BUCKETS - max_exact)).astype(jnp.int32)
    large = jnp.minimum(large, N_BUCKETS - 1)
    return jnp.where(d < max_exact, d, large)


def _bias_by_distance(table, dists):
    return table[_rel_bucket(dists)].astype(jnp.float32).T


def _toeplitz_kernel(u_ref, o_ref):
    t = o_ref.shape[2]
    x = jnp.broadcast_to(u_ref[0], (t, 2 * t))
    o_ref[0, 0] = pltpu.roll(x, 0, 1, stride=1, stride_axis=0)[:, :t]


def _near_tiles(table, t):
    nh = table.shape[1]
    k = jnp.arange(2 * t, dtype=jnp.int32)
    gens = []
    for off in (0, t):
        d = jnp.where(k < t, off - k, off + 2 * t - k)
        gens.append(jnp.where(d[None] >= 0, _bias_by_distance(table, d), NEG))
    u = jnp.stack(gens, axis=1).reshape(nh * 2, 1, 2 * t)
    return pl.pallas_call(
        _toeplitz_kernel,
        grid=(nh, 2),
        in_specs=[pl.BlockSpec((1, 1, 2 * t), lambda h, o: (h * 2 + o, 0, 0))],
        out_specs=pl.BlockSpec((1, 1, t, t), lambda h, o: (h, o, 0, 0)),
        out_shape=jax.ShapeDtypeStruct((nh, 2, t, t), jnp.float32),
        compiler_params=_cparams(("arbitrary", "arbitrary")),
        name="bias_tiles",
    )(u)


def _fold_lanes(x, op):
    out = x[:, :LANES]
    for c in range(1, x.shape[1] // LANES):
        out = op(out, x[:, c * LANES:(c + 1) * LANES])
    return out


def _pairwise(n, fn):
    def body(i, carry):
        fn(2 * i)
        fn(2 * i + 1)
        return carry

    lax.fori_loop(0, n // 2, body, 0)

    @pl.when(n % 2 == 1)
    def _():
        fn(n - 1)


def _da_prompt_kernel(cfar_ref, lam_ref, q_ref, k_ref, v_ref, tile_ref, g_ref, o_ref,
                      s_ref, mpart_ref, shift_ref, lpart_ref, acc_ref, *, tq, out_scale):
    h = pl.program_id(1)
    qi = pl.program_id(2)
    q = q_ref[...]
    lane = lax.broadcasted_iota(jnp.int32, q.shape, 1)
    zero = jnp.zeros_like(q)
    q2 = jnp.concatenate([jnp.where(lane < DA_DK, q, zero), jnp.where(lane >= DA_DK, q, zero)], axis=0)
    cfar = cfar_ref[h]
    n_far = jnp.maximum(qi - 1, 0)
    r2 = 2 * tq

    def chunk_rows(kc):
        return pl.ds(pl.multiple_of(kc * tq, tq), tq)

    def scores(kc, bias):
        s = _dot_nt(q2, k_ref[chunk_rows(kc), :])
        if bias is not None:
            s = s + jnp.concatenate([bias, bias], axis=0)
        s_ref[kc] = s
        mpart_ref[...] = jnp.maximum(mpart_ref[...], _fold_lanes(s, jnp.maximum))

    mpart_ref[...] = jnp.full((r2, LANES), NEG, jnp.float32)
    _pairwise(n_far, lambda kc: scores(kc, None))
    m_far = jnp.max(mpart_ref[...], axis=-1, keepdims=True) + cfar
    mpart_ref[...] = jnp.full((r2, LANES), NEG, jnp.float32)

    @pl.when(qi >= 1)
    def _():
        scores(qi - 1, tile_ref[0, 1])

    scores(qi, tile_ref[0, 0])
    m = jnp.maximum(m_far, jnp.max(mpart_ref[...], axis=-1, keepdims=True))
    shift_ref[0] = jnp.broadcast_to(m - cfar, (r2, LANES))
    shift_ref[1] = jnp.broadcast_to(m, (r2, LANES))

    lpart_ref[...] = jnp.zeros((r2, LANES), jnp.float32)
    acc_ref[...] = jnp.zeros((r2, DA_DV), jnp.float32)

    def weights(kc, which):
        s = s_ref[kc]
        sh = shift_ref[which]
        ps = [jnp.exp(s[:, c * LANES:(c + 1) * LANES] - sh) for c in range(tq // LANES)]
        tot = ps[0]
        for pc in ps[1:]:
            tot = tot + pc
        lpart_ref[...] = lpart_ref[...] + tot
        p = jnp.concatenate(ps, axis=1).astype(BF16)
        acc_ref[...] = acc_ref[...] + _dot(p, v_ref[chunk_rows(kc), :])

    _pairwise(n_far, lambda kc: weights(kc, 0))

    @pl.when(qi >= 1)
    def _():
        weights(qi - 1, 1)

    weights(qi, 1)

    lam = lam_ref[0]
    a = acc_ref[...] / jnp.sum(lpart_ref[...], axis=-1, keepdims=True)
    o = a[:tq] - lam * a[tq:]
    o = o * lax.rsqrt(jnp.mean(o * o, axis=-1, keepdims=True) + EPS) * g_ref[...]
    o_ref[...] = (o * out_scale).astype(o_ref.dtype)


def _da_prompt(q, k, v, tiles, cfar, lam, g_subln, nb, t, tq, out_scale):
    n = q.shape[0]
    nq = t // tq
    grid_spec = pltpu.PrefetchScalarGridSpec(
        num_scalar_prefetch=0,
        grid=(nb, DA_HEADS, nq),
        in_specs=[pl.BlockSpec(memory_space=pltpu.SMEM),
                  pl.BlockSpec(memory_space=pltpu.SMEM),
                  pl.BlockSpec((tq, LANES), lambda b, h, i: (b * nq + i, h)),
                  pl.BlockSpec((t, LANES), lambda b, h, i: (b, h)),
                  pl.BlockSpec((t, LANES), lambda b, h, i: (b, h)),
                  pl.BlockSpec((1, 2, tq, tq), lambda b, h, i: (h, 0, 0, 0)),
                  pl.BlockSpec((1, DA_DV), lambda b, h, i: (0, 0))],
        out_specs=pl.BlockSpec((tq, LANES), lambda b, h, i: (b * nq + i, h)),
        scratch_shapes=[pltpu.VMEM((nq, 2 * tq, tq), jnp.float32),
                        pltpu.VMEM((2 * tq, LANES), jnp.float32),
                        pltpu.VMEM((2, 2 * tq, LANES), jnp.float32),
                        pltpu.VMEM((2 * tq, LANES), jnp.float32),
                        pltpu.VMEM((2 * tq, DA_DV), jnp.float32)])
    return pl.pallas_call(
        functools.partial(_da_prompt_kernel, tq=tq, out_scale=out_scale),
        grid_spec=grid_spec,
        out_shape=jax.ShapeDtypeStruct((n, DA_HEADS * DA_DV), BF16),
        compiler_params=_cparams(("arbitrary", "arbitrary", "arbitrary")),
        name="da_prompt",
    )(cfar, lam, q, k, v, tiles, g_subln.reshape(1, DA_DV))


def _sa_prompt_kernel(cfar_ref, qs_ref, qx_ref, mq_ref, mk_ref, ks_ref, vs_ref, tile_ref, o_ref,
                      k2_ref, key_ref, hi_ref, lo_ref, cut_ref, s_ref, mpart_ref, shift_ref, lpart_ref,
                      acc_ref, *, tq, topk):
    qi = pl.program_id(1)
    n_chunks = qi + 1
    t = mk_ref.shape[0]

    @pl.when(qi == 0)
    def _():
        kix = mk_ref[:, :LANES].astype(jnp.float32)
        lane = lax.broadcasted_iota(jnp.int32, kix.shape, 1)
        k2_ref[0] = jnp.where(lane < IDX_DK, kix, 0.0).astype(BF16)
        k2_ref[1] = jnp.where(lane >= IDX_DK, pltpu.roll(kix, IDX_DK, axis=1), 0.0).astype(BF16)

    wix = mq_ref[:, IDX_DK:IDX_DK + IDX_HEADS]
    wcols = [wix[:, hh:hh + 1] for hh in range(IDX_HEADS)]
    row = lax.broadcasted_iota(jnp.int32, (tq, tq), 0)
    col = lax.broadcasted_iota(jnp.int32, (tq, tq), 1)

    def score_body(kc, carry):
        rows = pl.ds(pl.multiple_of(kc * tq, tq), tq)
        ke = k2_ref[0, rows, :]
        ko = k2_ref[1, rows, :]
        sc = jnp.zeros((tq, tq), jnp.float32)
        for p in range(IDX_HEADS // 2):
            qp = qx_ref[:, p * LANES:(p + 1) * LANES]
            sc = sc + wcols[2 * p] * jnp.maximum(_dot_nt(qp, ke), 0.0)
            sc = sc + wcols[2 * p + 1] * jnp.maximum(_dot_nt(qp, ko), 0.0)
        sc = sc * (IDX_DK ** -0.5 * IDX_HEADS ** -0.5)
        sc = jnp.where((kc < qi) | (row >= col), sc, -jnp.inf)
        key = _float_key(sc)
        key_ref[kc] = key
        hi_ref[kc] = (key >> 16).astype(jnp.int16)
        return carry

    lax.fori_loop(0, n_chunks, score_body, 0)

    i16 = jnp.int16
    i16_min = -(2 ** 15)
    one_i = jnp.ones((tq, tq), i16)
    zero_i = jnp.zeros((tq, tq), i16)
    ones_col = jnp.ones((tq, LANES), BF16)

    def wide16(x):
        return jnp.concatenate([x] * (tq // LANES), axis=1).astype(i16)

    def count_ge16(ref16, cand):
        c16 = wide16(cand)

        def body(kc, acc):
            return acc + jnp.where(ref16[kc] >= c16, one_i, zero_i)
        acc = lax.fori_loop(0, n_chunks, body, zero_i)
        return _dot(acc.astype(jnp.float32).astype(BF16), ones_col)

    def search16(ref16, base, need):
        def bit_body(it, carry):
            v, cnt_v = carry
            cand = v + (jnp.int32(1) << (15 - it))
            cnt = base + count_ge16(ref16, cand)
            ok = cnt >= need
            return jnp.where(ok, cand, v), jnp.where(ok, cnt, cnt_v)
        v0 = jnp.full((tq, LANES), i16_min, jnp.int32)
        c0 = jnp.full((tq, LANES), 3.0e38, jnp.float32)
        return lax.fori_loop(0, 16, bit_body, (v0, c0))

    zero_cnt = jnp.zeros((tq, LANES), jnp.float32)
    t_hi, _ = search16(hi_ref, zero_cnt, float(topk))
    n_above = jnp.where(t_hi < 2 ** 15 - 1, count_ge16(hi_ref, jnp.minimum(t_hi + 1, 2 ** 15 - 1)), 0.0)
    t_hi16 = wide16(t_hi)

    def lower_body(kc, carry):
        lo = ((key_ref[kc] & jnp.int32(0xFFFF)) - 2 ** 15).astype(i16)
        lo_ref[kc] = jnp.where(hi_ref[kc] == t_hi16, lo, jnp.full((tq, tq), i16_min, i16))
        return carry

    lax.fori_loop(0, n_chunks, lower_body, 0)
    t_lo, cnt_w = search16(lo_ref, n_above, float(topk))
    thr = ((t_hi << 16) + (t_lo + 2 ** 15))[:, :1]
    cnt_thr = cnt_w[:, :1]
    tied = (cnt_thr > float(topk)) & (thr > jnp.int32(KEY_NEG_INF))
    need_tie = jnp.max(jnp.where(tied, 1.0, 0.0)) > 0.0
    thr = jnp.maximum(thr, jnp.int32(KEY_NEG_INF + 1))

    cut_ref[...] = jnp.full((tq, 1), 2 ** 30, jnp.int32)

    @pl.when(need_tie)
    def _():
        def gt_body(kc, acc):
            g = jnp.where(key_ref[kc] > thr, 1.0, 0.0)
            return acc + jnp.sum(g, axis=-1, keepdims=True)
        n_gt = lax.fori_loop(0, n_chunks, gt_body, jnp.zeros((tq, 1), jnp.float32))
        need = float(topk) - n_gt
        n_bits = max(1, int(t).bit_length())

        def cut_body(it, cut):
            cand = cut + (jnp.int32(1) << (n_bits - 1 - it))

            def eq_body(kc, acc):
                pos = kc * tq + col
                e = jnp.where((key_ref[kc] == thr) & (pos < cand), 1.0, 0.0)
                return acc + jnp.sum(e, axis=-1, keepdims=True)
            n_eq = lax.fori_loop(0, n_chunks, eq_body, jnp.zeros((tq, 1), jnp.float32))
            return jnp.where(n_eq <= need, cand, cut)
        cut_ref[...] = lax.fori_loop(0, n_bits, cut_body, jnp.zeros((tq, 1), jnp.int32))

    cut = cut_ref[...]

    n_far = jnp.maximum(qi - 1, 0)
    rg = SA_GROUP * tq

    def chunk_rows(kc):
        return pl.ds(pl.multiple_of(kc * tq, tq), tq)

    for g in range(SA_KV_HEADS):
        heads = [g * SA_GROUP + j for j in range(SA_GROUP)]
        qg = jnp.concatenate([qs_ref[:, hh * LANES:(hh + 1) * LANES] for hh in heads], axis=0)
        cf_rows = jnp.concatenate([jnp.full((tq, 1), cfar_ref[hh], jnp.float32) for hh in heads], axis=0)

        def scores(kc, kind, qg=qg, heads=heads, g=g):
            key = key_ref[kc]
            sel = (key > thr) | ((key == thr) & (kc * tq + col < cut))
            s_all = _dot_nt(qg, ks_ref[chunk_rows(kc), g * SA_DH:(g + 1) * SA_DH])
            parts = []
            for j, hh in enumerate(heads):
                s = s_all[j * tq:(j + 1) * tq]
                if kind is not None:
                    s = s + tile_ref[hh, kind]
                parts.append(jnp.where(sel, s, NEG))
            s = jnp.concatenate(parts, axis=0)
            s_ref[kc] = s
            mpart_ref[...] = jnp.maximum(mpart_ref[...], _fold_lanes(s, jnp.maximum))

        mpart_ref[...] = jnp.full((rg, LANES), NEG, jnp.float32)
        _pairwise(n_far, lambda kc, f=scores: f(kc, None))
        m_far = jnp.max(mpart_ref[...], axis=-1, keepdims=True) + cf_rows
        mpart_ref[...] = jnp.full((rg, LANES), NEG, jnp.float32)

        @pl.when(qi >= 1)
        def _(f=scores):
            f(qi - 1, 1)

        scores(qi, 0)
        m = jnp.maximum(m_far, jnp.max(mpart_ref[...], axis=-1, keepdims=True))
        shift_ref[0] = jnp.broadcast_to(m - cf_rows, (rg, LANES))
        shift_ref[1] = jnp.broadcast_to(m, (rg, LANES))
        lpart_ref[...] = jnp.zeros((rg, LANES), jnp.float32)
        acc_ref[...] = jnp.zeros((rg, SA_DH), jnp.float32)

        def weights(kc, which, g=g):
            s = s_ref[kc]
            sh = shift_ref[which]
            ps = [jnp.exp(s[:, c * LANES:(c + 1) * LANES] - sh) for c in range(tq // LANES)]
            tot = ps[0]
            for pc in ps[1:]:
                tot = tot + pc
            lpart_ref[...] = lpart_ref[...] + tot
            p = jnp.concatenate(ps, axis=1).astype(BF16)
            acc_ref[...] = acc_ref[...] + _dot(p, vs_ref[chunk_rows(kc), g * SA_DH:(g + 1) * SA_DH])

        _pairwise(n_far, lambda kc, f=weights: f(kc, 0))

        @pl.when(qi >= 1)
        def _(f=weights):
            f(qi - 1, 1)

        weights(qi, 1)
        a = acc_ref[...] / jnp.sum(lpart_ref[...], axis=-1, keepdims=True)
        for j, hh in enumerate(heads):
            o_ref[:, hh * SA_DH:(hh + 1) * SA_DH] = a[j * tq:(j + 1) * tq].astype(o_ref.dtype)


def _sa_prompt(q_sa, q_ix, misc_q, misc_k, k_sa, v_sa, tiles, cfar, nb, t, tq, topk):
    n = q_sa.shape[0]
    nq = t // tq
    mw = misc_q.shape[1]
    kvw = SA_KV_HEADS * SA_DH
    once = dict(pipeline_mode=pl.Buffered(1))
    rg = SA_GROUP * tq
    return pl.pallas_call(
        functools.partial(_sa_prompt_kernel, tq=tq, topk=topk),
        grid=(nb, nq),
        in_specs=[pl.BlockSpec(memory_space=pltpu.SMEM),
                  pl.BlockSpec((tq, SA_HEADS * SA_DH), lambda b, i: (b * nq + i, 0)),
                  pl.BlockSpec((tq, IDX_HEADS * IDX_DK), lambda b, i: (b * nq + i, 0)),
                  pl.BlockSpec((tq, mw), lambda b, i: (b * nq + i, 0)),
                  pl.BlockSpec((t, mw), lambda b, i: (b, 0), **once),
                  pl.BlockSpec((t, kvw), lambda b, i: (b, 0), **once),
                  pl.BlockSpec((t, kvw), lambda b, i: (b, 0), **once),
                  pl.BlockSpec((SA_HEADS, 2, tq, tq), lambda b, i: (0, 0, 0, 0), **once)],
        out_specs=pl.BlockSpec((tq, SA_HEADS * SA_DH), lambda b, i: (b * nq + i, 0)),
        out_shape=jax.ShapeDtypeStruct((n, SA_HEADS * SA_DH), BF16),
        scratch_shapes=[pltpu.VMEM((2, t, LANES), BF16),
                        pltpu.VMEM((nq, tq, tq), jnp.int32),
                        pltpu.VMEM((nq, tq, tq), jnp.int16),
                        pltpu.VMEM((nq, tq, tq), jnp.int16),
                        pltpu.VMEM((tq, 1), jnp.int32),
                        pltpu.VMEM((nq, rg, tq), jnp.float32),
                        pltpu.VMEM((rg, LANES), jnp.float32),
                        pltpu.VMEM((2, rg, LANES), jnp.float32),
                        pltpu.VMEM((rg, LANES), jnp.float32),
                        pltpu.VMEM((rg, SA_DH), jnp.float32)],
        compiler_params=_cparams(("arbitrary", "arbitrary")),
        name="sa_prompt",
    )(cfar, q_sa, q_ix, misc_q, misc_k, k_sa, v_sa, tiles)


def _proj_gate_kernel(oda_ref, osa_ref, wpd_ref, wps_ref, gda_ref, gsa_ref, o_ref, wbf_ref):
    @pl.when(pl.program_id(1) == 0)
    def _():
        wbf_ref[0] = wpd_ref[...].astype(BF16)
        wbf_ref[1] = wps_ref[...].astype(BF16)

    a = _dot(oda_ref[...], wbf_ref[0])
    b = _dot(osa_ref[...], wbf_ref[1])
    o_ref[...] = (gda_ref[...].astype(jnp.float32) * a
                  + gsa_ref[...].astype(jnp.float32) * b).astype(o_ref.dtype)


def _proj_gate(o_da, o_sa, w_pd, w_ps, gates, d, tm=512, tn=512):
    n, kd = o_da.shape
    ks = o_sa.shape[1]
    tn = min(tn, d)
    tm = min(tm, n)
    nj = d // tn
    return pl.pallas_call(
        _proj_gate_kernel,
        grid=(nj, n // tm),
        in_specs=[pl.BlockSpec((tm, kd), lambda j, i: (i, 0)),
                  pl.BlockSpec((tm, ks), lambda j, i: (i, 0)),
                  pl.BlockSpec((kd, tn), lambda j, i: (0, j)),
                  pl.BlockSpec((ks, tn), lambda j, i: (0, j)),
                  pl.BlockSpec((tm, tn), lambda j, i: (i, j)),
                  pl.BlockSpec((tm, tn), lambda j, i: (i, nj + j))],
        out_specs=pl.BlockSpec((tm, tn), lambda j, i: (i, j)),
        out_shape=jax.ShapeDtypeStruct((n, d), BF16),
        scratch_shapes=[pltpu.VMEM((2, kd, tn), BF16)],
        compiler_params=_cparams(("arbitrary", "arbitrary")),
        name="proj_gate",
    )(o_da, o_sa, w_pd, w_ps, gates, gates)


def _mm_resid_kernel(m_ref, w_ref, x_ref, gt_ref, o_ref, wbf_ref):
    @pl.when(pl.program_id(1) == 0)
    def _():
        wbf_ref[...] = w_ref[...].astype(BF16)

    o_ref[...] = x_ref[...] + gt_ref[0] * _dot(m_ref[...], wbf_ref[...])


def _mm_resid(m, w, x, gt, rows_per_batch, tm=512, tn=512):
    n, k = m.shape
    d = w.shape[1]
    tn = min(tn, d)
    tm = min(tm, n)
    per = rows_per_batch // tm
    return pl.pallas_call(
        _mm_resid_kernel,
        grid=(d // tn, n // tm),
        in_specs=[pl.BlockSpec((tm, k), lambda j, i: (i, 0)),
                  pl.BlockSpec((k, tn), lambda j, i: (0, j)),
                  pl.BlockSpec((tm, tn), lambda j, i: (i, j)),
                  pl.BlockSpec((1, 1, tn), lambda j, i: (i // per, 0, j))],
        out_specs=pl.BlockSpec((tm, tn), lambda j, i: (i, j)),
        out_shape=jax.ShapeDtypeStruct((n, d), jnp.float32),
        scratch_shapes=[pltpu.VMEM((k, tn), BF16)],
        compiler_params=_cparams(("arbitrary", "arbitrary")),
        name="mm_resid",
    )(m, w, x, gt[:, None, :])


def _norm_router_kernel(x_ref, g_ref, sc_ref, sh_ref, wr_ref, th_ref, tl_ref, h_ref, lg_ref):
    last = pl.num_programs(0) - 1

    @pl.when(pl.program_id(0) < last)
    def _():
        x = x_ref[...]
        y = x * lax.rsqrt(jnp.mean(x * x, axis=-1, keepdims=True) + EPS) * g_ref[...]
        h = y * (1.0 + sc_ref[0]) + sh_ref[0]
        h_ref[...] = h.astype(h_ref.dtype)
        lg_ref[...] = _dot_nt(_bf(wr_ref[...]), _bf(h))

    @pl.when(pl.program_id(0) == last)
    def _():
        h_ref[...] = th_ref[...]
        lg_ref[...] = tl_ref[...]


def _norm_router(x, g, sc, sh, w_router_t, tail_h, tail_lg, rows_per_batch, tm):
    n, d = x.shape
    per = rows_per_batch // tm
    nt = n // tm
    assert tail_h.shape == (tm, d) and tail_lg.shape == (N_EXPERTS, tm)
    row = lambda i: jnp.minimum(i, nt - 1)
    return pl.pallas_call(
        _norm_router_kernel,
        grid=(nt + 1,),
        in_specs=[pl.BlockSpec((tm, d), lambda i: (row(i), 0)),
                  pl.BlockSpec((1, d), lambda i: (0, 0)),
                  pl.BlockSpec((1, 1, d), lambda i: (row(i) // per, 0, 0)),
                  pl.BlockSpec((1, 1, d), lambda i: (row(i) // per, 0, 0)),
                  pl.BlockSpec((N_EXPERTS, d), lambda i: (0, 0)),
                  pl.BlockSpec((tm, d), lambda i: (0, 0)),
                  pl.BlockSpec((N_EXPERTS, tm), lambda i: (0, 0))],
        out_specs=[pl.BlockSpec((tm, d), lambda i: (i, 0)),
                   pl.BlockSpec((N_EXPERTS, tm), lambda i: (0, i))],
        out_shape=[jax.ShapeDtypeStruct((n + tm, d), BF16),
                   jax.ShapeDtypeStruct((N_EXPERTS, n + tm), jnp.float32)],
        compiler_params=_cparams(("arbitrary",)),
        name="norm_router",
    )(x, g.reshape(1, d), sc[:, None, :], sh[:, None, :], w_router_t, tail_h, tail_lg)


def _route_kernel(lg_ref, b_ref, eidx_ref, wts_ref, rank_ref, cnt_ref, *, tn):
    shape = (N_GROUPS, GROUP_SIZE, tn)
    sc = _sigmoid(lg_ref[...])
    biased = sc + b_ref[...]
    e_iota = lax.broadcasted_iota(jnp.int32, shape, 1)
    g_iota3 = lax.broadcasted_iota(jnp.int32, shape, 0)
    flat_iota = g_iota3 * GROUP_SIZE + e_iota
    g_iota = lax.broadcasted_iota(jnp.int32, (N_GROUPS, 1, tn), 0)
    ninf = -jnp.inf

    m1 = jnp.max(biased, axis=1, keepdims=True)
    first = jnp.min(jnp.where(biased == m1, e_iota, GROUP_SIZE), axis=1, keepdims=True)
    m2 = jnp.max(jnp.where(e_iota == first, ninf, biased), axis=1, keepdims=True)
    cur = m1 + m2
    gsel = jnp.zeros((N_GROUPS, 1, tn), jnp.float32)
    for _ in range(TOPK_GROUPS):
        mx = jnp.max(cur, axis=0, keepdims=True)
        idx = jnp.min(jnp.where(cur == mx, g_iota, N_GROUPS), axis=0, keepdims=True)
        hit = g_iota == idx
        gsel = jnp.where(hit, 1.0, gsel)
        cur = jnp.where(hit, ninf, cur)

    cur = jnp.where(gsel > 0.0, biased, ninf)
    hits, ws = [], []
    for k in range(TOP_K):
        mx = jnp.max(jnp.max(cur, axis=1, keepdims=True), axis=0, keepdims=True)
        cand = jnp.where(cur == mx, flat_iota, N_EXPERTS)
        idx = jnp.min(jnp.min(cand, axis=1, keepdims=True), axis=0, keepdims=True)
        hit = flat_iota == idx
        w = jnp.sum(jnp.sum(jnp.where(hit, sc, 0.0), axis=1, keepdims=True), axis=0, keepdims=True)
        eidx_ref[k:k + 1, :] = idx.reshape(1, tn)
        hits.append(hit)
        ws.append(w)
        cur = jnp.where(hit, ninf, cur)
    wsum = ws[0]
    for w in ws[1:]:
        wsum = wsum + w
    for k in range(TOP_K):
        wts_ref[k:k + 1, :] = (ws[k] / wsum * ROUTED_SCALE).reshape(1, tn)

    member = jnp.zeros(shape, jnp.float32)
    for hit in hits:
        member = jnp.where(hit, 1.0, member)
    member2 = member.reshape(N_EXPERTS, tn)
    r = lax.broadcasted_iota(jnp.int32, (tn, tn), 0)
    c = lax.broadcasted_iota(jnp.int32, (tn, tn), 1)
    upper = jnp.where(r < c, 1.0, 0.0).astype(BF16)
    prefix = _dot(member2.astype(BF16), upper).reshape(shape)
    for k in range(TOP_K):
        rk = jnp.sum(jnp.sum(jnp.where(hits[k], prefix, 0.0), axis=1, keepdims=True), axis=0, keepdims=True)
        rank_ref[k:k + 1, :] = rk.reshape(1, tn)
    cnt = jnp.sum(member2, axis=1, keepdims=True)
    cnt_ref[0] = jnp.broadcast_to(cnt, (N_EXPERTS, LANES))


def _route(logits_t, b_router, tn):
    n_pad = logits_t.shape[1]
    nt = n_pad // tn
    lg3 = logits_t.reshape(N_GROUPS, GROUP_SIZE, n_pad)
    b3 = b_router.astype(jnp.float32).reshape(N_GROUPS, GROUP_SIZE, 1)
    row = lambda dt: jax.ShapeDtypeStruct((TOP_K, n_pad), dt)
    return pl.pallas_call(
        functools.partial(_route_kernel, tn=tn),
        grid=(nt,),
        in_specs=[pl.BlockSpec((N_GROUPS, GROUP_SIZE, tn), lambda i: (0, 0, i)),
                  pl.BlockSpec((N_GROUPS, GROUP_SIZE, 1), lambda i: (0, 0, 0))],
        out_specs=[pl.BlockSpec((TOP_K, tn), lambda i: (0, i)),
                   pl.BlockSpec((TOP_K, tn), lambda i: (0, i)),
                   pl.BlockSpec((TOP_K, tn), lambda i: (0, i)),
                   pl.BlockSpec((1, N_EXPERTS, LANES), lambda i: (i, 0, 0))],
        out_shape=[row(jnp.int32), row(jnp.float32), row(jnp.float32),
                   jax.ShapeDtypeStruct((nt, N_EXPERTS, LANES), jnp.float32)],
        compiler_params=_cparams(("arbitrary",)),
        name="route",
    )(lg3, b3)


def _dest_kernel(eidx_ref, rank_ref, base_ref, o_ref, *, tn):
    e_iota = lax.broadcasted_iota(jnp.int32, (N_EXPERTS, tn), 0)
    base = base_ref[0][:, :1]
    for k in range(TOP_K):
        onehot = e_iota == eidx_ref[k:k + 1, :]
        b = jnp.sum(jnp.where(onehot, base, 0.0), axis=0, keepdims=True)
        o_ref[k:k + 1, :] = (b + rank_ref[k:k + 1, :]).astype(jnp.int32)


def _dest(eidx, rank, base, tn):
    n_pad = eidx.shape[1]
    return pl.pallas_call(
        functools.partial(_dest_kernel, tn=tn),
        grid=(n_pad // tn,),
        in_specs=[pl.BlockSpec((TOP_K, tn), lambda i: (0, i)),
                  pl.BlockSpec((TOP_K, tn), lambda i: (0, i)),
                  pl.BlockSpec((1, N_EXPERTS, LANES), lambda i: (i, 0, 0))],
        out_specs=pl.BlockSpec((TOP_K, tn), lambda i: (0, i)),
        out_shape=jax.ShapeDtypeStruct((TOP_K, n_pad), jnp.int32),
        compiler_params=_cparams(("arbitrary",)),
        name="dest",
    )(eidx, rank, base)


def _pack_words(lo_f32, hi_f32):
    lo = lax.shift_right_logical(pltpu.bitcast(lo_f32, jnp.uint32), jnp.uint32(16))
    hi = pltpu.bitcast(hi_f32, jnp.uint32) & jnp.uint32(0xFFFF0000)
    return hi | lo


def _unpack_words(w):
    lo = pltpu.bitcast(lax.shift_left(w, jnp.uint32(16)), jnp.float32)
    hi = pltpu.bitcast(w & jnp.uint32(0xFFFF0000), jnp.float32)
    return lo, hi


def _bf16_exact(x):
    return x.astype(BF16).astype(jnp.float32)


def _dispatch_kernel(pstart_ref, pcnt_ref, h_ref, dest_hbm, xs_hbm, dsm, pk, zrow, sem_d, sem_r,
                     *, tn, nt):
    i = pl.program_id(0)
    half = pk.shape[1]

    @pl.when(i < nt)
    def _():
        cp = pltpu.make_async_copy(dest_hbm.at[pl.ds(i * (TOP_K * tn), TOP_K * tn)], dsm, sem_d)
        cp.start()
        x = h_ref[...]
        pk[...] = _pack_words(x[:, :half].astype(jnp.float32), x[:, half:].astype(jnp.float32))
        cp.wait()

        def body(r, carry):
            for k in range(TOP_K):
                d = dsm[k * tn + r]
                pltpu.make_async_copy(pk.at[pl.ds(r, 1), :], xs_hbm.at[pl.ds(d, 1), :],
                                      sem_r).start(priority=k % 2)
            return carry

        lax.fori_loop(0, tn, body, 0)
        for k in range(TOP_K):
            pltpu.make_async_copy(pk, xs_hbm.at[pl.ds(0, tn), :], sem_r).wait()

    @pl.when(i == nt)
    def _():
        zrow[...] = jnp.zeros(zrow.shape, zrow.dtype)

        def per_expert(e, carry):
            s0 = pstart_ref[e]
            c = pcnt_ref[e]

            def start(r, cc):
                pltpu.make_async_copy(zrow.at[pl.ds(0, 1), :], xs_hbm.at[pl.ds(s0 + r, 1), :], sem_r).start()
                return cc

            def wait(r, cc):
                pltpu.make_async_copy(zrow.at[pl.ds(0, 1), :], xs_hbm.at[pl.ds(s0, 1), :], sem_r).wait()
                return cc

            lax.fori_loop(0, c, start, 0)
            lax.fori_loop(0, c, wait, 0)
            return carry

        lax.fori_loop(0, N_EXPERTS, per_expert, 0)

        t0 = pstart_ref[N_EXPERTS]
        groups = pcnt_ref[N_EXPERTS]
        rows8 = lambda r: pl.ds(pl.multiple_of(t0 + r * 8, 8), 8)

        def tstart(r, cc):
            pltpu.make_async_copy(zrow, xs_hbm.at[rows8(r), :], sem_r).start()
            return cc

        def twait(r, cc):
            pltpu.make_async_copy(zrow, xs_hbm.at[rows8(0), :], sem_r).wait()
            return cc

        lax.fori_loop(0, groups, tstart, 0)
        lax.fori_loop(0, groups, twait, 0)


def _dispatch(h2, dest_flat, pad_start, pad_cnt, n_rows, tn):
    n_pad, d = h2.shape
    nt = n_pad // tn
    grid_spec = pltpu.PrefetchScalarGridSpec(
        num_scalar_prefetch=2,
        grid=(nt + 1,),
        in_specs=[pl.BlockSpec((tn, d), lambda i, a, b: (jnp.minimum(i, nt - 1), 0)),
                  pl.BlockSpec(memory_space=pl.ANY)],
        out_specs=pl.BlockSpec(memory_space=pl.ANY),
        scratch_shapes=[pltpu.SMEM((TOP_K * tn,), jnp.int32),
                        pltpu.VMEM((tn, d // 2), jnp.uint32),
                        pltpu.VMEM((8, d // 2), jnp.uint32),
                        pltpu.SemaphoreType.DMA(()),
                        pltpu.SemaphoreType.DMA(())])
    return pl.pallas_call(
        functools.partial(_dispatch_kernel, tn=tn, nt=nt),
        grid_spec=grid_spec,
        out_shape=jax.ShapeDtypeStruct((n_rows, d // 2), jnp.uint32),
        compiler_params=_cparams(("arbitrary",)),
        name="dispatch",
    )(pad_start, pad_cnt, h2, dest_flat)


def _expert_kernel(blk_e_ref, nused_ref, next_e_ref, slot_ref, x_ref, wg_hbm, wu_hbm, wd_hbm, y_ref,
                   wg_f, wu_f, wd_f, wgb, wub, wdb, sems):
    i = pl.program_id(0)
    nused = nused_ref[0]
    ii = jnp.minimum(i, nused - 1)
    e = blk_e_ref[ii]
    e_prev = blk_e_ref[jnp.maximum(ii - 1, 0)]
    half = x_ref.shape[1]

    def copies(ex, sl):
        return [pltpu.make_async_copy(src.at[ex], dst.at[sl], sems.at[sl, t])
                for t, (src, dst) in enumerate(((wg_hbm, wg_f), (wu_hbm, wu_f), (wd_hbm, wd_f)))]

    @pl.when(i == 0)
    def _():
        for c in copies(e, slot_ref[e]):
            c.start()

    @pl.when((i < nused) & ((i == 0) | (e != e_prev)))
    def _():
        sl = slot_ref[e]
        for c in copies(e, sl):
            c.wait()
        nxt = next_e_ref[e]

        @pl.when(nxt >= 0)
        def _():
            for c in copies(nxt, 1 - sl):
                c.start()

        wgb[...] = wg_f[sl].astype(BF16)
        wub[...] = wu_f[sl].astype(BF16)
        wdb[...] = wd_f[sl].astype(BF16)

    @pl.when(i < nused)
    def _():
        lo, hi = _unpack_words(x_ref[...])
        xl = lo.astype(BF16)
        xh = hi.astype(BF16)
        g = _dot(xl, wgb[:half, :]) + _dot(xh, wgb[half:, :])
        u = _dot(xl, wub[:half, :]) + _dot(xh, wub[half:, :])
        hmid = (_silu(g) * u).astype(BF16)
        y = _dot(hmid, wdb[...])
        y_ref[...] = _pack_words(_bf16_exact(y[:, :half]), _bf16_exact(y[:, half:]))

    @pl.when(i >= nused)
    def _():
        y_ref[...] = jnp.zeros(y_ref.shape, y_ref.dtype)


def _experts(xs, blk_e, nused, next_e, slot, w_gate, w_up, w_down, tb):
    n_rows, half = xs.shape
    _, d, f = w_gate.shape
    nblk = n_rows // tb

    def xmap(i, be, nu, ne, sl):
        return (jnp.minimum(i, nu[0] - 1), 0)

    grid_spec = pltpu.PrefetchScalarGridSpec(
        num_scalar_prefetch=4,
        grid=(nblk,),
        in_specs=[pl.BlockSpec((tb, half), xmap),
                  pl.BlockSpec(memory_space=pl.ANY),
                  pl.BlockSpec(memory_space=pl.ANY),
                  pl.BlockSpec(memory_space=pl.ANY)],
        out_specs=pl.BlockSpec((tb, half), lambda i, be, nu, ne, sl: (i, 0)),
        scratch_shapes=[pltpu.VMEM((2, d, f), jnp.float32),
                        pltpu.VMEM((2, d, f), jnp.float32),
                        pltpu.VMEM((2, f, d), jnp.float32),
                        pltpu.VMEM((d, f), BF16),
                        pltpu.VMEM((d, f), BF16),
                        pltpu.VMEM((f, d), BF16),
                        pltpu.SemaphoreType.DMA((2, 3))])
    return pl.pallas_call(
        _expert_kernel,
        grid_spec=grid_spec,
        out_shape=jax.ShapeDtypeStruct((n_rows, half), jnp.uint32),
        compiler_params=_cparams(("arbitrary",)),
        name="experts",
    )(blk_e, nused, next_e, slot, xs, w_gate, w_up, w_down)


def _ffn_kernel(h_ref, wg_ref, wu_ref, wd_ref, o_ref, wgb, wub, wdb):
    @pl.when(pl.program_id(0) == 0)
    def _():
        wgb[...] = wg_ref[...].astype(BF16)
        wub[...] = wu_ref[...].astype(BF16)
        wdb[...] = wd_ref[...].astype(BF16)

    x = h_ref[...]
    hmid = (_silu(_dot(x, wgb[...])) * _dot(x, wub[...])).astype(BF16)
    o_ref[...] = _dot(hmid, wdb[...]).astype(o_ref.dtype)


def _ffn_shared(h2, wg, wu, wd, tm):
    n_pad, d = h2.shape
    f = wg.shape[1]
    return pl.pallas_call(
        _ffn_kernel,
        grid=(n_pad // tm,),
        in_specs=[pl.BlockSpec((tm, d), lambda i: (i, 0)),
                  pl.BlockSpec((d, f), lambda i: (0, 0)),
                  pl.BlockSpec((d, f), lambda i: (0, 0)),
                  pl.BlockSpec((f, d), lambda i: (0, 0))],
        out_specs=pl.BlockSpec((tm, d), lambda i: (i, 0)),
        out_shape=jax.ShapeDtypeStruct((n_pad, d), BF16),
        scratch_shapes=[pltpu.VMEM((d, f), BF16),
                        pltpu.VMEM((d, f), BF16),
                        pltpu.VMEM((f, d), BF16)],
        compiler_params=_cparams(("arbitrary",)),
        name="ffn_shared",
    )(h2, wg, wu, wd)


def _combine_kernel(dest_hbm, ys_hbm, wts_ref, sh_ref, x_ref, gt_ref, g_ref, o_ref,
                    dsm, buf, sem_d, sem_r, *, tn, tile0, final_norm):
    i = pl.program_id(0)
    cp = pltpu.make_async_copy(dest_hbm.at[pl.ds((tile0 + i) * (TOP_K * tn), TOP_K * tn)], dsm, sem_d)
    cp.start()
    cp.wait()

    def body(r, carry):
        for k in range(TOP_K):
            d = dsm[k * tn + r]
            pltpu.make_async_copy(ys_hbm.at[pl.ds(d, 1), :], buf.at[k, pl.ds(r, 1), :],
                                  sem_r).start(priority=k % 2)
        return carry

    lax.fori_loop(0, tn, body, 0)
    for k in range(TOP_K):
        pltpu.make_async_copy(ys_hbm.at[pl.ds(0, tn), :], buf.at[k], sem_r).wait()

    half = buf.shape[2]
    wts = wts_ref[...]
    acc_lo = jnp.zeros((tn, half), jnp.float32)
    acc_hi = jnp.zeros((tn, half), jnp.float32)
    for k in range(TOP_K):
        lo, hi = _unpack_words(buf[k])
        wk = wts[:, k:k + 1]
        acc_lo = acc_lo + wk * lo
        acc_hi = acc_hi + wk * hi
    sh = sh_ref[...].astype(jnp.float32)
    gt = gt_ref[0]
    x_lo = x_ref[:, :half] + gt[:, :half] * (acc_lo + sh[:, :half])
    x_hi = x_ref[:, half:] + gt[:, half:] * (acc_hi + sh[:, half:])
    if final_norm:
        ms = (jnp.sum(x_lo * x_lo, axis=-1, keepdims=True)
              + jnp.sum(x_hi * x_hi, axis=-1, keepdims=True)) / (2 * half)
        inv = lax.rsqrt(ms + EPS)
        g = g_ref[...]
        x_lo = x_lo * inv * g[:, :half]
        x_hi = x_hi * inv * g[:, half:]
    o_ref[:, :half] = x_lo
    o_ref[:, half:] = x_hi


def _combine(dest_flat, ys, wts_tok, shared, x1, gt3, g_final, tile0, tn, final_norm):
    rows, d = x1.shape
    nt = rows // tn
    gr = gt3.shape[1]
    per = nt // gt3.shape[0]
    return pl.pallas_call(
        functools.partial(_combine_kernel, tn=tn, tile0=tile0, final_norm=final_norm),
        grid=(nt,),
        in_specs=[pl.BlockSpec(memory_space=pl.ANY),
                  pl.BlockSpec(memory_space=pl.ANY),
                  pl.BlockSpec((tn, TOP_K), lambda i: (tile0 + i, 0)),
                  pl.BlockSpec((tn, d), lambda i: (tile0 + i, 0)),
                  pl.BlockSpec((tn, d), lambda i: (i, 0)),
                  pl.BlockSpec((1, gr, d), lambda i: (i // per, 0, 0)),
                  pl.BlockSpec((1, d), lambda i: (0, 0))],
        out_specs=pl.BlockSpec((tn, d), lambda i: (i, 0)),
        out_shape=jax.ShapeDtypeStruct((rows, d), jnp.float32),
        scratch_shapes=[pltpu.SMEM((TOP_K * tn,), jnp.int32),
                        pltpu.VMEM((TOP_K, tn, d // 2), jnp.uint32),
                        pltpu.SemaphoreType.DMA(()),
                        pltpu.SemaphoreType.DMA(())],
        compiler_params=_cparams(("arbitrary",)),
        name="combine",
    )(dest_flat, ys, wts_tok, shared, x1, gt3, g_final.reshape(1, d))


def _page_specs(shape, n, pg):
    def mk(u):
        return pl.BlockSpec((1,) + shape, lambda b, j, pt: (pt[b, j * pg + u], 0, 0))
    return [mk(u) for u in range(n)]


def _kv_page_specs(kshape, vshape, pg, ns):
    def mk(shape, first):
        def one(u):
            def index(b, j, pt):
                step = jnp.minimum(j, ns - 1) if first else jnp.maximum(j - ns, 0)
                return (pt[b, step * pg + u], 0, 0)
            return pl.BlockSpec((1,) + shape, index)
        return [one(u) for u in range(pg)]
    return mk(kshape, True) + mk(vshape, False)


def _softmax_pages(s_ref):
    s = s_ref[...]
    m = jnp.max(jnp.max(s, axis=0, keepdims=True), axis=2, keepdims=True)
    e = jnp.exp(s - m)
    return e / jnp.sum(jnp.sum(e, axis=0, keepdims=True), axis=2, keepdims=True)


def _da_sample_kernel(pt_ref, lam_ref, q_ref, kn_ref, vn_ref, bl_ref, cf_ref, b0_ref, ex_ref, hm_ref,
                      *rest, pg, n_pages):
    k_refs, v_refs = rest[:pg], rest[pg:2 * pg]
    o_ref, s_ref, a_ref, acc_ref = rest[2 * pg:]
    j = pl.program_id(1)
    ns = n_pages // pg
    r = q_ref.shape[1]
    nh = r // 2
    lane = lax.broadcasted_iota(jnp.int32, (r, PAGE_SIZE), 1)

    @pl.when(j < ns)
    def _():
        q = q_ref[0]
        qb = _bf(q)
        for u in range(pg):
            page = j * pg + u
            s = _dot(qb, _bf(k_refs[u][0]))
            s_ref[page] = s + jnp.where(page == (n_pages - 1), bl_ref[...], cf_ref[...])

        @pl.when(j == 0)
        def _():
            s_new = jnp.sum(_rounded(q) * _rounded(kn_ref[0]), axis=-1, keepdims=True) + b0_ref[...]
            s_ref[n_pages] = jnp.where(lane == 0, s_new, NEG)

    @pl.when(j == ns)
    def _():
        p = _softmax_pages(s_ref)
        a = p[:, :nh, :] - lam_ref[0] * p[:, nh:, :]
        a_ref[...] = _bf(jnp.concatenate([a, jnp.zeros_like(a)], axis=1))
        a_new = a_ref[n_pages][:, 0:1].astype(jnp.float32)
        acc_ref[...] = a_new * _rounded(vn_ref[0])

    @pl.when(j >= ns)
    def _():
        acc = acc_ref[...]
        for u in range(pg):
            page = (j - ns) * pg + u
            pe = _bf(_dot(a_ref[page], ex_ref[...]) * hm_ref[...])
            acc = acc + _dot(pe, _bf(v_refs[u][0]))
        acc_ref[...] = acc

    @pl.when(j == pl.num_programs(1) - 1)
    def _():
        o_ref[0] = acc_ref[...]


def _da_sample(page_table, lam, qbd, knew, vnew, bias_last, cfar, bias0, kt, v2, pg):
    nb, n_pages = page_table.shape
    r, w = qbd.shape[1:]
    rows_v, dv = v2.shape[1:]
    nh = rows_v // PAGE_SIZE
    ns = n_pages // pg
    col = jnp.arange(rows_v, dtype=jnp.int32)
    expand = (col[None, :] // nh == jnp.arange(PAGE_SIZE, dtype=jnp.int32)[:, None]).astype(BF16)
    head_mask = (col[None, :] % nh == jnp.arange(r, dtype=jnp.int32)[:, None]).astype(jnp.float32)
    full = lambda shp: pl.BlockSpec(shp, lambda b, j, pt: (0,) * len(shp))
    per_b = lambda shp: pl.BlockSpec((1,) + shp, lambda b, j, pt: (b, 0, 0))
    grid_spec = pltpu.PrefetchScalarGridSpec(
        num_scalar_prefetch=1,
        grid=(nb, 2 * ns),
        in_specs=[pl.BlockSpec(memory_space=pltpu.SMEM),
                  per_b((r, w)), per_b((1, w)), per_b((r, dv)),
                  full((r, PAGE_SIZE)), full((r, 1)), full((r, 1)),
                  full((PAGE_SIZE, rows_v)), full((r, rows_v))]
                 + _kv_page_specs((w, PAGE_SIZE), (rows_v, dv), pg, ns),
        out_specs=per_b((r, dv)),
        scratch_shapes=[pltpu.VMEM((n_pages + 1, r, PAGE_SIZE), jnp.float32),
                        pltpu.VMEM((n_pages + 1, r, PAGE_SIZE), BF16),
                        pltpu.VMEM((r, dv), jnp.float32)])
    return pl.pallas_call(
        functools.partial(_da_sample_kernel, pg=pg, n_pages=n_pages),
        grid_spec=grid_spec,
        out_shape=jax.ShapeDtypeStruct((nb, r, dv), jnp.float32),
        compiler_params=_cparams(("arbitrary", "arbitrary")),
        name="da_sample",
    )(page_table, lam, qbd, knew, vnew, bias_last, cfar, bias0, expand, head_mask,
      *([kt] * pg), *([v2] * pg))


def _idx_sample_kernel(pt_ref, q_ref, w_ref, kn_ref, *rest, pg, n_pages, topk):
    k_refs = rest[:pg]
    sel_ref, sc_ref = rest[pg:]
    j = pl.program_id(1)
    q = q_ref[0]
    w = _rounded(w_ref[0])
    scale = IDX_DK ** -0.5 * IDX_HEADS ** -0.5
    rows = sc_ref.shape[0]
    lane = lax.broadcasted_iota(jnp.int32, (1, PAGE_SIZE), 1)

    @pl.when(j == 0)
    def _():
        sc_ref[...] = jnp.full(sc_ref.shape, -jnp.inf, jnp.float32)
        d = jnp.maximum(jnp.sum(_rounded(q) * _rounded(kn_ref[0]), axis=-1, keepdims=True), 0.0)
        s_new = jnp.sum(w * _rounded(d), axis=0, keepdims=True) * scale
        sc_ref[n_pages:n_pages + 1, :] = jnp.where(lane == 0, s_new, -jnp.inf)

    qb = _bf(q)
    for u in range(pg):
        d = _rounded(jnp.maximum(_dot(qb, _bf(k_refs[u][0])), 0.0))
        sc_ref[pl.ds(j * pg + u, 1), :] = jnp.sum(w * d, axis=0, keepdims=True) * scale

    @pl.when(j == pl.num_programs(1) - 1)
    def _():
        key = _float_key(sc_ref[...])
        pos = (lax.broadcasted_iota(jnp.int32, key.shape, 0) * PAGE_SIZE
               + lax.broadcasted_iota(jnp.int32, key.shape, 1))

        def bit_body(it, thr):
            cand = thr + (jnp.int32(1) << (31 - it))
            cnt = jnp.sum(jnp.where(key >= cand, 1.0, 0.0))
            return jnp.where(cnt >= float(topk), cand, thr)

        thr = lax.fori_loop(0, 32, bit_body, jnp.int32(INT_MIN))
        thr = jnp.maximum(thr, jnp.int32(KEY_NEG_INF + 1))
        need = float(topk) - jnp.sum(jnp.where(key > thr, 1.0, 0.0))
        n_bits = max(1, int(rows * PAGE_SIZE).bit_length())

        def cut_body(it, cut):
            cand = cut + (jnp.int32(1) << (n_bits - 1 - it))
            n_eq = jnp.sum(jnp.where((key == thr) & (pos < cand), 1.0, 0.0))
            return jnp.where(n_eq <= need, cand, cut)

        cut = lax.fori_loop(0, n_bits, cut_body, jnp.int32(0))
        sel_ref[0] = jnp.where((key > thr) | ((key == thr) & (pos < cut)), 1.0, 0.0)


def _idx_sample(page_table, qix, wix, knew, kc, pg, topk):
    nb, n_pages = page_table.shape
    rows = -(-(n_pages + 1) // 8) * 8
    per_b = lambda shp: pl.BlockSpec((1,) + shp, lambda b, j, pt: (b, 0, 0))
    grid_spec = pltpu.PrefetchScalarGridSpec(
        num_scalar_prefetch=1,
        grid=(nb, n_pages // pg),
        in_specs=[per_b((IDX_HEADS, IDX_DK)), per_b((IDX_HEADS, 1)), per_b((1, IDX_DK))]
                 + _page_specs((IDX_DK, PAGE_SIZE), pg, pg),
        out_specs=per_b((rows, PAGE_SIZE)),
        scratch_shapes=[pltpu.VMEM((rows, PAGE_SIZE), jnp.float32)])
    return pl.pallas_call(
        functools.partial(_idx_sample_kernel, pg=pg, n_pages=n_pages, topk=topk),
        grid_spec=grid_spec,
        out_shape=jax.ShapeDtypeStruct((nb, rows, PAGE_SIZE), jnp.float32),
        compiler_params=_cparams(("arbitrary", "arbitrary")),
        name="idx_sample",
    )(page_table, qix, wix, knew, *([kc] * pg))


def _sa_sample_kernel(pt_ref, q_ref, kn_ref, vn_ref, sel_ref, bl_ref, cf_ref, b0_ref, gm_ref, *rest,
                      pg, n_pages):
    k_refs, v_refs = rest[:pg], rest[pg:2 * pg]
    o_ref, s_ref, p_ref, acc_ref = rest[2 * pg:]
    j = pl.program_id(1)
    ns = n_pages // pg
    scale = SA_DH ** -0.5
    r, cols = gm_ref.shape
    lane = lax.broadcasted_iota(jnp.int32, (r, cols), 1)

    @pl.when(j < ns)
    def _():
        q = q_ref[0]
        qb = _bf(q)
        for u in range(pg):
            page = j * pg + u
            s = _dot_nt(qb, _bf(k_refs[u][0])) * scale
            s = s + jnp.where(page == (n_pages - 1), bl_ref[...], cf_ref[...])
            keep = (sel_ref[0, pl.ds(page, 1), :] > 0.0) & (gm_ref[...] > 0.0)
            s_ref[page] = jnp.where(keep, s, NEG)

        @pl.when(j == 0)
        def _():
            on = sel_ref[0, n_pages:n_pages + 1, 0:1] > 0.0
            s_new = jnp.sum(_rounded(q) * _rounded(kn_ref[0]), axis=-1, keepdims=True) * scale + b0_ref[...]
            s_ref[n_pages] = jnp.where((lane == 0) & on, s_new, NEG)

    @pl.when(j == ns)
    def _():
        p_ref[...] = _bf(_softmax_pages(s_ref))
        acc_ref[...] = p_ref[n_pages][:, 0:1].astype(jnp.float32) * _rounded(vn_ref[0])

    @pl.when(j >= ns)
    def _():
        acc = acc_ref[...]
        for u in range(pg):
            acc = acc + _dot(p_ref[(j - ns) * pg + u], _bf(v_refs[u][0]))
        acc_ref[...] = acc

    @pl.when(j == pl.num_programs(1) - 1)
    def _():
        o_ref[0] = acc_ref[...]


def _sa_sample(page_table, q, knew, vnew, sel2, bias_last2, cfar, bias0, group_mask, k2, v2, pg):
    nb, n_pages = page_table.shape
    r, dh = q.shape[1:]
    srows, cols = sel2.shape[1:]
    ns = n_pages // pg
    full = lambda shp: pl.BlockSpec(shp, lambda b, j, pt: (0,) * len(shp))
    per_b = lambda shp: pl.BlockSpec((1,) + shp, lambda b, j, pt: (b, 0, 0))
    grid_spec = pltpu.PrefetchScalarGridSpec(
        num_scalar_prefetch=1,
        grid=(nb, 2 * ns),
        in_specs=[per_b((r, dh)), per_b((r, dh)), per_b((r, dh)), per_b((srows, cols)),
                  full((r, cols)), full((r, 1)), full((r, 1)), full((r, cols))]
                 + _kv_page_specs((cols, dh), (cols, dh), pg, ns),
        out_specs=per_b((r, dh)),
        scratch_shapes=[pltpu.VMEM((n_pages + 1, r, cols), jnp.float32),
                        pltpu.VMEM((n_pages + 1, r, cols), BF16),
                        pltpu.VMEM((r, dh), jnp.float32)])
    return pl.pallas_call(
        functools.partial(_sa_sample_kernel, pg=pg, n_pages=n_pages),
        grid_spec=grid_spec,
        out_shape=jax.ShapeDtypeStruct((nb, r, dh), jnp.float32),
        compiler_params=_cparams(("arbitrary", "arbitrary")),
        name="sa_sample",
    )(page_table, q, knew, vnew, sel2, bias_last2, cfar, bias0, group_mask, *([k2] * pg), *([v2] * pg))


ROW_TILE = 256
COMBINE_TILE = 128
EXPERT_BLOCK = 256
TAIL_ROWS = 256
PAGES_PER_STEP_DA = 8
PAGES_PER_STEP_SA = 16
PAGES_PER_STEP_IDX = 32


def _rms_rows(x, g):
    return x * lax.rsqrt(jnp.mean(x * x, axis=-1, keepdims=True) + EPS) * g


def _pad_rows(x, rows):
    return jnp.pad(x, ((0, rows - x.shape[0]), (0, 0)))


def _tile_major(dest, tn):
    k, n = dest.shape
    return dest.reshape(k, n // tn, tn).transpose(1, 0, 2).reshape(-1)


def kernel(x_prompt, x_sample, c_prompt, c_sample, cache_da_k, cache_da_v, cache_sa_k, cache_sa_v, cache_idx_k, page_table, rel_bias_table, w_ada, b_ada, g_attn, g_ffn, w_in, lambda_q1, lambda_k1, lambda_q2, lambda_k2, g_subln, w_proj_da, w_proj_sa, w_out, w_router, b_router, w_gate, w_up, w_down, w_sh_gate, w_sh_up, w_sh_down, g_final):
    f32, bf16 = jnp.float32, BF16
    nb, t, d = x_prompt.shape
    nbs, dec_seq, _ = x_sample.shape
    assert dec_seq == 1
    depth = w_in.shape[0]
    n_pages = page_table.shape[1]
    past_len = n_pages * PAGE_SIZE
    n = nb * t
    n_pad = n + TAIL_ROWS
    tq = min(256, t)
    assert tq >= MAX_DISTANCE and t % tq == 0 and n % ROW_TILE == 0 and nbs <= TAIL_ROWS
    assert TAIL_ROWS == ROW_TILE and EXPERT_BLOCK % 8 == 0
    topk_p = min(SA_TOPK_MAX, t // 4)
    topk_s = min(SA_TOPK_MAX, (past_len + dec_seq) // 4)
    assert topk_p <= tq

    sizes = [DA_HEADS * 2 * DA_DK, DA_HEADS * 2 * DA_DK, DA_HEADS * DA_DV, SA_HEADS * SA_DH,
             SA_KV_HEADS * SA_DH, SA_KV_HEADS * SA_DH, IDX_HEADS * IDX_DK, IDX_DK, IDX_HEADS, d, d]
    offs = [sum(sizes[:i]) for i in range(len(sizes) + 1)]
    (o_qda, o_kda, o_vda, o_qsa, o_ksa, o_vsa, o_qix, o_kix, o_wix, o_gda, o_gsa, _) = offs
    misc_w = 2 * LANES

    table = rel_bias_table.astype(f32)
    cfar = table[N_BUCKETS - 1]
    tiles_da = _near_tiles(table[:, :DA_HEADS], tq)
    tiles_sa = _near_tiles(table[:, DA_HEADS:], tq)
    last_dist = past_len - ((n_pages - 1) * PAGE_SIZE + jnp.arange(PAGE_SIZE, dtype=jnp.int32))
    bias_last = _bias_by_distance(table, last_dist)
    bias0 = _bias_by_distance(table, jnp.zeros((1,), jnp.int32))
    rep2 = lambda a: jnp.repeat(a, 2, axis=0)

    xp = x_prompt.reshape(n, d)
    xs = x_sample.reshape(nbs, d)
    c_all = jnp.concatenate([c_prompt, c_sample], axis=0)
    leaves_p, leaves_s = [], []
    for l in range(depth):
        lam_init = 0.8 - 0.6 * math.exp(-0.3 * l)
        lam = (jnp.exp(jnp.sum(lambda_q1[l].astype(f32) * lambda_k1[l].astype(f32)))
               - jnp.exp(jnp.sum(lambda_q2[l].astype(f32) * lambda_k2[l].astype(f32))) + lam_init)
        mod = _linear_small(c_all, w_ada[l], b_ada[l], silu_in=True)
        sh1, sc1, gt1, sh2, sc2, gt2 = jnp.split(mod[:nb], 6, axis=-1)
        sh1s, sc1s, gt1s, sh2s, sc2s, gt2s = jnp.split(mod[nb:], 6, axis=-1)
        w = w_in[l]

        h = _norm_mod(xp, g_attn[l], sc1, sh1, t, ROW_TILE)
        q_da, = _mm(h, w, o_qda, sizes[0], (bf16,), scale=DA_DK ** -0.5)
        k_da_b, k_da_t = _mm(h, w, o_kda, sizes[1], (bf16,), rows_per_batch=t)
        k_da = k_da_t.reshape(nb, DA_HEADS, 2, DA_DK, t).transpose(0, 4, 1, 2, 3)
        v_da, v_da_b = _mm(h, w, o_vda, sizes[2], (f32, bf16))
        q_sa, = _mm(h, w, o_qsa, sizes[3], (bf16,), scale=SA_DH ** -0.5)
        k_sa, k_sa_b = _mm(h, w, o_ksa, sizes[4], (f32, bf16))
        v_sa, v_sa_b = _mm(h, w, o_vsa, sizes[5], (f32, bf16))
        q_ix, = _mm(h, w, o_qix, sizes[6], (bf16,), tn=512)
        misc, misc_b = _mm(h, w, o_kix, misc_w, (f32, bf16))
        gates, = _mm(h, w[:, o_gda:], 0, 2 * d, (bf16,), sigmoid=True)
        k_ix = misc[:, :IDX_DK]

        o_da = _da_prompt(q_da, k_da_b, v_da_b, tiles_da, cfar[:DA_HEADS], lam.reshape(1),
                          g_subln[l].astype(f32), nb, t, tq, 1.0 - lam_init)
        o_sa = _sa_prompt(q_sa, q_ix, misc, misc_b, k_sa_b, v_sa_b, tiles_sa, cfar[DA_HEADS:], nb, t, tq, topk_p)
        m = _proj_gate(o_da, o_sa, w_proj_da[l], w_proj_sa[l], gates, d)
        x1 = _mm_resid(m, w_out[l], xp, gt1, t)

        hs = _rms_rows(xs, g_attn[l].astype(f32)) * (1.0 + sc1s) + sh1s
        ps = _linear_small(hs, w)
        seg = lambda i: ps[:, offs[i]:offs[i + 1]]
        q_da_s, k_da_s, v_da_s, q_sa_s, k_sa_s, v_sa_s, q_ix_s, k_ix_s, w_ix_s, gda_s, gsa_s = [
            seg(i) for i in range(11)]

        r_da = 2 * DA_HEADS
        q16 = (q_da_s * DA_DK ** -0.5).reshape(nbs, DA_HEADS, 2, DA_DK).transpose(0, 2, 1, 3).reshape(nbs, r_da, DA_DK)
        blk_of_row = 2 * (jnp.arange(r_da) % DA_HEADS) + jnp.arange(r_da) // DA_HEADS
        place = (blk_of_row[:, None] == jnp.arange(r_da)[None, :]).astype(f32)
        qbd_da = (place[None, :, :, None] * q16[:, :, None, :]).reshape(nbs, r_da, -1)
        both = lambda a: jnp.concatenate([a, a], axis=0)
        kt_da = cache_da_k[l].transpose(0, 2, 3, 4, 1).reshape(-1, DA_HEADS * 2 * DA_DK, PAGE_SIZE)
        v2_da = cache_da_v[l].reshape(-1, PAGE_SIZE * DA_HEADS, DA_DV)
        v_rows = jnp.pad(v_da_s.reshape(nbs, DA_HEADS, DA_DV), ((0, 0), (0, r_da - DA_HEADS), (0, 0)))
        a_da = _da_sample(page_table, lam.reshape(1), qbd_da, k_da_s[:, None, :], v_rows,
                          both(bias_last[:DA_HEADS]), both(cfar[:DA_HEADS, None]), both(bias0[:DA_HEADS]),
                          kt_da, v2_da, math.gcd(PAGES_PER_STEP_DA, n_pages))
        o_da_s = a_da[:, :DA_HEADS]
        o_da_s = _rms_rows(o_da_s, g_subln[l].astype(f32)) * (1.0 - lam_init)

        sel = _idx_sample(page_table, q_ix_s.reshape(nbs, IDX_HEADS, IDX_DK), w_ix_s[:, :, None],
                          k_ix_s[:, None, :], cache_idx_k[l].transpose(0, 2, 1),
                          math.gcd(PAGES_PER_STEP_IDX, n_pages), topk_s)
        r_sa = 2 * SA_HEADS
        kv_of_row = jnp.minimum(jnp.arange(r_sa) // SA_GROUP, SA_KV_HEADS - 1)
        pad_sa = lambda a: jnp.pad(a, ((0, 0), (0, r_sa - SA_HEADS), (0, 0)))
        rep_kv = lambda a: jnp.repeat(a, SA_KV_HEADS, axis=-1)
        col_kv = jnp.arange(PAGE_SIZE * SA_KV_HEADS) % SA_KV_HEADS
        k2_sa = cache_sa_k[l].reshape(-1, PAGE_SIZE * SA_KV_HEADS, SA_DH)
        v2_sa = cache_sa_v[l].reshape(-1, PAGE_SIZE * SA_KV_HEADS, SA_DH)
        a_sa = _sa_sample(page_table, pad_sa(q_sa_s.reshape(nbs, SA_HEADS, SA_DH)),
                          k_sa_s.reshape(nbs, SA_KV_HEADS, SA_DH)[:, kv_of_row],
                          v_sa_s.reshape(nbs, SA_KV_HEADS, SA_DH)[:, kv_of_row],
                          rep_kv(sel), rep_kv(_pad_rows(bias_last[DA_HEADS:], r_sa)),
                          _pad_rows(cfar[DA_HEADS:, None], r_sa), _pad_rows(bias0[DA_HEADS:], r_sa),
                          (col_kv[None, :] == kv_of_row[:, None]).astype(f32),
                          k2_sa, v2_sa, math.gcd(PAGES_PER_STEP_SA, n_pages))
        o_sa_s = a_sa[:, :SA_HEADS]

        pda = _linear_small(o_da_s.reshape(nbs, -1), w_proj_da[l])
        psa = _linear_small(o_sa_s.reshape(nbs, -1), w_proj_sa[l])
        ms = _sigmoid(gda_s) * pda + _sigmoid(gsa_s) * psa
        x1s = xs + gt1s * _linear_small(ms, w_out[l])
        h2s = _rms_rows(x1s, g_ffn[l].astype(f32)) * (1.0 + sc2s) + sh2s
        lg_s = _linear_small(h2s, w_router[l])

        h2_all, lg_all = _norm_router(x1, g_ffn[l], sc2, sh2, w_router[l].T,
                                      _pad_rows(h2s.astype(bf16), TAIL_ROWS),
                                      _pad_rows(lg_s, TAIL_ROWS).T, t, ROW_TILE)
        eidx, wts, rank, cnt = _route(lg_all, b_router[l], ROW_TILE)
        cnt_tile = cnt[:, :, 0]
        total = jnp.sum(cnt_tile, axis=0)
        padded = jnp.ceil(total / EXPERT_BLOCK) * EXPERT_BLOCK
        pends = jnp.cumsum(padded)
        pstart = pends - padded
        base = pstart[None, :] + jnp.cumsum(cnt_tile, axis=0) - cnt_tile
        dest = _dest(eidx, rank, jnp.broadcast_to(base[:, :, None], base.shape + (LANES,)), ROW_TILE)
        n_blk = -(-(n_pad * TOP_K) // EXPERT_BLOCK) + N_EXPERTS
        blk_start = (jnp.arange(n_blk) * EXPERT_BLOCK).astype(f32)
        blk_e = jnp.minimum(jnp.sum(pends[None, :] <= blk_start[:, None], axis=1), N_EXPERTS - 1).astype(jnp.int32)
        n_used = (pends[-1] / EXPERT_BLOCK).astype(jnp.int32).reshape(1)
        n_rows = n_blk * EXPERT_BLOCK
        pad_start = jnp.concatenate([pstart + total, pends[-1:]]).astype(jnp.int32)
        pad_cnt = jnp.concatenate([padded - total, (n_rows - pends[-1:]) / 8]).astype(jnp.int32)
        xs_sorted = _dispatch(h2_all, _tile_major(dest, ROW_TILE), pad_start, pad_cnt, n_rows, ROW_TILE)
        owns = padded > 0
        e_ids = jnp.arange(N_EXPERTS, dtype=jnp.int32)
        later = jnp.where(owns[None, :] & (e_ids[None, :] > e_ids[:, None]), e_ids[None, :], N_EXPERTS)
        next_owner = jnp.min(later, axis=1)
        next_owner = jnp.where(next_owner < N_EXPERTS, next_owner, -1).astype(jnp.int32)
        slot_of_e = ((jnp.cumsum(owns.astype(jnp.int32)) - 1) % 2).astype(jnp.int32)
        ys = _experts(xs_sorted, blk_e, n_used, next_owner, slot_of_e,
                      w_gate[l], w_up[l], w_down[l], EXPERT_BLOCK)
        shared = _ffn_shared(h2_all, w_sh_gate[l], w_sh_up[l], w_sh_down[l], ROW_TILE)
        dest_c = _tile_major(dest, COMBINE_TILE)
        wts_tok = wts.T
        last = l == depth - 1
        xp = _combine(dest_c, ys, wts_tok, shared, x1, gt2[:, None, :], g_final.astype(f32),
                      0, COMBINE_TILE, last)
        tail = _combine(dest_c, ys, wts_tok, shared, _pad_rows(x1s, TAIL_ROWS),
                        _pad_rows(gt2s, TAIL_ROWS).reshape(-1, COMBINE_TILE, d), g_final.astype(f32),
                        n // COMBINE_TILE, COMBINE_TILE, last)
        xs = tail[:nbs]
        leaves_p.append((k_da, v_da, k_sa, v_sa, k_ix))
        leaves_s.append((k_da_s, v_da_s, k_sa_s, v_sa_s, k_ix_s))

    shapes = [(DA_HEADS, 2, DA_DK), (DA_HEADS, DA_DV), (SA_KV_HEADS, SA_DH), (SA_KV_HEADS, SA_DH), (IDX_DK,)]
    out_p = [jnp.stack([lv[i].reshape((nb, t) + shapes[i]) for lv in leaves_p]) for i in range(5)]
    out_s = [jnp.stack([lv[i].reshape((nbs, dec_seq) + shapes[i]) for lv in leaves_s]) for i in range(5)]
    return (xp.reshape(nb, t, d), xs.reshape(nbs, dec_seq, d), *out_p, *out_s)
```

```python
import functools
import math

import jax
import jax.numpy as jnp
from jax import lax
from jax.experimental import pallas as pl
from jax.experimental.pallas import tpu as pltpu

DA_HEADS = 8
DA_DK = 64
DA_DV = 2 * DA_DK
SA_HEADS = 8
SA_KV_HEADS = 2
SA_DH = 128
SA_GROUP = SA_HEADS // SA_KV_HEADS
IDX_HEADS = 16
IDX_DK = 64
SA_TOPK_MAX = 256
N_BUCKETS = 32
MAX_DISTANCE = 128
N_EXPERTS = 64
N_GROUPS = 8
GROUP_SIZE = N_EXPERTS // N_GROUPS
TOPK_GROUPS = 4
TOP_K = 8
ROUTED_SCALE = 2.5
PAGE_SIZE = 128
EPS = 1e-6

LANES = 128
VMEM_LIMIT = 56 * 1024 * 1024

BF16 = jnp.bfloat16
NEG = -1e30
INT_MIN = -(2 ** 31)
KEY_NEG_INF = (0xFF800000 ^ 0x7FFFFFFF) - (1 << 32)

_NT = (((1,), (1,)), ((), ()))


def _cparams(sem):
    return pltpu.CompilerParams(dimension_semantics=sem, vmem_limit_bytes=VMEM_LIMIT)


def _dot(a, b):
    return jnp.dot(a, b, preferred_element_type=jnp.float32)


def _dot_nt(a, b):
    return lax.dot_general(a, b, _NT, preferred_element_type=jnp.float32)


def _bf(x):
    return x.astype(BF16)


def _rounded(x):
    return x.astype(BF16).astype(jnp.float32)


def _sigmoid(x):
    return 1.0 / (1.0 + jnp.exp(-x))


def _silu(x):
    return x * _sigmoid(x)


def _float_key(s):
    b = pltpu.bitcast(s, jnp.int32)
    return b ^ ((b >> 31) & jnp.int32(0x7FFFFFFF))


def _linear_small_kernel(x_ref, w_ref, b_ref, o_ref, *, silu_in):
    x = x_ref[...]
    if silu_in:
        x = _silu(x)
    o_ref[...] = _dot(_bf(x), _bf(w_ref[...])) + b_ref[...]


def _linear_small(x, w, b=None, *, silu_in=False, tn=512):
    m0, k = x.shape
    m = -(-m0 // 16) * 16
    x = jnp.pad(x, ((0, m - m0), (0, 0)))
    n = w.shape[1]
    tn = min(tn, n)
    if b is None:
        b = jnp.zeros((1, n), jnp.float32)
    out = pl.pallas_call(
        functools.partial(_linear_small_kernel, silu_in=silu_in),
        grid=(pl.cdiv(n, tn),),
        in_specs=[pl.BlockSpec((m, k), lambda j: (0, 0)),
                  pl.BlockSpec((k, tn), lambda j: (0, j)),
                  pl.BlockSpec((1, tn), lambda j: (0, j))],
        out_specs=pl.BlockSpec((m, tn), lambda j: (0, j)),
        out_shape=jax.ShapeDtypeStruct((m, n), jnp.float32),
        compiler_params=_cparams(("arbitrary",)),
        name="linear_small",
    )(x, w, b.reshape(1, n))
    return out[:m0]


def _norm_mod_kernel(x_ref, g_ref, sc_ref, sh_ref, o_ref):
    x = x_ref[...]
    y = x * lax.rsqrt(jnp.mean(x * x, axis=-1, keepdims=True) + EPS) * g_ref[...]
    o_ref[...] = (y * (1.0 + sc_ref[0]) + sh_ref[0]).astype(o_ref.dtype)


def _norm_mod(x, g, sc, sh, rows_per_batch, tm):
    n, d = x.shape
    per = rows_per_batch // tm
    return pl.pallas_call(
        _norm_mod_kernel,
        grid=(n // tm,),
        in_specs=[pl.BlockSpec((tm, d), lambda i: (i, 0)),
                  pl.BlockSpec((1, d), lambda i: (0, 0)),
                  pl.BlockSpec((1, 1, d), lambda i: (i // per, 0, 0)),
                  pl.BlockSpec((1, 1, d), lambda i: (i // per, 0, 0))],
        out_specs=pl.BlockSpec((tm, d), lambda i: (i, 0)),
        out_shape=jax.ShapeDtypeStruct((n, d), BF16),
        compiler_params=_cparams(("arbitrary",)),
        name="norm_mod",
    )(x, g.reshape(1, d), sc[:, None, :], sh[:, None, :])


def _mm_kernel(x_ref, w_ref, *rest, scale, sigmoid, n_out, transposed_out):
    o_refs, wbf_ref = rest[:n_out], rest[-1]

    @pl.when(pl.program_id(1) == 0)
    def _():
        wbf_ref[...] = w_ref[...].astype(BF16)

    acc = _dot(x_ref[...], wbf_ref[...])
    if scale != 1.0:
        acc = acc * scale
    if sigmoid:
        acc = _sigmoid(acc)
    for o in o_refs:
        o[...] = acc.astype(o.dtype)
    if transposed_out:
        rest[n_out][0] = acc.T


def _mm(x, w, col0, ncols, out_dtypes, *, scale=1.0, sigmoid=False, tm=512, tn=1024, rows_per_batch=None):
    m, k = x.shape
    tn = min(tn, ncols)
    tm = min(tm, m)
    assert col0 % tn == 0 and ncols % tn == 0 and m % tm == 0
    jb = col0 // tn
    out_specs = [pl.BlockSpec((tm, tn), lambda j, i: (i, j)) for _ in out_dtypes]
    out_shape = [jax.ShapeDtypeStruct((m, ncols), dt) for dt in out_dtypes]
    if rows_per_batch is not None:
        per = rows_per_batch // tm
        out_specs.append(pl.BlockSpec((1, tn, tm), lambda j, i: (i // per, j, i % per)))
        out_shape.append(jax.ShapeDtypeStruct((m // rows_per_batch, ncols, rows_per_batch), jnp.float32))
    outs = pl.pallas_call(
        functools.partial(_mm_kernel, scale=scale, sigmoid=sigmoid, n_out=len(out_dtypes),
                          transposed_out=rows_per_batch is not None),
        grid=(ncols // tn, m // tm),
        in_specs=[pl.BlockSpec((tm, k), lambda j, i: (i, 0)),
                  pl.BlockSpec((k, tn), lambda j, i: (0, jb + j))],
        out_specs=out_specs,
        out_shape=out_shape,
        scratch_shapes=[pltpu.VMEM((k, tn), BF16)],
        compiler_params=_cparams(("arbitrary", "arbitrary")),
        name="mm_cols",
    )(x, w)
    return outs


def _rel_bucket(dist):
    max_exact = N_BUCKETS // 2
    d = jnp.maximum(dist, 0)
    large = max_exact + (jnp.log(jnp.maximum(d, 1).astype(jnp.float32) / max_exact)
                         / math.log(MAX_DISTANCE / max_exact)
                         * (N_BUCKETS - max_exact)).astype(jnp.int32)
    large = jnp.minimum(large, N_BUCKETS - 1)
    return jnp.where(d < max_exact, d, large)


def _bias_by_distance(table, dists):
    return table[_rel_bucket(dists)].astype(jnp.float32).T


def _toeplitz_kernel(u_ref, o_ref):
    t = o_ref.shape[2]
    x = jnp.broadcast_to(u_ref[0], (t, 2 * t))
    o_ref[0, 0] = pltpu.roll(x, 0, 1, stride=1, stride_axis=0)[:, :t]


def _near_tiles(table, t):
    nh = table.shape[1]
    k = jnp.arange(2 * t, dtype=jnp.int32)
    gens = []
    for off in (0, t):
        d = jnp.where(k < t, off - k, off + 2 * t - k)
        gens.append(jnp.where(d[None] >= 0, _bias_by_distance(table, d), NEG))
    u = jnp.stack(gens, axis=1).reshape(nh * 2, 1, 2 * t)
    return pl.pallas_call(
        _toeplitz_kernel,
        grid=(nh, 2),
        in_specs=[pl.BlockSpec((1, 1, 2 * t), lambda h, o: (h * 2 + o, 0, 0))],
        out_specs=pl.BlockSpec((1, 1, t, t), lambda h, o: (h, o, 0, 0)),
        out_shape=jax.ShapeDtypeStruct((nh, 2, t, t), jnp.float32),
        compiler_params=_cparams(("arbitrary", "arbitrary")),
        name="bias_tiles",
    )(u)


def _fold_lanes(x, op):
    out = x[:, :LANES]
    for c in range(1, x.shape[1] // LANES):
        out = op(out, x[:, c * LANES:(c + 1) * LANES])
    return out


def _pairwise(n, fn):
    def body(i, carry):
        fn(2 * i)
        fn(2 * i + 1)
        return carry

    lax.fori_loop(0, n // 2, body, 0)

    @pl.when(n % 2 == 1)
    def _():
        fn(n - 1)


def _da_prompt_kernel(cfar_ref, lam_ref, q_ref, k_ref, v_ref, tile_ref, g_ref, o_ref,
                      s_ref, mpart_ref, shift_ref, lpart_ref, acc_ref, *, tq, out_scale):
    h = pl.program_id(1)
    qi = pl.program_id(2)
    q = q_ref[...]
    lane = lax.broadcasted_iota(jnp.int32, q.shape, 1)
    zero = jnp.zeros_like(q)
    q2 = jnp.concatenate([jnp.where(lane < DA_DK, q, zero), jnp.where(lane >= DA_DK, q, zero)], axis=0)
    cfar = cfar_ref[h]
    n_far = jnp.maximum(qi - 1, 0)
    r2 = 2 * tq

    def chunk_rows(kc):
        return pl.ds(pl.multiple_of(kc * tq, tq), tq)

    def scores(kc, bias):
        s = _dot_nt(q2, k_ref[chunk_rows(kc), :])
        if bias is not None:
            s = s + jnp.concatenate([bias, bias], axis=0)
        s_ref[kc] = s
        mpart_ref[...] = jnp.maximum(mpart_ref[...], _fold_lanes(s, jnp.maximum))

    mpart_ref[...] = jnp.full((r2, LANES), NEG, jnp.float32)
    _pairwise(n_far, lambda kc: scores(kc, None))
    m_far = jnp.max(mpart_ref[...], axis=-1, keepdims=True) + cfar
    mpart_ref[...] = jnp.full((r2, LANES), NEG, jnp.float32)

    @pl.when(qi >= 1)
    def _():
        scores(qi - 1, tile_ref[0, 1])

    scores(qi, tile_ref[0, 0])
    m = jnp.maximum(m_far, jnp.max(mpart_ref[...], axis=-1, keepdims=True))
    shift_ref[0] = jnp.broadcast_to(m - cfar, (r2, LANES))
    shift_ref[1] = jnp.broadcast_to(m, (r2, LANES))

    lpart_ref[...] = jnp.zeros((r2, LANES), jnp.float32)
    acc_ref[...] = jnp.zeros((r2, DA_DV), jnp.float32)

    def weights(kc, which):
        s = s_ref[kc]
        sh = shift_ref[which]
        ps = [jnp.exp(s[:, c * LANES:(c + 1) * LANES] - sh) for c in range(tq // LANES)]
        tot = ps[0]
        for pc in ps[1:]:
            tot = tot + pc
        lpart_ref[...] = lpart_ref[...] + tot
        p = jnp.concatenate(ps, axis=1).astype(BF16)
        acc_ref[...] = acc_ref[...] + _dot(p, v_ref[chunk_rows(kc), :])

    _pairwise(n_far, lambda kc: weights(kc, 0))

    @pl.when(qi >= 1)
    def _():
        weights(qi - 1, 1)

    weights(qi, 1)

    lam = lam_ref[0]
    a = acc_ref[...] / jnp.sum(lpart_ref[...], axis=-1, keepdims=True)
    o = a[:tq] - lam * a[tq:]
    o = o * lax.rsqrt(jnp.mean(o * o, axis=-1, keepdims=True) + EPS) * g_ref[...]
    o_ref[...] = (o * out_scale).astype(o_ref.dtype)


def _da_prompt(q, k, v, tiles, cfar, lam, g_subln, nb, t, tq, out_scale):
    n = q.shape[0]
    nq = t // tq
    grid_spec = pltpu.PrefetchScalarGridSpec(
        num_scalar_prefetch=0,
        grid=(nb, DA_HEADS, nq),
        in_specs=[pl.BlockSpec(memory_space=pltpu.SMEM),
                  pl.BlockSpec(memory_space=pltpu.SMEM),
                  pl.BlockSpec((tq, LANES), lambda b, h, i: (b * nq + i, h)),
                  pl.BlockSpec((t, LANES), lambda b, h, i: (b, h)),
                  pl.BlockSpec((t, LANES), lambda b, h, i: (b, h)),
                  pl.BlockSpec((1, 2, tq, tq), lambda b, h, i: (h, 0, 0, 0)),
                  pl.BlockSpec((1, DA_DV), lambda b, h, i: (0, 0))],
        out_specs=pl.BlockSpec((tq, LANES), lambda b, h, i: (b * nq + i, h)),
        scratch_shapes=[pltpu.VMEM((nq, 2 * tq, tq), jnp.float32),
                        pltpu.VMEM((2 * tq, LANES), jnp.float32),
                        pltpu.VMEM((2, 2 * tq, LANES), jnp.float32),
                        pltpu.VMEM((2 * tq, LANES), jnp.float32),
                        pltpu.VMEM((2 * tq, DA_DV), jnp.float32)])
    return pl.pallas_call(
        functools.partial(_da_prompt_kernel, tq=tq, out_scale=out_scale),
        grid_spec=grid_spec,
        out_shape=jax.ShapeDtypeStruct((n, DA_HEADS * DA_DV), BF16),
        compiler_params=_cparams(("arbitrary", "arbitrary", "arbitrary")),
        name="da_prompt",
    )(cfar, lam, q, k, v, tiles, g_subln.reshape(1, DA_DV))


def _sa_prompt_kernel(cfar_ref, qs_ref, qx_ref, mq_ref, mk_ref, ks_ref, vs_ref, tile_ref, o_ref,
                      k2_ref, key_ref, hi_ref, lo_ref, cut_ref, s_ref, mpart_ref, shift_ref, lpart_ref,
                      acc_ref, *, tq, topk):
    qi = pl.program_id(1)
    n_chunks = qi + 1
    t = mk_ref.shape[0]

    @pl.when(qi == 0)
    def _():
        kix = mk_ref[:, :LANES].astype(jnp.float32)
        lane = lax.broadcasted_iota(jnp.int32, kix.shape, 1)
        k2_ref[0] = jnp.where(lane < IDX_DK, kix, 0.0).astype(BF16)
        k2_ref[1] = jnp.where(lane >= IDX_DK, pltpu.roll(kix, IDX_DK, axis=1), 0.0).astype(BF16)

    wix = mq_ref[:, IDX_DK:IDX_DK + IDX_HEADS]
    wcols = [wix[:, hh:hh + 1] for hh in range(IDX_HEADS)]
    row = lax.broadcasted_iota(jnp.int32, (tq, tq), 0)
    col = lax.broadcasted_iota(jnp.int32, (tq, tq), 1)

    def score_body(kc, carry):
        rows = pl.ds(pl.multiple_of(kc * tq, tq), tq)
        ke = k2_ref[0, rows, :]
        ko = k2_ref[1, rows, :]
        sc = jnp.zeros((tq, tq), jnp.float32)
        for p in range(IDX_HEADS // 2):
            qp = qx_ref[:, p * LANES:(p + 1) * LANES]
            sc = sc + wcols[2 * p] * jnp.maximum(_dot_nt(qp, ke), 0.0)
            sc = sc + wcols[2 * p + 1] * jnp.maximum(_dot_nt(qp, ko), 0.0)
        sc = sc * (IDX_DK ** -0.5 * IDX_HEADS ** -0.5)
        sc = jnp.where((kc < qi) | (row >= col), sc, -jnp.inf)
        key = _float_key(sc)
        key_ref[kc] = key
        hi_ref[kc] = (key >> 16).astype(jnp.int16)
        return carry

    lax.fori_loop(0, n_chunks, score_body, 0)

    i16 = jnp.int16
    i16_min = -(2 ** 15)
    one_i = jnp.ones((tq, tq), i16)
    zero_i = jnp.zeros((tq, tq), i16)
    ones_col = jnp.ones((tq, LANES), BF16)

    def wide16(x):
        return jnp.concatenate([x] * (tq // LANES), axis=1).astype(i16)

    def count_ge16(ref16, cand):
        c16 = wide16(cand)

        def body(kc, acc):
            return acc + jnp.where(ref16[kc] >= c16, one_i, zero_i)
        acc = lax.fori_loop(0, n_chunks, body, zero_i)
        return _dot(acc.astype(jnp.float32).astype(BF16), ones_col)

    def search16(ref16, base, need):
        def bit_body(it, carry):
            v, cnt_v = carry
            cand = v + (jnp.int32(1) << (15 - it))
            cnt = base + count_ge16(ref16, cand)
            ok = cnt >= need
            return jnp.where(ok, cand, v), jnp.where(ok, cnt, cnt_v)
        v0 = jnp.full((tq, LANES), i16_min, jnp.int32)
        c0 = jnp.full((tq, LANES), 3.0e38, jnp.float32)
        return lax.fori_loop(0, 16, bit_body, (v0, c0))

    zero_cnt = jnp.zeros((tq, LANES), jnp.float32)
    t_hi, _ = search16(hi_ref, zero_cnt, float(topk))
    n_above = jnp.where(t_hi < 2 ** 15 - 1, count_ge16(hi_ref, jnp.minimum(t_hi + 1, 2 ** 15 - 1)), 0.0)
    t_hi16 = wide16(t_hi)

    def lower_body(kc, carry):
        lo = ((key_ref[kc] & jnp.int32(0xFFFF)) - 2 ** 15).astype(i16)
        lo_ref[kc] = jnp.where(hi_ref[kc] == t_hi16, lo, jnp.full((tq, tq), i16_min, i16))
        return carry

    lax.fori_loop(0, n_chunks, lower_body, 0)
    t_lo, cnt_w = search16(lo_ref, n_above, float(topk))
    thr = ((t_hi << 16) + (t_lo + 2 ** 15))[:, :1]
    cnt_thr = cnt_w[:, :1]
    tied = (cnt_thr > float(topk)) & (thr > jnp.int32(KEY_NEG_INF))
    need_tie = jnp.max(jnp.where(tied, 1.0, 0.0)) > 0.0
    thr = jnp.maximum(thr, jnp.int32(KEY_NEG_INF + 1))

    cut_ref[...] = jnp.full((tq, 1), 2 ** 30, jnp.int32)

    @pl.when(need_tie)
    def _():
        def gt_body(kc, acc):
            g = jnp.where(key_ref[kc] > thr, 1.0, 0.0)
            return acc + jnp.sum(g, axis=-1, keepdims=True)
        n_gt = lax.fori_loop(0, n_chunks, gt_body, jnp.zeros((tq, 1), jnp.float32))
        need = float(topk) - n_gt
        n_bits = max(1, int(t).bit_length())

        def cut_body(it, cut):
            cand = cut + (jnp.int32(1) << (n_bits - 1 - it))

            def eq_body(kc, acc):
                pos = kc * tq + col
                e = jnp.where((key_ref[kc] == thr) & (pos < cand), 1.0, 0.0)
                return acc + jnp.sum(e, axis=-1, keepdims=True)
            n_eq = lax.fori_loop(0, n_chunks, eq_body, jnp.zeros((tq, 1), jnp.float32))
            return jnp.where(n_eq <= need, cand, cut)
        cut_ref[...] = lax.fori_loop(0, n_bits, cut_body, jnp.zeros((tq, 1), jnp.int32))

    cut = cut_ref[...]

    n_far = jnp.maximum(qi - 1, 0)
    rg = SA_GROUP * tq

    def chunk_rows(kc):
        return pl.ds(pl.multiple_of(kc * tq, tq), tq)

    for g in range(SA_KV_HEADS):
        heads = [g * SA_GROUP + j for j in range(SA_GROUP)]
        qg = jnp.concatenate([qs_ref[:, hh * LANES:(hh + 1) * LANES] for hh in heads], axis=0)
        cf_rows = jnp.concatenate([jnp.full((tq, 1), cfar_ref[hh], jnp.float32) for hh in heads], axis=0)

        def scores(kc, kind, qg=qg, heads=heads, g=g):
            key = key_ref[kc]
            sel = (key > thr) | ((key == thr) & (kc * tq + col < cut))
            s_all = _dot_nt(qg, ks_ref[chunk_rows(kc), g * SA_DH:(g + 1) * SA_DH])
            parts = []
            for j, hh in enumerate(heads):
                s = s_all[j * tq:(j + 1) * tq]
                if kind is not None:
                    s = s + tile_ref[hh, kind]
                parts.append(jnp.where(sel, s, NEG))
            s = jnp.concatenate(parts, axis=0)
            s_ref[kc] = s
            mpart_ref[...] = jnp.maximum(mpart_ref[...], _fold_lanes(s, jnp.maximum))

        mpart_ref[...] = jnp.full((rg, LANES), NEG, jnp.float32)
        _pairwise(n_far, lambda kc, f=scores: f(kc, None))
        m_far = jnp.max(mpart_ref[...], axis=-1, keepdims=True) + cf_rows
        mpart_ref[...] = jnp.full((rg, LANES), NEG, jnp.float32)

        @pl.when(qi >= 1)
        def _(f=scores):
            f(qi - 1, 1)

        scores(qi, 0)
        m = jnp.maximum(m_far, jnp.max(mpart_ref[...], axis=-1, keepdims=True))
        shift_ref[0] = jnp.broadcast_to(m - cf_rows, (rg, LANES))
        shift_ref[1] = jnp.broadcast_to(m, (rg, LANES))
        lpart_ref[...] = jnp.zeros((rg, LANES), jnp.float32)
        acc_ref[...] = jnp.zeros((rg, SA_DH), jnp.float32)

        def weights(kc, which, g=g):
            s = s_ref[kc]
            sh = shift_ref[which]
            ps = [jnp.exp(s[:, c * LANES:(c + 1) * LANES] - sh) for c in range(tq // LANES)]
            tot = ps[0]
            for pc in ps[1:]:
                tot = tot + pc
            lpart_ref[...] = lpart_ref[...] + tot
            p = jnp.concatenate(ps, axis=1).astype(BF16)
            acc_ref[...] = acc_ref[...] + _dot(p, vs_ref[chunk_rows(kc), g * SA_DH:(g + 1) * SA_DH])

        _pairwise(n_far, lambda kc, f=weights: f(kc, 0))

        @pl.when(qi >= 1)
        def _(f=weights):
            f(qi - 1, 1)

        weights(qi, 1)
        a = acc_ref[...] / jnp.sum(lpart_ref[...], axis=-1, keepdims=True)
        for j, hh in enumerate(heads):
            o_ref[:, hh * SA_DH:(hh + 1) * SA_DH] = a[j * tq:(j + 1) * tq].astype(o_ref.dtype)


def _sa_prompt(q_sa, q_ix, misc_q, misc_k, k_sa, v_sa, tiles, cfar, nb, t, tq, topk):
    n = q_sa.shape[0]
    nq = t // tq
    mw = misc_q.shape[1]
    kvw = SA_KV_HEADS * SA_DH
    once = dict(pipeline_mode=pl.Buffered(1))
    rg = SA_GROUP * tq
    return pl.pallas_call(
        functools.partial(_sa_prompt_kernel, tq=tq, topk=topk),
        grid=(nb, nq),
        in_specs=[pl.BlockSpec(memory_space=pltpu.SMEM),
                  pl.BlockSpec((tq, SA_HEADS * SA_DH), lambda b, i: (b * nq + i, 0)),
                  pl.BlockSpec((tq, IDX_HEADS * IDX_DK), lambda b, i: (b * nq + i, 0)),
                  pl.BlockSpec((tq, mw), lambda b, i: (b * nq + i, 0)),
                  pl.BlockSpec((t, mw), lambda b, i: (b, 0), **once),
                  pl.BlockSpec((t, kvw), lambda b, i: (b, 0), **once),
                  pl.BlockSpec((t, kvw), lambda b, i: (b, 0), **once),
                  pl.BlockSpec((SA_HEADS, 2, tq, tq), lambda b, i: (0, 0, 0, 0), **once)],
        out_specs=pl.BlockSpec((tq, SA_HEADS * SA_DH), lambda b, i: (b * nq + i, 0)),
        out_shape=jax.ShapeDtypeStruct((n, SA_HEADS * SA_DH), BF16),
        scratch_shapes=[pltpu.VMEM((2, t, LANES), BF16),
                        pltpu.VMEM((nq, tq, tq), jnp.int32),
                        pltpu.VMEM((nq, tq, tq), jnp.int16),
                        pltpu.VMEM((nq, tq, tq), jnp.int16),
                        pltpu.VMEM((tq, 1), jnp.int32),
                        pltpu.VMEM((nq, rg, tq), jnp.float32),
                        pltpu.VMEM((rg, LANES), jnp.float32),
                        pltpu.VMEM((2, rg, LANES), jnp.float32),
                        pltpu.VMEM((rg, LANES), jnp.float32),
                        pltpu.VMEM((rg, SA_DH), jnp.float32)],
        compiler_params=_cparams(("arbitrary", "arbitrary")),
        name="sa_prompt",
    )(cfar, q_sa, q_ix, misc_q, misc_k, k_sa, v_sa, tiles)


def _proj_gate_kernel(oda_ref, osa_ref, wpd_ref, wps_ref, gda_ref, gsa_ref, o_ref, wbf_ref):
    @pl.when(pl.program_id(1) == 0)
    def _():
        wbf_ref[0] = wpd_ref[...].astype(BF16)
        wbf_ref[1] = wps_ref[...].astype(BF16)

    a = _dot(oda_ref[...], wbf_ref[0])
    b = _dot(osa_ref[...], wbf_ref[1])
    o_ref[...] = (gda_ref[...].astype(jnp.float32) * a
                  + gsa_ref[...].astype(jnp.float32) * b).astype(o_ref.dtype)


def _proj_gate(o_da, o_sa, w_pd, w_ps, gates, d, tm=512, tn=512):
    n, kd = o_da.shape
    ks = o_sa.shape[1]
    tn = min(tn, d)
    tm = min(tm, n)
    nj = d // tn
    return pl.pallas_call(
        _proj_gate_kernel,
        grid=(nj, n // tm),
        in_specs=[pl.BlockSpec((tm, kd), lambda j, i: (i, 0)),
                  pl.BlockSpec((tm, ks), lambda j, i: (i, 0)),
                  pl.BlockSpec((kd, tn), lambda j, i: (0, j)),
                  pl.BlockSpec((ks, tn), lambda j, i: (0, j)),
                  pl.BlockSpec((tm, tn), lambda j, i: (i, j)),
                  pl.BlockSpec((tm, tn), lambda j, i: (i, nj + j))],
        out_specs=pl.BlockSpec((tm, tn), lambda j, i: (i, j)),
        out_shape=jax.ShapeDtypeStruct((n, d), BF16),
        scratch_shapes=[pltpu.VMEM((2, kd, tn), BF16)],
        compiler_params=_cparams(("arbitrary", "arbitrary")),
        name="proj_gate",
    )(o_da, o_sa, w_pd, w_ps, gates, gates)


def _mm_resid_kernel(m_ref, w_ref, x_ref, gt_ref, o_ref, wbf_ref):
    @pl.when(pl.program_id(1) == 0)
    def _():
        wbf_ref[...] = w_ref[...].astype(BF16)

    o_ref[...] = x_ref[...] + gt_ref[0] * _dot(m_ref[...], wbf_ref[...])


def _mm_resid(m, w, x, gt, rows_per_batch, tm=512, tn=512):
    n, k = m.shape
    d = w.shape[1]
    tn = min(tn, d)
    tm = min(tm, n)
    per = rows_per_batch // tm
    return pl.pallas_call(
        _mm_resid_kernel,
        grid=(d // tn, n // tm),
        in_specs=[pl.BlockSpec((tm, k), lambda j, i: (i, 0)),
                  pl.BlockSpec((k, tn), lambda j, i: (0, j)),
                  pl.BlockSpec((tm, tn), lambda j, i: (i, j)),
                  pl.BlockSpec((1, 1, tn), lambda j, i: (i // per, 0, j))],
        out_specs=pl.BlockSpec((tm, tn), lambda j, i: (i, j)),
        out_shape=jax.ShapeDtypeStruct((n, d), jnp.float32),
        scratch_shapes=[pltpu.VMEM((k, tn), BF16)],
        compiler_params=_cparams(("arbitrary", "arbitrary")),
        name="mm_resid",
    )(m, w, x, gt[:, None, :])


def _norm_router_kernel(x_ref, g_ref, sc_ref, sh_ref, wr_ref, th_ref, tl_ref, h_ref, lg_ref):
    last = pl.num_programs(0) - 1

    @pl.when(pl.program_id(0) < last)
    def _():
        x = x_ref[...]
        y = x * lax.rsqrt(jnp.mean(x * x, axis=-1, keepdims=True) + EPS) * g_ref[...]
        h = y * (1.0 + sc_ref[0]) + sh_ref[0]
        h_ref[...] = h.astype(h_ref.dtype)
        lg_ref[...] = _dot_nt(_bf(wr_ref[...]), _bf(h))

    @pl.when(pl.program_id(0) == last)
    def _():
        h_ref[...] = th_ref[...]
        lg_ref[...] = tl_ref[...]


def _norm_router(x, g, sc, sh, w_router_t, tail_h, tail_lg, rows_per_batch, tm):
    n, d = x.shape
    per = rows_per_batch // tm
    nt = n // tm
    assert tail_h.shape == (tm, d) and tail_lg.shape == (N_EXPERTS, tm)
    row = lambda i: jnp.minimum(i, nt - 1)
    return pl.pallas_call(
        _norm_router_kernel,
        grid=(nt + 1,),
        in_specs=[pl.BlockSpec((tm, d), lambda i: (row(i), 0)),
                  pl.BlockSpec((1, d), lambda i: (0, 0)),
                  pl.BlockSpec((1, 1, d), lambda i: (row(i) // per, 0, 0)),
                  pl.BlockSpec((1, 1, d), lambda i: (row(i) // per, 0, 0)),
                  pl.BlockSpec((N_EXPERTS, d), lambda i: (0, 0)),
                  pl.BlockSpec((tm, d), lambda i: (0, 0)),
                  pl.BlockSpec((N_EXPERTS, tm), lambda i: (0, 0))],
        out_specs=[pl.BlockSpec((tm, d), lambda i: (i, 0)),
                   pl.BlockSpec((N_EXPERTS, tm), lambda i: (0, i))],
        out_shape=[jax.ShapeDtypeStruct((n + tm, d), BF16),
                   jax.ShapeDtypeStruct((N_EXPERTS, n + tm), jnp.float32)],
        compiler_params=_cparams(("arbitrary",)),
        name="norm_router",
    )(x, g.reshape(1, d), sc[:, None, :], sh[:, None, :], w_router_t, tail_h, tail_lg)


def _route_kernel(lg_ref, b_ref, eidx_ref, wts_ref, rank_ref, cnt_ref, *, tn):
    shape = (N_GROUPS, GROUP_SIZE, tn)
    sc = _sigmoid(lg_ref[...])
    biased = sc + b_ref[...]
    e_iota = lax.broadcasted_iota(jnp.int32, shape, 1)
    g_iota3 = lax.broadcasted_iota(jnp.int32, shape, 0)
    flat_iota = g_iota3 * GROUP_SIZE + e_iota
    g_iota = lax.broadcasted_iota(jnp.int32, (N_GROUPS, 1, tn), 0)
    ninf = -jnp.inf

    m1 = jnp.max(biased, axis=1, keepdims=True)
    first = jnp.min(jnp.where(biased == m1, e_iota, GROUP_SIZE), axis=1, keepdims=True)
    m2 = jnp.max(jnp.where(e_iota == first, ninf, biased), axis=1, keepdims=True)
    cur = m1 + m2
    gsel = jnp.zeros((N_GROUPS, 1, tn), jnp.float32)
    for _ in range(TOPK_GROUPS):
        mx = jnp.max(cur, axis=0, keepdims=True)
        idx = jnp.min(jnp.where(cur == mx, g_iota, N_GROUPS), axis=0, keepdims=True)
        hit = g_iota == idx
        gsel = jnp.where(hit, 1.0, gsel)
        cur = jnp.where(hit, ninf, cur)

    cur = jnp.where(gsel > 0.0, biased, ninf)
    hits, ws = [], []
    for k in range(TOP_K):
        mx = jnp.max(jnp.max(cur, axis=1, keepdims=True), axis=0, keepdims=True)
        cand = jnp.where(cur == mx, flat_iota, N_EXPERTS)
        idx = jnp.min(jnp.min(cand, axis=1, keepdims=True), axis=0, keepdims=True)
        hit = flat_iota == idx
        w = jnp.sum(jnp.sum(jnp.where(hit, sc, 0.0), axis=1, keepdims=True), axis=0, keepdims=True)
        eidx_ref[k:k + 1, :] = idx.reshape(1, tn)
        hits.append(hit)
        ws.append(w)
        cur = jnp.where(hit, ninf, cur)
    wsum = ws[0]
    for w in ws[1:]:
        wsum = wsum + w
    for k in range(TOP_K):
        wts_ref[k:k + 1, :] = (ws[k] / wsum * ROUTED_SCALE).reshape(1, tn)

    member = jnp.zeros(shape, jnp.float32)
    for hit in hits:
        member = jnp.where(hit, 1.0, member)
    member2 = member.reshape(N_EXPERTS, tn)
    r = lax.broadcasted_iota(jnp.int32, (tn, tn), 0)
    c = lax.broadcasted_iota(jnp.int32, (tn, tn), 1)
    upper = jnp.where(r < c, 1.0, 0.0).astype(BF16)
    prefix = _dot(member2.astype(BF16), upper).reshape(shape)
    for k in range(TOP_K):
        rk = jnp.sum(jnp.sum(jnp.where(hits[k], prefix, 0.0), axis=1, keepdims=True), axis=0, keepdims=True)
        rank_ref[k:k + 1, :] = rk.reshape(1, tn)
    cnt = jnp.sum(member2, axis=1, keepdims=True)
    cnt_ref[0] = jnp.broadcast_to(cnt, (N_EXPERTS, LANES))


def _route(logits_t, b_router, tn):
    n_pad = logits_t.shape[1]
    nt = n_pad // tn
    lg3 = logits_t.reshape(N_GROUPS, GROUP_SIZE, n_pad)
    b3 = b_router.astype(jnp.float32).reshape(N_GROUPS, GROUP_SIZE, 1)
    row = lambda dt: jax.ShapeDtypeStruct((TOP_K, n_pad), dt)
    return pl.pallas_call(
        functools.partial(_route_kernel, tn=tn),
        grid=(nt,),
        in_specs=[pl.BlockSpec((N_GROUPS, GROUP_SIZE, tn), lambda i: (0, 0, i)),
                  pl.BlockSpec((N_GROUPS, GROUP_SIZE, 1), lambda i: (0, 0, 0))],
        out_specs=[pl.BlockSpec((TOP_K, tn), lambda i: (0, i)),
                   pl.BlockSpec((TOP_K, tn), lambda i: (0, i)),
                   pl.BlockSpec((TOP_K, tn), lambda i: (0, i)),
                   pl.BlockSpec((1, N_EXPERTS, LANES), lambda i: (i, 0, 0))],
        out_shape=[row(jnp.int32), row(jnp.float32), row(jnp.float32),
                   jax.ShapeDtypeStruct((nt, N_EXPERTS, LANES), jnp.float32)],
        compiler_params=_cparams(("arbitrary",)),
        name="route",
    )(lg3, b3)


def _dest_kernel(eidx_ref, rank_ref, base_ref, o_ref, *, tn):
    e_iota = lax.broadcasted_iota(jnp.int32, (N_EXPERTS, tn), 0)
    base = base_ref[0][:, :1]
    for k in range(TOP_K):
        onehot = e_iota == eidx_ref[k:k + 1, :]
        b = jnp.sum(jnp.where(onehot, base, 0.0), axis=0, keepdims=True)
        o_ref[k:k + 1, :] = (b + rank_ref[k:k + 1, :]).astype(jnp.int32)


def _dest(eidx, rank, base, tn):
    n_pad = eidx.shape[1]
    return pl.pallas_call(
        functools.partial(_dest_kernel, tn=tn),
        grid=(n_pad // tn,),
        in_specs=[pl.BlockSpec((TOP_K, tn), lambda i: (0, i)),
                  pl.BlockSpec((TOP_K, tn), lambda i: (0, i)),
                  pl.BlockSpec((1, N_EXPERTS, LANES), lambda i: (i, 0, 0))],
        out_specs=pl.BlockSpec((TOP_K, tn), lambda i: (0, i)),
        out_shape=jax.ShapeDtypeStruct((TOP_K, n_pad), jnp.int32),
        compiler_params=_cparams(("arbitrary",)),
        name="dest",
    )(eidx, rank, base)


def _pack_words(lo_f32, hi_f32):
    lo = lax.shift_right_logical(pltpu.bitcast(lo_f32, jnp.uint32), jnp.uint32(16))
    hi = pltpu.bitcast(hi_f32, jnp.uint32) & jnp.uint32(0xFFFF0000)
    return hi | lo


def _unpack_words(w):
    lo = pltpu.bitcast(lax.shift_left(w, jnp.uint32(16)), jnp.float32)
    hi = pltpu.bitcast(w & jnp.uint32(0xFFFF0000), jnp.float32)
    return lo, hi


def _bf16_exact(x):
    return x.astype(BF16).astype(jnp.float32)


def _dispatch_kernel(pstart_ref, pcnt_ref, h_ref, dest_hbm, xs_hbm, dsm, pk, zrow, sem_d, sem_r,
                     *, tn, nt):
    i = pl.program_id(0)
    half = pk.shape[1]

    @pl.when(i < nt)
    def _():
        cp = pltpu.make_async_copy(dest_hbm.at[pl.ds(i * (TOP_K * tn), TOP_K * tn)], dsm, sem_d)
        cp.start()
        x = h_ref[...]
        pk[...] = _pack_words(x[:, :half].astype(jnp.float32), x[:, half:].astype(jnp.float32))
        cp.wait()

        def body(r, carry):
            for k in range(TOP_K):
                d = dsm[k * tn + r]
                pltpu.make_async_copy(pk.at[pl.ds(r, 1), :], xs_hbm.at[pl.ds(d, 1), :],
                                      sem_r).start(priority=k % 2)
            return carry

        lax.fori_loop(0, tn, body, 0)
        for k in range(TOP_K):
            pltpu.make_async_copy(pk, xs_hbm.at[pl.ds(0, tn), :], sem_r).wait()

    @pl.when(i == nt)
    def _():
        zrow[...] = jnp.zeros(zrow.shape, zrow.dtype)

        def per_expert(e, carry):
            s0 = pstart_ref[e]
            c = pcnt_ref[e]
            head = jnp.minimum(c, (8 - s0 % 8) % 8)
            g0 = s0 + head
            ngrp = (c - head) // 8
            grp = lambda r: pl.ds(pl.multiple_of(g0 + r * 8, 8), 8)

            def start(r, cc):
                pltpu.make_async_copy(zrow.at[pl.ds(0, 1), :], xs_hbm.at[pl.ds(s0 + r, 1), :], sem_r).start()
                return cc

            def wait(r, cc):
                pltpu.make_async_copy(zrow.at[pl.ds(0, 1), :], xs_hbm.at[pl.ds(s0, 1), :], sem_r).wait()
                return cc

            def gstart(r, cc):
                pltpu.make_async_copy(zrow, xs_hbm.at[grp(r), :], sem_r).start()
                return cc

            def gwait(r, cc):
                pltpu.make_async_copy(zrow, xs_hbm.at[grp(0), :], sem_r).wait()
                return cc

            lax.fori_loop(0, head, start, 0)
            lax.fori_loop(0, ngrp, gstart, 0)
            lax.fori_loop(0, head, wait, 0)
            lax.fori_loop(0, ngrp, gwait, 0)
            return carry

        lax.fori_loop(0, N_EXPERTS, per_expert, 0)

        t0 = pstart_ref[N_EXPERTS]
        groups = pcnt_ref[N_EXPERTS]
        rows8 = lambda r: pl.ds(pl.multiple_of(t0 + r * 8, 8), 8)

        def tstart(r, cc):
            pltpu.make_async_copy(zrow, xs_hbm.at[rows8(r), :], sem_r).start()
            return cc

        def twait(r, cc):
            pltpu.make_async_copy(zrow, xs_hbm.at[rows8(0), :], sem_r).wait()
            return cc

        lax.fori_loop(0, groups, tstart, 0)
        lax.fori_loop(0, groups, twait, 0)


def _dispatch(h2, dest_flat, pad_start, pad_cnt, n_rows, tn):
    n_pad, d = h2.shape
    nt = n_pad // tn
    grid_spec = pltpu.PrefetchScalarGridSpec(
        num_scalar_prefetch=2,
        grid=(nt + 1,),
        in_specs=[pl.BlockSpec((tn, d), lambda i, a, b: (jnp.minimum(i, nt - 1), 0)),
                  pl.BlockSpec(memory_space=pl.ANY)],
        out_specs=pl.BlockSpec(memory_space=pl.ANY),
        scratch_shapes=[pltpu.SMEM((TOP_K * tn,), jnp.int32),
                        pltpu.VMEM((tn, d // 2), jnp.uint32),
                        pltpu.VMEM((8, d // 2), jnp.uint32),
                        pltpu.SemaphoreType.DMA(()),
                        pltpu.SemaphoreType.DMA(())])
    return pl.pallas_call(
        functools.partial(_dispatch_kernel, tn=tn, nt=nt),
        grid_spec=grid_spec,
        out_shape=jax.ShapeDtypeStruct((n_rows, d // 2), jnp.uint32),
        compiler_params=_cparams(("arbitrary",)),
        name="dispatch",
    )(pad_start, pad_cnt, h2, dest_flat)


def _expert_kernel(blk_e_ref, nused_ref, next_e_ref, slot_ref, x_ref, wg_hbm, wu_hbm, wd_hbm, y_ref,
                   wg_f, wu_f, wd_f, wgb, wub, wdb, sems):
    i = pl.program_id(0)
    nused = nused_ref[0]
    ii = jnp.minimum(i, nused - 1)
    e = blk_e_ref[ii]
    e_prev = blk_e_ref[jnp.maximum(ii - 1, 0)]
    half = x_ref.shape[1]

    def copies(ex, sl):
        return [pltpu.make_async_copy(src.at[ex], dst.at[sl], sems.at[sl, t])
                for t, (src, dst) in enumerate(((wg_hbm, wg_f), (wu_hbm, wu_f), (wd_hbm, wd_f)))]

    @pl.when(i == 0)
    def _():
        for c in copies(e, slot_ref[e]):
            c.start()

    @pl.when((i < nused) & ((i == 0) | (e != e_prev)))
    def _():
        sl = slot_ref[e]
        for c in copies(e, sl):
            c.wait()
        nxt = next_e_ref[e]

        @pl.when(nxt >= 0)
        def _():
            for c in copies(nxt, 1 - sl):
                c.start()

        wgb[...] = wg_f[sl].astype(BF16)
        wub[...] = wu_f[sl].astype(BF16)
        wdb[...] = wd_f[sl].astype(BF16)

    @pl.when(i < nused)
    def _():
        lo, hi = _unpack_words(x_ref[...])
        xl = lo.astype(BF16)
        xh = hi.astype(BF16)
        g = _dot(xl, wgb[:half, :]) + _dot(xh, wgb[half:, :])
        u = _dot(xl, wub[:half, :]) + _dot(xh, wub[half:, :])
        hmid = (_silu(g) * u).astype(BF16)
        y = _dot(hmid, wdb[...])
        y_ref[...] = _pack_words(_bf16_exact(y[:, :half]), _bf16_exact(y[:, half:]))

    @pl.when(i >= nused)
    def _():
        y_ref[...] = jnp.zeros(y_ref.shape, y_ref.dtype)


def _experts(xs, blk_e, nused, next_e, slot, w_gate, w_up, w_down, tb):
    n_rows, half = xs.shape
    _, d, f = w_gate.shape
    nblk = n_rows // tb

    def xmap(i, be, nu, ne, sl):
        return (jnp.minimum(i, nu[0] - 1), 0)

    grid_spec = pltpu.PrefetchScalarGridSpec(
        num_scalar_prefetch=4,
        grid=(nblk,),
        in_specs=[pl.BlockSpec((tb, half), xmap),
                  pl.BlockSpec(memory_space=pl.ANY),
                  pl.BlockSpec(memory_space=pl.ANY),
                  pl.BlockSpec(memory_space=pl.ANY)],
        out_specs=pl.BlockSpec((tb, half), lambda i, be, nu, ne, sl: (i, 0)),
        scratch_shapes=[pltpu.VMEM((2, d, f), jnp.float32),
                        pltpu.VMEM((2, d, f), jnp.float32),
                        pltpu.VMEM((2, f, d), jnp.float32),
                        pltpu.VMEM((d, f), BF16),
                        pltpu.VMEM((d, f), BF16),
                        pltpu.VMEM((f, d), BF16),
                        pltpu.SemaphoreType.DMA((2, 3))])
    return pl.pallas_call(
        _expert_kernel,
        grid_spec=grid_spec,
        out_shape=jax.ShapeDtypeStruct((n_rows, half), jnp.uint32),
        compiler_params=_cparams(("arbitrary",)),
        name="experts",
    )(blk_e, nused, next_e, slot, xs, w_gate, w_up, w_down)


def _ffn_kernel(h_ref, wg_ref, wu_ref, wd_ref, o_ref, wgb, wub, wdb):
    @pl.when(pl.program_id(0) == 0)
    def _():
        wgb[...] = wg_ref[...].astype(BF16)
        wub[...] = wu_ref[...].astype(BF16)
        wdb[...] = wd_ref[...].astype(BF16)

    x = h_ref[...]
    hmid = (_silu(_dot(x, wgb[...])) * _dot(x, wub[...])).astype(BF16)
    o_ref[...] = _dot(hmid, wdb[...]).astype(o_ref.dtype)


def _ffn_shared(h2, wg, wu, wd, tm):
    n_pad, d = h2.shape
    f = wg.shape[1]
    return pl.pallas_call(
        _ffn_kernel,
        grid=(n_pad // tm,),
        in_specs=[pl.BlockSpec((tm, d), lambda i: (i, 0)),
                  pl.BlockSpec((d, f), lambda i: (0, 0)),
                  pl.BlockSpec((d, f), lambda i: (0, 0)),
                  pl.BlockSpec((f, d), lambda i: (0, 0))],
        out_specs=pl.BlockSpec((tm, d), lambda i: (i, 0)),
        out_shape=jax.ShapeDtypeStruct((n_pad, d), BF16),
        scratch_shapes=[pltpu.VMEM((d, f), BF16),
                        pltpu.VMEM((d, f), BF16),
                        pltpu.VMEM((f, d), BF16)],
        compiler_params=_cparams(("arbitrary",)),
        name="ffn_shared",
    )(h2, wg, wu, wd)


def _combine_kernel(dest_hbm, ys_hbm, wts_ref, sh_ref, x_ref, gt_ref, g_ref, o_ref,
                    dsm, buf, sem_d, sem_r, *, tn, tile0, final_norm):
    i = pl.program_id(0)
    cp = pltpu.make_async_copy(dest_hbm.at[pl.ds((tile0 + i) * (TOP_K * tn), TOP_K * tn)], dsm, sem_d)
    cp.start()
    cp.wait()

    def body(r, carry):
        for k in range(TOP_K):
            d = dsm[k * tn + r]
            pltpu.make_async_copy(ys_hbm.at[pl.ds(d, 1), :], buf.at[k, pl.ds(r, 1), :],
                                  sem_r).start(priority=k % 2)
        return carry

    lax.fori_loop(0, tn, body, 0)
    for k in range(TOP_K):
        pltpu.make_async_copy(ys_hbm.at[pl.ds(0, tn), :], buf.at[k], sem_r).wait()

    half = buf.shape[2]
    wts = wts_ref[...]
    acc_lo = jnp.zeros((tn, half), jnp.float32)
    acc_hi = jnp.zeros((tn, half), jnp.float32)
    for k in range(TOP_K):
        lo, hi = _unpack_words(buf[k])
        wk = wts[:, k:k + 1]
        acc_lo = acc_lo + wk * lo
        acc_hi = acc_hi + wk * hi
    sh = sh_ref[...].astype(jnp.float32)
    gt = gt_ref[0]
    x_lo = x_ref[:, :half] + gt[:, :half] * (acc_lo + sh[:, :half])
    x_hi = x_ref[:, half:] + gt[:, half:] * (acc_hi + sh[:, half:])
    if final_norm:
        ms = (jnp.sum(x_lo * x_lo, axis=-1, keepdims=True)
              + jnp.sum(x_hi * x_hi, axis=-1, keepdims=True)) / (2 * half)
        inv = lax.rsqrt(ms + EPS)
        g = g_ref[...]
        x_lo = x_lo * inv * g[:, :half]
        x_hi = x_hi * inv * g[:, half:]
    o_ref[:, :half] = x_lo
    o_ref[:, half:] = x_hi


def _combine(dest_flat, ys, wts_tok, shared, x1, gt3, g_final, tile0, tn, final_norm):
    rows, d = x1.shape
    nt = rows // tn
    gr = gt3.shape[1]
    per = nt // gt3.shape[0]
    return pl.pallas_call(
        functools.partial(_combine_kernel, tn=tn, tile0=tile0, final_norm=final_norm),
        grid=(nt,),
        in_specs=[pl.BlockSpec(memory_space=pl.ANY),
                  pl.BlockSpec(memory_space=pl.ANY),
                  pl.BlockSpec((tn, TOP_K), lambda i: (tile0 + i, 0)),
                  pl.BlockSpec((tn, d), lambda i: (tile0 + i, 0)),
                  pl.BlockSpec((tn, d), lambda i: (i, 0)),
                  pl.BlockSpec((1, gr, d), lambda i: (i // per, 0, 0)),
                  pl.BlockSpec((1, d), lambda i: (0, 0))],
        out_specs=pl.BlockSpec((tn, d), lambda i: (i, 0)),
        out_shape=jax.ShapeDtypeStruct((rows, d), jnp.float32),
        scratch_shapes=[pltpu.SMEM((TOP_K * tn,), jnp.int32),
                        pltpu.VMEM((TOP_K, tn, d // 2), jnp.uint32),
                        pltpu.SemaphoreType.DMA(()),
                        pltpu.SemaphoreType.DMA(())],
        compiler_params=_cparams(("arbitrary",)),
        name="combine",
    )(dest_flat, ys, wts_tok, shared, x1, gt3, g_final.reshape(1, d))


def _page_specs(shape, n, pg):
    def mk(u):
        return pl.BlockSpec((1,) + shape, lambda b, j, pt: (pt[b, j * pg + u], 0, 0))
    return [mk(u) for u in range(n)]


def _kv_page_specs(kshape, vshape, pg, ns):
    def mk(shape, first):
        def one(u):
            def index(b, j, pt):
                step = jnp.minimum(j, ns - 1) if first else jnp.maximum(j - ns, 0)
                return (pt[b, step * pg + u], 0, 0)
            return pl.BlockSpec((1,) + shape, index)
        return [one(u) for u in range(pg)]
    return mk(kshape, True) + mk(vshape, False)


def _softmax_pages(s_ref):
    s = s_ref[...]
    m = jnp.max(jnp.max(s, axis=0, keepdims=True), axis=2, keepdims=True)
    e = jnp.exp(s - m)
    return e / jnp.sum(jnp.sum(e, axis=0, keepdims=True), axis=2, keepdims=True)


def _da_sample_kernel(pt_ref, lam_ref, q_ref, kn_ref, vn_ref, bl_ref, cf_ref, b0_ref, ex_ref, hm_ref,
                      *rest, pg, n_pages):
    k_refs, v_refs = rest[:pg], rest[pg:2 * pg]
    o_ref, s_ref, a_ref, acc_ref = rest[2 * pg:]
    j = pl.program_id(1)
    ns = n_pages // pg
    r = q_ref.shape[1]
    nh = r // 2
    lane = lax.broadcasted_iota(jnp.int32, (r, PAGE_SIZE), 1)

    @pl.when(j < ns)
    def _():
        q = q_ref[0]
        qb = _bf(q)
        for u in range(pg):
            page = j * pg + u
            s = _dot(qb, _bf(k_refs[u][0]))
            s_ref[page] = s + jnp.where(page == (n_pages - 1), bl_ref[...], cf_ref[...])

        @pl.when(j == 0)
        def _():
            s_new = jnp.sum(_rounded(q) * _rounded(kn_ref[0]), axis=-1, keepdims=True) + b0_ref[...]
            s_ref[n_pages] = jnp.where(lane == 0, s_new, NEG)

    @pl.when(j == ns)
    def _():
        p = _softmax_pages(s_ref)
        a = p[:, :nh, :] - lam_ref[0] * p[:, nh:, :]
        a_ref[...] = _bf(jnp.concatenate([a, jnp.zeros_like(a)], axis=1))
        a_new = a_ref[n_pages][:, 0:1].astype(jnp.float32)
        acc_ref[...] = a_new * _rounded(vn_ref[0])

    @pl.when(j >= ns)
    def _():
        acc = acc_ref[...]
        for u in range(pg):
            page = (j - ns) * pg + u
            pe = _bf(_dot(a_ref[page], ex_ref[...]) * hm_ref[...])
            acc = acc + _dot(pe, _bf(v_refs[u][0]))
        acc_ref[...] = acc

    @pl.when(j == pl.num_programs(1) - 1)
    def _():
        o_ref[0] = acc_ref[...]


def _da_sample(page_table, lam, qbd, knew, vnew, bias_last, cfar, bias0, kt, v2, pg):
    nb, n_pages = page_table.shape
    r, w = qbd.shape[1:]
    rows_v, dv = v2.shape[1:]
    nh = rows_v // PAGE_SIZE
    ns = n_pages // pg
    col = jnp.arange(rows_v, dtype=jnp.int32)
    expand = (col[None, :] // nh == jnp.arange(PAGE_SIZE, dtype=jnp.int32)[:, None]).astype(BF16)
    head_mask = (col[None, :] % nh == jnp.arange(r, dtype=jnp.int32)[:, None]).astype(jnp.float32)
    full = lambda shp: pl.BlockSpec(shp, lambda b, j, pt: (0,) * len(shp))
    per_b = lambda shp: pl.BlockSpec((1,) + shp, lambda b, j, pt: (b, 0, 0))
    grid_spec = pltpu.PrefetchScalarGridSpec(
        num_scalar_prefetch=1,
        grid=(nb, 2 * ns),
        in_specs=[pl.BlockSpec(memory_space=pltpu.SMEM),
                  per_b((r, w)), per_b((1, w)), per_b((r, dv)),
                  full((r, PAGE_SIZE)), full((r, 1)), full((r, 1)),
                  full((PAGE_SIZE, rows_v)), full((r, rows_v))]
                 + _kv_page_specs((w, PAGE_SIZE), (rows_v, dv), pg, ns),
        out_specs=per_b((r, dv)),
        scratch_shapes=[pltpu.VMEM((n_pages + 1, r, PAGE_SIZE), jnp.float32),
                        pltpu.VMEM((n_pages + 1, r, PAGE_SIZE), BF16),
                        pltpu.VMEM((r, dv), jnp.float32)])
    return pl.pallas_call(
        functools.partial(_da_sample_kernel, pg=pg, n_pages=n_pages),
        grid_spec=grid_spec,
        out_shape=jax.ShapeDtypeStruct((nb, r, dv), jnp.float32),
        compiler_params=_cparams(("arbitrary", "arbitrary")),
        name="da_sample",
    )(page_table, lam, qbd, knew, vnew, bias_last, cfar, bias0, expand, head_mask,
      *([kt] * pg), *([v2] * pg))


def _idx_sample_kernel(pt_ref, q_ref, w_ref, kn_ref, *rest, pg, n_pages, topk):
    k_refs = rest[:pg]
    sel_ref, sc_ref = rest[pg:]
    j = pl.program_id(1)
    q = q_ref[0]
    w = _rounded(w_ref[0])
    scale = IDX_DK ** -0.5 * IDX_HEADS ** -0.5
    rows = sc_ref.shape[0]
    lane = lax.broadcasted_iota(jnp.int32, (1, PAGE_SIZE), 1)

    @pl.when(j == 0)
    def _():
        sc_ref[...] = jnp.full(sc_ref.shape, -jnp.inf, jnp.float32)
        d = jnp.maximum(jnp.sum(_rounded(q) * _rounded(kn_ref[0]), axis=-1, keepdims=True), 0.0)
        s_new = jnp.sum(w * _rounded(d), axis=0, keepdims=True) * scale
        sc_ref[n_pages:n_pages + 1, :] = jnp.where(lane == 0, s_new, -jnp.inf)

    qb = _bf(q)
    for u in range(pg):
        d = _rounded(jnp.maximum(_dot(qb, _bf(k_refs[u][0])), 0.0))
        sc_ref[pl.ds(j * pg + u, 1), :] = jnp.sum(w * d, axis=0, keepdims=True) * scale

    @pl.when(j == pl.num_programs(1) - 1)
    def _():
        key = _float_key(sc_ref[...])
        pos = (lax.broadcasted_iota(jnp.int32, key.shape, 0) * PAGE_SIZE
               + lax.broadcasted_iota(jnp.int32, key.shape, 1))

        def bit_body(it, thr):
            cand = thr + (jnp.int32(1) << (31 - it))
            cnt = jnp.sum(jnp.where(key >= cand, 1.0, 0.0))
            return jnp.where(cnt >= float(topk), cand, thr)

        thr = lax.fori_loop(0, 32, bit_body, jnp.int32(INT_MIN))
        thr = jnp.maximum(thr, jnp.int32(KEY_NEG_INF + 1))
        need = float(topk) - jnp.sum(jnp.where(key > thr, 1.0, 0.0))
        n_bits = max(1, int(rows * PAGE_SIZE).bit_length())

        def cut_body(it, cut):
            cand = cut + (jnp.int32(1) << (n_bits - 1 - it))
            n_eq = jnp.sum(jnp.where((key == thr) & (pos < cand), 1.0, 0.0))
            return jnp.where(n_eq <= need, cand, cut)

        cut = lax.fori_loop(0, n_bits, cut_body, jnp.int32(0))
        sel_ref[0] = jnp.where((key > thr) | ((key == thr) & (pos < cut)), 1.0, 0.0)


def _idx_sample(page_table, qix, wix, knew, kc, pg, topk):
    nb, n_pages = page_table.shape
    rows = -(-(n_pages + 1) // 8) * 8
    per_b = lambda shp: pl.BlockSpec((1,) + shp, lambda b, j, pt: (b, 0, 0))
    grid_spec = pltpu.PrefetchScalarGridSpec(
        num_scalar_prefetch=1,
        grid=(nb, n_pages // pg),
        in_specs=[per_b((IDX_HEADS, IDX_DK)), per_b((IDX_HEADS, 1)), per_b((1, IDX_DK))]
                 + _page_specs((IDX_DK, PAGE_SIZE), pg, pg),
        out_specs=per_b((rows, PAGE_SIZE)),
        scratch_shapes=[pltpu.VMEM((rows, PAGE_SIZE), jnp.float32)])
    return pl.pallas_call(
        functools.partial(_idx_sample_kernel, pg=pg, n_pages=n_pages, topk=topk),
        grid_spec=grid_spec,
        out_shape=jax.ShapeDtypeStruct((nb, rows, PAGE_SIZE), jnp.float32),
        compiler_params=_cparams(("arbitrary", "arbitrary")),
        name="idx_sample",
    )(page_table, qix, wix, knew, *([kc] * pg))


def _sa_sample_kernel(pt_ref, q_ref, kn_ref, vn_ref, sel_ref, bl_ref, cf_ref, b0_ref, gm_ref, *rest,
                      pg, n_pages):
    k_refs, v_refs = rest[:pg], rest[pg:2 * pg]
    o_ref, s_ref, p_ref, acc_ref = rest[2 * pg:]
    j = pl.program_id(1)
    ns = n_pages // pg
    scale = SA_DH ** -0.5
    r, cols = gm_ref.shape
    lane = lax.broadcasted_iota(jnp.int32, (r, cols), 1)

    @pl.when(j < ns)
    def _():
        q = q_ref[0]
        qb = _bf(q)
        for u in range(pg):
            page = j * pg + u
            s = _dot_nt(qb, _bf(k_refs[u][0])) * scale
            s = s + jnp.where(page == (n_pages - 1), bl_ref[...], cf_ref[...])
            keep = (sel_ref[0, pl.ds(page, 1), :] > 0.0) & (gm_ref[...] > 0.0)
            s_ref[page] = jnp.where(keep, s, NEG)

        @pl.when(j == 0)
        def _():
            on = sel_ref[0, n_pages:n_pages + 1, 0:1] > 0.0
            s_new = jnp.sum(_rounded(q) * _rounded(kn_ref[0]), axis=-1, keepdims=True) * scale + b0_ref[...]
            s_ref[n_pages] = jnp.where((lane == 0) & on, s_new, NEG)

    @pl.when(j == ns)
    def _():
        p_ref[...] = _bf(_softmax_pages(s_ref))
        acc_ref[...] = p_ref[n_pages][:, 0:1].astype(jnp.float32) * _rounded(vn_ref[0])

    @pl.when(j >= ns)
    def _():
        acc = acc_ref[...]
        for u in range(pg):
            acc = acc + _dot(p_ref[(j - ns) * pg + u], _bf(v_refs[u][0]))
        acc_ref[...] = acc

    @pl.when(j == pl.num_programs(1) - 1)
    def _():
        o_ref[0] = acc_ref[...]


def _sa_sample(page_table, q, knew, vnew, sel2, bias_last2, cfar, bias0, group_mask, k2, v2, pg):
    nb, n_pages = page_table.shape
    r, dh = q.shape[1:]
    srows, cols = sel2.shape[1:]
    ns = n_pages // pg
    full = lambda shp: pl.BlockSpec(shp, lambda b, j, pt: (0,) * len(shp))
    per_b = lambda shp: pl.BlockSpec((1,) + shp, lambda b, j, pt: (b, 0, 0))
    grid_spec = pltpu.PrefetchScalarGridSpec(
        num_scalar_prefetch=1,
        grid=(nb, 2 * ns),
        in_specs=[per_b((r, dh)), per_b((r, dh)), per_b((r, dh)), per_b((srows, cols)),
                  full((r, cols)), full((r, 1)), full((r, 1)), full((r, cols))]
                 + _kv_page_specs((cols, dh), (cols, dh), pg, ns),
        out_specs=per_b((r, dh)),
        scratch_shapes=[pltpu.VMEM((n_pages + 1, r, cols), jnp.float32),
                        pltpu.VMEM((n_pages + 1, r, cols), BF16),
                        pltpu.VMEM((r, dh), jnp.float32)])
    return pl.pallas_call(
        functools.partial(_sa_sample_kernel, pg=pg, n_pages=n_pages),
        grid_spec=grid_spec,
        out_shape=jax.ShapeDtypeStruct((nb, r, dh), jnp.float32),
        compiler_params=_cparams(("arbitrary", "arbitrary")),
        name="sa_sample",
    )(page_table, q, knew, vnew, sel2, bias_last2, cfar, bias0, group_mask, *([k2] * pg), *([v2] * pg))


ROW_TILE = 256
COMBINE_TILE = 128
EXPERT_BLOCK = 512
TAIL_ROWS = 256
PAGES_PER_STEP_DA = 16
PAGES_PER_STEP_SA = 32
PAGES_PER_STEP_IDX = 64


def _rms_rows(x, g):
    return x * lax.rsqrt(jnp.mean(x * x, axis=-1, keepdims=True) + EPS) * g


def _pad_rows(x, rows):
    return jnp.pad(x, ((0, rows - x.shape[0]), (0, 0)))


def _tile_major(dest, tn):
    k, n = dest.shape
    return dest.reshape(k, n // tn, tn).transpose(1, 0, 2).reshape(-1)


def kernel(x_prompt, x_sample, c_prompt, c_sample, cache_da_k, cache_da_v, cache_sa_k, cache_sa_v, cache_idx_k, page_table, rel_bias_table, w_ada, b_ada, g_attn, g_ffn, w_in, lambda_q1, lambda_k1, lambda_q2, lambda_k2, g_subln, w_proj_da, w_proj_sa, w_out, w_router, b_router, w_gate, w_up, w_down, w_sh_gate, w_sh_up, w_sh_down, g_final):
    f32, bf16 = jnp.float32, BF16
    nb, t, d = x_prompt.shape
    nbs, dec_seq, _ = x_sample.shape
    assert dec_seq == 1
    depth = w_in.shape[0]
    n_pages = page_table.shape[1]
    past_len = n_pages * PAGE_SIZE
    n = nb * t
    n_pad = n + TAIL_ROWS
    tq = min(256, t)
    assert tq >= MAX_DISTANCE and t % tq == 0 and n % ROW_TILE == 0 and nbs <= TAIL_ROWS
    assert TAIL_ROWS == ROW_TILE and EXPERT_BLOCK % 8 == 0
    topk_p = min(SA_TOPK_MAX, t // 4)
    topk_s = min(SA_TOPK_MAX, (past_len + dec_seq) // 4)
    assert topk_p <= tq

    sizes = [DA_HEADS * 2 * DA_DK, DA_HEADS * 2 * DA_DK, DA_HEADS * DA_DV, SA_HEADS * SA_DH,
             SA_KV_HEADS * SA_DH, SA_KV_HEADS * SA_DH, IDX_HEADS * IDX_DK, IDX_DK, IDX_HEADS, d, d]
    offs = [sum(sizes[:i]) for i in range(len(sizes) + 1)]
    (o_qda, o_kda, o_vda, o_qsa, o_ksa, o_vsa, o_qix, o_kix, o_wix, o_gda, o_gsa, _) = offs
    misc_w = 2 * LANES

    table = rel_bias_table.astype(f32)
    cfar = table[N_BUCKETS - 1]
    tiles_da = _near_tiles(table[:, :DA_HEADS], tq)
    tiles_sa = _near_tiles(table[:, DA_HEADS:], tq)
    last_dist = past_len - ((n_pages - 1) * PAGE_SIZE + jnp.arange(PAGE_SIZE, dtype=jnp.int32))
    bias_last = _bias_by_distance(table, last_dist)
    bias0 = _bias_by_distance(table, jnp.zeros((1,), jnp.int32))
    rep2 = lambda a: jnp.repeat(a, 2, axis=0)

    xp = x_prompt.reshape(n, d)
    xs = x_sample.reshape(nbs, d)
    c_all = jnp.concatenate([c_prompt, c_sample], axis=0)
    leaves_p, leaves_s = [], []
    for l in range(depth):
        lam_init = 0.8 - 0.6 * math.exp(-0.3 * l)
        lam = (jnp.exp(jnp.sum(lambda_q1[l].astype(f32) * lambda_k1[l].astype(f32)))
               - jnp.exp(jnp.sum(lambda_q2[l].astype(f32) * lambda_k2[l].astype(f32))) + lam_init)
        mod = _linear_small(c_all, w_ada[l], b_ada[l], silu_in=True)
        sh1, sc1, gt1, sh2, sc2, gt2 = jnp.split(mod[:nb], 6, axis=-1)
        sh1s, sc1s, gt1s, sh2s, sc2s, gt2s = jnp.split(mod[nb:], 6, axis=-1)
        w = w_in[l]

        h = _norm_mod(xp, g_attn[l], sc1, sh1, t, ROW_TILE)
        q_da, = _mm(h, w, o_qda, sizes[0], (bf16,), scale=DA_DK ** -0.5)
        k_da_b, k_da_t = _mm(h, w, o_kda, sizes[1], (bf16,), rows_per_batch=t)
        k_da = k_da_t.reshape(nb, DA_HEADS, 2, DA_DK, t).transpose(0, 4, 1, 2, 3)
        v_da, v_da_b = _mm(h, w, o_vda, sizes[2], (f32, bf16))
        q_sa, = _mm(h, w, o_qsa, sizes[3], (bf16,), scale=SA_DH ** -0.5)
        k_sa, k_sa_b = _mm(h, w, o_ksa, sizes[4], (f32, bf16))
        v_sa, v_sa_b = _mm(h, w, o_vsa, sizes[5], (f32, bf16))
        q_ix, = _mm(h, w, o_qix, sizes[6], (bf16,), tn=512)
        misc, misc_b = _mm(h, w, o_kix, misc_w, (f32, bf16))
        gates, = _mm(h, w[:, o_gda:], 0, 2 * d, (bf16,), sigmoid=True)
        k_ix = misc[:, :IDX_DK]

        o_da = _da_prompt(q_da, k_da_b, v_da_b, tiles_da, cfar[:DA_HEADS], lam.reshape(1),
                          g_subln[l].astype(f32), nb, t, tq, 1.0 - lam_init)
        o_sa = _sa_prompt(q_sa, q_ix, misc, misc_b, k_sa_b, v_sa_b, tiles_sa, cfar[DA_HEADS:], nb, t, tq, topk_p)
        m = _proj_gate(o_da, o_sa, w_proj_da[l], w_proj_sa[l], gates, d)
        x1 = _mm_resid(m, w_out[l], xp, gt1, t)

        hs = _rms_rows(xs, g_attn[l].astype(f32)) * (1.0 + sc1s) + sh1s
        ps = _linear_small(hs, w)
        seg = lambda i: ps[:, offs[i]:offs[i + 1]]
        q_da_s, k_da_s, v_da_s, q_sa_s, k_sa_s, v_sa_s, q_ix_s, k_ix_s, w_ix_s, gda_s, gsa_s = [
            seg(i) for i in range(11)]

        r_da = 2 * DA_HEADS
        q16 = (q_da_s * DA_DK ** -0.5).reshape(nbs, DA_HEADS, 2, DA_DK).transpose(0, 2, 1, 3).reshape(nbs, r_da, DA_DK)
        blk_of_row = 2 * (jnp.arange(r_da) % DA_HEADS) + jnp.arange(r_da) // DA_HEADS
        place = (blk_of_row[:, None] == jnp.arange(r_da)[None, :]).astype(f32)
        qbd_da = (place[None, :, :, None] * q16[:, :, None, :]).reshape(nbs, r_da, -1)
        both = lambda a: jnp.concatenate([a, a], axis=0)
        kt_da = cache_da_k[l].transpose(0, 2, 3, 4, 1).reshape(-1, DA_HEADS * 2 * DA_DK, PAGE_SIZE)
        v2_da = cache_da_v[l].reshape(-1, PAGE_SIZE * DA_HEADS, DA_DV)
        v_rows = jnp.pad(v_da_s.reshape(nbs, DA_HEADS, DA_DV), ((0, 0), (0, r_da - DA_HEADS), (0, 0)))
        a_da = _da_sample(page_table, lam.reshape(1), qbd_da, k_da_s[:, None, :], v_rows,
                          both(bias_last[:DA_HEADS]), both(cfar[:DA_HEADS, None]), both(bias0[:DA_HEADS]),
                          kt_da, v2_da, math.gcd(PAGES_PER_STEP_DA, n_pages))
        o_da_s = a_da[:, :DA_HEADS]
        o_da_s = _rms_rows(o_da_s, g_subln[l].astype(f32)) * (1.0 - lam_init)

        sel = _idx_sample(page_table, q_ix_s.reshape(nbs, IDX_HEADS, IDX_DK), w_ix_s[:, :, None],
                          k_ix_s[:, None, :], cache_idx_k[l].transpose(0, 2, 1),
                          math.gcd(PAGES_PER_STEP_IDX, n_pages), topk_s)
        r_sa = 2 * SA_HEADS
        kv_of_row = jnp.minimum(jnp.arange(r_sa) // SA_GROUP, SA_KV_HEADS - 1)
        pad_sa = lambda a: jnp.pad(a, ((0, 0), (0, r_sa - SA_HEADS), (0, 0)))
        rep_kv = lambda a: jnp.repeat(a, SA_KV_HEADS, axis=-1)
        col_kv = jnp.arange(PAGE_SIZE * SA_KV_HEADS) % SA_KV_HEADS
        k2_sa = cache_sa_k[l].reshape(-1, PAGE_SIZE * SA_KV_HEADS, SA_DH)
        v2_sa = cache_sa_v[l].reshape(-1, PAGE_SIZE * SA_KV_HEADS, SA_DH)
        a_sa = _sa_sample(page_table, pad_sa(q_sa_s.reshape(nbs, SA_HEADS, SA_DH)),
                          k_sa_s.reshape(nbs, SA_KV_HEADS, SA_DH)[:, kv_of_row],
                          v_sa_s.reshape(nbs, SA_KV_HEADS, SA_DH)[:, kv_of_row],
                          rep_kv(sel), rep_kv(_pad_rows(bias_last[DA_HEADS:], r_sa)),
                          _pad_rows(cfar[DA_HEADS:, None], r_sa), _pad_rows(bias0[DA_HEADS:], r_sa),
                          (col_kv[None, :] == kv_of_row[:, None]).astype(f32),
                          k2_sa, v2_sa, math.gcd(PAGES_PER_STEP_SA, n_pages))
        o_sa_s = a_sa[:, :SA_HEADS]

        pda = _linear_small(o_da_s.reshape(nbs, -1), w_proj_da[l])
        psa = _linear_small(o_sa_s.reshape(nbs, -1), w_proj_sa[l])
        ms = _sigmoid(gda_s) * pda + _sigmoid(gsa_s) * psa
        x1s = xs + gt1s * _linear_small(ms, w_out[l])
        h2s = _rms_rows(x1s, g_ffn[l].astype(f32)) * (1.0 + sc2s) + sh2s
        lg_s = _linear_small(h2s, w_router[l])

        h2_all, lg_all = _norm_router(x1, g_ffn[l], sc2, sh2, w_router[l].T,
                                      _pad_rows(h2s.astype(bf16), TAIL_ROWS),
                                      _pad_rows(lg_s, TAIL_ROWS).T, t, ROW_TILE)
        eidx, wts, rank, cnt = _route(lg_all, b_router[l], ROW_TILE)
        cnt_tile = cnt[:, :, 0]
        total = jnp.sum(cnt_tile, axis=0)
        padded = jnp.ceil(total / EXPERT_BLOCK) * EXPERT_BLOCK
        pends = jnp.cumsum(padded)
        pstart = pends - padded
        base = pstart[None, :] + jnp.cumsum(cnt_tile, axis=0) - cnt_tile
        dest = _dest(eidx, rank, jnp.broadcast_to(base[:, :, None], base.shape + (LANES,)), ROW_TILE)
        n_blk = -(-(n_pad * TOP_K) // EXPERT_BLOCK) + N_EXPERTS
        blk_start = (jnp.arange(n_blk) * EXPERT_BLOCK).astype(f32)
        blk_e = jnp.minimum(jnp.sum(pends[None, :] <= blk_start[:, None], axis=1), N_EXPERTS - 1).astype(jnp.int32)
        n_used = (pends[-1] / EXPERT_BLOCK).astype(jnp.int32).reshape(1)
        n_rows = n_blk * EXPERT_BLOCK
        pad_start = jnp.concatenate([pstart + total, pends[-1:]]).astype(jnp.int32)
        pad_cnt = jnp.concatenate([padded - total, (n_rows - pends[-1:]) / 8]).astype(jnp.int32)
        xs_sorted = _dispatch(h2_all, _tile_major(dest, ROW_TILE), pad_start, pad_cnt, n_rows, ROW_TILE)
        owns = padded > 0
        e_ids = jnp.arange(N_EXPERTS, dtype=jnp.int32)
        later = jnp.where(owns[None, :] & (e_ids[None, :] > e_ids[:, None]), e_ids[None, :], N_EXPERTS)
        next_owner = jnp.min(later, axis=1)
        next_owner = jnp.where(next_owner < N_EXPERTS, next_owner, -1).astype(jnp.int32)
        slot_of_e = ((jnp.cumsum(owns.astype(jnp.int32)) - 1) % 2).astype(jnp.int32)
        ys = _experts(xs_sorted, blk_e, n_used, next_owner, slot_of_e,
                      w_gate[l], w_up[l], w_down[l], EXPERT_BLOCK)
        shared = _ffn_shared(h2_all, w_sh_gate[l], w_sh_up[l], w_sh_down[l], ROW_TILE)
        dest_c = _tile_major(dest, COMBINE_TILE)
        wts_tok = wts.T
        last = l == depth - 1
        xp = _combine(dest_c, ys, wts_tok, shared, x1, gt2[:, None, :], g_final.astype(f32),
                      0, COMBINE_TILE, last)
        tail = _combine(dest_c, ys, wts_tok, shared, _pad_rows(x1s, TAIL_ROWS),
                        _pad_rows(gt2s, TAIL_ROWS).reshape(-1, COMBINE_TILE, d), g_final.astype(f32),
                        n // COMBINE_TILE, COMBINE_TILE, last)
        xs = tail[:nbs]
        leaves_p.append((k_da, v_da, k_sa, v_sa, k_ix))
        leaves_s.append((k_da_s, v_da_s, k_sa_s, v_sa_s, k_ix_s))

    shapes = [(DA_HEADS, 2, DA_DK), (DA_HEADS, DA_DV), (SA_KV_HEADS, SA_DH), (SA_KV_HEADS, SA_DH), (IDX_DK,)]
    out_p = [jnp.stack([lv[i].reshape((nb, t) + shapes[i]) for lv in leaves_p]) for i in range(5)]
    out_s = [jnp.stack([lv[i].reshape((nbs, dec_seq) + shapes[i]) for lv in leaves_s]) for i in range(5)]
    return (xp.reshape(nb, t, d), xs.reshape(nbs, dec_seq, d), *out_p, *out_s)
```

```python
import functools
import math

import jax
import jax.numpy as jnp
from jax import lax
from jax.experimental import pallas as pl
from jax.experimental.pallas import tpu as pltpu

DA_HEADS = 8
DA_DK = 64
DA_DV = 2 * DA_DK
SA_HEADS = 8
SA_KV_HEADS = 2
SA_DH = 128
SA_GROUP = SA_HEADS // SA_KV_HEADS
IDX_HEADS = 16
IDX_DK = 64
SA_TOPK_MAX = 256
N_BUCKETS = 32
MAX_DISTANCE = 128
N_EXPERTS = 64
N_GROUPS = 8
GROUP_SIZE = N_EXPERTS // N_GROUPS
TOPK_GROUPS = 4
TOP_K = 8
ROUTED_SCALE = 2.5
PAGE_SIZE = 128
EPS = 1e-6

LANES = 128
VMEM_LIMIT = 56 * 1024 * 1024

BF16 = jnp.bfloat16
NEG = -1e30
INT_MIN = -(2 ** 31)
KEY_NEG_INF = (0xFF800000 ^ 0x7FFFFFFF) - (1 << 32)

_NT = (((1,), (1,)), ((), ()))


def _cparams(sem):
    return pltpu.CompilerParams(dimension_semantics=sem, vmem_limit_bytes=VMEM_LIMIT)


def _dot(a, b):
    return jnp.dot(a, b, preferred_element_type=jnp.float32)


def _dot_nt(a, b):
    return lax.dot_general(a, b, _NT, preferred_element_type=jnp.float32)


def _bf(x):
    return x.astype(BF16)


def _rounded(x):
    return x.astype(BF16).astype(jnp.float32)


def _sigmoid(x):
    return 1.0 / (1.0 + jnp.exp(-x))


def _silu(x):
    return x * _sigmoid(x)


def _float_key(s):
    b = pltpu.bitcast(s, jnp.int32)
    return b ^ ((b >> 31) & jnp.int32(0x7FFFFFFF))


def _linear_small_kernel(x_ref, w_ref, b_ref, o_ref, *, silu_in):
    x = x_ref[...]
    if silu_in:
        x = _silu(x)
    o_ref[...] = _dot(_bf(x), _bf(w_ref[...])) + b_ref[...]


def _linear_small(x, w, b=None, *, silu_in=False, tn=512):
    m0, k = x.shape
    m = -(-m0 // 16) * 16
    x = jnp.pad(x, ((0, m - m0), (0, 0)))
    n = w.shape[1]
    tn = min(tn, n)
    if b is None:
        b = jnp.zeros((1, n), jnp.float32)
    out = pl.pallas_call(
        functools.partial(_linear_small_kernel, silu_in=silu_in),
        grid=(pl.cdiv(n, tn),),
        in_specs=[pl.BlockSpec((m, k), lambda j: (0, 0)),
                  pl.BlockSpec((k, tn), lambda j: (0, j)),
                  pl.BlockSpec((1, tn), lambda j: (0, j))],
        out_specs=pl.BlockSpec((m, tn), lambda j: (0, j)),
        out_shape=jax.ShapeDtypeStruct((m, n), jnp.float32),
        compiler_params=_cparams(("arbitrary",)),
        name="linear_small",
    )(x, w, b.reshape(1, n))
    return out[:m0]


def _norm_mod_kernel(x_ref, g_ref, sc_ref, sh_ref, o_ref):
    x = x_ref[...]
    y = x * lax.rsqrt(jnp.mean(x * x, axis=-1, keepdims=True) + EPS) * g_ref[...]
    o_ref[...] = (y * (1.0 + sc_ref[0]) + sh_ref[0]).astype(o_ref.dtype)


def _norm_mod(x, g, sc, sh, rows_per_batch, tm):
    n, d = x.shape
    per = rows_per_batch // tm
    return pl.pallas_call(
        _norm_mod_kernel,
        grid=(n // tm,),
        in_specs=[pl.BlockSpec((tm, d), lambda i: (i, 0)),
                  pl.BlockSpec((1, d), lambda i: (0, 0)),
                  pl.BlockSpec((1, 1, d), lambda i: (i // per, 0, 0)),
                  pl.BlockSpec((1, 1, d), lambda i: (i // per, 0, 0))],
        out_specs=pl.BlockSpec((tm, d), lambda i: (i, 0)),
        out_shape=jax.ShapeDtypeStruct((n, d), BF16),
        compiler_params=_cparams(("arbitrary",)),
        name="norm_mod",
    )(x, g.reshape(1, d), sc[:, None, :], sh[:, None, :])


def _mm_kernel(x_ref, w_ref, *rest, scale, sigmoid, n_out, transposed_out):
    o_refs, wbf_ref = rest[:n_out], rest[-1]

    @pl.when(pl.program_id(1) == 0)
    def _():
        wbf_ref[...] = w_ref[...].astype(BF16)

    acc = _dot(x_ref[...], wbf_ref[...])
    if scale != 1.0:
        acc = acc * scale
    if sigmoid:
        acc = _sigmoid(acc)
    for o in o_refs:
        o[...] = acc.astype(o.dtype)
    if transposed_out:
        rest[n_out][0] = acc.T


def _mm(x, w, col0, ncols, out_dtypes, *, scale=1.0, sigmoid=False, tm=512, tn=1024, rows_per_batch=None):
    m, k = x.shape
    tn = min(tn, ncols)
    tm = min(tm, m)
    assert col0 % tn == 0 and ncols % tn == 0 and m % tm == 0
    jb = col0 // tn
    out_specs = [pl.BlockSpec((tm, tn), lambda j, i: (i, j)) for _ in out_dtypes]
    out_shape = [jax.ShapeDtypeStruct((m, ncols), dt) for dt in out_dtypes]
    if rows_per_batch is not None:
        per = rows_per_batch // tm
        out_specs.append(pl.BlockSpec((1, tn, tm), lambda j, i: (i // per, j, i % per)))
        out_shape.append(jax.ShapeDtypeStruct((m // rows_per_batch, ncols, rows_per_batch), jnp.float32))
    outs = pl.pallas_call(
        functools.partial(_mm_kernel, scale=scale, sigmoid=sigmoid, n_out=len(out_dtypes),
                          transposed_out=rows_per_batch is not None),
        grid=(ncols // tn, m // tm),
        in_specs=[pl.BlockSpec((tm, k), lambda j, i: (i, 0)),
                  pl.BlockSpec((k, tn), lambda j, i: (0, jb + j))],
        out_specs=out_specs,
        out_shape=out_shape,
        scratch_shapes=[pltpu.VMEM((k, tn), BF16)],
        compiler_params=_cparams(("arbitrary", "arbitrary")),
        name="mm_cols",
    )(x, w)
    return outs


def _rel_bucket(dist):
    max_exact = N_BUCKETS // 2
    d = jnp.maximum(dist, 0)
    large = max_exact + (jnp.log(jnp.maximum(d, 1).astype(jnp.float32) / max_exact)
                         / math.log(MAX_DISTANCE / max_exact)
                         * (N_BUCKETS - max_exact)).astype(jnp.int32)
    large = jnp.minimum(large, N_BUCKETS - 1)
    return jnp.where(d < max_exact, d, large)


def _bias_by_distance(table, dists):
    return table[_rel_bucket(dists)].astype(jnp.float32).T


def _toeplitz_kernel(u_ref, o_ref):
    t = o_ref.shape[2]
    x = jnp.broadcast_to(u_ref[0], (t, 2 * t))
    o_ref[0, 0] = pltpu.roll(x, 0, 1, stride=1, stride_axis=0)[:, :t]


def _near_tiles(table, t):
    nh = table.shape[1]
    k = jnp.arange(2 * t, dtype=jnp.int32)
    gens = []
    for off in (0, t):
        d = jnp.where(k < t, off - k, off + 2 * t - k)
        gens.append(jnp.where(d[None] >= 0, _bias_by_distance(table, d), NEG))
    u = jnp.stack(gens, axis=1).reshape(nh * 2, 1, 2 * t)
    return pl.pallas_call(
        _toeplitz_kernel,
        grid=(nh, 2),
        in_specs=[pl.BlockSpec((1, 1, 2 * t), lambda h, o: (h * 2 + o, 0, 0))],
        out_specs=pl.BlockSpec((1, 1, t, t), lambda h, o: (h, o, 0, 0)),
        out_shape=jax.ShapeDtypeStruct((nh, 2, t, t), jnp.float32),
        compiler_params=_cparams(("arbitrary", "arbitrary")),
        name="bias_tiles",
    )(u)


def _fold_lanes(x, op):
    out = x[:, :LANES]
    for c in range(1, x.shape[1] // LANES):
        out = op(out, x[:, c * LANES:(c + 1) * LANES])
    return out


def _chunk_loop(n, fn):
    def body(i, carry):
        for u in range(4):
            fn(4 * i + u)
        return carry

    lax.fori_loop(0, n // 4, body, 0)
    base = (n // 4) * 4

    @pl.when(n % 4 >= 2)
    def _():
        fn(base)
        fn(base + 1)

    @pl.when(n % 2 == 1)
    def _():
        fn(n - 1)


def _da_prompt_kernel(cfar_ref, lam_ref, q_ref, k_ref, v_ref, tile_ref, g_ref, o_ref,
                      s_ref, mpart_ref, shift_ref, lpart_ref, acc_ref, *, tq, out_scale):
    h = pl.program_id(1)
    qi = pl.program_id(2)
    q = q_ref[...]
    lane = lax.broadcasted_iota(jnp.int32, q.shape, 1)
    zero = jnp.zeros_like(q)
    q2 = jnp.concatenate([jnp.where(lane < DA_DK, q, zero), jnp.where(lane >= DA_DK, q, zero)], axis=0)
    cfar = cfar_ref[h]
    n_far = jnp.maximum(qi - 1, 0)
    r2 = 2 * tq

    def chunk_rows(kc):
        return pl.ds(pl.multiple_of(kc * tq, tq), tq)

    def scores(kc, bias):
        s = _dot_nt(q2, k_ref[chunk_rows(kc), :])
        if bias is not None:
            s = s + jnp.concatenate([bias, bias], axis=0)
        s_ref[kc] = s
        mpart_ref[...] = jnp.maximum(mpart_ref[...], _fold_lanes(s, jnp.maximum))

    mpart_ref[...] = jnp.full((r2, LANES), NEG, jnp.float32)
    _chunk_loop(n_far, lambda kc: scores(kc, None))
    m_far = jnp.max(mpart_ref[...], axis=-1, keepdims=True) + cfar
    mpart_ref[...] = jnp.full((r2, LANES), NEG, jnp.float32)

    @pl.when(qi >= 1)
    def _():
        scores(qi - 1, tile_ref[0, 1])

    scores(qi, tile_ref[0, 0])
    m = jnp.maximum(m_far, jnp.max(mpart_ref[...], axis=-1, keepdims=True))
    shift_ref[0] = jnp.broadcast_to(m - cfar, (r2, LANES))
    shift_ref[1] = jnp.broadcast_to(m, (r2, LANES))

    lpart_ref[...] = jnp.zeros((r2, LANES), jnp.float32)
    acc_ref[...] = jnp.zeros((r2, DA_DV), jnp.float32)

    def weights(kc, which):
        s = s_ref[kc]
        sh = shift_ref[which]
        ps = [jnp.exp(s[:, c * LANES:(c + 1) * LANES] - sh) for c in range(tq // LANES)]
        tot = ps[0]
        for pc in ps[1:]:
            tot = tot + pc
        lpart_ref[...] = lpart_ref[...] + tot
        p = jnp.concatenate(ps, axis=1).astype(BF16)
        acc_ref[...] = acc_ref[...] + _dot(p, v_ref[chunk_rows(kc), :])

    _chunk_loop(n_far, lambda kc: weights(kc, 0))

    @pl.when(qi >= 1)
    def _():
        weights(qi - 1, 1)

    weights(qi, 1)

    lam = lam_ref[0]
    a = acc_ref[...] / jnp.sum(lpart_ref[...], axis=-1, keepdims=True)
    o = a[:tq] - lam * a[tq:]
    o = o * lax.rsqrt(jnp.mean(o * o, axis=-1, keepdims=True) + EPS) * g_ref[...]
    o_ref[...] = (o * out_scale).astype(o_ref.dtype)


def _da_prompt(q, k, v, tiles, cfar, lam, g_subln, nb, t, tq, out_scale):
    n = q.shape[0]
    nq = t // tq
    grid_spec = pltpu.PrefetchScalarGridSpec(
        num_scalar_prefetch=0,
        grid=(nb, DA_HEADS, nq),
        in_specs=[pl.BlockSpec(memory_space=pltpu.SMEM),
                  pl.BlockSpec(memory_space=pltpu.SMEM),
                  pl.BlockSpec((tq, LANES), lambda b, h, i: (b * nq + i, h)),
                  pl.BlockSpec((t, LANES), lambda b, h, i: (b, h)),
                  pl.BlockSpec((t, LANES), lambda b, h, i: (b, h)),
                  pl.BlockSpec((1, 2, tq, tq), lambda b, h, i: (h, 0, 0, 0)),
                  pl.BlockSpec((1, DA_DV), lambda b, h, i: (0, 0))],
        out_specs=pl.BlockSpec((tq, LANES), lambda b, h, i: (b * nq + i, h)),
        scratch_shapes=[pltpu.VMEM((nq, 2 * tq, tq), jnp.float32),
                        pltpu.VMEM((2 * tq, LANES), jnp.float32),
                        pltpu.VMEM((2, 2 * tq, LANES), jnp.float32),
                        pltpu.VMEM((2 * tq, LANES), jnp.float32),
                        pltpu.VMEM((2 * tq, DA_DV), jnp.float32)])
    return pl.pallas_call(
        functools.partial(_da_prompt_kernel, tq=tq, out_scale=out_scale),
        grid_spec=grid_spec,
        out_shape=jax.ShapeDtypeStruct((n, DA_HEADS * DA_DV), BF16),
        compiler_params=_cparams(("arbitrary", "arbitrary", "arbitrary")),
        name="da_prompt",
    )(cfar, lam, q, k, v, tiles, g_subln.reshape(1, DA_DV))


def _sa_prompt_kernel(cfar_ref, qs_ref, qx_ref, mq_ref, mk_ref, ks_ref, vs_ref, tile_ref, o_ref,
                      k2_ref, key_ref, hi_ref, lo_ref, cut_ref, s_ref, mpart_ref, shift_ref, lpart_ref,
                      acc_ref, *, tq, topk):
    qi = pl.program_id(1)
    n_chunks = qi + 1
    t = mk_ref.shape[0]

    @pl.when(qi == 0)
    def _():
        kix = mk_ref[:, :LANES].astype(jnp.float32)
        lane = lax.broadcasted_iota(jnp.int32, kix.shape, 1)
        k2_ref[0] = jnp.where(lane < IDX_DK, kix, 0.0).astype(BF16)
        k2_ref[1] = jnp.where(lane >= IDX_DK, pltpu.roll(kix, IDX_DK, axis=1), 0.0).astype(BF16)

    wix = mq_ref[:, IDX_DK:IDX_DK + IDX_HEADS]
    wcols = [wix[:, hh:hh + 1] for hh in range(IDX_HEADS)]
    row = lax.broadcasted_iota(jnp.int32, (tq, tq), 0)
    col = lax.broadcasted_iota(jnp.int32, (tq, tq), 1)

    def score_body(kc, carry):
        rows = pl.ds(pl.multiple_of(kc * tq, tq), tq)
        ke = k2_ref[0, rows, :]
        ko = k2_ref[1, rows, :]
        sc = jnp.zeros((tq, tq), jnp.float32)
        for p in range(IDX_HEADS // 2):
            qp = qx_ref[:, p * LANES:(p + 1) * LANES]
            sc = sc + wcols[2 * p] * jnp.maximum(_dot_nt(qp, ke), 0.0)
            sc = sc + wcols[2 * p + 1] * jnp.maximum(_dot_nt(qp, ko), 0.0)
        sc = sc * (IDX_DK ** -0.5 * IDX_HEADS ** -0.5)
        sc = jnp.where((kc < qi) | (row >= col), sc, -jnp.inf)
        key = _float_key(sc)
        key_ref[kc] = key
        hi_ref[kc] = (key >> 16).astype(jnp.int16)
        return carry

    lax.fori_loop(0, n_chunks, score_body, 0)

    i16 = jnp.int16
    i16_min = -(2 ** 15)
    one_i = jnp.ones((tq, tq), i16)
    zero_i = jnp.zeros((tq, tq), i16)
    ones_col = jnp.ones((tq, LANES), BF16)

    def wide16(x):
        return jnp.concatenate([x] * (tq // LANES), axis=1).astype(i16)

    def count_ge16(ref16, cand):
        c16 = wide16(cand)

        def body(kc, acc):
            return acc + jnp.where(ref16[kc] >= c16, one_i, zero_i)
        acc = lax.fori_loop(0, n_chunks, body, zero_i)
        return _dot(acc.astype(jnp.float32).astype(BF16), ones_col)

    def search16(ref16, base, need):
        def bit_body(it, carry):
            v, cnt_v = carry
            cand = v + (jnp.int32(1) << (15 - it))
            cnt = base + count_ge16(ref16, cand)
            ok = cnt >= need
            return jnp.where(ok, cand, v), jnp.where(ok, cnt, cnt_v)
        v0 = jnp.full((tq, LANES), i16_min, jnp.int32)
        c0 = jnp.full((tq, LANES), 3.0e38, jnp.float32)
        return lax.fori_loop(0, 16, bit_body, (v0, c0))

    zero_cnt = jnp.zeros((tq, LANES), jnp.float32)
    t_hi, _ = search16(hi_ref, zero_cnt, float(topk))
    n_above = jnp.where(t_hi < 2 ** 15 - 1, count_ge16(hi_ref, jnp.minimum(t_hi + 1, 2 ** 15 - 1)), 0.0)
    t_hi16 = wide16(t_hi)

    def lower_body(kc, carry):
        lo = ((key_ref[kc] & jnp.int32(0xFFFF)) - 2 ** 15).astype(i16)
        lo_ref[kc] = jnp.where(hi_ref[kc] == t_hi16, lo, jnp.full((tq, tq), i16_min, i16))
        return carry

    lax.fori_loop(0, n_chunks, lower_body, 0)
    t_lo, cnt_w = search16(lo_ref, n_above, float(topk))
    thr = ((t_hi << 16) + (t_lo + 2 ** 15))[:, :1]
    cnt_thr = cnt_w[:, :1]
    tied = (cnt_thr > float(topk)) & (thr > jnp.int32(KEY_NEG_INF))
    need_tie = jnp.max(jnp.where(tied, 1.0, 0.0)) > 0.0
    thr = jnp.maximum(thr, jnp.int32(KEY_NEG_INF + 1))

    cut_ref[...] = jnp.full((tq, 1), 2 ** 30, jnp.int32)

    @pl.when(need_tie)
    def _():
        def gt_body(kc, acc):
            g = jnp.where(key_ref[kc] > thr, 1.0, 0.0)
            return acc + jnp.sum(g, axis=-1, keepdims=True)
        n_gt = lax.fori_loop(0, n_chunks, gt_body, jnp.zeros((tq, 1), jnp.float32))
        need = float(topk) - n_gt
        n_bits = max(1, int(t).bit_length())

        def cut_body(it, cut):
            cand = cut + (jnp.int32(1) << (n_bits - 1 - it))

            def eq_body(kc, acc):
                pos = kc * tq + col
                e = jnp.where((key_ref[kc] == thr) & (pos < cand), 1.0, 0.0)
                return acc + jnp.sum(e, axis=-1, keepdims=True)
            n_eq = lax.fori_loop(0, n_chunks, eq_body, jnp.zeros((tq, 1), jnp.float32))
            return jnp.where(n_eq <= need, cand, cut)
        cut_ref[...] = lax.fori_loop(0, n_bits, cut_body, jnp.zeros((tq, 1), jnp.int32))

    cut = cut_ref[...]

    n_far = jnp.maximum(qi - 1, 0)
    rg = SA_GROUP * tq

    def chunk_rows(kc):
        return pl.ds(pl.multiple_of(kc * tq, tq), tq)

    for g in range(SA_KV_HEADS):
        heads = [g * SA_GROUP + j for j in range(SA_GROUP)]
        qg = jnp.concatenate([qs_ref[:, hh * LANES:(hh + 1) * LANES] for hh in heads], axis=0)
        cf_rows = jnp.concatenate([jnp.full((tq, 1), cfar_ref[hh], jnp.float32) for hh in heads], axis=0)

        def scores(kc, kind, qg=qg, heads=heads, g=g):
            key = key_ref[kc]
            sel = (key > thr) | ((key == thr) & (kc * tq + col < cut))
            s_all = _dot_nt(qg, ks_ref[chunk_rows(kc), g * SA_DH:(g + 1) * SA_DH])
            parts = []
            for j, hh in enumerate(heads):
                s = s_all[j * tq:(j + 1) * tq]
                if kind is not None:
                    s = s + tile_ref[hh, kind]
                parts.append(jnp.where(sel, s, NEG))
            s = jnp.concatenate(parts, axis=0)
            s_ref[kc] = s
            mpart_ref[...] = jnp.maximum(mpart_ref[...], _fold_lanes(s, jnp.maximum))

        mpart_ref[...] = jnp.full((rg, LANES), NEG, jnp.float32)
        _chunk_loop(n_far, lambda kc, f=scores: f(kc, None))
        m_far = jnp.max(mpart_ref[...], axis=-1, keepdims=True) + cf_rows
        mpart_ref[...] = jnp.full((rg, LANES), NEG, jnp.float32)

        @pl.when(qi >= 1)
        def _(f=scores):
            f(qi - 1, 1)

        scores(qi, 0)
        m = jnp.maximum(m_far, jnp.max(mpart_ref[...], axis=-1, keepdims=True))
        shift_ref[0] = jnp.broadcast_to(m - cf_rows, (rg, LANES))
        shift_ref[1] = jnp.broadcast_to(m, (rg, LANES))
        lpart_ref[...] = jnp.zeros((rg, LANES), jnp.float32)
        acc_ref[...] = jnp.zeros((rg, SA_DH), jnp.float32)

        def weights(kc, which, g=g):
            s = s_ref[kc]
            sh = shift_ref[which]
            ps = [jnp.exp(s[:, c * LANES:(c + 1) * LANES] - sh) for c in range(tq // LANES)]
            tot = ps[0]
            for pc in ps[1:]:
                tot = tot + pc
            lpart_ref[...] = lpart_ref[...] + tot
            p = jnp.concatenate(ps, axis=1).astype(BF16)
            acc_ref[...] = acc_ref[...] + _dot(p, vs_ref[chunk_rows(kc), g * SA_DH:(g + 1) * SA_DH])

        _chunk_loop(n_far, lambda kc, f=weights: f(kc, 0))

        @pl.when(qi >= 1)
        def _(f=weights):
            f(qi - 1, 1)

        weights(qi, 1)
        a = acc_ref[...] / jnp.sum(lpart_ref[...], axis=-1, keepdims=True)
        for j, hh in enumerate(heads):
            o_ref[:, hh * SA_DH:(hh + 1) * SA_DH] = a[j * tq:(j + 1) * tq].astype(o_ref.dtype)


def _sa_prompt(q_sa, q_ix, misc_q, misc_k, k_sa, v_sa, tiles, cfar, nb, t, tq, topk):
    n = q_sa.shape[0]
    nq = t // tq
    mw = misc_q.shape[1]
    kvw = SA_KV_HEADS * SA_DH
    once = dict(pipeline_mode=pl.Buffered(1))
    rg = SA_GROUP * tq
    return pl.pallas_call(
        functools.partial(_sa_prompt_kernel, tq=tq, topk=topk),
        grid=(nb, nq),
        in_specs=[pl.BlockSpec(memory_space=pltpu.SMEM),
                  pl.BlockSpec((tq, SA_HEADS * SA_DH), lambda b, i: (b * nq + i, 0)),
                  pl.BlockSpec((tq, IDX_HEADS * IDX_DK), lambda b, i: (b * nq + i, 0)),
                  pl.BlockSpec((tq, mw), lambda b, i: (b * nq + i, 0)),
                  pl.BlockSpec((t, mw), lambda b, i: (b, 0), **once),
                  pl.BlockSpec((t, kvw), lambda b, i: (b, 0), **once),
                  pl.BlockSpec((t, kvw), lambda b, i: (b, 0), **once),
                  pl.BlockSpec((SA_HEADS, 2, tq, tq), lambda b, i: (0, 0, 0, 0), **once)],
        out_specs=pl.BlockSpec((tq, SA_HEADS * SA_DH), lambda b, i: (b * nq + i, 0)),
        out_shape=jax.ShapeDtypeStruct((n, SA_HEADS * SA_DH), BF16),
        scratch_shapes=[pltpu.VMEM((2, t, LANES), BF16),
                        pltpu.VMEM((nq, tq, tq), jnp.int32),
                        pltpu.VMEM((nq, tq, tq), jnp.int16),
                        pltpu.VMEM((nq, tq, tq), jnp.int16),
                        pltpu.VMEM((tq, 1), jnp.int32),
                        pltpu.VMEM((nq, rg, tq), jnp.float32),
                        pltpu.VMEM((rg, LANES), jnp.float32),
                        pltpu.VMEM((2, rg, LANES), jnp.float32),
                        pltpu.VMEM((rg, LANES), jnp.float32),
                        pltpu.VMEM((rg, SA_DH), jnp.float32)],
        compiler_params=_cparams(("arbitrary", "arbitrary")),
        name="sa_prompt",
    )(cfar, q_sa, q_ix, misc_q, misc_k, k_sa, v_sa, tiles)


def _proj_gate_kernel(oda_ref, osa_ref, wpd_ref, wps_ref, gda_ref, gsa_ref, o_ref, wbf_ref):
    @pl.when(pl.program_id(1) == 0)
    def _():
        wbf_ref[0] = wpd_ref[...].astype(BF16)
        wbf_ref[1] = wps_ref[...].astype(BF16)

    a = _dot(oda_ref[...], wbf_ref[0])
    b = _dot(osa_ref[...], wbf_ref[1])
    o_ref[...] = (gda_ref[...].astype(jnp.float32) * a
                  + gsa_ref[...].astype(jnp.float32) * b).astype(o_ref.dtype)


def _proj_gate(o_da, o_sa, w_pd, w_ps, gates, d, tm=512, tn=512):
    n, kd = o_da.shape
    ks = o_sa.shape[1]
    tn = min(tn, d)
    tm = min(tm, n)
    nj = d // tn
    return pl.pallas_call(
        _proj_gate_kernel,
        grid=(nj, n // tm),
        in_specs=[pl.BlockSpec((tm, kd), lambda j, i: (i, 0)),
                  pl.BlockSpec((tm, ks), lambda j, i: (i, 0)),
                  pl.BlockSpec((kd, tn), lambda j, i: (0, j)),
                  pl.BlockSpec((ks, tn), lambda j, i: (0, j)),
                  pl.BlockSpec((tm, tn), lambda j, i: (i, j)),
                  pl.BlockSpec((tm, tn), lambda j, i: (i, nj + j))],
        out_specs=pl.BlockSpec((tm, tn), lambda j, i: (i, j)),
        out_shape=jax.ShapeDtypeStruct((n, d), BF16),
        scratch_shapes=[pltpu.VMEM((2, kd, tn), BF16)],
        compiler_params=_cparams(("arbitrary", "arbitrary")),
        name="proj_gate",
    )(o_da, o_sa, w_pd, w_ps, gates, gates)


def _mm_resid_kernel(m_ref, w_ref, x_ref, gt_ref, o_ref, wbf_ref):
    @pl.when(pl.program_id(1) == 0)
    def _():
        wbf_ref[...] = w_ref[...].astype(BF16)

    o_ref[...] = x_ref[...] + gt_ref[0] * _dot(m_ref[...], wbf_ref[...])


def _mm_resid(m, w, x, gt, rows_per_batch, tm=512, tn=512):
    n, k = m.shape
    d = w.shape[1]
    tn = min(tn, d)
    tm = min(tm, n)
    per = rows_per_batch // tm
    return pl.pallas_call(
        _mm_resid_kernel,
        grid=(d // tn, n // tm),
        in_specs=[pl.BlockSpec((tm, k), lambda j, i: (i, 0)),
                  pl.BlockSpec((k, tn), lambda j, i: (0, j)),
                  pl.BlockSpec((tm, tn), lambda j, i: (i, j)),
                  pl.BlockSpec((1, 1, tn), lambda j, i: (i // per, 0, j))],
        out_specs=pl.BlockSpec((tm, tn), lambda j, i: (i, j)),
        out_shape=jax.ShapeDtypeStruct((n, d), jnp.float32),
        scratch_shapes=[pltpu.VMEM((k, tn), BF16)],
        compiler_params=_cparams(("arbitrary", "arbitrary")),
        name="mm_resid",
    )(m, w, x, gt[:, None, :])


def _norm_router_kernel(x_ref, g_ref, sc_ref, sh_ref, wr_ref, th_ref, tl_ref, h_ref, lg_ref):
    last = pl.num_programs(0) - 1

    @pl.when(pl.program_id(0) < last)
    def _():
        x = x_ref[...]
        y = x * lax.rsqrt(jnp.mean(x * x, axis=-1, keepdims=True) + EPS) * g_ref[...]
        h = y * (1.0 + sc_ref[0]) + sh_ref[0]
        h_ref[...] = h.astype(h_ref.dtype)
        lg_ref[...] = _dot_nt(_bf(wr_ref[...]), _bf(h))

    @pl.when(pl.program_id(0) == last)
    def _():
        h_ref[...] = th_ref[...]
        lg_ref[...] = tl_ref[...]


def _norm_router(x, g, sc, sh, w_router_t, tail_h, tail_lg, rows_per_batch, tm):
    n, d = x.shape
    per = rows_per_batch // tm
    nt = n // tm
    assert tail_h.shape == (tm, d) and tail_lg.shape == (N_EXPERTS, tm)
    row = lambda i: jnp.minimum(i, nt - 1)
    return pl.pallas_call(
        _norm_router_kernel,
        grid=(nt + 1,),
        in_specs=[pl.BlockSpec((tm, d), lambda i: (row(i), 0)),
                  pl.BlockSpec((1, d), lambda i: (0, 0)),
                  pl.BlockSpec((1, 1, d), lambda i: (row(i) // per, 0, 0)),
                  pl.BlockSpec((1, 1, d), lambda i: (row(i) // per, 0, 0)),
                  pl.BlockSpec((N_EXPERTS, d), lambda i: (0, 0)),
                  pl.BlockSpec((tm, d), lambda i: (0, 0)),
                  pl.BlockSpec((N_EXPERTS, tm), lambda i: (0, 0))],
        out_specs=[pl.BlockSpec((tm, d), lambda i: (i, 0)),
                   pl.BlockSpec((N_EXPERTS, tm), lambda i: (0, i))],
        out_shape=[jax.ShapeDtypeStruct((n + tm, d), BF16),
                   jax.ShapeDtypeStruct((N_EXPERTS, n + tm), jnp.float32)],
        compiler_params=_cparams(("arbitrary",)),
        name="norm_router",
    )(x, g.reshape(1, d), sc[:, None, :], sh[:, None, :], w_router_t, tail_h, tail_lg)


def _route_kernel(lg_ref, b_ref, eidx_ref, wts_ref, rank_ref, cnt_ref, *, tn):
    shape = (N_GROUPS, GROUP_SIZE, tn)
    sc = _sigmoid(lg_ref[...])
    biased = sc + b_ref[...]
    e_iota = lax.broadcasted_iota(jnp.int32, shape, 1)
    g_iota3 = lax.broadcasted_iota(jnp.int32, shape, 0)
    flat_iota = g_iota3 * GROUP_SIZE + e_iota
    g_iota = lax.broadcasted_iota(jnp.int32, (N_GROUPS, 1, tn), 0)
    ninf = -jnp.inf

    m1 = jnp.max(biased, axis=1, keepdims=True)
    first = jnp.min(jnp.where(biased == m1, e_iota, GROUP_SIZE), axis=1, keepdims=True)
    m2 = jnp.max(jnp.where(e_iota == first, ninf, biased), axis=1, keepdims=True)
    cur = m1 + m2
    gsel = jnp.zeros((N_GROUPS, 1, tn), jnp.float32)
    for _ in range(TOPK_GROUPS):
        mx = jnp.max(cur, axis=0, keepdims=True)
        idx = jnp.min(jnp.where(cur == mx, g_iota, N_GROUPS), axis=0, keepdims=True)
        hit = g_iota == idx
        gsel = jnp.where(hit, 1.0, gsel)
        cur = jnp.where(hit, ninf, cur)

    cur = jnp.where(gsel > 0.0, biased, ninf)
    hits, ws = [], []
    for k in range(TOP_K):
        mx = jnp.max(jnp.max(cur, axis=1, keepdims=True), axis=0, keepdims=True)
        cand = jnp.where(cur == mx, flat_iota, N_EXPERTS)
        idx = jnp.min(jnp.min(cand, axis=1, keepdims=True), axis=0, keepdims=True)
        hit = flat_iota == idx
        w = jnp.sum(jnp.sum(jnp.where(hit, sc, 0.0), axis=1, keepdims=True), axis=0, keepdims=True)
        eidx_ref[k:k + 1, :] = idx.reshape(1, tn)
        hits.append(hit)
        ws.append(w)
        cur = jnp.where(hit, ninf, cur)
    wsum = ws[0]
    for w in ws[1:]:
        wsum = wsum + w
    for k in range(TOP_K):
        wts_ref[k:k + 1, :] = (ws[k] / wsum * ROUTED_SCALE).reshape(1, tn)

    member = jnp.zeros(shape, jnp.float32)
    for hit in hits:
        member = jnp.where(hit, 1.0, member)
    member2 = member.reshape(N_EXPERTS, tn)
    r = lax.broadcasted_iota(jnp.int32, (tn, tn), 0)
    c = lax.broadcasted_iota(jnp.int32, (tn, tn), 1)
    upper = jnp.where(r < c, 1.0, 0.0).astype(BF16)
    prefix = _dot(member2.astype(BF16), upper).reshape(shape)
    for k in range(TOP_K):
        rk = jnp.sum(jnp.sum(jnp.where(hits[k], prefix, 0.0), axis=1, keepdims=True), axis=0, keepdims=True)
        rank_ref[k:k + 1, :] = rk.reshape(1, tn)
    cnt = jnp.sum(member2, axis=1, keepdims=True)
    cnt_ref[0] = jnp.broadcast_to(cnt, (N_EXPERTS, LANES))


def _route(logits_t, b_router, tn):
    n_pad = logits_t.shape[1]
    nt = n_pad // tn
    lg3 = logits_t.reshape(N_GROUPS, GROUP_SIZE, n_pad)
    b3 = b_router.astype(jnp.float32).reshape(N_GROUPS, GROUP_SIZE, 1)
    row = lambda dt: jax.ShapeDtypeStruct((TOP_K, n_pad), dt)
    return pl.pallas_call(
        functools.partial(_route_kernel, tn=tn),
        grid=(nt,),
        in_specs=[pl.BlockSpec((N_GROUPS, GROUP_SIZE, tn), lambda i: (0, 0, i)),
                  pl.BlockSpec((N_GROUPS, GROUP_SIZE, 1), lambda i: (0, 0, 0))],
        out_specs=[pl.BlockSpec((TOP_K, tn), lambda i: (0, i)),
                   pl.BlockSpec((TOP_K, tn), lambda i: (0, i)),
                   pl.BlockSpec((TOP_K, tn), lambda i: (0, i)),
                   pl.BlockSpec((1, N_EXPERTS, LANES), lambda i: (i, 0, 0))],
        out_shape=[row(jnp.int32), row(jnp.float32), row(jnp.float32),
                   jax.ShapeDtypeStruct((nt, N_EXPERTS, LANES), jnp.float32)],
        compiler_params=_cparams(("arbitrary",)),
        name="route",
    )(lg3, b3)


def _dest_kernel(eidx_ref, rank_ref, base_ref, o_ref, *, tn):
    e_iota = lax.broadcasted_iota(jnp.int32, (N_EXPERTS, tn), 0)
    base = base_ref[0][:, :1]
    for k in range(TOP_K):
        onehot = e_iota == eidx_ref[k:k + 1, :]
        b = jnp.sum(jnp.where(onehot, base, 0.0), axis=0, keepdims=True)
        o_ref[k:k + 1, :] = (b + rank_ref[k:k + 1, :]).astype(jnp.int32)


def _dest(eidx, rank, base, tn):
    n_pad = eidx.shape[1]
    return pl.pallas_call(
        functools.partial(_dest_kernel, tn=tn),
        grid=(n_pad // tn,),
        in_specs=[pl.BlockSpec((TOP_K, tn), lambda i: (0, i)),
                  pl.BlockSpec((TOP_K, tn), lambda i: (0, i)),
                  pl.BlockSpec((1, N_EXPERTS, LANES), lambda i: (i, 0, 0))],
        out_specs=pl.BlockSpec((TOP_K, tn), lambda i: (0, i)),
        out_shape=jax.ShapeDtypeStruct((TOP_K, n_pad), jnp.int32),
        compiler_params=_cparams(("arbitrary",)),
        name="dest",
    )(eidx, rank, base)


def _pack_words(lo_f32, hi_f32):
    lo = lax.shift_right_logical(pltpu.bitcast(lo_f32, jnp.uint32), jnp.uint32(16))
    hi = pltpu.bitcast(hi_f32, jnp.uint32) & jnp.uint32(0xFFFF0000)
    return hi | lo


def _unpack_words(w):
    lo = pltpu.bitcast(lax.shift_left(w, jnp.uint32(16)), jnp.float32)
    hi = pltpu.bitcast(w & jnp.uint32(0xFFFF0000), jnp.float32)
    return lo, hi


def _bf16_exact(x):
    return x.astype(BF16).astype(jnp.float32)


def _dispatch_kernel(pstart_ref, pcnt_ref, h_ref, dest_hbm, xs_hbm, dsm, pk, zrow, sem_d, sem_r,
                     *, tn, nt):
    i = pl.program_id(0)
    half = pk.shape[1]

    @pl.when(i < nt)
    def _():
        cp = pltpu.make_async_copy(dest_hbm.at[pl.ds(i * (TOP_K * tn), TOP_K * tn)], dsm, sem_d)
        cp.start()
        x = h_ref[...]
        pk[...] = _pack_words(x[:, :half].astype(jnp.float32), x[:, half:].astype(jnp.float32))
        cp.wait()

        def body(r, carry):
            for k in range(TOP_K):
                d = dsm[k * tn + r]
                pltpu.make_async_copy(pk.at[pl.ds(r, 1), :], xs_hbm.at[pl.ds(d, 1), :],
                                      sem_r).start(priority=k % 2)
            return carry

        lax.fori_loop(0, tn, body, 0)
        for k in range(TOP_K):
            pltpu.make_async_copy(pk, xs_hbm.at[pl.ds(0, tn), :], sem_r).wait()

    @pl.when(i == nt)
    def _():
        zrow[...] = jnp.zeros(zrow.shape, zrow.dtype)

        def per_expert(e, carry):
            s0 = pstart_ref[e]
            c = pcnt_ref[e]
            head = jnp.minimum(c, (8 - s0 % 8) % 8)
            g0 = s0 + head
            ngrp = (c - head) // 8
            grp = lambda r: pl.ds(pl.multiple_of(g0 + r * 8, 8), 8)

            def start(r, cc):
                pltpu.make_async_copy(zrow.at[pl.ds(0, 1), :], xs_hbm.at[pl.ds(s0 + r, 1), :], sem_r).start()
                return cc

            def wait(r, cc):
                pltpu.make_async_copy(zrow.at[pl.ds(0, 1), :], xs_hbm.at[pl.ds(s0, 1), :], sem_r).wait()
                return cc

            def gstart(r, cc):
                pltpu.make_async_copy(zrow, xs_hbm.at[grp(r), :], sem_r).start()
                return cc

            def gwait(r, cc):
                pltpu.make_async_copy(zrow, xs_hbm.at[grp(0), :], sem_r).wait()
                return cc

            lax.fori_loop(0, head, start, 0)
            lax.fori_loop(0, ngrp, gstart, 0)
            lax.fori_loop(0, head, wait, 0)
            lax.fori_loop(0, ngrp, gwait, 0)
            return carry

        lax.fori_loop(0, N_EXPERTS, per_expert, 0)

        t0 = pstart_ref[N_EXPERTS]
        groups = pcnt_ref[N_EXPERTS]
        rows8 = lambda r: pl.ds(pl.multiple_of(t0 + r * 8, 8), 8)

        def tstart(r, cc):
            pltpu.make_async_copy(zrow, xs_hbm.at[rows8(r), :], sem_r).start()
            return cc

        def twait(r, cc):
            pltpu.make_async_copy(zrow, xs_hbm.at[rows8(0), :], sem_r).wait()
            return cc

        lax.fori_loop(0, groups, tstart, 0)
        lax.fori_loop(0, groups, twait, 0)


def _dispatch(h2, dest_flat, pad_start, pad_cnt, n_rows, tn):
    n_pad, d = h2.shape
    nt = n_pad // tn
    grid_spec = pltpu.PrefetchScalarGridSpec(
        num_scalar_prefetch=2,
        grid=(nt + 1,),
        in_specs=[pl.BlockSpec((tn, d), lambda i, a, b: (jnp.minimum(i, nt - 1), 0)),
                  pl.BlockSpec(memory_space=pl.ANY)],
        out_specs=pl.BlockSpec(memory_space=pl.ANY),
        scratch_shapes=[pltpu.SMEM((TOP_K * tn,), jnp.int32),
                        pltpu.VMEM((tn, d // 2), jnp.uint32),
                        pltpu.VMEM((8, d // 2), jnp.uint32),
                        pltpu.SemaphoreType.DMA(()),
                        pltpu.SemaphoreType.DMA(())])
    return pl.pallas_call(
        functools.partial(_dispatch_kernel, tn=tn, nt=nt),
        grid_spec=grid_spec,
        out_shape=jax.ShapeDtypeStruct((n_rows, d // 2), jnp.uint32),
        compiler_params=_cparams(("arbitrary",)),
        name="dispatch",
    )(pad_start, pad_cnt, h2, dest_flat)


def _expert_kernel(blk_e_ref, nused_ref, next_e_ref, slot_ref, x_ref, wg_hbm, wu_hbm, wd_hbm, y_ref,
                   wg_f, wu_f, wd_f, wgb, wub, wdb, sems):
    i = pl.program_id(0)
    nused = nused_ref[0]
    ii = jnp.minimum(i, nused - 1)
    e = blk_e_ref[ii]
    e_prev = blk_e_ref[jnp.maximum(ii - 1, 0)]
    half = x_ref.shape[1]

    def copies(ex, sl):
        return [pltpu.make_async_copy(src.at[ex], dst.at[sl], sems.at[sl, t])
                for t, (src, dst) in enumerate(((wg_hbm, wg_f), (wu_hbm, wu_f), (wd_hbm, wd_f)))]

    @pl.when(i == 0)
    def _():
        for c in copies(e, slot_ref[e]):
            c.start()

    @pl.when((i < nused) & ((i == 0) | (e != e_prev)))
    def _():
        sl = slot_ref[e]
        for c in copies(e, sl):
            c.wait()
        nxt = next_e_ref[e]

        @pl.when(nxt >= 0)
        def _():
            for c in copies(nxt, 1 - sl):
                c.start()

        wgb[...] = wg_f[sl].astype(BF16)
        wub[...] = wu_f[sl].astype(BF16)
        wdb[...] = wd_f[sl].astype(BF16)

    @pl.when(i < nused)
    def _():
        lo, hi = _unpack_words(x_ref[...])
        xl = lo.astype(BF16)
        xh = hi.astype(BF16)
        g = _dot(xl, wgb[:half, :]) + _dot(xh, wgb[half:, :])
        u = _dot(xl, wub[:half, :]) + _dot(xh, wub[half:, :])
        hmid = (_silu(g) * u).astype(BF16)
        y = _dot(hmid, wdb[...])
        y_ref[...] = _pack_words(_bf16_exact(y[:, :half]), _bf16_exact(y[:, half:]))

    @pl.when(i >= nused)
    def _():
        y_ref[...] = jnp.zeros(y_ref.shape, y_ref.dtype)


def _experts(xs, blk_e, nused, next_e, slot, w_gate, w_up, w_down, tb):
    n_rows, half = xs.shape
    _, d, f = w_gate.shape
    nblk = n_rows // tb

    def xmap(i, be, nu, ne, sl):
        return (jnp.minimum(i, nu[0] - 1), 0)

    grid_spec = pltpu.PrefetchScalarGridSpec(
        num_scalar_prefetch=4,
        grid=(nblk,),
        in_specs=[pl.BlockSpec((tb, half), xmap),
                  pl.BlockSpec(memory_space=pl.ANY),
                  pl.BlockSpec(memory_space=pl.ANY),
                  pl.BlockSpec(memory_space=pl.ANY)],
        out_specs=pl.BlockSpec((tb, half), lambda i, be, nu, ne, sl: (i, 0)),
        scratch_shapes=[pltpu.VMEM((2, d, f), jnp.float32),
                        pltpu.VMEM((2, d, f), jnp.float32),
                        pltpu.VMEM((2, f, d), jnp.float32),
                        pltpu.VMEM((d, f), BF16),
                        pltpu.VMEM((d, f), BF16),
                        pltpu.VMEM((f, d), BF16),
                        pltpu.SemaphoreType.DMA((2, 3))])
    return pl.pallas_call(
        _expert_kernel,
        grid_spec=grid_spec,
        out_shape=jax.ShapeDtypeStruct((n_rows, half), jnp.uint32),
        compiler_params=_cparams(("arbitrary",)),
        name="experts",
    )(blk_e, nused, next_e, slot, xs, w_gate, w_up, w_down)


def _ffn_kernel(h_ref, wg_ref, wu_ref, wd_ref, o_ref, wgb, wub, wdb):
    @pl.when(pl.program_id(0) == 0)
    def _():
        wgb[...] = wg_ref[...].astype(BF16)
        wub[...] = wu_ref[...].astype(BF16)
        wdb[...] = wd_ref[...].astype(BF16)

    x = h_ref[...]
    hmid = (_silu(_dot(x, wgb[...])) * _dot(x, wub[...])).astype(BF16)
    o_ref[...] = _dot(hmid, wdb[...]).astype(o_ref.dtype)


def _ffn_shared(h2, wg, wu, wd, tm):
    n_pad, d = h2.shape
    f = wg.shape[1]
    return pl.pallas_call(
        _ffn_kernel,
        grid=(n_pad // tm,),
        in_specs=[pl.BlockSpec((tm, d), lambda i: (i, 0)),
                  pl.BlockSpec((d, f), lambda i: (0, 0)),
                  pl.BlockSpec((d, f), lambda i: (0, 0)),
                  pl.BlockSpec((f, d), lambda i: (0, 0))],
        out_specs=pl.BlockSpec((tm, d), lambda i: (i, 0)),
        out_shape=jax.ShapeDtypeStruct((n_pad, d), BF16),
        scratch_shapes=[pltpu.VMEM((d, f), BF16),
                        pltpu.VMEM((d, f), BF16),
                        pltpu.VMEM((f, d), BF16)],
        compiler_params=_cparams(("arbitrary",)),
        name="ffn_shared",
    )(h2, wg, wu, wd)


def _combine_kernel(dest_hbm, ys_hbm, wts_ref, sh_ref, x_ref, gt_ref, g_ref, o_ref,
                    dsm, buf, sem_d, sem_r, *, tn, tile0, final_norm):
    i = pl.program_id(0)
    cp = pltpu.make_async_copy(dest_hbm.at[pl.ds((tile0 + i) * (TOP_K * tn), TOP_K * tn)], dsm, sem_d)
    cp.start()
    cp.wait()

    def body(r, carry):
        for k in range(TOP_K):
            d = dsm[k * tn + r]
            pltpu.make_async_copy(ys_hbm.at[pl.ds(d, 1), :], buf.at[k, pl.ds(r, 1), :],
                                  sem_r).start(priority=k % 2)
        return carry

    lax.fori_loop(0, tn, body, 0)
    for k in range(TOP_K):
        pltpu.make_async_copy(ys_hbm.at[pl.ds(0, tn), :], buf.at[k], sem_r).wait()

    half = buf.shape[2]
    wts = wts_ref[...]
    acc_lo = jnp.zeros((tn, half), jnp.float32)
    acc_hi = jnp.zeros((tn, half), jnp.float32)
    for k in range(TOP_K):
        lo, hi = _unpack_words(buf[k])
        wk = wts[:, k:k + 1]
        acc_lo = acc_lo + wk * lo
        acc_hi = acc_hi + wk * hi
    sh = sh_ref[...].astype(jnp.float32)
    gt = gt_ref[0]
    x_lo = x_ref[:, :half] + gt[:, :half] * (acc_lo + sh[:, :half])
    x_hi = x_ref[:, half:] + gt[:, half:] * (acc_hi + sh[:, half:])
    if final_norm:
        ms = (jnp.sum(x_lo * x_lo, axis=-1, keepdims=True)
              + jnp.sum(x_hi * x_hi, axis=-1, keepdims=True)) / (2 * half)
        inv = lax.rsqrt(ms + EPS)
        g = g_ref[...]
        x_lo = x_lo * inv * g[:, :half]
        x_hi = x_hi * inv * g[:, half:]
    o_ref[:, :half] = x_lo
    o_ref[:, half:] = x_hi


def _combine(dest_flat, ys, wts_tok, shared, x1, gt3, g_final, tile0, tn, final_norm):
    rows, d = x1.shape
    nt = rows // tn
    gr = gt3.shape[1]
    per = nt // gt3.shape[0]
    return pl.pallas_call(
        functools.partial(_combine_kernel, tn=tn, tile0=tile0, final_norm=final_norm),
        grid=(nt,),
        in_specs=[pl.BlockSpec(memory_space=pl.ANY),
                  pl.BlockSpec(memory_space=pl.ANY),
                  pl.BlockSpec((tn, TOP_K), lambda i: (tile0 + i, 0)),
                  pl.BlockSpec((tn, d), lambda i: (tile0 + i, 0)),
                  pl.BlockSpec((tn, d), lambda i: (i, 0)),
                  pl.BlockSpec((1, gr, d), lambda i: (i // per, 0, 0)),
                  pl.BlockSpec((1, d), lambda i: (0, 0))],
        out_specs=pl.BlockSpec((tn, d), lambda i: (i, 0)),
        out_shape=jax.ShapeDtypeStruct((rows, d), jnp.float32),
        scratch_shapes=[pltpu.SMEM((TOP_K * tn,), jnp.int32),
                        pltpu.VMEM((TOP_K, tn, d // 2), jnp.uint32),
                        pltpu.SemaphoreType.DMA(()),
                        pltpu.SemaphoreType.DMA(())],
        compiler_params=_cparams(("arbitrary",)),
        name="combine",
    )(dest_flat, ys, wts_tok, shared, x1, gt3, g_final.reshape(1, d))


def _page_specs(shape, n, pg):
    def mk(u):
        return pl.BlockSpec((1,) + shape, lambda b, j, pt: (pt[b, j * pg + u], 0, 0))
    return [mk(u) for u in range(n)]


def _kv_page_specs(kshape, vshape, pg, ns):
    def mk(shape, first):
        def one(u):
            def index(b, j, pt):
                step = jnp.minimum(j, ns - 1) if first else jnp.maximum(j - ns, 0)
                return (pt[b, step * pg + u], 0, 0)
            return pl.BlockSpec((1,) + shape, index)
        return [one(u) for u in range(pg)]
    return mk(kshape, True) + mk(vshape, False)


def _softmax_pages(s_ref):
    s = s_ref[...]
    m = jnp.max(jnp.max(s, axis=0, keepdims=True), axis=2, keepdims=True)
    e = jnp.exp(s - m)
    return e / jnp.sum(jnp.sum(e, axis=0, keepdims=True), axis=2, keepdims=True)


def _da_sample_kernel(pt_ref, lam_ref, q_ref, kn_ref, vn_ref, bl_ref, cf_ref, b0_ref, ex_ref, hm_ref,
                      *rest, pg, n_pages):
    k_refs, v_refs = rest[:pg], rest[pg:2 * pg]
    o_ref, s_ref, a_ref, acc_ref = rest[2 * pg:]
    j = pl.program_id(1)
    ns = n_pages // pg
    r = q_ref.shape[1]
    nh = r // 2
    lane = lax.broadcasted_iota(jnp.int32, (r, PAGE_SIZE), 1)

    @pl.when(j < ns)
    def _():
        q = q_ref[0]
        qb = _bf(q)
        for u in range(pg):
            page = j * pg + u
            s = _dot(qb, _bf(k_refs[u][0]))
            s_ref[page] = s + jnp.where(page == (n_pages - 1), bl_ref[...], cf_ref[...])

        @pl.when(j == 0)
        def _():
            s_new = jnp.sum(_rounded(q) * _rounded(kn_ref[0]), axis=-1, keepdims=True) + b0_ref[...]
            s_ref[n_pages] = jnp.where(lane == 0, s_new, NEG)

    @pl.when(j == ns)
    def _():
        p = _softmax_pages(s_ref)
        a = p[:, :nh, :] - lam_ref[0] * p[:, nh:, :]
        a_ref[...] = _bf(jnp.concatenate([a, jnp.zeros_like(a)], axis=1))
        a_new = a_ref[n_pages][:, 0:1].astype(jnp.float32)
        acc_ref[...] = a_new * _rounded(vn_ref[0])

    @pl.when(j >= ns)
    def _():
        acc = acc_ref[...]
        for u in range(pg):
            page = (j - ns) * pg + u
            pe = _bf(_dot(a_ref[page], ex_ref[...]) * hm_ref[...])
            acc = acc + _dot(pe, _bf(v_refs[u][0]))
        acc_ref[...] = acc

    @pl.when(j == pl.num_programs(1) - 1)
    def _():
        o_ref[0] = acc_ref[...]


def _da_sample(page_table, lam, qbd, knew, vnew, bias_last, cfar, bias0, kt, v2, pg):
    nb, n_pages = page_table.shape
    r, w = qbd.shape[1:]
    rows_v, dv = v2.shape[1:]
    nh = rows_v // PAGE_SIZE
    ns = n_pages // pg
    col = jnp.arange(rows_v, dtype=jnp.int32)
    expand = (col[None, :] // nh == jnp.arange(PAGE_SIZE, dtype=jnp.int32)[:, None]).astype(BF16)
    head_mask = (col[None, :] % nh == jnp.arange(r, dtype=jnp.int32)[:, None]).astype(jnp.float32)
    full = lambda shp: pl.BlockSpec(shp, lambda b, j, pt: (0,) * len(shp))
    per_b = lambda shp: pl.BlockSpec((1,) + shp, lambda b, j, pt: (b, 0, 0))
    grid_spec = pltpu.PrefetchScalarGridSpec(
        num_scalar_prefetch=1,
        grid=(nb, 2 * ns),
        in_specs=[pl.BlockSpec(memory_space=pltpu.SMEM),
                  per_b((r, w)), per_b((1, w)), per_b((r, dv)),
                  full((r, PAGE_SIZE)), full((r, 1)), full((r, 1)),
                  full((PAGE_SIZE, rows_v)), full((r, rows_v))]
                 + _kv_page_specs((w, PAGE_SIZE), (rows_v, dv), pg, ns),
        out_specs=per_b((r, dv)),
        scratch_shapes=[pltpu.VMEM((n_pages + 1, r, PAGE_SIZE), jnp.float32),
                        pltpu.VMEM((n_pages + 1, r, PAGE_SIZE), BF16),
                        pltpu.VMEM((r, dv), jnp.float32)])
    return pl.pallas_call(
        functools.partial(_da_sample_kernel, pg=pg, n_pages=n_pages),
        grid_spec=grid_spec,
        out_shape=jax.ShapeDtypeStruct((nb, r, dv), jnp.float32),
        compiler_params=_cparams(("arbitrary", "arbitrary")),
        name="da_sample",
    )(page_table, lam, qbd, knew, vnew, bias_last, cfar, bias0, expand, head_mask,
      *([kt] * pg), *([v2] * pg))


def _idx_sample_kernel(pt_ref, q_ref, w_ref, kn_ref, *rest, pg, n_pages, topk):
    k_refs = rest[:pg]
    sel_ref, sc_ref = rest[pg:]
    j = pl.program_id(1)
    q = q_ref[0]
    w = _rounded(w_ref[0])
    scale = IDX_DK ** -0.5 * IDX_HEADS ** -0.5
    rows = sc_ref.shape[0]
    lane = lax.broadcasted_iota(jnp.int32, (1, PAGE_SIZE), 1)

    @pl.when(j == 0)
    def _():
        sc_ref[...] = jnp.full(sc_ref.shape, -jnp.inf, jnp.float32)
        d = jnp.maximum(jnp.sum(_rounded(q) * _rounded(kn_ref[0]), axis=-1, keepdims=True), 0.0)
        s_new = jnp.sum(w * _rounded(d), axis=0, keepdims=True) * scale
        sc_ref[n_pages:n_pages + 1, :] = jnp.where(lane == 0, s_new, -jnp.inf)

    qb = _bf(q)
    for u in range(pg):
        d = _rounded(jnp.maximum(_dot(qb, _bf(k_refs[u][0])), 0.0))
        sc_ref[pl.ds(j * pg + u, 1), :] = jnp.sum(w * d, axis=0, keepdims=True) * scale

    @pl.when(j == pl.num_programs(1) - 1)
    def _():
        key = _float_key(sc_ref[...])
        pos = (lax.broadcasted_iota(jnp.int32, key.shape, 0) * PAGE_SIZE
               + lax.broadcasted_iota(jnp.int32, key.shape, 1))

        def bit_body(it, thr):
            cand = thr + (jnp.int32(1) << (31 - it))
            cnt = jnp.sum(jnp.where(key >= cand, 1.0, 0.0))
            return jnp.where(cnt >= float(topk), cand, thr)

        thr = lax.fori_loop(0, 32, bit_body, jnp.int32(INT_MIN))
        thr = jnp.maximum(thr, jnp.int32(KEY_NEG_INF + 1))
        need = float(topk) - jnp.sum(jnp.where(key > thr, 1.0, 0.0))
        n_bits = max(1, int(rows * PAGE_SIZE).bit_length())

        def cut_body(it, cut):
            cand = cut + (jnp.int32(1) << (n_bits - 1 - it))
            n_eq = jnp.sum(jnp.where((key == thr) & (pos < cand), 1.0, 0.0))
            return jnp.where(n_eq <= need, cand, cut)

        cut = lax.fori_loop(0, n_bits, cut_body, jnp.int32(0))
        sel_ref[0] = jnp.where((key > thr) | ((key == thr) & (pos < cut)), 1.0, 0.0)


def _idx_sample(page_table, qix, wix, knew, kc, pg, topk):
    nb, n_pages = page_table.shape
    rows = -(-(n_pages + 1) // 8) * 8
    per_b = lambda shp: pl.BlockSpec((1,) + shp, lambda b, j, pt: (b, 0, 0))
    grid_spec = pltpu.PrefetchScalarGridSpec(
        num_scalar_prefetch=1,
        grid=(nb, n_pages // pg),
        in_specs=[per_b((IDX_HEADS, IDX_DK)), per_b((IDX_HEADS, 1)), per_b((1, IDX_DK))]
                 + _page_specs((IDX_DK, PAGE_SIZE), pg, pg),
        out_specs=per_b((rows, PAGE_SIZE)),
        scratch_shapes=[pltpu.VMEM((rows, PAGE_SIZE), jnp.float32)])
    return pl.pallas_call(
        functools.partial(_idx_sample_kernel, pg=pg, n_pages=n_pages, topk=topk),
        grid_spec=grid_spec,
        out_shape=jax.ShapeDtypeStruct((nb, rows, PAGE_SIZE), jnp.float32),
        compiler_params=_cparams(("arbitrary", "arbitrary")),
        name="idx_sample",
    )(page_table, qix, wix, knew, *([kc] * pg))


def _sa_sample_kernel(pt_ref, q_ref, kn_ref, vn_ref, sel_ref, bl_ref, cf_ref, b0_ref, gm_ref, *rest,
                      pg, n_pages):
    k_refs, v_refs = rest[:pg], rest[pg:2 * pg]
    o_ref, s_ref, p_ref, acc_ref = rest[2 * pg:]
    j = pl.program_id(1)
    ns = n_pages // pg
    scale = SA_DH ** -0.5
    r, cols = gm_ref.shape
    lane = lax.broadcasted_iota(jnp.int32, (r, cols), 1)

    @pl.when(j < ns)
    def _():
        q = q_ref[0]
        qb = _bf(q)
        for u in range(pg):
            page = j * pg + u
            s = _dot_nt(qb, _bf(k_refs[u][0])) * scale
            s = s + jnp.where(page == (n_pages - 1), bl_ref[...], cf_ref[...])
            keep = (sel_ref[0, pl.ds(page, 1), :] > 0.0) & (gm_ref[...] > 0.0)
            s_ref[page] = jnp.where(keep, s, NEG)

        @pl.when(j == 0)
        def _():
            on = sel_ref[0, n_pages:n_pages + 1, 0:1] > 0.0
            s_new = jnp.sum(_rounded(q) * _rounded(kn_ref[0]), axis=-1, keepdims=True) * scale + b0_ref[...]
            s_ref[n_pages] = jnp.where((lane == 0) & on, s_new, NEG)

    @pl.when(j == ns)
    def _():
        p_ref[...] = _bf(_softmax_pages(s_ref))
        acc_ref[...] = p_ref[n_pages][:, 0:1].astype(jnp.float32) * _rounded(vn_ref[0])

    @pl.when(j >= ns)
    def _():
        acc = acc_ref[...]
        for u in range(pg):
            acc = acc + _dot(p_ref[(j - ns) * pg + u], _bf(v_refs[u][0]))
        acc_ref[...] = acc

    @pl.when(j == pl.num_programs(1) - 1)
    def _():
        o_ref[0] = acc_ref[...]


def _sa_sample(page_table, q, knew, vnew, sel2, bias_last2, cfar, bias0, group_mask, k2, v2, pg):
    nb, n_pages = page_table.shape
    r, dh = q.shape[1:]
    srows, cols = sel2.shape[1:]
    ns = n_pages // pg
    full = lambda shp: pl.BlockSpec(shp, lambda b, j, pt: (0,) * len(shp))
    per_b = lambda shp: pl.BlockSpec((1,) + shp, lambda b, j, pt: (b, 0, 0))
    grid_spec = pltpu.PrefetchScalarGridSpec(
        num_scalar_prefetch=1,
        grid=(nb, 2 * ns),
        in_specs=[per_b((r, dh)), per_b((r, dh)), per_b((r, dh)), per_b((srows, cols)),
                  full((r, cols)), full((r, 1)), full((r, 1)), full((r, cols))]
                 + _kv_page_specs((cols, dh), (cols, dh), pg, ns),
        out_specs=per_b((r, dh)),
        scratch_shapes=[pltpu.VMEM((n_pages + 1, r, cols), jnp.float32),
                        pltpu.VMEM((n_pages + 1, r, cols), BF16),
                        pltpu.VMEM((r, dh), jnp.float32)])
    return pl.pallas_call(
        functools.partial(_sa_sample_kernel, pg=pg, n_pages=n_pages),
        grid_spec=grid_spec,
        out_shape=jax.ShapeDtypeStruct((nb, r, dh), jnp.float32),
        compiler_params=_cparams(("arbitrary", "arbitrary")),
        name="sa_sample",
    )(page_table, q, knew, vnew, sel2, bias_last2, cfar, bias0, group_mask, *([k2] * pg), *([v2] * pg))


ROW_TILE = 256
COMBINE_TILE = 128
EXPERT_BLOCK = 512
TAIL_ROWS = 256
PAGES_PER_STEP_DA = 16
PAGES_PER_STEP_SA = 32
PAGES_PER_STEP_IDX = 64


def _rms_rows(x, g):
    return x * lax.rsqrt(jnp.mean(x * x, axis=-1, keepdims=True) + EPS) * g


def _pad_rows(x, rows):
    return jnp.pad(x, ((0, rows - x.shape[0]), (0, 0)))


def _tile_major(dest, tn):
    k, n = dest.shape
    return dest.reshape(k, n // tn, tn).transpose(1, 0, 2).reshape(-1)


def kernel(x_prompt, x_sample, c_prompt, c_sample, cache_da_k, cache_da_v, cache_sa_k, cache_sa_v, cache_idx_k, page_table, rel_bias_table, w_ada, b_ada, g_attn, g_ffn, w_in, lambda_q1, lambda_k1, lambda_q2, lambda_k2, g_subln, w_proj_da, w_proj_sa, w_out, w_router, b_router, w_gate, w_up, w_down, w_sh_gate, w_sh_up, w_sh_down, g_final):
    f32, bf16 = jnp.float32, BF16
    nb, t, d = x_prompt.shape
    nbs, dec_seq, _ = x_sample.shape
    assert dec_seq == 1
    depth = w_in.shape[0]
    n_pages = page_table.shape[1]
    past_len = n_pages * PAGE_SIZE
    n = nb * t
    n_pad = n + TAIL_ROWS
    tq = min(256, t)
    assert tq >= MAX_DISTANCE and t % tq == 0 and n % ROW_TILE == 0 and nbs <= TAIL_ROWS
    assert TAIL_ROWS == ROW_TILE and EXPERT_BLOCK % 8 == 0
    topk_p = min(SA_TOPK_MAX, t // 4)
    topk_s = min(SA_TOPK_MAX, (past_len + dec_seq) // 4)
    assert topk_p <= tq

    sizes = [DA_HEADS * 2 * DA_DK, DA_HEADS * 2 * DA_DK, DA_HEADS * DA_DV, SA_HEADS * SA_DH,
             SA_KV_HEADS * SA_DH, SA_KV_HEADS * SA_DH, IDX_HEADS * IDX_DK, IDX_DK, IDX_HEADS, d, d]
    offs = [sum(sizes[:i]) for i in range(len(sizes) + 1)]
    (o_qda, o_kda, o_vda, o_qsa, o_ksa, o_vsa, o_qix, o_kix, o_wix, o_gda, o_gsa, _) = offs
    misc_w = 2 * LANES

    table = rel_bias_table.astype(f32)
    cfar = table[N_BUCKETS - 1]
    tiles_da = _near_tiles(table[:, :DA_HEADS], tq)
    tiles_sa = _near_tiles(table[:, DA_HEADS:], tq)
    last_dist = past_len - ((n_pages - 1) * PAGE_SIZE + jnp.arange(PAGE_SIZE, dtype=jnp.int32))
    bias_last = _bias_by_distance(table, last_dist)
    bias0 = _bias_by_distance(table, jnp.zeros((1,), jnp.int32))
    rep2 = lambda a: jnp.repeat(a, 2, axis=0)

    xp = x_prompt.reshape(n, d)
    xs = x_sample.reshape(nbs, d)
    c_all = jnp.concatenate([c_prompt, c_sample], axis=0)
    leaves_p, leaves_s = [], []
    for l in range(depth):
        lam_init = 0.8 - 0.6 * math.exp(-0.3 * l)
        lam = (jnp.exp(jnp.sum(lambda_q1[l].astype(f32) * lambda_k1[l].astype(f32)))
               - jnp.exp(jnp.sum(lambda_q2[l].astype(f32) * lambda_k2[l].astype(f32))) + lam_init)
        mod = _linear_small(c_all, w_ada[l], b_ada[l], silu_in=True)
        sh1, sc1, gt1, sh2, sc2, gt2 = jnp.split(mod[:nb], 6, axis=-1)
        sh1s, sc1s, gt1s, sh2s, sc2s, gt2s = jnp.split(mod[nb:], 6, axis=-1)
        w = w_in[l]

        h = _norm_mod(xp, g_attn[l], sc1, sh1, t, ROW_TILE)
        q_da, = _mm(h, w, o_qda, sizes[0], (bf16,), scale=DA_DK ** -0.5)
        k_da_b, k_da_t = _mm(h, w, o_kda, sizes[1], (bf16,), rows_per_batch=t)
        k_da = k_da_t.reshape(nb, DA_HEADS, 2, DA_DK, t).transpose(0, 4, 1, 2, 3)
        v_da, v_da_b = _mm(h, w, o_vda, sizes[2], (f32, bf16))
        q_sa, = _mm(h, w, o_qsa, sizes[3], (bf16,), scale=SA_DH ** -0.5)
        k_sa, k_sa_b = _mm(h, w, o_ksa, sizes[4], (f32, bf16))
        v_sa, v_sa_b = _mm(h, w, o_vsa, sizes[5], (f32, bf16))
        q_ix, = _mm(h, w, o_qix, sizes[6], (bf16,), tn=512)
        misc, misc_b = _mm(h, w, o_kix, misc_w, (f32, bf16))
        gates, = _mm(h, w[:, o_gda:], 0, 2 * d, (bf16,), sigmoid=True)
        k_ix = misc[:, :IDX_DK]

        o_da = _da_prompt(q_da, k_da_b, v_da_b, tiles_da, cfar[:DA_HEADS], lam.reshape(1),
                          g_subln[l].astype(f32), nb, t, tq, 1.0 - lam_init)
        o_sa = _sa_prompt(q_sa, q_ix, misc, misc_b, k_sa_b, v_sa_b, tiles_sa, cfar[DA_HEADS:], nb, t, tq, topk_p)
        m = _proj_gate(o_da, o_sa, w_proj_da[l], w_proj_sa[l], gates, d)
        x1 = _mm_resid(m, w_out[l], xp, gt1, t)

        hs = _rms_rows(xs, g_attn[l].astype(f32)) * (1.0 + sc1s) + sh1s
        ps = _linear_small(hs, w)
        seg = lambda i: ps[:, offs[i]:offs[i + 1]]
        q_da_s, k_da_s, v_da_s, q_sa_s, k_sa_s, v_sa_s, q_ix_s, k_ix_s, w_ix_s, gda_s, gsa_s = [
            seg(i) for i in range(11)]

        r_da = 2 * DA_HEADS
        q16 = (q_da_s * DA_DK ** -0.5).reshape(nbs, DA_HEADS, 2, DA_DK).transpose(0, 2, 1, 3).reshape(nbs, r_da, DA_DK)
        blk_of_row = 2 * (jnp.arange(r_da) % DA_HEADS) + jnp.arange(r_da) // DA_HEADS
        place = (blk_of_row[:, None] == jnp.arange(r_da)[None, :]).astype(f32)
        qbd_da = (place[None, :, :, None] * q16[:, :, None, :]).reshape(nbs, r_da, -1)
        both = lambda a: jnp.concatenate([a, a], axis=0)
        kt_da = cache_da_k[l].transpose(0, 2, 3, 4, 1).reshape(-1, DA_HEADS * 2 * DA_DK, PAGE_SIZE)
        v2_da = cache_da_v[l].reshape(-1, PAGE_SIZE * DA_HEADS, DA_DV)
        v_rows = jnp.pad(v_da_s.reshape(nbs, DA_HEADS, DA_DV), ((0, 0), (0, r_da - DA_HEADS), (0, 0)))
        a_da = _da_sample(page_table, lam.reshape(1), qbd_da, k_da_s[:, None, :], v_rows,
                          both(bias_last[:DA_HEADS]), both(cfar[:DA_HEADS, None]), both(bias0[:DA_HEADS]),
                          kt_da, v2_da, math.gcd(PAGES_PER_STEP_DA, n_pages))
        o_da_s = a_da[:, :DA_HEADS]
        o_da_s = _rms_rows(o_da_s, g_subln[l].astype(f32)) * (1.0 - lam_init)

        sel = _idx_sample(page_table, q_ix_s.reshape(nbs, IDX_HEADS, IDX_DK), w_ix_s[:, :, None],
                          k_ix_s[:, None, :], cache_idx_k[l].transpose(0, 2, 1),
                          math.gcd(PAGES_PER_STEP_IDX, n_pages), topk_s)
        r_sa = 2 * SA_HEADS
        kv_of_row = jnp.minimum(jnp.arange(r_sa) // SA_GROUP, SA_KV_HEADS - 1)
        pad_sa = lambda a: jnp.pad(a, ((0, 0), (0, r_sa - SA_HEADS), (0, 0)))
        rep_kv = lambda a: jnp.repeat(a, SA_KV_HEADS, axis=-1)
        col_kv = jnp.arange(PAGE_SIZE * SA_KV_HEADS) % SA_KV_HEADS
        k2_sa = cache_sa_k[l].reshape(-1, PAGE_SIZE * SA_KV_HEADS, SA_DH)
        v2_sa = cache_sa_v[l].reshape(-1, PAGE_SIZE * SA_KV_HEADS, SA_DH)
        a_sa = _sa_sample(page_table, pad_sa(q_sa_s.reshape(nbs, SA_HEADS, SA_DH)),
                          k_sa_s.reshape(nbs, SA_KV_HEADS, SA_DH)[:, kv_of_row],
                          v_sa_s.reshape(nbs, SA_KV_HEADS, SA_DH)[:, kv_of_row],
                          rep_kv(sel), rep_kv(_pad_rows(bias_last[DA_HEADS:], r_sa)),
                          _pad_rows(cfar[DA_HEADS:, None], r_sa), _pad_rows(bias0[DA_HEADS:], r_sa),
                          (col_kv[None, :] == kv_of_row[:, None]).astype(f32),
                          k2_sa, v2_sa, math.gcd(PAGES_PER_STEP_SA, n_pages))
        o_sa_s = a_sa[:, :SA_HEADS]

        pda = _linear_small(o_da_s.reshape(nbs, -1), w_proj_da[l])
        psa = _linear_small(o_sa_s.reshape(nbs, -1), w_proj_sa[l])
        ms = _sigmoid(gda_s) * pda + _sigmoid(gsa_s) * psa
        x1s = xs + gt1s * _linear_small(ms, w_out[l])
        h2s = _rms_rows(x1s, g_ffn[l].astype(f32)) * (1.0 + sc2s) + sh2s
        lg_s = _linear_small(h2s, w_router[l])

        h2_all, lg_all = _norm_router(x1, g_ffn[l], sc2, sh2, w_router[l].T,
                                      _pad_rows(h2s.astype(bf16), TAIL_ROWS),
                                      _pad_rows(lg_s, TAIL_ROWS).T, t, ROW_TILE)
        eidx, wts, rank, cnt = _route(lg_all, b_router[l], ROW_TILE)
        cnt_tile = cnt[:, :, 0]
        total = jnp.sum(cnt_tile, axis=0)
        padded = jnp.ceil(total / EXPERT_BLOCK) * EXPERT_BLOCK
        pends = jnp.cumsum(padded)
        pstart = pends - padded
        base = pstart[None, :] + jnp.cumsum(cnt_tile, axis=0) - cnt_tile
        dest = _dest(eidx, rank, jnp.broadcast_to(base[:, :, None], base.shape + (LANES,)), ROW_TILE)
        n_blk = -(-(n_pad * TOP_K) // EXPERT_BLOCK) + N_EXPERTS
        blk_start = (jnp.arange(n_blk) * EXPERT_BLOCK).astype(f32)
        blk_e = jnp.minimum(jnp.sum(pends[None, :] <= blk_start[:, None], axis=1), N_EXPERTS - 1).astype(jnp.int32)
        n_used = (pends[-1] / EXPERT_BLOCK).astype(jnp.int32).reshape(1)
        n_rows = n_blk * EXPERT_BLOCK
        pad_start = jnp.concatenate([pstart + total, pends[-1:]]).astype(jnp.int32)
        pad_cnt = jnp.concatenate([padded - total, (n_rows - pends[-1:]) / 8]).astype(jnp.int32)
        xs_sorted = _dispatch(h2_all, _tile_major(dest, ROW_TILE), pad_start, pad_cnt, n_rows, ROW_TILE)
        owns = padded > 0
        e_ids = jnp.arange(N_EXPERTS, dtype=jnp.int32)
        later = jnp.where(owns[None, :] & (e_ids[None, :] > e_ids[:, None]), e_ids[None, :], N_EXPERTS)
        next_owner = jnp.min(later, axis=1)
        next_owner = jnp.where(next_owner < N_EXPERTS, next_owner, -1).astype(jnp.int32)
        slot_of_e = ((jnp.cumsum(owns.astype(jnp.int32)) - 1) % 2).astype(jnp.int32)
        ys = _experts(xs_sorted, blk_e, n_used, next_owner, slot_of_e,
                      w_gate[l], w_up[l], w_down[l], EXPERT_BLOCK)
        shared = _ffn_shared(h2_all, w_sh_gate[l], w_sh_up[l], w_sh_down[l], ROW_TILE)
        dest_c = _tile_major(dest, COMBINE_TILE)
        wts_tok = wts.T
        last = l == depth - 1
        xp = _combine(dest_c, ys, wts_tok, shared, x1, gt2[:, None, :], g_final.astype(f32),
                      0, COMBINE_TILE, last)
        tail = _combine(dest_c, ys, wts_tok, shared, _pad_rows(x1s, TAIL_ROWS),
                        _pad_rows(gt2s, TAIL_ROWS).reshape(-1, COMBINE_TILE, d), g_final.astype(f32),
                        n // COMBINE_TILE, COMBINE_TILE, last)
        xs = tail[:nbs]
        leaves_p.append((k_da, v_da, k_sa, v_sa, k_ix))
        leaves_s.append((k_da_s, v_da_s, k_sa_s, v_sa_s, k_ix_s))

    shapes = [(DA_HEADS, 2, DA_DK), (DA_HEADS, DA_DV), (SA_KV_HEADS, SA_DH), (SA_KV_HEADS, SA_DH), (IDX_DK,)]
    out_p = [jnp.stack([lv[i].reshape((nb, t) + shapes[i]) for lv in leaves_p]) for i in range(5)]
    out_s = [jnp.stack([lv[i].reshape((nbs, dec_seq) + shapes[i]) for lv in leaves_s]) for i in range(5)]
    return (xp.reshape(nb, t, d), xs.reshape(nbs, dec_seq, d), *out_p, *out_s)
```

```python
import functools
import math

import jax
import jax.numpy as jnp
from jax import lax
from jax.experimental import pallas as pl
from jax.experimental.pallas import tpu as pltpu

DA_HEADS = 8
DA_DK = 64
DA_DV = 2 * DA_DK
SA_HEADS = 8
SA_KV_HEADS = 2
SA_DH = 128
SA_GROUP = SA_HEADS // SA_KV_HEADS
IDX_HEADS = 16
IDX_DK = 64
SA_TOPK_MAX = 256
N_BUCKETS = 32
MAX_DISTANCE = 128
N_EXPERTS = 64
N_GROUPS = 8
GROUP_SIZE = N_EXPERTS // N_GROUPS
TOPK_GROUPS = 4
TOP_K = 8
ROUTED_SCALE = 2.5
PAGE_SIZE = 128
EPS = 1e-6

LANES = 128
VMEM_LIMIT = 56 * 1024 * 1024

BF16 = jnp.bfloat16
NEG = -1e30
INT_MIN = -(2 ** 31)
KEY_NEG_INF = (0xFF800000 ^ 0x7FFFFFFF) - (1 << 32)

_NT = (((1,), (1,)), ((), ()))


def _cparams(sem):
    return pltpu.CompilerParams(dimension_semantics=sem, vmem_limit_bytes=VMEM_LIMIT)


def _dot(a, b):
    return jnp.dot(a, b, preferred_element_type=jnp.float32)


def _dot_nt(a, b):
    return lax.dot_general(a, b, _NT, preferred_element_type=jnp.float32)


def _bf(x):
    return x.astype(BF16)


def _rounded(x):
    return x.astype(BF16).astype(jnp.float32)


def _sigmoid(x):
    return 1.0 / (1.0 + jnp.exp(-x))


def _silu(x):
    return x * _sigmoid(x)


def _float_key(s):
    b = pltpu.bitcast(s, jnp.int32)
    return b ^ ((b >> 31) & jnp.int32(0x7FFFFFFF))


def _linear_small_kernel(x_ref, w_ref, b_ref, o_ref, *, silu_in):
    x = x_ref[...]
    if silu_in:
        x = _silu(x)
    o_ref[...] = _dot(_bf(x), _bf(w_ref[...])) + b_ref[...]


def _linear_small(x, w, b=None, *, silu_in=False, tn=512):
    m0, k = x.shape
    m = -(-m0 // 16) * 16
    x = jnp.pad(x, ((0, m - m0), (0, 0)))
    n = w.shape[1]
    tn = min(tn, n)
    if b is None:
        b = jnp.zeros((1, n), jnp.float32)
    out = pl.pallas_call(
        functools.partial(_linear_small_kernel, silu_in=silu_in),
        grid=(pl.cdiv(n, tn),),
        in_specs=[pl.BlockSpec((m, k), lambda j: (0, 0)),
                  pl.BlockSpec((k, tn), lambda j: (0, j)),
                  pl.BlockSpec((1, tn), lambda j: (0, j))],
        out_specs=pl.BlockSpec((m, tn), lambda j: (0, j)),
        out_shape=jax.ShapeDtypeStruct((m, n), jnp.float32),
        compiler_params=_cparams(("arbitrary",)),
        name="linear_small",
    )(x, w, b.reshape(1, n))
    return out[:m0]


def _norm_mod_kernel(x_ref, g_ref, sc_ref, sh_ref, o_ref):
    x = x_ref[...]
    y = x * lax.rsqrt(jnp.mean(x * x, axis=-1, keepdims=True) + EPS) * g_ref[...]
    o_ref[...] = (y * (1.0 + sc_ref[0]) + sh_ref[0]).astype(o_ref.dtype)


def _norm_mod(x, g, sc, sh, rows_per_batch, tm):
    n, d = x.shape
    per = rows_per_batch // tm
    return pl.pallas_call(
        _norm_mod_kernel,
        grid=(n // tm,),
        in_specs=[pl.BlockSpec((tm, d), lambda i: (i, 0)),
                  pl.BlockSpec((1, d), lambda i: (0, 0)),
                  pl.BlockSpec((1, 1, d), lambda i: (i // per, 0, 0)),
                  pl.BlockSpec((1, 1, d), lambda i: (i // per, 0, 0))],
        out_specs=pl.BlockSpec((tm, d), lambda i: (i, 0)),
        out_shape=jax.ShapeDtypeStruct((n, d), BF16),
        compiler_params=_cparams(("arbitrary",)),
        name="norm_mod",
    )(x, g.reshape(1, d), sc[:, None, :], sh[:, None, :])


def _mm_kernel(x_ref, w_ref, *rest, scale, sigmoid, n_out, transposed_out):
    o_refs, wbf_ref = rest[:n_out], rest[-1]

    @pl.when(pl.program_id(1) == 0)
    def _():
        wbf_ref[...] = w_ref[...].astype(BF16)

    acc = _dot(x_ref[...], wbf_ref[...])
    if scale != 1.0:
        acc = acc * scale
    if sigmoid:
        acc = _sigmoid(acc)
    for o in o_refs:
        o[...] = acc.astype(o.dtype)
    if transposed_out:
        rest[n_out][0] = acc.T


def _mm(x, w, col0, ncols, out_dtypes, *, scale=1.0, sigmoid=False, tm=1024, tn=1024, rows_per_batch=None):
    m, k = x.shape
    tn = min(tn, ncols)
    tm = min(tm, m)
    assert col0 % tn == 0 and ncols % tn == 0 and m % tm == 0
    jb = col0 // tn
    out_specs = [pl.BlockSpec((tm, tn), lambda j, i: (i, j)) for _ in out_dtypes]
    out_shape = [jax.ShapeDtypeStruct((m, ncols), dt) for dt in out_dtypes]
    if rows_per_batch is not None:
        per = rows_per_batch // tm
        out_specs.append(pl.BlockSpec((1, tn, tm), lambda j, i: (i // per, j, i % per)))
        out_shape.append(jax.ShapeDtypeStruct((m // rows_per_batch, ncols, rows_per_batch), jnp.float32))
    outs = pl.pallas_call(
        functools.partial(_mm_kernel, scale=scale, sigmoid=sigmoid, n_out=len(out_dtypes),
                          transposed_out=rows_per_batch is not None),
        grid=(ncols // tn, m // tm),
        in_specs=[pl.BlockSpec((tm, k), lambda j, i: (i, 0)),
                  pl.BlockSpec((k, tn), lambda j, i: (0, jb + j))],
        out_specs=out_specs,
        out_shape=out_shape,
        scratch_shapes=[pltpu.VMEM((k, tn), BF16)],
        compiler_params=_cparams(("arbitrary", "arbitrary")),
        name="mm_cols",
    )(x, w)
    return outs


def _rel_bucket(dist):
    max_exact = N_BUCKETS // 2
    d = jnp.maximum(dist, 0)
    large = max_exact + (jnp.log(jnp.maximum(d, 1).astype(jnp.float32) / max_exact)
                         / math.log(MAX_DISTANCE / max_exact)
                         * (N_BUCKETS - max_exact)).astype(jnp.int32)
    large = jnp.minimum(large, N_BUCKETS - 1)
    return jnp.where(d < max_exact, d, large)


def _bias_by_distance(table, dists):
    return table[_rel_bucket(dists)].astype(jnp.float32).T


def _toeplitz_kernel(u_ref, o_ref):
    t = o_ref.shape[2]
    x = jnp.broadcast_to(u_ref[0], (t, 2 * t))
    o_ref[0, 0] = pltpu.roll(x, 0, 1, stride=1, stride_axis=0)[:, :t]


def _near_tiles(table, t):
    nh = table.shape[1]
    k = jnp.arange(2 * t, dtype=jnp.int32)
    gens = []
    for off in (0, t):
        d = jnp.where(k < t, off - k, off + 2 * t - k)
        gens.append(jnp.where(d[None] >= 0, _bias_by_distance(table, d), NEG))
    u = jnp.stack(gens, axis=1).reshape(nh * 2, 1, 2 * t)
    return pl.pallas_call(
        _toeplitz_kernel,
        grid=(nh, 2),
        in_specs=[pl.BlockSpec((1, 1, 2 * t), lambda h, o: (h * 2 + o, 0, 0))],
        out_specs=pl.BlockSpec((1, 1, t, t), lambda h, o: (h, o, 0, 0)),
        out_shape=jax.ShapeDtypeStruct((nh, 2, t, t), jnp.float32),
        compiler_params=_cparams(("arbitrary", "arbitrary")),
        name="bias_tiles",
    )(u)


def _fold_lanes(x, op):
    out = x[:, :LANES]
    for c in range(1, x.shape[1] // LANES):
        out = op(out, x[:, c * LANES:(c + 1) * LANES])
    return out


def _chunk_loop(n, fn):
    def body(i, carry):
        for u in range(4):
            fn(4 * i + u)
        return carry

    lax.fori_loop(0, n // 4, body, 0)
    base = (n // 4) * 4

    @pl.when(n % 4 >= 2)
    def _():
        fn(base)
        fn(base + 1)

    @pl.when(n % 2 == 1)
    def _():
        fn(n - 1)


def _da_prompt_kernel(cfar_ref, lam_ref, q_ref, k_ref, v_ref, tile_ref, g_ref, o_ref,
                      s_ref, mpart_ref, shift_ref, lpart_ref, acc_ref, *, tq, out_scale):
    h = pl.program_id(1)
    qi = pl.program_id(2)
    q = q_ref[...]
    lane = lax.broadcasted_iota(jnp.int32, q.shape, 1)
    zero = jnp.zeros_like(q)
    q2 = jnp.concatenate([jnp.where(lane < DA_DK, q, zero), jnp.where(lane >= DA_DK, q, zero)], axis=0)
    cfar = cfar_ref[h]
    n_far = jnp.maximum(qi - 1, 0)
    r2 = 2 * tq

    def chunk_rows(kc):
        return pl.ds(pl.multiple_of(kc * tq, tq), tq)

    def scores(kc, bias):
        s = _dot_nt(q2, k_ref[chunk_rows(kc), :])
        if bias is not None:
            s = s + jnp.concatenate([bias, bias], axis=0)
        s_ref[kc] = s
        mpart_ref[...] = jnp.maximum(mpart_ref[...], _fold_lanes(s, jnp.maximum))

    mpart_ref[...] = jnp.full((r2, LANES), NEG, jnp.float32)
    _chunk_loop(n_far, lambda kc: scores(kc, None))
    m_far = jnp.max(mpart_ref[...], axis=-1, keepdims=True) + cfar
    mpart_ref[...] = jnp.full((r2, LANES), NEG, jnp.float32)

    @pl.when(qi >= 1)
    def _():
        scores(qi - 1, tile_ref[0, 1])
        scores(qi, tile_ref[0, 0])

    @pl.when(qi == 0)
    def _():
        scores(0, tile_ref[0, 0])

    m = jnp.maximum(m_far, jnp.max(mpart_ref[...], axis=-1, keepdims=True))
    shift_ref[0] = jnp.broadcast_to(m - cfar, (r2, LANES))
    shift_ref[1] = jnp.broadcast_to(m, (r2, LANES))

    lpart_ref[...] = jnp.zeros((r2, LANES), jnp.float32)
    acc_ref[...] = jnp.zeros((r2, DA_DV), jnp.float32)

    def weights(kc, which):
        s = s_ref[kc]
        sh = shift_ref[which]
        ps = [jnp.exp(s[:, c * LANES:(c + 1) * LANES] - sh) for c in range(tq // LANES)]
        tot = ps[0]
        for pc in ps[1:]:
            tot = tot + pc
        lpart_ref[...] = lpart_ref[...] + tot
        p = jnp.concatenate(ps, axis=1).astype(BF16)
        acc_ref[...] = acc_ref[...] + _dot(p, v_ref[chunk_rows(kc), :])

    _chunk_loop(n_far, lambda kc: weights(kc, 0))

    @pl.when(qi >= 1)
    def _():
        weights(qi - 1, 1)
        weights(qi, 1)

    @pl.when(qi == 0)
    def _():
        weights(0, 1)

    lam = lam_ref[0]
    a = acc_ref[...] / jnp.sum(lpart_ref[...], axis=-1, keepdims=True)
    o = a[:tq] - lam * a[tq:]
    o = o * lax.rsqrt(jnp.mean(o * o, axis=-1, keepdims=True) + EPS) * g_ref[...]
    o_ref[...] = (o * out_scale).astype(o_ref.dtype)


def _da_prompt(q, k, v, tiles, cfar, lam, g_subln, nb, t, tq, out_scale):
    n = q.shape[0]
    nq = t // tq
    grid_spec = pltpu.PrefetchScalarGridSpec(
        num_scalar_prefetch=0,
        grid=(nb, DA_HEADS, nq),
        in_specs=[pl.BlockSpec(memory_space=pltpu.SMEM),
                  pl.BlockSpec(memory_space=pltpu.SMEM),
                  pl.BlockSpec((tq, LANES), lambda b, h, i: (b * nq + i, h)),
                  pl.BlockSpec((t, LANES), lambda b, h, i: (b, h)),
                  pl.BlockSpec((t, LANES), lambda b, h, i: (b, h)),
                  pl.BlockSpec((1, 2, tq, tq), lambda b, h, i: (h, 0, 0, 0)),
                  pl.BlockSpec((1, DA_DV), lambda b, h, i: (0, 0))],
        out_specs=pl.BlockSpec((tq, LANES), lambda b, h, i: (b * nq + i, h)),
        scratch_shapes=[pltpu.VMEM((nq, 2 * tq, tq), jnp.float32),
                        pltpu.VMEM((2 * tq, LANES), jnp.float32),
                        pltpu.VMEM((2, 2 * tq, LANES), jnp.float32),
                        pltpu.VMEM((2 * tq, LANES), jnp.float32),
                        pltpu.VMEM((2 * tq, DA_DV), jnp.float32)])
    return pl.pallas_call(
        functools.partial(_da_prompt_kernel, tq=tq, out_scale=out_scale),
        grid_spec=grid_spec,
        out_shape=jax.ShapeDtypeStruct((n, DA_HEADS * DA_DV), BF16),
        compiler_params=_cparams(("arbitrary", "arbitrary", "arbitrary")),
        name="da_prompt",
    )(cfar, lam, q, k, v, tiles, g_subln.reshape(1, DA_DV))


def _sa_prompt_kernel(cfar_ref, qs_ref, qx_ref, mq_ref, mk_ref, ks_ref, vs_ref, tile_ref, o_ref,
                      k2_ref, key_ref, hi_ref, lo_ref, cut_ref, s_ref, mpart_ref, shift_ref, lpart_ref,
                      acc_ref, *, tq, topk):
    qi = pl.program_id(1)
    n_chunks = qi + 1
    t = mk_ref.shape[0]

    @pl.when(qi == 0)
    def _():
        kix = mk_ref[:, :LANES].astype(jnp.float32)
        lane = lax.broadcasted_iota(jnp.int32, kix.shape, 1)
        k2_ref[0] = jnp.where(lane < IDX_DK, kix, 0.0).astype(BF16)
        k2_ref[1] = jnp.where(lane >= IDX_DK, pltpu.roll(kix, IDX_DK, axis=1), 0.0).astype(BF16)

    wix = mq_ref[:, IDX_DK:IDX_DK + IDX_HEADS]
    wcols = [wix[:, hh:hh + 1] for hh in range(IDX_HEADS)]
    row = lax.broadcasted_iota(jnp.int32, (tq, tq), 0)
    col = lax.broadcasted_iota(jnp.int32, (tq, tq), 1)

    def score_body(kc, carry):
        rows = pl.ds(pl.multiple_of(kc * tq, tq), tq)
        ke = k2_ref[0, rows, :]
        ko = k2_ref[1, rows, :]
        sc = jnp.zeros((tq, tq), jnp.float32)
        for p in range(IDX_HEADS // 2):
            qp = qx_ref[:, p * LANES:(p + 1) * LANES]
            sc = sc + wcols[2 * p] * jnp.maximum(_dot_nt(qp, ke), 0.0)
            sc = sc + wcols[2 * p + 1] * jnp.maximum(_dot_nt(qp, ko), 0.0)
        sc = sc * (IDX_DK ** -0.5 * IDX_HEADS ** -0.5)
        sc = jnp.where((kc < qi) | (row >= col), sc, -jnp.inf)
        key = _float_key(sc)
        key_ref[kc] = key
        hi_ref[kc] = (key >> 16).astype(jnp.int16)
        return carry

    lax.fori_loop(0, n_chunks, score_body, 0)

    i16 = jnp.int16
    i16_min = -(2 ** 15)
    one_i = jnp.ones((tq, tq), i16)
    zero_i = jnp.zeros((tq, tq), i16)
    ones_col = jnp.ones((tq, LANES), BF16)

    def wide16(x):
        return jnp.concatenate([x] * (tq // LANES), axis=1).astype(i16)

    def count_ge16(ref16, cand):
        c16 = wide16(cand)

        def body(kc, acc):
            return acc + jnp.where(ref16[kc] >= c16, one_i, zero_i)
        acc = lax.fori_loop(0, n_chunks, body, zero_i)
        return _dot(acc.astype(jnp.float32).astype(BF16), ones_col)

    def search16(ref16, base, need):
        def bit_body(it, carry):
            v, cnt_v = carry
            cand = v + (jnp.int32(1) << (15 - it))
            cnt = base + count_ge16(ref16, cand)
            ok = cnt >= need
            return jnp.where(ok, cand, v), jnp.where(ok, cnt, cnt_v)
        v0 = jnp.full((tq, LANES), i16_min, jnp.int32)
        c0 = jnp.full((tq, LANES), 3.0e38, jnp.float32)
        return lax.fori_loop(0, 16, bit_body, (v0, c0))

    zero_cnt = jnp.zeros((tq, LANES), jnp.float32)
    t_hi, _ = search16(hi_ref, zero_cnt, float(topk))
    n_above = jnp.where(t_hi < 2 ** 15 - 1, count_ge16(hi_ref, jnp.minimum(t_hi + 1, 2 ** 15 - 1)), 0.0)
    t_hi16 = wide16(t_hi)

    def lower_body(kc, carry):
        lo = ((key_ref[kc] & jnp.int32(0xFFFF)) - 2 ** 15).astype(i16)
        lo_ref[kc] = jnp.where(hi_ref[kc] == t_hi16, lo, jnp.full((tq, tq), i16_min, i16))
        return carry

    lax.fori_loop(0, n_chunks, lower_body, 0)
    t_lo, cnt_w = search16(lo_ref, n_above, float(topk))
    thr = ((t_hi << 16) + (t_lo + 2 ** 15))[:, :1]
    cnt_thr = cnt_w[:, :1]
    tied = (cnt_thr > float(topk)) & (thr > jnp.int32(KEY_NEG_INF))
    need_tie = jnp.max(jnp.where(tied, 1.0, 0.0)) > 0.0
    thr = jnp.maximum(thr, jnp.int32(KEY_NEG_INF + 1))

    cut_ref[...] = jnp.full((tq, 1), 2 ** 30, jnp.int32)

    @pl.when(need_tie)
    def _():
        def gt_body(kc, acc):
            g = jnp.where(key_ref[kc] > thr, 1.0, 0.0)
            return acc + jnp.sum(g, axis=-1, keepdims=True)
        n_gt = lax.fori_loop(0, n_chunks, gt_body, jnp.zeros((tq, 1), jnp.float32))
        need = float(topk) - n_gt
        n_bits = max(1, int(t).bit_length())

        def cut_body(it, cut):
            cand = cut + (jnp.int32(1) << (n_bits - 1 - it))

            def eq_body(kc, acc):
                pos = kc * tq + col
                e = jnp.where((key_ref[kc] == thr) & (pos < cand), 1.0, 0.0)
                return acc + jnp.sum(e, axis=-1, keepdims=True)
            n_eq = lax.fori_loop(0, n_chunks, eq_body, jnp.zeros((tq, 1), jnp.float32))
            return jnp.where(n_eq <= need, cand, cut)
        cut_ref[...] = lax.fori_loop(0, n_bits, cut_body, jnp.zeros((tq, 1), jnp.int32))

    cut = cut_ref[...]

    n_far = jnp.maximum(qi - 1, 0)
    rg = SA_GROUP * tq

    def chunk_rows(kc):
        return pl.ds(pl.multiple_of(kc * tq, tq), tq)

    for g in range(SA_KV_HEADS):
        heads = [g * SA_GROUP + j for j in range(SA_GROUP)]
        qg = jnp.concatenate([qs_ref[:, hh * LANES:(hh + 1) * LANES] for hh in heads], axis=0)
        cf_rows = jnp.concatenate([jnp.full((tq, 1), cfar_ref[hh], jnp.float32) for hh in heads], axis=0)

        def scores(kc, kind, qg=qg, heads=heads, g=g):
            key = key_ref[kc]
            sel = (key > thr) | ((key == thr) & (kc * tq + col < cut))
            s_all = _dot_nt(qg, ks_ref[chunk_rows(kc), g * SA_DH:(g + 1) * SA_DH])
            parts = []
            for j, hh in enumerate(heads):
                s = s_all[j * tq:(j + 1) * tq]
                if kind is not None:
                    s = s + tile_ref[hh, kind]
                parts.append(jnp.where(sel, s, NEG))
            s = jnp.concatenate(parts, axis=0)
            s_ref[kc] = s
            mpart_ref[...] = jnp.maximum(mpart_ref[...], _fold_lanes(s, jnp.maximum))

        mpart_ref[...] = jnp.full((rg, LANES), NEG, jnp.float32)
        _chunk_loop(n_far, lambda kc, f=scores: f(kc, None))
        m_far = jnp.max(mpart_ref[...], axis=-1, keepdims=True) + cf_rows
        mpart_ref[...] = jnp.full((rg, LANES), NEG, jnp.float32)

        @pl.when(qi >= 1)
        def _(f=scores):
            f(qi - 1, 1)
            f(qi, 0)

        @pl.when(qi == 0)
        def _(f=scores):
            f(0, 0)

        m = jnp.maximum(m_far, jnp.max(mpart_ref[...], axis=-1, keepdims=True))
        shift_ref[0] = jnp.broadcast_to(m - cf_rows, (rg, LANES))
        shift_ref[1] = jnp.broadcast_to(m, (rg, LANES))
        lpart_ref[...] = jnp.zeros((rg, LANES), jnp.float32)
        acc_ref[...] = jnp.zeros((rg, SA_DH), jnp.float32)

        def weights(kc, which, g=g):
            s = s_ref[kc]
            sh = shift_ref[which]
            ps = [jnp.exp(s[:, c * LANES:(c + 1) * LANES] - sh) for c in range(tq // LANES)]
            tot = ps[0]
            for pc in ps[1:]:
                tot = tot + pc
            lpart_ref[...] = lpart_ref[...] + tot
            p = jnp.concatenate(ps, axis=1).astype(BF16)
            acc_ref[...] = acc_ref[...] + _dot(p, vs_ref[chunk_rows(kc), g * SA_DH:(g + 1) * SA_DH])

        _chunk_loop(n_far, lambda kc, f=weights: f(kc, 0))

        @pl.when(qi >= 1)
        def _(f=weights):
            f(qi - 1, 1)
            f(qi, 1)

        @pl.when(qi == 0)
        def _(f=weights):
            f(0, 1)

        a = acc_ref[...] / jnp.sum(lpart_ref[...], axis=-1, keepdims=True)
        for j, hh in enumerate(heads):
            o_ref[:, hh * SA_DH:(hh + 1) * SA_DH] = a[j * tq:(j + 1) * tq].astype(o_ref.dtype)


def _sa_prompt(q_sa, q_ix, misc_q, misc_k, k_sa, v_sa, tiles, cfar, nb, t, tq, topk):
    n = q_sa.shape[0]
    nq = t // tq
    mw = misc_q.shape[1]
    kvw = SA_KV_HEADS * SA_DH
    once = dict(pipeline_mode=pl.Buffered(1))
    rg = SA_GROUP * tq
    return pl.pallas_call(
        functools.partial(_sa_prompt_kernel, tq=tq, topk=topk),
        grid=(nb, nq),
        in_specs=[pl.BlockSpec(memory_space=pltpu.SMEM),
                  pl.BlockSpec((tq, SA_HEADS * SA_DH), lambda b, i: (b * nq + i, 0)),
                  pl.BlockSpec((tq, IDX_HEADS * IDX_DK), lambda b, i: (b * nq + i, 0)),
                  pl.BlockSpec((tq, mw), lambda b, i: (b * nq + i, 0)),
                  pl.BlockSpec((t, mw), lambda b, i: (b, 0), **once),
                  pl.BlockSpec((t, kvw), lambda b, i: (b, 0), **once),
                  pl.BlockSpec((t, kvw), lambda b, i: (b, 0), **once),
                  pl.BlockSpec((SA_HEADS, 2, tq, tq), lambda b, i: (0, 0, 0, 0), **once)],
        out_specs=pl.BlockSpec((tq, SA_HEADS * SA_DH), lambda b, i: (b * nq + i, 0)),
        out_shape=jax.ShapeDtypeStruct((n, SA_HEADS * SA_DH), BF16),
        scratch_shapes=[pltpu.VMEM((2, t, LANES), BF16),
                        pltpu.VMEM((nq, tq, tq), jnp.int32),
                        pltpu.VMEM((nq, tq, tq), jnp.int16),
                        pltpu.VMEM((nq, tq, tq), jnp.int16),
                        pltpu.VMEM((tq, 1), jnp.int32),
                        pltpu.VMEM((nq, rg, tq), jnp.float32),
                        pltpu.VMEM((rg, LANES), jnp.float32),
                        pltpu.VMEM((2, rg, LANES), jnp.float32),
                        pltpu.VMEM((rg, LANES), jnp.float32),
                        pltpu.VMEM((rg, SA_DH), jnp.float32)],
        compiler_params=_cparams(("arbitrary", "arbitrary")),
        name="sa_prompt",
    )(cfar, q_sa, q_ix, misc_q, misc_k, k_sa, v_sa, tiles)


def _proj_gate_kernel(oda_ref, osa_ref, wpd_ref, wps_ref, gda_ref, gsa_ref, o_ref, wbf_ref):
    @pl.when(pl.program_id(1) == 0)
    def _():
        wbf_ref[0] = wpd_ref[...].astype(BF16)
        wbf_ref[1] = wps_ref[...].astype(BF16)

    a = _dot(oda_ref[...], wbf_ref[0])
    b = _dot(osa_ref[...], wbf_ref[1])
    o_ref[...] = (gda_ref[...].astype(jnp.float32) * a
                  + gsa_ref[...].astype(jnp.float32) * b).astype(o_ref.dtype)


def _proj_gate(o_da, o_sa, w_pd, w_ps, gates, d, tm=1024, tn=512):
    n, kd = o_da.shape
    ks = o_sa.shape[1]
    tn = min(tn, d)
    tm = min(tm, n)
    nj = d // tn
    return pl.pallas_call(
        _proj_gate_kernel,
        grid=(nj, n // tm),
        in_specs=[pl.BlockSpec((tm, kd), lambda j, i: (i, 0)),
                  pl.BlockSpec((tm, ks), lambda j, i: (i, 0)),
                  pl.BlockSpec((kd, tn), lambda j, i: (0, j)),
                  pl.BlockSpec((ks, tn), lambda j, i: (0, j)),
                  pl.BlockSpec((tm, tn), lambda j, i: (i, j)),
                  pl.BlockSpec((tm, tn), lambda j, i: (i, nj + j))],
        out_specs=pl.BlockSpec((tm, tn), lambda j, i: (i, j)),
        out_shape=jax.ShapeDtypeStruct((n, d), BF16),
        scratch_shapes=[pltpu.VMEM((2, kd, tn), BF16)],
        compiler_params=_cparams(("arbitrary", "arbitrary")),
        name="proj_gate",
    )(o_da, o_sa, w_pd, w_ps, gates, gates)


def _mm_resid_kernel(m_ref, w_ref, x_ref, gt_ref, o_ref, wbf_ref):
    @pl.when(pl.program_id(1) == 0)
    def _():
        wbf_ref[...] = w_ref[...].astype(BF16)

    o_ref[...] = x_ref[...] + gt_ref[0] * _dot(m_ref[...], wbf_ref[...])


def _mm_resid(m, w, x, gt, rows_per_batch, tm=1024, tn=1024):
    n, k = m.shape
    d = w.shape[1]
    tn = min(tn, d)
    tm = min(tm, n)
    per = rows_per_batch // tm
    return pl.pallas_call(
        _mm_resid_kernel,
        grid=(d // tn, n // tm),
        in_specs=[pl.BlockSpec((tm, k), lambda j, i: (i, 0)),
                  pl.BlockSpec((k, tn), lambda j, i: (0, j)),
                  pl.BlockSpec((tm, tn), lambda j, i: (i, j)),
                  pl.BlockSpec((1, 1, tn), lambda j, i: (i // per, 0, j))],
        out_specs=pl.BlockSpec((tm, tn), lambda j, i: (i, j)),
        out_shape=jax.ShapeDtypeStruct((n, d), jnp.float32),
        scratch_shapes=[pltpu.VMEM((k, tn), BF16)],
        compiler_params=_cparams(("arbitrary", "arbitrary")),
        name="mm_resid",
    )(m, w, x, gt[:, None, :])


def _norm_router_kernel(x_ref, g_ref, sc_ref, sh_ref, wr_ref, th_ref, tl_ref, h_ref, lg_ref):
    last = pl.num_programs(0) - 1

    @pl.when(pl.program_id(0) < last)
    def _():
        x = x_ref[...]
        y = x * lax.rsqrt(jnp.mean(x * x, axis=-1, keepdims=True) + EPS) * g_ref[...]
        h = y * (1.0 + sc_ref[0]) + sh_ref[0]
        h_ref[...] = h.astype(h_ref.dtype)
        lg_ref[...] = _dot_nt(_bf(wr_ref[...]), _bf(h))

    @pl.when(pl.program_id(0) == last)
    def _():
        h_ref[...] = th_ref[...]
        lg_ref[...] = tl_ref[...]


def _norm_router(x, g, sc, sh, w_router_t, tail_h, tail_lg, rows_per_batch, tm):
    n, d = x.shape
    per = rows_per_batch // tm
    nt = n // tm
    assert tail_h.shape == (tm, d) and tail_lg.shape == (N_EXPERTS, tm)
    row = lambda i: jnp.minimum(i, nt - 1)
    return pl.pallas_call(
        _norm_router_kernel,
        grid=(nt + 1,),
        in_specs=[pl.BlockSpec((tm, d), lambda i: (row(i), 0)),
                  pl.BlockSpec((1, d), lambda i: (0, 0)),
                  pl.BlockSpec((1, 1, d), lambda i: (row(i) // per, 0, 0)),
                  pl.BlockSpec((1, 1, d), lambda i: (row(i) // per, 0, 0)),
                  pl.BlockSpec((N_EXPERTS, d), lambda i: (0, 0)),
                  pl.BlockSpec((tm, d), lambda i: (0, 0)),
                  pl.BlockSpec((N_EXPERTS, tm), lambda i: (0, 0))],
        out_specs=[pl.BlockSpec((tm, d), lambda i: (i, 0)),
                   pl.BlockSpec((N_EXPERTS, tm), lambda i: (0, i))],
        out_shape=[jax.ShapeDtypeStruct((n + tm, d), BF16),
                   jax.ShapeDtypeStruct((N_EXPERTS, n + tm), jnp.float32)],
        compiler_params=_cparams(("arbitrary",)),
        name="norm_router",
    )(x, g.reshape(1, d), sc[:, None, :], sh[:, None, :], w_router_t, tail_h, tail_lg)


def _route_kernel(lg_ref, b_ref, eidx_ref, wts_ref, rank_ref, cnt_ref, *, tn):
    shape = (N_GROUPS, GROUP_SIZE, tn)
    sc = _sigmoid(lg_ref[...])
    biased = sc + b_ref[...]
    e_iota = lax.broadcasted_iota(jnp.int32, shape, 1)
    g_iota3 = lax.broadcasted_iota(jnp.int32, shape, 0)
    flat_iota = g_iota3 * GROUP_SIZE + e_iota
    g_iota = lax.broadcasted_iota(jnp.int32, (N_GROUPS, 1, tn), 0)
    ninf = -jnp.inf

    m1 = jnp.max(biased, axis=1, keepdims=True)
    first = jnp.min(jnp.where(biased == m1, e_iota, GROUP_SIZE), axis=1, keepdims=True)
    m2 = jnp.max(jnp.where(e_iota == first, ninf, biased), axis=1, keepdims=True)
    cur = m1 + m2
    gsel = jnp.zeros((N_GROUPS, 1, tn), jnp.float32)
    for _ in range(TOPK_GROUPS):
        mx = jnp.max(cur, axis=0, keepdims=True)
        idx = jnp.min(jnp.where(cur == mx, g_iota, N_GROUPS), axis=0, keepdims=True)
        hit = g_iota == idx
        gsel = jnp.where(hit, 1.0, gsel)
        cur = jnp.where(hit, ninf, cur)

    cur = jnp.where(gsel > 0.0, biased, ninf)
    hits, ws = [], []
    for k in range(TOP_K):
        mx = jnp.max(jnp.max(cur, axis=1, keepdims=True), axis=0, keepdims=True)
        cand = jnp.where(cur == mx, flat_iota, N_EXPERTS)
        idx = jnp.min(jnp.min(cand, axis=1, keepdims=True), axis=0, keepdims=True)
        hit = flat_iota == idx
        w = jnp.sum(jnp.sum(jnp.where(hit, sc, 0.0), axis=1, keepdims=True), axis=0, keepdims=True)
        eidx_ref[k:k + 1, :] = idx.reshape(1, tn)
        hits.append(hit)
        ws.append(w)
        cur = jnp.where(hit, ninf, cur)
    wsum = ws[0]
    for w in ws[1:]:
        wsum = wsum + w
    for k in range(TOP_K):
        wts_ref[k:k + 1, :] = (ws[k] / wsum * ROUTED_SCALE).reshape(1, tn)

    member = jnp.zeros(shape, jnp.float32)
    for hit in hits:
        member = jnp.where(hit, 1.0, member)
    member2 = member.reshape(N_EXPERTS, tn)
    r = lax.broadcasted_iota(jnp.int32, (tn, tn), 0)
    c = lax.broadcasted_iota(jnp.int32, (tn, tn), 1)
    upper = jnp.where(r < c, 1.0, 0.0).astype(BF16)
    prefix = _dot(member2.astype(BF16), upper).reshape(shape)
    for k in range(TOP_K):
        rk = jnp.sum(jnp.sum(jnp.where(hits[k], prefix, 0.0), axis=1, keepdims=True), axis=0, keepdims=True)
        rank_ref[k:k + 1, :] = rk.reshape(1, tn)
    cnt = jnp.sum(member2, axis=1, keepdims=True)
    cnt_ref[0] = jnp.broadcast_to(cnt, (N_EXPERTS, LANES))


def _route(logits_t, b_router, tn):
    n_pad = logits_t.shape[1]
    nt = n_pad // tn
    lg3 = logits_t.reshape(N_GROUPS, GROUP_SIZE, n_pad)
    b3 = b_router.astype(jnp.float32).reshape(N_GROUPS, GROUP_SIZE, 1)
    row = lambda dt: jax.ShapeDtypeStruct((TOP_K, n_pad), dt)
    return pl.pallas_call(
        functools.partial(_route_kernel, tn=tn),
        grid=(nt,),
        in_specs=[pl.BlockSpec((N_GROUPS, GROUP_SIZE, tn), lambda i: (0, 0, i)),
                  pl.BlockSpec((N_GROUPS, GROUP_SIZE, 1), lambda i: (0, 0, 0))],
        out_specs=[pl.BlockSpec((TOP_K, tn), lambda i: (0, i)),
                   pl.BlockSpec((TOP_K, tn), lambda i: (0, i)),
                   pl.BlockSpec((TOP_K, tn), lambda i: (0, i)),
                   pl.BlockSpec((1, N_EXPERTS, LANES), lambda i: (i, 0, 0))],
        out_shape=[row(jnp.int32), row(jnp.float32), row(jnp.float32),
                   jax.ShapeDtypeStruct((nt, N_EXPERTS, LANES), jnp.float32)],
        compiler_params=_cparams(("arbitrary",)),
        name="route",
    )(lg3, b3)


def _dest_kernel(eidx_ref, rank_ref, base_ref, o_ref, *, tn):
    e_iota = lax.broadcasted_iota(jnp.int32, (N_EXPERTS, tn), 0)
    base = base_ref[0][:, :1]
    for k in range(TOP_K):
        onehot = e_iota == eidx_ref[k:k + 1, :]
        b = jnp.sum(jnp.where(onehot, base, 0.0), axis=0, keepdims=True)
        o_ref[k:k + 1, :] = (b + rank_ref[k:k + 1, :]).astype(jnp.int32)


def _dest(eidx, rank, base, tn):
    n_pad = eidx.shape[1]
    return pl.pallas_call(
        functools.partial(_dest_kernel, tn=tn),
        grid=(n_pad // tn,),
        in_specs=[pl.BlockSpec((TOP_K, tn), lambda i: (0, i)),
                  pl.BlockSpec((TOP_K, tn), lambda i: (0, i)),
                  pl.BlockSpec((1, N_EXPERTS, LANES), lambda i: (i, 0, 0))],
        out_specs=pl.BlockSpec((TOP_K, tn), lambda i: (0, i)),
        out_shape=jax.ShapeDtypeStruct((TOP_K, n_pad), jnp.int32),
        compiler_params=_cparams(("arbitrary",)),
        name="dest",
    )(eidx, rank, base)


def _pack_words(lo_f32, hi_f32):
    lo = lax.shift_right_logical(pltpu.bitcast(lo_f32, jnp.uint32), jnp.uint32(16))
    hi = pltpu.bitcast(hi_f32, jnp.uint32) & jnp.uint32(0xFFFF0000)
    return hi | lo


def _unpack_words(w):
    lo = pltpu.bitcast(lax.shift_left(w, jnp.uint32(16)), jnp.float32)
    hi = pltpu.bitcast(w & jnp.uint32(0xFFFF0000), jnp.float32)
    return lo, hi


def _bf16_exact(x):
    return x.astype(BF16).astype(jnp.float32)


def _dispatch_kernel(pstart_ref, pcnt_ref, h_ref, dest_hbm, xs_hbm, dsm, pk, zrow, sem_d, sem_r,
                     *, tn, nt):
    i = pl.program_id(0)
    half = pk.shape[1]

    @pl.when(i < nt)
    def _():
        cp = pltpu.make_async_copy(dest_hbm.at[pl.ds(i * (TOP_K * tn), TOP_K * tn)], dsm, sem_d)
        cp.start()
        x = h_ref[...]
        pk[...] = _pack_words(x[:, :half].astype(jnp.float32), x[:, half:].astype(jnp.float32))
        cp.wait()

        def body(r, carry):
            for k in range(TOP_K):
                d = dsm[k * tn + r]
                pltpu.make_async_copy(pk.at[pl.ds(r, 1), :], xs_hbm.at[pl.ds(d, 1), :],
                                      sem_r).start(priority=k % 2)
            return carry

        lax.fori_loop(0, tn, body, 0)
        for k in range(TOP_K):
            pltpu.make_async_copy(pk, xs_hbm.at[pl.ds(0, tn), :], sem_r).wait()

    @pl.when(i == nt)
    def _():
        zrow[...] = jnp.zeros(zrow.shape, zrow.dtype)

        def per_expert(e, carry):
            s0 = pstart_ref[e]
            c = pcnt_ref[e]
            head = jnp.minimum(c, (8 - s0 % 8) % 8)
            g0 = s0 + head
            ngrp = (c - head) // 8
            grp = lambda r: pl.ds(pl.multiple_of(g0 + r * 8, 8), 8)

            def start(r, cc):
                pltpu.make_async_copy(zrow.at[pl.ds(0, 1), :], xs_hbm.at[pl.ds(s0 + r, 1), :], sem_r).start()
                return cc

            def wait(r, cc):
                pltpu.make_async_copy(zrow.at[pl.ds(0, 1), :], xs_hbm.at[pl.ds(s0, 1), :], sem_r).wait()
                return cc

            def gstart(r, cc):
                pltpu.make_async_copy(zrow, xs_hbm.at[grp(r), :], sem_r).start()
                return cc

            def gwait(r, cc):
                pltpu.make_async_copy(zrow, xs_hbm.at[grp(0), :], sem_r).wait()
                return cc

            lax.fori_loop(0, head, start, 0)
            lax.fori_loop(0, ngrp, gstart, 0)
            lax.fori_loop(0, head, wait, 0)
            lax.fori_loop(0, ngrp, gwait, 0)
            return carry

        lax.fori_loop(0, N_EXPERTS, per_expert, 0)

        t0 = pstart_ref[N_EXPERTS]
        groups = pcnt_ref[N_EXPERTS]
        rows8 = lambda r: pl.ds(pl.multiple_of(t0 + r * 8, 8), 8)

        def tstart(r, cc):
            pltpu.make_async_copy(zrow, xs_hbm.at[rows8(r), :], sem_r).start()
            return cc

        def twait(r, cc):
            pltpu.make_async_copy(zrow, xs_hbm.at[rows8(0), :], sem_r).wait()
            return cc

        lax.fori_loop(0, groups, tstart, 0)
        lax.fori_loop(0, groups, twait, 0)


def _dispatch(h2, dest_flat, pad_start, pad_cnt, n_rows, tn):
    n_pad, d = h2.shape
    nt = n_pad // tn
    grid_spec = pltpu.PrefetchScalarGridSpec(
        num_scalar_prefetch=2,
        grid=(nt + 1,),
        in_specs=[pl.BlockSpec((tn, d), lambda i, a, b: (jnp.minimum(i, nt - 1), 0)),
                  pl.BlockSpec(memory_space=pl.ANY)],
        out_specs=pl.BlockSpec(memory_space=pl.ANY),
        scratch_shapes=[pltpu.SMEM((TOP_K * tn,), jnp.int32),
                        pltpu.VMEM((tn, d // 2), jnp.uint32),
                        pltpu.VMEM((8, d // 2), jnp.uint32),
                        pltpu.SemaphoreType.DMA(()),
                        pltpu.SemaphoreType.DMA(())])
    return pl.pallas_call(
        functools.partial(_dispatch_kernel, tn=tn, nt=nt),
        grid_spec=grid_spec,
        out_shape=jax.ShapeDtypeStruct((n_rows, d // 2), jnp.uint32),
        compiler_params=_cparams(("arbitrary",)),
        name="dispatch",
    )(pad_start, pad_cnt, h2, dest_flat)


def _expert_kernel(blk_e_ref, nused_ref, next_e_ref, slot_ref, x_ref, wg_hbm, wu_hbm, wd_hbm, y_ref,
                   wg_f, wu_f, wd_f, wgb, wub, wdb, sems):
    i = pl.program_id(0)
    nused = nused_ref[0]
    ii = jnp.minimum(i, nused - 1)
    e = blk_e_ref[ii]
    e_prev = blk_e_ref[jnp.maximum(ii - 1, 0)]
    half = x_ref.shape[1]

    def copies(ex, sl):
        return [pltpu.make_async_copy(src.at[ex], dst.at[sl], sems.at[sl, t])
                for t, (src, dst) in enumerate(((wg_hbm, wg_f), (wu_hbm, wu_f), (wd_hbm, wd_f)))]

    @pl.when(i == 0)
    def _():
        for c in copies(e, slot_ref[e]):
            c.start()

    @pl.when((i < nused) & ((i == 0) | (e != e_prev)))
    def _():
        sl = slot_ref[e]
        for c in copies(e, sl):
            c.wait()
        nxt = next_e_ref[e]

        @pl.when(nxt >= 0)
        def _():
            for c in copies(nxt, 1 - sl):
                c.start()

        wgb[...] = wg_f[sl].astype(BF16)
        wub[...] = wu_f[sl].astype(BF16)
        wdb[...] = wd_f[sl].astype(BF16)

    @pl.when(i < nused)
    def _():
        lo, hi = _unpack_words(x_ref[...])
        xl = lo.astype(BF16)
        xh = hi.astype(BF16)
        g = _dot(xl, wgb[:half, :]) + _dot(xh, wgb[half:, :])
        u = _dot(xl, wub[:half, :]) + _dot(xh, wub[half:, :])
        hmid = (_silu(g) * u).astype(BF16)
        y = _dot(hmid, wdb[...])
        y_ref[...] = _pack_words(_bf16_exact(y[:, :half]), _bf16_exact(y[:, half:]))

    @pl.when(i >= nused)
    def _():
        y_ref[...] = jnp.zeros(y_ref.shape, y_ref.dtype)


def _experts(xs, blk_e, nused, next_e, slot, w_gate, w_up, w_down, tb):
    n_rows, half = xs.shape
    _, d, f = w_gate.shape
    nblk = n_rows // tb

    def xmap(i, be, nu, ne, sl):
        return (jnp.minimum(i, nu[0] - 1), 0)

    grid_spec = pltpu.PrefetchScalarGridSpec(
        num_scalar_prefetch=4,
        grid=(nblk,),
        in_specs=[pl.BlockSpec((tb, half), xmap),
                  pl.BlockSpec(memory_space=pl.ANY),
                  pl.BlockSpec(memory_space=pl.ANY),
                  pl.BlockSpec(memory_space=pl.ANY)],
        out_specs=pl.BlockSpec((tb, half), lambda i, be, nu, ne, sl: (i, 0)),
        scratch_shapes=[pltpu.VMEM((2, d, f), jnp.float32),
                        pltpu.VMEM((2, d, f), jnp.float32),
                        pltpu.VMEM((2, f, d), jnp.float32),
                        pltpu.VMEM((d, f), BF16),
                        pltpu.VMEM((d, f), BF16),
                        pltpu.VMEM((f, d), BF16),
                        pltpu.SemaphoreType.DMA((2, 3))])
    return pl.pallas_call(
        _expert_kernel,
        grid_spec=grid_spec,
        out_shape=jax.ShapeDtypeStruct((n_rows, half), jnp.uint32),
        compiler_params=_cparams(("arbitrary",)),
        name="experts",
    )(blk_e, nused, next_e, slot, xs, w_gate, w_up, w_down)


def _ffn_kernel(h_ref, wg_ref, wu_ref, wd_ref, o_ref, wgb, wub, wdb):
    @pl.when(pl.program_id(0) == 0)
    def _():
        wgb[...] = wg_ref[...].astype(BF16)
        wub[...] = wu_ref[...].astype(BF16)
        wdb[...] = wd_ref[...].astype(BF16)

    x = h_ref[...]
    hmid = (_silu(_dot(x, wgb[...])) * _dot(x, wub[...])).astype(BF16)
    o_ref[...] = _dot(hmid, wdb[...]).astype(o_ref.dtype)


def _ffn_shared(h2, wg, wu, wd, tm):
    n_pad, d = h2.shape
    f = wg.shape[1]
    return pl.pallas_call(
        _ffn_kernel,
        grid=(n_pad // tm,),
        in_specs=[pl.BlockSpec((tm, d), lambda i: (i, 0)),
                  pl.BlockSpec((d, f), lambda i: (0, 0)),
                  pl.BlockSpec((d, f), lambda i: (0, 0)),
                  pl.BlockSpec((f, d), lambda i: (0, 0))],
        out_specs=pl.BlockSpec((tm, d), lambda i: (i, 0)),
        out_shape=jax.ShapeDtypeStruct((n_pad, d), BF16),
        scratch_shapes=[pltpu.VMEM((d, f), BF16),
                        pltpu.VMEM((d, f), BF16),
                        pltpu.VMEM((f, d), BF16)],
        compiler_params=_cparams(("arbitrary",)),
        name="ffn_shared",
    )(h2, wg, wu, wd)


def _combine_kernel(dest_hbm, ys_hbm, wts_ref, sh_ref, x_ref, gt_ref, g_ref, o_ref,
                    dsm, buf, sem_d, sem_r, *, tn, tile0, nt, final_norm):
    i = pl.program_id(0)
    slot = i % 2

    def dest_copy(tile, sl):
        return pltpu.make_async_copy(dest_hbm.at[pl.ds((tile0 + tile) * (TOP_K * tn), TOP_K * tn)],
                                     dsm.at[sl], sem_d.at[sl])

    def issue_rows(sl):
        def body(r, carry):
            for k in range(TOP_K):
                d = dsm[sl, k * tn + r]
                pltpu.make_async_copy(ys_hbm.at[pl.ds(d, 1), :], buf.at[sl, k, pl.ds(r, 1), :],
                                      sem_r.at[sl]).start(priority=k % 2)
            return carry

        lax.fori_loop(0, tn, body, 0)

    @pl.when(i == 0)
    def _():
        first = dest_copy(0, 0)
        first.start()
        first.wait()
        issue_rows(0)
        if nt > 1:
            dest_copy(1, 1).start()

    @pl.when(i + 1 < nt)
    def _():
        dest_copy(i + 1, 1 - slot).wait()
        issue_rows(1 - slot)

        @pl.when(i + 2 < nt)
        def _():
            dest_copy(i + 2, slot).start()

    for k in range(TOP_K):
        pltpu.make_async_copy(ys_hbm.at[pl.ds(0, tn), :], buf.at[slot, k], sem_r.at[slot]).wait()

    half = buf.shape[3]
    wts = wts_ref[...]
    acc_lo = jnp.zeros((tn, half), jnp.float32)
    acc_hi = jnp.zeros((tn, half), jnp.float32)
    for k in range(TOP_K):
        lo, hi = _unpack_words(buf[slot, k])
        wk = wts[:, k:k + 1]
        acc_lo = acc_lo + wk * lo
        acc_hi = acc_hi + wk * hi
    sh = sh_ref[...].astype(jnp.float32)
    gt = gt_ref[0]
    x_lo = x_ref[:, :half] + gt[:, :half] * (acc_lo + sh[:, :half])
    x_hi = x_ref[:, half:] + gt[:, half:] * (acc_hi + sh[:, half:])
    if final_norm:
        ms = (jnp.sum(x_lo * x_lo, axis=-1, keepdims=True)
              + jnp.sum(x_hi * x_hi, axis=-1, keepdims=True)) / (2 * half)
        inv = lax.rsqrt(ms + EPS)
        g = g_ref[...]
        x_lo = x_lo * inv * g[:, :half]
        x_hi = x_hi * inv * g[:, half:]
    o_ref[:, :half] = x_lo
    o_ref[:, half:] = x_hi


def _combine(dest_flat, ys, wts_tok, shared, x1, gt3, g_final, tile0, tn, final_norm):
    rows, d = x1.shape
    nt = rows // tn
    gr = gt3.shape[1]
    per = nt // gt3.shape[0]
    return pl.pallas_call(
        functools.partial(_combine_kernel, tn=tn, tile0=tile0, nt=nt, final_norm=final_norm),
        grid=(nt,),
        in_specs=[pl.BlockSpec(memory_space=pl.ANY),
                  pl.BlockSpec(memory_space=pl.ANY),
                  pl.BlockSpec((tn, TOP_K), lambda i: (tile0 + i, 0)),
                  pl.BlockSpec((tn, d), lambda i: (tile0 + i, 0)),
                  pl.BlockSpec((tn, d), lambda i: (i, 0)),
                  pl.BlockSpec((1, gr, d), lambda i: (i // per, 0, 0)),
                  pl.BlockSpec((1, d), lambda i: (0, 0))],
        out_specs=pl.BlockSpec((tn, d), lambda i: (i, 0)),
        out_shape=jax.ShapeDtypeStruct((rows, d), jnp.float32),
        scratch_shapes=[pltpu.SMEM((2, TOP_K * tn), jnp.int32),
                        pltpu.VMEM((2, TOP_K, tn, d // 2), jnp.uint32),
                        pltpu.SemaphoreType.DMA((2,)),
                        pltpu.SemaphoreType.DMA((2,))],
        compiler_params=_cparams(("arbitrary",)),
        name="combine",
    )(dest_flat, ys, wts_tok, shared, x1, gt3, g_final.reshape(1, d))


def _page_specs(shape, n, pg):
    def mk(u):
        return pl.BlockSpec((1,) + shape, lambda b, j, pt: (pt[b, j * pg + u], 0, 0))
    return [mk(u) for u in range(n)]


def _kv_page_specs(kshape, vshape, pg, ns):
    def mk(shape, first):
        def one(u):
            def index(b, j, pt):
                step = jnp.minimum(j, ns - 1) if first else jnp.maximum(j - ns, 0)
                return (pt[b, step * pg + u], 0, 0)
            return pl.BlockSpec((1,) + shape, index)
        return [one(u) for u in range(pg)]
    return mk(kshape, True) + mk(vshape, False)


def _softmax_pages(s_ref):
    s = s_ref[...]
    m = jnp.max(jnp.max(s, axis=0, keepdims=True), axis=2, keepdims=True)
    e = jnp.exp(s - m)
    return e / jnp.sum(jnp.sum(e, axis=0, keepdims=True), axis=2, keepdims=True)


def _da_sample_kernel(pt_ref, lam_ref, q_ref, kn_ref, vn_ref, bl_ref, cf_ref, b0_ref, ex_ref, hm_ref,
                      *rest, pg, n_pages):
    k_refs, v_refs = rest[:pg], rest[pg:2 * pg]
    o_ref, s_ref, a_ref, acc_ref = rest[2 * pg:]
    j = pl.program_id(1)
    ns = n_pages // pg
    r = q_ref.shape[1]
    nh = r // 2
    lane = lax.broadcasted_iota(jnp.int32, (r, PAGE_SIZE), 1)

    @pl.when(j < ns)
    def _():
        q = q_ref[0]
        qb = _bf(q)
        for u in range(pg):
            page = j * pg + u
            s = _dot(qb, _bf(k_refs[u][0]))
            s_ref[page] = s + jnp.where(page == (n_pages - 1), bl_ref[...], cf_ref[...])

        @pl.when(j == 0)
        def _():
            s_new = jnp.sum(_rounded(q) * _rounded(kn_ref[0]), axis=-1, keepdims=True) + b0_ref[...]
            s_ref[n_pages] = jnp.where(lane == 0, s_new, NEG)

    @pl.when(j == ns)
    def _():
        p = _softmax_pages(s_ref)
        a = p[:, :nh, :] - lam_ref[0] * p[:, nh:, :]
        a_ref[...] = _bf(jnp.concatenate([a, jnp.zeros_like(a)], axis=1))
        a_new = a_ref[n_pages][:, 0:1].astype(jnp.float32)
        acc_ref[...] = a_new * _rounded(vn_ref[0])

    @pl.when(j >= ns)
    def _():
        acc = acc_ref[...]
        for u in range(pg):
            page = (j - ns) * pg + u
            pe = _bf(_dot(a_ref[page], ex_ref[...]) * hm_ref[...])
            acc = acc + _dot(pe, _bf(v_refs[u][0]))
        acc_ref[...] = acc

    @pl.when(j == pl.num_programs(1) - 1)
    def _():
        o_ref[0] = acc_ref[...]


def _da_sample(page_table, lam, qbd, knew, vnew, bias_last, cfar, bias0, kt, v2, pg):
    nb, n_pages = page_table.shape
    r, w = qbd.shape[1:]
    rows_v, dv = v2.shape[1:]
    nh = rows_v // PAGE_SIZE
    ns = n_pages // pg
    col = jnp.arange(rows_v, dtype=jnp.int32)
    expand = (col[None, :] // nh == jnp.arange(PAGE_SIZE, dtype=jnp.int32)[:, None]).astype(BF16)
    head_mask = (col[None, :] % nh == jnp.arange(r, dtype=jnp.int32)[:, None]).astype(jnp.float32)
    full = lambda shp: pl.BlockSpec(shp, lambda b, j, pt: (0,) * len(shp))
    per_b = lambda shp: pl.BlockSpec((1,) + shp, lambda b, j, pt: (b, 0, 0))
    grid_spec = pltpu.PrefetchScalarGridSpec(
        num_scalar_prefetch=1,
        grid=(nb, 2 * ns),
        in_specs=[pl.BlockSpec(memory_space=pltpu.SMEM),
                  per_b((r, w)), per_b((1, w)), per_b((r, dv)),
                  full((r, PAGE_SIZE)), full((r, 1)), full((r, 1)),
                  full((PAGE_SIZE, rows_v)), full((r, rows_v))]
                 + _kv_page_specs((w, PAGE_SIZE), (rows_v, dv), pg, ns),
        out_specs=per_b((r, dv)),
        scratch_shapes=[pltpu.VMEM((n_pages + 1, r, PAGE_SIZE), jnp.float32),
                        pltpu.VMEM((n_pages + 1, r, PAGE_SIZE), BF16),
                        pltpu.VMEM((r, dv), jnp.float32)])
    return pl.pallas_call(
        functools.partial(_da_sample_kernel, pg=pg, n_pages=n_pages),
        grid_spec=grid_spec,
        out_shape=jax.ShapeDtypeStruct((nb, r, dv), jnp.float32),
        compiler_params=_cparams(("arbitrary", "arbitrary")),
        name="da_sample",
    )(page_table, lam, qbd, knew, vnew, bias_last, cfar, bias0, expand, head_mask,
      *([kt] * pg), *([v2] * pg))


def _idx_sample_kernel(pt_ref, q_ref, w_ref, kn_ref, *rest, pg, n_pages, topk):
    k_refs = rest[:pg]
    sel_ref, sc_ref = rest[pg:]
    j = pl.program_id(1)
    q = q_ref[0]
    w = _rounded(w_ref[0])
    scale = IDX_DK ** -0.5 * IDX_HEADS ** -0.5
    rows = sc_ref.shape[0]
    lane = lax.broadcasted_iota(jnp.int32, (1, PAGE_SIZE), 1)

    @pl.when(j == 0)
    def _():
        sc_ref[...] = jnp.full(sc_ref.shape, -jnp.inf, jnp.float32)
        d = jnp.maximum(jnp.sum(_rounded(q) * _rounded(kn_ref[0]), axis=-1, keepdims=True), 0.0)
        s_new = jnp.sum(w * _rounded(d), axis=0, keepdims=True) * scale
        sc_ref[n_pages:n_pages + 1, :] = jnp.where(lane == 0, s_new, -jnp.inf)

    qb = _bf(q)
    for u in range(pg):
        d = _rounded(jnp.maximum(_dot(qb, _bf(k_refs[u][0])), 0.0))
        sc_ref[pl.ds(j * pg + u, 1), :] = jnp.sum(w * d, axis=0, keepdims=True) * scale

    @pl.when(j == pl.num_programs(1) - 1)
    def _():
        key = _float_key(sc_ref[...])
        pos = (lax.broadcasted_iota(jnp.int32, key.shape, 0) * PAGE_SIZE
               + lax.broadcasted_iota(jnp.int32, key.shape, 1))

        def bit_body(it, thr):
            cand = thr + (jnp.int32(1) << (31 - it))
            cnt = jnp.sum(jnp.where(key >= cand, 1.0, 0.0))
            return jnp.where(cnt >= float(topk), cand, thr)

        thr = lax.fori_loop(0, 32, bit_body, jnp.int32(INT_MIN))
        thr = jnp.maximum(thr, jnp.int32(KEY_NEG_INF + 1))
        need = float(topk) - jnp.sum(jnp.where(key > thr, 1.0, 0.0))
        n_bits = max(1, int(rows * PAGE_SIZE).bit_length())

        def cut_body(it, cut):
            cand = cut + (jnp.int32(1) << (n_bits - 1 - it))
            n_eq = jnp.sum(jnp.where((key == thr) & (pos < cand), 1.0, 0.0))
            return jnp.where(n_eq <= need, cand, cut)

        cut = lax.fori_loop(0, n_bits, cut_body, jnp.int32(0))
        sel_ref[0] = jnp.where((key > thr) | ((key == thr) & (pos < cut)), 1.0, 0.0)


def _idx_sample(page_table, qix, wix, knew, kc, pg, topk):
    nb, n_pages = page_table.shape
    rows = -(-(n_pages + 1) // 8) * 8
    per_b = lambda shp: pl.BlockSpec((1,) + shp, lambda b, j, pt: (b, 0, 0))
    grid_spec = pltpu.PrefetchScalarGridSpec(
        num_scalar_prefetch=1,
        grid=(nb, n_pages // pg),
        in_specs=[per_b((IDX_HEADS, IDX_DK)), per_b((IDX_HEADS, 1)), per_b((1, IDX_DK))]
                 + _page_specs((IDX_DK, PAGE_SIZE), pg, pg),
        out_specs=per_b((rows, PAGE_SIZE)),
        scratch_shapes=[pltpu.VMEM((rows, PAGE_SIZE), jnp.float32)])
    return pl.pallas_call(
        functools.partial(_idx_sample_kernel, pg=pg, n_pages=n_pages, topk=topk),
        grid_spec=grid_spec,
        out_shape=jax.ShapeDtypeStruct((nb, rows, PAGE_SIZE), jnp.float32),
        compiler_params=_cparams(("arbitrary", "arbitrary")),
        name="idx_sample",
    )(page_table, qix, wix, knew, *([kc] * pg))


def _sa_sample_kernel(pt_ref, q_ref, kn_ref, vn_ref, sel_ref, bl_ref, cf_ref, b0_ref, gm_ref, *rest,
                      pg, n_pages):
    k_refs, v_refs = rest[:pg], rest[pg:2 * pg]
    o_ref, s_ref, p_ref, acc_ref = rest[2 * pg:]
    j = pl.program_id(1)
    ns = n_pages // pg
    scale = SA_DH ** -0.5
    r, cols = gm_ref.shape
    lane = lax.broadcasted_iota(jnp.int32, (r, cols), 1)

    @pl.when(j < ns)
    def _():
        q = q_ref[0]
        qb = _bf(q)
        for u in range(pg):
            page = j * pg + u
            s = _dot_nt(qb, _bf(k_refs[u][0])) * scale
            s = s + jnp.where(page == (n_pages - 1), bl_ref[...], cf_ref[...])
            keep = (sel_ref[0, pl.ds(page, 1), :] > 0.0) & (gm_ref[...] > 0.0)
            s_ref[page] = jnp.where(keep, s, NEG)

        @pl.when(j == 0)
        def _():
            on = sel_ref[0, n_pages:n_pages + 1, 0:1] > 0.0
            s_new = jnp.sum(_rounded(q) * _rounded(kn_ref[0]), axis=-1, keepdims=True) * scale + b0_ref[...]
            s_ref[n_pages] = jnp.where((lane == 0) & on, s_new, NEG)

    @pl.when(j == ns)
    def _():
        p_ref[...] = _bf(_softmax_pages(s_ref))
        acc_ref[...] = p_ref[n_pages][:, 0:1].astype(jnp.float32) * _rounded(vn_ref[0])

    @pl.when(j >= ns)
    def _():
        acc = acc_ref[...]
        for u in range(pg):
            acc = acc + _dot(p_ref[(j - ns) * pg + u], _bf(v_refs[u][0]))
        acc_ref[...] = acc

    @pl.when(j == pl.num_programs(1) - 1)
    def _():
        o_ref[0] = acc_ref[...]


def _sa_sample(page_table, q, knew, vnew, sel2, bias_last2, cfar, bias0, group_mask, k2, v2, pg):
    nb, n_pages = page_table.shape
    r, dh = q.shape[1:]
    srows, cols = sel2.shape[1:]
    ns = n_pages // pg
    full = lambda shp: pl.BlockSpec(shp, lambda b, j, pt: (0,) * len(shp))
    per_b = lambda shp: pl.BlockSpec((1,) + shp, lambda b, j, pt: (b, 0, 0))
    grid_spec = pltpu.PrefetchScalarGridSpec(
        num_scalar_prefetch=1,
        grid=(nb, 2 * ns),
        in_specs=[per_b((r, dh)), per_b((r, dh)), per_b((r, dh)), per_b((srows, cols)),
                  full((r, cols)), full((r, 1)), full((r, 1)), full((r, cols))]
                 + _kv_page_specs((cols, dh), (cols, dh), pg, ns),
        out_specs=per_b((r, dh)),
        scratch_shapes=[pltpu.VMEM((n_pages + 1, r, cols), jnp.float32),
                        pltpu.VMEM((n_pages + 1, r, cols), BF16),
                        pltpu.VMEM((r, dh), jnp.float32)])
    return pl.pallas_call(
        functools.partial(_sa_sample_kernel, pg=pg, n_pages=n_pages),
        grid_spec=grid_spec,
        out_shape=jax.ShapeDtypeStruct((nb, r, dh), jnp.float32),
        compiler_params=_cparams(("arbitrary", "arbitrary")),
        name="sa_sample",
    )(page_table, q, knew, vnew, sel2, bias_last2, cfar, bias0, group_mask, *([k2] * pg), *([v2] * pg))


ROW_TILE = 256
COMBINE_TILE = 128
EXPERT_BLOCK = 512
TAIL_ROWS = 256
PAGES_PER_STEP_DA = 16
PAGES_PER_STEP_SA = 32
PAGES_PER_STEP_IDX = 64


def _rms_rows(x, g):
    return x * lax.rsqrt(jnp.mean(x * x, axis=-1, keepdims=True) + EPS) * g


def _pad_rows(x, rows):
    return jnp.pad(x, ((0, rows - x.shape[0]), (0, 0)))


def _tile_major(dest, tn):
    k, n = dest.shape
    return dest.reshape(k, n // tn, tn).transpose(1, 0, 2).reshape(-1)


def kernel(x_prompt, x_sample, c_prompt, c_sample, cache_da_k, cache_da_v, cache_sa_k, cache_sa_v, cache_idx_k, page_table, rel_bias_table, w_ada, b_ada, g_attn, g_ffn, w_in, lambda_q1, lambda_k1, lambda_q2, lambda_k2, g_subln, w_proj_da, w_proj_sa, w_out, w_router, b_router, w_gate, w_up, w_down, w_sh_gate, w_sh_up, w_sh_down, g_final):
    f32, bf16 = jnp.float32, BF16
    nb, t, d = x_prompt.shape
    nbs, dec_seq, _ = x_sample.shape
    assert dec_seq == 1
    depth = w_in.shape[0]
    n_pages = page_table.shape[1]
    past_len = n_pages * PAGE_SIZE
    n = nb * t
    n_pad = n + TAIL_ROWS
    tq = min(256, t)
    assert tq >= MAX_DISTANCE and t % tq == 0 and n % ROW_TILE == 0 and nbs <= TAIL_ROWS
    assert TAIL_ROWS == ROW_TILE and EXPERT_BLOCK % 8 == 0
    topk_p = min(SA_TOPK_MAX, t // 4)
    topk_s = min(SA_TOPK_MAX, (past_len + dec_seq) // 4)
    assert topk_p <= tq

    sizes = [DA_HEADS * 2 * DA_DK, DA_HEADS * 2 * DA_DK, DA_HEADS * DA_DV, SA_HEADS * SA_DH,
             SA_KV_HEADS * SA_DH, SA_KV_HEADS * SA_DH, IDX_HEADS * IDX_DK, IDX_DK, IDX_HEADS, d, d]
    offs = [sum(sizes[:i]) for i in range(len(sizes) + 1)]
    (o_qda, o_kda, o_vda, o_qsa, o_ksa, o_vsa, o_qix, o_kix, o_wix, o_gda, o_gsa, _) = offs
    misc_w = 2 * LANES

    table = rel_bias_table.astype(f32)
    cfar = table[N_BUCKETS - 1]
    tiles_da = _near_tiles(table[:, :DA_HEADS], tq)
    tiles_sa = _near_tiles(table[:, DA_HEADS:], tq)
    last_dist = past_len - ((n_pages - 1) * PAGE_SIZE + jnp.arange(PAGE_SIZE, dtype=jnp.int32))
    bias_last = _bias_by_distance(table, last_dist)
    bias0 = _bias_by_distance(table, jnp.zeros((1,), jnp.int32))
    rep2 = lambda a: jnp.repeat(a, 2, axis=0)

    xp = x_prompt.reshape(n, d)
    xs = x_sample.reshape(nbs, d)
    c_all = jnp.concatenate([c_prompt, c_sample], axis=0)
    leaves_p, leaves_s = [], []
    for l in range(depth):
        lam_init = 0.8 - 0.6 * math.exp(-0.3 * l)
        lam = (jnp.exp(jnp.sum(lambda_q1[l].astype(f32) * lambda_k1[l].astype(f32)))
               - jnp.exp(jnp.sum(lambda_q2[l].astype(f32) * lambda_k2[l].astype(f32))) + lam_init)
        mod = _linear_small(c_all, w_ada[l], b_ada[l], silu_in=True)
        sh1, sc1, gt1, sh2, sc2, gt2 = jnp.split(mod[:nb], 6, axis=-1)
        sh1s, sc1s, gt1s, sh2s, sc2s, gt2s = jnp.split(mod[nb:], 6, axis=-1)
        w = w_in[l]

        h = _norm_mod(xp, g_attn[l], sc1, sh1, t, ROW_TILE)
        q_da, = _mm(h, w, o_qda, sizes[0], (bf16,), scale=DA_DK ** -0.5)
        k_da_b, k_da_t = _mm(h, w, o_kda, sizes[1], (bf16,), rows_per_batch=t)
        k_da = k_da_t.reshape(nb, DA_HEADS, 2, DA_DK, t).transpose(0, 4, 1, 2, 3)
        v_da, v_da_b = _mm(h, w, o_vda, sizes[2], (f32, bf16))
        q_sa, = _mm(h, w, o_qsa, sizes[3], (bf16,), scale=SA_DH ** -0.5)
        k_sa, k_sa_b = _mm(h, w, o_ksa, sizes[4], (f32, bf16))
        v_sa, v_sa_b = _mm(h, w, o_vsa, sizes[5], (f32, bf16))
        q_ix, = _mm(h, w, o_qix, sizes[6], (bf16,), tn=512)
        misc, misc_b = _mm(h, w, o_kix, misc_w, (f32, bf16))
        gates, = _mm(h, w[:, o_gda:], 0, 2 * d, (bf16,), sigmoid=True)
        k_ix = misc[:, :IDX_DK]

        o_da = _da_prompt(q_da, k_da_b, v_da_b, tiles_da, cfar[:DA_HEADS], lam.reshape(1),
                          g_subln[l].astype(f32), nb, t, tq, 1.0 - lam_init)
        o_sa = _sa_prompt(q_sa, q_ix, misc, misc_b, k_sa_b, v_sa_b, tiles_sa, cfar[DA_HEADS:], nb, t, tq, topk_p)
        m = _proj_gate(o_da, o_sa, w_proj_da[l], w_proj_sa[l], gates, d)
        x1 = _mm_resid(m, w_out[l], xp, gt1, t)

        hs = _rms_rows(xs, g_attn[l].astype(f32)) * (1.0 + sc1s) + sh1s
        ps = _linear_small(hs, w)
        seg = lambda i: ps[:, offs[i]:offs[i + 1]]
        q_da_s, k_da_s, v_da_s, q_sa_s, k_sa_s, v_sa_s, q_ix_s, k_ix_s, w_ix_s, gda_s, gsa_s = [
            seg(i) for i in range(11)]

        r_da = 2 * DA_HEADS
        q16 = (q_da_s * DA_DK ** -0.5).reshape(nbs, DA_HEADS, 2, DA_DK).transpose(0, 2, 1, 3).reshape(nbs, r_da, DA_DK)
        blk_of_row = 2 * (jnp.arange(r_da) % DA_HEADS) + jnp.arange(r_da) // DA_HEADS
        place = (blk_of_row[:, None] == jnp.arange(r_da)[None, :]).astype(f32)
        qbd_da = (place[None, :, :, None] * q16[:, :, None, :]).reshape(nbs, r_da, -1)
        both = lambda a: jnp.concatenate([a, a], axis=0)
        kt_da = cache_da_k[l].transpose(0, 2, 3, 4, 1).reshape(-1, DA_HEADS * 2 * DA_DK, PAGE_SIZE)
        v2_da = cache_da_v[l].reshape(-1, PAGE_SIZE * DA_HEADS, DA_DV)
        v_rows = jnp.pad(v_da_s.reshape(nbs, DA_HEADS, DA_DV), ((0, 0), (0, r_da - DA_HEADS), (0, 0)))
        a_da = _da_sample(page_table, lam.reshape(1), qbd_da, k_da_s[:, None, :], v_rows,
                          both(bias_last[:DA_HEADS]), both(cfar[:DA_HEADS, None]), both(bias0[:DA_HEADS]),
                          kt_da, v2_da, math.gcd(PAGES_PER_STEP_DA, n_pages))
        o_da_s = a_da[:, :DA_HEADS]
        o_da_s = _rms_rows(o_da_s, g_subln[l].astype(f32)) * (1.0 - lam_init)

        sel = _idx_sample(page_table, q_ix_s.reshape(nbs, IDX_HEADS, IDX_DK), w_ix_s[:, :, None],
                          k_ix_s[:, None, :], cache_idx_k[l].transpose(0, 2, 1),
                          math.gcd(PAGES_PER_STEP_IDX, n_pages), topk_s)
        r_sa = 2 * SA_HEADS
        kv_of_row = jnp.minimum(jnp.arange(r_sa) // SA_GROUP, SA_KV_HEADS - 1)
        pad_sa = lambda a: jnp.pad(a, ((0, 0), (0, r_sa - SA_HEADS), (0, 0)))
        rep_kv = lambda a: jnp.repeat(a, SA_KV_HEADS, axis=-1)
        col_kv = jnp.arange(PAGE_SIZE * SA_KV_HEADS) % SA_KV_HEADS
        k2_sa = cache_sa_k[l].reshape(-1, PAGE_SIZE * SA_KV_HEADS, SA_DH)
        v2_sa = cache_sa_v[l].reshape(-1, PAGE_SIZE * SA_KV_HEADS, SA_DH)
        a_sa = _sa_sample(page_table, pad_sa(q_sa_s.reshape(nbs, SA_HEADS, SA_DH)),
                          k_sa_s.reshape(nbs, SA_KV_HEADS, SA_DH)[:, kv_of_row],
                          v_sa_s.reshape(nbs, SA_KV_HEADS, SA_DH)[:, kv_of_row],
                          rep_kv(sel), rep_kv(_pad_rows(bias_last[DA_HEADS:], r_sa)),
                          _pad_rows(cfar[DA_HEADS:, None], r_sa), _pad_rows(bias0[DA_HEADS:], r_sa),
                          (col_kv[None, :] == kv_of_row[:, None]).astype(f32),
                          k2_sa, v2_sa, math.gcd(PAGES_PER_STEP_SA, n_pages))
        o_sa_s = a_sa[:, :SA_HEADS]

        pda = _linear_small(o_da_s.reshape(nbs, -1), w_proj_da[l])
        psa = _linear_small(o_sa_s.reshape(nbs, -1), w_proj_sa[l])
        ms = _sigmoid(gda_s) * pda + _sigmoid(gsa_s) * psa
        x1s = xs + gt1s * _linear_small(ms, w_out[l])
        h2s = _rms_rows(x1s, g_ffn[l].astype(f32)) * (1.0 + sc2s) + sh2s
        lg_s = _linear_small(h2s, w_router[l])

        h2_all, lg_all = _norm_router(x1, g_ffn[l], sc2, sh2, w_router[l].T,
                                      _pad_rows(h2s.astype(bf16), TAIL_ROWS),
                                      _pad_rows(lg_s, TAIL_ROWS).T, t, ROW_TILE)
        eidx, wts, rank, cnt = _route(lg_all, b_router[l], ROW_TILE)
        cnt_tile = cnt[:, :, 0]
        total = jnp.sum(cnt_tile, axis=0)
        padded = jnp.ceil(total / EXPERT_BLOCK) * EXPERT_BLOCK
        pends = jnp.cumsum(padded)
        pstart = pends - padded
        base = pstart[None, :] + jnp.cumsum(cnt_tile, axis=0) - cnt_tile
        dest = _dest(eidx, rank, jnp.broadcast_to(base[:, :, None], base.shape + (LANES,)), ROW_TILE)
        n_blk = -(-(n_pad * TOP_K) // EXPERT_BLOCK) + N_EXPERTS
        blk_start = (jnp.arange(n_blk) * EXPERT_BLOCK).astype(f32)
        blk_e = jnp.minimum(jnp.sum(pends[None, :] <= blk_start[:, None], axis=1), N_EXPERTS - 1).astype(jnp.int32)
        n_used = (pends[-1] / EXPERT_BLOCK).astype(jnp.int32).reshape(1)
        n_rows = n_blk * EXPERT_BLOCK
        pad_start = jnp.concatenate([pstart + total, pends[-1:]]).astype(jnp.int32)
        pad_cnt = jnp.concatenate([padded - total, (n_rows - pends[-1:]) / 8]).astype(jnp.int32)
        xs_sorted = _dispatch(h2_all, _tile_major(dest, ROW_TILE), pad_start, pad_cnt, n_rows, ROW_TILE)
        owns = padded > 0
        e_ids = jnp.arange(N_EXPERTS, dtype=jnp.int32)
        later = jnp.where(owns[None, :] & (e_ids[None, :] > e_ids[:, None]), e_ids[None, :], N_EXPERTS)
        next_owner = jnp.min(later, axis=1)
        next_owner = jnp.where(next_owner < N_EXPERTS, next_owner, -1).astype(jnp.int32)
        slot_of_e = ((jnp.cumsum(owns.astype(jnp.int32)) - 1) % 2).astype(jnp.int32)
        ys = _experts(xs_sorted, blk_e, n_used, next_owner, slot_of_e,
                      w_gate[l], w_up[l], w_down[l], EXPERT_BLOCK)
        shared = _ffn_shared(h2_all, w_sh_gate[l], w_sh_up[l], w_sh_down[l], ROW_TILE)
        dest_c = _tile_major(dest, COMBINE_TILE)
        wts_tok = wts.T
        last = l == depth - 1
        xp = _combine(dest_c, ys, wts_tok, shared, x1, gt2[:, None, :], g_final.astype(f32),
                      0, COMBINE_TILE, last)
        tail = _combine(dest_c, ys, wts_tok, shared, _pad_rows(x1s, TAIL_ROWS),
                        _pad_rows(gt2s, TAIL_ROWS).reshape(-1, COMBINE_TILE, d), g_final.astype(f32),
                        n // COMBINE_TILE, COMBINE_TILE, last)
        xs = tail[:nbs]
        leaves_p.append((k_da, v_da, k_sa, v_sa, k_ix))
        leaves_s.append((k_da_s, v_da_s, k_sa_s, v_sa_s, k_ix_s))

    shapes = [(DA_HEADS, 2, DA_DK), (DA_HEADS, DA_DV), (SA_KV_HEADS, SA_DH), (SA_KV_HEADS, SA_DH), (IDX_DK,)]
    out_p = [jnp.stack([lv[i].reshape((nb, t) + shapes[i]) for lv in leaves_p]) for i in range(5)]
    out_s = [jnp.stack([lv[i].reshape((nbs, dec_seq) + shapes[i]) for lv in leaves_s]) for i in range(5)]
    return (xp.reshape(nb, t, d), xs.reshape(nbs, dec_seq, d), *out_p, *out_s)
```

```python
import functools
import math

import jax
import jax.numpy as jnp
from jax import lax
from jax.experimental import pallas as pl
from jax.experimental.pallas import tpu as pltpu

DA_HEADS = 8
DA_DK = 64
DA_DV = 2 * DA_DK
SA_HEADS = 8
SA_KV_HEADS = 2
SA_DH = 128
SA_GROUP = SA_HEADS // SA_KV_HEADS
IDX_HEADS = 16
IDX_DK = 64
SA_TOPK_MAX = 256
N_BUCKETS = 32
MAX_DISTANCE = 128
N_EXPERTS = 64
N_GROUPS = 8
GROUP_SIZE = N_EXPERTS // N_GROUPS
TOPK_GROUPS = 4
TOP_K = 8
ROUTED_SCALE = 2.5
PAGE_SIZE = 128
EPS = 1e-6

LANES = 128
VMEM_LIMIT = 56 * 1024 * 1024

BF16 = jnp.bfloat16
NEG = -1e30
INT_MIN = -(2 ** 31)
KEY_NEG_INF = (0xFF800000 ^ 0x7FFFFFFF) - (1 << 32)

_NT = (((1,), (1,)), ((), ()))


def _cparams(sem):
    return pltpu.CompilerParams(dimension_semantics=sem, vmem_limit_bytes=VMEM_LIMIT)


def _dot(a, b):
    return jnp.dot(a, b, preferred_element_type=jnp.float32)


def _dot_nt(a, b):
    return lax.dot_general(a, b, _NT, preferred_element_type=jnp.float32)


def _bf(x):
    return x.astype(BF16)


def _rounded(x):
    return x.astype(BF16).astype(jnp.float32)


def _sigmoid(x):
    return 1.0 / (1.0 + jnp.exp(-x))


def _silu(x):
    return x * _sigmoid(x)


def _float_key(s):
    b = pltpu.bitcast(s, jnp.int32)
    return b ^ ((b >> 31) & jnp.int32(0x7FFFFFFF))


def _linear_small_kernel(x_ref, w_ref, b_ref, o_ref, *, silu_in):
    x = x_ref[...]
    if silu_in:
        x = _silu(x)
    o_ref[...] = _dot(_bf(x), _bf(w_ref[...])) + b_ref[...]


def _linear_small(x, w, b=None, *, silu_in=False, tn=512):
    m0, k = x.shape
    m = -(-m0 // 16) * 16
    x = jnp.pad(x, ((0, m - m0), (0, 0)))
    n = w.shape[1]
    tn = min(tn, n)
    if b is None:
        b = jnp.zeros((1, n), jnp.float32)
    out = pl.pallas_call(
        functools.partial(_linear_small_kernel, silu_in=silu_in),
        grid=(pl.cdiv(n, tn),),
        in_specs=[pl.BlockSpec((m, k), lambda j: (0, 0)),
                  pl.BlockSpec((k, tn), lambda j: (0, j)),
                  pl.BlockSpec((1, tn), lambda j: (0, j))],
        out_specs=pl.BlockSpec((m, tn), lambda j: (0, j)),
        out_shape=jax.ShapeDtypeStruct((m, n), jnp.float32),
        compiler_params=_cparams(("arbitrary",)),
        name="linear_small",
    )(x, w, b.reshape(1, n))
    return out[:m0]


def _norm_mod_kernel(x_ref, g_ref, sc_ref, sh_ref, o_ref):
    x = x_ref[...]
    y = x * lax.rsqrt(jnp.mean(x * x, axis=-1, keepdims=True) + EPS) * g_ref[...]
    o_ref[...] = (y * (1.0 + sc_ref[0]) + sh_ref[0]).astype(o_ref.dtype)


def _norm_mod(x, g, sc, sh, rows_per_batch, tm):
    n, d = x.shape
    per = rows_per_batch // tm
    return pl.pallas_call(
        _norm_mod_kernel,
        grid=(n // tm,),
        in_specs=[pl.BlockSpec((tm, d), lambda i: (i, 0)),
                  pl.BlockSpec((1, d), lambda i: (0, 0)),
                  pl.BlockSpec((1, 1, d), lambda i: (i // per, 0, 0)),
                  pl.BlockSpec((1, 1, d), lambda i: (i // per, 0, 0))],
        out_specs=pl.BlockSpec((tm, d), lambda i: (i, 0)),
        out_shape=jax.ShapeDtypeStruct((n, d), BF16),
        compiler_params=_cparams(("arbitrary",)),
        name="norm_mod",
    )(x, g.reshape(1, d), sc[:, None, :], sh[:, None, :])


def _mm_kernel(x_ref, w_ref, *rest, scale, sigmoid, n_out, transposed_out):
    o_refs, wbf_ref = rest[:n_out], rest[-1]

    @pl.when(pl.program_id(1) == 0)
    def _():
        wbf_ref[...] = w_ref[...].astype(BF16)

    acc = _dot(x_ref[...], wbf_ref[...])
    if scale != 1.0:
        acc = acc * scale
    if sigmoid:
        acc = _sigmoid(acc)
    for o in o_refs:
        o[...] = acc.astype(o.dtype)
    if transposed_out:
        rest[n_out][0] = acc.T


def _mm(x, w, col0, ncols, out_dtypes, *, scale=1.0, sigmoid=False, tm=1024, tn=1024, rows_per_batch=None):
    m, k = x.shape
    tn = min(tn, ncols)
    tm = min(tm, m)
    assert col0 % tn == 0 and ncols % tn == 0 and m % tm == 0
    jb = col0 // tn
    out_specs = [pl.BlockSpec((tm, tn), lambda j, i: (i, j)) for _ in out_dtypes]
    out_shape = [jax.ShapeDtypeStruct((m, ncols), dt) for dt in out_dtypes]
    if rows_per_batch is not None:
        per = rows_per_batch // tm
        out_specs.append(pl.BlockSpec((1, tn, tm), lambda j, i: (i // per, j, i % per)))
        out_shape.append(jax.ShapeDtypeStruct((m // rows_per_batch, ncols, rows_per_batch), jnp.float32))
    outs = pl.pallas_call(
        functools.partial(_mm_kernel, scale=scale, sigmoid=sigmoid, n_out=len(out_dtypes),
                          transposed_out=rows_per_batch is not None),
        grid=(ncols // tn, m // tm),
        in_specs=[pl.BlockSpec((tm, k), lambda j, i: (i, 0)),
                  pl.BlockSpec((k, tn), lambda j, i: (0, jb + j))],
        out_specs=out_specs,
        out_shape=out_shape,
        scratch_shapes=[pltpu.VMEM((k, tn), BF16)],
        compiler_params=_cparams(("arbitrary", "arbitrary")),
        name="mm_cols",
    )(x, w)
    return outs


def _rel_bucket(dist):
    max_exact = N_BUCKETS // 2
    d = jnp.maximum(dist, 0)
    large = max_exact + (jnp.log(jnp.maximum(d, 1).astype(jnp.float32) / max_exact)
                         / math.log(MAX_DISTANCE / max_exact)
                         * (N_BUCKETS - max_exact)).astype(jnp.int32)
    large = jnp.minimum(large, N_BUCKETS - 1)
    return jnp.where(d < max_exact, d, large)


def _bias_by_distance(table, dists):
    return table[_rel_bucket(dists)].astype(jnp.float32).T


def _toeplitz_kernel(u_ref, o_ref):
    t = o_ref.shape[2]
    x = jnp.broadcast_to(u_ref[0], (t, 2 * t))
    o_ref[0, 0] = pltpu.roll(x, 0, 1, stride=1, stride_axis=0)[:, :t]


def _near_tiles(table, t):
    nh = table.shape[1]
    k = jnp.arange(2 * t, dtype=jnp.int32)
    gens = []
    for off in (0, t):
        d = jnp.where(k < t, off - k, off + 2 * t - k)
        gens.append(jnp.where(d[None] >= 0, _bias_by_distance(table, d), NEG))
    u = jnp.stack(gens, axis=1).reshape(nh * 2, 1, 2 * t)
    return pl.pallas_call(
        _toeplitz_kernel,
        grid=(nh, 2),
        in_specs=[pl.BlockSpec((1, 1, 2 * t), lambda h, o: (h * 2 + o, 0, 0))],
        out_specs=pl.BlockSpec((1, 1, t, t), lambda h, o: (h, o, 0, 0)),
        out_shape=jax.ShapeDtypeStruct((nh, 2, t, t), jnp.float32),
        compiler_params=_cparams(("arbitrary", "arbitrary")),
        name="bias_tiles",
    )(u)


def _fold_lanes(x, op):
    out = x[:, :LANES]
    for c in range(1, x.shape[1] // LANES):
        out = op(out, x[:, c * LANES:(c + 1) * LANES])
    return out


def _chunk_loop(n, fn):
    def body(i, carry):
        for u in range(4):
            fn(4 * i + u)
        return carry

    lax.fori_loop(0, n // 4, body, 0)
    base = (n // 4) * 4

    @pl.when(n % 4 >= 2)
    def _():
        fn(base)
        fn(base + 1)

    @pl.when(n % 2 == 1)
    def _():
        fn(n - 1)


def _da_prompt_kernel(cfar_ref, lam_ref, q_ref, k_ref, v_ref, tile_ref, g_ref, o_ref,
                      s_ref, mpart_ref, shift_ref, lpart_ref, acc_ref, *, tq, out_scale):
    h = pl.program_id(1)
    qi = pl.program_id(2)
    q = q_ref[...]
    lane = lax.broadcasted_iota(jnp.int32, q.shape, 1)
    zero = jnp.zeros_like(q)
    q2 = jnp.concatenate([jnp.where(lane < DA_DK, q, zero), jnp.where(lane >= DA_DK, q, zero)], axis=0)
    cfar = cfar_ref[h]
    n_far = jnp.maximum(qi - 1, 0)
    r2 = 2 * tq

    def chunk_rows(kc):
        return pl.ds(pl.multiple_of(kc * tq, tq), tq)

    def scores(kc, bias):
        s = _dot_nt(q2, k_ref[chunk_rows(kc), :])
        if bias is not None:
            s = s + jnp.concatenate([bias, bias], axis=0)
        s_ref[kc] = s
        mpart_ref[...] = jnp.maximum(mpart_ref[...], _fold_lanes(s, jnp.maximum))

    mpart_ref[...] = jnp.full((r2, LANES), NEG, jnp.float32)
    _chunk_loop(n_far, lambda kc: scores(kc, None))
    m_far = jnp.max(mpart_ref[...], axis=-1, keepdims=True) + cfar
    mpart_ref[...] = jnp.full((r2, LANES), NEG, jnp.float32)

    @pl.when(qi >= 1)
    def _():
        scores(qi - 1, tile_ref[0, 1])
        scores(qi, tile_ref[0, 0])

    @pl.when(qi == 0)
    def _():
        scores(0, tile_ref[0, 0])

    m = jnp.maximum(m_far, jnp.max(mpart_ref[...], axis=-1, keepdims=True))
    shift_ref[0] = jnp.broadcast_to(m - cfar, (r2, LANES))
    shift_ref[1] = jnp.broadcast_to(m, (r2, LANES))

    lpart_ref[...] = jnp.zeros((r2, LANES), jnp.float32)
    acc_ref[...] = jnp.zeros((r2, DA_DV), jnp.float32)

    def weights(kc, which):
        s = s_ref[kc]
        sh = shift_ref[which]
        ps = [jnp.exp(s[:, c * LANES:(c + 1) * LANES] - sh) for c in range(tq // LANES)]
        tot = ps[0]
        for pc in ps[1:]:
            tot = tot + pc
        lpart_ref[...] = lpart_ref[...] + tot
        p = jnp.concatenate(ps, axis=1).astype(BF16)
        acc_ref[...] = acc_ref[...] + _dot(p, v_ref[chunk_rows(kc), :])

    _chunk_loop(n_far, lambda kc: weights(kc, 0))

    @pl.when(qi >= 1)
    def _():
        weights(qi - 1, 1)
        weights(qi, 1)

    @pl.when(qi == 0)
    def _():
        weights(0, 1)

    lam = lam_ref[0]
    a = acc_ref[...] / jnp.sum(lpart_ref[...], axis=-1, keepdims=True)
    o = a[:tq] - lam * a[tq:]
    o = o * lax.rsqrt(jnp.mean(o * o, axis=-1, keepdims=True) + EPS) * g_ref[...]
    o_ref[...] = (o * out_scale).astype(o_ref.dtype)


def _da_prompt(q, k, v, tiles, cfar, lam, g_subln, nb, t, tq, out_scale):
    n = q.shape[0]
    nq = t // tq
    grid_spec = pltpu.PrefetchScalarGridSpec(
        num_scalar_prefetch=0,
        grid=(nb, DA_HEADS, nq),
        in_specs=[pl.BlockSpec(memory_space=pltpu.SMEM),
                  pl.BlockSpec(memory_space=pltpu.SMEM),
                  pl.BlockSpec((tq, LANES), lambda b, h, i: (b * nq + i, h)),
                  pl.BlockSpec((t, LANES), lambda b, h, i: (b, h)),
                  pl.BlockSpec((t, LANES), lambda b, h, i: (b, h)),
                  pl.BlockSpec((1, 2, tq, tq), lambda b, h, i: (h, 0, 0, 0)),
                  pl.BlockSpec((1, DA_DV), lambda b, h, i: (0, 0))],
        out_specs=pl.BlockSpec((tq, LANES), lambda b, h, i: (b * nq + i, h)),
        scratch_shapes=[pltpu.VMEM((nq, 2 * tq, tq), jnp.float32),
                        pltpu.VMEM((2 * tq, LANES), jnp.float32),
                        pltpu.VMEM((2, 2 * tq, LANES), jnp.float32),
                        pltpu.VMEM((2 * tq, LANES), jnp.float32),
                        pltpu.VMEM((2 * tq, DA_DV), jnp.float32)])
    return pl.pallas_call(
        functools.partial(_da_prompt_kernel, tq=tq, out_scale=out_scale),
        grid_spec=grid_spec,
        out_shape=jax.ShapeDtypeStruct((n, DA_HEADS * DA_DV), BF16),
        compiler_params=_cparams(("arbitrary", "arbitrary", "arbitrary")),
        name="da_prompt",
    )(cfar, lam, q, k, v, tiles, g_subln.reshape(1, DA_DV))


def _sa_prompt_kernel(cfar_ref, qs_ref, qx_ref, mq_ref, mk_ref, ks_ref, vs_ref, tile_ref, o_ref,
                      k2_ref, key_ref, hi_ref, lo_ref, cut_ref, s_ref, mpart_ref, shift_ref, lpart_ref,
                      acc_ref, *, tq, topk):
    qi = pl.program_id(1)
    n_chunks = qi + 1
    t = mk_ref.shape[0]

    @pl.when(qi == 0)
    def _():
        kix = mk_ref[:, :LANES].astype(jnp.float32)
        lane = lax.broadcasted_iota(jnp.int32, kix.shape, 1)
        k2_ref[0] = jnp.where(lane < IDX_DK, kix, 0.0).astype(BF16)
        k2_ref[1] = jnp.where(lane >= IDX_DK, pltpu.roll(kix, IDX_DK, axis=1), 0.0).astype(BF16)

    wix = mq_ref[:, IDX_DK:IDX_DK + IDX_HEADS]
    wcols = [wix[:, hh:hh + 1] for hh in range(IDX_HEADS)]
    row = lax.broadcasted_iota(jnp.int32, (tq, tq), 0)
    col = lax.broadcasted_iota(jnp.int32, (tq, tq), 1)

    def score_body(kc, carry):
        rows = pl.ds(pl.multiple_of(kc * tq, tq), tq)
        ke = k2_ref[0, rows, :]
        ko = k2_ref[1, rows, :]
        sc = jnp.zeros((tq, tq), jnp.float32)
        for p in range(IDX_HEADS // 2):
            qp = qx_ref[:, p * LANES:(p + 1) * LANES]
            sc = sc + wcols[2 * p] * jnp.maximum(_dot_nt(qp, ke), 0.0)
            sc = sc + wcols[2 * p + 1] * jnp.maximum(_dot_nt(qp, ko), 0.0)
        sc = sc * (IDX_DK ** -0.5 * IDX_HEADS ** -0.5)
        sc = jnp.where((kc < qi) | (row >= col), sc, -jnp.inf)
        key = _float_key(sc)
        key_ref[kc] = key
        hi_ref[kc] = (key >> 16).astype(jnp.int16)
        return carry

    lax.fori_loop(0, n_chunks, score_body, 0)

    i16 = jnp.int16
    i16_min = -(2 ** 15)
    one_i = jnp.ones((tq, tq), i16)
    zero_i = jnp.zeros((tq, tq), i16)
    ones_col = jnp.ones((tq, LANES), BF16)

    def wide16(x):
        return jnp.concatenate([x] * (tq // LANES), axis=1).astype(i16)

    def count_ge16(ref16, cand):
        c16 = wide16(cand)

        def body(kc, acc):
            return acc + jnp.where(ref16[kc] >= c16, one_i, zero_i)
        acc = lax.fori_loop(0, n_chunks, body, zero_i)
        return _dot(acc.astype(jnp.float32).astype(BF16), ones_col)

    def search16(ref16, base, need):
        def bit_body(it, carry):
            v, cnt_v = carry
            cand = v + (jnp.int32(1) << (15 - it))
            cnt = base + count_ge16(ref16, cand)
            ok = cnt >= need
            return jnp.where(ok, cand, v), jnp.where(ok, cnt, cnt_v)
        v0 = jnp.full((tq, LANES), i16_min, jnp.int32)
        c0 = jnp.full((tq, LANES), 3.0e38, jnp.float32)
        return lax.fori_loop(0, 16, bit_body, (v0, c0))

    zero_cnt = jnp.zeros((tq, LANES), jnp.float32)
    t_hi, _ = search16(hi_ref, zero_cnt, float(topk))
    n_above = jnp.where(t_hi < 2 ** 15 - 1, count_ge16(hi_ref, jnp.minimum(t_hi + 1, 2 ** 15 - 1)), 0.0)
    t_hi16 = wide16(t_hi)

    def lower_body(kc, carry):
        lo = ((key_ref[kc] & jnp.int32(0xFFFF)) - 2 ** 15).astype(i16)
        lo_ref[kc] = jnp.where(hi_ref[kc] == t_hi16, lo, jnp.full((tq, tq), i16_min, i16))
        return carry

    lax.fori_loop(0, n_chunks, lower_body, 0)
    t_lo, cnt_w = search16(lo_ref, n_above, float(topk))
    thr = ((t_hi << 16) + (t_lo + 2 ** 15))[:, :1]
    cnt_thr = cnt_w[:, :1]
    tied = (cnt_thr > float(topk)) & (thr > jnp.int32(KEY_NEG_INF))
    need_tie = jnp.max(jnp.where(tied, 1.0, 0.0)) > 0.0
    thr = jnp.maximum(thr, jnp.int32(KEY_NEG_INF + 1))

    cut_ref[...] = jnp.full((tq, 1), 2 ** 30, jnp.int32)

    @pl.when(need_tie)
    def _():
        def gt_body(kc, acc):
            g = jnp.where(key_ref[kc] > thr, 1.0, 0.0)
            return acc + jnp.sum(g, axis=-1, keepdims=True)
        n_gt = lax.fori_loop(0, n_chunks, gt_body, jnp.zeros((tq, 1), jnp.float32))
        need = float(topk) - n_gt
        n_bits = max(1, int(t).bit_length())

        def cut_body(it, cut):
            cand = cut + (jnp.int32(1) << (n_bits - 1 - it))

            def eq_body(kc, acc):
                pos = kc * tq + col
                e = jnp.where((key_ref[kc] == thr) & (pos < cand), 1.0, 0.0)
                return acc + jnp.sum(e, axis=-1, keepdims=True)
            n_eq = lax.fori_loop(0, n_chunks, eq_body, jnp.zeros((tq, 1), jnp.float32))
            return jnp.where(n_eq <= need, cand, cut)
        cut_ref[...] = lax.fori_loop(0, n_bits, cut_body, jnp.zeros((tq, 1), jnp.int32))

    cut = cut_ref[...]

    n_far = jnp.maximum(qi - 1, 0)
    rg = SA_GROUP * tq

    def chunk_rows(kc):
        return pl.ds(pl.multiple_of(kc * tq, tq), tq)

    for g in range(SA_KV_HEADS):
        heads = [g * SA_GROUP + j for j in range(SA_GROUP)]
        qg = jnp.concatenate([qs_ref[:, hh * LANES:(hh + 1) * LANES] for hh in heads], axis=0)
        cf_rows = jnp.concatenate([jnp.full((tq, 1), cfar_ref[hh], jnp.float32) for hh in heads], axis=0)

        def scores(kc, kind, qg=qg, heads=heads, g=g):
            key = key_ref[kc]
            sel = (key > thr) | ((key == thr) & (kc * tq + col < cut))
            s_all = _dot_nt(qg, ks_ref[chunk_rows(kc), g * SA_DH:(g + 1) * SA_DH])
            parts = []
            for j, hh in enumerate(heads):
                s = s_all[j * tq:(j + 1) * tq]
                if kind is not None:
                    s = s + tile_ref[hh, kind]
                parts.append(jnp.where(sel, s, NEG))
            s = jnp.concatenate(parts, axis=0)
            s_ref[kc] = s
            mpart_ref[...] = jnp.maximum(mpart_ref[...], _fold_lanes(s, jnp.maximum))

        mpart_ref[...] = jnp.full((rg, LANES), NEG, jnp.float32)
        _chunk_loop(n_far, lambda kc, f=scores: f(kc, None))
        m_far = jnp.max(mpart_ref[...], axis=-1, keepdims=True) + cf_rows
        mpart_ref[...] = jnp.full((rg, LANES), NEG, jnp.float32)

        @pl.when(qi >= 1)
        def _(f=scores):
            f(qi - 1, 1)
            f(qi, 0)

        @pl.when(qi == 0)
        def _(f=scores):
            f(0, 0)

        m = jnp.maximum(m_far, jnp.max(mpart_ref[...], axis=-1, keepdims=True))
        shift_ref[0] = jnp.broadcast_to(m - cf_rows, (rg, LANES))
        shift_ref[1] = jnp.broadcast_to(m, (rg, LANES))
        lpart_ref[...] = jnp.zeros((rg, LANES), jnp.float32)
        acc_ref[...] = jnp.zeros((rg, SA_DH), jnp.float32)

        def weights(kc, which, g=g):
            s = s_ref[kc]
            sh = shift_ref[which]
            ps = [jnp.exp(s[:, c * LANES:(c + 1) * LANES] - sh) for c in range(tq // LANES)]
            tot = ps[0]
            for pc in ps[1:]:
                tot = tot + pc
            lpart_ref[...] = lpart_ref[...] + tot
            p = jnp.concatenate(ps, axis=1).astype(BF16)
            acc_ref[...] = acc_ref[...] + _dot(p, vs_ref[chunk_rows(kc), g * SA_DH:(g + 1) * SA_DH])

        _chunk_loop(n_far, lambda kc, f=weights: f(kc, 0))

        @pl.when(qi >= 1)
        def _(f=weights):
            f(qi - 1, 1)
            f(qi, 1)

        @pl.when(qi == 0)
        def _(f=weights):
            f(0, 1)

        a = acc_ref[...] / jnp.sum(lpart_ref[...], axis=-1, keepdims=True)
        for j, hh in enumerate(heads):
            o_ref[:, hh * SA_DH:(hh + 1) * SA_DH] = a[j * tq:(j + 1) * tq].astype(o_ref.dtype)


def _sa_prompt(q_sa, q_ix, misc_q, misc_k, k_sa, v_sa, tiles, cfar, nb, t, tq, topk):
    n = q_sa.shape[0]
    nq = t // tq
    mw = misc_q.shape[1]
    kvw = SA_KV_HEADS * SA_DH
    once = dict(pipeline_mode=pl.Buffered(1))
    rg = SA_GROUP * tq
    return pl.pallas_call(
        functools.partial(_sa_prompt_kernel, tq=tq, topk=topk),
        grid=(nb, nq),
        in_specs=[pl.BlockSpec(memory_space=pltpu.SMEM),
                  pl.BlockSpec((tq, SA_HEADS * SA_DH), lambda b, i: (b * nq + i, 0)),
                  pl.BlockSpec((tq, IDX_HEADS * IDX_DK), lambda b, i: (b * nq + i, 0)),
                  pl.BlockSpec((tq, mw), lambda b, i: (b * nq + i, 0)),
                  pl.BlockSpec((t, mw), lambda b, i: (b, 0), **once),
                  pl.BlockSpec((t, kvw), lambda b, i: (b, 0), **once),
                  pl.BlockSpec((t, kvw), lambda b, i: (b, 0), **once),
                  pl.BlockSpec((SA_HEADS, 2, tq, tq), lambda b, i: (0, 0, 0, 0), **once)],
        out_specs=pl.BlockSpec((tq, SA_HEADS * SA_DH), lambda b, i: (b * nq + i, 0)),
        out_shape=jax.ShapeDtypeStruct((n, SA_HEADS * SA_DH), BF16),
        scratch_shapes=[pltpu.VMEM((2, t, LANES), BF16),
                        pltpu.VMEM((nq, tq, tq), jnp.int32),
                        pltpu.VMEM((nq, tq, tq), jnp.int16),
                        pltpu.VMEM((nq, tq, tq), jnp.int16),
                        pltpu.VMEM((tq, 1), jnp.int32),
                        pltpu.VMEM((nq, rg, tq), jnp.float32),
                        pltpu.VMEM((rg, LANES), jnp.float32),
                        pltpu.VMEM((2, rg, LANES), jnp.float32),
                        pltpu.VMEM((rg, LANES), jnp.float32),
                        pltpu.VMEM((rg, SA_DH), jnp.float32)],
        compiler_params=_cparams(("arbitrary", "arbitrary")),
        name="sa_prompt",
    )(cfar, q_sa, q_ix, misc_q, misc_k, k_sa, v_sa, tiles)


def _proj_gate_kernel(oda_ref, osa_ref, wpd_ref, wps_ref, gda_ref, gsa_ref, o_ref, wbf_ref):
    @pl.when(pl.program_id(1) == 0)
    def _():
        wbf_ref[0] = wpd_ref[...].astype(BF16)
        wbf_ref[1] = wps_ref[...].astype(BF16)

    a = _dot(oda_ref[...], wbf_ref[0])
    b = _dot(osa_ref[...], wbf_ref[1])
    o_ref[...] = (gda_ref[...].astype(jnp.float32) * a
                  + gsa_ref[...].astype(jnp.float32) * b).astype(o_ref.dtype)


def _proj_gate(o_da, o_sa, w_pd, w_ps, gates, d, tm=1024, tn=512):
    n, kd = o_da.shape
    ks = o_sa.shape[1]
    tn = min(tn, d)
    tm = min(tm, n)
    nj = d // tn
    return pl.pallas_call(
        _proj_gate_kernel,
        grid=(nj, n // tm),
        in_specs=[pl.BlockSpec((tm, kd), lambda j, i: (i, 0)),
                  pl.BlockSpec((tm, ks), lambda j, i: (i, 0)),
                  pl.BlockSpec((kd, tn), lambda j, i: (0, j)),
                  pl.BlockSpec((ks, tn), lambda j, i: (0, j)),
                  pl.BlockSpec((tm, tn), lambda j, i: (i, j)),
                  pl.BlockSpec((tm, tn), lambda j, i: (i, nj + j))],
        out_specs=pl.BlockSpec((tm, tn), lambda j, i: (i, j)),
        out_shape=jax.ShapeDtypeStruct((n, d), BF16),
        scratch_shapes=[pltpu.VMEM((2, kd, tn), BF16)],
        compiler_params=_cparams(("arbitrary", "arbitrary")),
        name="proj_gate",
    )(o_da, o_sa, w_pd, w_ps, gates, gates)


def _mm_resid_kernel(m_ref, w_ref, x_ref, gt_ref, o_ref, wbf_ref):
    @pl.when(pl.program_id(1) == 0)
    def _():
        wbf_ref[...] = w_ref[...].astype(BF16)

    o_ref[...] = x_ref[...] + gt_ref[0] * _dot(m_ref[...], wbf_ref[...])


def _mm_resid(m, w, x, gt, rows_per_batch, tm=1024, tn=1024):
    n, k = m.shape
    d = w.shape[1]
    tn = min(tn, d)
    tm = min(tm, n)
    per = rows_per_batch // tm
    return pl.pallas_call(
        _mm_resid_kernel,
        grid=(d // tn, n // tm),
        in_specs=[pl.BlockSpec((tm, k), lambda j, i: (i, 0)),
                  pl.BlockSpec((k, tn), lambda j, i: (0, j)),
                  pl.BlockSpec((tm, tn), lambda j, i: (i, j)),
                  pl.BlockSpec((1, 1, tn), lambda j, i: (i // per, 0, j))],
        out_specs=pl.BlockSpec((tm, tn), lambda j, i: (i, j)),
        out_shape=jax.ShapeDtypeStruct((n, d), jnp.float32),
        scratch_shapes=[pltpu.VMEM((k, tn), BF16)],
        compiler_params=_cparams(("arbitrary", "arbitrary")),
        name="mm_resid",
    )(m, w, x, gt[:, None, :])


def _norm_router_kernel(x_ref, g_ref, sc_ref, sh_ref, wr_ref, th_ref, tl_ref, h_ref, lg_ref):
    last = pl.num_programs(0) - 1

    @pl.when(pl.program_id(0) < last)
    def _():
        x = x_ref[...]
        y = x * lax.rsqrt(jnp.mean(x * x, axis=-1, keepdims=True) + EPS) * g_ref[...]
        h = y * (1.0 + sc_ref[0]) + sh_ref[0]
        h_ref[...] = h.astype(h_ref.dtype)
        lg_ref[...] = _dot_nt(_bf(wr_ref[...]), _bf(h))

    @pl.when(pl.program_id(0) == last)
    def _():
        h_ref[...] = th_ref[...]
        lg_ref[...] = tl_ref[...]


def _norm_router(x, g, sc, sh, w_router_t, tail_h, tail_lg, rows_per_batch, tm):
    n, d = x.shape
    per = rows_per_batch // tm
    nt = n // tm
    assert tail_h.shape == (tm, d) and tail_lg.shape == (N_EXPERTS, tm)
    row = lambda i: jnp.minimum(i, nt - 1)
    return pl.pallas_call(
        _norm_router_kernel,
        grid=(nt + 1,),
        in_specs=[pl.BlockSpec((tm, d), lambda i: (row(i), 0)),
                  pl.BlockSpec((1, d), lambda i: (0, 0)),
                  pl.BlockSpec((1, 1, d), lambda i: (row(i) // per, 0, 0)),
                  pl.BlockSpec((1, 1, d), lambda i: (row(i) // per, 0, 0)),
                  pl.BlockSpec((N_EXPERTS, d), lambda i: (0, 0)),
                  pl.BlockSpec((tm, d), lambda i: (0, 0)),
                  pl.BlockSpec((N_EXPERTS, tm), lambda i: (0, 0))],
        out_specs=[pl.BlockSpec((tm, d), lambda i: (i, 0)),
                   pl.BlockSpec((N_EXPERTS, tm), lambda i: (0, i))],
        out_shape=[jax.ShapeDtypeStruct((n + tm, d), BF16),
                   jax.ShapeDtypeStruct((N_EXPERTS, n + tm), jnp.float32)],
        compiler_params=_cparams(("arbitrary",)),
        name="norm_router",
    )(x, g.reshape(1, d), sc[:, None, :], sh[:, None, :], w_router_t, tail_h, tail_lg)


def _route_kernel(lg_ref, b_ref, eidx_ref, wts_ref, rank_ref, cnt_ref, *, tn):
    shape = (N_GROUPS, GROUP_SIZE, tn)
    sc = _sigmoid(lg_ref[...])
    biased = sc + b_ref[...]
    e_iota = lax.broadcasted_iota(jnp.int32, shape, 1)
    g_iota3 = lax.broadcasted_iota(jnp.int32, shape, 0)
    flat_iota = g_iota3 * GROUP_SIZE + e_iota
    g_iota = lax.broadcasted_iota(jnp.int32, (N_GROUPS, 1, tn), 0)
    ninf = -jnp.inf

    m1 = jnp.max(biased, axis=1, keepdims=True)
    first = jnp.min(jnp.where(biased == m1, e_iota, GROUP_SIZE), axis=1, keepdims=True)
    m2 = jnp.max(jnp.where(e_iota == first, ninf, biased), axis=1, keepdims=True)
    cur = m1 + m2
    gsel = jnp.zeros((N_GROUPS, 1, tn), jnp.float32)
    for _ in range(TOPK_GROUPS):
        mx = jnp.max(cur, axis=0, keepdims=True)
        idx = jnp.min(jnp.where(cur == mx, g_iota, N_GROUPS), axis=0, keepdims=True)
        hit = g_iota == idx
        gsel = jnp.where(hit, 1.0, gsel)
        cur = jnp.where(hit, ninf, cur)

    cur = jnp.where(gsel > 0.0, biased, ninf)
    hits, ws = [], []
    for k in range(TOP_K):
        mx = jnp.max(jnp.max(cur, axis=1, keepdims=True), axis=0, keepdims=True)
        cand = jnp.where(cur == mx, flat_iota, N_EXPERTS)
        idx = jnp.min(jnp.min(cand, axis=1, keepdims=True), axis=0, keepdims=True)
        hit = flat_iota == idx
        w = jnp.sum(jnp.sum(jnp.where(hit, sc, 0.0), axis=1, keepdims=True), axis=0, keepdims=True)
        eidx_ref[k:k + 1, :] = idx.reshape(1, tn)
        hits.append(hit)
        ws.append(w)
        cur = jnp.where(hit, ninf, cur)
    wsum = ws[0]
    for w in ws[1:]:
        wsum = wsum + w
    for k in range(TOP_K):
        wts_ref[k:k + 1, :] = (ws[k] / wsum * ROUTED_SCALE).reshape(1, tn)

    member = jnp.zeros(shape, jnp.float32)
    for hit in hits:
        member = jnp.where(hit, 1.0, member)
    member2 = member.reshape(N_EXPERTS, tn)
    r = lax.broadcasted_iota(jnp.int32, (tn, tn), 0)
    c = lax.broadcasted_iota(jnp.int32, (tn, tn), 1)
    upper = jnp.where(r < c, 1.0, 0.0).astype(BF16)
    prefix = _dot(member2.astype(BF16), upper).reshape(shape)
    for k in range(TOP_K):
        rk = jnp.sum(jnp.sum(jnp.where(hits[k], prefix, 0.0), axis=1, keepdims=True), axis=0, keepdims=True)
        rank_ref[k:k + 1, :] = rk.reshape(1, tn)
    cnt = jnp.sum(member2, axis=1, keepdims=True)
    cnt_ref[0] = jnp.broadcast_to(cnt, (N_EXPERTS, LANES))


def _route(logits_t, b_router, tn):
    n_pad = logits_t.shape[1]
    nt = n_pad // tn
    lg3 = logits_t.reshape(N_GROUPS, GROUP_SIZE, n_pad)
    b3 = b_router.astype(jnp.float32).reshape(N_GROUPS, GROUP_SIZE, 1)
    row = lambda dt: jax.ShapeDtypeStruct((TOP_K, n_pad), dt)
    return pl.pallas_call(
        functools.partial(_route_kernel, tn=tn),
        grid=(nt,),
        in_specs=[pl.BlockSpec((N_GROUPS, GROUP_SIZE, tn), lambda i: (0, 0, i)),
                  pl.BlockSpec((N_GROUPS, GROUP_SIZE, 1), lambda i: (0, 0, 0))],
        out_specs=[pl.BlockSpec((TOP_K, tn), lambda i: (0, i)),
                   pl.BlockSpec((TOP_K, tn), lambda i: (0, i)),
                   pl.BlockSpec((TOP_K, tn), lambda i: (0, i)),
                   pl.BlockSpec((1, N_EXPERTS, LANES), lambda i: (i, 0, 0))],
        out_shape=[row(jnp.int32), row(jnp.float32), row(jnp.float32),
                   jax.ShapeDtypeStruct((nt, N_EXPERTS, LANES), jnp.float32)],
        compiler_params=_cparams(("arbitrary",)),
        name="route",
    )(lg3, b3)


def _dest_kernel(eidx_ref, rank_ref, base_ref, o_ref, *, tn):
    e_iota = lax.broadcasted_iota(jnp.int32, (N_EXPERTS, tn), 0)
    base = base_ref[0][:, :1]
    for k in range(TOP_K):
        onehot = e_iota == eidx_ref[k:k + 1, :]
        b = jnp.sum(jnp.where(onehot, base, 0.0), axis=0, keepdims=True)
        o_ref[k:k + 1, :] = (b + rank_ref[k:k + 1, :]).astype(jnp.int32)


def _dest(eidx, rank, base, tn):
    n_pad = eidx.shape[1]
    return pl.pallas_call(
        functools.partial(_dest_kernel, tn=tn),
        grid=(n_pad // tn,),
        in_specs=[pl.BlockSpec((TOP_K, tn), lambda i: (0, i)),
                  pl.BlockSpec((TOP_K, tn), lambda i: (0, i)),
                  pl.BlockSpec((1, N_EXPERTS, LANES), lambda i: (i, 0, 0))],
        out_specs=pl.BlockSpec((TOP_K, tn), lambda i: (0, i)),
        out_shape=jax.ShapeDtypeStruct((TOP_K, n_pad), jnp.int32),
        compiler_params=_cparams(("arbitrary",)),
        name="dest",
    )(eidx, rank, base)


def _pack_words(lo_f32, hi_f32):
    lo = lax.shift_right_logical(pltpu.bitcast(lo_f32, jnp.uint32), jnp.uint32(16))
    hi = pltpu.bitcast(hi_f32, jnp.uint32) & jnp.uint32(0xFFFF0000)
    return hi | lo


def _unpack_words(w):
    lo = pltpu.bitcast(lax.shift_left(w, jnp.uint32(16)), jnp.float32)
    hi = pltpu.bitcast(w & jnp.uint32(0xFFFF0000), jnp.float32)
    return lo, hi


def _bf16_exact(x):
    return x.astype(BF16).astype(jnp.float32)


def _dispatch_kernel(pstart_ref, pcnt_ref, h_ref, dest_hbm, xs_hbm, dsm, pk, zrow, sem_d, sem_r,
                     *, tn, nt):
    i = pl.program_id(0)
    half = pk.shape[1]

    @pl.when(i < nt)
    def _():
        cp = pltpu.make_async_copy(dest_hbm.at[pl.ds(i * (TOP_K * tn), TOP_K * tn)], dsm, sem_d)
        cp.start()
        x = h_ref[...]
        pk[...] = _pack_words(x[:, :half].astype(jnp.float32), x[:, half:].astype(jnp.float32))
        cp.wait()

        def body(r, carry):
            for k in range(TOP_K):
                d = dsm[k * tn + r]
                pltpu.make_async_copy(pk.at[pl.ds(r, 1), :], xs_hbm.at[pl.ds(d, 1), :],
                                      sem_r).start(priority=k % 2)
            return carry

        lax.fori_loop(0, tn, body, 0)
        for k in range(TOP_K):
            pltpu.make_async_copy(pk, xs_hbm.at[pl.ds(0, tn), :], sem_r).wait()

    @pl.when(i == nt)
    def _():
        zrow[...] = jnp.zeros(zrow.shape, zrow.dtype)

        def per_expert(e, carry):
            s0 = pstart_ref[e]
            c = pcnt_ref[e]
            head = jnp.minimum(c, (8 - s0 % 8) % 8)
            g0 = s0 + head
            ngrp = (c - head) // 8
            grp = lambda r: pl.ds(pl.multiple_of(g0 + r * 8, 8), 8)

            def start(r, cc):
                pltpu.make_async_copy(zrow.at[pl.ds(0, 1), :], xs_hbm.at[pl.ds(s0 + r, 1), :], sem_r).start()
                return cc

            def wait(r, cc):
                pltpu.make_async_copy(zrow.at[pl.ds(0, 1), :], xs_hbm.at[pl.ds(s0, 1), :], sem_r).wait()
                return cc

            def gstart(r, cc):
                pltpu.make_async_copy(zrow, xs_hbm.at[grp(r), :], sem_r).start()
                return cc

            def gwait(r, cc):
                pltpu.make_async_copy(zrow, xs_hbm.at[grp(0), :], sem_r).wait()
                return cc

            lax.fori_loop(0, head, start, 0)
            lax.fori_loop(0, ngrp, gstart, 0)
            lax.fori_loop(0, head, wait, 0)
            lax.fori_loop(0, ngrp, gwait, 0)
            return carry

        lax.fori_loop(0, N_EXPERTS, per_expert, 0)

        t0 = pstart_ref[N_EXPERTS]
        groups = pcnt_ref[N_EXPERTS]
        rows8 = lambda r: pl.ds(pl.multiple_of(t0 + r * 8, 8), 8)

        def tstart(r, cc):
            pltpu.make_async_copy(zrow, xs_hbm.at[rows8(r), :], sem_r).start()
            return cc

        def twait(r, cc):
            pltpu.make_async_copy(zrow, xs_hbm.at[rows8(0), :], sem_r).wait()
            return cc

        lax.fori_loop(0, groups, tstart, 0)
        lax.fori_loop(0, groups, twait, 0)


def _dispatch(h2, dest_flat, pad_start, pad_cnt, n_rows, tn):
    n_pad, d = h2.shape
    nt = n_pad // tn
    grid_spec = pltpu.PrefetchScalarGridSpec(
        num_scalar_prefetch=2,
        grid=(nt + 1,),
        in_specs=[pl.BlockSpec((tn, d), lambda i, a, b: (jnp.minimum(i, nt - 1), 0)),
                  pl.BlockSpec(memory_space=pl.ANY)],
        out_specs=pl.BlockSpec(memory_space=pl.ANY),
        scratch_shapes=[pltpu.SMEM((TOP_K * tn,), jnp.int32),
                        pltpu.VMEM((tn, d // 2), jnp.uint32),
                        pltpu.VMEM((8, d // 2), jnp.uint32),
                        pltpu.SemaphoreType.DMA(()),
                        pltpu.SemaphoreType.DMA(())])
    return pl.pallas_call(
        functools.partial(_dispatch_kernel, tn=tn, nt=nt),
        grid_spec=grid_spec,
        out_shape=jax.ShapeDtypeStruct((n_rows, d // 2), jnp.uint32),
        compiler_params=_cparams(("arbitrary",)),
        name="dispatch",
    )(pad_start, pad_cnt, h2, dest_flat)


def _expert_kernel(blk_e_ref, nused_ref, next_e_ref, slot_ref, x_ref, wg_hbm, wu_hbm, wd_hbm, y_ref,
                   wg_f, wu_f, wd_f, wgb, wub, wdb, sems):
    i = pl.program_id(0)
    nused = nused_ref[0]
    ii = jnp.minimum(i, nused - 1)
    e = blk_e_ref[ii]
    e_prev = blk_e_ref[jnp.maximum(ii - 1, 0)]
    half = x_ref.shape[1]

    def copies(ex, sl):
        return [pltpu.make_async_copy(src.at[ex], dst.at[sl], sems.at[sl, t])
                for t, (src, dst) in enumerate(((wg_hbm, wg_f), (wu_hbm, wu_f), (wd_hbm, wd_f)))]

    @pl.when(i == 0)
    def _():
        for c in copies(e, slot_ref[e]):
            c.start()

    @pl.when((i < nused) & ((i == 0) | (e != e_prev)))
    def _():
        sl = slot_ref[e]
        for c in copies(e, sl):
            c.wait()
        nxt = next_e_ref[e]

        @pl.when(nxt >= 0)
        def _():
            for c in copies(nxt, 1 - sl):
                c.start()

        wgb[...] = wg_f[sl].astype(BF16)
        wub[...] = wu_f[sl].astype(BF16)
        wdb[...] = wd_f[sl].astype(BF16)

    @pl.when(i < nused)
    def _():
        lo, hi = _unpack_words(x_ref[...])
        xl = lo.astype(BF16)
        xh = hi.astype(BF16)
        g = _dot(xl, wgb[:half, :]) + _dot(xh, wgb[half:, :])
        u = _dot(xl, wub[:half, :]) + _dot(xh, wub[half:, :])
        hmid = (_silu(g) * u).astype(BF16)
        y = _dot(hmid, wdb[...])
        y_ref[...] = _pack_words(_bf16_exact(y[:, :half]), _bf16_exact(y[:, half:]))

    @pl.when(i >= nused)
    def _():
        y_ref[...] = jnp.zeros(y_ref.shape, y_ref.dtype)


def _experts(xs, blk_e, nused, next_e, slot, w_gate, w_up, w_down, tb):
    n_rows, half = xs.shape
    _, d, f = w_gate.shape
    nblk = n_rows // tb

    def xmap(i, be, nu, ne, sl):
        return (jnp.minimum(i, nu[0] - 1), 0)

    grid_spec = pltpu.PrefetchScalarGridSpec(
        num_scalar_prefetch=4,
        grid=(nblk,),
        in_specs=[pl.BlockSpec((tb, half), xmap),
                  pl.BlockSpec(memory_space=pl.ANY),
                  pl.BlockSpec(memory_space=pl.ANY),
                  pl.BlockSpec(memory_space=pl.ANY)],
        out_specs=pl.BlockSpec((tb, half), lambda i, be, nu, ne, sl: (i, 0)),
        scratch_shapes=[pltpu.VMEM((2, d, f), jnp.float32),
                        pltpu.VMEM((2, d, f), jnp.float32),
                        pltpu.VMEM((2, f, d), jnp.float32),
                        pltpu.VMEM((d, f), BF16),
                        pltpu.VMEM((d, f), BF16),
                        pltpu.VMEM((f, d), BF16),
                        pltpu.SemaphoreType.DMA((2, 3))])
    return pl.pallas_call(
        _expert_kernel,
        grid_spec=grid_spec,
        out_shape=jax.ShapeDtypeStruct((n_rows, half), jnp.uint32),
        compiler_params=_cparams(("arbitrary",)),
        name="experts",
    )(blk_e, nused, next_e, slot, xs, w_gate, w_up, w_down)


def _ffn_kernel(h_ref, wg_ref, wu_ref, wd_ref, o_ref, wgb, wub, wdb):
    @pl.when(pl.program_id(0) == 0)
    def _():
        wgb[...] = wg_ref[...].astype(BF16)
        wub[...] = wu_ref[...].astype(BF16)
        wdb[...] = wd_ref[...].astype(BF16)

    x = h_ref[...]
    hmid = (_silu(_dot(x, wgb[...])) * _dot(x, wub[...])).astype(BF16)
    o_ref[...] = _dot(hmid, wdb[...]).astype(o_ref.dtype)


def _ffn_shared(h2, wg, wu, wd, tm):
    n_pad, d = h2.shape
    f = wg.shape[1]
    return pl.pallas_call(
        _ffn_kernel,
        grid=(n_pad // tm,),
        in_specs=[pl.BlockSpec((tm, d), lambda i: (i, 0)),
                  pl.BlockSpec((d, f), lambda i: (0, 0)),
                  pl.BlockSpec((d, f), lambda i: (0, 0)),
                  pl.BlockSpec((f, d), lambda i: (0, 0))],
        out_specs=pl.BlockSpec((tm, d), lambda i: (i, 0)),
        out_shape=jax.ShapeDtypeStruct((n_pad, d), BF16),
        scratch_shapes=[pltpu.VMEM((d, f), BF16),
                        pltpu.VMEM((d, f), BF16),
                        pltpu.VMEM((f, d), BF16)],
        compiler_params=_cparams(("arbitrary",)),
        name="ffn_shared",
    )(h2, wg, wu, wd)


def _combine_kernel(dest_hbm, ys_hbm, wts_ref, sh_ref, x_ref, gt_ref, g_ref, o_ref,
                    dsm, buf, sem_d, sem_r, *, tn, tile0, final_norm):
    i = pl.program_id(0)
    cp = pltpu.make_async_copy(dest_hbm.at[pl.ds((tile0 + i) * (TOP_K * tn), TOP_K * tn)], dsm, sem_d)
    cp.start()
    cp.wait()

    def body(r, carry):
        for k in range(TOP_K):
            d = dsm[k * tn + r]
            pltpu.make_async_copy(ys_hbm.at[pl.ds(d, 1), :], buf.at[k, pl.ds(r, 1), :],
                                  sem_r).start(priority=k % 2)
        return carry

    lax.fori_loop(0, tn, body, 0)
    for k in range(TOP_K):
        pltpu.make_async_copy(ys_hbm.at[pl.ds(0, tn), :], buf.at[k], sem_r).wait()

    half = buf.shape[2]
    wts = wts_ref[...]
    acc_lo = jnp.zeros((tn, half), jnp.float32)
    acc_hi = jnp.zeros((tn, half), jnp.float32)
    for k in range(TOP_K):
        lo, hi = _unpack_words(buf[k])
        wk = wts[:, k:k + 1]
        acc_lo = acc_lo + wk * lo
        acc_hi = acc_hi + wk * hi
    sh = sh_ref[...].astype(jnp.float32)
    gt = gt_ref[0]
    x_lo = x_ref[:, :half] + gt[:, :half] * (acc_lo + sh[:, :half])
    x_hi = x_ref[:, half:] + gt[:, half:] * (acc_hi + sh[:, half:])
    if final_norm:
        ms = (jnp.sum(x_lo * x_lo, axis=-1, keepdims=True)
              + jnp.sum(x_hi * x_hi, axis=-1, keepdims=True)) / (2 * half)
        inv = lax.rsqrt(ms + EPS)
        g = g_ref[...]
        x_lo = x_lo * inv * g[:, :half]
        x_hi = x_hi * inv * g[:, half:]
    o_ref[:, :half] = x_lo
    o_ref[:, half:] = x_hi


def _combine(dest_flat, ys, wts_tok, shared, x1, gt3, g_final, tile0, tn, final_norm):
    rows, d = x1.shape
    nt = rows // tn
    gr = gt3.shape[1]
    per = nt // gt3.shape[0]
    return pl.pallas_call(
        functools.partial(_combine_kernel, tn=tn, tile0=tile0, final_norm=final_norm),
        grid=(nt,),
        in_specs=[pl.BlockSpec(memory_space=pl.ANY),
                  pl.BlockSpec(memory_space=pl.ANY),
                  pl.BlockSpec((tn, TOP_K), lambda i: (tile0 + i, 0)),
                  pl.BlockSpec((tn, d), lambda i: (tile0 + i, 0)),
                  pl.BlockSpec((tn, d), lambda i: (i, 0)),
                  pl.BlockSpec((1, gr, d), lambda i: (i // per, 0, 0)),
                  pl.BlockSpec((1, d), lambda i: (0, 0))],
        out_specs=pl.BlockSpec((tn, d), lambda i: (i, 0)),
        out_shape=jax.ShapeDtypeStruct((rows, d), jnp.float32),
        scratch_shapes=[pltpu.SMEM((TOP_K * tn,), jnp.int32),
                        pltpu.VMEM((TOP_K, tn, d // 2), jnp.uint32),
                        pltpu.SemaphoreType.DMA(()),
                        pltpu.SemaphoreType.DMA(())],
        compiler_params=_cparams(("arbitrary",)),
        name="combine",
    )(dest_flat, ys, wts_tok, shared, x1, gt3, g_final.reshape(1, d))


def _page_specs(shape, n, pg):
    def mk(u):
        return pl.BlockSpec((1,) + shape, lambda b, j, pt: (pt[b, j * pg + u], 0, 0))
    return [mk(u) for u in range(n)]


def _kv_page_specs(kshape, vshape, pg, ns):
    def mk(shape, first):
        def one(u):
            def index(b, j, pt):
                step = jnp.minimum(j, ns - 1) if first else jnp.maximum(j - ns, 0)
                return (pt[b, step * pg + u], 0, 0)
            return pl.BlockSpec((1,) + shape, index)
        return [one(u) for u in range(pg)]
    return mk(kshape, True) + mk(vshape, False)


def _softmax_pages(s_ref):
    s = s_ref[...]
    m = jnp.max(jnp.max(s, axis=0, keepdims=True), axis=2, keepdims=True)
    e = jnp.exp(s - m)
    return e / jnp.sum(jnp.sum(e, axis=0, keepdims=True), axis=2, keepdims=True)


def _da_sample_kernel(pt_ref, lam_ref, q_ref, kn_ref, vn_ref, bl_ref, cf_ref, b0_ref, ex_ref, hm_ref,
                      *rest, pg, n_pages):
    k_refs, v_refs = rest[:pg], rest[pg:2 * pg]
    o_ref, s_ref, a_ref, acc_ref = rest[2 * pg:]
    j = pl.program_id(1)
    ns = n_pages // pg
    r = q_ref.shape[1]
    nh = r // 2
    lane = lax.broadcasted_iota(jnp.int32, (r, PAGE_SIZE), 1)

    @pl.when(j < ns)
    def _():
        q = q_ref[0]
        qb = _bf(q)
        for u in range(pg):
            page = j * pg + u
            s = _dot(qb, _bf(k_refs[u][0]))
            s_ref[page] = s + jnp.where(page == (n_pages - 1), bl_ref[...], cf_ref[...])

        @pl.when(j == 0)
        def _():
            s_new = jnp.sum(_rounded(q) * _rounded(kn_ref[0]), axis=-1, keepdims=True) + b0_ref[...]
            s_ref[n_pages] = jnp.where(lane == 0, s_new, NEG)

    @pl.when(j == ns)
    def _():
        p = _softmax_pages(s_ref)
        a = p[:, :nh, :] - lam_ref[0] * p[:, nh:, :]
        a_ref[...] = _bf(jnp.concatenate([a, jnp.zeros_like(a)], axis=1))
        a_new = a_ref[n_pages][:, 0:1].astype(jnp.float32)
        acc_ref[...] = a_new * _rounded(vn_ref[0])

    @pl.when(j >= ns)
    def _():
        acc = acc_ref[...]
        for u in range(pg):
            page = (j - ns) * pg + u
            pe = _bf(_dot(a_ref[page], ex_ref[...]) * hm_ref[...])
            acc = acc + _dot(pe, _bf(v_refs[u][0]))
        acc_ref[...] = acc

    @pl.when(j == pl.num_programs(1) - 1)
    def _():
        o_ref[0] = acc_ref[...]


def _da_sample(page_table, lam, qbd, knew, vnew, bias_last, cfar, bias0, kt, v2, pg):
    nb, n_pages = page_table.shape
    r, w = qbd.shape[1:]
    rows_v, dv = v2.shape[1:]
    nh = rows_v // PAGE_SIZE
    ns = n_pages // pg
    col = jnp.arange(rows_v, dtype=jnp.int32)
    expand = (col[None, :] // nh == jnp.arange(PAGE_SIZE, dtype=jnp.int32)[:, None]).astype(BF16)
    head_mask = (col[None, :] % nh == jnp.arange(r, dtype=jnp.int32)[:, None]).astype(jnp.float32)
    full = lambda shp: pl.BlockSpec(shp, lambda b, j, pt: (0,) * len(shp))
    per_b = lambda shp: pl.BlockSpec((1,) + shp, lambda b, j, pt: (b, 0, 0))
    grid_spec = pltpu.PrefetchScalarGridSpec(
        num_scalar_prefetch=1,
        grid=(nb, 2 * ns),
        in_specs=[pl.BlockSpec(memory_space=pltpu.SMEM),
                  per_b((r, w)), per_b((1, w)), per_b((r, dv)),
                  full((r, PAGE_SIZE)), full((r, 1)), full((r, 1)),
                  full((PAGE_SIZE, rows_v)), full((r, rows_v))]
                 + _kv_page_specs((w, PAGE_SIZE), (rows_v, dv), pg, ns),
        out_specs=per_b((r, dv)),
        scratch_shapes=[pltpu.VMEM((n_pages + 1, r, PAGE_SIZE), jnp.float32),
                        pltpu.VMEM((n_pages + 1, r, PAGE_SIZE), BF16),
                        pltpu.VMEM((r, dv), jnp.float32)])
    return pl.pallas_call(
        functools.partial(_da_sample_kernel, pg=pg, n_pages=n_pages),
        grid_spec=grid_spec,
        out_shape=jax.ShapeDtypeStruct((nb, r, dv), jnp.float32),
        compiler_params=_cparams(("arbitrary", "arbitrary")),
        name="da_sample",
    )(page_table, lam, qbd, knew, vnew, bias_last, cfar, bias0, expand, head_mask,
      *([kt] * pg), *([v2] * pg))


def _idx_sample_kernel(pt_ref, q_ref, w_ref, kn_ref, *rest, pg, n_pages, topk):
    k_refs = rest[:pg]
    sel_ref, sc_ref = rest[pg:]
    j = pl.program_id(1)
    q = q_ref[0]
    w = _rounded(w_ref[0])
    scale = IDX_DK ** -0.5 * IDX_HEADS ** -0.5
    rows = sc_ref.shape[0]
    lane = lax.broadcasted_iota(jnp.int32, (1, PAGE_SIZE), 1)

    @pl.when(j == 0)
    def _():
        sc_ref[...] = jnp.full(sc_ref.shape, -jnp.inf, jnp.float32)
        d = jnp.maximum(jnp.sum(_rounded(q) * _rounded(kn_ref[0]), axis=-1, keepdims=True), 0.0)
        s_new = jnp.sum(w * _rounded(d), axis=0, keepdims=True) * scale
        sc_ref[n_pages:n_pages + 1, :] = jnp.where(lane == 0, s_new, -jnp.inf)

    qb = _bf(q)
    for u in range(pg):
        d = _rounded(jnp.maximum(_dot(qb, _bf(k_refs[u][0])), 0.0))
        sc_ref[pl.ds(j * pg + u, 1), :] = jnp.sum(w * d, axis=0, keepdims=True) * scale

    @pl.when(j == pl.num_programs(1) - 1)
    def _():
        key = _float_key(sc_ref[...])
        pos = (lax.broadcasted_iota(jnp.int32, key.shape, 0) * PAGE_SIZE
               + lax.broadcasted_iota(jnp.int32, key.shape, 1))

        def bit_body(it, thr):
            cand = thr + (jnp.int32(1) << (31 - it))
            cnt = jnp.sum(jnp.where(key >= cand, 1.0, 0.0))
            return jnp.where(cnt >= float(topk), cand, thr)

        thr = lax.fori_loop(0, 32, bit_body, jnp.int32(INT_MIN))
        thr = jnp.maximum(thr, jnp.int32(KEY_NEG_INF + 1))
        need = float(topk) - jnp.sum(jnp.where(key > thr, 1.0, 0.0))
        n_bits = max(1, int(rows * PAGE_SIZE).bit_length())

        def cut_body(it, cut):
            cand = cut + (jnp.int32(1) << (n_bits - 1 - it))
            n_eq = jnp.sum(jnp.where((key == thr) & (pos < cand), 1.0, 0.0))
            return jnp.where(n_eq <= need, cand, cut)

        cut = lax.fori_loop(0, n_bits, cut_body, jnp.int32(0))
        sel_ref[0] = jnp.where((key > thr) | ((key == thr) & (pos < cut)), 1.0, 0.0)


def _idx_sample(page_table, qix, wix, knew, kc, pg, topk):
    nb, n_pages = page_table.shape
    rows = -(-(n_pages + 1) // 8) * 8
    per_b = lambda shp: pl.BlockSpec((1,) + shp, lambda b, j, pt: (b, 0, 0))
    grid_spec = pltpu.PrefetchScalarGridSpec(
        num_scalar_prefetch=1,
        grid=(nb, n_pages // pg),
        in_specs=[per_b((IDX_HEADS, IDX_DK)), per_b((IDX_HEADS, 1)), per_b((1, IDX_DK))]
                 + _page_specs((IDX_DK, PAGE_SIZE), pg, pg),
        out_specs=per_b((rows, PAGE_SIZE)),
        scratch_shapes=[pltpu.VMEM((rows, PAGE_SIZE), jnp.float32)])
    return pl.pallas_call(
        functools.partial(_idx_sample_kernel, pg=pg, n_pages=n_pages, topk=topk),
        grid_spec=grid_spec,
        out_shape=jax.ShapeDtypeStruct((nb, rows, PAGE_SIZE), jnp.float32),
        compiler_params=_cparams(("arbitrary", "arbitrary")),
        name="idx_sample",
    )(page_table, qix, wix, knew, *([kc] * pg))


def _sa_sample_kernel(pt_ref, q_ref, kn_ref, vn_ref, sel_ref, bl_ref, cf_ref, b0_ref, gm_ref, *rest,
                      pg, n_pages):
    k_refs, v_refs = rest[:pg], rest[pg:2 * pg]
    o_ref, s_ref, p_ref, acc_ref = rest[2 * pg:]
    j = pl.program_id(1)
    ns = n_pages // pg
    scale = SA_DH ** -0.5
    r, cols = gm_ref.shape
    lane = lax.broadcasted_iota(jnp.int32, (r, cols), 1)

    @pl.when(j < ns)
    def _():
        q = q_ref[0]
        qb = _bf(q)
        for u in range(pg):
            page = j * pg + u
            s = _dot_nt(qb, _bf(k_refs[u][0])) * scale
            s = s + jnp.where(page == (n_pages - 1), bl_ref[...], cf_ref[...])
            keep = (sel_ref[0, pl.ds(page, 1), :] > 0.0) & (gm_ref[...] > 0.0)
            s_ref[page] = jnp.where(keep, s, NEG)

        @pl.when(j == 0)
        def _():
            on = sel_ref[0, n_pages:n_pages + 1, 0:1] > 0.0
            s_new = jnp.sum(_rounded(q) * _rounded(kn_ref[0]), axis=-1, keepdims=True) * scale + b0_ref[...]
            s_ref[n_pages] = jnp.where((lane == 0) & on, s_new, NEG)

    @pl.when(j == ns)
    def _():
        p_ref[...] = _bf(_softmax_pages(s_ref))
        acc_ref[...] = p_ref[n_pages][:, 0:1].astype(jnp.float32) * _rounded(vn_ref[0])

    @pl.when(j >= ns)
    def _():
        acc = acc_ref[...]
        for u in range(pg):
            acc = acc + _dot(p_ref[(j - ns) * pg + u], _bf(v_refs[u][0]))
        acc_ref[...] = acc

    @pl.when(j == pl.num_programs(1) - 1)
    def _():
        o_ref[0] = acc_ref[...]


def _sa_sample(page_table, q, knew, vnew, sel2, bias_last2, cfar, bias0, group_mask, k2, v2, pg):
    nb, n_pages = page_table.shape
    r, dh = q.shape[1:]
    srows, cols = sel2.shape[1:]
    ns = n_pages // pg
    full = lambda shp: pl.BlockSpec(shp, lambda b, j, pt: (0,) * len(shp))
    per_b = lambda shp: pl.BlockSpec((1,) + shp, lambda b, j, pt: (b, 0, 0))
    grid_spec = pltpu.PrefetchScalarGridSpec(
        num_scalar_prefetch=1,
        grid=(nb, 2 * ns),
        in_specs=[per_b((r, dh)), per_b((r, dh)), per_b((r, dh)), per_b((srows, cols)),
                  full((r, cols)), full((r, 1)), full((r, 1)), full((r, cols))]
                 + _kv_page_specs((cols, dh), (cols, dh), pg, ns),
        out_specs=per_b((r, dh)),
        scratch_shapes=[pltpu.VMEM((n_pages + 1, r, cols), jnp.float32),
                        pltpu.VMEM((n_pages + 1, r, cols), BF16),
                        pltpu.VMEM((r, dh), jnp.float32)])
    return pl.pallas_call(
        functools.partial(_sa_sample_kernel, pg=pg, n_pages=n_pages),
        grid_spec=grid_spec,
        out_shape=jax.ShapeDtypeStruct((nb, r, dh), jnp.float32),
        compiler_params=_cparams(("arbitrary", "arbitrary")),
        name="sa_sample",
    )(page_table, q, knew, vnew, sel2, bias_last2, cfar, bias0, group_mask, *([k2] * pg), *([v2] * pg))


ROW_TILE = 256
COMBINE_TILE = 128
EXPERT_BLOCK = 512
TAIL_ROWS = 256
PAGES_PER_STEP_DA = 16
PAGES_PER_STEP_SA = 32
PAGES_PER_STEP_IDX = 64


def _rms_rows(x, g):
    return x * lax.rsqrt(jnp.mean(x * x, axis=-1, keepdims=True) + EPS) * g


def _pad_rows(x, rows):
    return jnp.pad(x, ((0, rows - x.shape[0]), (0, 0)))


def _tile_major(dest, tn):
    k, n = dest.shape
    return dest.reshape(k, n // tn, tn).transpose(1, 0, 2).reshape(-1)


def kernel(x_prompt, x_sample, c_prompt, c_sample, cache_da_k, cache_da_v, cache_sa_k, cache_sa_v, cache_idx_k, page_table, rel_bias_table, w_ada, b_ada, g_attn, g_ffn, w_in, lambda_q1, lambda_k1, lambda_q2, lambda_k2, g_subln, w_proj_da, w_proj_sa, w_out, w_router, b_router, w_gate, w_up, w_down, w_sh_gate, w_sh_up, w_sh_down, g_final):
    f32, bf16 = jnp.float32, BF16
    nb, t, d = x_prompt.shape
    nbs, dec_seq, _ = x_sample.shape
    assert dec_seq == 1
    depth = w_in.shape[0]
    n_pages = page_table.shape[1]
    past_len = n_pages * PAGE_SIZE
    n = nb * t
    n_pad = n + TAIL_ROWS
    tq = min(256, t)
    tq_da = min(512, t)
    assert t % tq_da == 0
    assert tq >= MAX_DISTANCE and t % tq == 0 and n % ROW_TILE == 0 and nbs <= TAIL_ROWS
    assert TAIL_ROWS == ROW_TILE and EXPERT_BLOCK % 8 == 0
    topk_p = min(SA_TOPK_MAX, t // 4)
    topk_s = min(SA_TOPK_MAX, (past_len + dec_seq) // 4)
    assert topk_p <= tq

    sizes = [DA_HEADS * 2 * DA_DK, DA_HEADS * 2 * DA_DK, DA_HEADS * DA_DV, SA_HEADS * SA_DH,
             SA_KV_HEADS * SA_DH, SA_KV_HEADS * SA_DH, IDX_HEADS * IDX_DK, IDX_DK, IDX_HEADS, d, d]
    offs = [sum(sizes[:i]) for i in range(len(sizes) + 1)]
    (o_qda, o_kda, o_vda, o_qsa, o_ksa, o_vsa, o_qix, o_kix, o_wix, o_gda, o_gsa, _) = offs
    misc_w = 2 * LANES

    table = rel_bias_table.astype(f32)
    cfar = table[N_BUCKETS - 1]
    tiles_da = _near_tiles(table[:, :DA_HEADS], tq_da)
    tiles_sa = _near_tiles(table[:, DA_HEADS:], tq)
    last_dist = past_len - ((n_pages - 1) * PAGE_SIZE + jnp.arange(PAGE_SIZE, dtype=jnp.int32))
    bias_last = _bias_by_distance(table, last_dist)
    bias0 = _bias_by_distance(table, jnp.zeros((1,), jnp.int32))
    rep2 = lambda a: jnp.repeat(a, 2, axis=0)

    xp = x_prompt.reshape(n, d)
    xs = x_sample.reshape(nbs, d)
    c_all = jnp.concatenate([c_prompt, c_sample], axis=0)
    leaves_p, leaves_s = [], []
    for l in range(depth):
        lam_init = 0.8 - 0.6 * math.exp(-0.3 * l)
        lam = (jnp.exp(jnp.sum(lambda_q1[l].astype(f32) * lambda_k1[l].astype(f32)))
               - jnp.exp(jnp.sum(lambda_q2[l].astype(f32) * lambda_k2[l].astype(f32))) + lam_init)
        mod = _linear_small(c_all, w_ada[l], b_ada[l], silu_in=True)
        sh1, sc1, gt1, sh2, sc2, gt2 = jnp.split(mod[:nb], 6, axis=-1)
        sh1s, sc1s, gt1s, sh2s, sc2s, gt2s = jnp.split(mod[nb:], 6, axis=-1)
        w = w_in[l]

        h = _norm_mod(xp, g_attn[l], sc1, sh1, t, ROW_TILE)
        q_da, = _mm(h, w, o_qda, sizes[0], (bf16,), scale=DA_DK ** -0.5)
        k_da_b, k_da_t = _mm(h, w, o_kda, sizes[1], (bf16,), rows_per_batch=t)
        k_da = k_da_t.reshape(nb, DA_HEADS, 2, DA_DK, t).transpose(0, 4, 1, 2, 3)
        v_da, v_da_b = _mm(h, w, o_vda, sizes[2], (f32, bf16))
        q_sa, = _mm(h, w, o_qsa, sizes[3], (bf16,), scale=SA_DH ** -0.5)
        k_sa, k_sa_b = _mm(h, w, o_ksa, sizes[4], (f32, bf16))
        v_sa, v_sa_b = _mm(h, w, o_vsa, sizes[5], (f32, bf16))
        q_ix, = _mm(h, w, o_qix, sizes[6], (bf16,), tn=512)
        misc, misc_b = _mm(h, w, o_kix, misc_w, (f32, bf16))
        gates, = _mm(h, w[:, o_gda:], 0, 2 * d, (bf16,), sigmoid=True)
        k_ix = misc[:, :IDX_DK]

        o_da = _da_prompt(q_da, k_da_b, v_da_b, tiles_da, cfar[:DA_HEADS], lam.reshape(1),
                          g_subln[l].astype(f32), nb, t, tq_da, 1.0 - lam_init)
        o_sa = _sa_prompt(q_sa, q_ix, misc, misc_b, k_sa_b, v_sa_b, tiles_sa, cfar[DA_HEADS:], nb, t, tq, topk_p)
        m = _proj_gate(o_da, o_sa, w_proj_da[l], w_proj_sa[l], gates, d)
        x1 = _mm_resid(m, w_out[l], xp, gt1, t)

        hs = _rms_rows(xs, g_attn[l].astype(f32)) * (1.0 + sc1s) + sh1s
        ps = _linear_small(hs, w)
        seg = lambda i: ps[:, offs[i]:offs[i + 1]]
        q_da_s, k_da_s, v_da_s, q_sa_s, k_sa_s, v_sa_s, q_ix_s, k_ix_s, w_ix_s, gda_s, gsa_s = [
            seg(i) for i in range(11)]

        r_da = 2 * DA_HEADS
        q16 = (q_da_s * DA_DK ** -0.5).reshape(nbs, DA_HEADS, 2, DA_DK).transpose(0, 2, 1, 3).reshape(nbs, r_da, DA_DK)
        blk_of_row = 2 * (jnp.arange(r_da) % DA_HEADS) + jnp.arange(r_da) // DA_HEADS
        place = (blk_of_row[:, None] == jnp.arange(r_da)[None, :]).astype(f32)
        qbd_da = (place[None, :, :, None] * q16[:, :, None, :]).reshape(nbs, r_da, -1)
        both = lambda a: jnp.concatenate([a, a], axis=0)
        kt_da = cache_da_k[l].transpose(0, 2, 3, 4, 1).reshape(-1, DA_HEADS * 2 * DA_DK, PAGE_SIZE)
        v2_da = cache_da_v[l].reshape(-1, PAGE_SIZE * DA_HEADS, DA_DV)
        v_rows = jnp.pad(v_da_s.reshape(nbs, DA_HEADS, DA_DV), ((0, 0), (0, r_da - DA_HEADS), (0, 0)))
        a_da = _da_sample(page_table, lam.reshape(1), qbd_da, k_da_s[:, None, :], v_rows,
                          both(bias_last[:DA_HEADS]), both(cfar[:DA_HEADS, None]), both(bias0[:DA_HEADS]),
                          kt_da, v2_da, math.gcd(PAGES_PER_STEP_DA, n_pages))
        o_da_s = a_da[:, :DA_HEADS]
        o_da_s = _rms_rows(o_da_s, g_subln[l].astype(f32)) * (1.0 - lam_init)

        sel = _idx_sample(page_table, q_ix_s.reshape(nbs, IDX_HEADS, IDX_DK), w_ix_s[:, :, None],
                          k_ix_s[:, None, :], cache_idx_k[l].transpose(0, 2, 1),
                          math.gcd(PAGES_PER_STEP_IDX, n_pages), topk_s)
        r_sa = 2 * SA_HEADS
        kv_of_row = jnp.minimum(jnp.arange(r_sa) // SA_GROUP, SA_KV_HEADS - 1)
        pad_sa = lambda a: jnp.pad(a, ((0, 0), (0, r_sa - SA_HEADS), (0, 0)))
        rep_kv = lambda a: jnp.repeat(a, SA_KV_HEADS, axis=-1)
        col_kv = jnp.arange(PAGE_SIZE * SA_KV_HEADS) % SA_KV_HEADS
        k2_sa = cache_sa_k[l].reshape(-1, PAGE_SIZE * SA_KV_HEADS, SA_DH)
        v2_sa = cache_sa_v[l].reshape(-1, PAGE_SIZE * SA_KV_HEADS, SA_DH)
        a_sa = _sa_sample(page_table, pad_sa(q_sa_s.reshape(nbs, SA_HEADS, SA_DH)),
                          k_sa_s.reshape(nbs, SA_KV_HEADS, SA_DH)[:, kv_of_row],
                          v_sa_s.reshape(nbs, SA_KV_HEADS, SA_DH)[:, kv_of_row],
                          rep_kv(sel), rep_kv(_pad_rows(bias_last[DA_HEADS:], r_sa)),
                          _pad_rows(cfar[DA_HEADS:, None], r_sa), _pad_rows(bias0[DA_HEADS:], r_sa),
                          (col_kv[None, :] == kv_of_row[:, None]).astype(f32),
                          k2_sa, v2_sa, math.gcd(PAGES_PER_STEP_SA, n_pages))
        o_sa_s = a_sa[:, :SA_HEADS]

        pda = _linear_small(o_da_s.reshape(nbs, -1), w_proj_da[l])
        psa = _linear_small(o_sa_s.reshape(nbs, -1), w_proj_sa[l])
        ms = _sigmoid(gda_s) * pda + _sigmoid(gsa_s) * psa
        x1s = xs + gt1s * _linear_small(ms, w_out[l])
        h2s = _rms_rows(x1s, g_ffn[l].astype(f32)) * (1.0 + sc2s) + sh2s
        lg_s = _linear_small(h2s, w_router[l])

        h2_all, lg_all = _norm_router(x1, g_ffn[l], sc2, sh2, w_router[l].T,
                                      _pad_rows(h2s.astype(bf16), TAIL_ROWS),
                                      _pad_rows(lg_s, TAIL_ROWS).T, t, ROW_TILE)
        eidx, wts, rank, cnt = _route(lg_all, b_router[l], ROW_TILE)
        cnt_tile = cnt[:, :, 0]
        total = jnp.sum(cnt_tile, axis=0)
        padded = jnp.ceil(total / EXPERT_BLOCK) * EXPERT_BLOCK
        pends = jnp.cumsum(padded)
        pstart = pends - padded
        base = pstart[None, :] + jnp.cumsum(cnt_tile, axis=0) - cnt_tile
        dest = _dest(eidx, rank, jnp.broadcast_to(base[:, :, None], base.shape + (LANES,)), ROW_TILE)
        n_blk = -(-(n_pad * TOP_K) // EXPERT_BLOCK) + N_EXPERTS
        blk_start = (jnp.arange(n_blk) * EXPERT_BLOCK).astype(f32)
        blk_e = jnp.minimum(jnp.sum(pends[None, :] <= blk_start[:, None], axis=1), N_EXPERTS - 1).astype(jnp.int32)
        n_used = (pends[-1] / EXPERT_BLOCK).astype(jnp.int32).reshape(1)
        n_rows = n_blk * EXPERT_BLOCK
        pad_start = jnp.concatenate([pstart + total, pends[-1:]]).astype(jnp.int32)
        pad_cnt = jnp.concatenate([padded - total, (n_rows - pends[-1:]) / 8]).astype(jnp.int32)
        xs_sorted = _dispatch(h2_all, _tile_major(dest, ROW_TILE), pad_start, pad_cnt, n_rows, ROW_TILE)
        owns = padded > 0
        e_ids = jnp.arange(N_EXPERTS, dtype=jnp.int32)
        later = jnp.where(owns[None, :] & (e_ids[None, :] > e_ids[:, None]), e_ids[None, :], N_EXPERTS)
        next_owner = jnp.min(later, axis=1)
        next_owner = jnp.where(next_owner < N_EXPERTS, next_owner, -1).astype(jnp.int32)
        slot_of_e = ((jnp.cumsum(owns.astype(jnp.int32)) - 1) % 2).astype(jnp.int32)
        ys = _experts(xs_sorted, blk_e, n_used, next_owner, slot_of_e,
                      w_gate[l], w_up[l], w_down[l], EXPERT_BLOCK)
        shared = _ffn_shared(h2_all, w_sh_gate[l], w_sh_up[l], w_sh_down[l], ROW_TILE)
        dest_c = _tile_major(dest, COMBINE_TILE)
        wts_tok = wts.T
        last = l == depth - 1
        xp = _combine(dest_c, ys, wts_tok, shared, x1, gt2[:, None, :], g_final.astype(f32),
                      0, COMBINE_TILE, last)
        tail = _combine(dest_c, ys, wts_tok, shared, _pad_rows(x1s, TAIL_ROWS),
                        _pad_rows(gt2s, TAIL_ROWS).reshape(-1, COMBINE_TILE, d), g_final.astype(f32),
                        n // COMBINE_TILE, COMBINE_TILE, last)
        xs = tail[:nbs]
        leaves_p.append((k_da, v_da, k_sa, v_sa, k_ix))
        leaves_s.append((k_da_s, v_da_s, k_sa_s, v_sa_s, k_ix_s))

    shapes = [(DA_HEADS, 2, DA_DK), (DA_HEADS, DA_DV), (SA_KV_HEADS, SA_DH), (SA_KV_HEADS, SA_DH), (IDX_DK,)]
    out_p = [jnp.stack([lv[i].reshape((nb, t) + shapes[i]) for lv in leaves_p]) for i in range(5)]
    out_s = [jnp.stack([lv[i].reshape((nbs, dec_seq) + shapes[i]) for lv in leaves_s]) for i in range(5)]
    return (xp.reshape(nb, t, d), xs.reshape(nbs, dec_seq, d), *out_p, *out_s)
```

```python
import functools
import math

import jax
import jax.numpy as jnp
from jax import lax
from jax.experimental import pallas as pl
from jax.experimental.pallas import tpu as pltpu

DA_HEADS = 8
DA_DK = 64
DA_DV = 2 * DA_DK
SA_HEADS = 8
SA_KV_HEADS = 2
SA_DH = 128
SA_GROUP = SA_HEADS // SA_KV_HEADS
IDX_HEADS = 16
IDX_DK = 64
SA_TOPK_MAX = 256
N_BUCKETS = 32
MAX_DISTANCE = 128
N_EXPERTS = 64
N_GROUPS = 8
GROUP_SIZE = N_EXPERTS // N_GROUPS
TOPK_GROUPS = 4
TOP_K = 8
ROUTED_SCALE = 2.5
PAGE_SIZE = 128
EPS = 1e-6

LANES = 128
VMEM_LIMIT = 56 * 1024 * 1024

BF16 = jnp.bfloat16
NEG = -1e30
INT_MIN = -(2 ** 31)
KEY_NEG_INF = (0xFF800000 ^ 0x7FFFFFFF) - (1 << 32)

_NT = (((1,), (1,)), ((), ()))


def _cparams(sem):
    return pltpu.CompilerParams(dimension_semantics=sem, vmem_limit_bytes=VMEM_LIMIT)


def _dot(a, b):
    return jnp.dot(a, b, preferred_element_type=jnp.float32)


def _dot_nt(a, b):
    return lax.dot_general(a, b, _NT, preferred_element_type=jnp.float32)


def _bf(x):
    return x.astype(BF16)


def _rounded(x):
    return x.astype(BF16).astype(jnp.float32)


def _sigmoid(x):
    return 1.0 / (1.0 + jnp.exp(-x))


def _silu(x):
    return x * _sigmoid(x)


def _float_key(s):
    b = pltpu.bitcast(s, jnp.int32)
    return b ^ ((b >> 31) & jnp.int32(0x7FFFFFFF))


def _linear_small_kernel(x_ref, w_ref, b_ref, o_ref, *, silu_in):
    x = x_ref[...]
    if silu_in:
        x = _silu(x)
    o_ref[...] = _dot(_bf(x), _bf(w_ref[...])) + b_ref[...]


def _linear_small(x, w, b=None, *, silu_in=False, tn=512):
    m0, k = x.shape
    m = -(-m0 // 16) * 16
    x = jnp.pad(x, ((0, m - m0), (0, 0)))
    n = w.shape[1]
    tn = min(tn, n)
    if b is None:
        b = jnp.zeros((1, n), jnp.float32)
    out = pl.pallas_call(
        functools.partial(_linear_small_kernel, silu_in=silu_in),
        grid=(pl.cdiv(n, tn),),
        in_specs=[pl.BlockSpec((m, k), lambda j: (0, 0)),
                  pl.BlockSpec((k, tn), lambda j: (0, j)),
                  pl.BlockSpec((1, tn), lambda j: (0, j))],
        out_specs=pl.BlockSpec((m, tn), lambda j: (0, j)),
        out_shape=jax.ShapeDtypeStruct((m, n), jnp.float32),
        compiler_params=_cparams(("arbitrary",)),
        name="linear_small",
    )(x, w, b.reshape(1, n))
    return out[:m0]


def _norm_mod_kernel(x_ref, g_ref, sc_ref, sh_ref, o_ref):
    x = x_ref[...]
    y = x * lax.rsqrt(jnp.mean(x * x, axis=-1, keepdims=True) + EPS) * g_ref[...]
    o_ref[...] = (y * (1.0 + sc_ref[0]) + sh_ref[0]).astype(o_ref.dtype)


def _norm_mod(x, g, sc, sh, rows_per_batch, tm):
    n, d = x.shape
    per = rows_per_batch // tm
    return pl.pallas_call(
        _norm_mod_kernel,
        grid=(n // tm,),
        in_specs=[pl.BlockSpec((tm, d), lambda i: (i, 0)),
                  pl.BlockSpec((1, d), lambda i: (0, 0)),
                  pl.BlockSpec((1, 1, d), lambda i: (i // per, 0, 0)),
                  pl.BlockSpec((1, 1, d), lambda i: (i // per, 0, 0))],
        out_specs=pl.BlockSpec((tm, d), lambda i: (i, 0)),
        out_shape=jax.ShapeDtypeStruct((n, d), BF16),
        compiler_params=_cparams(("arbitrary",)),
        name="norm_mod",
    )(x, g.reshape(1, d), sc[:, None, :], sh[:, None, :])


def _mm_kernel(x_ref, w_ref, *rest, scale, sigmoid, n_out, transposed_out):
    o_refs, wbf_ref = rest[:n_out], rest[-1]

    @pl.when(pl.program_id(1) == 0)
    def _():
        wbf_ref[...] = w_ref[...].astype(BF16)

    acc = _dot(x_ref[...], wbf_ref[...])
    if scale != 1.0:
        acc = acc * scale
    if sigmoid:
        acc = _sigmoid(acc)
    for o in o_refs:
        o[...] = acc.astype(o.dtype)
    if transposed_out:
        rest[n_out][0] = acc.T


def _mm(x, w, col0, ncols, out_dtypes, *, scale=1.0, sigmoid=False, tm=1024, tn=1024, rows_per_batch=None):
    m, k = x.shape
    tn = min(tn, ncols)
    tm = min(tm, m)
    assert col0 % tn == 0 and ncols % tn == 0 and m % tm == 0
    jb = col0 // tn
    out_specs = [pl.BlockSpec((tm, tn), lambda j, i: (i, j)) for _ in out_dtypes]
    out_shape = [jax.ShapeDtypeStruct((m, ncols), dt) for dt in out_dtypes]
    if rows_per_batch is not None:
        per = rows_per_batch // tm
        out_specs.append(pl.BlockSpec((1, tn, tm), lambda j, i: (i // per, j, i % per)))
        out_shape.append(jax.ShapeDtypeStruct((m // rows_per_batch, ncols, rows_per_batch), jnp.float32))
    outs = pl.pallas_call(
        functools.partial(_mm_kernel, scale=scale, sigmoid=sigmoid, n_out=len(out_dtypes),
                          transposed_out=rows_per_batch is not None),
        grid=(ncols // tn, m // tm),
        in_specs=[pl.BlockSpec((tm, k), lambda j, i: (i, 0)),
                  pl.BlockSpec((k, tn), lambda j, i: (0, jb + j))],
        out_specs=out_specs,
        out_shape=out_shape,
        scratch_shapes=[pltpu.VMEM((k, tn), BF16)],
        compiler_params=_cparams(("arbitrary", "arbitrary")),
        name="mm_cols",
    )(x, w)
    return outs


def _rel_bucket(dist):
    max_exact = N_BUCKETS // 2
    d = jnp.maximum(dist, 0)
    large = max_exact + (jnp.log(jnp.maximum(d, 1).astype(jnp.float32) / max_exact)
                         / math.log(MAX_DISTANCE / max_exact)
                         * (N_BUCKETS - max_exact)).astype(jnp.int32)
    large = jnp.minimum(large, N_BUCKETS - 1)
    return jnp.where(d < max_exact, d, large)


def _bias_by_distance(table, dists):
    return table[_rel_bucket(dists)].astype(jnp.float32).T


def _toeplitz_kernel(u_ref, o_ref):
    t = o_ref.shape[2]
    x = jnp.broadcast_to(u_ref[0], (t, 2 * t))
    o_ref[0, 0] = pltpu.roll(x, 0, 1, stride=1, stride_axis=0)[:, :t]


def _near_tiles(table, t):
    nh = table.shape[1]
    k = jnp.arange(2 * t, dtype=jnp.int32)
    gens = []
    for off in (0, t):
        d = jnp.where(k < t, off - k, off + 2 * t - k)
        gens.append(jnp.where(d[None] >= 0, _bias_by_distance(table, d), NEG))
    u = jnp.stack(gens, axis=1).reshape(nh * 2, 1, 2 * t)
    return pl.pallas_call(
        _toeplitz_kernel,
        grid=(nh, 2),
        in_specs=[pl.BlockSpec((1, 1, 2 * t), lambda h, o: (h * 2 + o, 0, 0))],
        out_specs=pl.BlockSpec((1, 1, t, t), lambda h, o: (h, o, 0, 0)),
        out_shape=jax.ShapeDtypeStruct((nh, 2, t, t), jnp.float32),
        compiler_params=_cparams(("arbitrary", "arbitrary")),
        name="bias_tiles",
    )(u)


def _fold_lanes(x, op):
    out = x[:, :LANES]
    for c in range(1, x.shape[1] // LANES):
        out = op(out, x[:, c * LANES:(c + 1) * LANES])
    return out


def _chunk_loop(n, fn):
    def body(i, carry):
        for u in range(4):
            fn(4 * i + u)
        return carry

    lax.fori_loop(0, n // 4, body, 0)
    base = (n // 4) * 4

    @pl.when(n % 4 >= 2)
    def _():
        fn(base)
        fn(base + 1)

    @pl.when(n % 2 == 1)
    def _():
        fn(n - 1)


def _da_prompt_kernel(cfar_ref, lam_ref, q_ref, k_ref, v_ref, tile_ref, g_ref, o_ref,
                      s_ref, mpart_ref, shift_ref, lpart_ref, acc_ref, *, tq, out_scale):
    h = pl.program_id(1)
    qi = pl.program_id(2)
    q = q_ref[...]
    lane = lax.broadcasted_iota(jnp.int32, q.shape, 1)
    zero = jnp.zeros_like(q)
    q2 = jnp.concatenate([jnp.where(lane < DA_DK, q, zero), jnp.where(lane >= DA_DK, q, zero)], axis=0)
    cfar = cfar_ref[h]
    n_far = jnp.maximum(qi - 1, 0)
    r2 = 2 * tq

    def chunk_rows(kc):
        return pl.ds(pl.multiple_of(kc * tq, tq), tq)

    def scores(kc, bias):
        s = _dot_nt(q2, k_ref[chunk_rows(kc), :])
        if bias is not None:
            s = s + jnp.concatenate([bias, bias], axis=0)
        s_ref[kc] = s
        mpart_ref[...] = jnp.maximum(mpart_ref[...], _fold_lanes(s, jnp.maximum))

    mpart_ref[...] = jnp.full((r2, LANES), NEG, jnp.float32)
    _chunk_loop(n_far, lambda kc: scores(kc, None))
    m_far = jnp.max(mpart_ref[...], axis=-1, keepdims=True) + cfar
    mpart_ref[...] = jnp.full((r2, LANES), NEG, jnp.float32)

    @pl.when(qi >= 1)
    def _():
        scores(qi - 1, tile_ref[0, 1])
        scores(qi, tile_ref[0, 0])

    @pl.when(qi == 0)
    def _():
        scores(0, tile_ref[0, 0])

    m = jnp.maximum(m_far, jnp.max(mpart_ref[...], axis=-1, keepdims=True))
    shift_ref[0] = jnp.broadcast_to(m - cfar, (r2, LANES))
    shift_ref[1] = jnp.broadcast_to(m, (r2, LANES))

    lpart_ref[...] = jnp.zeros((r2, LANES), jnp.float32)
    acc_ref[...] = jnp.zeros((r2, DA_DV), jnp.float32)

    def weights(kc, which):
        s = s_ref[kc]
        sh = shift_ref[which]
        ps = [jnp.exp(s[:, c * LANES:(c + 1) * LANES] - sh) for c in range(tq // LANES)]
        tot = ps[0]
        for pc in ps[1:]:
            tot = tot + pc
        lpart_ref[...] = lpart_ref[...] + tot
        p = jnp.concatenate(ps, axis=1).astype(BF16)
        acc_ref[...] = acc_ref[...] + _dot(p, v_ref[chunk_rows(kc), :])

    _chunk_loop(n_far, lambda kc: weights(kc, 0))

    @pl.when(qi >= 1)
    def _():
        weights(qi - 1, 1)
        weights(qi, 1)

    @pl.when(qi == 0)
    def _():
        weights(0, 1)

    lam = lam_ref[0]
    a = acc_ref[...] / jnp.sum(lpart_ref[...], axis=-1, keepdims=True)
    o = a[:tq] - lam * a[tq:]
    o = o * lax.rsqrt(jnp.mean(o * o, axis=-1, keepdims=True) + EPS) * g_ref[...]
    o_ref[...] = (o * out_scale).astype(o_ref.dtype)


def _da_prompt(q, k, v, tiles, cfar, lam, g_subln, nb, t, tq, out_scale):
    n = q.shape[0]
    nq = t // tq
    grid_spec = pltpu.PrefetchScalarGridSpec(
        num_scalar_prefetch=0,
        grid=(nb, DA_HEADS, nq),
        in_specs=[pl.BlockSpec(memory_space=pltpu.SMEM),
                  pl.BlockSpec(memory_space=pltpu.SMEM),
                  pl.BlockSpec((tq, LANES), lambda b, h, i: (b * nq + i, h)),
                  pl.BlockSpec((t, LANES), lambda b, h, i: (b, h)),
                  pl.BlockSpec((t, LANES), lambda b, h, i: (b, h)),
                  pl.BlockSpec((1, 2, tq, tq), lambda b, h, i: (h, 0, 0, 0)),
                  pl.BlockSpec((1, DA_DV), lambda b, h, i: (0, 0))],
        out_specs=pl.BlockSpec((tq, LANES), lambda b, h, i: (b * nq + i, h)),
        scratch_shapes=[pltpu.VMEM((nq, 2 * tq, tq), jnp.float32),
                        pltpu.VMEM((2 * tq, LANES), jnp.float32),
                        pltpu.VMEM((2, 2 * tq, LANES), jnp.float32),
                        pltpu.VMEM((2 * tq, LANES), jnp.float32),
                        pltpu.VMEM((2 * tq, DA_DV), jnp.float32)])
    return pl.pallas_call(
        functools.partial(_da_prompt_kernel, tq=tq, out_scale=out_scale),
        grid_spec=grid_spec,
        out_shape=jax.ShapeDtypeStruct((n, DA_HEADS * DA_DV), BF16),
        compiler_params=_cparams(("arbitrary", "arbitrary", "arbitrary")),
        name="da_prompt",
    )(cfar, lam, q, k, v, tiles, g_subln.reshape(1, DA_DV))


def _sa_prompt_kernel(cfar_ref, qs_ref, qx_ref, mq_ref, mk_ref, ks_ref, vs_ref, tile_ref, o_ref,
                      k2_ref, key_ref, hi_ref, lo_ref, cut_ref, s_ref, mpart_ref, shift_ref, lpart_ref,
                      acc_ref, *, tq, topk):
    qi = pl.program_id(1)
    n_chunks = qi + 1
    t = mk_ref.shape[0]

    @pl.when(qi == 0)
    def _():
        kix = mk_ref[:, :LANES].astype(jnp.float32)
        lane = lax.broadcasted_iota(jnp.int32, kix.shape, 1)
        k2_ref[0] = jnp.where(lane < IDX_DK, kix, 0.0).astype(BF16)
        k2_ref[1] = jnp.where(lane >= IDX_DK, pltpu.roll(kix, IDX_DK, axis=1), 0.0).astype(BF16)

    wix = mq_ref[:, IDX_DK:IDX_DK + IDX_HEADS]
    wcols = [wix[:, hh:hh + 1] for hh in range(IDX_HEADS)]
    row = lax.broadcasted_iota(jnp.int32, (tq, tq), 0)
    col = lax.broadcasted_iota(jnp.int32, (tq, tq), 1)

    def score_body(kc, carry):
        rows = pl.ds(pl.multiple_of(kc * tq, tq), tq)
        ke = k2_ref[0, rows, :]
        ko = k2_ref[1, rows, :]
        sc = jnp.zeros((tq, tq), jnp.float32)
        for p in range(IDX_HEADS // 2):
            qp = qx_ref[:, p * LANES:(p + 1) * LANES]
            sc = sc + wcols[2 * p] * jnp.maximum(_dot_nt(qp, ke), 0.0)
            sc = sc + wcols[2 * p + 1] * jnp.maximum(_dot_nt(qp, ko), 0.0)
        sc = sc * (IDX_DK ** -0.5 * IDX_HEADS ** -0.5)
        sc = jnp.where((kc < qi) | (row >= col), sc, -jnp.inf)
        key = _float_key(sc)
        key_ref[kc] = key
        hi_ref[kc] = (key >> 16).astype(jnp.int16)
        return carry

    lax.fori_loop(0, n_chunks, score_body, 0)

    i16 = jnp.int16
    i16_min = -(2 ** 15)
    one_i = jnp.ones((tq, tq), i16)
    zero_i = jnp.zeros((tq, tq), i16)
    ones_col = jnp.ones((tq, LANES), BF16)

    def wide16(x):
        return jnp.concatenate([x] * (tq // LANES), axis=1).astype(i16)

    def count_ge16(ref16, cand):
        c16 = wide16(cand)

        def body(kc, acc):
            return acc + jnp.where(ref16[kc] >= c16, one_i, zero_i)
        acc = lax.fori_loop(0, n_chunks, body, zero_i)
        return _dot(acc.astype(jnp.float32).astype(BF16), ones_col)

    def search16(ref16, base, need):
        def bit_body(it, carry):
            v, cnt_v = carry
            cand = v + (jnp.int32(1) << (15 - it))
            cnt = base + count_ge16(ref16, cand)
            ok = cnt >= need
            return jnp.where(ok, cand, v), jnp.where(ok, cnt, cnt_v)
        v0 = jnp.full((tq, LANES), i16_min, jnp.int32)
        c0 = jnp.full((tq, LANES), 3.0e38, jnp.float32)
        return lax.fori_loop(0, 16, bit_body, (v0, c0))

    zero_cnt = jnp.zeros((tq, LANES), jnp.float32)
    t_hi, _ = search16(hi_ref, zero_cnt, float(topk))
    n_above = jnp.where(t_hi < 2 ** 15 - 1, count_ge16(hi_ref, jnp.minimum(t_hi + 1, 2 ** 15 - 1)), 0.0)
    t_hi16 = wide16(t_hi)

    def lower_body(kc, carry):
        lo = ((key_ref[kc] & jnp.int32(0xFFFF)) - 2 ** 15).astype(i16)
        lo_ref[kc] = jnp.where(hi_ref[kc] == t_hi16, lo, jnp.full((tq, tq), i16_min, i16))
        return carry

    lax.fori_loop(0, n_chunks, lower_body, 0)
    t_lo, cnt_w = search16(lo_ref, n_above, float(topk))
    thr = ((t_hi << 16) + (t_lo + 2 ** 15))[:, :1]
    cnt_thr = cnt_w[:, :1]
    tied = (cnt_thr > float(topk)) & (thr > jnp.int32(KEY_NEG_INF))
    need_tie = jnp.max(jnp.where(tied, 1.0, 0.0)) > 0.0
    thr = jnp.maximum(thr, jnp.int32(KEY_NEG_INF + 1))

    cut_ref[...] = jnp.full((tq, 1), 2 ** 30, jnp.int32)

    @pl.when(need_tie)
    def _():
        def gt_body(kc, acc):
            g = jnp.where(key_ref[kc] > thr, 1.0, 0.0)
            return acc + jnp.sum(g, axis=-1, keepdims=True)
        n_gt = lax.fori_loop(0, n_chunks, gt_body, jnp.zeros((tq, 1), jnp.float32))
        need = float(topk) - n_gt
        n_bits = max(1, int(t).bit_length())

        def cut_body(it, cut):
            cand = cut + (jnp.int32(1) << (n_bits - 1 - it))

            def eq_body(kc, acc):
                pos = kc * tq + col
                e = jnp.where((key_ref[kc] == thr) & (pos < cand), 1.0, 0.0)
                return acc + jnp.sum(e, axis=-1, keepdims=True)
            n_eq = lax.fori_loop(0, n_chunks, eq_body, jnp.zeros((tq, 1), jnp.float32))
            return jnp.where(n_eq <= need, cand, cut)
        cut_ref[...] = lax.fori_loop(0, n_bits, cut_body, jnp.zeros((tq, 1), jnp.int32))

    cut = cut_ref[...]

    n_far = jnp.maximum(qi - 1, 0)
    rg = SA_GROUP * tq

    def chunk_rows(kc):
        return pl.ds(pl.multiple_of(kc * tq, tq), tq)

    for g in range(SA_KV_HEADS):
        heads = [g * SA_GROUP + j for j in range(SA_GROUP)]
        qg = jnp.concatenate([qs_ref[:, hh * LANES:(hh + 1) * LANES] for hh in heads], axis=0)
        cf_rows = jnp.concatenate([jnp.full((tq, 1), cfar_ref[hh], jnp.float32) for hh in heads], axis=0)

        def scores(kc, kind, qg=qg, heads=heads, g=g):
            key = key_ref[kc]
            sel = (key > thr) | ((key == thr) & (kc * tq + col < cut))
            s_all = _dot_nt(qg, ks_ref[chunk_rows(kc), g * SA_DH:(g + 1) * SA_DH])
            parts = []
            for j, hh in enumerate(heads):
                s = s_all[j * tq:(j + 1) * tq]
                if kind is not None:
                    s = s + tile_ref[hh, kind]
                parts.append(jnp.where(sel, s, NEG))
            s = jnp.concatenate(parts, axis=0)
            s_ref[kc] = s
            mpart_ref[...] = jnp.maximum(mpart_ref[...], _fold_lanes(s, jnp.maximum))

        mpart_ref[...] = jnp.full((rg, LANES), NEG, jnp.float32)
        _chunk_loop(n_far, lambda kc, f=scores: f(kc, None))
        m_far = jnp.max(mpart_ref[...], axis=-1, keepdims=True) + cf_rows
        mpart_ref[...] = jnp.full((rg, LANES), NEG, jnp.float32)

        @pl.when(qi >= 1)
        def _(f=scores):
            f(qi - 1, 1)
            f(qi, 0)

        @pl.when(qi == 0)
        def _(f=scores):
            f(0, 0)

        m = jnp.maximum(m_far, jnp.max(mpart_ref[...], axis=-1, keepdims=True))
        shift_ref[0] = jnp.broadcast_to(m - cf_rows, (rg, LANES))
        shift_ref[1] = jnp.broadcast_to(m, (rg, LANES))
        lpart_ref[...] = jnp.zeros((rg, LANES), jnp.float32)
        acc_ref[...] = jnp.zeros((rg, SA_DH), jnp.float32)

        def weights(kc, which, g=g):
            s = s_ref[kc]
            sh = shift_ref[which]
            ps = [jnp.exp(s[:, c * LANES:(c + 1) * LANES] - sh) for c in range(tq // LANES)]
            tot = ps[0]
            for pc in ps[1:]:
                tot = tot + pc
            lpart_ref[...] = lpart_ref[...] + tot
            p = jnp.concatenate(ps, axis=1).astype(BF16)
            acc_ref[...] = acc_ref[...] + _dot(p, vs_ref[chunk_rows(kc), g * SA_DH:(g + 1) * SA_DH])

        _chunk_loop(n_far, lambda kc, f=weights: f(kc, 0))

        @pl.when(qi >= 1)
        def _(f=weights):
            f(qi - 1, 1)
            f(qi, 1)

        @pl.when(qi == 0)
        def _(f=weights):
            f(0, 1)

        a = acc_ref[...] / jnp.sum(lpart_ref[...], axis=-1, keepdims=True)
        for j, hh in enumerate(heads):
            o_ref[:, hh * SA_DH:(hh + 1) * SA_DH] = a[j * tq:(j + 1) * tq].astype(o_ref.dtype)


def _sa_prompt(q_sa, q_ix, misc_q, misc_k, k_sa, v_sa, tiles, cfar, nb, t, tq, topk):
    n = q_sa.shape[0]
    nq = t // tq
    mw = misc_q.shape[1]
    kvw = SA_KV_HEADS * SA_DH
    once = dict(pipeline_mode=pl.Buffered(1))
    rg = SA_GROUP * tq
    return pl.pallas_call(
        functools.partial(_sa_prompt_kernel, tq=tq, topk=topk),
        grid=(nb, nq),
        in_specs=[pl.BlockSpec(memory_space=pltpu.SMEM),
                  pl.BlockSpec((tq, SA_HEADS * SA_DH), lambda b, i: (b * nq + i, 0)),
                  pl.BlockSpec((tq, IDX_HEADS * IDX_DK), lambda b, i: (b * nq + i, 0)),
                  pl.BlockSpec((tq, mw), lambda b, i: (b * nq + i, 0)),
                  pl.BlockSpec((t, mw), lambda b, i: (b, 0), **once),
                  pl.BlockSpec((t, kvw), lambda b, i: (b, 0), **once),
                  pl.BlockSpec((t, kvw), lambda b, i: (b, 0), **once),
                  pl.BlockSpec((SA_HEADS, 2, tq, tq), lambda b, i: (0, 0, 0, 0), **once)],
        out_specs=pl.BlockSpec((tq, SA_HEADS * SA_DH), lambda b, i: (b * nq + i, 0)),
        out_shape=jax.ShapeDtypeStruct((n, SA_HEADS * SA_DH), BF16),
        scratch_shapes=[pltpu.VMEM((2, t, LANES), BF16),
                        pltpu.VMEM((nq, tq, tq), jnp.int32),
                        pltpu.VMEM((nq, tq, tq), jnp.int16),
                        pltpu.VMEM((nq, tq, tq), jnp.int16),
                        pltpu.VMEM((tq, 1), jnp.int32),
                        pltpu.VMEM((nq, rg, tq), jnp.float32),
                        pltpu.VMEM((rg, LANES), jnp.float32),
                        pltpu.VMEM((2, rg, LANES), jnp.float32),
                        pltpu.VMEM((rg, LANES), jnp.float32),
                        pltpu.VMEM((rg, SA_DH), jnp.float32)],
        compiler_params=_cparams(("arbitrary", "arbitrary")),
        name="sa_prompt",
    )(cfar, q_sa, q_ix, misc_q, misc_k, k_sa, v_sa, tiles)


def _proj_gate_kernel(oda_ref, osa_ref, wpd_ref, wps_ref, gda_ref, gsa_ref, o_ref, wbf_ref):
    @pl.when(pl.program_id(1) == 0)
    def _():
        wbf_ref[0] = wpd_ref[...].astype(BF16)
        wbf_ref[1] = wps_ref[...].astype(BF16)

    a = _dot(oda_ref[...], wbf_ref[0])
    b = _dot(osa_ref[...], wbf_ref[1])
    o_ref[...] = (gda_ref[...].astype(jnp.float32) * a
                  + gsa_ref[...].astype(jnp.float32) * b).astype(o_ref.dtype)


def _proj_gate(o_da, o_sa, w_pd, w_ps, gates, d, tm=1024, tn=512):
    n, kd = o_da.shape
    ks = o_sa.shape[1]
    tn = min(tn, d)
    tm = min(tm, n)
    nj = d // tn
    return pl.pallas_call(
        _proj_gate_kernel,
        grid=(nj, n // tm),
        in_specs=[pl.BlockSpec((tm, kd), lambda j, i: (i, 0)),
                  pl.BlockSpec((tm, ks), lambda j, i: (i, 0)),
                  pl.BlockSpec((kd, tn), lambda j, i: (0, j)),
                  pl.BlockSpec((ks, tn), lambda j, i: (0, j)),
                  pl.BlockSpec((tm, tn), lambda j, i: (i, j)),
                  pl.BlockSpec((tm, tn), lambda j, i: (i, nj + j))],
        out_specs=pl.BlockSpec((tm, tn), lambda j, i: (i, j)),
        out_shape=jax.ShapeDtypeStruct((n, d), BF16),
        scratch_shapes=[pltpu.VMEM((2, kd, tn), BF16)],
        compiler_params=_cparams(("arbitrary", "arbitrary")),
        name="proj_gate",
    )(o_da, o_sa, w_pd, w_ps, gates, gates)


def _mm_resid_kernel(m_ref, w_ref, x_ref, gt_ref, o_ref, wbf_ref):
    @pl.when(pl.program_id(1) == 0)
    def _():
        wbf_ref[...] = w_ref[...].astype(BF16)

    o_ref[...] = x_ref[...] + gt_ref[0] * _dot(m_ref[...], wbf_ref[...])


def _mm_resid(m, w, x, gt, rows_per_batch, tm=1024, tn=1024):
    n, k = m.shape
    d = w.shape[1]
    tn = min(tn, d)
    tm = min(tm, n)
    per = rows_per_batch // tm
    return pl.pallas_call(
        _mm_resid_kernel,
        grid=(d // tn, n // tm),
        in_specs=[pl.BlockSpec((tm, k), lambda j, i: (i, 0)),
                  pl.BlockSpec((k, tn), lambda j, i: (0, j)),
                  pl.BlockSpec((tm, tn), lambda j, i: (i, j)),
                  pl.BlockSpec((1, 1, tn), lambda j, i: (i // per, 0, j))],
        out_specs=pl.BlockSpec((tm, tn), lambda j, i: (i, j)),
        out_shape=jax.ShapeDtypeStruct((n, d), jnp.float32),
        scratch_shapes=[pltpu.VMEM((k, tn), BF16)],
        compiler_params=_cparams(("arbitrary", "arbitrary")),
        name="mm_resid",
    )(m, w, x, gt[:, None, :])


def _norm_router_kernel(x_ref, g_ref, sc_ref, sh_ref, wr_ref, th_ref, tl_ref, h_ref, lg_ref):
    last = pl.num_programs(0) - 1

    @pl.when(pl.program_id(0) < last)
    def _():
        x = x_ref[...]
        y = x * lax.rsqrt(jnp.mean(x * x, axis=-1, keepdims=True) + EPS) * g_ref[...]
        h = y * (1.0 + sc_ref[0]) + sh_ref[0]
        h_ref[...] = h.astype(h_ref.dtype)
        lg_ref[...] = _dot_nt(_bf(wr_ref[...]), _bf(h))

    @pl.when(pl.program_id(0) == last)
    def _():
        h_ref[...] = th_ref[...]
        lg_ref[...] = tl_ref[...]


def _norm_router(x, g, sc, sh, w_router_t, tail_h, tail_lg, rows_per_batch, tm):
    n, d = x.shape
    per = rows_per_batch // tm
    nt = n // tm
    assert tail_h.shape == (tm, d) and tail_lg.shape == (N_EXPERTS, tm)
    row = lambda i: jnp.minimum(i, nt - 1)
    return pl.pallas_call(
        _norm_router_kernel,
        grid=(nt + 1,),
        in_specs=[pl.BlockSpec((tm, d), lambda i: (row(i), 0)),
                  pl.BlockSpec((1, d), lambda i: (0, 0)),
                  pl.BlockSpec((1, 1, d), lambda i: (row(i) // per, 0, 0)),
                  pl.BlockSpec((1, 1, d), lambda i: (row(i) // per, 0, 0)),
                  pl.BlockSpec((N_EXPERTS, d), lambda i: (0, 0)),
                  pl.BlockSpec((tm, d), lambda i: (0, 0)),
                  pl.BlockSpec((N_EXPERTS, tm), lambda i: (0, 0))],
        out_specs=[pl.BlockSpec((tm, d), lambda i: (i, 0)),
                   pl.BlockSpec((N_EXPERTS, tm), lambda i: (0, i))],
        out_shape=[jax.ShapeDtypeStruct((n + tm, d), BF16),
                   jax.ShapeDtypeStruct((N_EXPERTS, n + tm), jnp.float32)],
        compiler_params=_cparams(("arbitrary",)),
        name="norm_router",
    )(x, g.reshape(1, d), sc[:, None, :], sh[:, None, :], w_router_t, tail_h, tail_lg)


def _route_kernel(lg_ref, b_ref, eidx_ref, wts_ref, rank_ref, cnt_ref, *, tn):
    shape = (N_GROUPS, GROUP_SIZE, tn)
    sc = _sigmoid(lg_ref[...])
    biased = sc + b_ref[...]
    e_iota = lax.broadcasted_iota(jnp.int32, shape, 1)
    g_iota3 = lax.broadcasted_iota(jnp.int32, shape, 0)
    flat_iota = g_iota3 * GROUP_SIZE + e_iota
    g_iota = lax.broadcasted_iota(jnp.int32, (N_GROUPS, 1, tn), 0)
    ninf = -jnp.inf

    m1 = jnp.max(biased, axis=1, keepdims=True)
    first = jnp.min(jnp.where(biased == m1, e_iota, GROUP_SIZE), axis=1, keepdims=True)
    m2 = jnp.max(jnp.where(e_iota == first, ninf, biased), axis=1, keepdims=True)
    cur = m1 + m2
    gsel = jnp.zeros((N_GROUPS, 1, tn), jnp.float32)
    for _ in range(TOPK_GROUPS):
        mx = jnp.max(cur, axis=0, keepdims=True)
        idx = jnp.min(jnp.where(cur == mx, g_iota, N_GROUPS), axis=0, keepdims=True)
        hit = g_iota == idx
        gsel = jnp.where(hit, 1.0, gsel)
        cur = jnp.where(hit, ninf, cur)

    cur = jnp.where(gsel > 0.0, biased, ninf)
    hits, ws = [], []
    for k in range(TOP_K):
        mx = jnp.max(jnp.max(cur, axis=1, keepdims=True), axis=0, keepdims=True)
        cand = jnp.where(cur == mx, flat_iota, N_EXPERTS)
        idx = jnp.min(jnp.min(cand, axis=1, keepdims=True), axis=0, keepdims=True)
        hit = flat_iota == idx
        w = jnp.sum(jnp.sum(jnp.where(hit, sc, 0.0), axis=1, keepdims=True), axis=0, keepdims=True)
        eidx_ref[k:k + 1, :] = idx.reshape(1, tn)
        hits.append(hit)
        ws.append(w)
        cur = jnp.where(hit, ninf, cur)
    wsum = ws[0]
    for w in ws[1:]:
        wsum = wsum + w
    for k in range(TOP_K):
        wts_ref[k:k + 1, :] = (ws[k] / wsum * ROUTED_SCALE).reshape(1, tn)

    member = jnp.zeros(shape, jnp.float32)
    for hit in hits:
        member = jnp.where(hit, 1.0, member)
    member2 = member.reshape(N_EXPERTS, tn)
    r = lax.broadcasted_iota(jnp.int32, (tn, tn), 0)
    c = lax.broadcasted_iota(jnp.int32, (tn, tn), 1)
    upper = jnp.where(r < c, 1.0, 0.0).astype(BF16)
    prefix = _dot(member2.astype(BF16), upper).reshape(shape)
    for k in range(TOP_K):
        rk = jnp.sum(jnp.sum(jnp.where(hits[k], prefix, 0.0), axis=1, keepdims=True), axis=0, keepdims=True)
        rank_ref[k:k + 1, :] = rk.reshape(1, tn)
    cnt = jnp.sum(member2, axis=1, keepdims=True)
    cnt_ref[0] = jnp.broadcast_to(cnt, (N_EXPERTS, LANES))


def _route(logits_t, b_router, tn):
    n_pad = logits_t.shape[1]
    nt = n_pad // tn
    lg3 = logits_t.reshape(N_GROUPS, GROUP_SIZE, n_pad)
    b3 = b_router.astype(jnp.float32).reshape(N_GROUPS, GROUP_SIZE, 1)
    row = lambda dt: jax.ShapeDtypeStruct((TOP_K, n_pad), dt)
    return pl.pallas_call(
        functools.partial(_route_kernel, tn=tn),
        grid=(nt,),
        in_specs=[pl.BlockSpec((N_GROUPS, GROUP_SIZE, tn), lambda i: (0, 0, i)),
                  pl.BlockSpec((N_GROUPS, GROUP_SIZE, 1), lambda i: (0, 0, 0))],
        out_specs=[pl.BlockSpec((TOP_K, tn), lambda i: (0, i)),
                   pl.BlockSpec((TOP_K, tn), lambda i: (0, i)),
                   pl.BlockSpec((TOP_K, tn), lambda i: (0, i)),
                   pl.BlockSpec((1, N_EXPERTS, LANES), lambda i: (i, 0, 0))],
        out_shape=[row(jnp.int32), row(jnp.float32), row(jnp.float32),
                   jax.ShapeDtypeStruct((nt, N_EXPERTS, LANES), jnp.float32)],
        compiler_params=_cparams(("arbitrary",)),
        name="route",
    )(lg3, b3)


def _dest_kernel(eidx_ref, rank_ref, base_ref, o_ref, *, tn):
    e_iota = lax.broadcasted_iota(jnp.int32, (N_EXPERTS, tn), 0)
    base = base_ref[0][:, :1]
    for k in range(TOP_K):
        onehot = e_iota == eidx_ref[k:k + 1, :]
        b = jnp.sum(jnp.where(onehot, base, 0.0), axis=0, keepdims=True)
        o_ref[k:k + 1, :] = (b + rank_ref[k:k + 1, :]).astype(jnp.int32)


def _dest(eidx, rank, base, tn):
    n_pad = eidx.shape[1]
    return pl.pallas_call(
        functools.partial(_dest_kernel, tn=tn),
        grid=(n_pad // tn,),
        in_specs=[pl.BlockSpec((TOP_K, tn), lambda i: (0, i)),
                  pl.BlockSpec((TOP_K, tn), lambda i: (0, i)),
                  pl.BlockSpec((1, N_EXPERTS, LANES), lambda i: (i, 0, 0))],
        out_specs=pl.BlockSpec((TOP_K, tn), lambda i: (0, i)),
        out_shape=jax.ShapeDtypeStruct((TOP_K, n_pad), jnp.int32),
        compiler_params=_cparams(("arbitrary",)),
        name="dest",
    )(eidx, rank, base)


def _pack_words(lo_f32, hi_f32):
    lo = lax.shift_right_logical(pltpu.bitcast(lo_f32, jnp.uint32), jnp.uint32(16))
    hi = pltpu.bitcast(hi_f32, jnp.uint32) & jnp.uint32(0xFFFF0000)
    return hi | lo


def _unpack_words(w):
    lo = pltpu.bitcast(lax.shift_left(w, jnp.uint32(16)), jnp.float32)
    hi = pltpu.bitcast(w & jnp.uint32(0xFFFF0000), jnp.float32)
    return lo, hi


def _bf16_exact(x):
    return x.astype(BF16).astype(jnp.float32)


def _dispatch_kernel(pstart_ref, pcnt_ref, h_ref, dest_hbm, xs_hbm, dsm, pk, zrow, sem_d, sem_r,
                     *, tn, nt):
    i = pl.program_id(0)
    half = pk.shape[1]

    @pl.when(i < nt)
    def _():
        cp = pltpu.make_async_copy(dest_hbm.at[pl.ds(i * (TOP_K * tn), TOP_K * tn)], dsm, sem_d)
        cp.start()
        x = h_ref[...]
        pk[...] = _pack_words(x[:, :half].astype(jnp.float32), x[:, half:].astype(jnp.float32))
        cp.wait()

        def body(r, carry):
            for k in range(TOP_K):
                d = dsm[k * tn + r]
                pltpu.make_async_copy(pk.at[pl.ds(r, 1), :], xs_hbm.at[pl.ds(d, 1), :],
                                      sem_r).start(priority=k % 2)
            return carry

        lax.fori_loop(0, tn, body, 0)
        for k in range(TOP_K):
            pltpu.make_async_copy(pk, xs_hbm.at[pl.ds(0, tn), :], sem_r).wait()

    @pl.when(i == nt)
    def _():
        zrow[...] = jnp.zeros(zrow.shape, zrow.dtype)

        def per_expert(e, carry):
            s0 = pstart_ref[e]
            c = pcnt_ref[e]
            head = jnp.minimum(c, (8 - s0 % 8) % 8)
            g0 = s0 + head
            ngrp = (c - head) // 8
            grp = lambda r: pl.ds(pl.multiple_of(g0 + r * 8, 8), 8)

            def start(r, cc):
                pltpu.make_async_copy(zrow.at[pl.ds(0, 1), :], xs_hbm.at[pl.ds(s0 + r, 1), :], sem_r).start()
                return cc

            def wait(r, cc):
                pltpu.make_async_copy(zrow.at[pl.ds(0, 1), :], xs_hbm.at[pl.ds(s0, 1), :], sem_r).wait()
                return cc

            def gstart(r, cc):
                pltpu.make_async_copy(zrow, xs_hbm.at[grp(r), :], sem_r).start()
                return cc

            def gwait(r, cc):
                pltpu.make_async_copy(zrow, xs_hbm.at[grp(0), :], sem_r).wait()
                return cc

            lax.fori_loop(0, head, start, 0)
            lax.fori_loop(0, ngrp, gstart, 0)
            lax.fori_loop(0, head, wait, 0)
            lax.fori_loop(0, ngrp, gwait, 0)
            return carry

        lax.fori_loop(0, N_EXPERTS, per_expert, 0)

        t0 = pstart_ref[N_EXPERTS]
        groups = pcnt_ref[N_EXPERTS]
        rows8 = lambda r: pl.ds(pl.multiple_of(t0 + r * 8, 8), 8)

        def tstart(r, cc):
            pltpu.make_async_copy(zrow, xs_hbm.at[rows8(r), :], sem_r).start()
            return cc

        def twait(r, cc):
            pltpu.make_async_copy(zrow, xs_hbm.at[rows8(0), :], sem_r).wait()
            return cc

        lax.fori_loop(0, groups, tstart, 0)
        lax.fori_loop(0, groups, twait, 0)


def _dispatch(h2, dest_flat, pad_start, pad_cnt, n_rows, tn):
    n_pad, d = h2.shape
    nt = n_pad // tn
    grid_spec = pltpu.PrefetchScalarGridSpec(
        num_scalar_prefetch=2,
        grid=(nt + 1,),
        in_specs=[pl.BlockSpec((tn, d), lambda i, a, b: (jnp.minimum(i, nt - 1), 0)),
                  pl.BlockSpec(memory_space=pl.ANY)],
        out_specs=pl.BlockSpec(memory_space=pl.ANY),
        scratch_shapes=[pltpu.SMEM((TOP_K * tn,), jnp.int32),
                        pltpu.VMEM((tn, d // 2), jnp.uint32),
                        pltpu.VMEM((8, d // 2), jnp.uint32),
                        pltpu.SemaphoreType.DMA(()),
                        pltpu.SemaphoreType.DMA(())])
    return pl.pallas_call(
        functools.partial(_dispatch_kernel, tn=tn, nt=nt),
        grid_spec=grid_spec,
        out_shape=jax.ShapeDtypeStruct((n_rows, d // 2), jnp.uint32),
        compiler_params=_cparams(("arbitrary",)),
        name="dispatch",
    )(pad_start, pad_cnt, h2, dest_flat)


def _expert_kernel(blk_e_ref, nused_ref, next_e_ref, slot_ref, x_ref, wg_hbm, wu_hbm, wd_hbm, y_ref,
                   wg_f, wu_f, wd_f, wgb, wub, wdb, sems):
    i = pl.program_id(0)
    nused = nused_ref[0]
    ii = jnp.minimum(i, nused - 1)
    e = blk_e_ref[ii]
    e_prev = blk_e_ref[jnp.maximum(ii - 1, 0)]
    half = x_ref.shape[1]

    def copies(ex, sl):
        return [pltpu.make_async_copy(src.at[ex], dst.at[sl], sems.at[sl, t])
                for t, (src, dst) in enumerate(((wg_hbm, wg_f), (wu_hbm, wu_f), (wd_hbm, wd_f)))]

    @pl.when(i == 0)
    def _():
        for c in copies(e, slot_ref[e]):
            c.start()

    @pl.when((i < nused) & ((i == 0) | (e != e_prev)))
    def _():
        sl = slot_ref[e]
        for c in copies(e, sl):
            c.wait()
        nxt = next_e_ref[e]

        @pl.when(nxt >= 0)
        def _():
            for c in copies(nxt, 1 - sl):
                c.start()

        wgb[...] = wg_f[sl].astype(BF16)
        wub[...] = wu_f[sl].astype(BF16)
        wdb[...] = wd_f[sl].astype(BF16)

    @pl.when(i < nused)
    def _():
        lo, hi = _unpack_words(x_ref[...])
        xl = lo.astype(BF16)
        xh = hi.astype(BF16)
        g = _dot(xl, wgb[:half, :]) + _dot(xh, wgb[half:, :])
        u = _dot(xl, wub[:half, :]) + _dot(xh, wub[half:, :])
        hmid = (_silu(g) * u).astype(BF16)
        y = _dot(hmid, wdb[...])
        y_ref[...] = _pack_words(_bf16_exact(y[:, :half]), _bf16_exact(y[:, half:]))

    @pl.when(i >= nused)
    def _():
        y_ref[...] = jnp.zeros(y_ref.shape, y_ref.dtype)


def _experts(xs, blk_e, nused, next_e, slot, w_gate, w_up, w_down, tb):
    n_rows, half = xs.shape
    _, d, f = w_gate.shape
    nblk = n_rows // tb

    def xmap(i, be, nu, ne, sl):
        return (jnp.minimum(i, nu[0] - 1), 0)

    grid_spec = pltpu.PrefetchScalarGridSpec(
        num_scalar_prefetch=4,
        grid=(nblk,),
        in_specs=[pl.BlockSpec((tb, half), xmap),
                  pl.BlockSpec(memory_space=pl.ANY),
                  pl.BlockSpec(memory_space=pl.ANY),
                  pl.BlockSpec(memory_space=pl.ANY)],
        out_specs=pl.BlockSpec((tb, half), lambda i, be, nu, ne, sl: (i, 0)),
        scratch_shapes=[pltpu.VMEM((2, d, f), jnp.float32),
                        pltpu.VMEM((2, d, f), jnp.float32),
                        pltpu.VMEM((2, f, d), jnp.float32),
                        pltpu.VMEM((d, f), BF16),
                        pltpu.VMEM((d, f), BF16),
                        pltpu.VMEM((f, d), BF16),
                        pltpu.SemaphoreType.DMA((2, 3))])
    return pl.pallas_call(
        _expert_kernel,
        grid_spec=grid_spec,
        out_shape=jax.ShapeDtypeStruct((n_rows, half), jnp.uint32),
        compiler_params=_cparams(("arbitrary",)),
        name="experts",
    )(blk_e, nused, next_e, slot, xs, w_gate, w_up, w_down)


def _ffn_kernel(h_ref, wg_ref, wu_ref, wd_ref, o_ref, wgb, wub, wdb):
    @pl.when(pl.program_id(0) == 0)
    def _():
        wgb[...] = wg_ref[...].astype(BF16)
        wub[...] = wu_ref[...].astype(BF16)
        wdb[...] = wd_ref[...].astype(BF16)

    x = h_ref[...]
    hmid = (_silu(_dot(x, wgb[...])) * _dot(x, wub[...])).astype(BF16)
    o_ref[...] = _dot(hmid, wdb[...]).astype(o_ref.dtype)


def _ffn_shared(h2, wg, wu, wd, tm):
    n_pad, d = h2.shape
    f = wg.shape[1]
    return pl.pallas_call(
        _ffn_kernel,
        grid=(n_pad // tm,),
        in_specs=[pl.BlockSpec((tm, d), lambda i: (i, 0)),
                  pl.BlockSpec((d, f), lambda i: (0, 0)),
                  pl.BlockSpec((d, f), lambda i: (0, 0)),
                  pl.BlockSpec((f, d), lambda i: (0, 0))],
        out_specs=pl.BlockSpec((tm, d), lambda i: (i, 0)),
        out_shape=jax.ShapeDtypeStruct((n_pad, d), BF16),
        scratch_shapes=[pltpu.VMEM((d, f), BF16),
                        pltpu.VMEM((d, f), BF16),
                        pltpu.VMEM((f, d), BF16)],
        compiler_params=_cparams(("arbitrary",)),
        name="ffn_shared",
    )(h2, wg, wu, wd)


def _combine_kernel(dest_hbm, ys_hbm, wts_ref, sh_ref, x_ref, gt_ref, g_ref, o_ref,
                    dsm, buf, sem_d, sem_r, *, tn, tile0, nt, final_norm):
    i = pl.program_id(0)
    slot = i % 2

    def dest_copy(tile, sl):
        return pltpu.make_async_copy(dest_hbm.at[pl.ds((tile0 + tile) * (TOP_K * tn), TOP_K * tn)],
                                     dsm.at[sl], sem_d.at[sl])

    @pl.when(i == 0)
    def _():
        dest_copy(0, 0).start()

    dest_copy(i, slot).wait()

    @pl.when(i + 1 < nt)
    def _():
        dest_copy(i + 1, 1 - slot).start()

    def body(r, carry):
        for k in range(TOP_K):
            d = dsm[slot, k * tn + r]
            pltpu.make_async_copy(ys_hbm.at[pl.ds(d, 1), :], buf.at[k, pl.ds(r, 1), :],
                                  sem_r).start(priority=k % 2)
        return carry

    lax.fori_loop(0, tn, body, 0)
    for k in range(TOP_K):
        pltpu.make_async_copy(ys_hbm.at[pl.ds(0, tn), :], buf.at[k], sem_r).wait()

    half = buf.shape[2]
    wts = wts_ref[...]
    acc_lo = jnp.zeros((tn, half), jnp.float32)
    acc_hi = jnp.zeros((tn, half), jnp.float32)
    for k in range(TOP_K):
        lo, hi = _unpack_words(buf[k])
        wk = wts[:, k:k + 1]
        acc_lo = acc_lo + wk * lo
        acc_hi = acc_hi + wk * hi
    sh = sh_ref[...].astype(jnp.float32)
    gt = gt_ref[0]
    x_lo = x_ref[:, :half] + gt[:, :half] * (acc_lo + sh[:, :half])
    x_hi = x_ref[:, half:] + gt[:, half:] * (acc_hi + sh[:, half:])
    if final_norm:
        ms = (jnp.sum(x_lo * x_lo, axis=-1, keepdims=True)
              + jnp.sum(x_hi * x_hi, axis=-1, keepdims=True)) / (2 * half)
        inv = lax.rsqrt(ms + EPS)
        g = g_ref[...]
        x_lo = x_lo * inv * g[:, :half]
        x_hi = x_hi * inv * g[:, half:]
    o_ref[:, :half] = x_lo
    o_ref[:, half:] = x_hi


def _combine(dest_flat, ys, wts_tok, shared, x1, gt3, g_final, tile0, tn, final_norm):
    rows, d = x1.shape
    nt = rows // tn
    gr = gt3.shape[1]
    per = nt // gt3.shape[0]
    return pl.pallas_call(
        functools.partial(_combine_kernel, tn=tn, tile0=tile0, nt=nt, final_norm=final_norm),
        grid=(nt,),
        in_specs=[pl.BlockSpec(memory_space=pl.ANY),
                  pl.BlockSpec(memory_space=pl.ANY),
                  pl.BlockSpec((tn, TOP_K), lambda i: (tile0 + i, 0)),
                  pl.BlockSpec((tn, d), lambda i: (tile0 + i, 0)),
                  pl.BlockSpec((tn, d), lambda i: (i, 0)),
                  pl.BlockSpec((1, gr, d), lambda i: (i // per, 0, 0)),
                  pl.BlockSpec((1, d), lambda i: (0, 0))],
        out_specs=pl.BlockSpec((tn, d), lambda i: (i, 0)),
        out_shape=jax.ShapeDtypeStruct((rows, d), jnp.float32),
        scratch_shapes=[pltpu.SMEM((2, TOP_K * tn), jnp.int32),
                        pltpu.VMEM((TOP_K, tn, d // 2), jnp.uint32),
                        pltpu.SemaphoreType.DMA((2,)),
                        pltpu.SemaphoreType.DMA(())],
        compiler_params=_cparams(("arbitrary",)),
        name="combine",
    )(dest_flat, ys, wts_tok, shared, x1, gt3, g_final.reshape(1, d))


def _page_specs(shape, n, pg):
    def mk(u):
        return pl.BlockSpec((1,) + shape, lambda b, j, pt: (pt[b, j * pg + u], 0, 0))
    return [mk(u) for u in range(n)]


def _kv_page_specs(kshape, vshape, pg, ns):
    def mk(shape, first):
        def one(u):
            def index(b, j, pt):
                step = jnp.minimum(j, ns - 1) if first else jnp.maximum(j - ns, 0)
                return (pt[b, step * pg + u], 0, 0)
            return pl.BlockSpec((1,) + shape, index)
        return [one(u) for u in range(pg)]
    return mk(kshape, True) + mk(vshape, False)


def _softmax_pages(s_ref):
    s = s_ref[...]
    m = jnp.max(jnp.max(s, axis=0, keepdims=True), axis=2, keepdims=True)
    e = jnp.exp(s - m)
    return e / jnp.sum(jnp.sum(e, axis=0, keepdims=True), axis=2, keepdims=True)


def _da_sample_kernel(pt_ref, lam_ref, q_ref, kn_ref, vn_ref, bl_ref, cf_ref, b0_ref, ex_ref, hm_ref,
                      *rest, pg, n_pages):
    k_refs, v_refs = rest[:pg], rest[pg:2 * pg]
    o_ref, s_ref, a_ref, acc_ref = rest[2 * pg:]
    j = pl.program_id(1)
    ns = n_pages // pg
    r = q_ref.shape[1]
    nh = r // 2
    lane = lax.broadcasted_iota(jnp.int32, (r, PAGE_SIZE), 1)

    @pl.when(j < ns)
    def _():
        q = q_ref[0]
        qb = _bf(q)
        for u in range(pg):
            page = j * pg + u
            s = _dot(qb, _bf(k_refs[u][0]))
            s_ref[page] = s + jnp.where(page == (n_pages - 1), bl_ref[...], cf_ref[...])

        @pl.when(j == 0)
        def _():
            s_new = jnp.sum(_rounded(q) * _rounded(kn_ref[0]), axis=-1, keepdims=True) + b0_ref[...]
            s_ref[n_pages] = jnp.where(lane == 0, s_new, NEG)

    @pl.when(j == ns)
    def _():
        p = _softmax_pages(s_ref)
        a = p[:, :nh, :] - lam_ref[0] * p[:, nh:, :]
        a_ref[...] = _bf(jnp.concatenate([a, jnp.zeros_like(a)], axis=1))
        a_new = a_ref[n_pages][:, 0:1].astype(jnp.float32)
        acc_ref[...] = a_new * _rounded(vn_ref[0])

    @pl.when(j >= ns)
    def _():
        acc = acc_ref[...]
        for u in range(pg):
            page = (j - ns) * pg + u
            pe = _bf(_dot(a_ref[page], ex_ref[...]) * hm_ref[...])
            acc = acc + _dot(pe, _bf(v_refs[u][0]))
        acc_ref[...] = acc

    @pl.when(j == pl.num_programs(1) - 1)
    def _():
        o_ref[0] = acc_ref[...]


def _da_sample(page_table, lam, qbd, knew, vnew, bias_last, cfar, bias0, kt, v2, pg):
    nb, n_pages = page_table.shape
    r, w = qbd.shape[1:]
    rows_v, dv = v2.shape[1:]
    nh = rows_v // PAGE_SIZE
    ns = n_pages // pg
    col = jnp.arange(rows_v, dtype=jnp.int32)
    expand = (col[None, :] // nh == jnp.arange(PAGE_SIZE, dtype=jnp.int32)[:, None]).astype(BF16)
    head_mask = (col[None, :] % nh == jnp.arange(r, dtype=jnp.int32)[:, None]).astype(jnp.float32)
    full = lambda shp: pl.BlockSpec(shp, lambda b, j, pt: (0,) * len(shp))
    per_b = lambda shp: pl.BlockSpec((1,) + shp, lambda b, j, pt: (b, 0, 0))
    grid_spec = pltpu.PrefetchScalarGridSpec(
        num_scalar_prefetch=1,
        grid=(nb, 2 * ns),
        in_specs=[pl.BlockSpec(memory_space=pltpu.SMEM),
                  per_b((r, w)), per_b((1, w)), per_b((r, dv)),
                  full((r, PAGE_SIZE)), full((r, 1)), full((r, 1)),
                  full((PAGE_SIZE, rows_v)), full((r, rows_v))]
                 + _kv_page_specs((w, PAGE_SIZE), (rows_v, dv), pg, ns),
        out_specs=per_b((r, dv)),
        scratch_shapes=[pltpu.VMEM((n_pages + 1, r, PAGE_SIZE), jnp.float32),
                        pltpu.VMEM((n_pages + 1, r, PAGE_SIZE), BF16),
                        pltpu.VMEM((r, dv), jnp.float32)])
    return pl.pallas_call(
        functools.partial(_da_sample_kernel, pg=pg, n_pages=n_pages),
        grid_spec=grid_spec,
        out_shape=jax.ShapeDtypeStruct((nb, r, dv), jnp.float32),
        compiler_params=_cparams(("arbitrary", "arbitrary")),
        name="da_sample",
    )(page_table, lam, qbd, knew, vnew, bias_last, cfar, bias0, expand, head_mask,
      *([kt] * pg), *([v2] * pg))


def _idx_sample_kernel(pt_ref, q_ref, w_ref, kn_ref, *rest, pg, n_pages, topk):
    k_refs = rest[:pg]
    sel_ref, sc_ref = rest[pg:]
    j = pl.program_id(1)
    q = q_ref[0]
    w = _rounded(w_ref[0])
    scale = IDX_DK ** -0.5 * IDX_HEADS ** -0.5
    rows = sc_ref.shape[0]
    lane = lax.broadcasted_iota(jnp.int32, (1, PAGE_SIZE), 1)

    @pl.when(j == 0)
    def _():
        sc_ref[...] = jnp.full(sc_ref.shape, -jnp.inf, jnp.float32)
        d = jnp.maximum(jnp.sum(_rounded(q) * _rounded(kn_ref[0]), axis=-1, keepdims=True), 0.0)
        s_new = jnp.sum(w * _rounded(d), axis=0, keepdims=True) * scale
        sc_ref[n_pages:n_pages + 1, :] = jnp.where(lane == 0, s_new, -jnp.inf)

    qb = _bf(q)
    for u in range(pg):
        d = _rounded(jnp.maximum(_dot(qb, _bf(k_refs[u][0])), 0.0))
        sc_ref[pl.ds(j * pg + u, 1), :] = jnp.sum(w * d, axis=0, keepdims=True) * scale

    @pl.when(j == pl.num_programs(1) - 1)
    def _():
        key = _float_key(sc_ref[...])
        pos = (lax.broadcasted_iota(jnp.int32, key.shape, 0) * PAGE_SIZE
               + lax.broadcasted_iota(jnp.int32, key.shape, 1))

        def bit_body(it, thr):
            cand = thr + (jnp.int32(1) << (31 - it))
            cnt = jnp.sum(jnp.where(key >= cand, 1.0, 0.0))
            return jnp.where(cnt >= float(topk), cand, thr)

        thr = lax.fori_loop(0, 32, bit_body, jnp.int32(INT_MIN))
        thr = jnp.maximum(thr, jnp.int32(KEY_NEG_INF + 1))
        need = float(topk) - jnp.sum(jnp.where(key > thr, 1.0, 0.0))
        n_bits = max(1, int(rows * PAGE_SIZE).bit_length())

        def cut_body(it, cut):
            cand = cut + (jnp.int32(1) << (n_bits - 1 - it))
            n_eq = jnp.sum(jnp.where((key == thr) & (pos < cand), 1.0, 0.0))
            return jnp.where(n_eq <= need, cand, cut)

        cut = lax.fori_loop(0, n_bits, cut_body, jnp.int32(0))
        sel_ref[0] = jnp.where((key > thr) | ((key == thr) & (pos < cut)), 1.0, 0.0)


def _idx_sample(page_table, qix, wix, knew, kc, pg, topk):
    nb, n_pages = page_table.shape
    rows = -(-(n_pages + 1) // 8) * 8
    per_b = lambda shp: pl.BlockSpec((1,) + shp, lambda b, j, pt: (b, 0, 0))
    grid_spec = pltpu.PrefetchScalarGridSpec(
        num_scalar_prefetch=1,
        grid=(nb, n_pages // pg),
        in_specs=[per_b((IDX_HEADS, IDX_DK)), per_b((IDX_HEADS, 1)), per_b((1, IDX_DK))]
                 + _page_specs((IDX_DK, PAGE_SIZE), pg, pg),
        out_specs=per_b((rows, PAGE_SIZE)),
        scratch_shapes=[pltpu.VMEM((rows, PAGE_SIZE), jnp.float32)])
    return pl.pallas_call(
        functools.partial(_idx_sample_kernel, pg=pg, n_pages=n_pages, topk=topk),
        grid_spec=grid_spec,
        out_shape=jax.ShapeDtypeStruct((nb, rows, PAGE_SIZE), jnp.float32),
        compiler_params=_cparams(("arbitrary", "arbitrary")),
        name="idx_sample",
    )(page_table, qix, wix, knew, *([kc] * pg))


def _sa_sample_kernel(pt_ref, q_ref, kn_ref, vn_ref, sel_ref, bl_ref, cf_ref, b0_ref, gm_ref, *rest,
                      pg, n_pages):
    k_refs, v_refs = rest[:pg], rest[pg:2 * pg]
    o_ref, s_ref, p_ref, acc_ref = rest[2 * pg:]
    j = pl.program_id(1)
    ns = n_pages // pg
    scale = SA_DH ** -0.5
    r, cols = gm_ref.shape
    lane = lax.broadcasted_iota(jnp.int32, (r, cols), 1)

    @pl.when(j < ns)
    def _():
        q = q_ref[0]
        qb = _bf(q)
        for u in range(pg):
            page = j * pg + u
            s = _dot_nt(qb, _bf(k_refs[u][0])) * scale
            s = s + jnp.where(page == (n_pages - 1), bl_ref[...], cf_ref[...])
            keep = (sel_ref[0, pl.ds(page, 1), :] > 0.0) & (gm_ref[...] > 0.0)
            s_ref[page] = jnp.where(keep, s, NEG)

        @pl.when(j == 0)
        def _():
            on = sel_ref[0, n_pages:n_pages + 1, 0:1] > 0.0
            s_new = jnp.sum(_rounded(q) * _rounded(kn_ref[0]), axis=-1, keepdims=True) * scale + b0_ref[...]
            s_ref[n_pages] = jnp.where((lane == 0) & on, s_new, NEG)

    @pl.when(j == ns)
    def _():
        p_ref[...] = _bf(_softmax_pages(s_ref))
        acc_ref[...] = p_ref[n_pages][:, 0:1].astype(jnp.float32) * _rounded(vn_ref[0])

    @pl.when(j >= ns)
    def _():
        acc = acc_ref[...]
        for u in range(pg):
            acc = acc + _dot(p_ref[(j - ns) * pg + u], _bf(v_refs[u][0]))
        acc_ref[...] = acc

    @pl.when(j == pl.num_programs(1) - 1)
    def _():
        o_ref[0] = acc_ref[...]


def _sa_sample(page_table, q, knew, vnew, sel2, bias_last2, cfar, bias0, group_mask, k2, v2, pg):
    nb, n_pages = page_table.shape
    r, dh = q.shape[1:]
    srows, cols = sel2.shape[1:]
    ns = n_pages // pg
    full = lambda shp: pl.BlockSpec(shp, lambda b, j, pt: (0,) * len(shp))
    per_b = lambda shp: pl.BlockSpec((1,) + shp, lambda b, j, pt: (b, 0, 0))
    grid_spec = pltpu.PrefetchScalarGridSpec(
        num_scalar_prefetch=1,
        grid=(nb, 2 * ns),
        in_specs=[per_b((r, dh)), per_b((r, dh)), per_b((r, dh)), per_b((srows, cols)),
                  full((r, cols)), full((r, 1)), full((r, 1)), full((r, cols))]
                 + _kv_page_specs((cols, dh), (cols, dh), pg, ns),
        out_specs=per_b((r, dh)),
        scratch_shapes=[pltpu.VMEM((n_pages + 1, r, cols), jnp.float32),
                        pltpu.VMEM((n_pages + 1, r, cols), BF16),
                        pltpu.VMEM((r, dh), jnp.float32)])
    return pl.pallas_call(
        functools.partial(_sa_sample_kernel, pg=pg, n_pages=n_pages),
        grid_spec=grid_spec,
        out_shape=jax.ShapeDtypeStruct((nb, r, dh), jnp.float32),
        compiler_params=_cparams(("arbitrary", "arbitrary")),
        name="sa_sample",
    )(page_table, q, knew, vnew, sel2, bias_last2, cfar, bias0, group_mask, *([k2] * pg), *([v2] * pg))


ROW_TILE = 256
COMBINE_TILE = 128
EXPERT_BLOCK = 512
TAIL_ROWS = 256
PAGES_PER_STEP_DA = 16
PAGES_PER_STEP_SA = 32
PAGES_PER_STEP_IDX = 64


def _rms_rows(x, g):
    return x * lax.rsqrt(jnp.mean(x * x, axis=-1, keepdims=True) + EPS) * g


def _pad_rows(x, rows):
    return jnp.pad(x, ((0, rows - x.shape[0]), (0, 0)))


def _tile_major(dest, tn):
    k, n = dest.shape
    return dest.reshape(k, n // tn, tn).transpose(1, 0, 2).reshape(-1)


def kernel(x_prompt, x_sample, c_prompt, c_sample, cache_da_k, cache_da_v, cache_sa_k, cache_sa_v, cache_idx_k, page_table, rel_bias_table, w_ada, b_ada, g_attn, g_ffn, w_in, lambda_q1, lambda_k1, lambda_q2, lambda_k2, g_subln, w_proj_da, w_proj_sa, w_out, w_router, b_router, w_gate, w_up, w_down, w_sh_gate, w_sh_up, w_sh_down, g_final):
    f32, bf16 = jnp.float32, BF16
    nb, t, d = x_prompt.shape
    nbs, dec_seq, _ = x_sample.shape
    assert dec_seq == 1
    depth = w_in.shape[0]
    n_pages = page_table.shape[1]
    past_len = n_pages * PAGE_SIZE
    n = nb * t
    n_pad = n + TAIL_ROWS
    tq = min(256, t)
    tq_da = min(512, t)
    assert t % tq_da == 0
    assert tq >= MAX_DISTANCE and t % tq == 0 and n % ROW_TILE == 0 and nbs <= TAIL_ROWS
    assert TAIL_ROWS == ROW_TILE and EXPERT_BLOCK % 8 == 0
    topk_p = min(SA_TOPK_MAX, t // 4)
    topk_s = min(SA_TOPK_MAX, (past_len + dec_seq) // 4)
    assert topk_p <= tq

    sizes = [DA_HEADS * 2 * DA_DK, DA_HEADS * 2 * DA_DK, DA_HEADS * DA_DV, SA_HEADS * SA_DH,
             SA_KV_HEADS * SA_DH, SA_KV_HEADS * SA_DH, IDX_HEADS * IDX_DK, IDX_DK, IDX_HEADS, d, d]
    offs = [sum(sizes[:i]) for i in range(len(sizes) + 1)]
    (o_qda, o_kda, o_vda, o_qsa, o_ksa, o_vsa, o_qix, o_kix, o_wix, o_gda, o_gsa, _) = offs
    misc_w = 2 * LANES

    table = rel_bias_table.astype(f32)
    cfar = table[N_BUCKETS - 1]
    tiles_da = _near_tiles(table[:, :DA_HEADS], tq_da)
    tiles_sa = _near_tiles(table[:, DA_HEADS:], tq)
    last_dist = past_len - ((n_pages - 1) * PAGE_SIZE + jnp.arange(PAGE_SIZE, dtype=jnp.int32))
    bias_last = _bias_by_distance(table, last_dist)
    bias0 = _bias_by_distance(table, jnp.zeros((1,), jnp.int32))

    xp = x_prompt.reshape(n, d)
    xs = x_sample.reshape(nbs, d)
    c_all = jnp.concatenate([c_prompt, c_sample], axis=0)
    leaves_p, leaves_s = [], []
    for l in range(depth):
        lam_init = 0.8 - 0.6 * math.exp(-0.3 * l)
        lam = (jnp.exp(jnp.sum(lambda_q1[l].astype(f32) * lambda_k1[l].astype(f32)))
               - jnp.exp(jnp.sum(lambda_q2[l].astype(f32) * lambda_k2[l].astype(f32))) + lam_init)
        mod = _linear_small(c_all, w_ada[l], b_ada[l], silu_in=True)
        sh1, sc1, gt1, sh2, sc2, gt2 = jnp.split(mod[:nb], 6, axis=-1)
        sh1s, sc1s, gt1s, sh2s, sc2s, gt2s = jnp.split(mod[nb:], 6, axis=-1)
        w = w_in[l]

        h = _norm_mod(xp, g_attn[l], sc1, sh1, t, ROW_TILE)
        q_da, = _mm(h, w, o_qda, sizes[0], (bf16,), scale=DA_DK ** -0.5)
        k_da_b, k_da_t = _mm(h, w, o_kda, sizes[1], (bf16,), rows_per_batch=t)
        k_da = k_da_t.reshape(nb, DA_HEADS, 2, DA_DK, t).transpose(0, 4, 1, 2, 3)
        v_da, v_da_b = _mm(h, w, o_vda, sizes[2], (f32, bf16))
        q_sa, = _mm(h, w, o_qsa, sizes[3], (bf16,), scale=SA_DH ** -0.5)
        k_sa, k_sa_b = _mm(h, w, o_ksa, sizes[4], (f32, bf16))
        v_sa, v_sa_b = _mm(h, w, o_vsa, sizes[5], (f32, bf16))
        q_ix, = _mm(h, w, o_qix, sizes[6], (bf16,), tn=512)
        misc, misc_b = _mm(h, w, o_kix, misc_w, (f32, bf16))
        gates, = _mm(h, w[:, o_gda:], 0, 2 * d, (bf16,), sigmoid=True)
        k_ix = misc[:, :IDX_DK]

        o_da = _da_prompt(q_da, k_da_b, v_da_b, tiles_da, cfar[:DA_HEADS], lam.reshape(1),
                          g_subln[l].astype(f32), nb, t, tq_da, 1.0 - lam_init)
        o_sa = _sa_prompt(q_sa, q_ix, misc, misc_b, k_sa_b, v_sa_b, tiles_sa, cfar[DA_HEADS:], nb, t, tq, topk_p)
        m = _proj_gate(o_da, o_sa, w_proj_da[l], w_proj_sa[l], gates, d)
        x1 = _mm_resid(m, w_out[l], xp, gt1, t)

        hs = _rms_rows(xs, g_attn[l].astype(f32)) * (1.0 + sc1s) + sh1s
        ps = _linear_small(hs, w)
        seg = lambda i: ps[:, offs[i]:offs[i + 1]]
        q_da_s, k_da_s, v_da_s, q_sa_s, k_sa_s, v_sa_s, q_ix_s, k_ix_s, w_ix_s, gda_s, gsa_s = [
            seg(i) for i in range(11)]

        r_da = 2 * DA_HEADS
        q16 = (q_da_s * DA_DK ** -0.5).reshape(nbs, DA_HEADS, 2, DA_DK).transpose(0, 2, 1, 3).reshape(nbs, r_da, DA_DK)
        blk_of_row = 2 * (jnp.arange(r_da) % DA_HEADS) + jnp.arange(r_da) // DA_HEADS
        place = (blk_of_row[:, None] == jnp.arange(r_da)[None, :]).astype(f32)
        qbd_da = (place[None, :, :, None] * q16[:, :, None, :]).reshape(nbs, r_da, -1)
        both = lambda a: jnp.concatenate([a, a], axis=0)
        kt_da = cache_da_k[l].transpose(0, 2, 3, 4, 1).reshape(-1, DA_HEADS * 2 * DA_DK, PAGE_SIZE)
        v2_da = cache_da_v[l].reshape(-1, PAGE_SIZE * DA_HEADS, DA_DV)
        v_rows = jnp.pad(v_da_s.reshape(nbs, DA_HEADS, DA_DV), ((0, 0), (0, r_da - DA_HEADS), (0, 0)))
        a_da = _da_sample(page_table, lam.reshape(1), qbd_da, k_da_s[:, None, :], v_rows,
                          both(bias_last[:DA_HEADS]), both(cfar[:DA_HEADS, None]), both(bias0[:DA_HEADS]),
                          kt_da, v2_da, math.gcd(PAGES_PER_STEP_DA, n_pages))
        o_da_s = a_da[:, :DA_HEADS]
        o_da_s = _rms_rows(o_da_s, g_subln[l].astype(f32)) * (1.0 - lam_init)

        sel = _idx_sample(page_table, q_ix_s.reshape(nbs, IDX_HEADS, IDX_DK), w_ix_s[:, :, None],
                          k_ix_s[:, None, :], cache_idx_k[l].transpose(0, 2, 1),
                          math.gcd(PAGES_PER_STEP_IDX, n_pages), topk_s)
        r_sa = 2 * SA_HEADS
        kv_of_row = jnp.minimum(jnp.arange(r_sa) // SA_GROUP, SA_KV_HEADS - 1)
        pad_sa = lambda a: jnp.pad(a, ((0, 0), (0, r_sa - SA_HEADS), (0, 0)))
        rep_kv = lambda a: jnp.repeat(a, SA_KV_HEADS, axis=-1)
        col_kv = jnp.arange(PAGE_SIZE * SA_KV_HEADS) % SA_KV_HEADS
        k2_sa = cache_sa_k[l].reshape(-1, PAGE_SIZE * SA_KV_HEADS, SA_DH)
        v2_sa = cache_sa_v[l].reshape(-1, PAGE_SIZE * SA_KV_HEADS, SA_DH)
        a_sa = _sa_sample(page_table, pad_sa(q_sa_s.reshape(nbs, SA_HEADS, SA_DH)),
                          k_sa_s.reshape(nbs, SA_KV_HEADS, SA_DH)[:, kv_of_row],
                          v_sa_s.reshape(nbs, SA_KV_HEADS, SA_DH)[:, kv_of_row],
                          rep_kv(sel), rep_kv(_pad_rows(bias_last[DA_HEADS:], r_sa)),
                          _pad_rows(cfar[DA_HEADS:, None], r_sa), _pad_rows(bias0[DA_HEADS:], r_sa),
                          (col_kv[None, :] == kv_of_row[:, None]).astype(f32),
                          k2_sa, v2_sa, math.gcd(PAGES_PER_STEP_SA, n_pages))
        o_sa_s = a_sa[:, :SA_HEADS]

        pda = _linear_small(o_da_s.reshape(nbs, -1), w_proj_da[l])
        psa = _linear_small(o_sa_s.reshape(nbs, -1), w_proj_sa[l])
        ms = _sigmoid(gda_s) * pda + _sigmoid(gsa_s) * psa
        x1s = xs + gt1s * _linear_small(ms, w_out[l])
        h2s = _rms_rows(x1s, g_ffn[l].astype(f32)) * (1.0 + sc2s) + sh2s
        lg_s = _linear_small(h2s, w_router[l])

        h2_all, lg_all = _norm_router(x1, g_ffn[l], sc2, sh2, w_router[l].T,
                                      _pad_rows(h2s.astype(bf16), TAIL_ROWS),
                                      _pad_rows(lg_s, TAIL_ROWS).T, t, ROW_TILE)
        eidx, wts, rank, cnt = _route(lg_all, b_router[l], ROW_TILE)
        cnt_tile = cnt[:, :, 0]
        total = jnp.sum(cnt_tile, axis=0)
        padded = jnp.ceil(total / EXPERT_BLOCK) * EXPERT_BLOCK
        pends = jnp.cumsum(padded)
        pstart = pends - padded
        base = pstart[None, :] + jnp.cumsum(cnt_tile, axis=0) - cnt_tile
        dest = _dest(eidx, rank, jnp.broadcast_to(base[:, :, None], base.shape + (LANES,)), ROW_TILE)
        n_blk = -(-(n_pad * TOP_K) // EXPERT_BLOCK) + N_EXPERTS
        blk_start = (jnp.arange(n_blk) * EXPERT_BLOCK).astype(f32)
        blk_e = jnp.minimum(jnp.sum(pends[None, :] <= blk_start[:, None], axis=1), N_EXPERTS - 1).astype(jnp.int32)
        n_used = (pends[-1] / EXPERT_BLOCK).astype(jnp.int32).reshape(1)
        n_rows = n_blk * EXPERT_BLOCK
        pad_start = jnp.concatenate([pstart + total, pends[-1:]]).astype(jnp.int32)
        pad_cnt = jnp.concatenate([padded - total, (n_rows - pends[-1:]) / 8]).astype(jnp.int32)
        xs_sorted = _dispatch(h2_all, _tile_major(dest, ROW_TILE), pad_start, pad_cnt, n_rows, ROW_TILE)
        owns = padded > 0
        e_ids = jnp.arange(N_EXPERTS, dtype=jnp.int32)
        later = jnp.where(owns[None, :] & (e_ids[None, :] > e_ids[:, None]), e_ids[None, :], N_EXPERTS)
        next_owner = jnp.min(later, axis=1)
        next_owner = jnp.where(next_owner < N_EXPERTS, next_owner, -1).astype(jnp.int32)
        slot_of_e = ((jnp.cumsum(owns.astype(jnp.int32)) - 1) % 2).astype(jnp.int32)
        ys = _experts(xs_sorted, blk_e, n_used, next_owner, slot_of_e,
                      w_gate[l], w_up[l], w_down[l], EXPERT_BLOCK)
        shared = _ffn_shared(h2_all, w_sh_gate[l], w_sh_up[l], w_sh_down[l], ROW_TILE)
        dest_c = _tile_major(dest, COMBINE_TILE)
        wts_tok = wts.T
        last = l == depth - 1
        xp = _combine(dest_c, ys, wts_tok, shared, x1, gt2[:, None, :], g_final.astype(f32),
                      0, COMBINE_TILE, last)
        tail = _combine(dest_c, ys, wts_tok, shared, _pad_rows(x1s, TAIL_ROWS),
                        _pad_rows(gt2s, TAIL_ROWS).reshape(-1, COMBINE_TILE, d), g_final.astype(f32),
                        n // COMBINE_TILE, COMBINE_TILE, last)
        xs = tail[:nbs]
        leaves_p.append((k_da, v_da, k_sa, v_sa, k_ix))
        leaves_s.append((k_da_s, v_da_s, k_sa_s, v_sa_s, k_ix_s))

    shapes = [(DA_HEADS, 2, DA_DK), (DA_HEADS, DA_DV), (SA_KV_HEADS, SA_DH), (SA_KV_HEADS, SA_DH), (IDX_DK,)]
    out_p = [jnp.stack([lv[i].reshape((nb, t) + shapes[i]) for lv in leaves_p]) for i in range(5)]
    out_s = [jnp.stack([lv[i].reshape((nbs, dec_seq) + shapes[i]) for lv in leaves_s]) for i in range(5)]
    return (xp.reshape(nb, t, d), xs.reshape(nbs, dec_seq, d), *out_p, *out_s)
```

```python
import functools
import math

import jax
import jax.numpy as jnp
from jax import lax
from jax.experimental import pallas as pl
from jax.experimental.pallas import tpu as pltpu

DA_HEADS = 8
DA_DK = 64
DA_DV = 2 * DA_DK
SA_HEADS = 8
SA_KV_HEADS = 2
SA_DH = 128
SA_GROUP = SA_HEADS // SA_KV_HEADS
IDX_HEADS = 16
IDX_DK = 64
SA_TOPK_MAX = 256
N_BUCKETS = 32
MAX_DISTANCE = 128
N_EXPERTS = 64
N_GROUPS = 8
GROUP_SIZE = N_EXPERTS // N_GROUPS
TOPK_GROUPS = 4
TOP_K = 8
ROUTED_SCALE = 2.5
PAGE_SIZE = 128
EPS = 1e-6

LANES = 128
VMEM_LIMIT = 56 * 1024 * 1024

BF16 = jnp.bfloat16
NEG = -1e30
INT_MIN = -(2 ** 31)
KEY_NEG_INF = (0xFF800000 ^ 0x7FFFFFFF) - (1 << 32)

_NT = (((1,), (1,)), ((), ()))


def _cparams(sem):
    return pltpu.CompilerParams(dimension_semantics=sem, vmem_limit_bytes=VMEM_LIMIT)


def _dot(a, b):
    return jnp.dot(a, b, preferred_element_type=jnp.float32)


def _dot_nt(a, b):
    return lax.dot_general(a, b, _NT, preferred_element_type=jnp.float32)


def _bf(x):
    return x.astype(BF16)


def _rounded(x):
    return x.astype(BF16).astype(jnp.float32)


def _sigmoid(x):
    return 1.0 / (1.0 + jnp.exp(-x))


def _silu(x):
    return x * _sigmoid(x)


def _float_key(s):
    b = pltpu.bitcast(s, jnp.int32)
    return b ^ ((b >> 31) & jnp.int32(0x7FFFFFFF))


def _linear_small_kernel(x_ref, w_ref, b_ref, o_ref, *, silu_in):
    x = x_ref[...]
    if silu_in:
        x = _silu(x)
    o_ref[...] = _dot(_bf(x), _bf(w_ref[...])) + b_ref[...]


def _linear_small(x, w, b=None, *, silu_in=False, tn=512):
    m0, k = x.shape
    m = -(-m0 // 16) * 16
    x = jnp.pad(x, ((0, m - m0), (0, 0)))
    n = w.shape[1]
    tn = min(tn, n)
    if b is None:
        b = jnp.zeros((1, n), jnp.float32)
    out = pl.pallas_call(
        functools.partial(_linear_small_kernel, silu_in=silu_in),
        grid=(pl.cdiv(n, tn),),
        in_specs=[pl.BlockSpec((m, k), lambda j: (0, 0)),
                  pl.BlockSpec((k, tn), lambda j: (0, j)),
                  pl.BlockSpec((1, tn), lambda j: (0, j))],
        out_specs=pl.BlockSpec((m, tn), lambda j: (0, j)),
        out_shape=jax.ShapeDtypeStruct((m, n), jnp.float32),
        compiler_params=_cparams(("arbitrary",)),
        name="linear_small",
    )(x, w, b.reshape(1, n))
    return out[:m0]


def _norm_mod_kernel(x_ref, g_ref, sc_ref, sh_ref, o_ref):
    x = x_ref[...]
    y = x * lax.rsqrt(jnp.mean(x * x, axis=-1, keepdims=True) + EPS) * g_ref[...]
    o_ref[...] = (y * (1.0 + sc_ref[0]) + sh_ref[0]).astype(o_ref.dtype)


def _norm_mod(x, g, sc, sh, rows_per_batch, tm):
    n, d = x.shape
    per = rows_per_batch // tm
    return pl.pallas_call(
        _norm_mod_kernel,
        grid=(n // tm,),
        in_specs=[pl.BlockSpec((tm, d), lambda i: (i, 0)),
                  pl.BlockSpec((1, d), lambda i: (0, 0)),
                  pl.BlockSpec((1, 1, d), lambda i: (i // per, 0, 0)),
                  pl.BlockSpec((1, 1, d), lambda i: (i // per, 0, 0))],
        out_specs=pl.BlockSpec((tm, d), lambda i: (i, 0)),
        out_shape=jax.ShapeDtypeStruct((n, d), BF16),
        compiler_params=_cparams(("arbitrary",)),
        name="norm_mod",
    )(x, g.reshape(1, d), sc[:, None, :], sh[:, None, :])


def _mm_kernel(x_ref, w_ref, *rest, scale, sigmoid, n_out, transposed_out):
    o_refs, wbf_ref = rest[:n_out], rest[-1]

    @pl.when(pl.program_id(1) == 0)
    def _():
        wbf_ref[...] = w_ref[...].astype(BF16)

    acc = _dot(x_ref[...], wbf_ref[...])
    if scale != 1.0:
        acc = acc * scale
    if sigmoid:
        acc = _sigmoid(acc)
    for o in o_refs:
        o[...] = acc.astype(o.dtype)
    if transposed_out:
        rest[n_out][0] = acc.T


def _mm(x, w, col0, ncols, out_dtypes, *, scale=1.0, sigmoid=False, tm=1024, tn=1024, rows_per_batch=None):
    m, k = x.shape
    tn = min(tn, ncols)
    tm = min(tm, m)
    assert col0 % tn == 0 and ncols % tn == 0 and m % tm == 0
    jb = col0 // tn
    out_specs = [pl.BlockSpec((tm, tn), lambda j, i: (i, j)) for _ in out_dtypes]
    out_shape = [jax.ShapeDtypeStruct((m, ncols), dt) for dt in out_dtypes]
    if rows_per_batch is not None:
        per = rows_per_batch // tm
        out_specs.append(pl.BlockSpec((1, tn, tm), lambda j, i: (i // per, j, i % per)))
        out_shape.append(jax.ShapeDtypeStruct((m // rows_per_batch, ncols, rows_per_batch), jnp.float32))
    outs = pl.pallas_call(
        functools.partial(_mm_kernel, scale=scale, sigmoid=sigmoid, n_out=len(out_dtypes),
                          transposed_out=rows_per_batch is not None),
        grid=(ncols // tn, m // tm),
        in_specs=[pl.BlockSpec((tm, k), lambda j, i: (i, 0)),
                  pl.BlockSpec((k, tn), lambda j, i: (0, jb + j))],
        out_specs=out_specs,
        out_shape=out_shape,
        scratch_shapes=[pltpu.VMEM((k, tn), BF16)],
        compiler_params=_cparams(("arbitrary", "arbitrary")),
        name="mm_cols",
    )(x, w)
    return outs


def _rel_bucket(dist):
    max_exact = N_BUCKETS // 2
    d = jnp.maximum(dist, 0)
    large = max_exact + (jnp.log(jnp.maximum(d, 1).astype(jnp.float32) / max_exact)
                         / math.log(MAX_DISTANCE / max_exact)
                         * (N_BUCKETS - max_exact)).astype(jnp.int32)
    large = jnp.minimum(large, N_BUCKETS - 1)
    return jnp.where(d < max_exact, d, large)


def _bias_by_distance(table, dists):
    return table[_rel_bucket(dists)].astype(jnp.float32).T


def _toeplitz_kernel(u_ref, o_ref):
    t = o_ref.shape[2]
    x = jnp.broadcast_to(u_ref[0], (t, 2 * t))
    o_ref[0, 0] = pltpu.roll(x, 0, 1, stride=1, stride_axis=0)[:, :t]


def _near_tiles(table, t):
    nh = table.shape[1]
    k = jnp.arange(2 * t, dtype=jnp.int32)
    gens = []
    for off in (0, t):
        d = jnp.where(k < t, off - k, off + 2 * t - k)
        gens.append(jnp.where(d[None] >= 0, _bias_by_distance(table, d), NEG))
    u = jnp.stack(gens, axis=1).reshape(nh * 2, 1, 2 * t)
    return pl.pallas_call(
        _toeplitz_kernel,
        grid=(nh, 2),
        in_specs=[pl.BlockSpec((1, 1, 2 * t), lambda h, o: (h * 2 + o, 0, 0))],
        out_specs=pl.BlockSpec((1, 1, t, t), lambda h, o: (h, o, 0, 0)),
        out_shape=jax.ShapeDtypeStruct((nh, 2, t, t), jnp.float32),
        compiler_params=_cparams(("arbitrary", "arbitrary")),
        name="bias_tiles",
    )(u)


def _fold_lanes(x, op):
    out = x[:, :LANES]
    for c in range(1, x.shape[1] // LANES):
        out = op(out, x[:, c * LANES:(c + 1) * LANES])
    return out


def _chunk_loop(n, fn):
    def body(i, carry):
        for u in range(4):
            fn(4 * i + u)
        return carry

    lax.fori_loop(0, n // 4, body, 0)
    base = (n // 4) * 4

    @pl.when(n % 4 >= 2)
    def _():
        fn(base)
        fn(base + 1)

    @pl.when(n % 2 == 1)
    def _():
        fn(n - 1)


def _da_prompt_kernel(cfar_ref, lam_ref, q_ref, k_ref, v_ref, tile_ref, g_ref, o_ref,
                      s_ref, mpart_ref, shift_ref, lpart_ref, acc_ref, *, tq, out_scale):
    h = pl.program_id(1)
    qi = pl.program_id(2)
    q = q_ref[...]
    lane = lax.broadcasted_iota(jnp.int32, q.shape, 1)
    zero = jnp.zeros_like(q)
    q2 = jnp.concatenate([jnp.where(lane < DA_DK, q, zero), jnp.where(lane >= DA_DK, q, zero)], axis=0)
    cfar = cfar_ref[h]
    n_far = jnp.maximum(qi - 1, 0)
    r2 = 2 * tq

    def chunk_rows(kc):
        return pl.ds(pl.multiple_of(kc * tq, tq), tq)

    def scores(kc, bias):
        s = _dot_nt(q2, k_ref[chunk_rows(kc), :])
        if bias is not None:
            s = s + jnp.concatenate([bias, bias], axis=0)
        s_ref[kc] = s
        mpart_ref[...] = jnp.maximum(mpart_ref[...], _fold_lanes(s, jnp.maximum))

    mpart_ref[...] = jnp.full((r2, LANES), NEG, jnp.float32)
    _chunk_loop(n_far, lambda kc: scores(kc, None))
    m_far = jnp.max(mpart_ref[...], axis=-1, keepdims=True) + cfar
    mpart_ref[...] = jnp.full((r2, LANES), NEG, jnp.float32)

    @pl.when(qi >= 1)
    def _():
        scores(qi - 1, tile_ref[0, 1])
        scores(qi, tile_ref[0, 0])

    @pl.when(qi == 0)
    def _():
        scores(0, tile_ref[0, 0])

    m = jnp.maximum(m_far, jnp.max(mpart_ref[...], axis=-1, keepdims=True))
    shift_ref[0] = jnp.broadcast_to(m - cfar, (r2, LANES))
    shift_ref[1] = jnp.broadcast_to(m, (r2, LANES))

    lpart_ref[...] = jnp.zeros((r2, LANES), jnp.float32)
    acc_ref[...] = jnp.zeros((r2, DA_DV), jnp.float32)

    def weights(kc, which):
        s = s_ref[kc]
        sh = shift_ref[which]
        ps = [jnp.exp(s[:, c * LANES:(c + 1) * LANES] - sh) for c in range(tq // LANES)]
        tot = ps[0]
        for pc in ps[1:]:
            tot = tot + pc
        lpart_ref[...] = lpart_ref[...] + tot
        p = jnp.concatenate(ps, axis=1).astype(BF16)
        acc_ref[...] = acc_ref[...] + _dot(p, v_ref[chunk_rows(kc), :])

    _chunk_loop(n_far, lambda kc: weights(kc, 0))

    @pl.when(qi >= 1)
    def _():
        weights(qi - 1, 1)
        weights(qi, 1)

    @pl.when(qi == 0)
    def _():
        weights(0, 1)

    lam = lam_ref[0]
    a = acc_ref[...] / jnp.sum(lpart_ref[...], axis=-1, keepdims=True)
    o = a[:tq] - lam * a[tq:]
    o = o * lax.rsqrt(jnp.mean(o * o, axis=-1, keepdims=True) + EPS) * g_ref[...]
    o_ref[...] = (o * out_scale).astype(o_ref.dtype)


def _da_prompt(q, k, v, tiles, cfar, lam, g_subln, nb, t, tq, out_scale):
    n = q.shape[0]
    nq = t // tq
    grid_spec = pltpu.PrefetchScalarGridSpec(
        num_scalar_prefetch=0,
        grid=(nb, DA_HEADS, nq),
        in_specs=[pl.BlockSpec(memory_space=pltpu.SMEM),
                  pl.BlockSpec(memory_space=pltpu.SMEM),
                  pl.BlockSpec((tq, LANES), lambda b, h, i: (b * nq + i, h)),
                  pl.BlockSpec((t, LANES), lambda b, h, i: (b, h)),
                  pl.BlockSpec((t, LANES), lambda b, h, i: (b, h)),
                  pl.BlockSpec((1, 2, tq, tq), lambda b, h, i: (h, 0, 0, 0)),
                  pl.BlockSpec((1, DA_DV), lambda b, h, i: (0, 0))],
        out_specs=pl.BlockSpec((tq, LANES), lambda b, h, i: (b * nq + i, h)),
        scratch_shapes=[pltpu.VMEM((nq, 2 * tq, tq), jnp.float32),
                        pltpu.VMEM((2 * tq, LANES), jnp.float32),
                        pltpu.VMEM((2, 2 * tq, LANES), jnp.float32),
                        pltpu.VMEM((2 * tq, LANES), jnp.float32),
                        pltpu.VMEM((2 * tq, DA_DV), jnp.float32)])
    return pl.pallas_call(
        functools.partial(_da_prompt_kernel, tq=tq, out_scale=out_scale),
        grid_spec=grid_spec,
        out_shape=jax.ShapeDtypeStruct((n, DA_HEADS * DA_DV), BF16),
        compiler_params=_cparams(("arbitrary", "arbitrary", "arbitrary")),
        name="da_prompt",
    )(cfar, lam, q, k, v, tiles, g_subln.reshape(1, DA_DV))


def _sa_prompt_kernel(cfar_ref, qs_ref, qx_ref, mq_ref, mk_ref, ks_ref, vs_ref, tile_ref, o_ref,
                      k2_ref, key_ref, hi_ref, lo_ref, cut_ref, s_ref, mpart_ref, shift_ref, lpart_ref,
                      acc_ref, *, tq, topk):
    qi = pl.program_id(1)
    n_chunks = qi + 1
    t = mk_ref.shape[0]

    @pl.when(qi == 0)
    def _():
        kix = mk_ref[:, :LANES].astype(jnp.float32)
        lane = lax.broadcasted_iota(jnp.int32, kix.shape, 1)
        k2_ref[0] = jnp.where(lane < IDX_DK, kix, 0.0).astype(BF16)
        k2_ref[1] = jnp.where(lane >= IDX_DK, pltpu.roll(kix, IDX_DK, axis=1), 0.0).astype(BF16)

    wix = mq_ref[:, IDX_DK:IDX_DK + IDX_HEADS]
    wcols = [wix[:, hh:hh + 1] for hh in range(IDX_HEADS)]
    row = lax.broadcasted_iota(jnp.int32, (tq, tq), 0)
    col = lax.broadcasted_iota(jnp.int32, (tq, tq), 1)

    def score_body(kc, carry):
        rows = pl.ds(pl.multiple_of(kc * tq, tq), tq)
        ke = k2_ref[0, rows, :]
        ko = k2_ref[1, rows, :]
        sc = jnp.zeros((tq, tq), jnp.float32)
        for p in range(IDX_HEADS // 2):
            qp = qx_ref[:, p * LANES:(p + 1) * LANES]
            sc = sc + wcols[2 * p] * jnp.maximum(_dot_nt(qp, ke), 0.0)
            sc = sc + wcols[2 * p + 1] * jnp.maximum(_dot_nt(qp, ko), 0.0)
        sc = sc * (IDX_DK ** -0.5 * IDX_HEADS ** -0.5)
        sc = jnp.where((kc < qi) | (row >= col), sc, -jnp.inf)
        key = _float_key(sc)
        key_ref[kc] = key
        hi_ref[kc] = (key >> 16).astype(jnp.int16)
        return carry

    lax.fori_loop(0, n_chunks, score_body, 0)

    i16 = jnp.int16
    i16_min = -(2 ** 15)
    one_i = jnp.ones((tq, tq), i16)
    zero_i = jnp.zeros((tq, tq), i16)
    ones_col = jnp.ones((tq, LANES), BF16)

    def wide16(x):
        return jnp.concatenate([x] * (tq // LANES), axis=1).astype(i16)

    def count_ge16(ref16, cand):
        c16 = wide16(cand)

        def body(kc, acc):
            return acc + jnp.where(ref16[kc] >= c16, one_i, zero_i)
        acc = lax.fori_loop(0, n_chunks, body, zero_i)
        return _dot(acc.astype(jnp.float32).astype(BF16), ones_col)

    def search16(ref16, base, need):
        def bit_body(it, carry):
            v, cnt_v = carry
            cand = v + (jnp.int32(1) << (15 - it))
            cnt = base + count_ge16(ref16, cand)
            ok = cnt >= need
            return jnp.where(ok, cand, v), jnp.where(ok, cnt, cnt_v)
        v0 = jnp.full((tq, LANES), i16_min, jnp.int32)
        c0 = jnp.full((tq, LANES), 3.0e38, jnp.float32)
        return lax.fori_loop(0, 16, bit_body, (v0, c0))

    zero_cnt = jnp.zeros((tq, LANES), jnp.float32)
    t_hi, _ = search16(hi_ref, zero_cnt, float(topk))
    n_above = jnp.where(t_hi < 2 ** 15 - 1, count_ge16(hi_ref, jnp.minimum(t_hi + 1, 2 ** 15 - 1)), 0.0)
    t_hi16 = wide16(t_hi)

    def lower_body(kc, carry):
        lo = ((key_ref[kc] & jnp.int32(0xFFFF)) - 2 ** 15).astype(i16)
        lo_ref[kc] = jnp.where(hi_ref[kc] == t_hi16, lo, jnp.full((tq, tq), i16_min, i16))
        return carry

    lax.fori_loop(0, n_chunks, lower_body, 0)
    t_lo, cnt_w = search16(lo_ref, n_above, float(topk))
    thr = ((t_hi << 16) + (t_lo + 2 ** 15))[:, :1]
    cnt_thr = cnt_w[:, :1]
    tied = (cnt_thr > float(topk)) & (thr > jnp.int32(KEY_NEG_INF))
    need_tie = jnp.max(jnp.where(tied, 1.0, 0.0)) > 0.0
    thr = jnp.maximum(thr, jnp.int32(KEY_NEG_INF + 1))

    cut_ref[...] = jnp.full((tq, 1), 2 ** 30, jnp.int32)

    @pl.when(need_tie)
    def _():
        def gt_body(kc, acc):
            g = jnp.where(key_ref[kc] > thr, 1.0, 0.0)
            return acc + jnp.sum(g, axis=-1, keepdims=True)
        n_gt = lax.fori_loop(0, n_chunks, gt_body, jnp.zeros((tq, 1), jnp.float32))
        need = float(topk) - n_gt
        n_bits = max(1, int(t).bit_length())

        def cut_body(it, cut):
            cand = cut + (jnp.int32(1) << (n_bits - 1 - it))

            def eq_body(kc, acc):
                pos = kc * tq + col
                e = jnp.where((key_ref[kc] == thr) & (pos < cand), 1.0, 0.0)
                return acc + jnp.sum(e, axis=-1, keepdims=True)
            n_eq = lax.fori_loop(0, n_chunks, eq_body, jnp.zeros((tq, 1), jnp.float32))
            return jnp.where(n_eq <= need, cand, cut)
        cut_ref[...] = lax.fori_loop(0, n_bits, cut_body, jnp.zeros((tq, 1), jnp.int32))

    cut = cut_ref[...]

    n_far = jnp.maximum(qi - 1, 0)
    rg = SA_GROUP * tq

    def chunk_rows(kc):
        return pl.ds(pl.multiple_of(kc * tq, tq), tq)

    for g in range(SA_KV_HEADS):
        heads = [g * SA_GROUP + j for j in range(SA_GROUP)]
        qg = jnp.concatenate([qs_ref[:, hh * LANES:(hh + 1) * LANES] for hh in heads], axis=0)
        cf_rows = jnp.concatenate([jnp.full((tq, 1), cfar_ref[hh], jnp.float32) for hh in heads], axis=0)

        def scores(kc, kind, qg=qg, heads=heads, g=g):
            key = key_ref[kc]
            sel = (key > thr) | ((key == thr) & (kc * tq + col < cut))
            s_all = _dot_nt(qg, ks_ref[chunk_rows(kc), g * SA_DH:(g + 1) * SA_DH])
            parts = []
            for j, hh in enumerate(heads):
                s = s_all[j * tq:(j + 1) * tq]
                if kind is not None:
                    s = s + tile_ref[hh, kind]
                parts.append(jnp.where(sel, s, NEG))
            s = jnp.concatenate(parts, axis=0)
            s_ref[kc] = s
            mpart_ref[...] = jnp.maximum(mpart_ref[...], _fold_lanes(s, jnp.maximum))

        mpart_ref[...] = jnp.full((rg, LANES), NEG, jnp.float32)
        _chunk_loop(n_far, lambda kc, f=scores: f(kc, None))
        m_far = jnp.max(mpart_ref[...], axis=-1, keepdims=True) + cf_rows
        mpart_ref[...] = jnp.full((rg, LANES), NEG, jnp.float32)

        @pl.when(qi >= 1)
        def _(f=scores):
            f(qi - 1, 1)
            f(qi, 0)

        @pl.when(qi == 0)
        def _(f=scores):
            f(0, 0)

        m = jnp.maximum(m_far, jnp.max(mpart_ref[...], axis=-1, keepdims=True))
        shift_ref[0] = jnp.broadcast_to(m - cf_rows, (rg, LANES))
        shift_ref[1] = jnp.broadcast_to(m, (rg, LANES))
        lpart_ref[...] = jnp.zeros((rg, LANES), jnp.float32)
        acc_ref[...] = jnp.zeros((rg, SA_DH), jnp.float32)

        def weights(kc, which, g=g):
            s = s_ref[kc]
            sh = shift_ref[which]
            ps = [jnp.exp(s[:, c * LANES:(c + 1) * LANES] - sh) for c in range(tq // LANES)]
            tot = ps[0]
            for pc in ps[1:]:
                tot = tot + pc
            lpart_ref[...] = lpart_ref[...] + tot
            p = jnp.concatenate(ps, axis=1).astype(BF16)
            acc_ref[...] = acc_ref[...] + _dot(p, vs_ref[chunk_rows(kc), g * SA_DH:(g + 1) * SA_DH])

        _chunk_loop(n_far, lambda kc, f=weights: f(kc, 0))

        @pl.when(qi >= 1)
        def _(f=weights):
            f(qi - 1, 1)
            f(qi, 1)

        @pl.when(qi == 0)
        def _(f=weights):
            f(0, 1)

        a = acc_ref[...] / jnp.sum(lpart_ref[...], axis=-1, keepdims=True)
        for j, hh in enumerate(heads):
            o_ref[:, hh * SA_DH:(hh + 1) * SA_DH] = a[j * tq:(j + 1) * tq].astype(o_ref.dtype)


def _sa_prompt(q_sa, q_ix, misc_q, misc_k, k_sa, v_sa, tiles, cfar, nb, t, tq, topk):
    n = q_sa.shape[0]
    nq = t // tq
    mw = misc_q.shape[1]
    kvw = SA_KV_HEADS * SA_DH
    once = dict(pipeline_mode=pl.Buffered(1))
    rg = SA_GROUP * tq
    return pl.pallas_call(
        functools.partial(_sa_prompt_kernel, tq=tq, topk=topk),
        grid=(nb, nq),
        in_specs=[pl.BlockSpec(memory_space=pltpu.SMEM),
                  pl.BlockSpec((tq, SA_HEADS * SA_DH), lambda b, i: (b * nq + i, 0)),
                  pl.BlockSpec((tq, IDX_HEADS * IDX_DK), lambda b, i: (b * nq + i, 0)),
                  pl.BlockSpec((tq, mw), lambda b, i: (b * nq + i, 0)),
                  pl.BlockSpec((t, mw), lambda b, i: (b, 0), **once),
                  pl.BlockSpec((t, kvw), lambda b, i: (b, 0), **once),
                  pl.BlockSpec((t, kvw), lambda b, i: (b, 0), **once),
                  pl.BlockSpec((SA_HEADS, 2, tq, tq), lambda b, i: (0, 0, 0, 0), **once)],
        out_specs=pl.BlockSpec((tq, SA_HEADS * SA_DH), lambda b, i: (b * nq + i, 0)),
        out_shape=jax.ShapeDtypeStruct((n, SA_HEADS * SA_DH), BF16),
        scratch_shapes=[pltpu.VMEM((2, t, LANES), BF16),
                        pltpu.VMEM((nq, tq, tq), jnp.int32),
                        pltpu.VMEM((nq, tq, tq), jnp.int16),
                        pltpu.VMEM((nq, tq, tq), jnp.int16),
                        pltpu.VMEM((tq, 1), jnp.int32),
                        pltpu.VMEM((nq, rg, tq), jnp.float32),
                        pltpu.VMEM((rg, LANES), jnp.float32),
                        pltpu.VMEM((2, rg, LANES), jnp.float32),
                        pltpu.VMEM((rg, LANES), jnp.float32),
                        pltpu.VMEM((rg, SA_DH), jnp.float32)],
        compiler_params=_cparams(("arbitrary", "arbitrary")),
        name="sa_prompt",
    )(cfar, q_sa, q_ix, misc_q, misc_k, k_sa, v_sa, tiles)


def _proj_gate_kernel(oda_ref, osa_ref, wpd_ref, wps_ref, gda_ref, gsa_ref, o_ref, wbf_ref):
    @pl.when(pl.program_id(1) == 0)
    def _():
        wbf_ref[0] = wpd_ref[...].astype(BF16)
        wbf_ref[1] = wps_ref[...].astype(BF16)

    a = _dot(oda_ref[...], wbf_ref[0])
    b = _dot(osa_ref[...], wbf_ref[1])
    o_ref[...] = (gda_ref[...].astype(jnp.float32) * a
                  + gsa_ref[...].astype(jnp.float32) * b).astype(o_ref.dtype)


def _proj_gate(o_da, o_sa, w_pd, w_ps, gates, d, tm=1024, tn=512):
    n, kd = o_da.shape
    ks = o_sa.shape[1]
    tn = min(tn, d)
    tm = min(tm, n)
    nj = d // tn
    return pl.pallas_call(
        _proj_gate_kernel,
        grid=(nj, n // tm),
        in_specs=[pl.BlockSpec((tm, kd), lambda j, i: (i, 0)),
                  pl.BlockSpec((tm, ks), lambda j, i: (i, 0)),
                  pl.BlockSpec((kd, tn), lambda j, i: (0, j)),
                  pl.BlockSpec((ks, tn), lambda j, i: (0, j)),
                  pl.BlockSpec((tm, tn), lambda j, i: (i, j)),
                  pl.BlockSpec((tm, tn), lambda j, i: (i, nj + j))],
        out_specs=pl.BlockSpec((tm, tn), lambda j, i: (i, j)),
        out_shape=jax.ShapeDtypeStruct((n, d), BF16),
        scratch_shapes=[pltpu.VMEM((2, kd, tn), BF16)],
        compiler_params=_cparams(("arbitrary", "arbitrary")),
        name="proj_gate",
    )(o_da, o_sa, w_pd, w_ps, gates, gates)


def _mm_resid_kernel(m_ref, w_ref, x_ref, gt_ref, o_ref, wbf_ref):
    @pl.when(pl.program_id(1) == 0)
    def _():
        wbf_ref[...] = w_ref[...].astype(BF16)

    o_ref[...] = x_ref[...] + gt_ref[0] * _dot(m_ref[...], wbf_ref[...])


def _mm_resid(m, w, x, gt, rows_per_batch, tm=1024, tn=1024):
    n, k = m.shape
    d = w.shape[1]
    tn = min(tn, d)
    tm = min(tm, n)
    per = rows_per_batch // tm
    return pl.pallas_call(
        _mm_resid_kernel,
        grid=(d // tn, n // tm),
        in_specs=[pl.BlockSpec((tm, k), lambda j, i: (i, 0)),
                  pl.BlockSpec((k, tn), lambda j, i: (0, j)),
                  pl.BlockSpec((tm, tn), lambda j, i: (i, j)),
                  pl.BlockSpec((1, 1, tn), lambda j, i: (i // per, 0, j))],
        out_specs=pl.BlockSpec((tm, tn), lambda j, i: (i, j)),
        out_shape=jax.ShapeDtypeStruct((n, d), jnp.float32),
        scratch_shapes=[pltpu.VMEM((k, tn), BF16)],
        compiler_params=_cparams(("arbitrary", "arbitrary")),
        name="mm_resid",
    )(m, w, x, gt[:, None, :])


def _norm_router_kernel(x_ref, g_ref, sc_ref, sh_ref, wr_ref, th_ref, tl_ref, h_ref, lg_ref):
    last = pl.num_programs(0) - 1

    @pl.when(pl.program_id(0) < last)
    def _():
        x = x_ref[...]
        y = x * lax.rsqrt(jnp.mean(x * x, axis=-1, keepdims=True) + EPS) * g_ref[...]
        h = y * (1.0 + sc_ref[0]) + sh_ref[0]
        h_ref[...] = h.astype(h_ref.dtype)
        lg_ref[...] = _dot_nt(_bf(wr_ref[...]), _bf(h))

    @pl.when(pl.program_id(0) == last)
    def _():
        h_ref[...] = th_ref[...]
        lg_ref[...] = tl_ref[...]


def _norm_router(x, g, sc, sh, w_router_t, tail_h, tail_lg, rows_per_batch, tm):
    n, d = x.shape
    per = rows_per_batch // tm
    nt = n // tm
    assert tail_h.shape == (tm, d) and tail_lg.shape == (N_EXPERTS, tm)
    row = lambda i: jnp.minimum(i, nt - 1)
    return pl.pallas_call(
        _norm_router_kernel,
        grid=(nt + 1,),
        in_specs=[pl.BlockSpec((tm, d), lambda i: (row(i), 0)),
                  pl.BlockSpec((1, d), lambda i: (0, 0)),
                  pl.BlockSpec((1, 1, d), lambda i: (row(i) // per, 0, 0)),
                  pl.BlockSpec((1, 1, d), lambda i: (row(i) // per, 0, 0)),
                  pl.BlockSpec((N_EXPERTS, d), lambda i: (0, 0)),
                  pl.BlockSpec((tm, d), lambda i: (0, 0)),
                  pl.BlockSpec((N_EXPERTS, tm), lambda i: (0, 0))],
        out_specs=[pl.BlockSpec((tm, d), lambda i: (i, 0)),
                   pl.BlockSpec((N_EXPERTS, tm), lambda i: (0, i))],
        out_shape=[jax.ShapeDtypeStruct((n + tm, d), BF16),
                   jax.ShapeDtypeStruct((N_EXPERTS, n + tm), jnp.float32)],
        compiler_params=_cparams(("arbitrary",)),
        name="norm_router",
    )(x, g.reshape(1, d), sc[:, None, :], sh[:, None, :], w_router_t, tail_h, tail_lg)


def _route_kernel(lg_ref, b_ref, eidx_ref, wts_ref, rank_ref, cnt_ref, *, tn):
    shape = (N_GROUPS, GROUP_SIZE, tn)
    sc = _sigmoid(lg_ref[...])
    biased = sc + b_ref[...]
    e_iota = lax.broadcasted_iota(jnp.int32, shape, 1)
    g_iota3 = lax.broadcasted_iota(jnp.int32, shape, 0)
    flat_iota = g_iota3 * GROUP_SIZE + e_iota
    g_iota = lax.broadcasted_iota(jnp.int32, (N_GROUPS, 1, tn), 0)
    ninf = -jnp.inf

    m1 = jnp.max(biased, axis=1, keepdims=True)
    first = jnp.min(jnp.where(biased == m1, e_iota, GROUP_SIZE), axis=1, keepdims=True)
    m2 = jnp.max(jnp.where(e_iota == first, ninf, biased), axis=1, keepdims=True)
    cur = m1 + m2
    gsel = jnp.zeros((N_GROUPS, 1, tn), jnp.float32)
    for _ in range(TOPK_GROUPS):
        mx = jnp.max(cur, axis=0, keepdims=True)
        idx = jnp.min(jnp.where(cur == mx, g_iota, N_GROUPS), axis=0, keepdims=True)
        hit = g_iota == idx
        gsel = jnp.where(hit, 1.0, gsel)
        cur = jnp.where(hit, ninf, cur)

    cur = jnp.where(gsel > 0.0, biased, ninf)
    hits, ws = [], []
    for k in range(TOP_K):
        mx = jnp.max(jnp.max(cur, axis=1, keepdims=True), axis=0, keepdims=True)
        cand = jnp.where(cur == mx, flat_iota, N_EXPERTS)
        idx = jnp.min(jnp.min(cand, axis=1, keepdims=True), axis=0, keepdims=True)
        hit = flat_iota == idx
        w = jnp.sum(jnp.sum(jnp.where(hit, sc, 0.0), axis=1, keepdims=True), axis=0, keepdims=True)
        eidx_ref[k:k + 1, :] = idx.reshape(1, tn)
        hits.append(hit)
        ws.append(w)
        cur = jnp.where(hit, ninf, cur)
    wsum = ws[0]
    for w in ws[1:]:
        wsum = wsum + w
    for k in range(TOP_K):
        wts_ref[k:k + 1, :] = (ws[k] / wsum * ROUTED_SCALE).reshape(1, tn)

    member = jnp.zeros(shape, jnp.float32)
    for hit in hits:
        member = jnp.where(hit, 1.0, member)
    member2 = member.reshape(N_EXPERTS, tn)
    r = lax.broadcasted_iota(jnp.int32, (tn, tn), 0)
    c = lax.broadcasted_iota(jnp.int32, (tn, tn), 1)
    upper = jnp.where(r < c, 1.0, 0.0).astype(BF16)
    prefix = _dot(member2.astype(BF16), upper).reshape(shape)
    for k in range(TOP_K):
        rk = jnp.sum(jnp.sum(jnp.where(hits[k], prefix, 0.0), axis=1, keepdims=True), axis=0, keepdims=True)
        rank_ref[k:k + 1, :] = rk.reshape(1, tn)
    cnt = jnp.sum(member2, axis=1, keepdims=True)
    cnt_ref[0] = jnp.broadcast_to(cnt, (N_EXPERTS, LANES))


def _route(logits_t, b_router, tn):
    n_pad = logits_t.shape[1]
    nt = n_pad // tn
    lg3 = logits_t.reshape(N_GROUPS, GROUP_SIZE, n_pad)
    b3 = b_router.astype(jnp.float32).reshape(N_GROUPS, GROUP_SIZE, 1)
    row = lambda dt: jax.ShapeDtypeStruct((TOP_K, n_pad), dt)
    return pl.pallas_call(
        functools.partial(_route_kernel, tn=tn),
        grid=(nt,),
        in_specs=[pl.BlockSpec((N_GROUPS, GROUP_SIZE, tn), lambda i: (0, 0, i)),
                  pl.BlockSpec((N_GROUPS, GROUP_SIZE, 1), lambda i: (0, 0, 0))],
        out_specs=[pl.BlockSpec((TOP_K, tn), lambda i: (0, i)),
                   pl.BlockSpec((TOP_K, tn), lambda i: (0, i)),
                   pl.BlockSpec((TOP_K, tn), lambda i: (0, i)),
                   pl.BlockSpec((1, N_EXPERTS, LANES), lambda i: (i, 0, 0))],
        out_shape=[row(jnp.int32), row(jnp.float32), row(jnp.float32),
                   jax.ShapeDtypeStruct((nt, N_EXPERTS, LANES), jnp.float32)],
        compiler_params=_cparams(("arbitrary",)),
        name="route",
    )(lg3, b3)


def _dest_kernel(eidx_ref, rank_ref, base_ref, o_ref, *, tn):
    e_iota = lax.broadcasted_iota(jnp.int32, (N_EXPERTS, tn), 0)
    base = base_ref[0][:, :1]
    for k in range(TOP_K):
        onehot = e_iota == eidx_ref[k:k + 1, :]
        b = jnp.sum(jnp.where(onehot, base, 0.0), axis=0, keepdims=True)
        o_ref[k:k + 1, :] = (b + rank_ref[k:k + 1, :]).astype(jnp.int32)


def _dest(eidx, rank, base, tn):
    n_pad = eidx.shape[1]
    return pl.pallas_call(
        functools.partial(_dest_kernel, tn=tn),
        grid=(n_pad // tn,),
        in_specs=[pl.BlockSpec((TOP_K, tn), lambda i: (0, i)),
                  pl.BlockSpec((TOP_K, tn), lambda i: (0, i)),
                  pl.BlockSpec((1, N_EXPERTS, LANES), lambda i: (i, 0, 0))],
        out_specs=pl.BlockSpec((TOP_K, tn), lambda i: (0, i)),
        out_shape=jax.ShapeDtypeStruct((TOP_K, n_pad), jnp.int32),
        compiler_params=_cparams(("arbitrary",)),
        name="dest",
    )(eidx, rank, base)


def _pack_words(lo_f32, hi_f32):
    lo = lax.shift_right_logical(pltpu.bitcast(lo_f32, jnp.uint32), jnp.uint32(16))
    hi = pltpu.bitcast(hi_f32, jnp.uint32) & jnp.uint32(0xFFFF0000)
    return hi | lo


def _unpack_words(w):
    lo = pltpu.bitcast(lax.shift_left(w, jnp.uint32(16)), jnp.float32)
    hi = pltpu.bitcast(w & jnp.uint32(0xFFFF0000), jnp.float32)
    return lo, hi


def _bf16_exact(x):
    return x.astype(BF16).astype(jnp.float32)


def _dispatch_kernel(pstart_ref, pcnt_ref, h_ref, dest_hbm, xs_hbm, dsm, pk, zrow, sem_d, sem_r,
                     *, tn, nt):
    i = pl.program_id(0)
    half = pk.shape[1]

    @pl.when(i < nt)
    def _():
        cp = pltpu.make_async_copy(dest_hbm.at[pl.ds(i * (TOP_K * tn), TOP_K * tn)], dsm, sem_d)
        cp.start()
        x = h_ref[...]
        pk[...] = _pack_words(x[:, :half].astype(jnp.float32), x[:, half:].astype(jnp.float32))
        cp.wait()

        def body(r, carry):
            for k in range(TOP_K):
                d = dsm[k * tn + r]
                pltpu.make_async_copy(pk.at[pl.ds(r, 1), :], xs_hbm.at[pl.ds(d, 1), :],
                                      sem_r).start(priority=k % 2)
            return carry

        lax.fori_loop(0, tn, body, 0)
        for k in range(TOP_K):
            pltpu.make_async_copy(pk, xs_hbm.at[pl.ds(0, tn), :], sem_r).wait()

    @pl.when(i == nt)
    def _():
        zrow[...] = jnp.zeros(zrow.shape, zrow.dtype)

        def per_expert(e, carry):
            s0 = pstart_ref[e]
            c = pcnt_ref[e]
            head = jnp.minimum(c, (8 - s0 % 8) % 8)
            g0 = s0 + head
            ngrp = (c - head) // 8
            grp = lambda r: pl.ds(pl.multiple_of(g0 + r * 8, 8), 8)

            def start(r, cc):
                pltpu.make_async_copy(zrow.at[pl.ds(0, 1), :], xs_hbm.at[pl.ds(s0 + r, 1), :], sem_r).start()
                return cc

            def wait(r, cc):
                pltpu.make_async_copy(zrow.at[pl.ds(0, 1), :], xs_hbm.at[pl.ds(s0, 1), :], sem_r).wait()
                return cc

            def gstart(r, cc):
                pltpu.make_async_copy(zrow, xs_hbm.at[grp(r), :], sem_r).start()
                return cc

            def gwait(r, cc):
                pltpu.make_async_copy(zrow, xs_hbm.at[grp(0), :], sem_r).wait()
                return cc

            lax.fori_loop(0, head, start, 0)
            lax.fori_loop(0, ngrp, gstart, 0)
            lax.fori_loop(0, head, wait, 0)
            lax.fori_loop(0, ngrp, gwait, 0)
            return carry

        lax.fori_loop(0, N_EXPERTS, per_expert, 0)

        t0 = pstart_ref[N_EXPERTS]
        groups = pcnt_ref[N_EXPERTS]
        rows8 = lambda r: pl.ds(pl.multiple_of(t0 + r * 8, 8), 8)

        def tstart(r, cc):
            pltpu.make_async_copy(zrow, xs_hbm.at[rows8(r), :], sem_r).start()
            return cc

        def twait(r, cc):
            pltpu.make_async_copy(zrow, xs_hbm.at[rows8(0), :], sem_r).wait()
            return cc

        lax.fori_loop(0, groups, tstart, 0)
        lax.fori_loop(0, groups, twait, 0)


def _dispatch(h2, dest_flat, pad_start, pad_cnt, n_rows, tn):
    n_pad, d = h2.shape
    nt = n_pad // tn
    grid_spec = pltpu.PrefetchScalarGridSpec(
        num_scalar_prefetch=2,
        grid=(nt + 1,),
        in_specs=[pl.BlockSpec((tn, d), lambda i, a, b: (jnp.minimum(i, nt - 1), 0)),
                  pl.BlockSpec(memory_space=pl.ANY)],
        out_specs=pl.BlockSpec(memory_space=pl.ANY),
        scratch_shapes=[pltpu.SMEM((TOP_K * tn,), jnp.int32),
                        pltpu.VMEM((tn, d // 2), jnp.uint32),
                        pltpu.VMEM((8, d // 2), jnp.uint32),
                        pltpu.SemaphoreType.DMA(()),
                        pltpu.SemaphoreType.DMA(())])
    return pl.pallas_call(
        functools.partial(_dispatch_kernel, tn=tn, nt=nt),
        grid_spec=grid_spec,
        out_shape=jax.ShapeDtypeStruct((n_rows, d // 2), jnp.uint32),
        compiler_params=_cparams(("arbitrary",)),
        name="dispatch",
    )(pad_start, pad_cnt, h2, dest_flat)


def _expert_kernel(blk_e_ref, nused_ref, next_e_ref, slot_ref, x_ref, wg_hbm, wu_hbm, wd_hbm, y_ref,
                   wg_f, wu_f, wd_f, wgb, wub, wdb, sems):
    i = pl.program_id(0)
    nused = nused_ref[0]
    ii = jnp.minimum(i, nused - 1)
    e = blk_e_ref[ii]
    e_prev = blk_e_ref[jnp.maximum(ii - 1, 0)]
    half = x_ref.shape[1]

    def copies(ex, sl):
        return [pltpu.make_async_copy(src.at[ex], dst.at[sl], sems.at[sl, t])
                for t, (src, dst) in enumerate(((wg_hbm, wg_f), (wu_hbm, wu_f), (wd_hbm, wd_f)))]

    @pl.when(i == 0)
    def _():
        for c in copies(e, slot_ref[e]):
            c.start()

    @pl.when((i < nused) & ((i == 0) | (e != e_prev)))
    def _():
        sl = slot_ref[e]
        for c in copies(e, sl):
            c.wait()
        nxt = next_e_ref[e]

        @pl.when(nxt >= 0)
        def _():
            for c in copies(nxt, 1 - sl):
                c.start()

        wgb[...] = wg_f[sl].astype(BF16)
        wub[...] = wu_f[sl].astype(BF16)
        wdb[...] = wd_f[sl].astype(BF16)

    @pl.when(i < nused)
    def _():
        lo, hi = _unpack_words(x_ref[...])
        xl = lo.astype(BF16)
        xh = hi.astype(BF16)
        g = _dot(xl, wgb[:half, :]) + _dot(xh, wgb[half:, :])
        u = _dot(xl, wub[:half, :]) + _dot(xh, wub[half:, :])
        hmid = (_silu(g) * u).astype(BF16)
        y = _dot(hmid, wdb[...])
        y_ref[...] = _pack_words(_bf16_exact(y[:, :half]), _bf16_exact(y[:, half:]))

    @pl.when(i >= nused)
    def _():
        y_ref[...] = jnp.zeros(y_ref.shape, y_ref.dtype)


def _experts(xs, blk_e, nused, next_e, slot, w_gate, w_up, w_down, tb):
    n_rows, half = xs.shape
    _, d, f = w_gate.shape
    nblk = n_rows // tb

    def xmap(i, be, nu, ne, sl):
        return (jnp.minimum(i, nu[0] - 1), 0)

    grid_spec = pltpu.PrefetchScalarGridSpec(
        num_scalar_prefetch=4,
        grid=(nblk,),
        in_specs=[pl.BlockSpec((tb, half), xmap),
                  pl.BlockSpec(memory_space=pl.ANY),
                  pl.BlockSpec(memory_space=pl.ANY),
                  pl.BlockSpec(memory_space=pl.ANY)],
        out_specs=pl.BlockSpec((tb, half), lambda i, be, nu, ne, sl: (i, 0)),
        scratch_shapes=[pltpu.VMEM((2, d, f), jnp.float32),
                        pltpu.VMEM((2, d, f), jnp.float32),
                        pltpu.VMEM((2, f, d), jnp.float32),
                        pltpu.VMEM((d, f), BF16),
                        pltpu.VMEM((d, f), BF16),
                        pltpu.VMEM((f, d), BF16),
                        pltpu.SemaphoreType.DMA((2, 3))])
    return pl.pallas_call(
        _expert_kernel,
        grid_spec=grid_spec,
        out_shape=jax.ShapeDtypeStruct((n_rows, half), jnp.uint32),
        compiler_params=_cparams(("arbitrary",)),
        name="experts",
    )(blk_e, nused, next_e, slot, xs, w_gate, w_up, w_down)


def _ffn_kernel(h_ref, wg_ref, wu_ref, wd_ref, o_ref, wgb, wub, wdb):
    @pl.when(pl.program_id(0) == 0)
    def _():
        wgb[...] = wg_ref[...].astype(BF16)
        wub[...] = wu_ref[...].astype(BF16)
        wdb[...] = wd_ref[...].astype(BF16)

    x = h_ref[...]
    hmid = (_silu(_dot(x, wgb[...])) * _dot(x, wub[...])).astype(BF16)
    o_ref[...] = _dot(hmid, wdb[...]).astype(o_ref.dtype)


def _ffn_shared(h2, wg, wu, wd, tm):
    n_pad, d = h2.shape
    f = wg.shape[1]
    return pl.pallas_call(
        _ffn_kernel,
        grid=(n_pad // tm,),
        in_specs=[pl.BlockSpec((tm, d), lambda i: (i, 0)),
                  pl.BlockSpec((d, f), lambda i: (0, 0)),
                  pl.BlockSpec((d, f), lambda i: (0, 0)),
                  pl.BlockSpec((f, d), lambda i: (0, 0))],
        out_specs=pl.BlockSpec((tm, d), lambda i: (i, 0)),
        out_shape=jax.ShapeDtypeStruct((n_pad, d), BF16),
        scratch_shapes=[pltpu.VMEM((d, f), BF16),
                        pltpu.VMEM((d, f), BF16),
                        pltpu.VMEM((f, d), BF16)],
        compiler_params=_cparams(("arbitrary",)),
        name="ffn_shared",
    )(h2, wg, wu, wd)


def _combine_kernel(dest_hbm, ys_hbm, wts_ref, sh_ref, x_ref, gt_ref, g_ref, o_ref,
                    dsm, buf, sem_d, sem_r, *, tn, tile0, final_norm):
    i = pl.program_id(0)
    cp = pltpu.make_async_copy(dest_hbm.at[pl.ds((tile0 + i) * (TOP_K * tn), TOP_K * tn)], dsm, sem_d)
    cp.start()
    cp.wait()

    def body(r, carry):
        for k in range(TOP_K):
            d = dsm[k * tn + r]
            pltpu.make_async_copy(ys_hbm.at[pl.ds(d, 1), :], buf.at[k, pl.ds(r, 1), :],
                                  sem_r).start(priority=k % 2)
        return carry

    lax.fori_loop(0, tn, body, 0)
    for k in range(TOP_K):
        pltpu.make_async_copy(ys_hbm.at[pl.ds(0, tn), :], buf.at[k], sem_r).wait()

    half = buf.shape[2]
    wts = wts_ref[...]
    acc_lo = jnp.zeros((tn, half), jnp.float32)
    acc_hi = jnp.zeros((tn, half), jnp.float32)
    for k in range(TOP_K):
        lo, hi = _unpack_words(buf[k])
        wk = wts[:, k:k + 1]
        acc_lo = acc_lo + wk * lo
        acc_hi = acc_hi + wk * hi
    sh = sh_ref[...].astype(jnp.float32)
    gt = gt_ref[0]
    x_lo = x_ref[:, :half] + gt[:, :half] * (acc_lo + sh[:, :half])
    x_hi = x_ref[:, half:] + gt[:, half:] * (acc_hi + sh[:, half:])
    if final_norm:
        ms = (jnp.sum(x_lo * x_lo, axis=-1, keepdims=True)
              + jnp.sum(x_hi * x_hi, axis=-1, keepdims=True)) / (2 * half)
        inv = lax.rsqrt(ms + EPS)
        g = g_ref[...]
        x_lo = x_lo * inv * g[:, :half]
        x_hi = x_hi * inv * g[:, half:]
    o_ref[:, :half] = x_lo
    o_ref[:, half:] = x_hi


def _combine(dest_flat, ys, wts_tok, shared, x1, gt3, g_final, tile0, tn, final_norm):
    rows, d = x1.shape
    nt = rows // tn
    gr = gt3.shape[1]
    per = nt // gt3.shape[0]
    return pl.pallas_call(
        functools.partial(_combine_kernel, tn=tn, tile0=tile0, final_norm=final_norm),
        grid=(nt,),
        in_specs=[pl.BlockSpec(memory_space=pl.ANY),
                  pl.BlockSpec(memory_space=pl.ANY),
                  pl.BlockSpec((tn, TOP_K), lambda i: (tile0 + i, 0)),
                  pl.BlockSpec((tn, d), lambda i: (tile0 + i, 0)),
                  pl.BlockSpec((tn, d), lambda i: (i, 0)),
                  pl.BlockSpec((1, gr, d), lambda i: (i // per, 0, 0)),
                  pl.BlockSpec((1, d), lambda i: (0, 0))],
        out_specs=pl.BlockSpec((tn, d), lambda i: (i, 0)),
        out_shape=jax.ShapeDtypeStruct((rows, d), jnp.float32),
        scratch_shapes=[pltpu.SMEM((TOP_K * tn,), jnp.int32),
                        pltpu.VMEM((TOP_K, tn, d // 2), jnp.uint32),
                        pltpu.SemaphoreType.DMA(()),
                        pltpu.SemaphoreType.DMA(())],
        compiler_params=_cparams(("arbitrary",)),
        name="combine",
    )(dest_flat, ys, wts_tok, shared, x1, gt3, g_final.reshape(1, d))


def _page_specs(shape, n, pg):
    def mk(u):
        return pl.BlockSpec((1,) + shape, lambda b, j, pt: (pt[b, j * pg + u], 0, 0))
    return [mk(u) for u in range(n)]


def _kv_page_specs(kshape, vshape, pg, ns):
    def mk(shape, first):
        def one(u):
            def index(b, j, pt):
                step = jnp.minimum(j, ns - 1) if first else jnp.maximum(j - ns, 0)
                return (pt[b, step * pg + u], 0, 0)
            return pl.BlockSpec((1,) + shape, index)
        return [one(u) for u in range(pg)]
    return mk(kshape, True) + mk(vshape, False)


def _softmax_pages(s_ref):
    s = s_ref[...]
    m = jnp.max(jnp.max(s, axis=0, keepdims=True), axis=2, keepdims=True)
    e = jnp.exp(s - m)
    return e / jnp.sum(jnp.sum(e, axis=0, keepdims=True), axis=2, keepdims=True)


def _da_sample_kernel(pt_ref, lam_ref, q_ref, kn_ref, vn_ref, bl_ref, cf_ref, b0_ref, ex_ref, hm_ref,
                      *rest, pg, n_pages):
    k_refs, v_refs = rest[:pg], rest[pg:2 * pg]
    o_ref, s_ref, a_ref, acc_ref = rest[2 * pg:]
    j = pl.program_id(1)
    ns = n_pages // pg
    r = q_ref.shape[1]
    nh = r // 2
    lane = lax.broadcasted_iota(jnp.int32, (r, PAGE_SIZE), 1)

    @pl.when(j < ns)
    def _():
        q = q_ref[0]
        qb = _bf(q)
        for u in range(pg):
            page = j * pg + u
            s = _dot(qb, _bf(k_refs[u][0]))
            s_ref[page] = s + jnp.where(page == (n_pages - 1), bl_ref[...], cf_ref[...])

        @pl.when(j == 0)
        def _():
            s_new = jnp.sum(_rounded(q) * _rounded(kn_ref[0]), axis=-1, keepdims=True) + b0_ref[...]
            s_ref[n_pages] = jnp.where(lane == 0, s_new, NEG)

    @pl.when(j == ns)
    def _():
        p = _softmax_pages(s_ref)
        a = p[:, :nh, :] - lam_ref[0] * p[:, nh:, :]
        a_ref[...] = _bf(jnp.concatenate([a, jnp.zeros_like(a)], axis=1))
        a_new = a_ref[n_pages][:, 0:1].astype(jnp.float32)
        acc_ref[...] = a_new * _rounded(vn_ref[0])

    @pl.when(j >= ns)
    def _():
        acc = acc_ref[...]
        for u in range(pg):
            page = (j - ns) * pg + u
            pe = _bf(_dot(a_ref[page], ex_ref[...]) * hm_ref[...])
            acc = acc + _dot(pe, _bf(v_refs[u][0]))
        acc_ref[...] = acc

    @pl.when(j == pl.num_programs(1) - 1)
    def _():
        o_ref[0] = acc_ref[...]


def _da_sample(page_table, lam, qbd, knew, vnew, bias_last, cfar, bias0, kt, v2, pg):
    nb, n_pages = page_table.shape
    r, w = qbd.shape[1:]
    rows_v, dv = v2.shape[1:]
    nh = rows_v // PAGE_SIZE
    ns = n_pages // pg
    col = jnp.arange(rows_v, dtype=jnp.int32)
    expand = (col[None, :] // nh == jnp.arange(PAGE_SIZE, dtype=jnp.int32)[:, None]).astype(BF16)
    head_mask = (col[None, :] % nh == jnp.arange(r, dtype=jnp.int32)[:, None]).astype(jnp.float32)
    full = lambda shp: pl.BlockSpec(shp, lambda b, j, pt: (0,) * len(shp))
    per_b = lambda shp: pl.BlockSpec((1,) + shp, lambda b, j, pt: (b, 0, 0))
    grid_spec = pltpu.PrefetchScalarGridSpec(
        num_scalar_prefetch=1,
        grid=(nb, 2 * ns),
        in_specs=[pl.BlockSpec(memory_space=pltpu.SMEM),
                  per_b((r, w)), per_b((1, w)), per_b((r, dv)),
                  full((r, PAGE_SIZE)), full((r, 1)), full((r, 1)),
                  full((PAGE_SIZE, rows_v)), full((r, rows_v))]
                 + _kv_page_specs((w, PAGE_SIZE), (rows_v, dv), pg, ns),
        out_specs=per_b((r, dv)),
        scratch_shapes=[pltpu.VMEM((n_pages + 1, r, PAGE_SIZE), jnp.float32),
                        pltpu.VMEM((n_pages + 1, r, PAGE_SIZE), BF16),
                        pltpu.VMEM((r, dv), jnp.float32)])
    return pl.pallas_call(
        functools.partial(_da_sample_kernel, pg=pg, n_pages=n_pages),
        grid_spec=grid_spec,
        out_shape=jax.ShapeDtypeStruct((nb, r, dv), jnp.float32),
        compiler_params=_cparams(("arbitrary", "arbitrary")),
        name="da_sample",
    )(page_table, lam, qbd, knew, vnew, bias_last, cfar, bias0, expand, head_mask,
      *([kt] * pg), *([v2] * pg))


def _idx_sample_kernel(pt_ref, q_ref, w_ref, kn_ref, *rest, pg, n_pages, topk):
    k_refs = rest[:pg]
    sel_ref, sc_ref = rest[pg:]
    j = pl.program_id(1)
    q = q_ref[0]
    w = _rounded(w_ref[0])
    scale = IDX_DK ** -0.5 * IDX_HEADS ** -0.5
    rows = sc_ref.shape[0]
    lane = lax.broadcasted_iota(jnp.int32, (1, PAGE_SIZE), 1)

    @pl.when(j == 0)
    def _():
        sc_ref[...] = jnp.full(sc_ref.shape, -jnp.inf, jnp.float32)
        d = jnp.maximum(jnp.sum(_rounded(q) * _rounded(kn_ref[0]), axis=-1, keepdims=True), 0.0)
        s_new = jnp.sum(w * _rounded(d), axis=0, keepdims=True) * scale
        sc_ref[n_pages:n_pages + 1, :] = jnp.where(lane == 0, s_new, -jnp.inf)

    qb = _bf(q)
    for u in range(pg):
        d = _rounded(jnp.maximum(_dot(qb, _bf(k_refs[u][0])), 0.0))
        sc_ref[pl.ds(j * pg + u, 1), :] = jnp.sum(w * d, axis=0, keepdims=True) * scale

    @pl.when(j == pl.num_programs(1) - 1)
    def _():
        key = _float_key(sc_ref[...])
        pos = (lax.broadcasted_iota(jnp.int32, key.shape, 0) * PAGE_SIZE
               + lax.broadcasted_iota(jnp.int32, key.shape, 1))

        def bit_body(it, thr):
            cand = thr + (jnp.int32(1) << (31 - it))
            cnt = jnp.sum(jnp.where(key >= cand, 1.0, 0.0))
            return jnp.where(cnt >= float(topk), cand, thr)

        thr = lax.fori_loop(0, 32, bit_body, jnp.int32(INT_MIN))
        thr = jnp.maximum(thr, jnp.int32(KEY_NEG_INF + 1))
        need = float(topk) - jnp.sum(jnp.where(key > thr, 1.0, 0.0))
        n_bits = max(1, int(rows * PAGE_SIZE).bit_length())

        def cut_body(it, cut):
            cand = cut + (jnp.int32(1) << (n_bits - 1 - it))
            n_eq = jnp.sum(jnp.where((key == thr) & (pos < cand), 1.0, 0.0))
            return jnp.where(n_eq <= need, cand, cut)

        cut = lax.fori_loop(0, n_bits, cut_body, jnp.int32(0))
        sel_ref[0] = jnp.where((key > thr) | ((key == thr) & (pos < cut)), 1.0, 0.0)


def _idx_sample(page_table, qix, wix, knew, kc, pg, topk):
    nb, n_pages = page_table.shape
    rows = -(-(n_pages + 1) // 8) * 8
    per_b = lambda shp: pl.BlockSpec((1,) + shp, lambda b, j, pt: (b, 0, 0))
    grid_spec = pltpu.PrefetchScalarGridSpec(
        num_scalar_prefetch=1,
        grid=(nb, n_pages // pg),
        in_specs=[per_b((IDX_HEADS, IDX_DK)), per_b((IDX_HEADS, 1)), per_b((1, IDX_DK))]
                 + _page_specs((IDX_DK, PAGE_SIZE), pg, pg),
        out_specs=per_b((rows, PAGE_SIZE)),
        scratch_shapes=[pltpu.VMEM((rows, PAGE_SIZE), jnp.float32)])
    return pl.pallas_call(
        functools.partial(_idx_sample_kernel, pg=pg, n_pages=n_pages, topk=topk),
        grid_spec=grid_spec,
        out_shape=jax.ShapeDtypeStruct((nb, rows, PAGE_SIZE), jnp.float32),
        compiler_params=_cparams(("arbitrary", "arbitrary")),
        name="idx_sample",
    )(page_table, qix, wix, knew, *([kc] * pg))


def _sa_sample_kernel(pt_ref, q_ref, kn_ref, vn_ref, sel_ref, bl_ref, cf_ref, b0_ref, gm_ref, *rest,
                      pg, n_pages):
    k_refs, v_refs = rest[:pg], rest[pg:2 * pg]
    o_ref, s_ref, p_ref, acc_ref = rest[2 * pg:]
    j = pl.program_id(1)
    ns = n_pages // pg
    scale = SA_DH ** -0.5
    r, cols = gm_ref.shape
    lane = lax.broadcasted_iota(jnp.int32, (r, cols), 1)

    @pl.when(j < ns)
    def _():
        q = q_ref[0]
        qb = _bf(q)
        for u in range(pg):
            page = j * pg + u
            s = _dot_nt(qb, _bf(k_refs[u][0])) * scale
            s = s + jnp.where(page == (n_pages - 1), bl_ref[...], cf_ref[...])
            keep = (sel_ref[0, pl.ds(page, 1), :] > 0.0) & (gm_ref[...] > 0.0)
            s_ref[page] = jnp.where(keep, s, NEG)

        @pl.when(j == 0)
        def _():
            on = sel_ref[0, n_pages:n_pages + 1, 0:1] > 0.0
            s_new = jnp.sum(_rounded(q) * _rounded(kn_ref[0]), axis=-1, keepdims=True) * scale + b0_ref[...]
            s_ref[n_pages] = jnp.where((lane == 0) & on, s_new, NEG)

    @pl.when(j == ns)
    def _():
        p_ref[...] = _bf(_softmax_pages(s_ref))
        acc_ref[...] = p_ref[n_pages][:, 0:1].astype(jnp.float32) * _rounded(vn_ref[0])

    @pl.when(j >= ns)
    def _():
        acc = acc_ref[...]
        for u in range(pg):
            acc = acc + _dot(p_ref[(j - ns) * pg + u], _bf(v_refs[u][0]))
        acc_ref[...] = acc

    @pl.when(j == pl.num_programs(1) - 1)
    def _():
        o_ref[0] = acc_ref[...]


def _sa_sample(page_table, q, knew, vnew, sel2, bias_last2, cfar, bias0, group_mask, k2, v2, pg):
    nb, n_pages = page_table.shape
    r, dh = q.shape[1:]
    srows, cols = sel2.shape[1:]
    ns = n_pages // pg
    full = lambda shp: pl.BlockSpec(shp, lambda b, j, pt: (0,) * len(shp))
    per_b = lambda shp: pl.BlockSpec((1,) + shp, lambda b, j, pt: (b, 0, 0))
    grid_spec = pltpu.PrefetchScalarGridSpec(
        num_scalar_prefetch=1,
        grid=(nb, 2 * ns),
        in_specs=[per_b((r, dh)), per_b((r, dh)), per_b((r, dh)), per_b((srows, cols)),
                  full((r, cols)), full((r, 1)), full((r, 1)), full((r, cols))]
                 + _kv_page_specs((cols, dh), (cols, dh), pg, ns),
        out_specs=per_b((r, dh)),
        scratch_shapes=[pltpu.VMEM((n_pages + 1, r, cols), jnp.float32),
                        pltpu.VMEM((n_pages + 1, r, cols), BF16),
                        pltpu.VMEM((r, dh), jnp.float32)])
    return pl.pallas_call(
        functools.partial(_sa_sample_kernel, pg=pg, n_pages=n_pages),
        grid_spec=grid_spec,
        out_shape=jax.ShapeDtypeStruct((nb, r, dh), jnp.float32),
        compiler_params=_cparams(("arbitrary", "arbitrary")),
        name="sa_sample",
    )(page_table, q, knew, vnew, sel2, bias_last2, cfar, bias0, group_mask, *([k2] * pg), *([v2] * pg))


ROW_TILE = 256
COMBINE_TILE = 256
EXPERT_BLOCK = 512
TAIL_ROWS = 256
PAGES_PER_STEP_DA = 16
PAGES_PER_STEP_SA = 32
PAGES_PER_STEP_IDX = 64


def _rms_rows(x, g):
    return x * lax.rsqrt(jnp.mean(x * x, axis=-1, keepdims=True) + EPS) * g


def _pad_rows(x, rows):
    return jnp.pad(x, ((0, rows - x.shape[0]), (0, 0)))


def _tile_major(dest, tn):
    k, n = dest.shape
    return dest.reshape(k, n // tn, tn).transpose(1, 0, 2).reshape(-1)


def kernel(x_prompt, x_sample, c_prompt, c_sample, cache_da_k, cache_da_v, cache_sa_k, cache_sa_v, cache_idx_k, page_table, rel_bias_table, w_ada, b_ada, g_attn, g_ffn, w_in, lambda_q1, lambda_k1, lambda_q2, lambda_k2, g_subln, w_proj_da, w_proj_sa, w_out, w_router, b_router, w_gate, w_up, w_down, w_sh_gate, w_sh_up, w_sh_down, g_final):
    f32, bf16 = jnp.float32, BF16
    nb, t, d = x_prompt.shape
    nbs, dec_seq, _ = x_sample.shape
    assert dec_seq == 1
    depth = w_in.shape[0]
    n_pages = page_table.shape[1]
    past_len = n_pages * PAGE_SIZE
    n = nb * t
    n_pad = n + TAIL_ROWS
    tq = min(256, t)
    tq_da = min(512, t)
    assert t % tq_da == 0
    assert tq >= MAX_DISTANCE and t % tq == 0 and n % ROW_TILE == 0 and nbs <= TAIL_ROWS
    assert TAIL_ROWS == ROW_TILE and EXPERT_BLOCK % 8 == 0
    topk_p = min(SA_TOPK_MAX, t // 4)
    topk_s = min(SA_TOPK_MAX, (past_len + dec_seq) // 4)
    assert topk_p <= tq

    sizes = [DA_HEADS * 2 * DA_DK, DA_HEADS * 2 * DA_DK, DA_HEADS * DA_DV, SA_HEADS * SA_DH,
             SA_KV_HEADS * SA_DH, SA_KV_HEADS * SA_DH, IDX_HEADS * IDX_DK, IDX_DK, IDX_HEADS, d, d]
    offs = [sum(sizes[:i]) for i in range(len(sizes) + 1)]
    (o_qda, o_kda, o_vda, o_qsa, o_ksa, o_vsa, o_qix, o_kix, o_wix, o_gda, o_gsa, _) = offs
    misc_w = 2 * LANES

    table = rel_bias_table.astype(f32)
    cfar = table[N_BUCKETS - 1]
    tiles_da = _near_tiles(table[:, :DA_HEADS], tq_da)
    tiles_sa = _near_tiles(table[:, DA_HEADS:], tq)
    last_dist = past_len - ((n_pages - 1) * PAGE_SIZE + jnp.arange(PAGE_SIZE, dtype=jnp.int32))
    bias_last = _bias_by_distance(table, last_dist)
    bias0 = _bias_by_distance(table, jnp.zeros((1,), jnp.int32))

    xp = x_prompt.reshape(n, d)
    xs = x_sample.reshape(nbs, d)
    c_all = jnp.concatenate([c_prompt, c_sample], axis=0)
    leaves_p, leaves_s = [], []
    for l in range(depth):
        lam_init = 0.8 - 0.6 * math.exp(-0.3 * l)
        lam = (jnp.exp(jnp.sum(lambda_q1[l].astype(f32) * lambda_k1[l].astype(f32)))
               - jnp.exp(jnp.sum(lambda_q2[l].astype(f32) * lambda_k2[l].astype(f32))) + lam_init)
        mod = _linear_small(c_all, w_ada[l], b_ada[l], silu_in=True)
        sh1, sc1, gt1, sh2, sc2, gt2 = jnp.split(mod[:nb], 6, axis=-1)
        sh1s, sc1s, gt1s, sh2s, sc2s, gt2s = jnp.split(mod[nb:], 6, axis=-1)
        w = w_in[l]

        h = _norm_mod(xp, g_attn[l], sc1, sh1, t, ROW_TILE)
        q_da, = _mm(h, w, o_qda, sizes[0], (bf16,), scale=DA_DK ** -0.5)
        k_da_b, k_da_t = _mm(h, w, o_kda, sizes[1], (bf16,), rows_per_batch=t)
        k_da = k_da_t.reshape(nb, DA_HEADS, 2, DA_DK, t).transpose(0, 4, 1, 2, 3)
        v_da, v_da_b = _mm(h, w, o_vda, sizes[2], (f32, bf16))
        q_sa, = _mm(h, w, o_qsa, sizes[3], (bf16,), scale=SA_DH ** -0.5)
        k_sa, k_sa_b = _mm(h, w, o_ksa, sizes[4], (f32, bf16))
        v_sa, v_sa_b = _mm(h, w, o_vsa, sizes[5], (f32, bf16))
        q_ix, = _mm(h, w, o_qix, sizes[6], (bf16,), tn=512)
        misc, misc_b = _mm(h, w, o_kix, misc_w, (f32, bf16))
        gates, = _mm(h, w[:, o_gda:], 0, 2 * d, (bf16,), sigmoid=True)
        k_ix = misc[:, :IDX_DK]

        o_da = _da_prompt(q_da, k_da_b, v_da_b, tiles_da, cfar[:DA_HEADS], lam.reshape(1),
                          g_subln[l].astype(f32), nb, t, tq_da, 1.0 - lam_init)
        o_sa = _sa_prompt(q_sa, q_ix, misc, misc_b, k_sa_b, v_sa_b, tiles_sa, cfar[DA_HEADS:], nb, t, tq, topk_p)
        m = _proj_gate(o_da, o_sa, w_proj_da[l], w_proj_sa[l], gates, d)
        x1 = _mm_resid(m, w_out[l], xp, gt1, t)

        hs = _rms_rows(xs, g_attn[l].astype(f32)) * (1.0 + sc1s) + sh1s
        ps = _linear_small(hs, w)
        seg = lambda i: ps[:, offs[i]:offs[i + 1]]
        q_da_s, k_da_s, v_da_s, q_sa_s, k_sa_s, v_sa_s, q_ix_s, k_ix_s, w_ix_s, gda_s, gsa_s = [
            seg(i) for i in range(11)]

        r_da = 2 * DA_HEADS
        q16 = (q_da_s * DA_DK ** -0.5).reshape(nbs, DA_HEADS, 2, DA_DK).transpose(0, 2, 1, 3).reshape(nbs, r_da, DA_DK)
        blk_of_row = 2 * (jnp.arange(r_da) % DA_HEADS) + jnp.arange(r_da) // DA_HEADS
        place = (blk_of_row[:, None] == jnp.arange(r_da)[None, :]).astype(f32)
        qbd_da = (place[None, :, :, None] * q16[:, :, None, :]).reshape(nbs, r_da, -1)
        both = lambda a: jnp.concatenate([a, a], axis=0)
        kt_da = cache_da_k[l].transpose(0, 2, 3, 4, 1).reshape(-1, DA_HEADS * 2 * DA_DK, PAGE_SIZE)
        v2_da = cache_da_v[l].reshape(-1, PAGE_SIZE * DA_HEADS, DA_DV)
        v_rows = jnp.pad(v_da_s.reshape(nbs, DA_HEADS, DA_DV), ((0, 0), (0, r_da - DA_HEADS), (0, 0)))
        a_da = _da_sample(page_table, lam.reshape(1), qbd_da, k_da_s[:, None, :], v_rows,
                          both(bias_last[:DA_HEADS]), both(cfar[:DA_HEADS, None]), both(bias0[:DA_HEADS]),
                          kt_da, v2_da, math.gcd(PAGES_PER_STEP_DA, n_pages))
        o_da_s = a_da[:, :DA_HEADS]
        o_da_s = _rms_rows(o_da_s, g_subln[l].astype(f32)) * (1.0 - lam_init)

        sel = _idx_sample(page_table, q_ix_s.reshape(nbs, IDX_HEADS, IDX_DK), w_ix_s[:, :, None],
                          k_ix_s[:, None, :], cache_idx_k[l].transpose(0, 2, 1),
                          math.gcd(PAGES_PER_STEP_IDX, n_pages), topk_s)
        r_sa = 2 * SA_HEADS
        kv_of_row = jnp.minimum(jnp.arange(r_sa) // SA_GROUP, SA_KV_HEADS - 1)
        pad_sa = lambda a: jnp.pad(a, ((0, 0), (0, r_sa - SA_HEADS), (0, 0)))
        rep_kv = lambda a: jnp.repeat(a, SA_KV_HEADS, axis=-1)
        col_kv = jnp.arange(PAGE_SIZE * SA_KV_HEADS) % SA_KV_HEADS
        k2_sa = cache_sa_k[l].reshape(-1, PAGE_SIZE * SA_KV_HEADS, SA_DH)
        v2_sa = cache_sa_v[l].reshape(-1, PAGE_SIZE * SA_KV_HEADS, SA_DH)
        a_sa = _sa_sample(page_table, pad_sa(q_sa_s.reshape(nbs, SA_HEADS, SA_DH)),
                          k_sa_s.reshape(nbs, SA_KV_HEADS, SA_DH)[:, kv_of_row],
                          v_sa_s.reshape(nbs, SA_KV_HEADS, SA_DH)[:, kv_of_row],
                          rep_kv(sel), rep_kv(_pad_rows(bias_last[DA_HEADS:], r_sa)),
                          _pad_rows(cfar[DA_HEADS:, None], r_sa), _pad_rows(bias0[DA_HEADS:], r_sa),
                          (col_kv[None, :] == kv_of_row[:, None]).astype(f32),
                          k2_sa, v2_sa, math.gcd(PAGES_PER_STEP_SA, n_pages))
        o_sa_s = a_sa[:, :SA_HEADS]

        pda = _linear_small(o_da_s.reshape(nbs, -1), w_proj_da[l])
        psa = _linear_small(o_sa_s.reshape(nbs, -1), w_proj_sa[l])
        ms = _sigmoid(gda_s) * pda + _sigmoid(gsa_s) * psa
        x1s = xs + gt1s * _linear_small(ms, w_out[l])
        h2s = _rms_rows(x1s, g_ffn[l].astype(f32)) * (1.0 + sc2s) + sh2s
        lg_s = _linear_small(h2s, w_router[l])

        h2_all, lg_all = _norm_router(x1, g_ffn[l], sc2, sh2, w_router[l].T,
                                      _pad_rows(h2s.astype(bf16), TAIL_ROWS),
                                      _pad_rows(lg_s, TAIL_ROWS).T, t, ROW_TILE)
        eidx, wts, rank, cnt = _route(lg_all, b_router[l], ROW_TILE)
        cnt_tile = cnt[:, :, 0]
        total = jnp.sum(cnt_tile, axis=0)
        padded = jnp.ceil(total / EXPERT_BLOCK) * EXPERT_BLOCK
        pends = jnp.cumsum(padded)
        pstart = pends - padded
        base = pstart[None, :] + jnp.cumsum(cnt_tile, axis=0) - cnt_tile
        dest = _dest(eidx, rank, jnp.broadcast_to(base[:, :, None], base.shape + (LANES,)), ROW_TILE)
        n_blk = -(-(n_pad * TOP_K) // EXPERT_BLOCK) + N_EXPERTS
        blk_start = (jnp.arange(n_blk) * EXPERT_BLOCK).astype(f32)
        blk_e = jnp.minimum(jnp.sum(pends[None, :] <= blk_start[:, None], axis=1), N_EXPERTS - 1).astype(jnp.int32)
        n_used = (pends[-1] / EXPERT_BLOCK).astype(jnp.int32).reshape(1)
        n_rows = n_blk * EXPERT_BLOCK
        pad_start = jnp.concatenate([pstart + total, pends[-1:]]).astype(jnp.int32)
        pad_cnt = jnp.concatenate([padded - total, (n_rows - pends[-1:]) / 8]).astype(jnp.int32)
        xs_sorted = _dispatch(h2_all, _tile_major(dest, ROW_TILE), pad_start, pad_cnt, n_rows, ROW_TILE)
        owns = padded > 0
        e_ids = jnp.arange(N_EXPERTS, dtype=jnp.int32)
        later = jnp.where(owns[None, :] & (e_ids[None, :] > e_ids[:, None]), e_ids[None, :], N_EXPERTS)
        next_owner = jnp.min(later, axis=1)
        next_owner = jnp.where(next_owner < N_EXPERTS, next_owner, -1).astype(jnp.int32)
        slot_of_e = ((jnp.cumsum(owns.astype(jnp.int32)) - 1) % 2).astype(jnp.int32)
        ys = _experts(xs_sorted, blk_e, n_used, next_owner, slot_of_e,
                      w_gate[l], w_up[l], w_down[l], EXPERT_BLOCK)
        shared = _ffn_shared(h2_all, w_sh_gate[l], w_sh_up[l], w_sh_down[l], ROW_TILE)
        dest_c = _tile_major(dest, COMBINE_TILE)
        wts_tok = wts.T
        last = l == depth - 1
        xp = _combine(dest_c, ys, wts_tok, shared, x1, gt2[:, None, :], g_final.astype(f32),
                      0, COMBINE_TILE, last)
        tail = _combine(dest_c, ys, wts_tok, shared, _pad_rows(x1s, TAIL_ROWS),
                        _pad_rows(gt2s, TAIL_ROWS).reshape(-1, COMBINE_TILE, d), g_final.astype(f32),
                        n // COMBINE_TILE, COMBINE_TILE, last)
        xs = tail[:nbs]
        leaves_p.append((k_da, v_da, k_sa, v_sa, k_ix))
        leaves_s.append((k_da_s, v_da_s, k_sa_s, v_sa_s, k_ix_s))

    shapes = [(DA_HEADS, 2, DA_DK), (DA_HEADS, DA_DV), (SA_KV_HEADS, SA_DH), (SA_KV_HEADS, SA_DH), (IDX_DK,)]
    out_p = [jnp.stack([lv[i].reshape((nb, t) + shapes[i]) for lv in leaves_p]) for i in range(5)]
    out_s = [jnp.stack([lv[i].reshape((nbs, dec_seq) + shapes[i]) for lv in leaves_s]) for i in range(5)]
    return (xp.reshape(nb, t, d), xs.reshape(nbs, dec_seq, d), *out_p, *out_s)
```

```python
import functools
import math

import jax
import jax.numpy as jnp
from jax import lax
from jax.experimental import pallas as pl
from jax.experimental.pallas import tpu as pltpu

DA_HEADS = 8
DA_DK = 64
DA_DV = 2 * DA_DK
SA_HEADS = 8
SA_KV_HEADS = 2
SA_DH = 128
SA_GROUP = SA_HEADS // SA_KV_HEADS
IDX_HEADS = 16
IDX_DK = 64
SA_TOPK_MAX = 256
N_BUCKETS = 32
MAX_DISTANCE = 128
N_EXPERTS = 64
N_GROUPS = 8
GROUP_SIZE = N_EXPERTS // N_GROUPS
TOPK_GROUPS = 4
TOP_K = 8
ROUTED_SCALE = 2.5
PAGE_SIZE = 128
EPS = 1e-6

LANES = 128
VMEM_LIMIT = 56 * 1024 * 1024

BF16 = jnp.bfloat16
NEG = -1e30
INT_MIN = -(2 ** 31)
KEY_NEG_INF = (0xFF800000 ^ 0x7FFFFFFF) - (1 << 32)

_NT = (((1,), (1,)), ((), ()))


def _cparams(sem):
    return pltpu.CompilerParams(dimension_semantics=sem, vmem_limit_bytes=VMEM_LIMIT)


def _dot(a, b):
    return jnp.dot(a, b, preferred_element_type=jnp.float32)


def _dot_nt(a, b):
    return lax.dot_general(a, b, _NT, preferred_element_type=jnp.float32)


def _bf(x):
    return x.astype(BF16)


def _rounded(x):
    return x.astype(BF16).astype(jnp.float32)


def _sigmoid(x):
    return 1.0 / (1.0 + jnp.exp(-x))


def _silu(x):
    return x * _sigmoid(x)


def _float_key(s):
    b = pltpu.bitcast(s, jnp.int32)
    return b ^ ((b >> 31) & jnp.int32(0x7FFFFFFF))


def _linear_small_kernel(x_ref, w_ref, b_ref, o_ref, *, silu_in):
    x = x_ref[...]
    if silu_in:
        x = _silu(x)
    o_ref[...] = _dot(_bf(x), _bf(w_ref[...])) + b_ref[...]


def _linear_small(x, w, b=None, *, silu_in=False, tn=512):
    m0, k = x.shape
    m = -(-m0 // 16) * 16
    x = jnp.pad(x, ((0, m - m0), (0, 0)))
    n = w.shape[1]
    tn = min(tn, n)
    if b is None:
        b = jnp.zeros((1, n), jnp.float32)
    out = pl.pallas_call(
        functools.partial(_linear_small_kernel, silu_in=silu_in),
        grid=(pl.cdiv(n, tn),),
        in_specs=[pl.BlockSpec((m, k), lambda j: (0, 0)),
                  pl.BlockSpec((k, tn), lambda j: (0, j)),
                  pl.BlockSpec((1, tn), lambda j: (0, j))],
        out_specs=pl.BlockSpec((m, tn), lambda j: (0, j)),
        out_shape=jax.ShapeDtypeStruct((m, n), jnp.float32),
        compiler_params=_cparams(("arbitrary",)),
        name="linear_small",
    )(x, w, b.reshape(1, n))
    return out[:m0]


def _norm_mod_kernel(x_ref, g_ref, sc_ref, sh_ref, o_ref):
    x = x_ref[...]
    y = x * lax.rsqrt(jnp.mean(x * x, axis=-1, keepdims=True) + EPS) * g_ref[...]
    o_ref[...] = (y * (1.0 + sc_ref[0]) + sh_ref[0]).astype(o_ref.dtype)


def _norm_mod(x, g, sc, sh, rows_per_batch, tm):
    n, d = x.shape
    per = rows_per_batch // tm
    return pl.pallas_call(
        _norm_mod_kernel,
        grid=(n // tm,),
        in_specs=[pl.BlockSpec((tm, d), lambda i: (i, 0)),
                  pl.BlockSpec((1, d), lambda i: (0, 0)),
                  pl.BlockSpec((1, 1, d), lambda i: (i // per, 0, 0)),
                  pl.BlockSpec((1, 1, d), lambda i: (i // per, 0, 0))],
        out_specs=pl.BlockSpec((tm, d), lambda i: (i, 0)),
        out_shape=jax.ShapeDtypeStruct((n, d), BF16),
        compiler_params=_cparams(("arbitrary",)),
        name="norm_mod",
    )(x, g.reshape(1, d), sc[:, None, :], sh[:, None, :])


def _mm_kernel(x_ref, w_ref, *rest, scale, sigmoid, n_out, transposed_out):
    o_refs, wbf_ref = rest[:n_out], rest[-1]

    @pl.when(pl.program_id(1) == 0)
    def _():
        wbf_ref[...] = w_ref[...].astype(BF16)

    acc = _dot(x_ref[...], wbf_ref[...])
    if scale != 1.0:
        acc = acc * scale
    if sigmoid:
        acc = _sigmoid(acc)
    for o in o_refs:
        o[...] = acc.astype(o.dtype)
    if transposed_out:
        rest[n_out][0] = acc.T


def _mm(x, w, col0, ncols, out_dtypes, *, scale=1.0, sigmoid=False, tm=1024, tn=1024, rows_per_batch=None):
    m, k = x.shape
    tn = min(tn, ncols)
    tm = min(tm, m)
    assert col0 % tn == 0 and ncols % tn == 0 and m % tm == 0
    jb = col0 // tn
    out_specs = [pl.BlockSpec((tm, tn), lambda j, i: (i, j)) for _ in out_dtypes]
    out_shape = [jax.ShapeDtypeStruct((m, ncols), dt) for dt in out_dtypes]
    if rows_per_batch is not None:
        per = rows_per_batch // tm
        out_specs.append(pl.BlockSpec((1, tn, tm), lambda j, i: (i // per, j, i % per)))
        out_shape.append(jax.ShapeDtypeStruct((m // rows_per_batch, ncols, rows_per_batch), jnp.float32))
    outs = pl.pallas_call(
        functools.partial(_mm_kernel, scale=scale, sigmoid=sigmoid, n_out=len(out_dtypes),
                          transposed_out=rows_per_batch is not None),
        grid=(ncols // tn, m // tm),
        in_specs=[pl.BlockSpec((tm, k), lambda j, i: (i, 0)),
                  pl.BlockSpec((k, tn), lambda j, i: (0, jb + j))],
        out_specs=out_specs,
        out_shape=out_shape,
        scratch_shapes=[pltpu.VMEM((k, tn), BF16)],
        compiler_params=_cparams(("arbitrary", "arbitrary")),
        name="mm_cols",
    )(x, w)
    return outs


def _rel_bucket(dist):
    max_exact = N_BUCKETS // 2
    d = jnp.maximum(dist, 0)
    large = max_exact + (jnp.log(jnp.maximum(d, 1).astype(jnp.float32) / max_exact)
                         / math.log(MAX_DISTANCE / max_exact)
                         * (N_BUCKETS - max_exact)).astype(jnp.int32)
    large = jnp.minimum(large, N_BUCKETS - 1)
    return jnp.where(d < max_exact, d, large)


def _bias_by_distance(table, dists):
    return table[_rel_bucket(dists)].astype(jnp.float32).T


def _toeplitz_kernel(u_ref, o_ref):
    t = o_ref.shape[2]
    x = jnp.broadcast_to(u_ref[0], (t, 2 * t))
    o_ref[0, 0] = pltpu.roll(x, 0, 1, stride=1, stride_axis=0)[:, :t]


def _near_tiles(table, t):
    nh = table.shape[1]
    k = jnp.arange(2 * t, dtype=jnp.int32)
    gens = []
    for off in (0, t):
        d = jnp.where(k < t, off - k, off + 2 * t - k)
        gens.append(jnp.where(d[None] >= 0, _bias_by_distance(table, d), NEG))
    u = jnp.stack(gens, axis=1).reshape(nh * 2, 1, 2 * t)
    return pl.pallas_call(
        _toeplitz_kernel,
        grid=(nh, 2),
        in_specs=[pl.BlockSpec((1, 1, 2 * t), lambda h, o: (h * 2 + o, 0, 0))],
        out_specs=pl.BlockSpec((1, 1, t, t), lambda h, o: (h, o, 0, 0)),
        out_shape=jax.ShapeDtypeStruct((nh, 2, t, t), jnp.float32),
        compiler_params=_cparams(("arbitrary", "arbitrary")),
        name="bias_tiles",
    )(u)


def _fold_lanes(x, op):
    out = x[:, :LANES]
    for c in range(1, x.shape[1] // LANES):
        out = op(out, x[:, c * LANES:(c + 1) * LANES])
    return out


def _chunk_loop(n, fn):
    def body(i, carry):
        for u in range(4):
            fn(4 * i + u)
        return carry

    lax.fori_loop(0, n // 4, body, 0)
    base = (n // 4) * 4

    @pl.when(n % 4 >= 2)
    def _():
        fn(base)
        fn(base + 1)

    @pl.when(n % 2 == 1)
    def _():
        fn(n - 1)


def _da_prompt_kernel(cfar_ref, lam_ref, q_ref, k_ref, v_ref, tile_ref, g_ref, o_ref,
                      s_ref, mpart_ref, shift_ref, lpart_ref, acc_ref, *, tq, out_scale):
    h = pl.program_id(1)
    qi = pl.program_id(2)
    q = q_ref[...]
    lane = lax.broadcasted_iota(jnp.int32, q.shape, 1)
    zero = jnp.zeros_like(q)
    q2 = jnp.concatenate([jnp.where(lane < DA_DK, q, zero), jnp.where(lane >= DA_DK, q, zero)], axis=0)
    cfar = cfar_ref[h]
    n_far = jnp.maximum(qi - 1, 0)
    r2 = 2 * tq

    def chunk_rows(kc):
        return pl.ds(pl.multiple_of(kc * tq, tq), tq)

    def scores(kc, bias):
        s = _dot_nt(q2, k_ref[chunk_rows(kc), :])
        if bias is not None:
            s = s + jnp.concatenate([bias, bias], axis=0)
        s_ref[kc] = s
        mpart_ref[...] = jnp.maximum(mpart_ref[...], _fold_lanes(s, jnp.maximum))

    mpart_ref[...] = jnp.full((r2, LANES), NEG, jnp.float32)
    _chunk_loop(n_far, lambda kc: scores(kc, None))
    m_far = jnp.max(mpart_ref[...], axis=-1, keepdims=True) + cfar
    mpart_ref[...] = jnp.full((r2, LANES), NEG, jnp.float32)

    @pl.when(qi >= 1)
    def _():
        scores(qi - 1, tile_ref[0, 1])
        scores(qi, tile_ref[0, 0])

    @pl.when(qi == 0)
    def _():
        scores(0, tile_ref[0, 0])

    m = jnp.maximum(m_far, jnp.max(mpart_ref[...], axis=-1, keepdims=True))
    shift_ref[0] = jnp.broadcast_to(m - cfar, (r2, LANES))
    shift_ref[1] = jnp.broadcast_to(m, (r2, LANES))

    lpart_ref[...] = jnp.zeros((r2, LANES), jnp.float32)
    acc_ref[...] = jnp.zeros((r2, DA_DV), jnp.float32)

    def weights(kc, which):
        s = s_ref[kc]
        sh = shift_ref[which]
        ps = [jnp.exp(s[:, c * LANES:(c + 1) * LANES] - sh) for c in range(tq // LANES)]
        tot = ps[0]
        for pc in ps[1:]:
            tot = tot + pc
        lpart_ref[...] = lpart_ref[...] + tot
        p = jnp.concatenate(ps, axis=1).astype(BF16)
        acc_ref[...] = acc_ref[...] + _dot(p, v_ref[chunk_rows(kc), :])

    _chunk_loop(n_far, lambda kc: weights(kc, 0))

    @pl.when(qi >= 1)
    def _():
        weights(qi - 1, 1)
        weights(qi, 1)

    @pl.when(qi == 0)
    def _():
        weights(0, 1)

    lam = lam_ref[0]
    a = acc_ref[...] / jnp.sum(lpart_ref[...], axis=-1, keepdims=True)
    o = a[:tq] - lam * a[tq:]
    o = o * lax.rsqrt(jnp.mean(o * o, axis=-1, keepdims=True) + EPS) * g_ref[...]
    o_ref[...] = (o * out_scale).astype(o_ref.dtype)


def _da_prompt(q, k, v, tiles, cfar, lam, g_subln, nb, t, tq, out_scale):
    n = q.shape[0]
    nq = t // tq
    grid_spec = pltpu.PrefetchScalarGridSpec(
        num_scalar_prefetch=0,
        grid=(nb, DA_HEADS, nq),
        in_specs=[pl.BlockSpec(memory_space=pltpu.SMEM),
                  pl.BlockSpec(memory_space=pltpu.SMEM),
                  pl.BlockSpec((tq, LANES), lambda b, h, i: (b * nq + i, h)),
                  pl.BlockSpec((t, LANES), lambda b, h, i: (b, h)),
                  pl.BlockSpec((t, LANES), lambda b, h, i: (b, h)),
                  pl.BlockSpec((1, 2, tq, tq), lambda b, h, i: (h, 0, 0, 0)),
                  pl.BlockSpec((1, DA_DV), lambda b, h, i: (0, 0))],
        out_specs=pl.BlockSpec((tq, LANES), lambda b, h, i: (b * nq + i, h)),
        scratch_shapes=[pltpu.VMEM((nq, 2 * tq, tq), jnp.float32),
                        pltpu.VMEM((2 * tq, LANES), jnp.float32),
                        pltpu.VMEM((2, 2 * tq, LANES), jnp.float32),
                        pltpu.VMEM((2 * tq, LANES), jnp.float32),
                        pltpu.VMEM((2 * tq, DA_DV), jnp.float32)])
    return pl.pallas_call(
        functools.partial(_da_prompt_kernel, tq=tq, out_scale=out_scale),
        grid_spec=grid_spec,
        out_shape=jax.ShapeDtypeStruct((n, DA_HEADS * DA_DV), BF16),
        compiler_params=_cparams(("arbitrary", "arbitrary", "arbitrary")),
        name="da_prompt",
    )(cfar, lam, q, k, v, tiles, g_subln.reshape(1, DA_DV))


def _sa_prompt_kernel(cfar_ref, qs_ref, qx_ref, mq_ref, mk_ref, ks_ref, vs_ref, tile_ref, o_ref,
                      k2_ref, key_ref, hi_ref, lo_ref, cut_ref, s_ref, mpart_ref, shift_ref, lpart_ref,
                      acc_ref, *, tq, topk):
    qi = pl.program_id(1)
    n_chunks = qi + 1
    t = mk_ref.shape[0]

    @pl.when(qi == 0)
    def _():
        kix = mk_ref[:, :LANES].astype(jnp.float32)
        lane = lax.broadcasted_iota(jnp.int32, kix.shape, 1)
        k2_ref[0] = jnp.where(lane < IDX_DK, kix, 0.0).astype(BF16)
        k2_ref[1] = jnp.where(lane >= IDX_DK, pltpu.roll(kix, IDX_DK, axis=1), 0.0).astype(BF16)

    wix = mq_ref[:, IDX_DK:IDX_DK + IDX_HEADS]
    wcols = [wix[:, hh:hh + 1] for hh in range(IDX_HEADS)]
    row = lax.broadcasted_iota(jnp.int32, (tq, tq), 0)
    col = lax.broadcasted_iota(jnp.int32, (tq, tq), 1)

    def score_body(kc, carry):
        rows = pl.ds(pl.multiple_of(kc * tq, tq), tq)
        ke = k2_ref[0, rows, :]
        ko = k2_ref[1, rows, :]
        sc = jnp.zeros((tq, tq), jnp.float32)
        for p in range(IDX_HEADS // 2):
            qp = qx_ref[:, p * LANES:(p + 1) * LANES]
            sc = sc + wcols[2 * p] * jnp.maximum(_dot_nt(qp, ke), 0.0)
            sc = sc + wcols[2 * p + 1] * jnp.maximum(_dot_nt(qp, ko), 0.0)
        sc = sc * (IDX_DK ** -0.5 * IDX_HEADS ** -0.5)
        sc = jnp.where((kc < qi) | (row >= col), sc, -jnp.inf)
        key = _float_key(sc)
        key_ref[kc] = key
        hi_ref[kc] = (key >> 16).astype(jnp.int16)
        return carry

    lax.fori_loop(0, n_chunks, score_body, 0)

    i16 = jnp.int16
    i16_min = -(2 ** 15)
    one_i = jnp.ones((tq, tq), i16)
    zero_i = jnp.zeros((tq, tq), i16)
    ones_col = jnp.ones((tq, LANES), BF16)

    def wide16(x):
        return jnp.concatenate([x] * (tq // LANES), axis=1).astype(i16)

    def count_ge16(ref16, cand):
        c16 = wide16(cand)

        def body(kc, acc):
            return acc + jnp.where(ref16[kc] >= c16, one_i, zero_i)
        acc = lax.fori_loop(0, n_chunks, body, zero_i)
        return _dot(acc.astype(jnp.float32).astype(BF16), ones_col)

    def search16(ref16, base, need):
        def bit_body(it, carry):
            v, cnt_v = carry
            cand = v + (jnp.int32(1) << (15 - it))
            cnt = base + count_ge16(ref16, cand)
            ok = cnt >= need
            return jnp.where(ok, cand, v), jnp.where(ok, cnt, cnt_v)
        v0 = jnp.full((tq, LANES), i16_min, jnp.int32)
        c0 = jnp.full((tq, LANES), 3.0e38, jnp.float32)
        return lax.fori_loop(0, 16, bit_body, (v0, c0))

    zero_cnt = jnp.zeros((tq, LANES), jnp.float32)
    t_hi, _ = search16(hi_ref, zero_cnt, float(topk))
    n_above = jnp.where(t_hi < 2 ** 15 - 1, count_ge16(hi_ref, jnp.minimum(t_hi + 1, 2 ** 15 - 1)), 0.0)
    t_hi16 = wide16(t_hi)

    def lower_body(kc, carry):
        lo = ((key_ref[kc] & jnp.int32(0xFFFF)) - 2 ** 15).astype(i16)
        lo_ref[kc] = jnp.where(hi_ref[kc] == t_hi16, lo, jnp.full((tq, tq), i16_min, i16))
        return carry

    lax.fori_loop(0, n_chunks, lower_body, 0)
    t_lo, cnt_w = search16(lo_ref, n_above, float(topk))
    thr = ((t_hi << 16) + (t_lo + 2 ** 15))[:, :1]
    cnt_thr = cnt_w[:, :1]
    tied = (cnt_thr > float(topk)) & (thr > jnp.int32(KEY_NEG_INF))
    need_tie = jnp.max(jnp.where(tied, 1.0, 0.0)) > 0.0
    thr = jnp.maximum(thr, jnp.int32(KEY_NEG_INF + 1))

    cut_ref[...] = jnp.full((tq, 1), 2 ** 30, jnp.int32)

    @pl.when(need_tie)
    def _():
        def gt_body(kc, acc):
            g = jnp.where(key_ref[kc] > thr, 1.0, 0.0)
            return acc + jnp.sum(g, axis=-1, keepdims=True)
        n_gt = lax.fori_loop(0, n_chunks, gt_body, jnp.zeros((tq, 1), jnp.float32))
        need = float(topk) - n_gt
        n_bits = max(1, int(t).bit_length())

        def cut_body(it, cut):
            cand = cut + (jnp.int32(1) << (n_bits - 1 - it))

            def eq_body(kc, acc):
                pos = kc * tq + col
                e = jnp.where((key_ref[kc] == thr) & (pos < cand), 1.0, 0.0)
                return acc + jnp.sum(e, axis=-1, keepdims=True)
            n_eq = lax.fori_loop(0, n_chunks, eq_body, jnp.zeros((tq, 1), jnp.float32))
            return jnp.where(n_eq <= need, cand, cut)
        cut_ref[...] = lax.fori_loop(0, n_bits, cut_body, jnp.zeros((tq, 1), jnp.int32))

    cut = cut_ref[...]

    n_far = jnp.maximum(qi - 1, 0)
    rg = SA_GROUP * tq

    def chunk_rows(kc):
        return pl.ds(pl.multiple_of(kc * tq, tq), tq)

    for g in range(SA_KV_HEADS):
        heads = [g * SA_GROUP + j for j in range(SA_GROUP)]
        qg = jnp.concatenate([qs_ref[:, hh * LANES:(hh + 1) * LANES] for hh in heads], axis=0)
        cf_rows = jnp.concatenate([jnp.full((tq, 1), cfar_ref[hh], jnp.float32) for hh in heads], axis=0)

        def scores(kc, kind, qg=qg, heads=heads, g=g):
            key = key_ref[kc]
            sel = (key > thr) | ((key == thr) & (kc * tq + col < cut))
            s_all = _dot_nt(qg, ks_ref[chunk_rows(kc), g * SA_DH:(g + 1) * SA_DH])
            parts = []
            for j, hh in enumerate(heads):
                s = s_all[j * tq:(j + 1) * tq]
                if kind is not None:
                    s = s + tile_ref[hh, kind]
                parts.append(jnp.where(sel, s, NEG))
            s = jnp.concatenate(parts, axis=0)
            s_ref[kc] = s
            mpart_ref[...] = jnp.maximum(mpart_ref[...], _fold_lanes(s, jnp.maximum))

        mpart_ref[...] = jnp.full((rg, LANES), NEG, jnp.float32)
        _chunk_loop(n_far, lambda kc, f=scores: f(kc, None))
        m_far = jnp.max(mpart_ref[...], axis=-1, keepdims=True) + cf_rows
        mpart_ref[...] = jnp.full((rg, LANES), NEG, jnp.float32)

        @pl.when(qi >= 1)
        def _(f=scores):
            f(qi - 1, 1)
            f(qi, 0)

        @pl.when(qi == 0)
        def _(f=scores):
            f(0, 0)

        m = jnp.maximum(m_far, jnp.max(mpart_ref[...], axis=-1, keepdims=True))
        shift_ref[0] = jnp.broadcast_to(m - cf_rows, (rg, LANES))
        shift_ref[1] = jnp.broadcast_to(m, (rg, LANES))
        lpart_ref[...] = jnp.zeros((rg, LANES), jnp.float32)
        acc_ref[...] = jnp.zeros((rg, SA_DH), jnp.float32)

        def weights(kc, which, g=g):
            s = s_ref[kc]
            sh = shift_ref[which]
            ps = [jnp.exp(s[:, c * LANES:(c + 1) * LANES] - sh) for c in range(tq // LANES)]
            tot = ps[0]
            for pc in ps[1:]:
                tot = tot + pc
            lpart_ref[...] = lpart_ref[...] + tot
            p = jnp.concatenate(ps, axis=1).astype(BF16)
            acc_ref[...] = acc_ref[...] + _dot(p, vs_ref[chunk_rows(kc), g * SA_DH:(g + 1) * SA_DH])

        _chunk_loop(n_far, lambda kc, f=weights: f(kc, 0))

        @pl.when(qi >= 1)
        def _(f=weights):
            f(qi - 1, 1)
            f(qi, 1)

        @pl.when(qi == 0)
        def _(f=weights):
            f(0, 1)

        a = acc_ref[...] / jnp.sum(lpart_ref[...], axis=-1, keepdims=True)
        for j, hh in enumerate(heads):
            o_ref[:, hh * SA_DH:(hh + 1) * SA_DH] = a[j * tq:(j + 1) * tq].astype(o_ref.dtype)


def _sa_prompt(q_sa, q_ix, misc_q, misc_k, k_sa, v_sa, tiles, cfar, nb, t, tq, topk):
    n = q_sa.shape[0]
    nq = t // tq
    mw = misc_q.shape[1]
    kvw = SA_KV_HEADS * SA_DH
    once = dict(pipeline_mode=pl.Buffered(1))
    rg = SA_GROUP * tq
    return pl.pallas_call(
        functools.partial(_sa_prompt_kernel, tq=tq, topk=topk),
        grid=(nb, nq),
        in_specs=[pl.BlockSpec(memory_space=pltpu.SMEM),
                  pl.BlockSpec((tq, SA_HEADS * SA_DH), lambda b, i: (b * nq + i, 0)),
                  pl.BlockSpec((tq, IDX_HEADS * IDX_DK), lambda b, i: (b * nq + i, 0)),
                  pl.BlockSpec((tq, mw), lambda b, i: (b * nq + i, 0)),
                  pl.BlockSpec((t, mw), lambda b, i: (b, 0), **once),
                  pl.BlockSpec((t, kvw), lambda b, i: (b, 0), **once),
                  pl.BlockSpec((t, kvw), lambda b, i: (b, 0), **once),
                  pl.BlockSpec((SA_HEADS, 2, tq, tq), lambda b, i: (0, 0, 0, 0), **once)],
        out_specs=pl.BlockSpec((tq, SA_HEADS * SA_DH), lambda b, i: (b * nq + i, 0)),
        out_shape=jax.ShapeDtypeStruct((n, SA_HEADS * SA_DH), BF16),
        scratch_shapes=[pltpu.VMEM((2, t, LANES), BF16),
                        pltpu.VMEM((nq, tq, tq), jnp.int32),
                        pltpu.VMEM((nq, tq, tq), jnp.int16),
                        pltpu.VMEM((nq, tq, tq), jnp.int16),
                        pltpu.VMEM((tq, 1), jnp.int32),
                        pltpu.VMEM((nq, rg, tq), jnp.float32),
                        pltpu.VMEM((rg, LANES), jnp.float32),
                        pltpu.VMEM((2, rg, LANES), jnp.float32),
                        pltpu.VMEM((rg, LANES), jnp.float32),
                        pltpu.VMEM((rg, SA_DH), jnp.float32)],
        compiler_params=_cparams(("arbitrary", "arbitrary")),
        name="sa_prompt",
    )(cfar, q_sa, q_ix, misc_q, misc_k, k_sa, v_sa, tiles)


def _proj_gate_kernel(oda_ref, osa_ref, wpd_ref, wps_ref, gda_ref, gsa_ref, o_ref, wbf_ref):
    @pl.when(pl.program_id(1) == 0)
    def _():
        wbf_ref[0] = wpd_ref[...].astype(BF16)
        wbf_ref[1] = wps_ref[...].astype(BF16)

    a = _dot(oda_ref[...], wbf_ref[0])
    b = _dot(osa_ref[...], wbf_ref[1])
    o_ref[...] = (gda_ref[...].astype(jnp.float32) * a
                  + gsa_ref[...].astype(jnp.float32) * b).astype(o_ref.dtype)


def _proj_gate(o_da, o_sa, w_pd, w_ps, gates, d, tm=1024, tn=512):
    n, kd = o_da.shape
    ks = o_sa.shape[1]
    tn = min(tn, d)
    tm = min(tm, n)
    nj = d // tn
    return pl.pallas_call(
        _proj_gate_kernel,
        grid=(nj, n // tm),
        in_specs=[pl.BlockSpec((tm, kd), lambda j, i: (i, 0)),
                  pl.BlockSpec((tm, ks), lambda j, i: (i, 0)),
                  pl.BlockSpec((kd, tn), lambda j, i: (0, j)),
                  pl.BlockSpec((ks, tn), lambda j, i: (0, j)),
                  pl.BlockSpec((tm, tn), lambda j, i: (i, j)),
                  pl.BlockSpec((tm, tn), lambda j, i: (i, nj + j))],
        out_specs=pl.BlockSpec((tm, tn), lambda j, i: (i, j)),
        out_shape=jax.ShapeDtypeStruct((n, d), BF16),
        scratch_shapes=[pltpu.VMEM((2, kd, tn), BF16)],
        compiler_params=_cparams(("arbitrary", "arbitrary")),
        name="proj_gate",
    )(o_da, o_sa, w_pd, w_ps, gates, gates)


def _mm_resid_kernel(m_ref, w_ref, x_ref, gt_ref, o_ref, wbf_ref):
    @pl.when(pl.program_id(1) == 0)
    def _():
        wbf_ref[...] = w_ref[...].astype(BF16)

    o_ref[...] = x_ref[...] + gt_ref[0] * _dot(m_ref[...], wbf_ref[...])


def _mm_resid(m, w, x, gt, rows_per_batch, tm=1024, tn=1024):
    n, k = m.shape
    d = w.shape[1]
    tn = min(tn, d)
    tm = min(tm, n)
    per = rows_per_batch // tm
    return pl.pallas_call(
        _mm_resid_kernel,
        grid=(d // tn, n // tm),
        in_specs=[pl.BlockSpec((tm, k), lambda j, i: (i, 0)),
                  pl.BlockSpec((k, tn), lambda j, i: (0, j)),
                  pl.BlockSpec((tm, tn), lambda j, i: (i, j)),
                  pl.BlockSpec((1, 1, tn), lambda j, i: (i // per, 0, j))],
        out_specs=pl.BlockSpec((tm, tn), lambda j, i: (i, j)),
        out_shape=jax.ShapeDtypeStruct((n, d), jnp.float32),
        scratch_shapes=[pltpu.VMEM((k, tn), BF16)],
        compiler_params=_cparams(("arbitrary", "arbitrary")),
        name="mm_resid",
    )(m, w, x, gt[:, None, :])


def _norm_router_kernel(x_ref, g_ref, sc_ref, sh_ref, wr_ref, th_ref, tl_ref, h_ref, lg_ref):
    last = pl.num_programs(0) - 1

    @pl.when(pl.program_id(0) < last)
    def _():
        x = x_ref[...]
        y = x * lax.rsqrt(jnp.mean(x * x, axis=-1, keepdims=True) + EPS) * g_ref[...]
        h = y * (1.0 + sc_ref[0]) + sh_ref[0]
        h_ref[...] = h.astype(h_ref.dtype)
        lg_ref[...] = _dot_nt(_bf(wr_ref[...]), _bf(h))

    @pl.when(pl.program_id(0) == last)
    def _():
        h_ref[...] = th_ref[...]
        lg_ref[...] = tl_ref[...]


def _norm_router(x, g, sc, sh, w_router_t, tail_h, tail_lg, rows_per_batch, tm):
    n, d = x.shape
    per = rows_per_batch // tm
    nt = n // tm
    assert tail_h.shape == (tm, d) and tail_lg.shape == (N_EXPERTS, tm)
    row = lambda i: jnp.minimum(i, nt - 1)
    return pl.pallas_call(
        _norm_router_kernel,
        grid=(nt + 1,),
        in_specs=[pl.BlockSpec((tm, d), lambda i: (row(i), 0)),
                  pl.BlockSpec((1, d), lambda i: (0, 0)),
                  pl.BlockSpec((1, 1, d), lambda i: (row(i) // per, 0, 0)),
                  pl.BlockSpec((1, 1, d), lambda i: (row(i) // per, 0, 0)),
                  pl.BlockSpec((N_EXPERTS, d), lambda i: (0, 0)),
                  pl.BlockSpec((tm, d), lambda i: (0, 0)),
                  pl.BlockSpec((N_EXPERTS, tm), lambda i: (0, 0))],
        out_specs=[pl.BlockSpec((tm, d), lambda i: (i, 0)),
                   pl.BlockSpec((N_EXPERTS, tm), lambda i: (0, i))],
        out_shape=[jax.ShapeDtypeStruct((n + tm, d), BF16),
                   jax.ShapeDtypeStruct((N_EXPERTS, n + tm), jnp.float32)],
        compiler_params=_cparams(("arbitrary",)),
        name="norm_router",
    )(x, g.reshape(1, d), sc[:, None, :], sh[:, None, :], w_router_t, tail_h, tail_lg)


def _route_kernel(lg_ref, b_ref, eidx_ref, wts_ref, rank_ref, cnt_ref, *, tn):
    shape = (N_GROUPS, GROUP_SIZE, tn)
    sc = _sigmoid(lg_ref[...])
    biased = sc + b_ref[...]
    e_iota = lax.broadcasted_iota(jnp.int32, shape, 1)
    g_iota3 = lax.broadcasted_iota(jnp.int32, shape, 0)
    flat_iota = g_iota3 * GROUP_SIZE + e_iota
    g_iota = lax.broadcasted_iota(jnp.int32, (N_GROUPS, 1, tn), 0)
    ninf = -jnp.inf

    m1 = jnp.max(biased, axis=1, keepdims=True)
    first = jnp.min(jnp.where(biased == m1, e_iota, GROUP_SIZE), axis=1, keepdims=True)
    m2 = jnp.max(jnp.where(e_iota == first, ninf, biased), axis=1, keepdims=True)
    cur = m1 + m2
    gsel = jnp.zeros((N_GROUPS, 1, tn), jnp.float32)
    for _ in range(TOPK_GROUPS):
        mx = jnp.max(cur, axis=0, keepdims=True)
        idx = jnp.min(jnp.where(cur == mx, g_iota, N_GROUPS), axis=0, keepdims=True)
        hit = g_iota == idx
        gsel = jnp.where(hit, 1.0, gsel)
        cur = jnp.where(hit, ninf, cur)

    cur = jnp.where(gsel > 0.0, biased, ninf)
    hits, ws = [], []
    for k in range(TOP_K):
        mx = jnp.max(jnp.max(cur, axis=1, keepdims=True), axis=0, keepdims=True)
        cand = jnp.where(cur == mx, flat_iota, N_EXPERTS)
        idx = jnp.min(jnp.min(cand, axis=1, keepdims=True), axis=0, keepdims=True)
        hit = flat_iota == idx
        w = jnp.sum(jnp.sum(jnp.where(hit, sc, 0.0), axis=1, keepdims=True), axis=0, keepdims=True)
        eidx_ref[k:k + 1, :] = idx.reshape(1, tn)
        hits.append(hit)
        ws.append(w)
        cur = jnp.where(hit, ninf, cur)
    wsum = ws[0]
    for w in ws[1:]:
        wsum = wsum + w
    for k in range(TOP_K):
        wts_ref[k:k + 1, :] = (ws[k] / wsum * ROUTED_SCALE).reshape(1, tn)

    member = jnp.zeros(shape, jnp.float32)
    for hit in hits:
        member = jnp.where(hit, 1.0, member)
    member2 = member.reshape(N_EXPERTS, tn)
    r = lax.broadcasted_iota(jnp.int32, (tn, tn), 0)
    c = lax.broadcasted_iota(jnp.int32, (tn, tn), 1)
    upper = jnp.where(r < c, 1.0, 0.0).astype(BF16)
    prefix = _dot(member2.astype(BF16), upper).reshape(shape)
    for k in range(TOP_K):
        rk = jnp.sum(jnp.sum(jnp.where(hits[k], prefix, 0.0), axis=1, keepdims=True), axis=0, keepdims=True)
        rank_ref[k:k + 1, :] = rk.reshape(1, tn)
    cnt = jnp.sum(member2, axis=1, keepdims=True)
    cnt_ref[0] = jnp.broadcast_to(cnt, (N_EXPERTS, LANES))


def _route(logits_t, b_router, tn):
    n_pad = logits_t.shape[1]
    nt = n_pad // tn
    lg3 = logits_t.reshape(N_GROUPS, GROUP_SIZE, n_pad)
    b3 = b_router.astype(jnp.float32).reshape(N_GROUPS, GROUP_SIZE, 1)
    row = lambda dt: jax.ShapeDtypeStruct((TOP_K, n_pad), dt)
    return pl.pallas_call(
        functools.partial(_route_kernel, tn=tn),
        grid=(nt,),
        in_specs=[pl.BlockSpec((N_GROUPS, GROUP_SIZE, tn), lambda i: (0, 0, i)),
                  pl.BlockSpec((N_GROUPS, GROUP_SIZE, 1), lambda i: (0, 0, 0))],
        out_specs=[pl.BlockSpec((TOP_K, tn), lambda i: (0, i)),
                   pl.BlockSpec((TOP_K, tn), lambda i: (0, i)),
                   pl.BlockSpec((TOP_K, tn), lambda i: (0, i)),
                   pl.BlockSpec((1, N_EXPERTS, LANES), lambda i: (i, 0, 0))],
        out_shape=[row(jnp.int32), row(jnp.float32), row(jnp.float32),
                   jax.ShapeDtypeStruct((nt, N_EXPERTS, LANES), jnp.float32)],
        compiler_params=_cparams(("arbitrary",)),
        name="route",
    )(lg3, b3)


def _dest_kernel(eidx_ref, rank_ref, base_ref, o_ref, *, tn):
    e_iota = lax.broadcasted_iota(jnp.int32, (N_EXPERTS, tn), 0)
    base = base_ref[0][:, :1]
    for k in range(TOP_K):
        onehot = e_iota == eidx_ref[k:k + 1, :]
        b = jnp.sum(jnp.where(onehot, base, 0.0), axis=0, keepdims=True)
        o_ref[k:k + 1, :] = (b + rank_ref[k:k + 1, :]).astype(jnp.int32)


def _dest(eidx, rank, base, tn):
    n_pad = eidx.shape[1]
    return pl.pallas_call(
        functools.partial(_dest_kernel, tn=tn),
        grid=(n_pad // tn,),
        in_specs=[pl.BlockSpec((TOP_K, tn), lambda i: (0, i)),
                  pl.BlockSpec((TOP_K, tn), lambda i: (0, i)),
                  pl.BlockSpec((1, N_EXPERTS, LANES), lambda i: (i, 0, 0))],
        out_specs=pl.BlockSpec((TOP_K, tn), lambda i: (0, i)),
        out_shape=jax.ShapeDtypeStruct((TOP_K, n_pad), jnp.int32),
        compiler_params=_cparams(("arbitrary",)),
        name="dest",
    )(eidx, rank, base)


def _pack_words(lo_f32, hi_f32):
    lo = lax.shift_right_logical(pltpu.bitcast(lo_f32, jnp.uint32), jnp.uint32(16))
    hi = pltpu.bitcast(hi_f32, jnp.uint32) & jnp.uint32(0xFFFF0000)
    return hi | lo


def _unpack_words(w):
    lo = pltpu.bitcast(lax.shift_left(w, jnp.uint32(16)), jnp.float32)
    hi = pltpu.bitcast(w & jnp.uint32(0xFFFF0000), jnp.float32)
    return lo, hi


def _bf16_exact(x):
    return x.astype(BF16).astype(jnp.float32)


def _dispatch_kernel(pstart_ref, pcnt_ref, h_ref, dest_hbm, xs_hbm, dsm, pk, zrow, sem_d, sem_r,
                     *, tn, nt):
    i = pl.program_id(0)
    half = pk.shape[1]

    @pl.when(i < nt)
    def _():
        cp = pltpu.make_async_copy(dest_hbm.at[pl.ds(i * (TOP_K * tn), TOP_K * tn)], dsm, sem_d)
        cp.start()
        x = h_ref[...]
        pk[...] = _pack_words(x[:, :half].astype(jnp.float32), x[:, half:].astype(jnp.float32))
        cp.wait()

        def body(r, carry):
            for k in range(TOP_K):
                d = dsm[k * tn + r]
                pltpu.make_async_copy(pk.at[pl.ds(r, 1), :], xs_hbm.at[pl.ds(d, 1), :],
                                      sem_r).start(priority=k % 2)
            return carry

        lax.fori_loop(0, tn, body, 0)
        for k in range(TOP_K):
            pltpu.make_async_copy(pk, xs_hbm.at[pl.ds(0, tn), :], sem_r).wait()

    @pl.when(i == nt)
    def _():
        zrow[...] = jnp.zeros(zrow.shape, zrow.dtype)

        def per_expert(e, carry):
            s0 = pstart_ref[e]
            c = pcnt_ref[e]
            head = jnp.minimum(c, (8 - s0 % 8) % 8)
            g0 = s0 + head
            ngrp = (c - head) // 8
            grp = lambda r: pl.ds(pl.multiple_of(g0 + r * 8, 8), 8)

            def start(r, cc):
                pltpu.make_async_copy(zrow.at[pl.ds(0, 1), :], xs_hbm.at[pl.ds(s0 + r, 1), :], sem_r).start()
                return cc

            def wait(r, cc):
                pltpu.make_async_copy(zrow.at[pl.ds(0, 1), :], xs_hbm.at[pl.ds(s0, 1), :], sem_r).wait()
                return cc

            def gstart(r, cc):
                pltpu.make_async_copy(zrow, xs_hbm.at[grp(r), :], sem_r).start()
                return cc

            def gwait(r, cc):
                pltpu.make_async_copy(zrow, xs_hbm.at[grp(0), :], sem_r).wait()
                return cc

            lax.fori_loop(0, head, start, 0)
            lax.fori_loop(0, ngrp, gstart, 0)
            lax.fori_loop(0, head, wait, 0)
            lax.fori_loop(0, ngrp, gwait, 0)
            return carry

        lax.fori_loop(0, N_EXPERTS, per_expert, 0)

        t0 = pstart_ref[N_EXPERTS]
        groups = pcnt_ref[N_EXPERTS]
        rows8 = lambda r: pl.ds(pl.multiple_of(t0 + r * 8, 8), 8)

        def tstart(r, cc):
            pltpu.make_async_copy(zrow, xs_hbm.at[rows8(r), :], sem_r).start()
            return cc

        def twait(r, cc):
            pltpu.make_async_copy(zrow, xs_hbm.at[rows8(0), :], sem_r).wait()
            return cc

        lax.fori_loop(0, groups, tstart, 0)
        lax.fori_loop(0, groups, twait, 0)


def _dispatch(h2, dest_flat, pad_start, pad_cnt, n_rows, tn):
    n_pad, d = h2.shape
    nt = n_pad // tn
    grid_spec = pltpu.PrefetchScalarGridSpec(
        num_scalar_prefetch=2,
        grid=(nt + 1,),
        in_specs=[pl.BlockSpec((tn, d), lambda i, a, b: (jnp.minimum(i, nt - 1), 0)),
                  pl.BlockSpec(memory_space=pl.ANY)],
        out_specs=pl.BlockSpec(memory_space=pl.ANY),
        scratch_shapes=[pltpu.SMEM((TOP_K * tn,), jnp.int32),
                        pltpu.VMEM((tn, d // 2), jnp.uint32),
                        pltpu.VMEM((8, d // 2), jnp.uint32),
                        pltpu.SemaphoreType.DMA(()),
                        pltpu.SemaphoreType.DMA(())])
    return pl.pallas_call(
        functools.partial(_dispatch_kernel, tn=tn, nt=nt),
        grid_spec=grid_spec,
        out_shape=jax.ShapeDtypeStruct((n_rows, d // 2), jnp.uint32),
        compiler_params=_cparams(("arbitrary",)),
        name="dispatch",
    )(pad_start, pad_cnt, h2, dest_flat)


def _expert_kernel(blk_e_ref, nused_ref, next_e_ref, slot_ref, x_ref, wg_hbm, wu_hbm, wd_hbm, y_ref,
                   wg_f, wu_f, wd_f, wgb, wub, wdb, sems):
    i = pl.program_id(0)
    nused = nused_ref[0]
    ii = jnp.minimum(i, nused - 1)
    e = blk_e_ref[ii]
    e_prev = blk_e_ref[jnp.maximum(ii - 1, 0)]
    half = x_ref.shape[1]

    def copies(ex, sl):
        return [pltpu.make_async_copy(src.at[ex], dst.at[sl], sems.at[sl, t])
                for t, (src, dst) in enumerate(((wg_hbm, wg_f), (wu_hbm, wu_f), (wd_hbm, wd_f)))]

    @pl.when(i == 0)
    def _():
        for c in copies(e, slot_ref[e]):
            c.start()

    @pl.when((i < nused) & ((i == 0) | (e != e_prev)))
    def _():
        sl = slot_ref[e]
        for c in copies(e, sl):
            c.wait()
        nxt = next_e_ref[e]

        @pl.when(nxt >= 0)
        def _():
            for c in copies(nxt, 1 - sl):
                c.start()

        wgb[...] = wg_f[sl].astype(BF16)
        wub[...] = wu_f[sl].astype(BF16)
        wdb[...] = wd_f[sl].astype(BF16)

    @pl.when(i < nused)
    def _():
        lo, hi = _unpack_words(x_ref[...])
        xl = lo.astype(BF16)
        xh = hi.astype(BF16)
        g = _dot(xl, wgb[:half, :]) + _dot(xh, wgb[half:, :])
        u = _dot(xl, wub[:half, :]) + _dot(xh, wub[half:, :])
        hmid = (_silu(g) * u).astype(BF16)
        y = _dot(hmid, wdb[...])
        y_ref[...] = _pack_words(_bf16_exact(y[:, :half]), _bf16_exact(y[:, half:]))

    @pl.when(i >= nused)
    def _():
        y_ref[...] = jnp.zeros(y_ref.shape, y_ref.dtype)


def _experts(xs, blk_e, nused, next_e, slot, w_gate, w_up, w_down, tb):
    n_rows, half = xs.shape
    _, d, f = w_gate.shape
    nblk = n_rows // tb

    def xmap(i, be, nu, ne, sl):
        return (jnp.minimum(i, nu[0] - 1), 0)

    grid_spec = pltpu.PrefetchScalarGridSpec(
        num_scalar_prefetch=4,
        grid=(nblk,),
        in_specs=[pl.BlockSpec((tb, half), xmap),
                  pl.BlockSpec(memory_space=pl.ANY),
                  pl.BlockSpec(memory_space=pl.ANY),
                  pl.BlockSpec(memory_space=pl.ANY)],
        out_specs=pl.BlockSpec((tb, half), lambda i, be, nu, ne, sl: (i, 0)),
        scratch_shapes=[pltpu.VMEM((2, d, f), jnp.float32),
                        pltpu.VMEM((2, d, f), jnp.float32),
                        pltpu.VMEM((2, f, d), jnp.float32),
                        pltpu.VMEM((d, f), BF16),
                        pltpu.VMEM((d, f), BF16),
                        pltpu.VMEM((f, d), BF16),
                        pltpu.SemaphoreType.DMA((2, 3))])
    return pl.pallas_call(
        _expert_kernel,
        grid_spec=grid_spec,
        out_shape=jax.ShapeDtypeStruct((n_rows, half), jnp.uint32),
        compiler_params=_cparams(("arbitrary",)),
        name="experts",
    )(blk_e, nused, next_e, slot, xs, w_gate, w_up, w_down)


def _ffn_kernel(h_ref, wg_ref, wu_ref, wd_ref, o_ref, wgb, wub, wdb):
    @pl.when(pl.program_id(0) == 0)
    def _():
        wgb[...] = wg_ref[...].astype(BF16)
        wub[...] = wu_ref[...].astype(BF16)
        wdb[...] = wd_ref[...].astype(BF16)

    x = h_ref[...]
    hmid = (_silu(_dot(x, wgb[...])) * _dot(x, wub[...])).astype(BF16)
    o_ref[...] = _dot(hmid, wdb[...]).astype(o_ref.dtype)


def _ffn_shared(h2, wg, wu, wd, tm):
    n_pad, d = h2.shape
    f = wg.shape[1]
    return pl.pallas_call(
        _ffn_kernel,
        grid=(n_pad // tm,),
        in_specs=[pl.BlockSpec((tm, d), lambda i: (i, 0)),
                  pl.BlockSpec((d, f), lambda i: (0, 0)),
                  pl.BlockSpec((d, f), lambda i: (0, 0)),
                  pl.BlockSpec((f, d), lambda i: (0, 0))],
        out_specs=pl.BlockSpec((tm, d), lambda i: (i, 0)),
        out_shape=jax.ShapeDtypeStruct((n_pad, d), BF16),
        scratch_shapes=[pltpu.VMEM((d, f), BF16),
                        pltpu.VMEM((d, f), BF16),
                        pltpu.VMEM((f, d), BF16)],
        compiler_params=_cparams(("arbitrary",)),
        name="ffn_shared",
    )(h2, wg, wu, wd)


def _combine_kernel(dest_hbm, ys_hbm, wts_ref, sh_ref, x_ref, gt_ref, g_ref, o_ref,
                    dsm, buf, sem_d, sem_r, *, tn, tile0, final_norm):
    i = pl.program_id(0)
    cp = pltpu.make_async_copy(dest_hbm.at[pl.ds((tile0 + i) * (TOP_K * tn), TOP_K * tn)], dsm, sem_d)
    cp.start()
    cp.wait()

    def body(r, carry):
        for k in range(TOP_K):
            d = dsm[k * tn + r]
            pltpu.make_async_copy(ys_hbm.at[pl.ds(d, 1), :], buf.at[k, pl.ds(r, 1), :],
                                  sem_r).start(priority=k % 2)
        return carry

    lax.fori_loop(0, tn, body, 0)
    for k in range(TOP_K):
        pltpu.make_async_copy(ys_hbm.at[pl.ds(0, tn), :], buf.at[k], sem_r).wait()

    half = buf.shape[2]
    wts = wts_ref[...]
    acc_lo = jnp.zeros((tn, half), jnp.float32)
    acc_hi = jnp.zeros((tn, half), jnp.float32)
    for k in range(TOP_K):
        lo, hi = _unpack_words(buf[k])
        wk = wts[:, k:k + 1]
        acc_lo = acc_lo + wk * lo
        acc_hi = acc_hi + wk * hi
    sh = sh_ref[...].astype(jnp.float32)
    gt = gt_ref[0]
    x_lo = x_ref[:, :half] + gt[:, :half] * (acc_lo + sh[:, :half])
    x_hi = x_ref[:, half:] + gt[:, half:] * (acc_hi + sh[:, half:])
    if final_norm:
        ms = (jnp.sum(x_lo * x_lo, axis=-1, keepdims=True)
              + jnp.sum(x_hi * x_hi, axis=-1, keepdims=True)) / (2 * half)
        inv = lax.rsqrt(ms + EPS)
        g = g_ref[...]
        x_lo = x_lo * inv * g[:, :half]
        x_hi = x_hi * inv * g[:, half:]
    o_ref[:, :half] = x_lo
    o_ref[:, half:] = x_hi


def _combine(dest_flat, ys, wts_tok, shared, x1, gt3, g_final, tile0, tn, final_norm):
    rows, d = x1.shape
    nt = rows // tn
    gr = gt3.shape[1]
    per = nt // gt3.shape[0]
    return pl.pallas_call(
        functools.partial(_combine_kernel, tn=tn, tile0=tile0, final_norm=final_norm),
        grid=(nt,),
        in_specs=[pl.BlockSpec(memory_space=pl.ANY),
                  pl.BlockSpec(memory_space=pl.ANY),
                  pl.BlockSpec((tn, TOP_K), lambda i: (tile0 + i, 0)),
                  pl.BlockSpec((tn, d), lambda i: (tile0 + i, 0)),
                  pl.BlockSpec((tn, d), lambda i: (i, 0)),
                  pl.BlockSpec((1, gr, d), lambda i: (i // per, 0, 0)),
                  pl.BlockSpec((1, d), lambda i: (0, 0))],
        out_specs=pl.BlockSpec((tn, d), lambda i: (i, 0)),
        out_shape=jax.ShapeDtypeStruct((rows, d), jnp.float32),
        scratch_shapes=[pltpu.SMEM((TOP_K * tn,), jnp.int32),
                        pltpu.VMEM((TOP_K, tn, d // 2), jnp.uint32),
                        pltpu.SemaphoreType.DMA(()),
                        pltpu.SemaphoreType.DMA(())],
        compiler_params=_cparams(("arbitrary",)),
        name="combine",
    )(dest_flat, ys, wts_tok, shared, x1, gt3, g_final.reshape(1, d))


def _page_specs(shape, n, pg):
    def mk(u):
        return pl.BlockSpec((1,) + shape, lambda b, j, pt: (pt[b, j * pg + u], 0, 0))
    return [mk(u) for u in range(n)]


def _kv_page_specs(kshape, vshape, pg, ns):
    def mk(shape, first):
        def one(u):
            def index(b, j, pt):
                step = jnp.minimum(j, ns - 1) if first else jnp.maximum(j - ns, 0)
                return (pt[b, step * pg + u], 0, 0)
            return pl.BlockSpec((1,) + shape, index)
        return [one(u) for u in range(pg)]
    return mk(kshape, True) + mk(vshape, False)


def _softmax_pages(s_ref):
    s = s_ref[...]
    m = jnp.max(jnp.max(s, axis=0, keepdims=True), axis=2, keepdims=True)
    e = jnp.exp(s - m)
    return e / jnp.sum(jnp.sum(e, axis=0, keepdims=True), axis=2, keepdims=True)


def _da_sample_kernel(pt_ref, lam_ref, q_ref, kn_ref, vn_ref, bl_ref, cf_ref, b0_ref, ex_ref, hm_ref,
                      *rest, pg, n_pages):
    k_refs, v_refs = rest[:pg], rest[pg:2 * pg]
    o_ref, s_ref, a_ref, acc_ref = rest[2 * pg:]
    j = pl.program_id(1)
    ns = n_pages // pg
    r = q_ref.shape[1]
    nh = r // 2
    lane = lax.broadcasted_iota(jnp.int32, (r, PAGE_SIZE), 1)

    @pl.when(j < ns)
    def _():
        q = q_ref[0]
        qb = _bf(q)
        for u in range(pg):
            page = j * pg + u
            s = _dot(qb, _bf(k_refs[u][0]))
            s_ref[page] = s + jnp.where(page == (n_pages - 1), bl_ref[...], cf_ref[...])

        @pl.when(j == 0)
        def _():
            s_new = jnp.sum(_rounded(q) * _rounded(kn_ref[0]), axis=-1, keepdims=True) + b0_ref[...]
            s_ref[n_pages] = jnp.where(lane == 0, s_new, NEG)

    @pl.when(j == ns)
    def _():
        p = _softmax_pages(s_ref)
        a = p[:, :nh, :] - lam_ref[0] * p[:, nh:, :]
        a_ref[...] = _bf(jnp.concatenate([a, jnp.zeros_like(a)], axis=1))
        a_new = a_ref[n_pages][:, 0:1].astype(jnp.float32)
        acc_ref[...] = a_new * _rounded(vn_ref[0])

    @pl.when(j >= ns)
    def _():
        acc = acc_ref[...]
        for u in range(pg):
            page = (j - ns) * pg + u
            pe = _bf(_dot(a_ref[page], ex_ref[...]) * hm_ref[...])
            acc = acc + _dot(pe, _bf(v_refs[u][0]))
        acc_ref[...] = acc

    @pl.when(j == pl.num_programs(1) - 1)
    def _():
        o_ref[0] = acc_ref[...]


def _da_sample(page_table, lam, qbd, knew, vnew, bias_last, cfar, bias0, kt, v2, pg):
    nb, n_pages = page_table.shape
    r, w = qbd.shape[1:]
    rows_v, dv = v2.shape[1:]
    nh = rows_v // PAGE_SIZE
    ns = n_pages // pg
    col = jnp.arange(rows_v, dtype=jnp.int32)
    expand = (col[None, :] // nh == jnp.arange(PAGE_SIZE, dtype=jnp.int32)[:, None]).astype(BF16)
    head_mask = (col[None, :] % nh == jnp.arange(r, dtype=jnp.int32)[:, None]).astype(jnp.float32)
    full = lambda shp: pl.BlockSpec(shp, lambda b, j, pt: (0,) * len(shp))
    per_b = lambda shp: pl.BlockSpec((1,) + shp, lambda b, j, pt: (b, 0, 0))
    grid_spec = pltpu.PrefetchScalarGridSpec(
        num_scalar_prefetch=1,
        grid=(nb, 2 * ns),
        in_specs=[pl.BlockSpec(memory_space=pltpu.SMEM),
                  per_b((r, w)), per_b((1, w)), per_b((r, dv)),
                  full((r, PAGE_SIZE)), full((r, 1)), full((r, 1)),
                  full((PAGE_SIZE, rows_v)), full((r, rows_v))]
                 + _kv_page_specs((w, PAGE_SIZE), (rows_v, dv), pg, ns),
        out_specs=per_b((r, dv)),
        scratch_shapes=[pltpu.VMEM((n_pages + 1, r, PAGE_SIZE), jnp.float32),
                        pltpu.VMEM((n_pages + 1, r, PAGE_SIZE), BF16),
                        pltpu.VMEM((r, dv), jnp.float32)])
    return pl.pallas_call(
        functools.partial(_da_sample_kernel, pg=pg, n_pages=n_pages),
        grid_spec=grid_spec,
        out_shape=jax.ShapeDtypeStruct((nb, r, dv), jnp.float32),
        compiler_params=_cparams(("arbitrary", "arbitrary")),
        name="da_sample",
    )(page_table, lam, qbd, knew, vnew, bias_last, cfar, bias0, expand, head_mask,
      *([kt] * pg), *([v2] * pg))


def _idx_sample_kernel(pt_ref, q_ref, w_ref, kn_ref, *rest, pg, n_pages, topk):
    k_refs = rest[:pg]
    sel_ref, sc_ref = rest[pg:]
    j = pl.program_id(1)
    q = q_ref[0]
    w = _rounded(w_ref[0])
    scale = IDX_DK ** -0.5 * IDX_HEADS ** -0.5
    rows = sc_ref.shape[0]
    lane = lax.broadcasted_iota(jnp.int32, (1, PAGE_SIZE), 1)

    @pl.when(j == 0)
    def _():
        sc_ref[...] = jnp.full(sc_ref.shape, -jnp.inf, jnp.float32)
        d = jnp.maximum(jnp.sum(_rounded(q) * _rounded(kn_ref[0]), axis=-1, keepdims=True), 0.0)
        s_new = jnp.sum(w * _rounded(d), axis=0, keepdims=True) * scale
        sc_ref[n_pages:n_pages + 1, :] = jnp.where(lane == 0, s_new, -jnp.inf)

    qb = _bf(q)
    for u in range(pg):
        d = _rounded(jnp.maximum(_dot(qb, _bf(k_refs[u][0])), 0.0))
        sc_ref[pl.ds(j * pg + u, 1), :] = jnp.sum(w * d, axis=0, keepdims=True) * scale

    @pl.when(j == pl.num_programs(1) - 1)
    def _():
        key = _float_key(sc_ref[...])
        pos = (lax.broadcasted_iota(jnp.int32, key.shape, 0) * PAGE_SIZE
               + lax.broadcasted_iota(jnp.int32, key.shape, 1))

        def bit_body(it, thr):
            cand = thr + (jnp.int32(1) << (31 - it))
            cnt = jnp.sum(jnp.where(key >= cand, 1.0, 0.0))
            return jnp.where(cnt >= float(topk), cand, thr)

        thr = lax.fori_loop(0, 32, bit_body, jnp.int32(INT_MIN))
        thr = jnp.maximum(thr, jnp.int32(KEY_NEG_INF + 1))
        need = float(topk) - jnp.sum(jnp.where(key > thr, 1.0, 0.0))
        n_bits = max(1, int(rows * PAGE_SIZE).bit_length())

        def cut_body(it, cut):
            cand = cut + (jnp.int32(1) << (n_bits - 1 - it))
            n_eq = jnp.sum(jnp.where((key == thr) & (pos < cand), 1.0, 0.0))
            return jnp.where(n_eq <= need, cand, cut)

        cut = lax.fori_loop(0, n_bits, cut_body, jnp.int32(0))
        sel_ref[0] = jnp.where((key > thr) | ((key == thr) & (pos < cut)), 1.0, 0.0)


def _idx_sample(page_table, qix, wix, knew, kc, pg, topk):
    nb, n_pages = page_table.shape
    rows = -(-(n_pages + 1) // 8) * 8
    per_b = lambda shp: pl.BlockSpec((1,) + shp, lambda b, j, pt: (b, 0, 0))
    grid_spec = pltpu.PrefetchScalarGridSpec(
        num_scalar_prefetch=1,
        grid=(nb, n_pages // pg),
        in_specs=[per_b((IDX_HEADS, IDX_DK)), per_b((IDX_HEADS, 1)), per_b((1, IDX_DK))]
                 + _page_specs((IDX_DK, PAGE_SIZE), pg, pg),
        out_specs=per_b((rows, PAGE_SIZE)),
        scratch_shapes=[pltpu.VMEM((rows, PAGE_SIZE), jnp.float32)])
    return pl.pallas_call(
        functools.partial(_idx_sample_kernel, pg=pg, n_pages=n_pages, topk=topk),
        grid_spec=grid_spec,
        out_shape=jax.ShapeDtypeStruct((nb, rows, PAGE_SIZE), jnp.float32),
        compiler_params=_cparams(("arbitrary", "arbitrary")),
        name="idx_sample",
    )(page_table, qix, wix, knew, *([kc] * pg))


def _sa_sample_kernel(pt_ref, q_ref, kn_ref, vn_ref, sel_ref, bl_ref, cf_ref, b0_ref, gm_ref, *rest,
                      pg, n_pages):
    k_refs, v_refs = rest[:pg], rest[pg:2 * pg]
    o_ref, s_ref, p_ref, acc_ref = rest[2 * pg:]
    j = pl.program_id(1)
    ns = n_pages // pg
    scale = SA_DH ** -0.5
    r, cols = gm_ref.shape
    lane = lax.broadcasted_iota(jnp.int32, (r, cols), 1)

    @pl.when(j < ns)
    def _():
        q = q_ref[0]
        qb = _bf(q)
        for u in range(pg):
            page = j * pg + u
            s = _dot_nt(qb, _bf(k_refs[u][0])) * scale
            s = s + jnp.where(page == (n_pages - 1), bl_ref[...], cf_ref[...])
            keep = (sel_ref[0, pl.ds(page, 1), :] > 0.0) & (gm_ref[...] > 0.0)
            s_ref[page] = jnp.where(keep, s, NEG)

        @pl.when(j == 0)
        def _():
            on = sel_ref[0, n_pages:n_pages + 1, 0:1] > 0.0
            s_new = jnp.sum(_rounded(q) * _rounded(kn_ref[0]), axis=-1, keepdims=True) * scale + b0_ref[...]
            s_ref[n_pages] = jnp.where((lane == 0) & on, s_new, NEG)

    @pl.when(j == ns)
    def _():
        p_ref[...] = _bf(_softmax_pages(s_ref))
        acc_ref[...] = p_ref[n_pages][:, 0:1].astype(jnp.float32) * _rounded(vn_ref[0])

    @pl.when(j >= ns)
    def _():
        acc = acc_ref[...]
        for u in range(pg):
            acc = acc + _dot(p_ref[(j - ns) * pg + u], _bf(v_refs[u][0]))
        acc_ref[...] = acc

    @pl.when(j == pl.num_programs(1) - 1)
    def _():
        o_ref[0] = acc_ref[...]


def _sa_sample(page_table, q, knew, vnew, sel2, bias_last2, cfar, bias0, group_mask, k2, v2, pg):
    nb, n_pages = page_table.shape
    r, dh = q.shape[1:]
    srows, cols = sel2.shape[1:]
    ns = n_pages // pg
    full = lambda shp: pl.BlockSpec(shp, lambda b, j, pt: (0,) * len(shp))
    per_b = lambda shp: pl.BlockSpec((1,) + shp, lambda b, j, pt: (b, 0, 0))
    grid_spec = pltpu.PrefetchScalarGridSpec(
        num_scalar_prefetch=1,
        grid=(nb, 2 * ns),
        in_specs=[per_b((r, dh)), per_b((r, dh)), per_b((r, dh)), per_b((srows, cols)),
                  full((r, cols)), full((r, 1)), full((r, 1)), full((r, cols))]
                 + _kv_page_specs((cols, dh), (cols, dh), pg, ns),
        out_specs=per_b((r, dh)),
        scratch_shapes=[pltpu.VMEM((n_pages + 1, r, cols), jnp.float32),
                        pltpu.VMEM((n_pages + 1, r, cols), BF16),
                        pltpu.VMEM((r, dh), jnp.float32)])
    return pl.pallas_call(
        functools.partial(_sa_sample_kernel, pg=pg, n_pages=n_pages),
        grid_spec=grid_spec,
        out_shape=jax.ShapeDtypeStruct((nb, r, dh), jnp.float32),
        compiler_params=_cparams(("arbitrary", "arbitrary")),
        name="sa_sample",
    )(page_table, q, knew, vnew, sel2, bias_last2, cfar, bias0, group_mask, *([k2] * pg), *([v2] * pg))


ROW_TILE = 256
COMBINE_TILE = 512
EXPERT_BLOCK = 512
TAIL_ROWS = 256
PAGES_PER_STEP_DA = 16
PAGES_PER_STEP_SA = 32
PAGES_PER_STEP_IDX = 64


def _rms_rows(x, g):
    return x * lax.rsqrt(jnp.mean(x * x, axis=-1, keepdims=True) + EPS) * g


def _pad_rows(x, rows):
    return jnp.pad(x, ((0, rows - x.shape[0]), (0, 0)))


def _tile_major(dest, tn):
    k, n = dest.shape
    return dest.reshape(k, n // tn, tn).transpose(1, 0, 2).reshape(-1)


def kernel(x_prompt, x_sample, c_prompt, c_sample, cache_da_k, cache_da_v, cache_sa_k, cache_sa_v, cache_idx_k, page_table, rel_bias_table, w_ada, b_ada, g_attn, g_ffn, w_in, lambda_q1, lambda_k1, lambda_q2, lambda_k2, g_subln, w_proj_da, w_proj_sa, w_out, w_router, b_router, w_gate, w_up, w_down, w_sh_gate, w_sh_up, w_sh_down, g_final):
    f32, bf16 = jnp.float32, BF16
    nb, t, d = x_prompt.shape
    nbs, dec_seq, _ = x_sample.shape
    assert dec_seq == 1
    depth = w_in.shape[0]
    n_pages = page_table.shape[1]
    past_len = n_pages * PAGE_SIZE
    n = nb * t
    n_pad = n + TAIL_ROWS
    tq = min(256, t)
    tq_da = min(512, t)
    assert t % tq_da == 0
    assert tq >= MAX_DISTANCE and t % tq == 0 and n % ROW_TILE == 0 and nbs <= TAIL_ROWS
    assert TAIL_ROWS == ROW_TILE and EXPERT_BLOCK % 8 == 0
    topk_p = min(SA_TOPK_MAX, t // 4)
    topk_s = min(SA_TOPK_MAX, (past_len + dec_seq) // 4)
    assert topk_p <= tq

    sizes = [DA_HEADS * 2 * DA_DK, DA_HEADS * 2 * DA_DK, DA_HEADS * DA_DV, SA_HEADS * SA_DH,
             SA_KV_HEADS * SA_DH, SA_KV_HEADS * SA_DH, IDX_HEADS * IDX_DK, IDX_DK, IDX_HEADS, d, d]
    offs = [sum(sizes[:i]) for i in range(len(sizes) + 1)]
    (o_qda, o_kda, o_vda, o_qsa, o_ksa, o_vsa, o_qix, o_kix, o_wix, o_gda, o_gsa, _) = offs
    misc_w = 2 * LANES

    table = rel_bias_table.astype(f32)
    cfar = table[N_BUCKETS - 1]
    tiles_da = _near_tiles(table[:, :DA_HEADS], tq_da)
    tiles_sa = _near_tiles(table[:, DA_HEADS:], tq)
    last_dist = past_len - ((n_pages - 1) * PAGE_SIZE + jnp.arange(PAGE_SIZE, dtype=jnp.int32))
    bias_last = _bias_by_distance(table, last_dist)
    bias0 = _bias_by_distance(table, jnp.zeros((1,), jnp.int32))

    xp = x_prompt.reshape(n, d)
    xs = x_sample.reshape(nbs, d)
    c_all = jnp.concatenate([c_prompt, c_sample], axis=0)
    leaves_p, leaves_s = [], []
    for l in range(depth):
        lam_init = 0.8 - 0.6 * math.exp(-0.3 * l)
        lam = (jnp.exp(jnp.sum(lambda_q1[l].astype(f32) * lambda_k1[l].astype(f32)))
               - jnp.exp(jnp.sum(lambda_q2[l].astype(f32) * lambda_k2[l].astype(f32))) + lam_init)
        mod = _linear_small(c_all, w_ada[l], b_ada[l], silu_in=True)
        sh1, sc1, gt1, sh2, sc2, gt2 = jnp.split(mod[:nb], 6, axis=-1)
        sh1s, sc1s, gt1s, sh2s, sc2s, gt2s = jnp.split(mod[nb:], 6, axis=-1)
        w = w_in[l]

        h = _norm_mod(xp, g_attn[l], sc1, sh1, t, ROW_TILE)
        q_da, = _mm(h, w, o_qda, sizes[0], (bf16,), scale=DA_DK ** -0.5)
        k_da_b, k_da_t = _mm(h, w, o_kda, sizes[1], (bf16,), rows_per_batch=t)
        k_da = k_da_t.reshape(nb, DA_HEADS, 2, DA_DK, t).transpose(0, 4, 1, 2, 3)
        v_da, v_da_b = _mm(h, w, o_vda, sizes[2], (f32, bf16))
        q_sa, = _mm(h, w, o_qsa, sizes[3], (bf16,), scale=SA_DH ** -0.5)
        k_sa, k_sa_b = _mm(h, w, o_ksa, sizes[4], (f32, bf16))
        v_sa, v_sa_b = _mm(h, w, o_vsa, sizes[5], (f32, bf16))
        q_ix, = _mm(h, w, o_qix, sizes[6], (bf16,), tn=512)
        misc, misc_b = _mm(h, w, o_kix, misc_w, (f32, bf16))
        gates, = _mm(h, w[:, o_gda:], 0, 2 * d, (bf16,), sigmoid=True)
        k_ix = misc[:, :IDX_DK]

        o_da = _da_prompt(q_da, k_da_b, v_da_b, tiles_da, cfar[:DA_HEADS], lam.reshape(1),
                          g_subln[l].astype(f32), nb, t, tq_da, 1.0 - lam_init)
        o_sa = _sa_prompt(q_sa, q_ix, misc, misc_b, k_sa_b, v_sa_b, tiles_sa, cfar[DA_HEADS:], nb, t, tq, topk_p)
        m = _proj_gate(o_da, o_sa, w_proj_da[l], w_proj_sa[l], gates, d)
        x1 = _mm_resid(m, w_out[l], xp, gt1, t)

        hs = _rms_rows(xs, g_attn[l].astype(f32)) * (1.0 + sc1s) + sh1s
        ps = _linear_small(hs, w)
        seg = lambda i: ps[:, offs[i]:offs[i + 1]]
        q_da_s, k_da_s, v_da_s, q_sa_s, k_sa_s, v_sa_s, q_ix_s, k_ix_s, w_ix_s, gda_s, gsa_s = [
            seg(i) for i in range(11)]

        r_da = 2 * DA_HEADS
        q16 = (q_da_s * DA_DK ** -0.5).reshape(nbs, DA_HEADS, 2, DA_DK).transpose(0, 2, 1, 3).reshape(nbs, r_da, DA_DK)
        blk_of_row = 2 * (jnp.arange(r_da) % DA_HEADS) + jnp.arange(r_da) // DA_HEADS
        place = (blk_of_row[:, None] == jnp.arange(r_da)[None, :]).astype(f32)
        qbd_da = (place[None, :, :, None] * q16[:, :, None, :]).reshape(nbs, r_da, -1)
        both = lambda a: jnp.concatenate([a, a], axis=0)
        kt_da = cache_da_k[l].transpose(0, 2, 3, 4, 1).reshape(-1, DA_HEADS * 2 * DA_DK, PAGE_SIZE)
        v2_da = cache_da_v[l].reshape(-1, PAGE_SIZE * DA_HEADS, DA_DV)
        v_rows = jnp.pad(v_da_s.reshape(nbs, DA_HEADS, DA_DV), ((0, 0), (0, r_da - DA_HEADS), (0, 0)))
        a_da = _da_sample(page_table, lam.reshape(1), qbd_da, k_da_s[:, None, :], v_rows,
                          both(bias_last[:DA_HEADS]), both(cfar[:DA_HEADS, None]), both(bias0[:DA_HEADS]),
                          kt_da, v2_da, math.gcd(PAGES_PER_STEP_DA, n_pages))
        o_da_s = a_da[:, :DA_HEADS]
        o_da_s = _rms_rows(o_da_s, g_subln[l].astype(f32)) * (1.0 - lam_init)

        sel = _idx_sample(page_table, q_ix_s.reshape(nbs, IDX_HEADS, IDX_DK), w_ix_s[:, :, None],
                          k_ix_s[:, None, :], cache_idx_k[l].transpose(0, 2, 1),
                          math.gcd(PAGES_PER_STEP_IDX, n_pages), topk_s)
        r_sa = 2 * SA_HEADS
        kv_of_row = jnp.minimum(jnp.arange(r_sa) // SA_GROUP, SA_KV_HEADS - 1)
        pad_sa = lambda a: jnp.pad(a, ((0, 0), (0, r_sa - SA_HEADS), (0, 0)))
        rep_kv = lambda a: jnp.repeat(a, SA_KV_HEADS, axis=-1)
        col_kv = jnp.arange(PAGE_SIZE * SA_KV_HEADS) % SA_KV_HEADS
        k2_sa = cache_sa_k[l].reshape(-1, PAGE_SIZE * SA_KV_HEADS, SA_DH)
        v2_sa = cache_sa_v[l].reshape(-1, PAGE_SIZE * SA_KV_HEADS, SA_DH)
        a_sa = _sa_sample(page_table, pad_sa(q_sa_s.reshape(nbs, SA_HEADS, SA_DH)),
                          k_sa_s.reshape(nbs, SA_KV_HEADS, SA_DH)[:, kv_of_row],
                          v_sa_s.reshape(nbs, SA_KV_HEADS, SA_DH)[:, kv_of_row],
                          rep_kv(sel), rep_kv(_pad_rows(bias_last[DA_HEADS:], r_sa)),
                          _pad_rows(cfar[DA_HEADS:, None], r_sa), _pad_rows(bias0[DA_HEADS:], r_sa),
                          (col_kv[None, :] == kv_of_row[:, None]).astype(f32),
                          k2_sa, v2_sa, math.gcd(PAGES_PER_STEP_SA, n_pages))
        o_sa_s = a_sa[:, :SA_HEADS]

        pda = _linear_small(o_da_s.reshape(nbs, -1), w_proj_da[l])
        psa = _linear_small(o_sa_s.reshape(nbs, -1), w_proj_sa[l])
        ms = _sigmoid(gda_s) * pda + _sigmoid(gsa_s) * psa
        x1s = xs + gt1s * _linear_small(ms, w_out[l])
        h2s = _rms_rows(x1s, g_ffn[l].astype(f32)) * (1.0 + sc2s) + sh2s
        lg_s = _linear_small(h2s, w_router[l])

        h2_all, lg_all = _norm_router(x1, g_ffn[l], sc2, sh2, w_router[l].T,
                                      _pad_rows(h2s.astype(bf16), TAIL_ROWS),
                                      _pad_rows(lg_s, TAIL_ROWS).T, t, ROW_TILE)
        eidx, wts, rank, cnt = _route(lg_all, b_router[l], ROW_TILE)
        cnt_tile = cnt[:, :, 0]
        total = jnp.sum(cnt_tile, axis=0)
        padded = jnp.ceil(total / EXPERT_BLOCK) * EXPERT_BLOCK
        pends = jnp.cumsum(padded)
        pstart = pends - padded
        base = pstart[None, :] + jnp.cumsum(cnt_tile, axis=0) - cnt_tile
        dest = _dest(eidx, rank, jnp.broadcast_to(base[:, :, None], base.shape + (LANES,)), ROW_TILE)
        n_blk = -(-(n_pad * TOP_K) // EXPERT_BLOCK) + N_EXPERTS
        blk_start = (jnp.arange(n_blk) * EXPERT_BLOCK).astype(f32)
        blk_e = jnp.minimum(jnp.sum(pends[None, :] <= blk_start[:, None], axis=1), N_EXPERTS - 1).astype(jnp.int32)
        n_used = (pends[-1] / EXPERT_BLOCK).astype(jnp.int32).reshape(1)
        n_rows = n_blk * EXPERT_BLOCK
        pad_start = jnp.concatenate([pstart + total, pends[-1:]]).astype(jnp.int32)
        pad_cnt = jnp.concatenate([padded - total, (n_rows - pends[-1:]) / 8]).astype(jnp.int32)
        xs_sorted = _dispatch(h2_all, _tile_major(dest, ROW_TILE), pad_start, pad_cnt, n_rows, ROW_TILE)
        owns = padded > 0
        e_ids = jnp.arange(N_EXPERTS, dtype=jnp.int32)
        later = jnp.where(owns[None, :] & (e_ids[None, :] > e_ids[:, None]), e_ids[None, :], N_EXPERTS)
        next_owner = jnp.min(later, axis=1)
        next_owner = jnp.where(next_owner < N_EXPERTS, next_owner, -1).astype(jnp.int32)
        slot_of_e = ((jnp.cumsum(owns.astype(jnp.int32)) - 1) % 2).astype(jnp.int32)
        ys = _experts(xs_sorted, blk_e, n_used, next_owner, slot_of_e,
                      w_gate[l], w_up[l], w_down[l], EXPERT_BLOCK)
        shared = _ffn_shared(h2_all, w_sh_gate[l], w_sh_up[l], w_sh_down[l], ROW_TILE)
        tn_c = math.gcd(COMBINE_TILE, t)
        wts_tok = wts.T
        last = l == depth - 1
        xp = _combine(_tile_major(dest[:, :n], tn_c), ys, wts_tok, shared, x1, gt2[:, None, :],
                      g_final.astype(f32), 0, tn_c, last)
        tail = _combine(_tile_major(dest, TAIL_ROWS), ys, wts_tok, shared, _pad_rows(x1s, TAIL_ROWS),
                        _pad_rows(gt2s, TAIL_ROWS).reshape(1, TAIL_ROWS, d), g_final.astype(f32),
                        n // TAIL_ROWS, TAIL_ROWS, last)
        xs = tail[:nbs]
        leaves_p.append((k_da, v_da, k_sa, v_sa, k_ix))
        leaves_s.append((k_da_s, v_da_s, k_sa_s, v_sa_s, k_ix_s))

    shapes = [(DA_HEADS, 2, DA_DK), (DA_HEADS, DA_DV), (SA_KV_HEADS, SA_DH), (SA_KV_HEADS, SA_DH), (IDX_DK,)]
    out_p = [jnp.stack([lv[i].reshape((nb, t) + shapes[i]) for lv in leaves_p]) for i in range(5)]
    out_s = [jnp.stack([lv[i].reshape((nbs, dec_seq) + shapes[i]) for lv in leaves_s]) for i in range(5)]
    return (xp.reshape(nb, t, d), xs.reshape(nbs, dec_seq, d), *out_p, *out_s)
```

```python
import functools
import math

import jax
import jax.numpy as jnp
from jax import lax
from jax.experimental import pallas as pl
from jax.experimental.pallas import tpu as pltpu

DA_HEADS = 8
DA_DK = 64
DA_DV = 2 * DA_DK
SA_HEADS = 8
SA_KV_HEADS = 2
SA_DH = 128
SA_GROUP = SA_HEADS // SA_KV_HEADS
IDX_HEADS = 16
IDX_DK = 64
SA_TOPK_MAX = 256
N_BUCKETS = 32
MAX_DISTANCE = 128
N_EXPERTS = 64
N_GROUPS = 8
GROUP_SIZE = N_EXPERTS // N_GROUPS
TOPK_GROUPS = 4
TOP_K = 8
ROUTED_SCALE = 2.5
PAGE_SIZE = 128
EPS = 1e-6

LANES = 128
VMEM_LIMIT = 56 * 1024 * 1024

BF16 = jnp.bfloat16
NEG = -1e30
INT_MIN = -(2 ** 31)
KEY_NEG_INF = (0xFF800000 ^ 0x7FFFFFFF) - (1 << 32)

_NT = (((1,), (1,)), ((), ()))


def _cparams(sem):
    return pltpu.CompilerParams(dimension_semantics=sem, vmem_limit_bytes=VMEM_LIMIT)


def _dot(a, b):
    return jnp.dot(a, b, preferred_element_type=jnp.float32)


def _dot_nt(a, b):
    return lax.dot_general(a, b, _NT, preferred_element_type=jnp.float32)


def _bf(x):
    return x.astype(BF16)


def _rounded(x):
    return x.astype(BF16).astype(jnp.float32)


def _sigmoid(x):
    return 1.0 / (1.0 + jnp.exp(-x))


def _silu(x):
    return x * _sigmoid(x)


def _float_key(s):
    b = pltpu.bitcast(s, jnp.int32)
    return b ^ ((b >> 31) & jnp.int32(0x7FFFFFFF))


def _linear_small_kernel(x_ref, w_ref, b_ref, o_ref, *, silu_in):
    x = x_ref[...]
    if silu_in:
        x = _silu(x)
    o_ref[...] = _dot(_bf(x), _bf(w_ref[...])) + b_ref[...]


def _linear_small(x, w, b=None, *, silu_in=False, tn=512):
    m0, k = x.shape
    m = -(-m0 // 16) * 16
    x = jnp.pad(x, ((0, m - m0), (0, 0)))
    n = w.shape[1]
    tn = min(tn, n)
    if b is None:
        b = jnp.zeros((1, n), jnp.float32)
    out = pl.pallas_call(
        functools.partial(_linear_small_kernel, silu_in=silu_in),
        grid=(pl.cdiv(n, tn),),
        in_specs=[pl.BlockSpec((m, k), lambda j: (0, 0)),
                  pl.BlockSpec((k, tn), lambda j: (0, j)),
                  pl.BlockSpec((1, tn), lambda j: (0, j))],
        out_specs=pl.BlockSpec((m, tn), lambda j: (0, j)),
        out_shape=jax.ShapeDtypeStruct((m, n), jnp.float32),
        compiler_params=_cparams(("arbitrary",)),
        name="linear_small",
    )(x, w, b.reshape(1, n))
    return out[:m0]


def _norm_mod_kernel(x_ref, g_ref, sc_ref, sh_ref, o_ref):
    x = x_ref[...]
    y = x * lax.rsqrt(jnp.mean(x * x, axis=-1, keepdims=True) + EPS) * g_ref[...]
    o_ref[...] = (y * (1.0 + sc_ref[0]) + sh_ref[0]).astype(o_ref.dtype)


def _norm_mod(x, g, sc, sh, rows_per_batch, tm):
    n, d = x.shape
    per = rows_per_batch // tm
    return pl.pallas_call(
        _norm_mod_kernel,
        grid=(n // tm,),
        in_specs=[pl.BlockSpec((tm, d), lambda i: (i, 0)),
                  pl.BlockSpec((1, d), lambda i: (0, 0)),
                  pl.BlockSpec((1, 1, d), lambda i: (i // per, 0, 0)),
                  pl.BlockSpec((1, 1, d), lambda i: (i // per, 0, 0))],
        out_specs=pl.BlockSpec((tm, d), lambda i: (i, 0)),
        out_shape=jax.ShapeDtypeStruct((n, d), BF16),
        compiler_params=_cparams(("arbitrary",)),
        name="norm_mod",
    )(x, g.reshape(1, d), sc[:, None, :], sh[:, None, :])


def _mm_kernel(x_ref, w_ref, *rest, scale, sigmoid, n_out, transposed_out):
    o_refs, wbf_ref = rest[:n_out], rest[-1]

    @pl.when(pl.program_id(1) == 0)
    def _():
        wbf_ref[...] = w_ref[...].astype(BF16)

    acc = _dot(x_ref[...], wbf_ref[...])
    if scale != 1.0:
        acc = acc * scale
    if sigmoid:
        acc = _sigmoid(acc)
    for o in o_refs:
        o[...] = acc.astype(o.dtype)
    if transposed_out:
        rest[n_out][0] = acc.T


def _mm(x, w, col0, ncols, out_dtypes, *, scale=1.0, sigmoid=False, tm=1024, tn=1024, rows_per_batch=None):
    m, k = x.shape
    tn = min(tn, ncols)
    tm = min(tm, m)
    assert col0 % tn == 0 and ncols % tn == 0 and m % tm == 0
    jb = col0 // tn
    out_specs = [pl.BlockSpec((tm, tn), lambda j, i: (i, j)) for _ in out_dtypes]
    out_shape = [jax.ShapeDtypeStruct((m, ncols), dt) for dt in out_dtypes]
    if rows_per_batch is not None:
        per = rows_per_batch // tm
        out_specs.append(pl.BlockSpec((1, tn, tm), lambda j, i: (i // per, j, i % per)))
        out_shape.append(jax.ShapeDtypeStruct((m // rows_per_batch, ncols, rows_per_batch), jnp.float32))
    outs = pl.pallas_call(
        functools.partial(_mm_kernel, scale=scale, sigmoid=sigmoid, n_out=len(out_dtypes),
                          transposed_out=rows_per_batch is not None),
        grid=(ncols // tn, m // tm),
        in_specs=[pl.BlockSpec((tm, k), lambda j, i: (i, 0)),
                  pl.BlockSpec((k, tn), lambda j, i: (0, jb + j))],
        out_specs=out_specs,
        out_shape=out_shape,
        scratch_shapes=[pltpu.VMEM((k, tn), BF16)],
        compiler_params=_cparams(("arbitrary", "arbitrary")),
        name="mm_cols",
    )(x, w)
    return outs


def _rel_bucket(dist):
    max_exact = N_BUCKETS // 2
    d = jnp.maximum(dist, 0)
    large = max_exact + (jnp.log(jnp.maximum(d, 1).astype(jnp.float32) / max_exact)
                         / math.log(MAX_DISTANCE / max_exact)
                         * (N_BUCKETS - max_exact)).astype(jnp.int32)
    large = jnp.minimum(large, N_BUCKETS - 1)
    return jnp.where(d < max_exact, d, large)


def _bias_by_distance(table, dists):
    return table[_rel_bucket(dists)].astype(jnp.float32).T


def _toeplitz_kernel(u_ref, o_ref):
    t = o_ref.shape[2]
    x = jnp.broadcast_to(u_ref[0], (t, 2 * t))
    o_ref[0, 0] = pltpu.roll(x, 0, 1, stride=1, stride_axis=0)[:, :t]


def _near_tiles(table, t):
    nh = table.shape[1]
    k = jnp.arange(2 * t, dtype=jnp.int32)
    gens = []
    for off in (0, t):
        d = jnp.where(k < t, off - k, off + 2 * t - k)
        gens.append(jnp.where(d[None] >= 0, _bias_by_distance(table, d), NEG))
    u = jnp.stack(gens, axis=1).reshape(nh * 2, 1, 2 * t)
    return pl.pallas_call(
        _toeplitz_kernel,
        grid=(nh, 2),
        in_specs=[pl.BlockSpec((1, 1, 2 * t), lambda h, o: (h * 2 + o, 0, 0))],
        out_specs=pl.BlockSpec((1, 1, t, t), lambda h, o: (h, o, 0, 0)),
        out_shape=jax.ShapeDtypeStruct((nh, 2, t, t), jnp.float32),
        compiler_params=_cparams(("arbitrary", "arbitrary")),
        name="bias_tiles",
    )(u)


def _fold_lanes(x, op):
    out = x[:, :LANES]
    for c in range(1, x.shape[1] // LANES):
        out = op(out, x[:, c * LANES:(c + 1) * LANES])
    return out


def _chunk_loop(n, fn):
    def body(i, carry):
        for u in range(4):
            fn(4 * i + u)
        return carry

    lax.fori_loop(0, n // 4, body, 0)
    base = (n // 4) * 4

    @pl.when(n % 4 >= 2)
    def _():
        fn(base)
        fn(base + 1)

    @pl.when(n % 2 == 1)
    def _():
        fn(n - 1)


def _da_prompt_kernel(cfar_ref, lam_ref, q_ref, k_ref, v_ref, tile_ref, g_ref, o_ref,
                      s_ref, mpart_ref, shift_ref, lpart_ref, acc_ref, *, tq, out_scale):
    h = pl.program_id(1)
    qi = pl.program_id(2)
    q = q_ref[...]
    lane = lax.broadcasted_iota(jnp.int32, q.shape, 1)
    zero = jnp.zeros_like(q)
    q2 = jnp.concatenate([jnp.where(lane < DA_DK, q, zero), jnp.where(lane >= DA_DK, q, zero)], axis=0)
    cfar = cfar_ref[h]
    n_far = jnp.maximum(qi - 1, 0)
    r2 = 2 * tq

    def chunk_rows(kc):
        return pl.ds(pl.multiple_of(kc * tq, tq), tq)

    def scores(kc, bias):
        s = _dot_nt(q2, k_ref[chunk_rows(kc), :])
        if bias is not None:
            s = s + jnp.concatenate([bias, bias], axis=0)
        s_ref[kc] = s
        mpart_ref[...] = jnp.maximum(mpart_ref[...], _fold_lanes(s, jnp.maximum))

    mpart_ref[...] = jnp.full((r2, LANES), NEG, jnp.float32)
    _chunk_loop(n_far, lambda kc: scores(kc, None))
    m_far = jnp.max(mpart_ref[...], axis=-1, keepdims=True) + cfar
    mpart_ref[...] = jnp.full((r2, LANES), NEG, jnp.float32)

    @pl.when(qi >= 1)
    def _():
        scores(qi - 1, tile_ref[0, 1])
        scores(qi, tile_ref[0, 0])

    @pl.when(qi == 0)
    def _():
        scores(0, tile_ref[0, 0])

    m = jnp.maximum(m_far, jnp.max(mpart_ref[...], axis=-1, keepdims=True))
    shift_ref[0] = jnp.broadcast_to(m - cfar, (r2, LANES))
    shift_ref[1] = jnp.broadcast_to(m, (r2, LANES))

    lpart_ref[...] = jnp.zeros((r2, LANES), jnp.float32)
    acc_ref[...] = jnp.zeros((r2, DA_DV), jnp.float32)

    def weights(kc, which):
        s = s_ref[kc]
        sh = shift_ref[which]
        ps = [jnp.exp(s[:, c * LANES:(c + 1) * LANES] - sh) for c in range(tq // LANES)]
        tot = ps[0]
        for pc in ps[1:]:
            tot = tot + pc
        lpart_ref[...] = lpart_ref[...] + tot
        p = jnp.concatenate(ps, axis=1).astype(BF16)
        acc_ref[...] = acc_ref[...] + _dot(p, v_ref[chunk_rows(kc), :])

    _chunk_loop(n_far, lambda kc: weights(kc, 0))

    @pl.when(qi >= 1)
    def _():
        weights(qi - 1, 1)
        weights(qi, 1)

    @pl.when(qi == 0)
    def _():
        weights(0, 1)

    lam = lam_ref[0]
    a = acc_ref[...] / jnp.sum(lpart_ref[...], axis=-1, keepdims=True)
    o = a[:tq] - lam * a[tq:]
    o = o * lax.rsqrt(jnp.mean(o * o, axis=-1, keepdims=True) + EPS) * g_ref[...]
    o_ref[...] = (o * out_scale).astype(o_ref.dtype)


def _da_prompt(q, k, v, tiles, cfar, lam, g_subln, nb, t, tq, out_scale):
    n = q.shape[0]
    nq = t // tq
    grid_spec = pltpu.PrefetchScalarGridSpec(
        num_scalar_prefetch=0,
        grid=(nb, DA_HEADS, nq),
        in_specs=[pl.BlockSpec(memory_space=pltpu.SMEM),
                  pl.BlockSpec(memory_space=pltpu.SMEM),
                  pl.BlockSpec((tq, LANES), lambda b, h, i: (b * nq + i, h)),
                  pl.BlockSpec((t, LANES), lambda b, h, i: (b, h)),
                  pl.BlockSpec((t, LANES), lambda b, h, i: (b, h)),
                  pl.BlockSpec((1, 2, tq, tq), lambda b, h, i: (h, 0, 0, 0)),
                  pl.BlockSpec((1, DA_DV), lambda b, h, i: (0, 0))],
        out_specs=pl.BlockSpec((tq, LANES), lambda b, h, i: (b * nq + i, h)),
        scratch_shapes=[pltpu.VMEM((nq, 2 * tq, tq), jnp.float32),
                        pltpu.VMEM((2 * tq, LANES), jnp.float32),
                        pltpu.VMEM((2, 2 * tq, LANES), jnp.float32),
                        pltpu.VMEM((2 * tq, LANES), jnp.float32),
                        pltpu.VMEM((2 * tq, DA_DV), jnp.float32)])
    return pl.pallas_call(
        functools.partial(_da_prompt_kernel, tq=tq, out_scale=out_scale),
        grid_spec=grid_spec,
        out_shape=jax.ShapeDtypeStruct((n, DA_HEADS * DA_DV), BF16),
        compiler_params=_cparams(("arbitrary", "arbitrary", "arbitrary")),
        name="da_prompt",
    )(cfar, lam, q, k, v, tiles, g_subln.reshape(1, DA_DV))


def _sa_prompt_kernel(cfar_ref, qs_ref, qx_ref, mq_ref, mk_ref, ks_ref, vs_ref, tile_ref, o_ref,
                      k2_ref, key_ref, hi_ref, lo_ref, cut_ref, s_ref, mpart_ref, shift_ref, lpart_ref,
                      acc_ref, *, tq, topk):
    qi = pl.program_id(1)
    n_chunks = qi + 1
    t = mk_ref.shape[0]

    @pl.when(qi == 0)
    def _():
        kix = mk_ref[:, :LANES].astype(jnp.float32)
        lane = lax.broadcasted_iota(jnp.int32, kix.shape, 1)
        k2_ref[0] = jnp.where(lane < IDX_DK, kix, 0.0).astype(BF16)
        k2_ref[1] = jnp.where(lane >= IDX_DK, pltpu.roll(kix, IDX_DK, axis=1), 0.0).astype(BF16)

    wix = mq_ref[:, IDX_DK:IDX_DK + IDX_HEADS]
    wcols = [wix[:, hh:hh + 1] for hh in range(IDX_HEADS)]
    row = lax.broadcasted_iota(jnp.int32, (tq, tq), 0)
    col = lax.broadcasted_iota(jnp.int32, (tq, tq), 1)

    def score_body(kc, carry):
        rows = pl.ds(pl.multiple_of(kc * tq, tq), tq)
        ke = k2_ref[0, rows, :]
        ko = k2_ref[1, rows, :]
        sc = jnp.zeros((tq, tq), jnp.float32)
        for p in range(IDX_HEADS // 2):
            qp = qx_ref[:, p * LANES:(p + 1) * LANES]
            sc = sc + wcols[2 * p] * jnp.maximum(_dot_nt(qp, ke), 0.0)
            sc = sc + wcols[2 * p + 1] * jnp.maximum(_dot_nt(qp, ko), 0.0)
        sc = sc * (IDX_DK ** -0.5 * IDX_HEADS ** -0.5)
        sc = jnp.where((kc < qi) | (row >= col), sc, -jnp.inf)
        key = _float_key(sc)
        key_ref[kc] = key
        hi_ref[kc] = (key >> 16).astype(jnp.int16)
        return carry

    lax.fori_loop(0, n_chunks, score_body, 0)

    i16 = jnp.int16
    i16_min = -(2 ** 15)
    one_i = jnp.ones((tq, tq), i16)
    zero_i = jnp.zeros((tq, tq), i16)
    ones_col = jnp.ones((tq, LANES), BF16)

    def wide16(x):
        return jnp.concatenate([x] * (tq // LANES), axis=1).astype(i16)

    def count_ge16(ref16, cand):
        c16 = wide16(cand)

        def body(kc, acc):
            return acc + jnp.where(ref16[kc] >= c16, one_i, zero_i)
        acc = lax.fori_loop(0, n_chunks, body, zero_i)
        return _dot(acc.astype(jnp.float32).astype(BF16), ones_col)

    def search16(ref16, base, need):
        def bit_body(it, carry):
            v, cnt_v = carry
            cand = v + (jnp.int32(1) << (15 - it))
            cnt = base + count_ge16(ref16, cand)
            ok = cnt >= need
            return jnp.where(ok, cand, v), jnp.where(ok, cnt, cnt_v)
        v0 = jnp.full((tq, LANES), i16_min, jnp.int32)
        c0 = jnp.full((tq, LANES), 3.0e38, jnp.float32)
        return lax.fori_loop(0, 16, bit_body, (v0, c0))

    zero_cnt = jnp.zeros((tq, LANES), jnp.float32)
    t_hi, _ = search16(hi_ref, zero_cnt, float(topk))
    n_above = jnp.where(t_hi < 2 ** 15 - 1, count_ge16(hi_ref, jnp.minimum(t_hi + 1, 2 ** 15 - 1)), 0.0)
    t_hi16 = wide16(t_hi)

    def lower_body(kc, carry):
        lo = ((key_ref[kc] & jnp.int32(0xFFFF)) - 2 ** 15).astype(i16)
        lo_ref[kc] = jnp.where(hi_ref[kc] == t_hi16, lo, jnp.full((tq, tq), i16_min, i16))
        return carry

    lax.fori_loop(0, n_chunks, lower_body, 0)
    t_lo, cnt_w = search16(lo_ref, n_above, float(topk))
    thr = ((t_hi << 16) + (t_lo + 2 ** 15))[:, :1]
    cnt_thr = cnt_w[:, :1]
    tied = (cnt_thr > float(topk)) & (thr > jnp.int32(KEY_NEG_INF))
    need_tie = jnp.max(jnp.where(tied, 1.0, 0.0)) > 0.0
    thr = jnp.maximum(thr, jnp.int32(KEY_NEG_INF + 1))

    cut_ref[...] = jnp.full((tq, 1), 2 ** 30, jnp.int32)

    @pl.when(need_tie)
    def _():
        def gt_body(kc, acc):
            g = jnp.where(key_ref[kc] > thr, 1.0, 0.0)
            return acc + jnp.sum(g, axis=-1, keepdims=True)
        n_gt = lax.fori_loop(0, n_chunks, gt_body, jnp.zeros((tq, 1), jnp.float32))
        need = float(topk) - n_gt
        n_bits = max(1, int(t).bit_length())

        def cut_body(it, cut):
            cand = cut + (jnp.int32(1) << (n_bits - 1 - it))

            def eq_body(kc, acc):
                pos = kc * tq + col
                e = jnp.where((key_ref[kc] == thr) & (pos < cand), 1.0, 0.0)
                return acc + jnp.sum(e, axis=-1, keepdims=True)
            n_eq = lax.fori_loop(0, n_chunks, eq_body, jnp.zeros((tq, 1), jnp.float32))
            return jnp.where(n_eq <= need, cand, cut)
        cut_ref[...] = lax.fori_loop(0, n_bits, cut_body, jnp.zeros((tq, 1), jnp.int32))

    cut = cut_ref[...]

    n_far = jnp.maximum(qi - 1, 0)
    rg = SA_GROUP * tq

    def chunk_rows(kc):
        return pl.ds(pl.multiple_of(kc * tq, tq), tq)

    for g in range(SA_KV_HEADS):
        heads = [g * SA_GROUP + j for j in range(SA_GROUP)]
        qg = jnp.concatenate([qs_ref[:, hh * LANES:(hh + 1) * LANES] for hh in heads], axis=0)
        cf_rows = jnp.concatenate([jnp.full((tq, 1), cfar_ref[hh], jnp.float32) for hh in heads], axis=0)

        def scores(kc, kind, qg=qg, heads=heads, g=g):
            key = key_ref[kc]
            sel = (key > thr) | ((key == thr) & (kc * tq + col < cut))
            s_all = _dot_nt(qg, ks_ref[chunk_rows(kc), g * SA_DH:(g + 1) * SA_DH])
            parts = []
            for j, hh in enumerate(heads):
                s = s_all[j * tq:(j + 1) * tq]
                if kind is not None:
                    s = s + tile_ref[hh, kind]
                parts.append(jnp.where(sel, s, NEG))
            s = jnp.concatenate(parts, axis=0)
            s_ref[kc] = s
            mpart_ref[...] = jnp.maximum(mpart_ref[...], _fold_lanes(s, jnp.maximum))

        mpart_ref[...] = jnp.full((rg, LANES), NEG, jnp.float32)
        _chunk_loop(n_far, lambda kc, f=scores: f(kc, None))
        m_far = jnp.max(mpart_ref[...], axis=-1, keepdims=True) + cf_rows
        mpart_ref[...] = jnp.full((rg, LANES), NEG, jnp.float32)

        @pl.when(qi >= 1)
        def _(f=scores):
            f(qi - 1, 1)
            f(qi, 0)

        @pl.when(qi == 0)
        def _(f=scores):
            f(0, 0)

        m = jnp.maximum(m_far, jnp.max(mpart_ref[...], axis=-1, keepdims=True))
        shift_ref[0] = jnp.broadcast_to(m - cf_rows, (rg, LANES))
        shift_ref[1] = jnp.broadcast_to(m, (rg, LANES))
        lpart_ref[...] = jnp.zeros((rg, LANES), jnp.float32)
        acc_ref[...] = jnp.zeros((rg, SA_DH), jnp.float32)

        def weights(kc, which, g=g):
            s = s_ref[kc]
            sh = shift_ref[which]
            ps = [jnp.exp(s[:, c * LANES:(c + 1) * LANES] - sh) for c in range(tq // LANES)]
            tot = ps[0]
            for pc in ps[1:]:
                tot = tot + pc
            lpart_ref[...] = lpart_ref[...] + tot
            p = jnp.concatenate(ps, axis=1).astype(BF16)
            acc_ref[...] = acc_ref[...] + _dot(p, vs_ref[chunk_rows(kc), g * SA_DH:(g + 1) * SA_DH])

        _chunk_loop(n_far, lambda kc, f=weights: f(kc, 0))

        @pl.when(qi >= 1)
        def _(f=weights):
            f(qi - 1, 1)
            f(qi, 1)

        @pl.when(qi == 0)
        def _(f=weights):
            f(0, 1)

        a = acc_ref[...] / jnp.sum(lpart_ref[...], axis=-1, keepdims=True)
        for j, hh in enumerate(heads):
            o_ref[:, hh * SA_DH:(hh + 1) * SA_DH] = a[j * tq:(j + 1) * tq].astype(o_ref.dtype)


def _sa_prompt(q_sa, q_ix, misc_q, misc_k, k_sa, v_sa, tiles, cfar, nb, t, tq, topk):
    n = q_sa.shape[0]
    nq = t // tq
    mw = misc_q.shape[1]
    kvw = SA_KV_HEADS * SA_DH
    once = dict(pipeline_mode=pl.Buffered(1))
    rg = SA_GROUP * tq
    return pl.pallas_call(
        functools.partial(_sa_prompt_kernel, tq=tq, topk=topk),
        grid=(nb, nq),
        in_specs=[pl.BlockSpec(memory_space=pltpu.SMEM),
                  pl.BlockSpec((tq, SA_HEADS * SA_DH), lambda b, i: (b * nq + i, 0)),
                  pl.BlockSpec((tq, IDX_HEADS * IDX_DK), lambda b, i: (b * nq + i, 0)),
                  pl.BlockSpec((tq, mw), lambda b, i: (b * nq + i, 0)),
                  pl.BlockSpec((t, mw), lambda b, i: (b, 0), **once),
                  pl.BlockSpec((t, kvw), lambda b, i: (b, 0), **once),
                  pl.BlockSpec((t, kvw), lambda b, i: (b, 0), **once),
                  pl.BlockSpec((SA_HEADS, 2, tq, tq), lambda b, i: (0, 0, 0, 0), **once)],
        out_specs=pl.BlockSpec((tq, SA_HEADS * SA_DH), lambda b, i: (b * nq + i, 0)),
        out_shape=jax.ShapeDtypeStruct((n, SA_HEADS * SA_DH), BF16),
        scratch_shapes=[pltpu.VMEM((2, t, LANES), BF16),
                        pltpu.VMEM((nq, tq, tq), jnp.int32),
                        pltpu.VMEM((nq, tq, tq), jnp.int16),
                        pltpu.VMEM((nq, tq, tq), jnp.int16),
                        pltpu.VMEM((tq, 1), jnp.int32),
                        pltpu.VMEM((nq, rg, tq), jnp.float32),
                        pltpu.VMEM((rg, LANES), jnp.float32),
                        pltpu.VMEM((2, rg, LANES), jnp.float32),
                        pltpu.VMEM((rg, LANES), jnp.float32),
                        pltpu.VMEM((rg, SA_DH), jnp.float32)],
        compiler_params=_cparams(("arbitrary", "arbitrary")),
        name="sa_prompt",
    )(cfar, q_sa, q_ix, misc_q, misc_k, k_sa, v_sa, tiles)


def _proj_gate_kernel(oda_ref, osa_ref, wpd_ref, wps_ref, gda_ref, gsa_ref, o_ref, wbf_ref):
    @pl.when(pl.program_id(1) == 0)
    def _():
        wbf_ref[0] = wpd_ref[...].astype(BF16)
        wbf_ref[1] = wps_ref[...].astype(BF16)

    a = _dot(oda_ref[...], wbf_ref[0])
    b = _dot(osa_ref[...], wbf_ref[1])
    o_ref[...] = (gda_ref[...].astype(jnp.float32) * a
                  + gsa_ref[...].astype(jnp.float32) * b).astype(o_ref.dtype)


def _proj_gate(o_da, o_sa, w_pd, w_ps, gates, d, tm=1024, tn=512):
    n, kd = o_da.shape
    ks = o_sa.shape[1]
    tn = min(tn, d)
    tm = min(tm, n)
    nj = d // tn
    return pl.pallas_call(
        _proj_gate_kernel,
        grid=(nj, n // tm),
        in_specs=[pl.BlockSpec((tm, kd), lambda j, i: (i, 0)),
                  pl.BlockSpec((tm, ks), lambda j, i: (i, 0)),
                  pl.BlockSpec((kd, tn), lambda j, i: (0, j)),
                  pl.BlockSpec((ks, tn), lambda j, i: (0, j)),
                  pl.BlockSpec((tm, tn), lambda j, i: (i, j)),
                  pl.BlockSpec((tm, tn), lambda j, i: (i, nj + j))],
        out_specs=pl.BlockSpec((tm, tn), lambda j, i: (i, j)),
        out_shape=jax.ShapeDtypeStruct((n, d), BF16),
        scratch_shapes=[pltpu.VMEM((2, kd, tn), BF16)],
        compiler_params=_cparams(("arbitrary", "arbitrary")),
        name="proj_gate",
    )(o_da, o_sa, w_pd, w_ps, gates, gates)


def _mm_resid_kernel(m_ref, w_ref, x_ref, gt_ref, o_ref, wbf_ref):
    @pl.when(pl.program_id(1) == 0)
    def _():
        wbf_ref[...] = w_ref[...].astype(BF16)

    o_ref[...] = x_ref[...] + gt_ref[0] * _dot(m_ref[...], wbf_ref[...])


def _mm_resid(m, w, x, gt, rows_per_batch, tm=1024, tn=1024):
    n, k = m.shape
    d = w.shape[1]
    tn = min(tn, d)
    tm = min(tm, n)
    per = rows_per_batch // tm
    return pl.pallas_call(
        _mm_resid_kernel,
        grid=(d // tn, n // tm),
        in_specs=[pl.BlockSpec((tm, k), lambda j, i: (i, 0)),
                  pl.BlockSpec((k, tn), lambda j, i: (0, j)),
                  pl.BlockSpec((tm, tn), lambda j, i: (i, j)),
                  pl.BlockSpec((1, 1, tn), lambda j, i: (i // per, 0, j))],
        out_specs=pl.BlockSpec((tm, tn), lambda j, i: (i, j)),
        out_shape=jax.ShapeDtypeStruct((n, d), jnp.float32),
        scratch_shapes=[pltpu.VMEM((k, tn), BF16)],
        compiler_params=_cparams(("arbitrary", "arbitrary")),
        name="mm_resid",
    )(m, w, x, gt[:, None, :])


def _norm_router_kernel(x_ref, g_ref, sc_ref, sh_ref, wr_ref, th_ref, tl_ref, h_ref, lg_ref):
    last = pl.num_programs(0) - 1

    @pl.when(pl.program_id(0) < last)
    def _():
        x = x_ref[...]
        y = x * lax.rsqrt(jnp.mean(x * x, axis=-1, keepdims=True) + EPS) * g_ref[...]
        h = y * (1.0 + sc_ref[0]) + sh_ref[0]
        h_ref[...] = h.astype(h_ref.dtype)
        lg_ref[...] = _dot_nt(_bf(wr_ref[...]), _bf(h))

    @pl.when(pl.program_id(0) == last)
    def _():
        h_ref[...] = th_ref[...]
        lg_ref[...] = tl_ref[...]


def _norm_router(x, g, sc, sh, w_router_t, tail_h, tail_lg, rows_per_batch, tm):
    n, d = x.shape
    per = rows_per_batch // tm
    nt = n // tm
    assert tail_h.shape == (tm, d) and tail_lg.shape == (N_EXPERTS, tm)
    row = lambda i: jnp.minimum(i, nt - 1)
    return pl.pallas_call(
        _norm_router_kernel,
        grid=(nt + 1,),
        in_specs=[pl.BlockSpec((tm, d), lambda i: (row(i), 0)),
                  pl.BlockSpec((1, d), lambda i: (0, 0)),
                  pl.BlockSpec((1, 1, d), lambda i: (row(i) // per, 0, 0)),
                  pl.BlockSpec((1, 1, d), lambda i: (row(i) // per, 0, 0)),
                  pl.BlockSpec((N_EXPERTS, d), lambda i: (0, 0)),
                  pl.BlockSpec((tm, d), lambda i: (0, 0)),
                  pl.BlockSpec((N_EXPERTS, tm), lambda i: (0, 0))],
        out_specs=[pl.BlockSpec((tm, d), lambda i: (i, 0)),
                   pl.BlockSpec((N_EXPERTS, tm), lambda i: (0, i))],
        out_shape=[jax.ShapeDtypeStruct((n + tm, d), BF16),
                   jax.ShapeDtypeStruct((N_EXPERTS, n + tm), jnp.float32)],
        compiler_params=_cparams(("arbitrary",)),
        name="norm_router",
    )(x, g.reshape(1, d), sc[:, None, :], sh[:, None, :], w_router_t, tail_h, tail_lg)


def _route_kernel(lg_ref, b_ref, eidx_ref, wts_ref, rank_ref, cnt_ref, *, tn):
    shape = (N_GROUPS, GROUP_SIZE, tn)
    sc = _sigmoid(lg_ref[...])
    biased = sc + b_ref[...]
    e_iota = lax.broadcasted_iota(jnp.int32, shape, 1)
    g_iota3 = lax.broadcasted_iota(jnp.int32, shape, 0)
    flat_iota = g_iota3 * GROUP_SIZE + e_iota
    g_iota = lax.broadcasted_iota(jnp.int32, (N_GROUPS, 1, tn), 0)
    ninf = -jnp.inf

    m1 = jnp.max(biased, axis=1, keepdims=True)
    first = jnp.min(jnp.where(biased == m1, e_iota, GROUP_SIZE), axis=1, keepdims=True)
    m2 = jnp.max(jnp.where(e_iota == first, ninf, biased), axis=1, keepdims=True)
    cur = m1 + m2
    gsel = jnp.zeros((N_GROUPS, 1, tn), jnp.float32)
    for _ in range(TOPK_GROUPS):
        mx = jnp.max(cur, axis=0, keepdims=True)
        idx = jnp.min(jnp.where(cur == mx, g_iota, N_GROUPS), axis=0, keepdims=True)
        hit = g_iota == idx
        gsel = jnp.where(hit, 1.0, gsel)
        cur = jnp.where(hit, ninf, cur)

    cur = jnp.where(gsel > 0.0, biased, ninf)
    hits, ws = [], []
    for k in range(TOP_K):
        mx = jnp.max(jnp.max(cur, axis=1, keepdims=True), axis=0, keepdims=True)
        cand = jnp.where(cur == mx, flat_iota, N_EXPERTS)
        idx = jnp.min(jnp.min(cand, axis=1, keepdims=True), axis=0, keepdims=True)
        hit = flat_iota == idx
        w = jnp.sum(jnp.sum(jnp.where(hit, sc, 0.0), axis=1, keepdims=True), axis=0, keepdims=True)
        eidx_ref[k:k + 1, :] = idx.reshape(1, tn)
        hits.append(hit)
        ws.append(w)
        cur = jnp.where(hit, ninf, cur)
    wsum = ws[0]
    for w in ws[1:]:
        wsum = wsum + w
    for k in range(TOP_K):
        wts_ref[k:k + 1, :] = (ws[k] / wsum * ROUTED_SCALE).reshape(1, tn)

    member = jnp.zeros(shape, jnp.float32)
    for hit in hits:
        member = jnp.where(hit, 1.0, member)
    member2 = member.reshape(N_EXPERTS, tn)
    r = lax.broadcasted_iota(jnp.int32, (tn, tn), 0)
    c = lax.broadcasted_iota(jnp.int32, (tn, tn), 1)
    upper = jnp.where(r < c, 1.0, 0.0).astype(BF16)
    prefix = _dot(member2.astype(BF16), upper).reshape(shape)
    for k in range(TOP_K):
        rk = jnp.sum(jnp.sum(jnp.where(hits[k], prefix, 0.0), axis=1, keepdims=True), axis=0, keepdims=True)
        rank_ref[k:k + 1, :] = rk.reshape(1, tn)
    cnt = jnp.sum(member2, axis=1, keepdims=True)
    cnt_ref[0] = jnp.broadcast_to(cnt, (N_EXPERTS, LANES))


def _route(logits_t, b_router, tn):
    n_pad = logits_t.shape[1]
    nt = n_pad // tn
    lg3 = logits_t.reshape(N_GROUPS, GROUP_SIZE, n_pad)
    b3 = b_router.astype(jnp.float32).reshape(N_GROUPS, GROUP_SIZE, 1)
    row = lambda dt: jax.ShapeDtypeStruct((TOP_K, n_pad), dt)
    return pl.pallas_call(
        functools.partial(_route_kernel, tn=tn),
        grid=(nt,),
        in_specs=[pl.BlockSpec((N_GROUPS, GROUP_SIZE, tn), lambda i: (0, 0, i)),
                  pl.BlockSpec((N_GROUPS, GROUP_SIZE, 1), lambda i: (0, 0, 0))],
        out_specs=[pl.BlockSpec((TOP_K, tn), lambda i: (0, i)),
                   pl.BlockSpec((TOP_K, tn), lambda i: (0, i)),
                   pl.BlockSpec((TOP_K, tn), lambda i: (0, i)),
                   pl.BlockSpec((1, N_EXPERTS, LANES), lambda i: (i, 0, 0))],
        out_shape=[row(jnp.int32), row(jnp.float32), row(jnp.float32),
                   jax.ShapeDtypeStruct((nt, N_EXPERTS, LANES), jnp.float32)],
        compiler_params=_cparams(("arbitrary",)),
        name="route",
    )(lg3, b3)


def _dest_kernel(eidx_ref, rank_ref, base_ref, o_ref, *, tn):
    e_iota = lax.broadcasted_iota(jnp.int32, (N_EXPERTS, tn), 0)
    base = base_ref[0][:, :1]
    for k in range(TOP_K):
        onehot = e_iota == eidx_ref[k:k + 1, :]
        b = jnp.sum(jnp.where(onehot, base, 0.0), axis=0, keepdims=True)
        o_ref[k:k + 1, :] = (b + rank_ref[k:k + 1, :]).astype(jnp.int32)


def _dest(eidx, rank, base, tn):
    n_pad = eidx.shape[1]
    return pl.pallas_call(
        functools.partial(_dest_kernel, tn=tn),
        grid=(n_pad // tn,),
        in_specs=[pl.BlockSpec((TOP_K, tn), lambda i: (0, i)),
                  pl.BlockSpec((TOP_K, tn), lambda i: (0, i)),
                  pl.BlockSpec((1, N_EXPERTS, LANES), lambda i: (i, 0, 0))],
        out_specs=pl.BlockSpec((TOP_K, tn), lambda i: (0, i)),
        out_shape=jax.ShapeDtypeStruct((TOP_K, n_pad), jnp.int32),
        compiler_params=_cparams(("arbitrary",)),
        name="dest",
    )(eidx, rank, base)


def _pack_words(lo_f32, hi_f32):
    lo = lax.shift_right_logical(pltpu.bitcast(lo_f32, jnp.uint32), jnp.uint32(16))
    hi = pltpu.bitcast(hi_f32, jnp.uint32) & jnp.uint32(0xFFFF0000)
    return hi | lo


def _unpack_words(w):
    lo = pltpu.bitcast(lax.shift_left(w, jnp.uint32(16)), jnp.float32)
    hi = pltpu.bitcast(w & jnp.uint32(0xFFFF0000), jnp.float32)
    return lo, hi


def _bf16_exact(x):
    return x.astype(BF16).astype(jnp.float32)


def _dispatch_kernel(pstart_ref, pcnt_ref, h_ref, dest_hbm, xs_hbm, dsm, pk, zrow, sem_d, sem_r,
                     *, tn, nt):
    i = pl.program_id(0)
    half = pk.shape[1]

    @pl.when(i < nt)
    def _():
        cp = pltpu.make_async_copy(dest_hbm.at[pl.ds(i * (TOP_K * tn), TOP_K * tn)], dsm, sem_d)
        cp.start()
        x = h_ref[...]
        pk[...] = _pack_words(x[:, :half].astype(jnp.float32), x[:, half:].astype(jnp.float32))
        cp.wait()

        def body(r, carry):
            for k in range(TOP_K):
                d = dsm[k * tn + r]
                pltpu.make_async_copy(pk.at[pl.ds(r, 1), :], xs_hbm.at[pl.ds(d, 1), :],
                                      sem_r).start(priority=k % 2)
            return carry

        lax.fori_loop(0, tn, body, 0)
        for k in range(TOP_K):
            pltpu.make_async_copy(pk, xs_hbm.at[pl.ds(0, tn), :], sem_r).wait()

    @pl.when(i == nt)
    def _():
        zrow[...] = jnp.zeros(zrow.shape, zrow.dtype)

        def per_expert(e, carry):
            s0 = pstart_ref[e]
            c = pcnt_ref[e]
            head = jnp.minimum(c, (8 - s0 % 8) % 8)
            g0 = s0 + head
            ngrp = (c - head) // 8
            grp = lambda r: pl.ds(pl.multiple_of(g0 + r * 8, 8), 8)

            def start(r, cc):
                pltpu.make_async_copy(zrow.at[pl.ds(0, 1), :], xs_hbm.at[pl.ds(s0 + r, 1), :], sem_r).start()
                return cc

            def wait(r, cc):
                pltpu.make_async_copy(zrow.at[pl.ds(0, 1), :], xs_hbm.at[pl.ds(s0, 1), :], sem_r).wait()
                return cc

            def gstart(r, cc):
                pltpu.make_async_copy(zrow, xs_hbm.at[grp(r), :], sem_r).start()
                return cc

            def gwait(r, cc):
                pltpu.make_async_copy(zrow, xs_hbm.at[grp(0), :], sem_r).wait()
                return cc

            lax.fori_loop(0, head, start, 0)
            lax.fori_loop(0, ngrp, gstart, 0)
            lax.fori_loop(0, head, wait, 0)
            lax.fori_loop(0, ngrp, gwait, 0)
            return carry

        lax.fori_loop(0, N_EXPERTS, per_expert, 0)

        t0 = pstart_ref[N_EXPERTS]
        groups = pcnt_ref[N_EXPERTS]
        rows8 = lambda r: pl.ds(pl.multiple_of(t0 + r * 8, 8), 8)

        def tstart(r, cc):
            pltpu.make_async_copy(zrow, xs_hbm.at[rows8(r), :], sem_r).start()
            return cc

        def twait(r, cc):
            pltpu.make_async_copy(zrow, xs_hbm.at[rows8(0), :], sem_r).wait()
            return cc

        lax.fori_loop(0, groups, tstart, 0)
        lax.fori_loop(0, groups, twait, 0)


def _dispatch(h2, dest_flat, pad_start, pad_cnt, n_rows, tn):
    n_pad, d = h2.shape
    nt = n_pad // tn
    grid_spec = pltpu.PrefetchScalarGridSpec(
        num_scalar_prefetch=2,
        grid=(nt + 1,),
        in_specs=[pl.BlockSpec((tn, d), lambda i, a, b: (jnp.minimum(i, nt - 1), 0)),
                  pl.BlockSpec(memory_space=pl.ANY)],
        out_specs=pl.BlockSpec(memory_space=pl.ANY),
        scratch_shapes=[pltpu.SMEM((TOP_K * tn,), jnp.int32),
                        pltpu.VMEM((tn, d // 2), jnp.uint32),
                        pltpu.VMEM((8, d // 2), jnp.uint32),
                        pltpu.SemaphoreType.DMA(()),
                        pltpu.SemaphoreType.DMA(())])
    return pl.pallas_call(
        functools.partial(_dispatch_kernel, tn=tn, nt=nt),
        grid_spec=grid_spec,
        out_shape=jax.ShapeDtypeStruct((n_rows, d // 2), jnp.uint32),
        compiler_params=_cparams(("arbitrary",)),
        name="dispatch",
    )(pad_start, pad_cnt, h2, dest_flat)


def _expert_kernel(blk_e_ref, nused_ref, next_e_ref, slot_ref, x_ref, wg_hbm, wu_hbm, wd_hbm, y_ref,
                   wg_f, wu_f, wd_f, wgb, wub, wdb, sems):
    i = pl.program_id(0)
    nused = nused_ref[0]
    ii = jnp.minimum(i, nused - 1)
    e = blk_e_ref[ii]
    e_prev = blk_e_ref[jnp.maximum(ii - 1, 0)]
    half = x_ref.shape[1]

    def copies(ex, sl):
        return [pltpu.make_async_copy(src.at[ex], dst.at[sl], sems.at[sl, t])
                for t, (src, dst) in enumerate(((wg_hbm, wg_f), (wu_hbm, wu_f), (wd_hbm, wd_f)))]

    @pl.when(i == 0)
    def _():
        for c in copies(e, slot_ref[e]):
            c.start()

    @pl.when((i < nused) & ((i == 0) | (e != e_prev)))
    def _():
        sl = slot_ref[e]
        for c in copies(e, sl):
            c.wait()
        nxt = next_e_ref[e]

        @pl.when(nxt >= 0)
        def _():
            for c in copies(nxt, 1 - sl):
                c.start()

        wgb[...] = wg_f[sl].astype(BF16)
        wub[...] = wu_f[sl].astype(BF16)
        wdb[...] = wd_f[sl].astype(BF16)

    @pl.when(i < nused)
    def _():
        lo, hi = _unpack_words(x_ref[...])
        xl = lo.astype(BF16)
        xh = hi.astype(BF16)
        g = _dot(xl, wgb[:half, :]) + _dot(xh, wgb[half:, :])
        u = _dot(xl, wub[:half, :]) + _dot(xh, wub[half:, :])
        hmid = (_silu(g) * u).astype(BF16)
        y = _dot(hmid, wdb[...])
        y_ref[...] = _pack_words(_bf16_exact(y[:, :half]), _bf16_exact(y[:, half:]))

    @pl.when(i >= nused)
    def _():
        y_ref[...] = jnp.zeros(y_ref.shape, y_ref.dtype)


def _experts(xs, blk_e, nused, next_e, slot, w_gate, w_up, w_down, tb):
    n_rows, half = xs.shape
    _, d, f = w_gate.shape
    nblk = n_rows // tb

    def xmap(i, be, nu, ne, sl):
        return (jnp.minimum(i, nu[0] - 1), 0)

    grid_spec = pltpu.PrefetchScalarGridSpec(
        num_scalar_prefetch=4,
        grid=(nblk,),
        in_specs=[pl.BlockSpec((tb, half), xmap),
                  pl.BlockSpec(memory_space=pl.ANY),
                  pl.BlockSpec(memory_space=pl.ANY),
                  pl.BlockSpec(memory_space=pl.ANY)],
        out_specs=pl.BlockSpec((tb, half), lambda i, be, nu, ne, sl: (i, 0)),
        scratch_shapes=[pltpu.VMEM((2, d, f), jnp.float32),
                        pltpu.VMEM((2, d, f), jnp.float32),
                        pltpu.VMEM((2, f, d), jnp.float32),
                        pltpu.VMEM((d, f), BF16),
                        pltpu.VMEM((d, f), BF16),
                        pltpu.VMEM((f, d), BF16),
                        pltpu.SemaphoreType.DMA((2, 3))])
    return pl.pallas_call(
        _expert_kernel,
        grid_spec=grid_spec,
        out_shape=jax.ShapeDtypeStruct((n_rows, half), jnp.uint32),
        compiler_params=_cparams(("arbitrary",)),
        name="experts",
    )(blk_e, nused, next_e, slot, xs, w_gate, w_up, w_down)


def _ffn_kernel(h_ref, wg_ref, wu_ref, wd_ref, o_ref, wgb, wub, wdb):
    @pl.when(pl.program_id(0) == 0)
    def _():
        wgb[...] = wg_ref[...].astype(BF16)
        wub[...] = wu_ref[...].astype(BF16)
        wdb[...] = wd_ref[...].astype(BF16)

    x = h_ref[...]
    hmid = (_silu(_dot(x, wgb[...])) * _dot(x, wub[...])).astype(BF16)
    o_ref[...] = _dot(hmid, wdb[...]).astype(o_ref.dtype)


def _ffn_shared(h2, wg, wu, wd, tm):
    n_pad, d = h2.shape
    f = wg.shape[1]
    return pl.pallas_call(
        _ffn_kernel,
        grid=(n_pad // tm,),
        in_specs=[pl.BlockSpec((tm, d), lambda i: (i, 0)),
                  pl.BlockSpec((d, f), lambda i: (0, 0)),
                  pl.BlockSpec((d, f), lambda i: (0, 0)),
                  pl.BlockSpec((f, d), lambda i: (0, 0))],
        out_specs=pl.BlockSpec((tm, d), lambda i: (i, 0)),
        out_shape=jax.ShapeDtypeStruct((n_pad, d), BF16),
        scratch_shapes=[pltpu.VMEM((d, f), BF16),
                        pltpu.VMEM((d, f), BF16),
                        pltpu.VMEM((f, d), BF16)],
        compiler_params=_cparams(("arbitrary",)),
        name="ffn_shared",
    )(h2, wg, wu, wd)


def _combine_kernel(dest_hbm, ys_hbm, wts_ref, sh_ref, x_ref, gt_ref, g_ref, o_ref,
                    dsm, buf, sem_d, sem_r, *, tn, tile0, final_norm):
    i = pl.program_id(0)
    cp = pltpu.make_async_copy(dest_hbm.at[pl.ds((tile0 + i) * (TOP_K * tn), TOP_K * tn)], dsm, sem_d)
    cp.start()
    cp.wait()

    hn = tn // 2
    half = buf.shape[2]
    for h in range(2):
        def body(r, carry, h=h):
            for k in range(TOP_K):
                d = dsm[k * tn + r]
                pltpu.make_async_copy(ys_hbm.at[pl.ds(d, 1), :], buf.at[k, pl.ds(r, 1), :],
                                      sem_r.at[h]).start(priority=k % 2)
            return carry

        lax.fori_loop(h * hn, (h + 1) * hn, body, 0)

    for h in range(2):
        rows = pl.ds(h * hn, hn)
        for k in range(TOP_K):
            pltpu.make_async_copy(ys_hbm.at[pl.ds(0, hn), :], buf.at[k, rows, :], sem_r.at[h]).wait()
        wts = wts_ref[rows, :]
        acc_lo = jnp.zeros((hn, half), jnp.float32)
        acc_hi = jnp.zeros((hn, half), jnp.float32)
        for k in range(TOP_K):
            lo, hi = _unpack_words(buf[k, rows, :])
            wk = wts[:, k:k + 1]
            acc_lo = acc_lo + wk * lo
            acc_hi = acc_hi + wk * hi
        sh = sh_ref[rows, :].astype(jnp.float32)
        gt = gt_ref[0] if gt_ref.shape[1] == 1 else gt_ref[0, rows, :]
        x_lo = x_ref[rows, :half] + gt[:, :half] * (acc_lo + sh[:, :half])
        x_hi = x_ref[rows, half:] + gt[:, half:] * (acc_hi + sh[:, half:])
        if final_norm:
            ms = (jnp.sum(x_lo * x_lo, axis=-1, keepdims=True)
                  + jnp.sum(x_hi * x_hi, axis=-1, keepdims=True)) / (2 * half)
            inv = lax.rsqrt(ms + EPS)
            g = g_ref[...]
            x_lo = x_lo * inv * g[:, :half]
            x_hi = x_hi * inv * g[:, half:]
        o_ref[rows, :half] = x_lo
        o_ref[rows, half:] = x_hi


def _combine(dest_flat, ys, wts_tok, shared, x1, gt3, g_final, tile0, tn, final_norm):
    rows, d = x1.shape
    nt = rows // tn
    gr = gt3.shape[1]
    per = nt // gt3.shape[0]
    return pl.pallas_call(
        functools.partial(_combine_kernel, tn=tn, tile0=tile0, final_norm=final_norm),
        grid=(nt,),
        in_specs=[pl.BlockSpec(memory_space=pl.ANY),
                  pl.BlockSpec(memory_space=pl.ANY),
                  pl.BlockSpec((tn, TOP_K), lambda i: (tile0 + i, 0)),
                  pl.BlockSpec((tn, d), lambda i: (tile0 + i, 0)),
                  pl.BlockSpec((tn, d), lambda i: (i, 0)),
                  pl.BlockSpec((1, gr, d), lambda i: (i // per, 0, 0)),
                  pl.BlockSpec((1, d), lambda i: (0, 0))],
        out_specs=pl.BlockSpec((tn, d), lambda i: (i, 0)),
        out_shape=jax.ShapeDtypeStruct((rows, d), jnp.float32),
        scratch_shapes=[pltpu.SMEM((TOP_K * tn,), jnp.int32),
                        pltpu.VMEM((TOP_K, tn, d // 2), jnp.uint32),
                        pltpu.SemaphoreType.DMA(()),
                        pltpu.SemaphoreType.DMA((2,))],
        compiler_params=_cparams(("arbitrary",)),
        name="combine",
    )(dest_flat, ys, wts_tok, shared, x1, gt3, g_final.reshape(1, d))


def _page_specs(shape, n, pg):
    def mk(u):
        return pl.BlockSpec((1,) + shape, lambda b, j, pt: (pt[b, j * pg + u], 0, 0))
    return [mk(u) for u in range(n)]


def _kv_page_specs(kshape, vshape, pg, ns):
    def mk(shape, first):
        def one(u):
            def index(b, j, pt):
                step = jnp.minimum(j, ns - 1) if first else jnp.maximum(j - ns, 0)
                return (pt[b, step * pg + u], 0, 0)
            return pl.BlockSpec((1,) + shape, index)
        return [one(u) for u in range(pg)]
    return mk(kshape, True) + mk(vshape, False)


def _softmax_pages(s_ref):
    s = s_ref[...]
    m = jnp.max(jnp.max(s, axis=0, keepdims=True), axis=2, keepdims=True)
    e = jnp.exp(s - m)
    return e / jnp.sum(jnp.sum(e, axis=0, keepdims=True), axis=2, keepdims=True)


def _da_sample_kernel(pt_ref, lam_ref, q_ref, kn_ref, vn_ref, bl_ref, cf_ref, b0_ref, ex_ref, hm_ref,
                      *rest, pg, n_pages):
    k_refs, v_refs = rest[:pg], rest[pg:2 * pg]
    o_ref, s_ref, a_ref, acc_ref = rest[2 * pg:]
    j = pl.program_id(1)
    ns = n_pages // pg
    r = q_ref.shape[1]
    nh = r // 2
    lane = lax.broadcasted_iota(jnp.int32, (r, PAGE_SIZE), 1)

    @pl.when(j < ns)
    def _():
        q = q_ref[0]
        qb = _bf(q)
        for u in range(pg):
            page = j * pg + u
            s = _dot(qb, _bf(k_refs[u][0]))
            s_ref[page] = s + jnp.where(page == (n_pages - 1), bl_ref[...], cf_ref[...])

        @pl.when(j == 0)
        def _():
            s_new = jnp.sum(_rounded(q) * _rounded(kn_ref[0]), axis=-1, keepdims=True) + b0_ref[...]
            s_ref[n_pages] = jnp.where(lane == 0, s_new, NEG)

    @pl.when(j == ns)
    def _():
        p = _softmax_pages(s_ref)
        a = p[:, :nh, :] - lam_ref[0] * p[:, nh:, :]
        a_ref[...] = _bf(jnp.concatenate([a, jnp.zeros_like(a)], axis=1))
        a_new = a_ref[n_pages][:, 0:1].astype(jnp.float32)
        acc_ref[...] = a_new * _rounded(vn_ref[0])

    @pl.when(j >= ns)
    def _():
        acc = acc_ref[...]
        for u in range(pg):
            page = (j - ns) * pg + u
            pe = _bf(_dot(a_ref[page], ex_ref[...]) * hm_ref[...])
            acc = acc + _dot(pe, _bf(v_refs[u][0]))
        acc_ref[...] = acc

    @pl.when(j == pl.num_programs(1) - 1)
    def _():
        o_ref[0] = acc_ref[...]


def _da_sample(page_table, lam, qbd, knew, vnew, bias_last, cfar, bias0, kt, v2, pg):
    nb, n_pages = page_table.shape
    r, w = qbd.shape[1:]
    rows_v, dv = v2.shape[1:]
    nh = rows_v // PAGE_SIZE
    ns = n_pages // pg
    col = jnp.arange(rows_v, dtype=jnp.int32)
    expand = (col[None, :] // nh == jnp.arange(PAGE_SIZE, dtype=jnp.int32)[:, None]).astype(BF16)
    head_mask = (col[None, :] % nh == jnp.arange(r, dtype=jnp.int32)[:, None]).astype(jnp.float32)
    full = lambda shp: pl.BlockSpec(shp, lambda b, j, pt: (0,) * len(shp))
    per_b = lambda shp: pl.BlockSpec((1,) + shp, lambda b, j, pt: (b, 0, 0))
    grid_spec = pltpu.PrefetchScalarGridSpec(
        num_scalar_prefetch=1,
        grid=(nb, 2 * ns),
        in_specs=[pl.BlockSpec(memory_space=pltpu.SMEM),
                  per_b((r, w)), per_b((1, w)), per_b((r, dv)),
                  full((r, PAGE_SIZE)), full((r, 1)), full((r, 1)),
                  full((PAGE_SIZE, rows_v)), full((r, rows_v))]
                 + _kv_page_specs((w, PAGE_SIZE), (rows_v, dv), pg, ns),
        out_specs=per_b((r, dv)),
        scratch_shapes=[pltpu.VMEM((n_pages + 1, r, PAGE_SIZE), jnp.float32),
                        pltpu.VMEM((n_pages + 1, r, PAGE_SIZE), BF16),
                        pltpu.VMEM((r, dv), jnp.float32)])
    return pl.pallas_call(
        functools.partial(_da_sample_kernel, pg=pg, n_pages=n_pages),
        grid_spec=grid_spec,
        out_shape=jax.ShapeDtypeStruct((nb, r, dv), jnp.float32),
        compiler_params=_cparams(("arbitrary", "arbitrary")),
        name="da_sample",
    )(page_table, lam, qbd, knew, vnew, bias_last, cfar, bias0, expand, head_mask,
      *([kt] * pg), *([v2] * pg))


def _idx_sample_kernel(pt_ref, q_ref, w_ref, kn_ref, *rest, pg, n_pages, topk):
    k_refs = rest[:pg]
    sel_ref, sc_ref = rest[pg:]
    j = pl.program_id(1)
    q = q_ref[0]
    w = _rounded(w_ref[0])
    scale = IDX_DK ** -0.5 * IDX_HEADS ** -0.5
    rows = sc_ref.shape[0]
    lane = lax.broadcasted_iota(jnp.int32, (1, PAGE_SIZE), 1)

    @pl.when(j == 0)
    def _():
        sc_ref[...] = jnp.full(sc_ref.shape, -jnp.inf, jnp.float32)
        d = jnp.maximum(jnp.sum(_rounded(q) * _rounded(kn_ref[0]), axis=-1, keepdims=True), 0.0)
        s_new = jnp.sum(w * _rounded(d), axis=0, keepdims=True) * scale
        sc_ref[n_pages:n_pages + 1, :] = jnp.where(lane == 0, s_new, -jnp.inf)

    qb = _bf(q)
    for u in range(pg):
        d = _rounded(jnp.maximum(_dot(qb, _bf(k_refs[u][0])), 0.0))
        sc_ref[pl.ds(j * pg + u, 1), :] = jnp.sum(w * d, axis=0, keepdims=True) * scale

    @pl.when(j == pl.num_programs(1) - 1)
    def _():
        key = _float_key(sc_ref[...])
        pos = (lax.broadcasted_iota(jnp.int32, key.shape, 0) * PAGE_SIZE
               + lax.broadcasted_iota(jnp.int32, key.shape, 1))

        def bit_body(it, thr):
            cand = thr + (jnp.int32(1) << (31 - it))
            cnt = jnp.sum(jnp.where(key >= cand, 1.0, 0.0))
            return jnp.where(cnt >= float(topk), cand, thr)

        thr = lax.fori_loop(0, 32, bit_body, jnp.int32(INT_MIN))
        thr = jnp.maximum(thr, jnp.int32(KEY_NEG_INF + 1))
        need = float(topk) - jnp.sum(jnp.where(key > thr, 1.0, 0.0))
        n_bits = max(1, int(rows * PAGE_SIZE).bit_length())

        def cut_body(it, cut):
            cand = cut + (jnp.int32(1) << (n_bits - 1 - it))
            n_eq = jnp.sum(jnp.where((key == thr) & (pos < cand), 1.0, 0.0))
            return jnp.where(n_eq <= need, cand, cut)

        cut = lax.fori_loop(0, n_bits, cut_body, jnp.int32(0))
        sel_ref[0] = jnp.where((key > thr) | ((key == thr) & (pos < cut)), 1.0, 0.0)


def _idx_sample(page_table, qix, wix, knew, kc, pg, topk):
    nb, n_pages = page_table.shape
    rows = -(-(n_pages + 1) // 8) * 8
    per_b = lambda shp: pl.BlockSpec((1,) + shp, lambda b, j, pt: (b, 0, 0))
    grid_spec = pltpu.PrefetchScalarGridSpec(
        num_scalar_prefetch=1,
        grid=(nb, n_pages // pg),
        in_specs=[per_b((IDX_HEADS, IDX_DK)), per_b((IDX_HEADS, 1)), per_b((1, IDX_DK))]
                 + _page_specs((IDX_DK, PAGE_SIZE), pg, pg),
        out_specs=per_b((rows, PAGE_SIZE)),
        scratch_shapes=[pltpu.VMEM((rows, PAGE_SIZE), jnp.float32)])
    return pl.pallas_call(
        functools.partial(_idx_sample_kernel, pg=pg, n_pages=n_pages, topk=topk),
        grid_spec=grid_spec,
        out_shape=jax.ShapeDtypeStruct((nb, rows, PAGE_SIZE), jnp.float32),
        compiler_params=_cparams(("arbitrary", "arbitrary")),
        name="idx_sample",
    )(page_table, qix, wix, knew, *([kc] * pg))


def _sa_sample_kernel(pt_ref, q_ref, kn_ref, vn_ref, sel_ref, bl_ref, cf_ref, b0_ref, gm_ref, *rest,
                      pg, n_pages):
    k_refs, v_refs = rest[:pg], rest[pg:2 * pg]
    o_ref, s_ref, p_ref, acc_ref = rest[2 * pg:]
    j = pl.program_id(1)
    ns = n_pages // pg
    scale = SA_DH ** -0.5
    r, cols = gm_ref.shape
    lane = lax.broadcasted_iota(jnp.int32, (r, cols), 1)

    @pl.when(j < ns)
    def _():
        q = q_ref[0]
        qb = _bf(q)
        for u in range(pg):
            page = j * pg + u
            s = _dot_nt(qb, _bf(k_refs[u][0])) * scale
            s = s + jnp.where(page == (n_pages - 1), bl_ref[...], cf_ref[...])
            keep = (sel_ref[0, pl.ds(page, 1), :] > 0.0) & (gm_ref[...] > 0.0)
            s_ref[page] = jnp.where(keep, s, NEG)

        @pl.when(j == 0)
        def _():
            on = sel_ref[0, n_pages:n_pages + 1, 0:1] > 0.0
            s_new = jnp.sum(_rounded(q) * _rounded(kn_ref[0]), axis=-1, keepdims=True) * scale + b0_ref[...]
            s_ref[n_pages] = jnp.where((lane == 0) & on, s_new, NEG)

    @pl.when(j == ns)
    def _():
        p_ref[...] = _bf(_softmax_pages(s_ref))
        acc_ref[...] = p_ref[n_pages][:, 0:1].astype(jnp.float32) * _rounded(vn_ref[0])

    @pl.when(j >= ns)
    def _():
        acc = acc_ref[...]
        for u in range(pg):
            acc = acc + _dot(p_ref[(j - ns) * pg + u], _bf(v_refs[u][0]))
        acc_ref[...] = acc

    @pl.when(j == pl.num_programs(1) - 1)
    def _():
        o_ref[0] = acc_ref[...]


def _sa_sample(page_table, q, knew, vnew, sel2, bias_last2, cfar, bias0, group_mask, k2, v2, pg):
    nb, n_pages = page_table.shape
    r, dh = q.shape[1:]
    srows, cols = sel2.shape[1:]
    ns = n_pages // pg
    full = lambda shp: pl.BlockSpec(shp, lambda b, j, pt: (0,) * len(shp))
    per_b = lambda shp: pl.BlockSpec((1,) + shp, lambda b, j, pt: (b, 0, 0))
    grid_spec = pltpu.PrefetchScalarGridSpec(
        num_scalar_prefetch=1,
        grid=(nb, 2 * ns),
        in_specs=[per_b((r, dh)), per_b((r, dh)), per_b((r, dh)), per_b((srows, cols)),
                  full((r, cols)), full((r, 1)), full((r, 1)), full((r, cols))]
                 + _kv_page_specs((cols, dh), (cols, dh), pg, ns),
        out_specs=per_b((r, dh)),
        scratch_shapes=[pltpu.VMEM((n_pages + 1, r, cols), jnp.float32),
                        pltpu.VMEM((n_pages + 1, r, cols), BF16),
                        pltpu.VMEM((r, dh), jnp.float32)])
    return pl.pallas_call(
        functools.partial(_sa_sample_kernel, pg=pg, n_pages=n_pages),
        grid_spec=grid_spec,
        out_shape=jax.ShapeDtypeStruct((nb, r, dh), jnp.float32),
        compiler_params=_cparams(("arbitrary", "arbitrary")),
        name="sa_sample",
    )(page_table, q, knew, vnew, sel2, bias_last2, cfar, bias0, group_mask, *([k2] * pg), *([v2] * pg))


ROW_TILE = 256
COMBINE_TILE = 512
EXPERT_BLOCK = 512
TAIL_ROWS = 256
PAGES_PER_STEP_DA = 16
PAGES_PER_STEP_SA = 32
PAGES_PER_STEP_IDX = 64


def _rms_rows(x, g):
    return x * lax.rsqrt(jnp.mean(x * x, axis=-1, keepdims=True) + EPS) * g


def _pad_rows(x, rows):
    return jnp.pad(x, ((0, rows - x.shape[0]), (0, 0)))


def _tile_major(dest, tn):
    k, n = dest.shape
    return dest.reshape(k, n // tn, tn).transpose(1, 0, 2).reshape(-1)


def kernel(x_prompt, x_sample, c_prompt, c_sample, cache_da_k, cache_da_v, cache_sa_k, cache_sa_v, cache_idx_k, page_table, rel_bias_table, w_ada, b_ada, g_attn, g_ffn, w_in, lambda_q1, lambda_k1, lambda_q2, lambda_k2, g_subln, w_proj_da, w_proj_sa, w_out, w_router, b_router, w_gate, w_up, w_down, w_sh_gate, w_sh_up, w_sh_down, g_final):
    f32, bf16 = jnp.float32, BF16
    nb, t, d = x_prompt.shape
    nbs, dec_seq, _ = x_sample.shape
    assert dec_seq == 1
    depth = w_in.shape[0]
    n_pages = page_table.shape[1]
    past_len = n_pages * PAGE_SIZE
    n = nb * t
    n_pad = n + TAIL_ROWS
    tq = min(256, t)
    tq_da = min(512, t)
    assert t % tq_da == 0
    assert tq >= MAX_DISTANCE and t % tq == 0 and n % ROW_TILE == 0 and nbs <= TAIL_ROWS
    assert TAIL_ROWS == ROW_TILE and EXPERT_BLOCK % 8 == 0
    topk_p = min(SA_TOPK_MAX, t // 4)
    topk_s = min(SA_TOPK_MAX, (past_len + dec_seq) // 4)
    assert topk_p <= tq

    sizes = [DA_HEADS * 2 * DA_DK, DA_HEADS * 2 * DA_DK, DA_HEADS * DA_DV, SA_HEADS * SA_DH,
             SA_KV_HEADS * SA_DH, SA_KV_HEADS * SA_DH, IDX_HEADS * IDX_DK, IDX_DK, IDX_HEADS, d, d]
    offs = [sum(sizes[:i]) for i in range(len(sizes) + 1)]
    (o_qda, o_kda, o_vda, o_qsa, o_ksa, o_vsa, o_qix, o_kix, o_wix, o_gda, o_gsa, _) = offs
    misc_w = 2 * LANES

    table = rel_bias_table.astype(f32)
    cfar = table[N_BUCKETS - 1]
    tiles_da = _near_tiles(table[:, :DA_HEADS], tq_da)
    tiles_sa = _near_tiles(table[:, DA_HEADS:], tq)
    last_dist = past_len - ((n_pages - 1) * PAGE_SIZE + jnp.arange(PAGE_SIZE, dtype=jnp.int32))
    bias_last = _bias_by_distance(table, last_dist)
    bias0 = _bias_by_distance(table, jnp.zeros((1,), jnp.int32))

    xp = x_prompt.reshape(n, d)
    xs = x_sample.reshape(nbs, d)
    c_all = jnp.concatenate([c_prompt, c_sample], axis=0)
    leaves_p, leaves_s = [], []
    for l in range(depth):
        lam_init = 0.8 - 0.6 * math.exp(-0.3 * l)
        lam = (jnp.exp(jnp.sum(lambda_q1[l].astype(f32) * lambda_k1[l].astype(f32)))
               - jnp.exp(jnp.sum(lambda_q2[l].astype(f32) * lambda_k2[l].astype(f32))) + lam_init)
        mod = _linear_small(c_all, w_ada[l], b_ada[l], silu_in=True)
        sh1, sc1, gt1, sh2, sc2, gt2 = jnp.split(mod[:nb], 6, axis=-1)
        sh1s, sc1s, gt1s, sh2s, sc2s, gt2s = jnp.split(mod[nb:], 6, axis=-1)
        w = w_in[l]

        h = _norm_mod(xp, g_attn[l], sc1, sh1, t, ROW_TILE)
        q_da, = _mm(h, w, o_qda, sizes[0], (bf16,), scale=DA_DK ** -0.5)
        k_da_b, k_da_t = _mm(h, w, o_kda, sizes[1], (bf16,), rows_per_batch=t)
        k_da = k_da_t.reshape(nb, DA_HEADS, 2, DA_DK, t).transpose(0, 4, 1, 2, 3)
        v_da, v_da_b = _mm(h, w, o_vda, sizes[2], (f32, bf16))
        q_sa, = _mm(h, w, o_qsa, sizes[3], (bf16,), scale=SA_DH ** -0.5)
        k_sa, k_sa_b = _mm(h, w, o_ksa, sizes[4], (f32, bf16))
        v_sa, v_sa_b = _mm(h, w, o_vsa, sizes[5], (f32, bf16))
        q_ix, = _mm(h, w, o_qix, sizes[6], (bf16,), tn=512)
        misc, misc_b = _mm(h, w, o_kix, misc_w, (f32, bf16))
        gates, = _mm(h, w[:, o_gda:], 0, 2 * d, (bf16,), sigmoid=True)
        k_ix = misc[:, :IDX_DK]

        o_da = _da_prompt(q_da, k_da_b, v_da_b, tiles_da, cfar[:DA_HEADS], lam.reshape(1),
                          g_subln[l].astype(f32), nb, t, tq_da, 1.0 - lam_init)
        o_sa = _sa_prompt(q_sa, q_ix, misc, misc_b, k_sa_b, v_sa_b, tiles_sa, cfar[DA_HEADS:], nb, t, tq, topk_p)
        m = _proj_gate(o_da, o_sa, w_proj_da[l], w_proj_sa[l], gates, d)
        x1 = _mm_resid(m, w_out[l], xp, gt1, t)

        hs = _rms_rows(xs, g_attn[l].astype(f32)) * (1.0 + sc1s) + sh1s
        ps = _linear_small(hs, w)
        seg = lambda i: ps[:, offs[i]:offs[i + 1]]
        q_da_s, k_da_s, v_da_s, q_sa_s, k_sa_s, v_sa_s, q_ix_s, k_ix_s, w_ix_s, gda_s, gsa_s = [
            seg(i) for i in range(11)]

        r_da = 2 * DA_HEADS
        q16 = (q_da_s * DA_DK ** -0.5).reshape(nbs, DA_HEADS, 2, DA_DK).transpose(0, 2, 1, 3).reshape(nbs, r_da, DA_DK)
        blk_of_row = 2 * (jnp.arange(r_da) % DA_HEADS) + jnp.arange(r_da) // DA_HEADS
        place = (blk_of_row[:, None] == jnp.arange(r_da)[None, :]).astype(f32)
        qbd_da = (place[None, :, :, None] * q16[:, :, None, :]).reshape(nbs, r_da, -1)
        both = lambda a: jnp.concatenate([a, a], axis=0)
        kt_da = cache_da_k[l].transpose(0, 2, 3, 4, 1).reshape(-1, DA_HEADS * 2 * DA_DK, PAGE_SIZE)
        v2_da = cache_da_v[l].reshape(-1, PAGE_SIZE * DA_HEADS, DA_DV)
        v_rows = jnp.pad(v_da_s.reshape(nbs, DA_HEADS, DA_DV), ((0, 0), (0, r_da - DA_HEADS), (0, 0)))
        a_da = _da_sample(page_table, lam.reshape(1), qbd_da, k_da_s[:, None, :], v_rows,
                          both(bias_last[:DA_HEADS]), both(cfar[:DA_HEADS, None]), both(bias0[:DA_HEADS]),
                          kt_da, v2_da, math.gcd(PAGES_PER_STEP_DA, n_pages))
        o_da_s = a_da[:, :DA_HEADS]
        o_da_s = _rms_rows(o_da_s, g_subln[l].astype(f32)) * (1.0 - lam_init)

        sel = _idx_sample(page_table, q_ix_s.reshape(nbs, IDX_HEADS, IDX_DK), w_ix_s[:, :, None],
                          k_ix_s[:, None, :], cache_idx_k[l].transpose(0, 2, 1),
                          math.gcd(PAGES_PER_STEP_IDX, n_pages), topk_s)
        r_sa = 2 * SA_HEADS
        kv_of_row = jnp.minimum(jnp.arange(r_sa) // SA_GROUP, SA_KV_HEADS - 1)
        pad_sa = lambda a: jnp.pad(a, ((0, 0), (0, r_sa - SA_HEADS), (0, 0)))
        rep_kv = lambda a: jnp.repeat(a, SA_KV_HEADS, axis=-1)
        col_kv = jnp.arange(PAGE_SIZE * SA_KV_HEADS) % SA_KV_HEADS
        k2_sa = cache_sa_k[l].reshape(-1, PAGE_SIZE * SA_KV_HEADS, SA_DH)
        v2_sa = cache_sa_v[l].reshape(-1, PAGE_SIZE * SA_KV_HEADS, SA_DH)
        a_sa = _sa_sample(page_table, pad_sa(q_sa_s.reshape(nbs, SA_HEADS, SA_DH)),
                          k_sa_s.reshape(nbs, SA_KV_HEADS, SA_DH)[:, kv_of_row],
                          v_sa_s.reshape(nbs, SA_KV_HEADS, SA_DH)[:, kv_of_row],
                          rep_kv(sel), rep_kv(_pad_rows(bias_last[DA_HEADS:], r_sa)),
                          _pad_rows(cfar[DA_HEADS:, None], r_sa), _pad_rows(bias0[DA_HEADS:], r_sa),
                          (col_kv[None, :] == kv_of_row[:, None]).astype(f32),
                          k2_sa, v2_sa, math.gcd(PAGES_PER_STEP_SA, n_pages))
        o_sa_s = a_sa[:, :SA_HEADS]

        pda = _linear_small(o_da_s.reshape(nbs, -1), w_proj_da[l])
        psa = _linear_small(o_sa_s.reshape(nbs, -1), w_proj_sa[l])
        ms = _sigmoid(gda_s) * pda + _sigmoid(gsa_s) * psa
        x1s = xs + gt1s * _linear_small(ms, w_out[l])
        h2s = _rms_rows(x1s, g_ffn[l].astype(f32)) * (1.0 + sc2s) + sh2s
        lg_s = _linear_small(h2s, w_router[l])

        h2_all, lg_all = _norm_router(x1, g_ffn[l], sc2, sh2, w_router[l].T,
                                      _pad_rows(h2s.astype(bf16), TAIL_ROWS),
                                      _pad_rows(lg_s, TAIL_ROWS).T, t, ROW_TILE)
        eidx, wts, rank, cnt = _route(lg_all, b_router[l], ROW_TILE)
        cnt_tile = cnt[:, :, 0]
        total = jnp.sum(cnt_tile, axis=0)
        padded = jnp.ceil(total / EXPERT_BLOCK) * EXPERT_BLOCK
        pends = jnp.cumsum(padded)
        pstart = pends - padded
        base = pstart[None, :] + jnp.cumsum(cnt_tile, axis=0) - cnt_tile
        dest = _dest(eidx, rank, jnp.broadcast_to(base[:, :, None], base.shape + (LANES,)), ROW_TILE)
        n_blk = -(-(n_pad * TOP_K) // EXPERT_BLOCK) + N_EXPERTS
        blk_start = (jnp.arange(n_blk) * EXPERT_BLOCK).astype(f32)
        blk_e = jnp.minimum(jnp.sum(pends[None, :] <= blk_start[:, None], axis=1), N_EXPERTS - 1).astype(jnp.int32)
        n_used = (pends[-1] / EXPERT_BLOCK).astype(jnp.int32).reshape(1)
        n_rows = n_blk * EXPERT_BLOCK
        pad_start = jnp.concatenate([pstart + total, pends[-1:]]).astype(jnp.int32)
        pad_cnt = jnp.concatenate([padded - total, (n_rows - pends[-1:]) / 8]).astype(jnp.int32)
        xs_sorted = _dispatch(h2_all, _tile_major(dest, ROW_TILE), pad_start, pad_cnt, n_rows, ROW_TILE)
        owns = padded > 0
        e_ids = jnp.arange(N_EXPERTS, dtype=jnp.int32)
        later = jnp.where(owns[None, :] & (e_ids[None, :] > e_ids[:, None]), e_ids[None, :], N_EXPERTS)
        next_owner = jnp.min(later, axis=1)
        next_owner = jnp.where(next_owner < N_EXPERTS, next_owner, -1).astype(jnp.int32)
        slot_of_e = ((jnp.cumsum(owns.astype(jnp.int32)) - 1) % 2).astype(jnp.int32)
        ys = _experts(xs_sorted, blk_e, n_used, next_owner, slot_of_e,
                      w_gate[l], w_up[l], w_down[l], EXPERT_BLOCK)
        shared = _ffn_shared(h2_all, w_sh_gate[l], w_sh_up[l], w_sh_down[l], ROW_TILE)
        tn_c = math.gcd(COMBINE_TILE, t)
        wts_tok = wts.T
        last = l == depth - 1
        xp = _combine(_tile_major(dest[:, :n], tn_c), ys, wts_tok, shared, x1, gt2[:, None, :],
                      g_final.astype(f32), 0, tn_c, last)
        tail = _combine(_tile_major(dest, TAIL_ROWS), ys, wts_tok, shared, _pad_rows(x1s, TAIL_ROWS),
                        _pad_rows(gt2s, TAIL_ROWS).reshape(1, TAIL_ROWS, d), g_final.astype(f32),
                        n // TAIL_ROWS, TAIL_ROWS, last)
        xs = tail[:nbs]
        leaves_p.append((k_da, v_da, k_sa, v_sa, k_ix))
        leaves_s.append((k_da_s, v_da_s, k_sa_s, v_sa_s, k_ix_s))

    shapes = [(DA_HEADS, 2, DA_DK), (DA_HEADS, DA_DV), (SA_KV_HEADS, SA_DH), (SA_KV_HEADS, SA_DH), (IDX_DK,)]
    out_p = [jnp.stack([lv[i].reshape((nb, t) + shapes[i]) for lv in leaves_p]) for i in range(5)]
    out_s = [jnp.stack([lv[i].reshape((nbs, dec_seq) + shapes[i]) for lv in leaves_s]) for i in range(5)]
    return (xp.reshape(nb, t, d), xs.reshape(nbs, dec_seq, d), *out_p, *out_s)
```
